```python
import jax, jax.numpy as jnp
from jax import lax
import numpy as np

D_MODEL = 2048
BATCH = 8
SEQ = 4096
DEPTH = 1

HEAD_DIM = 128
D_ATTN = D_MODEL // 2
N_ATTN_HEADS = D_ATTN // HEAD_DIM
D_GMLP = D_MODEL - D_ATTN
N_GMLP_HEADS = D_GMLP // HEAD_DIM
D_MIX = D_ATTN + D_GMLP
CHUNK = 128
Q_BLOCK = 128
D_FF = 4 * D_MODEL
D_IN_PROJ = 3 * D_ATTN + N_ATTN_HEADS + 2 * D_GMLP
EPS = 1e-6

kernel_name = "hymba_fox_gmlp_hybrid_block"


def rmsnorm(x, g):
    xf = x.astype(jnp.float32)
    y = xf * lax.rsqrt(jnp.mean(xf * xf, axis=-1, keepdims=True) + EPS)
    return (y * g.astype(jnp.float32)).astype(x.dtype)


def layernorm(x, g, b):
    xf = x.astype(jnp.float32)
    mu = jnp.mean(xf, axis=-1, keepdims=True)
    xc = xf - mu
    y = xc * lax.rsqrt(jnp.mean(xc * xc, axis=-1, keepdims=True) + EPS)
    return (y * g.astype(jnp.float32) + b.astype(jnp.float32)).astype(x.dtype)


def forgetting_attention(q, k, v, log_f):
    B, S, H, D = q.shape
    nb = S // Q_BLOCK
    scale = 1.0 / np.sqrt(D).astype(np.float32)
    F = jnp.cumsum(log_f, axis=1).transpose(0, 2, 1)
    q_blocks = q.reshape(B, nb, Q_BLOCK, H, D).transpose(1, 0, 2, 3, 4)
    F_blocks = F.reshape(B, H, nb, Q_BLOCK).transpose(2, 0, 1, 3)
    k_pos = jnp.arange(S)

    def one_block(args):
        qi, Fq, i = args
        s = jnp.einsum('bqhd,bkhd->bhqk', qi, k, preferred_element_type=jnp.float32) * scale
        s = s + Fq[..., :, None] - F[:, :, None, :]
        q_pos = i * Q_BLOCK + jnp.arange(Q_BLOCK)
        causal = k_pos[None, :] <= q_pos[:, None]
        s = jnp.where(causal[None, None], s, -jnp.inf)
        p = jax.nn.softmax(s, axis=-1)
        return jnp.einsum('bhqk,bkhd->bqhd', p.astype(v.dtype), v)

    out = lax.map(one_block, (q_blocks, F_blocks, jnp.arange(nb)))
    return out.transpose(1, 0, 2, 3, 4).reshape(B, S, H * D)


def chunked_spatial_gating(zu, zv, ln_g, ln_b, w_s, b_s):
    B, S, _ = zu.shape
    nc = S // CHUNK
    u = jax.nn.gelu(zu)
    v = layernorm(jax.nn.gelu(zv), ln_g, ln_b)
    v = v.reshape(B, nc, CHUNK, N_GMLP_HEADS, HEAD_DIM)
    w_causal = jnp.tril(w_s)
    mix = jnp.einsum('hts,bcshd->bcthd', w_causal.astype(v.dtype), v)
    mix = mix + b_s.T[None, None, :, :, None]
    out = u.reshape(B, nc, CHUNK, N_GMLP_HEADS, HEAD_DIM) * mix
    return out.reshape(B, S, D_GMLP)


def _fwd_setup_inputs(seed: int = 0) -> dict:
    key = jax.random.key(seed)
    ks = jax.random.split(key, 20)
    L = DEPTH
    nrm = jax.random.normal
    x = nrm(ks[0], (BATCH, SEQ, D_MODEL), jnp.float32)
    norm_mix_g = 1.0 + 0.02 * nrm(ks[1], (L, D_MODEL), jnp.float32)
    w_qkv = nrm(ks[2], (L, D_MODEL, 3 * D_ATTN), jnp.float32) * D_MODEL ** -0.5
    w_f = nrm(ks[3], (L, D_MODEL, N_ATTN_HEADS), jnp.float32) * 0.1 * D_MODEL ** -0.5
    w_g = nrm(ks[4], (L, D_MODEL, 2 * D_GMLP), jnp.float32) * D_MODEL ** -0.5
    w_in = jnp.concatenate([w_qkv, w_f, w_g], axis=-1)
    b_f = jax.random.uniform(ks[5], (L, N_ATTN_HEADS), jnp.float32, 1.0, 5.0)
    gmlp_ln_g = 1.0 + 0.02 * nrm(ks[6], (L, D_GMLP), jnp.float32)
    gmlp_ln_b = 0.02 * nrm(ks[7], (L, D_GMLP), jnp.float32)
    w_s = nrm(ks[8], (L, N_GMLP_HEADS, CHUNK, CHUNK), jnp.float32) * CHUNK ** -0.5
    b_s = 1.0 + 0.1 * nrm(ks[9], (L, N_GMLP_HEADS, CHUNK), jnp.float32)
    attn_out_g = 1.0 + 0.02 * nrm(ks[10], (L, D_ATTN), jnp.float32)
    gmlp_out_g = 1.0 + 0.02 * nrm(ks[11], (L, D_GMLP), jnp.float32)
    w_out = nrm(ks[12], (L, D_MIX, D_MODEL), jnp.float32) * D_MIX ** -0.5
    norm_ffn_g = 1.0 + 0.02 * nrm(ks[13], (L, D_MODEL), jnp.float32)
    w_ff1 = nrm(ks[14], (L, D_MODEL, D_FF), jnp.float32) * D_MODEL ** -0.5
    w_ff2 = nrm(ks[15], (L, D_FF, D_MODEL), jnp.float32) * D_FF ** -0.5
    norm_final_g = 1.0 + 0.02 * nrm(ks[16], (D_MODEL,), jnp.float32)
    return {"x": x, "norm_mix_g": norm_mix_g, "w_in": w_in, "b_f": b_f,
            "gmlp_ln_g": gmlp_ln_g, "gmlp_ln_b": gmlp_ln_b, "w_s": w_s, "b_s": b_s,
            "attn_out_g": attn_out_g, "gmlp_out_g": gmlp_out_g, "w_out": w_out,
            "norm_ffn_g": norm_ffn_g, "w_ff1": w_ff1, "w_ff2": w_ff2,
            "norm_final_g": norm_final_g}


def _fwd_reference(x, norm_mix_g, w_in, b_f, gmlp_ln_g, gmlp_ln_b, w_s, b_s,
              attn_out_g, gmlp_out_g, w_out, norm_ffn_g, w_ff1, w_ff2, norm_final_g):
    B, S, _ = x.shape
    for l in range(DEPTH):
        h = rmsnorm(x, norm_mix_g[l])
        z = jnp.einsum('bsd,de->bse', h, w_in[l])
        o1 = D_ATTN; o2 = 2 * D_ATTN; o3 = 3 * D_ATTN; o4 = o3 + N_ATTN_HEADS
        q = z[..., :o1].reshape(B, S, N_ATTN_HEADS, HEAD_DIM)
        k = z[..., o1:o2].reshape(B, S, N_ATTN_HEADS, HEAD_DIM)
        v = z[..., o2:o3].reshape(B, S, N_ATTN_HEADS, HEAD_DIM)
        log_f = jax.nn.log_sigmoid(z[..., o3:o4].astype(jnp.float32) + b_f[l].astype(jnp.float32))
        zu = z[..., o4:o4 + D_GMLP]
        zv = z[..., o4 + D_GMLP:]
        attn = forgetting_attention(q, k, v, log_f)
        gm = chunked_spatial_gating(zu, zv, gmlp_ln_g[l], gmlp_ln_b[l], w_s[l], b_s[l])
        merged = jnp.concatenate([rmsnorm(attn, attn_out_g[l]),
                                  rmsnorm(gm, gmlp_out_g[l])], axis=-1)
        x = x + jnp.einsum('bse,ed->bsd', merged, w_out[l])
        h2 = rmsnorm(x, norm_ffn_g[l])
        a = jax.nn.relu(jnp.einsum('bsd,df->bsf', h2, w_ff1[l]))
        x = x + jnp.einsum('bsf,fd->bsd', a * a, w_ff2[l])
    return rmsnorm(x, norm_final_g)


import jax as _jax
import jax.numpy as _jnp

TWIN_FORMAT = 'train_step'
FWD_PARAMS = ['x', 'norm_mix_g', 'w_in', 'b_f', 'gmlp_ln_g', 'gmlp_ln_b', 'w_s', 'b_s', 'attn_out_g', 'gmlp_out_g', 'w_out', 'norm_ffn_g', 'w_ff1', 'w_ff2', 'norm_final_g']
TWIN_WEIGHTS = ['norm_mix_g', 'w_in', 'b_f', 'gmlp_ln_g', 'gmlp_ln_b', 'w_s', 'b_s', 'attn_out_g', 'gmlp_out_g', 'w_out', 'norm_ffn_g', 'w_ff1', 'w_ff2', 'norm_final_g']
TWIN_DIFF_INPUT = 'x'
TWIN_INPUTS = ['x', 'norm_mix_g', 'w_in', 'b_f', 'gmlp_ln_g', 'gmlp_ln_b', 'w_s', 'b_s', 'attn_out_g', 'gmlp_out_g', 'w_out', 'norm_ffn_g', 'w_ff1', 'w_ff2', 'norm_final_g', 'loss_target', 'm_norm_mix_g', 'm_w_in', 'm_b_f', 'm_gmlp_ln_g', 'm_gmlp_ln_b', 'm_w_s', 'm_b_s', 'm_attn_out_g', 'm_gmlp_out_g', 'm_w_out', 'm_norm_ffn_g', 'm_w_ff1', 'm_w_ff2', 'm_norm_final_g', 'v_norm_mix_g', 'v_w_in', 'v_b_f', 'v_gmlp_ln_g', 'v_gmlp_ln_b', 'v_w_s', 'v_b_s', 'v_attn_out_g', 'v_gmlp_out_g', 'v_w_out', 'v_norm_ffn_g', 'v_w_ff1', 'v_w_ff2', 'v_norm_final_g']
TWIN_OUTPUTS = ['loss', 'grad_x', 'grad_norm_mix_g', 'grad_w_in', 'grad_b_f', 'grad_gmlp_ln_g', 'grad_gmlp_ln_b', 'grad_w_s', 'grad_b_s', 'grad_attn_out_g', 'grad_gmlp_out_g', 'grad_w_out', 'grad_norm_ffn_g', 'grad_w_ff1', 'grad_w_ff2', 'grad_norm_final_g', 'delta_norm_mix_g', 'delta_w_in', 'delta_b_f', 'delta_gmlp_ln_g', 'delta_gmlp_ln_b', 'delta_w_s', 'delta_b_s', 'delta_attn_out_g', 'delta_gmlp_out_g', 'delta_w_out', 'delta_norm_ffn_g', 'delta_w_ff1', 'delta_w_ff2', 'delta_norm_final_g', 'new_m_norm_mix_g', 'new_m_w_in', 'new_m_b_f', 'new_m_gmlp_ln_g', 'new_m_gmlp_ln_b', 'new_m_w_s', 'new_m_b_s', 'new_m_attn_out_g', 'new_m_gmlp_out_g', 'new_m_w_out', 'new_m_norm_ffn_g', 'new_m_w_ff1', 'new_m_w_ff2', 'new_m_norm_final_g', 'new_v_norm_mix_g', 'new_v_w_in', 'new_v_b_f', 'new_v_gmlp_ln_g', 'new_v_gmlp_ln_b', 'new_v_w_s', 'new_v_b_s', 'new_v_attn_out_g', 'new_v_gmlp_out_g', 'new_v_w_out', 'new_v_norm_ffn_g', 'new_v_w_ff1', 'new_v_w_ff2', 'new_v_norm_final_g']
TWIN_LEAF_KINDS = {'loss': 'loss', 'grad_x': 'grad_x', 'grad_norm_mix_g': 'grad_w', 'grad_w_in': 'grad_w', 'grad_b_f': 'grad_w', 'grad_gmlp_ln_g': 'grad_w', 'grad_gmlp_ln_b': 'grad_w', 'grad_w_s': 'grad_w', 'grad_b_s': 'grad_w', 'grad_attn_out_g': 'grad_w', 'grad_gmlp_out_g': 'grad_w', 'grad_w_out': 'grad_w', 'grad_norm_ffn_g': 'grad_w', 'grad_w_ff1': 'grad_w', 'grad_w_ff2': 'grad_w', 'grad_norm_final_g': 'grad_w', 'delta_norm_mix_g': 'delta_w', 'delta_w_in': 'delta_w', 'delta_b_f': 'delta_w', 'delta_gmlp_ln_g': 'delta_w', 'delta_gmlp_ln_b': 'delta_w', 'delta_w_s': 'delta_w', 'delta_b_s': 'delta_w', 'delta_attn_out_g': 'delta_w', 'delta_gmlp_out_g': 'delta_w', 'delta_w_out': 'delta_w', 'delta_norm_ffn_g': 'delta_w', 'delta_w_ff1': 'delta_w', 'delta_w_ff2': 'delta_w', 'delta_norm_final_g': 'delta_w', 'new_m_norm_mix_g': 'new_m', 'new_m_w_in': 'new_m', 'new_m_b_f': 'new_m', 'new_m_gmlp_ln_g': 'new_m', 'new_m_gmlp_ln_b': 'new_m', 'new_m_w_s': 'new_m', 'new_m_b_s': 'new_m', 'new_m_attn_out_g': 'new_m', 'new_m_gmlp_out_g': 'new_m', 'new_m_w_out': 'new_m', 'new_m_norm_ffn_g': 'new_m', 'new_m_w_ff1': 'new_m', 'new_m_w_ff2': 'new_m', 'new_m_norm_final_g': 'new_m', 'new_v_norm_mix_g': 'new_v', 'new_v_w_in': 'new_v', 'new_v_b_f': 'new_v', 'new_v_gmlp_ln_g': 'new_v', 'new_v_gmlp_ln_b': 'new_v', 'new_v_w_s': 'new_v', 'new_v_b_s': 'new_v', 'new_v_attn_out_g': 'new_v', 'new_v_gmlp_out_g': 'new_v', 'new_v_w_out': 'new_v', 'new_v_norm_ffn_g': 'new_v', 'new_v_w_ff1': 'new_v', 'new_v_w_ff2': 'new_v', 'new_v_norm_final_g': 'new_v'}


def _forward(args):
    return _fwd_reference(*[args[k] for k in FWD_PARAMS])


def _output_shape():
    def fwd():
        inp = _fwd_setup_inputs(0)
        return _fwd_reference(*[inp[k] for k in FWD_PARAMS])
    out = _jax.eval_shape(fwd)
    return out.shape, out.dtype

N_MICROBATCH = 1
ADAM_LR = 0.001
ADAM_B1 = 0.9
ADAM_B2 = 0.999
ADAM_EPS = 1e-08
ADAM_WD = 0.01
ADAM_STEP = 10
PER_EXAMPLE_BATCH_AXIS = {'x': 0, 'loss_target': 0}
SHARED_INPUTS = []
_WEIGHT_DTYPES = {'norm_mix_g': _jnp.float32, 'w_in': _jnp.float32, 'b_f': _jnp.float32, 'gmlp_ln_g': _jnp.float32, 'gmlp_ln_b': _jnp.float32, 'w_s': _jnp.float32, 'b_s': _jnp.float32, 'attn_out_g': _jnp.float32, 'gmlp_out_g': _jnp.float32, 'w_out': _jnp.float32, 'norm_ffn_g': _jnp.float32, 'w_ff1': _jnp.float32, 'w_ff2': _jnp.float32, 'norm_final_g': _jnp.float32}
MOMENT_SCALE = {'norm_mix_g': 9.500079e-02, 'w_in': 5.943159e-02, 'b_f': 1.213585e+00, 'gmlp_ln_g': 3.843958e-02, 'gmlp_ln_b': 3.818028e-02, 'w_s': 3.731534e-02, 'b_s': 5.445319e-02, 'attn_out_g': 7.465345e-02, 'gmlp_out_g': 7.489515e-02, 'w_out': 7.023333e-02, 'norm_ffn_g': 6.701537e-02, 'w_ff1': 3.397012e-02, 'w_ff2': 6.873847e-02, 'norm_final_g': 1.613985e+01}


def _to_microbatches(a, axis):
    t = _jnp.moveaxis(a, axis, 0)
    t = t.reshape((N_MICROBATCH, t.shape[0] // N_MICROBATCH) + t.shape[1:])
    return _jnp.moveaxis(t, 1, axis + 1)


def setup_inputs(seed: int = 0) -> dict:
    inp = _fwd_setup_inputs(seed)
    key = _jax.random.fold_in(_jax.random.key(seed), 7919)
    shape, _ = _output_shape()
    out = dict(inp)
    out["loss_target"] = _jax.random.normal(_jax.random.fold_in(key, 0), shape, _jnp.float32)
    for i, name in enumerate(TWIN_WEIGHTS):
        w = inp[name].astype(_jnp.float32)
        if MOMENT_SCALE is None:
            s = _jnp.sqrt(_jnp.mean(_jnp.square(w)) + 1e-30)
        else:
            s = MOMENT_SCALE[name]
        km, kv = _jax.random.split(_jax.random.fold_in(key, i + 1))
        out[name] = w
        out["m_" + name] = s * _jax.random.normal(km, w.shape, _jnp.float32)
        out["v_" + name] = (s * s) * _jax.random.uniform(kv, w.shape, _jnp.float32, 0.5, 1.5)
    if N_MICROBATCH > 1:
        for name, axis in PER_EXAMPLE_BATCH_AXIS.items():
            out[name] = _to_microbatches(out[name], axis)
    return {'x': out['x'], 'norm_mix_g': out['norm_mix_g'], 'w_in': out['w_in'], 'b_f': out['b_f'], 'gmlp_ln_g': out['gmlp_ln_g'], 'gmlp_ln_b': out['gmlp_ln_b'], 'w_s': out['w_s'], 'b_s': out['b_s'], 'attn_out_g': out['attn_out_g'], 'gmlp_out_g': out['gmlp_out_g'], 'w_out': out['w_out'], 'norm_ffn_g': out['norm_ffn_g'], 'w_ff1': out['w_ff1'], 'w_ff2': out['w_ff2'], 'norm_final_g': out['norm_final_g'], 'loss_target': out['loss_target'], 'm_norm_mix_g': out['m_norm_mix_g'], 'm_w_in': out['m_w_in'], 'm_b_f': out['m_b_f'], 'm_gmlp_ln_g': out['m_gmlp_ln_g'], 'm_gmlp_ln_b': out['m_gmlp_ln_b'], 'm_w_s': out['m_w_s'], 'm_b_s': out['m_b_s'], 'm_attn_out_g': out['m_attn_out_g'], 'm_gmlp_out_g': out['m_gmlp_out_g'], 'm_w_out': out['m_w_out'], 'm_norm_ffn_g': out['m_norm_ffn_g'], 'm_w_ff1': out['m_w_ff1'], 'm_w_ff2': out['m_w_ff2'], 'm_norm_final_g': out['m_norm_final_g'], 'v_norm_mix_g': out['v_norm_mix_g'], 'v_w_in': out['v_w_in'], 'v_b_f': out['v_b_f'], 'v_gmlp_ln_g': out['v_gmlp_ln_g'], 'v_gmlp_ln_b': out['v_gmlp_ln_b'], 'v_w_s': out['v_w_s'], 'v_b_s': out['v_b_s'], 'v_attn_out_g': out['v_attn_out_g'], 'v_gmlp_out_g': out['v_gmlp_out_g'], 'v_w_out': out['v_w_out'], 'v_norm_ffn_g': out['v_norm_ffn_g'], 'v_w_ff1': out['v_w_ff1'], 'v_w_ff2': out['v_w_ff2'], 'v_norm_final_g': out['v_norm_final_g']}


def _loss(weights, diff, rest, loss_target):
    with _jax.named_scope("forward"):
        args = {**rest, TWIN_DIFF_INPUT: diff, **{k: w.astype(_WEIGHT_DTYPES[k]) for k, w in weights.items()}}
        y = _forward(args)
    with _jax.named_scope("loss_head"):
        err = _jnp.square(y.astype(_jnp.float32) - loss_target)
        return 0.5 * _jnp.sum(_jnp.mean(err, axis=-1)) if err.ndim else 0.5 * err


def _adamw(w, g, m, v):
    m = ADAM_B1 * m + (1.0 - ADAM_B1) * g
    v = ADAM_B2 * v + (1.0 - ADAM_B2) * _jnp.square(g)
    m_hat = m / (1.0 - ADAM_B1 ** ADAM_STEP)
    v_hat = v / (1.0 - ADAM_B2 ** ADAM_STEP)
    delta = -ADAM_LR * (m_hat / (_jnp.sqrt(v_hat) + ADAM_EPS) + ADAM_WD * w)
    return delta, m, v


def reference(x, norm_mix_g, w_in, b_f, gmlp_ln_g, gmlp_ln_b, w_s, b_s, attn_out_g, gmlp_out_g, w_out, norm_ffn_g, w_ff1, w_ff2, norm_final_g, loss_target, m_norm_mix_g, m_w_in, m_b_f, m_gmlp_ln_g, m_gmlp_ln_b, m_w_s, m_b_s, m_attn_out_g, m_gmlp_out_g, m_w_out, m_norm_ffn_g, m_w_ff1, m_w_ff2, m_norm_final_g, v_norm_mix_g, v_w_in, v_b_f, v_gmlp_ln_g, v_gmlp_ln_b, v_w_s, v_b_s, v_attn_out_g, v_gmlp_out_g, v_w_out, v_norm_ffn_g, v_w_ff1, v_w_ff2, v_norm_final_g):
    given = dict(x=x, norm_mix_g=norm_mix_g, w_in=w_in, b_f=b_f, gmlp_ln_g=gmlp_ln_g, gmlp_ln_b=gmlp_ln_b, w_s=w_s, b_s=b_s, attn_out_g=attn_out_g, gmlp_out_g=gmlp_out_g, w_out=w_out, norm_ffn_g=norm_ffn_g, w_ff1=w_ff1, w_ff2=w_ff2, norm_final_g=norm_final_g, loss_target=loss_target, m_norm_mix_g=m_norm_mix_g, m_w_in=m_w_in, m_b_f=m_b_f, m_gmlp_ln_g=m_gmlp_ln_g, m_gmlp_ln_b=m_gmlp_ln_b, m_w_s=m_w_s, m_b_s=m_b_s, m_attn_out_g=m_attn_out_g, m_gmlp_out_g=m_gmlp_out_g, m_w_out=m_w_out, m_norm_ffn_g=m_norm_ffn_g, m_w_ff1=m_w_ff1, m_w_ff2=m_w_ff2, m_norm_final_g=m_norm_final_g, v_norm_mix_g=v_norm_mix_g, v_w_in=v_w_in, v_b_f=v_b_f, v_gmlp_ln_g=v_gmlp_ln_g, v_gmlp_ln_b=v_gmlp_ln_b, v_w_s=v_w_s, v_b_s=v_b_s, v_attn_out_g=v_attn_out_g, v_gmlp_out_g=v_gmlp_out_g, v_w_out=v_w_out, v_norm_ffn_g=v_norm_ffn_g, v_w_ff1=v_w_ff1, v_w_ff2=v_w_ff2, v_norm_final_g=v_norm_final_g)
    weights = {n: given[n] for n in TWIN_WEIGHTS}
    shared = {n: given[n] for n in SHARED_INPUTS}
    per_example = {n: given[n] for n in ['x']}
    grad_fn = _jax.value_and_grad(_loss, argnums=(0, 1))

    def one_microbatch(ex, loss_target):
        ex = dict(ex)
        diff = ex.pop(TWIN_DIFF_INPUT)
        return grad_fn(weights, diff, {**shared, **ex}, loss_target)

    if N_MICROBATCH == 1:
        loss, (grad_w, grad_x) = one_microbatch(per_example, given["loss_target"])
    else:
        def body(carry, xs):
            loss_sum, grad_sum = carry
            l_k, (gw_k, gx_k) = one_microbatch(xs[0], xs[1])
            with _jax.named_scope("update"):
                return (loss_sum + l_k, _jax.tree.map(_jnp.add, grad_sum, gw_k)), gx_k

        init = (_jnp.zeros((), _jnp.float32), _jax.tree.map(_jnp.zeros_like, weights))
        (loss, grad_w), grad_x = _jax.lax.scan(body, init, (per_example, given["loss_target"]))
    with _jax.named_scope("update"):
        delta_w, new_m, new_v = {}, {}, {}
        for n in TWIN_WEIGHTS:
            delta_w[n], new_m[n], new_v[n] = _adamw(weights[n], grad_w[n], given["m_" + n], given["v_" + n])
    return (loss, grad_x, *[grad_w[n] for n in TWIN_WEIGHTS], *[delta_w[n] for n in TWIN_WEIGHTS],
            *[new_m[n] for n in TWIN_WEIGHTS], *[new_v[n] for n in TWIN_WEIGHTS])
```

```python
import functools
import math

import jax
import jax.numpy as jnp
from jax import lax
from jax.experimental import pallas as pl
from jax.experimental.pallas import tpu as pltpu

F32 = jnp.float32
BF16 = jnp.bfloat16
MESH = pl.DeviceIdType.MESH

HEAD_DIM = 128
CHUNK = 128
EPS = 1e-6
LANES = 128
N_DEV = 8

ADAM_LR = 0.001
ADAM_B1 = 0.9
ADAM_B2 = 0.999
ADAM_EPS = 1e-08
ADAM_WD = 0.01
ADAM_STEP = 10

VMEM_LIMIT_BYTES = 56 * 1024 * 1024
T_ATT_MAX = 512
TR_MAX = 256

NN = ((1,), (0,))
NT = ((1,), (1,))
TN = ((0,), (0,))


def _params(sem=None):
    return pltpu.CompilerParams(dimension_semantics=sem, vmem_limit_bytes=VMEM_LIMIT_BYTES)


def _dot(a, b, contract=NN):
    return lax.dot_general(a, b, (contract, ((), ())), preferred_element_type=F32)


def _dot3(x, t):
    x1 = x.astype(BF16)
    r1 = x - x1.astype(F32)
    x2 = r1.astype(BF16)
    x3 = (r1 - x2.astype(F32)).astype(BF16)
    return _dot(x1, t) + _dot(x2, t) + _dot(x3, t)


def _iota2(shape, dim):
    return lax.broadcasted_iota(jnp.int32, shape, dim)


def _row_call(name, fn, row_ins, bcast_ins, row_outs, acc_outs, tr):
    S = row_ins[0].shape[0]
    assert S % tr == 0
    n_ri, n_bi, n_ro, n_ao = len(row_ins), len(bcast_ins), len(row_outs), len(acc_outs)

    def body(*refs):
        ins = [r[...] for r in refs[:n_ri + n_bi]]
        ro_refs = refs[n_ri + n_bi:n_ri + n_bi + n_ro]
        ao_refs = refs[n_ri + n_bi + n_ro:]
        ro, ao = fn(*ins)
        for r, v in zip(ro_refs, ro):
            r[...] = v.astype(r.dtype)
        if n_ao:
            @pl.when(pl.program_id(0) == 0)
            def _():
                for r in ao_refs:
                    r[...] = jnp.zeros_like(r)
            for r, v in zip(ao_refs, ao):
                r[...] += v

    in_specs = [pl.BlockSpec((tr, a.shape[1]), lambda i: (i, 0)) for a in row_ins]
    in_specs += [pl.BlockSpec(a.shape, lambda i: (0, 0)) for a in bcast_ins]
    out_specs = [pl.BlockSpec((tr, d), lambda i: (i, 0)) for d, _ in row_outs]
    out_specs += [pl.BlockSpec((1, d), lambda i: (0, 0)) for d in acc_outs]
    out_shape = [jax.ShapeDtypeStruct((S, d), dt) for d, dt in row_outs]
    out_shape += [jax.ShapeDtypeStruct((1, d), F32) for d in acc_outs]
    outs = pl.pallas_call(
        body, name=name, grid=(S // tr,), in_specs=in_specs, out_specs=out_specs, out_shape=out_shape,
        compiler_params=_params(("arbitrary",) if n_ao else ("parallel",)),
    )(*row_ins, *bcast_ins)
    return outs[:n_ro], outs[n_ro:]


def _rms_fwd(x, g):
    rstd = lax.rsqrt(jnp.mean(x * x, axis=-1, keepdims=True) + EPS)
    return x * rstd * g


def _rms_bwd(dy, x, g):
    rstd = lax.rsqrt(jnp.mean(x * x, axis=-1, keepdims=True) + EPS)
    xhat = x * rstd
    dg = jnp.sum(dy * xhat, axis=0, keepdims=True)
    dxhat = dy * g
    dx = rstd * (dxhat - xhat * jnp.mean(dxhat * xhat, axis=-1, keepdims=True))
    return dx, dg


_GELU_C = math.sqrt(2.0 / math.pi)


def _gelu(x):
    return 0.5 * x * (1.0 + jnp.tanh(_GELU_C * (x + 0.044715 * (x * x * x))))


def _gelu_grad(x):
    t = jnp.tanh(_GELU_C * (x + 0.044715 * (x * x * x)))
    return 0.5 * (1.0 + t) + 0.5 * x * (1.0 - t * t) * (_GELU_C * (1.0 + 3.0 * 0.044715 * (x * x)))


def _mm(name, grid, a, a_spec, b, b_spec, contract, acc_shape, out_shape, out_specs, extras=(), epilogue=None):
    nk = grid[2]
    n_e = len(extras)
    n_o = len(out_shape)
    if epilogue is None:
        epilogue = lambda acc: (acc,)

    def body(a_ref, b_ref, *rest):
        e_refs = rest[:n_e]
        o_refs = rest[n_e:n_e + n_o]
        acc = rest[n_e + n_o]
        k = pl.program_id(2)

        @pl.when(k == 0)
        def _():
            acc[...] = jnp.zeros_like(acc)

        acc[...] += _dot(a_ref[...], b_ref[...], contract)

        @pl.when(k == nk - 1)
        def _():
            res = epilogue(acc[...], *[r[...] for r in e_refs])
            for o, r in zip(o_refs, res):
                o[...] = r.astype(o.dtype)

    return pl.pallas_call(
        body, name=name, grid=grid,
        in_specs=[a_spec, b_spec] + [s for _, s in extras],
        out_specs=list(out_specs), out_shape=list(out_shape),
        scratch_shapes=[pltpu.VMEM(acc_shape, F32)],
        compiler_params=_params(("parallel", "parallel", "arbitrary")),
    )(a, b, *[e for e, _ in extras])


def _mm_nn(name, a, b, out_dtypes, tm, tn, tk, extras=(), epilogue=None):
    M, K = a.shape
    N = b.shape[1]
    tm, tn, tk = min(tm, M), min(tn, N), min(tk, K)
    o_spec = pl.BlockSpec((tm, tn), lambda i, j, k: (i, j))
    return _mm(name, (M // tm, N // tn, K // tk),
               a, pl.BlockSpec((tm, tk), lambda i, j, k: (i, k)),
               b, pl.BlockSpec((tk, tn), lambda i, j, k: (k, j)), NN, (tm, tn),
               [jax.ShapeDtypeStruct((M, N), dt) for dt in out_dtypes], [o_spec] * len(out_dtypes),
               [(e, o_spec) for e in extras], epilogue)


def _mm_nt(name, a, b, out_dtypes, tm, tn, tk, extras=(), epilogue=None):
    M, K = a.shape
    N = b.shape[0]
    tm, tn, tk = min(tm, M), min(tn, N), min(tk, K)
    o_spec = pl.BlockSpec((tm, tn), lambda i, j, k: (i, j))
    return _mm(name, (M // tm, N // tn, K // tk),
               a, pl.BlockSpec((tm, tk), lambda i, j, k: (i, k)),
               b, pl.BlockSpec((tn, tk), lambda i, j, k: (j, k)), NT, (tm, tn),
               [jax.ShapeDtypeStruct((M, N), dt) for dt in out_dtypes], [o_spec] * len(out_dtypes),
               [(e, o_spec) for e in extras], epilogue)


def _mm_tn(name, a, b, out_dtypes, tm, tn, tk):
    K, M = a.shape
    N = b.shape[1]
    tm, tn, tk = min(tm, M), min(tn, N), min(tk, K)
    o_spec = pl.BlockSpec((tm, tn), lambda i, j, k: (i, j))
    return _mm(name, (M // tm, N // tn, K // tk),
               a, pl.BlockSpec((tk, tm), lambda i, j, k: (k, i)),
               b, pl.BlockSpec((tk, tn), lambda i, j, k: (k, j)), TN, (tm, tn),
               [jax.ShapeDtypeStruct((M, N), dt) for dt in out_dtypes], [o_spec] * len(out_dtypes),
               epilogue=lambda acc: (acc,) * len(out_dtypes))


def _fgate_fwd(zf, bf):
    S = zf.shape[0]
    nc = S // CHUNK

    def body(zf_ref, bf_ref, f_ref):
        upper = (_iota2((CHUNK, CHUNK), 0) <= _iota2((CHUNK, CHUNK), 1)).astype(BF16)
        carry = jnp.zeros((8, 1), F32)
        for c in range(nc):
            t = zf_ref[c * CHUNK:(c + 1) * CHUNK, :] + bf_ref[...]
            lf = jnp.minimum(t, 0.0) - jnp.log(1.0 + jnp.exp(-jnp.abs(t)))
            lf_rows = lf.T[0:8, :]
            f_ref[:, c * CHUNK:(c + 1) * CHUNK] = _dot3(lf_rows, upper) + carry
            carry = carry + jnp.sum(lf_rows, axis=-1, keepdims=True)

    return pl.pallas_call(
        body, name="fgate_fwd", out_shape=jax.ShapeDtypeStruct((8, S), F32),
        compiler_params=_params(),
    )(zf, bf)


def _fgate_bwd(df, zf, bf):
    S = zf.shape[0]
    nc = S // CHUNK

    def body(df_ref, zf_ref, bf_ref, dzf_ref, dbf_ref):
        lower = (_iota2((CHUNK, CHUNK), 0) >= _iota2((CHUNK, CHUNK), 1)).astype(BF16)
        carry = jnp.zeros((8, 1), F32)
        dbf = jnp.zeros((1, LANES), F32)
        for c in reversed(range(nc)):
            sl = slice(c * CHUNK, (c + 1) * CHUNK)
            df = df_ref[:, sl]
            r = _dot3(df, lower) + carry
            carry = carry + jnp.sum(df, axis=-1, keepdims=True)
            r_cols = jnp.concatenate([r, jnp.zeros((CHUNK - 8, CHUNK), F32)], axis=0).T
            t = zf_ref[sl, :] + bf_ref[...]
            dz = r_cols * (1.0 / (1.0 + jnp.exp(t)))
            dzf_ref[sl, :] = dz.astype(BF16)
            dbf = dbf + jnp.sum(dz, axis=0, keepdims=True)
        dbf_ref[...] = dbf

    return pl.pallas_call(
        body, name="fgate_bwd",
        out_shape=[jax.ShapeDtypeStruct((S, LANES), BF16), jax.ShapeDtypeStruct((1, LANES), F32)],
        compiler_params=_params(),
    )(df, zf, bf)


_NEG = -1e30


def _attn_fwd(zm, fcol, frow, T):
    S = zm.shape[0]
    H = fcol.shape[0]
    nb = S // T
    scale = 1.0 / math.sqrt(HEAD_DIM)

    def body(q_ref, k_ref, v_ref, fq_ref, fk_ref, o_ref, lse_ref, m_s, l_s, acc_s):
        i = pl.program_id(1)
        j = pl.program_id(2)

        @pl.when(j == 0)
        def _():
            m_s[...] = jnp.full_like(m_s, _NEG)
            l_s[...] = jnp.zeros_like(l_s)
            acc_s[...] = jnp.zeros_like(acc_s)

        @pl.when(j <= i)
        def _():
            s = _dot(q_ref[...], k_ref[...], NT) * scale + (fq_ref[...] - fk_ref[...])
            keep = (_iota2((T, T), 1) + j * T) <= (_iota2((T, T), 0) + i * T)
            s = jnp.where(keep, s, _NEG)
            m_new = jnp.maximum(m_s[...], jnp.max(s, axis=-1, keepdims=True))
            alpha = jnp.exp(m_s[...] - m_new)
            p = jnp.exp(s - m_new)
            l_s[...] = alpha * l_s[...] + jnp.sum(p, axis=-1, keepdims=True)
            acc_s[...] = alpha * acc_s[...] + _dot(p.astype(BF16), v_ref[...])
            m_s[...] = m_new

        @pl.when(j == nb - 1)
        def _():
            o_ref[...] = acc_s[...] / l_s[...]
            lse_ref[...] = m_s[...] + jnp.log(l_s[...])

    nh = H
    return pl.pallas_call(
        body, name="attn_fwd", grid=(H, nb, nb),
        in_specs=[
            pl.BlockSpec((T, HEAD_DIM), lambda h, i, j: (i, h)),
            pl.BlockSpec((T, HEAD_DIM), lambda h, i, j: (jnp.minimum(j, i), nh + h)),
            pl.BlockSpec((T, HEAD_DIM), lambda h, i, j: (jnp.minimum(j, i), 2 * nh + h)),
            pl.BlockSpec((None, T, 1), lambda h, i, j: (h, i, 0)),
            pl.BlockSpec((None, 1, T), lambda h, i, j: (h, 0, jnp.minimum(j, i))),
        ],
        out_specs=[
            pl.BlockSpec((T, HEAD_DIM), lambda h, i, j: (i, h)),
            pl.BlockSpec((None, T, 1), lambda h, i, j: (h, i, 0)),
        ],
        out_shape=[jax.ShapeDtypeStruct((S, H * HEAD_DIM), F32), jax.ShapeDtypeStruct((H, S, 1), F32)],
        scratch_shapes=[pltpu.VMEM((T, 1), F32), pltpu.VMEM((T, 1), F32), pltpu.VMEM((T, HEAD_DIM), F32)],
        compiler_params=_params(("parallel", "parallel", "arbitrary")),
    )(zm, zm, zm, fcol, frow)


def _attn_delta(dattn, attn, tr):
    S, DA = attn.shape
    H = DA // HEAD_DIM

    def body(do_ref, o_ref, out_ref):
        lo = _iota2((DA, LANES), 1) * HEAD_DIM
        sel = ((_iota2((DA, LANES), 0) >= lo) & (_iota2((DA, LANES), 0) < lo + HEAD_DIM)).astype(BF16)
        d = _dot3(do_ref[...] * o_ref[...], sel)
        for c in range(tr // CHUNK):
            out_ref[:, c * CHUNK:(c + 1) * CHUNK] = d[c * CHUNK:(c + 1) * CHUNK, :].T[0:H, :]

    return pl.pallas_call(
        body, name="attn_delta", grid=(S // tr,),
        in_specs=[pl.BlockSpec((tr, DA), lambda i: (i, 0))] * 2,
        out_specs=pl.BlockSpec((H, tr), lambda i: (0, i)),
        out_shape=jax.ShapeDtypeStruct((H, S), F32),
        compiler_params=_params(("parallel",)),
    )(dattn, attn)


def _attn_bwd_dq(zm, dattn, fcol, frow, lse_col, delta_col, T):
    S = zm.shape[0]
    H = fcol.shape[0]
    nb = S // T
    scale = 1.0 / math.sqrt(HEAD_DIM)

    def body(q_ref, k_ref, v_ref, do_ref, fq_ref, fk_ref, lse_ref, dl_ref, dq_ref, rs_ref, acc_s, rs_s):
        i = pl.program_id(1)
        j = pl.program_id(2)

        @pl.when(j == 0)
        def _():
            acc_s[...] = jnp.zeros_like(acc_s)
            rs_s[...] = jnp.zeros_like(rs_s)

        @pl.when(j <= i)
        def _():
            s = _dot(q_ref[...], k_ref[...], NT) * scale + (fq_ref[...] - fk_ref[...])
            keep = (_iota2((T, T), 1) + j * T) <= (_iota2((T, T), 0) + i * T)
            p = jnp.exp(jnp.where(keep, s - lse_ref[...], _NEG))
            dp = _dot(do_ref[...].astype(BF16), v_ref[...], NT)
            ds = p * (dp - dl_ref[...])
            acc_s[...] += _dot(ds.astype(BF16), k_ref[...])
            rs_s[...] += jnp.sum(ds, axis=-1, keepdims=True)

        @pl.when(j == nb - 1)
        def _():
            dq_ref[...] = (acc_s[...] * scale).astype(BF16)
            rs_ref[...] = rs_s[...]

    nh = H
    col = pl.BlockSpec((None, T, 1), lambda h, i, j: (h, i, 0))
    return pl.pallas_call(
        body, name="attn_bwd_dq", grid=(H, nb, nb),
        in_specs=[
            pl.BlockSpec((T, HEAD_DIM), lambda h, i, j: (i, h)),
            pl.BlockSpec((T, HEAD_DIM), lambda h, i, j: (jnp.minimum(j, i), nh + h)),
            pl.BlockSpec((T, HEAD_DIM), lambda h, i, j: (jnp.minimum(j, i), 2 * nh + h)),
            pl.BlockSpec((T, HEAD_DIM), lambda h, i, j: (i, h)),
            col,
            pl.BlockSpec((None, 1, T), lambda h, i, j: (h, 0, jnp.minimum(j, i))),
            col, col,
        ],
        out_specs=[pl.BlockSpec((T, HEAD_DIM), lambda h, i, j: (i, h)), col],
        out_shape=[jax.ShapeDtypeStruct((S, H * HEAD_DIM), BF16), jax.ShapeDtypeStruct((H, S, 1), F32)],
        scratch_shapes=[pltpu.VMEM((T, HEAD_DIM), F32), pltpu.VMEM((T, 1), F32)],
        compiler_params=_params(("parallel", "parallel", "arbitrary")),
    )(zm, zm, zm, dattn, fcol, frow, lse_col, delta_col)


def _attn_bwd_dkv(zm, dattn, fcol, frow, lse_row, delta_row, rowsum_row, T):
    S = zm.shape[0]
    H = fcol.shape[0]
    nb = S // T
    scale = 1.0 / math.sqrt(HEAD_DIM)

    def body(q_ref, k_ref, v_ref, do_ref, fk_ref, fq_ref, lse_ref, dl_ref, rs_ref,
             dk_ref, dv_ref, df_ref, dk_s, dv_s, df_s):
        j = pl.program_id(1)
        i = pl.program_id(2)

        @pl.when(i == 0)
        def _():
            dk_s[...] = jnp.zeros_like(dk_s)
            dv_s[...] = jnp.zeros_like(dv_s)
            df_s[...] = jnp.zeros_like(df_s)

        @pl.when(i >= j)
        def _():
            st = _dot(k_ref[...], q_ref[...], NT) * scale + (fq_ref[...] - fk_ref[...])
            keep = (_iota2((T, T), 0) + j * T) <= (_iota2((T, T), 1) + i * T)
            pt = jnp.exp(jnp.where(keep, st - lse_ref[...], _NEG))
            do = do_ref[...].astype(BF16)
            dpt = _dot(v_ref[...], do, NT)
            dst = pt * (dpt - (dl_ref[...] + rs_ref[...]))
            dv_s[...] += _dot(pt.astype(BF16), do)
            dk_s[...] += _dot(dst.astype(BF16), q_ref[...])
            df_s[...] -= jnp.sum(dst, axis=-1, keepdims=True)

        @pl.when(i == nb - 1)
        def _():
            dk_ref[...] = (dk_s[...] * scale).astype(BF16)
            dv_ref[...] = dv_s[...].astype(BF16)
            df_ref[...] = df_s[...]

    nh = H
    row = pl.BlockSpec((None, 1, T), lambda h, j, i: (h, 0, jnp.maximum(i, j)))
    kv_out = pl.BlockSpec((T, HEAD_DIM), lambda h, j, i: (j, h))
    return pl.pallas_call(
        body, name="attn_bwd_dkv", grid=(H, nb, nb),
        in_specs=[
            pl.BlockSpec((T, HEAD_DIM), lambda h, j, i: (jnp.maximum(i, j), h)),
            pl.BlockSpec((T, HEAD_DIM), lambda h, j, i: (j, nh + h)),
            pl.BlockSpec((T, HEAD_DIM), lambda h, j, i: (j, 2 * nh + h)),
            pl.BlockSpec((T, HEAD_DIM), lambda h, j, i: (jnp.maximum(i, j), h)),
            pl.BlockSpec((None, T, 1), lambda h, j, i: (h, j, 0)),
            row, row, row, row,
        ],
        out_specs=[kv_out, kv_out, pl.BlockSpec((None, T, 1), lambda h, j, i: (h, j, 0))],
        out_shape=[jax.ShapeDtypeStruct((S, H * HEAD_DIM), BF16), jax.ShapeDtypeStruct((S, H * HEAD_DIM), BF16),
                   jax.ShapeDtypeStruct((H, S, 1), F32)],
        scratch_shapes=[pltpu.VMEM((T, HEAD_DIM), F32), pltpu.VMEM((T, HEAD_DIM), F32), pltpu.VMEM((T, 1), F32)],
        compiler_params=_params(("parallel", "parallel", "arbitrary")),
    )(zm, zm, zm, dattn, fcol, frow, lse_row, delta_row, rowsum_row)


def _ln_stats(x):
    mu = jnp.mean(x, axis=-1, keepdims=True)
    xc = x - mu
    rstd = lax.rsqrt(jnp.mean(xc * xc, axis=-1, keepdims=True) + EPS)
    return xc * rstd, rstd


def _tril_mask():
    return _iota2((CHUNK, CHUNK), 0) >= _iota2((CHUNK, CHUNK), 1)


def _gmlp_fwd(zm, ln_g, ln_b, w_s, bs_col, tr):
    S = zm.shape[0]
    H = w_s.shape[0]
    DG = H * HEAD_DIM

    def body(zu_ref, zv_ref, g_ref, b_ref, w_ref, bs_ref, out_ref):
        u = _gelu(zu_ref[...].astype(F32))
        y, _ = _ln_stats(_gelu(zv_ref[...].astype(F32)))
        v = (y * g_ref[...] + b_ref[...]).astype(BF16)
        mask = _tril_mask()
        for h in range(H):
            wc = jnp.where(mask, w_ref[h], 0.0).astype(BF16)
            cs = slice(h * HEAD_DIM, (h + 1) * HEAD_DIM)
            for c in range(tr // CHUNK):
                rs = slice(c * CHUNK, (c + 1) * CHUNK)
                mix = _dot(wc, v[rs, cs]) + bs_ref[h]
                out_ref[rs, cs] = u[rs, cs] * mix

    full = lambda a: pl.BlockSpec(a.shape, lambda i: (0,) * a.ndim)
    return pl.pallas_call(
        body, name="gmlp_fwd", grid=(S // tr,),
        in_specs=[pl.BlockSpec((tr, DG), lambda i: (i, 3)), pl.BlockSpec((tr, DG), lambda i: (i, 4)),
                  full(ln_g), full(ln_b), full(w_s), full(bs_col)],
        out_specs=pl.BlockSpec((tr, DG), lambda i: (i, 0)),
        out_shape=jax.ShapeDtypeStruct((S, DG), F32),
        compiler_params=_params(("parallel",)),
    )(zm, zm, ln_g, ln_b, w_s, bs_col)


def _gmlp_bwd(dgm, zm, ln_g, ln_b, w_s, w_st, bs_col, tr):
    S = zm.shape[0]
    H = w_s.shape[0]
    DG = H * HEAD_DIM

    def body(dg_ref, zu_ref, zv_ref, g_ref, b_ref, w_ref, wt_ref, bs_ref,
             dzu_ref, dzv_ref, dw_ref, dbs_ref, dlg_ref, dlb_ref, dv_s):
        @pl.when(pl.program_id(0) == 0)
        def _():
            dw_ref[...] = jnp.zeros_like(dw_ref)
            dbs_ref[...] = jnp.zeros_like(dbs_ref)
            dlg_ref[...] = jnp.zeros_like(dlg_ref)
            dlb_ref[...] = jnp.zeros_like(dlb_ref)

        zu = zu_ref[...].astype(F32)
        zv = zv_ref[...].astype(F32)
        u = _gelu(zu)
        y, rstd = _ln_stats(_gelu(zv))
        v = (y * g_ref[...] + b_ref[...]).astype(BF16)
        dgm_blk = dg_ref[...]
        mask = _tril_mask()
        mask_t = _iota2((CHUNK, CHUNK), 0) <= _iota2((CHUNK, CHUNK), 1)
        for h in range(H):
            wc = jnp.where(mask, w_ref[h], 0.0).astype(BF16)
            wct = jnp.where(mask_t, wt_ref[h], 0.0).astype(BF16)
            cs = slice(h * HEAD_DIM, (h + 1) * HEAD_DIM)
            dw = jnp.zeros((CHUNK, CHUNK), F32)
            dbs = jnp.zeros((CHUNK, 1), F32)
            for c in range(tr // CHUNK):
                rs = slice(c * CHUNK, (c + 1) * CHUNK)
                vch = v[rs, cs]
                mix = _dot(wc, vch) + bs_ref[h]
                dg = dgm_blk[rs, cs]
                dzu_ref[rs, cs] = (dg * mix * _gelu_grad(zu[rs, cs])).astype(BF16)
                dmix = dg * u[rs, cs]
                dbs = dbs + jnp.sum(dmix, axis=-1, keepdims=True)
                dmix_b = dmix.astype(BF16)
                dw = dw + _dot(dmix_b, vch, NT)
                dv_s[rs, cs] = _dot(wct, dmix_b)
            dw_ref[h] += jnp.where(mask, dw, 0.0)
            dbs_ref[h] += dbs
        dv = dv_s[...]
        dlg_ref[...] += jnp.sum(dv * y, axis=0, keepdims=True)
        dlb_ref[...] += jnp.sum(dv, axis=0, keepdims=True)
        dy = dv * g_ref[...]
        dgv = rstd * (dy - jnp.mean(dy, axis=-1, keepdims=True) - y * jnp.mean(dy * y, axis=-1, keepdims=True))
        dzv_ref[...] = (dgv * _gelu_grad(zv)).astype(BF16)

    full = lambda a: pl.BlockSpec(a.shape, lambda i: (0,) * a.ndim)
    rows = pl.BlockSpec((tr, DG), lambda i: (i, 0))
    return pl.pallas_call(
        body, name="gmlp_bwd", grid=(S // tr,),
        in_specs=[rows, pl.BlockSpec((tr, DG), lambda i: (i, 3)), pl.BlockSpec((tr, DG), lambda i: (i, 4)),
                  full(ln_g), full(ln_b), full(w_s), full(w_st), full(bs_col)],
        out_specs=[rows, rows, full(w_s), full(bs_col), full(ln_g), full(ln_b)],
        out_shape=[jax.ShapeDtypeStruct((S, DG), BF16), jax.ShapeDtypeStruct((S, DG), BF16),
                   jax.ShapeDtypeStruct(w_s.shape, F32), jax.ShapeDtypeStruct(bs_col.shape, F32),
                   jax.ShapeDtypeStruct(ln_g.shape, F32), jax.ShapeDtypeStruct(ln_b.shape, F32)],
        scratch_shapes=[pltpu.VMEM((tr, DG), F32)],
        compiler_params=_params(("arbitrary",)),
    )(dgm, zm, zm, ln_g, ln_b, w_s, w_st, bs_col)


def _me():
    return lax.axis_index("x"), lax.axis_index("y"), lax.axis_index("c")


_ANY = pl.BlockSpec(memory_space=pl.ANY)


def _all_gather(name, blk):
    R, C = blk.shape

    def body(x_ref, out_ref, send_sems, recv_sems, local_sem):
        x, y, c = _me()
        me, sibling = (x, y, c), (x, y, 1 - c)
        chips = [(1 - x, y), (x, 1 - y), (1 - x, 1 - y)]

        def slab(px, py, pc):
            return out_ref.at[4 * px + 2 * py + pc]

        def copy(k, block, to, src=None):
            return pltpu.make_async_remote_copy(
                src_ref=slab(*block) if src is None else src, dst_ref=slab(*block),
                send_sem=send_sems.at[k], recv_sem=recv_sems.at[k], device_id=to, device_id_type=MESH)

        mine = pltpu.make_async_copy(x_ref, slab(*me), local_sem)
        mine.start()
        first = [copy(0, me, sibling, src=x_ref)]
        first += [copy(1 + n, me, (*chip, c), src=x_ref) for n, chip in enumerate(chips)]
        for cp in first:
            cp.start()
        passed = [copy(4 + n, (*chip, c), sibling) for n, chip in enumerate(chips)]
        for n, chip in enumerate(chips):
            copy(1 + n, (*chip, c), me).wait_recv()
            passed[n].start()
        copy(0, sibling, me).wait_recv()
        for n, chip in enumerate(chips):
            copy(4 + n, (*chip, 1 - c), me).wait_recv()
        for cp in first + passed:
            cp.wait_send()
        mine.wait()

    return pl.pallas_call(
        body, name=name, out_shape=jax.ShapeDtypeStruct((N_DEV, R, C), blk.dtype),
        in_specs=[_ANY], out_specs=_ANY,
        scratch_shapes=[pltpu.SemaphoreType.DMA((7,)), pltpu.SemaphoreType.DMA((7,)), pltpu.SemaphoreType.DMA(())],
    )(blk)


def _rs_sibling(name, gb):
    _, _, R, C = gb.shape

    def body(g_ref, recv_ref, send_sems, recv_sems):
        x, y, c = _me()
        copies = [pltpu.make_async_remote_copy(
            src_ref=g_ref.at[p, 1 - c], dst_ref=recv_ref.at[p], send_sem=send_sems.at[p], recv_sem=recv_sems.at[p],
            device_id=(x, y, 1 - c), device_id_type=MESH) for p in range(4)]
        for cp in copies:
            cp.start()
        for cp in copies:
            cp.wait_recv()
        for cp in copies:
            cp.wait_send()

    return pl.pallas_call(
        body, name=name, out_shape=jax.ShapeDtypeStruct((4, R, C), gb.dtype),
        in_specs=[_ANY], out_specs=_ANY,
        scratch_shapes=[pltpu.SemaphoreType.DMA((4,)), pltpu.SemaphoreType.DMA((4,))],
    )(gb)


def _rs_chips(name, hb):
    _, R, C = hb.shape

    def body(h_ref, recv_ref, send_sems, recv_sems):
        x, y, c = _me()
        chips = [(1 - x, y), (x, 1 - y), (1 - x, 1 - y)]
        copies = [pltpu.make_async_remote_copy(
            src_ref=h_ref.at[2 * cx + cy], dst_ref=recv_ref.at[n], send_sem=send_sems.at[n], recv_sem=recv_sems.at[n],
            device_id=(cx, cy, c), device_id_type=MESH) for n, (cx, cy) in enumerate(chips)]
        for cp in copies:
            cp.start()
        for cp in copies:
            cp.wait_recv()
        for cp in copies:
            cp.wait_send()

    return pl.pallas_call(
        body, name=name, out_shape=jax.ShapeDtypeStruct((3, R, C), hb.dtype),
        in_specs=[_ANY], out_specs=_ANY,
        scratch_shapes=[pltpu.SemaphoreType.DMA((3,)), pltpu.SemaphoreType.DMA((3,))],
    )(hb)


def _row_tile(R, C, itemsize=4, target_bytes=2 * 1024 * 1024):
    tr = R
    while tr % 2 == 0 and tr * C * itemsize > target_bytes and (tr // 2) % 16 == 0:
        tr //= 2
    return tr


def _rs_add1(name, g4, recv, c_idx):
    _, _, R, C = g4.shape
    tr = _row_tile(R, C)

    def body(c_ref, g_ref, r_ref, h_ref, hb_ref):
        h = g_ref[...] + r_ref[...].astype(F32)
        h_ref[...] = h
        hb_ref[...] = h.astype(BF16)

    blk = pl.BlockSpec((None, tr, C), lambda p, i, c_ref: (p, i, 0))
    return pl.pallas_call(
        body, name=name,
        grid_spec=pltpu.PrefetchScalarGridSpec(
            num_scalar_prefetch=1, grid=(4, R // tr),
            in_specs=[pl.BlockSpec((None, None, tr, C), lambda p, i, c_ref: (p, c_ref[0], i, 0)), blk],
            out_specs=[blk, blk]),
        out_shape=[jax.ShapeDtypeStruct((4, R, C), F32), jax.ShapeDtypeStruct((4, R, C), BF16)],
        compiler_params=_params(("parallel", "parallel")),
    )(c_idx, g4, recv)


def _rs_add2(name, h, recv, p_idx):
    _, R, C = h.shape
    tr = _row_tile(R, C)

    def body(p_ref, h_ref, r_ref, out_ref):
        out_ref[...] = ((h_ref[...] + r_ref[0].astype(F32)) + r_ref[1].astype(F32)) + r_ref[2].astype(F32)

    return pl.pallas_call(
        body, name=name,
        grid_spec=pltpu.PrefetchScalarGridSpec(
            num_scalar_prefetch=1, grid=(R // tr,),
            in_specs=[pl.BlockSpec((None, tr, C), lambda i, p_ref: (p_ref[0], i, 0)),
                      pl.BlockSpec((3, tr, C), lambda i, p_ref: (0, i, 0))],
            out_specs=pl.BlockSpec((tr, C), lambda i, p_ref: (i, 0))),
        out_shape=jax.ShapeDtypeStruct((R, C), F32),
        compiler_params=_params(("parallel",)),
    )(p_idx, h, recv)


def _reduce_scatter(tag, g4):
    x, y, c = _me()
    c_idx = jnp.reshape(c, (1,)).astype(jnp.int32)
    p_idx = jnp.reshape(2 * x + y, (1,)).astype(jnp.int32)
    recv1 = _rs_sibling("rs_sibling_" + tag, g4.astype(BF16))
    h, hb = _rs_add1("rs_add1_" + tag, g4, recv1, c_idx)
    recv2 = _rs_chips("rs_chips_" + tag, hb)
    return _rs_add2("rs_add2_" + tag, h, recv2, p_idx)


def _sum8(name, g):
    _, R, C = g.shape

    def body(g_ref, out_ref):
        acc = g_ref[0]
        for d in range(1, N_DEV):
            acc = acc + g_ref[d]
        out_ref[...] = acc

    return pl.pallas_call(body, name=name, out_shape=jax.ShapeDtypeStruct((R, C), F32),
                          compiler_params=_params())(g)


def _adamw(name, w, g, m, v):
    R, C = w.shape
    tr = _row_tile(R, C, target_bytes=1024 * 1024)

    def fn(w, g, m, v):
        m = ADAM_B1 * m + (1.0 - ADAM_B1) * g
        v = ADAM_B2 * v + (1.0 - ADAM_B2) * (g * g)
        m_hat = m / (1.0 - ADAM_B1 ** ADAM_STEP)
        v_hat = v / (1.0 - ADAM_B2 ** ADAM_STEP)
        delta = -ADAM_LR * (m_hat / (jnp.sqrt(v_hat) + ADAM_EPS) + ADAM_WD * w)
        return (delta, m, v), ()

    return _row_call(name, fn, [w, g, m, v], [], [(C, F32)] * 3, [], tr)


def _pack(parts):
    rows = []
    for a in parts:
        flat = a.reshape(-1).astype(F32)
        n = -(-flat.shape[0] // LANES) * LANES
        rows.append(jnp.pad(flat, (0, n - flat.shape[0])).reshape(-1, LANES))
    packed = jnp.concatenate(rows, axis=0)
    pad = -packed.shape[0] % 8
    return jnp.pad(packed, ((0, pad), (0, 0)))


def _unpack(packed, shapes):
    out = []
    r = 0
    for shp in shapes:
        n = math.prod(shp)
        nr = -(-n // LANES)
        out.append(packed[r:r + nr].reshape(-1)[:n].reshape(shp))
        r += nr
    return out


def kernel(x, norm_mix_g, w_in, b_f, gmlp_ln_g, gmlp_ln_b, w_s, b_s, attn_out_g, gmlp_out_g, w_out, norm_ffn_g, w_ff1, w_ff2, norm_final_g, loss_target, m_norm_mix_g, m_w_in, m_b_f, m_gmlp_ln_g, m_gmlp_ln_b, m_w_s, m_b_s, m_attn_out_g, m_gmlp_out_g, m_w_out, m_norm_ffn_g, m_w_ff1, m_w_ff2, m_norm_final_g, v_norm_mix_g, v_w_in, v_b_f, v_gmlp_ln_g, v_gmlp_ln_b, v_w_s, v_b_s, v_attn_out_g, v_gmlp_out_g, v_w_out, v_norm_ffn_g, v_w_ff1, v_w_ff2, v_norm_final_g):
    S, D = x.shape[1], x.shape[2]
    H = b_f.shape[1]
    DA = H * HEAD_DIM
    DG = gmlp_ln_g.shape[1]
    DQKV = 3 * DA
    DMAIN = DQKV + 2 * DG
    DIN = DMAIN + H
    DFF = w_ff1.shape[2] * N_DEV
    w_in_cols = w_in.shape[2]
    assert DIN == w_in_cols * N_DEV and DA == DG and D == DA + DG

    T_ATT = min(T_ATT_MAX, S)
    TR = min(TR_MAX, S)

    x0 = x[0]
    tgt = loss_target[0]
    g_final = norm_final_g.reshape(1, D)

    flat_rows = D * w_in_cols // LANES
    w_in_all = _all_gather("ag_w_in", w_in[0].astype(BF16).reshape(flat_rows, LANES))
    w_in_full = w_in_all.reshape(N_DEV, D, w_in_cols).transpose(1, 0, 2).reshape(D, DIN)
    w_main = jnp.concatenate([w_in_full[:, :DQKV], w_in_full[:, DQKV + H:]], axis=1)
    w_f = jnp.pad(w_in_full[:, DQKV:DQKV + H], ((0, 0), (0, LANES - H)))
    w_out_full = _all_gather("ag_w_out", w_out[0].astype(BF16)).reshape(D, D)
    w_ff1_all = _all_gather("ag_w_ff1", w_ff1[0].astype(BF16))
    w_ff2_full = _all_gather("ag_w_ff2", w_ff2[0].astype(BF16)).reshape(DFF, D)
    FB = DFF // N_DEV

    (h,), _ = _row_call("rms_mix", lambda xb, g: ((_rms_fwd(xb, g),), ()), [x0], [norm_mix_g], [(D, BF16)], [], TR)
    (zm,) = _mm_nn("in_proj", h, w_main, [BF16], 1024, 1024, 512)
    (zf,) = _mm_nn("in_proj_f", h, w_f, [F32], 1024, LANES, 2048)
    bf_pad = jnp.pad(b_f, ((0, 0), (0, LANES - H)))
    f_row = _fgate_fwd(zf, bf_pad)
    f_col3 = f_row.reshape(H, S, 1)
    f_row3 = f_row.reshape(H, 1, S)
    attn, lse_col3 = _attn_fwd(zm, f_col3, f_row3, T_ATT)
    bs_col = b_s[0].reshape(H, CHUNK, 1)
    gm = _gmlp_fwd(zm, gmlp_ln_g, gmlp_ln_b, w_s[0], bs_col, TR)

    def merge_fn(a, g, ga, gg):
        return (jnp.concatenate([_rms_fwd(a, ga), _rms_fwd(g, gg)], axis=1),), ()
    (merged,), _ = _row_call("rms_merge", merge_fn, [attn, gm], [attn_out_g, gmlp_out_g], [(D, BF16)], [], TR)

    (x1,) = _mm_nn("out_proj", merged, w_out_full, [F32], 1024, 1024, 512, extras=[x0],
                   epilogue=lambda acc, r: (acc + r,))
    (h2,), _ = _row_call("rms_ffn", lambda xb, g: ((_rms_fwd(xb, g),), ()), [x1], [norm_ffn_g], [(D, BF16)], [], TR)

    tm, tn, tk = min(1024, S), min(1024, FB), min(512, D)
    o_spec = pl.BlockSpec((tm, tn), lambda i, j, k: (i, j))

    def relu_sq(acc):
        a = jnp.maximum(acc, 0.0)
        return a, a * a
    nj = FB // tn
    a_act, a_sq = _mm(
        "ff1", (S // tm, DFF // tn, D // tk), h2, pl.BlockSpec((tm, tk), lambda i, j, k: (i, k)),
        w_ff1_all, pl.BlockSpec((None, tk, tn), lambda i, j, k: (j // nj, k, j % nj)), NN, (tm, tn),
        [jax.ShapeDtypeStruct((S, DFF), BF16)] * 2, [o_spec] * 2, epilogue=relu_sq)
    (x2,) = _mm_nn("ff2", a_sq, w_ff2_full, [F32], 1024, 1024, 512, extras=[x1], epilogue=lambda acc, r: (acc + r,))

    def head_fn(xb, t, g):
        rstd = lax.rsqrt(jnp.mean(xb * xb, axis=-1, keepdims=True) + EPS)
        xhat = xb * rstd
        err = xhat * g - t
        loss = 0.5 * jnp.sum(jnp.mean(err * err, axis=-1, keepdims=True), axis=0, keepdims=True)
        dy = err * (1.0 / D)
        dg = jnp.sum(dy * xhat, axis=0, keepdims=True)
        dxhat = dy * g
        dx = rstd * (dxhat - xhat * jnp.mean(dxhat * xhat, axis=-1, keepdims=True))
        return (dx, dx), (dg, jnp.broadcast_to(loss, (1, LANES)))
    (dx2, dx2_b), (dg_final, loss_part) = _row_call(
        "loss_head", head_fn, [x2, tgt], [g_final], [(D, F32), (D, BF16)], [D, LANES], TR)

    (da,) = _mm_nt("ff2_dx", dx2_b, w_ff2_full, [BF16], 1024, 1024, 512, extras=[a_act],
                   epilogue=lambda acc, a: (2.0 * a.astype(F32) * acc,))
    (dw_ff2,) = _mm_tn("ff2_dw", a_sq, dx2_b, [F32], 512, 2048, 512)
    tm2, tk2 = min(512, D), min(512, S)
    (dw_ff1,) = _mm(
        "ff1_dw", (D // tm2, DFF // FB, S // tk2), h2, pl.BlockSpec((tk2, tm2), lambda i, j, k: (k, i)),
        da, pl.BlockSpec((tk2, FB), lambda i, j, k: (k, j)), TN, (tm2, FB),
        [jax.ShapeDtypeStruct((N_DEV, D, FB), F32)], [pl.BlockSpec((None, tm2, FB), lambda i, j, k: (j, i, 0))])
    tkb = min(512, FB)
    nkb = FB // tkb
    tnb = min(1024, D)
    (dh2,) = _mm(
        "ff1_dx", (S // tm, D // tnb, DFF // tkb), da, pl.BlockSpec((tm, tkb), lambda i, j, k: (i, k)),
        w_ff1_all, pl.BlockSpec((None, tnb, tkb), lambda i, j, k: (k // nkb, j, k % nkb)), NT, (tm, tnb),
        [jax.ShapeDtypeStruct((S, D), F32)], [pl.BlockSpec((tm, tnb), lambda i, j, k: (i, j))])

    def ffn_bwd_fn(dh, xb, dres, g):
        dx, dg = _rms_bwd(dh, xb, g)
        dx = dx + dres
        return (dx, dx), (dg,)
    (dx1, dx1_b), (dg_ffn,) = _row_call("rms_ffn_bwd", ffn_bwd_fn, [dh2, x1, dx2], [norm_ffn_g],
                                        [(D, F32), (D, BF16)], [D], TR)

    (dmerged,) = _mm_nt("out_proj_dx", dx1_b, w_out_full, [F32], 1024, 1024, 512)
    (dw_out,) = _mm_tn("out_proj_dw", merged, dx1_b, [F32], 512, 2048, 512)

    def merge_bwd_fn(dm, a, g, ga, gg):
        da_, dga = _rms_bwd(dm[:, :DA], a, ga)
        dg_, dgg = _rms_bwd(dm[:, DA:], g, gg)
        return (da_, dg_), (dga, dgg)
    (dattn, dgm), (dg_attn, dg_gmlp) = _row_call(
        "rms_merge_bwd", merge_bwd_fn, [dmerged, attn, gm], [attn_out_g, gmlp_out_g], [(DA, F32), (DG, F32)], [DA, DG], TR)

    w_st = jnp.swapaxes(w_s[0], 1, 2)
    dzu, dzv, dw_s, dbs_col, dln_g, dln_b = _gmlp_bwd(dgm, zm, gmlp_ln_g, gmlp_ln_b, w_s[0], w_st, bs_col, TR)

    delta_row = _attn_delta(dattn, attn, TR)
    lse_row3 = lse_col3.reshape(H, 1, S)
    dq, ds_rowsum = _attn_bwd_dq(zm, dattn, f_col3, f_row3, lse_col3, delta_row.reshape(H, S, 1), T_ATT)
    dk, dv, df_col3 = _attn_bwd_dkv(zm, dattn, f_col3, f_row3, lse_row3, delta_row.reshape(H, 1, S),
                                    ds_rowsum.reshape(H, 1, S), T_ATT)
    dzf, dbf = _fgate_bwd(df_col3.reshape(H, S), zf, bf_pad)

    dz_main = jnp.concatenate([dq, dk, dv, dzu, dzv], axis=1)
    (dw_main,) = _mm_tn("in_proj_dw", h, dz_main, [F32], 512, 1024, 512)
    (dw_f,) = _mm_tn("in_proj_f_dw", h, dzf, [F32], 512, LANES, 512)
    (dh_f,) = _mm_nt("in_proj_f_dx", dzf, w_f, [F32], 1024, 1024, LANES)
    (dh,) = _mm_nt("in_proj_dx", dz_main, w_main, [F32], 1024, 1024, 512, extras=[dh_f],
                   epilogue=lambda acc, r: (acc + r,))

    def mix_bwd_fn(dhb, xb, dres, g):
        dx, dg = _rms_bwd(dhb, xb, g)
        return (dx + dres,), (dg,)
    (grad_x,), (dg_mix,) = _row_call("rms_mix_bwd", mix_bwd_fn, [dh, x0, dx1], [norm_mix_g], [(D, F32)], [D], TR)

    dw_in_full = jnp.concatenate([dw_main[:, :DQKV], dw_f[:, :H], dw_main[:, DQKV:]], axis=1)
    g4_in = dw_in_full.reshape(D, N_DEV, w_in_cols).transpose(1, 0, 2).reshape(4, 2, flat_rows, LANES)
    g_w_in = _reduce_scatter("w_in", g4_in).reshape(D, w_in_cols)
    g_w_out = _reduce_scatter("w_out", dw_out.reshape(4, 2, D // N_DEV, D))
    g_w_ff1 = _reduce_scatter("w_ff1", dw_ff1.reshape(4, 2, D, FB))
    g_w_ff2 = _reduce_scatter("w_ff2", dw_ff2.reshape(4, 2, FB, D))

    small_shapes = [norm_mix_g.shape, b_f.shape, gmlp_ln_g.shape, gmlp_ln_b.shape, w_s.shape, b_s.shape,
                    attn_out_g.shape, gmlp_out_g.shape, norm_ffn_g.shape, norm_final_g.shape]
    small_parts = [dg_mix, dbf[:, :H], dln_g, dln_b, dw_s, dbs_col, dg_attn, dg_gmlp, dg_ffn, dg_final]
    g_small = _sum8("small_sum", _all_gather("ag_small", _pack(small_parts)))
    w_small = _pack([norm_mix_g, b_f, gmlp_ln_g, gmlp_ln_b, w_s, b_s, attn_out_g, gmlp_out_g, norm_ffn_g, norm_final_g])
    m_small = _pack([m_norm_mix_g, m_b_f, m_gmlp_ln_g, m_gmlp_ln_b, m_w_s, m_b_s, m_attn_out_g, m_gmlp_out_g,
                     m_norm_ffn_g, m_norm_final_g])
    v_small = _pack([v_norm_mix_g, v_b_f, v_gmlp_ln_g, v_gmlp_ln_b, v_w_s, v_b_s, v_attn_out_g, v_gmlp_out_g,
                     v_norm_ffn_g, v_norm_final_g])
    (d_small, nm_small, nv_small), _ = _adamw("adamw_small", w_small, g_small, m_small, v_small)
    gs = _unpack(g_small, small_shapes)
    ds = _unpack(d_small, small_shapes)
    nms = _unpack(nm_small, small_shapes)
    nvs = _unpack(nv_small, small_shapes)

    big = {}
    for nm, w, g, m, v in (("w_in", w_in, g_w_in, m_w_in, v_w_in), ("w_out", w_out, g_w_out, m_w_out, v_w_out),
                           ("w_ff1", w_ff1, g_w_ff1, m_w_ff1, v_w_ff1), ("w_ff2", w_ff2, g_w_ff2, m_w_ff2, v_w_ff2)):
        (d_, m_, v_), _ = _adamw("adamw_" + nm, w[0], g, m[0], v[0])
        big[nm] = (g[None], d_[None], m_[None], v_[None])

    loss = lax.psum(loss_part[0, 0], ("x", "y", "c"))

    def leaves(n):
        sm = (gs, ds, nms, nvs)[n]
        return [sm[0], big["w_in"][n], sm[1], sm[2], sm[3], sm[4], sm[5], sm[6], sm[7], big["w_out"][n], sm[8],
                big["w_ff1"][n], big["w_ff2"][n], sm[9]]

    return (loss, grad_x[None], *leaves(0), *leaves(1), *leaves(2), *leaves(3))
```

```python
import functools
import math

import jax
import jax.numpy as jnp
from jax import lax
from jax.experimental import pallas as pl
from jax.experimental.pallas import tpu as pltpu

F32 = jnp.float32
BF16 = jnp.bfloat16
MESH = pl.DeviceIdType.MESH

HEAD_DIM = 128
CHUNK = 128
EPS = 1e-6
LANES = 128
N_DEV = 8

ADAM_LR = 0.001
ADAM_B1 = 0.9
ADAM_B2 = 0.999
ADAM_EPS = 1e-08
ADAM_WD = 0.01
ADAM_STEP = 10

VMEM_LIMIT_BYTES = 56 * 1024 * 1024
T_ATT_MAX = 512
TR_MAX = 256

NN = ((1,), (0,))
NT = ((1,), (1,))
TN = ((0,), (0,))


def _params(sem=None):
    return pltpu.CompilerParams(dimension_semantics=sem, vmem_limit_bytes=VMEM_LIMIT_BYTES)


def _dot(a, b, contract=NN):
    return lax.dot_general(a, b, (contract, ((), ())), preferred_element_type=F32)


def _dot3(x, t):
    x1 = x.astype(BF16)
    r1 = x - x1.astype(F32)
    x2 = r1.astype(BF16)
    x3 = (r1 - x2.astype(F32)).astype(BF16)
    return _dot(x1, t) + _dot(x2, t) + _dot(x3, t)


def _iota2(shape, dim):
    return lax.broadcasted_iota(jnp.int32, shape, dim)


def _row_call(name, fn, row_ins, bcast_ins, row_outs, acc_outs, tr):
    S = row_ins[0].shape[0]
    assert S % tr == 0
    n_ri, n_bi, n_ro, n_ao = len(row_ins), len(bcast_ins), len(row_outs), len(acc_outs)

    def body(*refs):
        ins = [r[...] for r in refs[:n_ri + n_bi]]
        ro_refs = refs[n_ri + n_bi:n_ri + n_bi + n_ro]
        ao_refs = refs[n_ri + n_bi + n_ro:]
        ro, ao = fn(*ins)
        for r, v in zip(ro_refs, ro):
            r[...] = v.astype(r.dtype)
        if n_ao:
            @pl.when(pl.program_id(0) == 0)
            def _():
                for r in ao_refs:
                    r[...] = jnp.zeros_like(r)
            for r, v in zip(ao_refs, ao):
                r[...] += v

    in_specs = [pl.BlockSpec((tr, a.shape[1]), lambda i: (i, 0)) for a in row_ins]
    in_specs += [pl.BlockSpec(a.shape, lambda i: (0, 0)) for a in bcast_ins]
    out_specs = [pl.BlockSpec((tr, d), lambda i: (i, 0)) for d, _ in row_outs]
    out_specs += [pl.BlockSpec((1, d), lambda i: (0, 0)) for d in acc_outs]
    out_shape = [jax.ShapeDtypeStruct((S, d), dt) for d, dt in row_outs]
    out_shape += [jax.ShapeDtypeStruct((1, d), F32) for d in acc_outs]
    outs = pl.pallas_call(
        body, name=name, grid=(S // tr,), in_specs=in_specs, out_specs=out_specs, out_shape=out_shape,
        compiler_params=_params(("arbitrary",) if n_ao else ("parallel",)),
    )(*row_ins, *bcast_ins)
    return outs[:n_ro], outs[n_ro:]


def _rms_fwd(x, g):
    rstd = lax.rsqrt(jnp.mean(x * x, axis=-1, keepdims=True) + EPS)
    return x * rstd * g


def _rms_bwd(dy, x, g):
    rstd = lax.rsqrt(jnp.mean(x * x, axis=-1, keepdims=True) + EPS)
    xhat = x * rstd
    dg = jnp.sum(dy * xhat, axis=0, keepdims=True)
    dxhat = dy * g
    dx = rstd * (dxhat - xhat * jnp.mean(dxhat * xhat, axis=-1, keepdims=True))
    return dx, dg


_GELU_C = math.sqrt(2.0 / math.pi)


def _gelu(x):
    return 0.5 * x * (1.0 + jnp.tanh(_GELU_C * (x + 0.044715 * (x * x * x))))


def _gelu_grad(x):
    t = jnp.tanh(_GELU_C * (x + 0.044715 * (x * x * x)))
    return 0.5 * (1.0 + t) + 0.5 * x * (1.0 - t * t) * (_GELU_C * (1.0 + 3.0 * 0.044715 * (x * x)))


def _me():
    return lax.axis_index("x"), lax.axis_index("y"), lax.axis_index("c")


def _other_chips(x, y):
    return [(1 - x, y), (x, 1 - y), (1 - x, 1 - y)]


_ANY = pl.BlockSpec(memory_space=pl.ANY)


class _Job:
    def __init__(self, ins, outs, n_sems, make, aliases=None):
        self.ins, self.outs, self.n_sems, self.make, self.aliases = ins, outs, n_sems, make, aliases or {}


def _job_gather_chips(blk):
    R, C = blk.shape

    def make(ins, outs, send_sems, recv_sems, base):
        (x_ref,), (out_ref,) = ins, outs
        x, y, c = _me()
        mine = 4 * x + 2 * y + c
        targets = [(x, y, 1 - c)] + [(cx, cy, c) for cx, cy in _other_chips(x, y)]

        def copy(k, slab, to):
            return pltpu.make_async_remote_copy(
                src_ref=x_ref, dst_ref=out_ref.at[slab], send_sem=send_sems.at[base + k],
                recv_sem=recv_sems.at[base + k], device_id=to, device_id_type=MESH)

        starts = [copy(k, mine, to) for k, to in enumerate(targets)]
        arrivals = [copy(k, 4 * tx + 2 * ty + tc, (tx, ty, tc)) for k, (tx, ty, tc) in enumerate(targets)]
        local = [pltpu.make_async_copy(x_ref, out_ref.at[mine], send_sems.at[base + 4])]
        return starts, arrivals, local

    return _Job([blk], [jax.ShapeDtypeStruct((N_DEV, R, C), blk.dtype)], 5, make)


def _job_gather_sibling(part):
    def make(ins, outs, send_sems, recv_sems, base):
        (out_ref,) = outs
        x, y, c = _me()

        def copy(k, slab):
            return pltpu.make_async_remote_copy(
                src_ref=out_ref.at[slab], dst_ref=out_ref.at[slab], send_sem=send_sems.at[base + k],
                recv_sem=recv_sems.at[base + k], device_id=(x, y, 1 - c), device_id_type=MESH)

        chips = _other_chips(x, y)
        starts = [copy(k, 4 * cx + 2 * cy + c) for k, (cx, cy) in enumerate(chips)]
        arrivals = [copy(k, 4 * cx + 2 * cy + (1 - c)) for k, (cx, cy) in enumerate(chips)]
        return starts, arrivals, []

    return _Job([part], [jax.ShapeDtypeStruct(part.shape, part.dtype)], 3, make, aliases={0: 0})


def _job_scatter_sibling(gb):
    _, _, R, C = gb.shape

    def make(ins, outs, send_sems, recv_sems, base):
        (g_ref,), (recv_ref,) = ins, outs
        x, y, c = _me()
        copies = [pltpu.make_async_remote_copy(
            src_ref=g_ref.at[p, 1 - c], dst_ref=recv_ref.at[p], send_sem=send_sems.at[base + p],
            recv_sem=recv_sems.at[base + p], device_id=(x, y, 1 - c), device_id_type=MESH) for p in range(4)]
        return copies, copies, []

    return _Job([gb], [jax.ShapeDtypeStruct((4, R, C), gb.dtype)], 4, make)


def _job_scatter_chips(hb):
    _, R, C = hb.shape

    def make(ins, outs, send_sems, recv_sems, base):
        (h_ref,), (recv_ref,) = ins, outs
        x, y, c = _me()
        copies = [pltpu.make_async_remote_copy(
            src_ref=h_ref.at[2 * cx + cy], dst_ref=recv_ref.at[n], send_sem=send_sems.at[base + n],
            recv_sem=recv_sems.at[base + n], device_id=(cx, cy, c), device_id_type=MESH)
            for n, (cx, cy) in enumerate(_other_chips(x, y))]
        return copies, copies, []

    return _Job([hb], [jax.ShapeDtypeStruct((3, R, C), hb.dtype)], 3, make)


def _carry_call(body, *, name, grid, in_specs, out_specs, out_shape, scratch_shapes, semantics, args, jobs=()):
    jobs = list(jobs)
    n_in, n_out, n_scr = len(in_specs), len(out_specs), len(scratch_shapes)
    j_ins = [a for j in jobs for a in j.ins]
    j_outs = [o for j in jobs for o in j.outs]
    n_sems = sum(j.n_sems for j in jobs)
    aliases = {}
    i0, o0 = n_in, n_out
    for j in jobs:
        for a, b in j.aliases.items():
            aliases[i0 + a] = o0 + b
        i0 += len(j.ins)
        o0 += len(j.outs)

    def full_body(*refs):
        ins = refs[:n_in]
        jin = refs[n_in:n_in + len(j_ins)]
        outs = refs[n_in + len(j_ins):n_in + len(j_ins) + n_out]
        jout = refs[n_in + len(j_ins) + n_out:n_in + len(j_ins) + n_out + len(j_outs)]
        scr = refs[n_in + len(j_ins) + n_out + len(j_outs):]
        if jobs:
            send_sems, recv_sems = scr[n_scr], scr[n_scr + 1]
            starts, arrivals, local = [], [], []
            base = i0 = o0 = 0
            for j in jobs:
                s, a, l = j.make(jin[i0:i0 + len(j.ins)], jout[o0:o0 + len(j.outs)], send_sems, recv_sems, base)
                starts += s
                arrivals += a
                local += l
                base += j.n_sems
                i0 += len(j.ins)
                o0 += len(j.outs)
            pids = [pl.program_id(d) for d in range(len(grid))]
            first = functools.reduce(jnp.logical_and, [p == 0 for p in pids])
            last = functools.reduce(jnp.logical_and, [p == n - 1 for p, n in zip(pids, grid)])

            @pl.when(first)
            def _():
                for cp in local + starts:
                    cp.start()

        body(*ins, *outs, *scr[:n_scr])

        if jobs:
            @pl.when(last)
            def _():
                for cp in arrivals:
                    cp.wait_recv()
                for cp in starts:
                    cp.wait_send()
                for cp in local:
                    cp.wait()

    sems = [pltpu.SemaphoreType.DMA((n_sems,)), pltpu.SemaphoreType.DMA((n_sems,))] if jobs else []
    res = pl.pallas_call(
        full_body, name=name, grid=grid,
        in_specs=list(in_specs) + [_ANY] * len(j_ins),
        out_specs=list(out_specs) + [_ANY] * len(j_outs),
        out_shape=list(out_shape) + j_outs,
        scratch_shapes=list(scratch_shapes) + sems,
        input_output_aliases=aliases,
        compiler_params=_params(("arbitrary",) * len(grid) if jobs else semantics),
    )(*args, *j_ins)
    body_res, job_res = res[:n_out], res[n_out:]
    per_job = []
    for j in jobs:
        per_job.append(job_res[:len(j.outs)])
        job_res = job_res[len(j.outs):]
    return body_res, per_job


def _mm(name, grid, a, a_spec, b, b_spec, contract, acc_shape, out_shape, out_specs, extras=(), epilogue=None, jobs=()):
    nk = grid[2]
    n_e = len(extras)
    n_o = len(out_shape)
    if epilogue is None:
        epilogue = lambda acc: (acc,)

    def body(a_ref, b_ref, *rest):
        e_refs = rest[:n_e]
        o_refs = rest[n_e:n_e + n_o]
        acc = rest[n_e + n_o]
        k = pl.program_id(2)

        @pl.when(k == 0)
        def _():
            acc[...] = jnp.zeros_like(acc)

        acc[...] += _dot(a_ref[...], b_ref[...], contract)

        @pl.when(k == nk - 1)
        def _():
            res = epilogue(acc[...], *[r[...] for r in e_refs])
            for o, r in zip(o_refs, res):
                o[...] = r.astype(o.dtype)

    outs, job_res = _carry_call(
        body, name=name, grid=grid, in_specs=[a_spec, b_spec] + [s for _, s in extras],
        out_specs=list(out_specs), out_shape=list(out_shape), scratch_shapes=[pltpu.VMEM(acc_shape, F32)],
        semantics=("parallel", "parallel", "arbitrary"), args=[a, b] + [e for e, _ in extras], jobs=jobs)
    return (outs, job_res) if jobs else outs


def _mm_nn(name, a, b, out_dtypes, tm, tn, tk, extras=(), epilogue=None, jobs=()):
    M, K = a.shape
    N = b.shape[1]
    tm, tn, tk = min(tm, M), min(tn, N), min(tk, K)
    o_spec = pl.BlockSpec((tm, tn), lambda i, j, k: (i, j))
    return _mm(name, (M // tm, N // tn, K // tk),
               a, pl.BlockSpec((tm, tk), lambda i, j, k: (i, k)),
               b, pl.BlockSpec((tk, tn), lambda i, j, k: (k, j)), NN, (tm, tn),
               [jax.ShapeDtypeStruct((M, N), dt) for dt in out_dtypes], [o_spec] * len(out_dtypes),
               [(e, o_spec) for e in extras], epilogue, jobs)


def _mm_nt(name, a, b, out_dtypes, tm, tn, tk, extras=(), epilogue=None, jobs=()):
    M, K = a.shape
    N = b.shape[0]
    tm, tn, tk = min(tm, M), min(tn, N), min(tk, K)
    o_spec = pl.BlockSpec((tm, tn), lambda i, j, k: (i, j))
    return _mm(name, (M // tm, N // tn, K // tk),
               a, pl.BlockSpec((tm, tk), lambda i, j, k: (i, k)),
               b, pl.BlockSpec((tn, tk), lambda i, j, k: (j, k)), NT, (tm, tn),
               [jax.ShapeDtypeStruct((M, N), dt) for dt in out_dtypes], [o_spec] * len(out_dtypes),
               [(e, o_spec) for e in extras], epilogue, jobs)


def _mm_tn(name, a, b, out_dtypes, tm, tn, tk, jobs=()):
    K, M = a.shape
    N = b.shape[1]
    tm, tn, tk = min(tm, M), min(tn, N), min(tk, K)
    o_spec = pl.BlockSpec((tm, tn), lambda i, j, k: (i, j))
    return _mm(name, (M // tm, N // tn, K // tk),
               a, pl.BlockSpec((tk, tm), lambda i, j, k: (k, i)),
               b, pl.BlockSpec((tk, tn), lambda i, j, k: (k, j)), TN, (tm, tn),
               [jax.ShapeDtypeStruct((M, N), dt) for dt in out_dtypes], [o_spec] * len(out_dtypes),
               epilogue=lambda acc: (acc,) * len(out_dtypes), jobs=jobs)


def _fgate_fwd(zf, bf):
    S = zf.shape[0]
    nc = S // CHUNK

    def body(zf_ref, bf_ref, f_ref):
        upper = (_iota2((CHUNK, CHUNK), 0) <= _iota2((CHUNK, CHUNK), 1)).astype(BF16)
        carry = jnp.zeros((8, 1), F32)
        for c in range(nc):
            t = zf_ref[c * CHUNK:(c + 1) * CHUNK, :] + bf_ref[...]
            lf = jnp.minimum(t, 0.0) - jnp.log(1.0 + jnp.exp(-jnp.abs(t)))
            lf_rows = lf.T[0:8, :]
            f_ref[:, c * CHUNK:(c + 1) * CHUNK] = _dot3(lf_rows, upper) + carry
            carry = carry + jnp.sum(lf_rows, axis=-1, keepdims=True)

    return pl.pallas_call(
        body, name="fgate_fwd", out_shape=jax.ShapeDtypeStruct((8, S), F32),
        compiler_params=_params(),
    )(zf, bf)


def _fgate_bwd(df, zf, bf):
    S = zf.shape[0]
    nc = S // CHUNK

    def body(df_ref, zf_ref, bf_ref, dzf_ref, dbf_ref):
        lower = (_iota2((CHUNK, CHUNK), 0) >= _iota2((CHUNK, CHUNK), 1)).astype(BF16)
        carry = jnp.zeros((8, 1), F32)
        dbf = jnp.zeros((1, LANES), F32)
        for c in reversed(range(nc)):
            sl = slice(c * CHUNK, (c + 1) * CHUNK)
            df = df_ref[:, sl]
            r = _dot3(df, lower) + carry
            carry = carry + jnp.sum(df, axis=-1, keepdims=True)
            r_cols = jnp.concatenate([r, jnp.zeros((CHUNK - 8, CHUNK), F32)], axis=0).T
            t = zf_ref[sl, :] + bf_ref[...]
            dz = r_cols * (1.0 / (1.0 + jnp.exp(t)))
            dzf_ref[sl, :] = dz.astype(BF16)
            dbf = dbf + jnp.sum(dz, axis=0, keepdims=True)
        dbf_ref[...] = dbf

    return pl.pallas_call(
        body, name="fgate_bwd",
        out_shape=[jax.ShapeDtypeStruct((S, LANES), BF16), jax.ShapeDtypeStruct((1, LANES), F32)],
        compiler_params=_params(),
    )(df, zf, bf)


_NEG = -1e30


def _attn_fwd(zm, fcol, frow, T, jobs=()):
    S = zm.shape[0]
    H = fcol.shape[0]
    nb = S // T
    scale = 1.0 / math.sqrt(HEAD_DIM)

    def body(q_ref, k_ref, v_ref, fq_ref, fk_ref, o_ref, lse_ref, m_s, l_s, acc_s):
        i = pl.program_id(1)
        j = pl.program_id(2)

        @pl.when(j == 0)
        def _():
            m_s[...] = jnp.full_like(m_s, _NEG)
            l_s[...] = jnp.zeros_like(l_s)
            acc_s[...] = jnp.zeros_like(acc_s)

        @pl.when(j <= i)
        def _():
            s = _dot(q_ref[...], k_ref[...], NT) * scale + (fq_ref[...] - fk_ref[...])
            keep = (_iota2((T, T), 1) + j * T) <= (_iota2((T, T), 0) + i * T)
            s = jnp.where(keep, s, _NEG)
            m_new = jnp.maximum(m_s[...], jnp.max(s, axis=-1, keepdims=True))
            alpha = jnp.exp(m_s[...] - m_new)
            p = jnp.exp(s - m_new)
            l_s[...] = alpha * l_s[...] + jnp.sum(p, axis=-1, keepdims=True)
            acc_s[...] = alpha * acc_s[...] + _dot(p.astype(BF16), v_ref[...])
            m_s[...] = m_new

        @pl.when(j == nb - 1)
        def _():
            o_ref[...] = acc_s[...] / l_s[...]
            lse_ref[...] = m_s[...] + jnp.log(l_s[...])

    nh = H
    return _carry_call(
        body, name="attn_fwd", grid=(H, nb, nb), jobs=jobs, args=[zm, zm, zm, fcol, frow],
        semantics=("parallel", "parallel", "arbitrary"),
        in_specs=[
            pl.BlockSpec((T, HEAD_DIM), lambda h, i, j: (i, h)),
            pl.BlockSpec((T, HEAD_DIM), lambda h, i, j: (jnp.minimum(j, i), nh + h)),
            pl.BlockSpec((T, HEAD_DIM), lambda h, i, j: (jnp.minimum(j, i), 2 * nh + h)),
            pl.BlockSpec((None, T, 1), lambda h, i, j: (h, i, 0)),
            pl.BlockSpec((None, 1, T), lambda h, i, j: (h, 0, jnp.minimum(j, i))),
        ],
        out_specs=[
            pl.BlockSpec((T, HEAD_DIM), lambda h, i, j: (i, h)),
            pl.BlockSpec((None, T, 1), lambda h, i, j: (h, i, 0)),
        ],
        out_shape=[jax.ShapeDtypeStruct((S, H * HEAD_DIM), F32), jax.ShapeDtypeStruct((H, S, 1), F32)],
        scratch_shapes=[pltpu.VMEM((T, 1), F32), pltpu.VMEM((T, 1), F32), pltpu.VMEM((T, HEAD_DIM), F32)],
    )


def _attn_delta(dattn, attn, tr):
    S, DA = attn.shape
    H = DA // HEAD_DIM

    def body(do_ref, o_ref, out_ref):
        lo = _iota2((DA, LANES), 1) * HEAD_DIM
        sel = ((_iota2((DA, LANES), 0) >= lo) & (_iota2((DA, LANES), 0) < lo + HEAD_DIM)).astype(BF16)
        d = _dot3(do_ref[...] * o_ref[...], sel)
        for c in range(tr // CHUNK):
            out_ref[:, c * CHUNK:(c + 1) * CHUNK] = d[c * CHUNK:(c + 1) * CHUNK, :].T[0:H, :]

    return pl.pallas_call(
        body, name="attn_delta", grid=(S // tr,),
        in_specs=[pl.BlockSpec((tr, DA), lambda i: (i, 0))] * 2,
        out_specs=pl.BlockSpec((H, tr), lambda i: (0, i)),
        out_shape=jax.ShapeDtypeStruct((H, S), F32),
        compiler_params=_params(("parallel",)),
    )(dattn, attn)


def _attn_bwd_dq(zm, dattn, fcol, frow, lse_col, delta_col, T, jobs=()):
    S = zm.shape[0]
    H = fcol.shape[0]
    nb = S // T
    scale = 1.0 / math.sqrt(HEAD_DIM)

    def body(q_ref, k_ref, v_ref, do_ref, fq_ref, fk_ref, lse_ref, dl_ref, dq_ref, rs_ref, acc_s, rs_s):
        i = pl.program_id(1)
        j = pl.program_id(2)

        @pl.when(j == 0)
        def _():
            acc_s[...] = jnp.zeros_like(acc_s)
            rs_s[...] = jnp.zeros_like(rs_s)

        @pl.when(j <= i)
        def _():
            s = _dot(q_ref[...], k_ref[...], NT) * scale + (fq_ref[...] - fk_ref[...])
            keep = (_iota2((T, T), 1) + j * T) <= (_iota2((T, T), 0) + i * T)
            p = jnp.exp(jnp.where(keep, s - lse_ref[...], _NEG))
            dp = _dot(do_ref[...].astype(BF16), v_ref[...], NT)
            ds = p * (dp - dl_ref[...])
            acc_s[...] += _dot(ds.astype(BF16), k_ref[...])
            rs_s[...] += jnp.sum(ds, axis=-1, keepdims=True)

        @pl.when(j == nb - 1)
        def _():
            dq_ref[...] = (acc_s[...] * scale).astype(BF16)
            rs_ref[...] = rs_s[...]

    nh = H
    col = pl.BlockSpec((None, T, 1), lambda h, i, j: (h, i, 0))
    return _carry_call(
        body, name="attn_bwd_dq", grid=(H, nb, nb), jobs=jobs,
        args=[zm, zm, zm, dattn, fcol, frow, lse_col, delta_col], semantics=("parallel", "parallel", "arbitrary"),
        in_specs=[
            pl.BlockSpec((T, HEAD_DIM), lambda h, i, j: (i, h)),
            pl.BlockSpec((T, HEAD_DIM), lambda h, i, j: (jnp.minimum(j, i), nh + h)),
            pl.BlockSpec((T, HEAD_DIM), lambda h, i, j: (jnp.minimum(j, i), 2 * nh + h)),
            pl.BlockSpec((T, HEAD_DIM), lambda h, i, j: (i, h)),
            col,
            pl.BlockSpec((None, 1, T), lambda h, i, j: (h, 0, jnp.minimum(j, i))),
            col, col,
        ],
        out_specs=[pl.BlockSpec((T, HEAD_DIM), lambda h, i, j: (i, h)), col],
        out_shape=[jax.ShapeDtypeStruct((S, H * HEAD_DIM), BF16), jax.ShapeDtypeStruct((H, S, 1), F32)],
        scratch_shapes=[pltpu.VMEM((T, HEAD_DIM), F32), pltpu.VMEM((T, 1), F32)],
    )


def _attn_bwd_dkv(zm, dattn, fcol, frow, lse_row, delta_row, rowsum_row, T, jobs=()):
    S = zm.shape[0]
    H = fcol.shape[0]
    nb = S // T
    scale = 1.0 / math.sqrt(HEAD_DIM)

    def body(q_ref, k_ref, v_ref, do_ref, fk_ref, fq_ref, lse_ref, dl_ref, rs_ref,
             dk_ref, dv_ref, df_ref, dk_s, dv_s, df_s):
        j = pl.program_id(1)
        i = pl.program_id(2)

        @pl.when(i == 0)
        def _():
            dk_s[...] = jnp.zeros_like(dk_s)
            dv_s[...] = jnp.zeros_like(dv_s)
            df_s[...] = jnp.zeros_like(df_s)

        @pl.when(i >= j)
        def _():
            st = _dot(k_ref[...], q_ref[...], NT) * scale + (fq_ref[...] - fk_ref[...])
            keep = (_iota2((T, T), 0) + j * T) <= (_iota2((T, T), 1) + i * T)
            pt = jnp.exp(jnp.where(keep, st - lse_ref[...], _NEG))
            do = do_ref[...].astype(BF16)
            dpt = _dot(v_ref[...], do, NT)
            dst = pt * (dpt - (dl_ref[...] + rs_ref[...]))
            dv_s[...] += _dot(pt.astype(BF16), do)
            dk_s[...] += _dot(dst.astype(BF16), q_ref[...])
            df_s[...] -= jnp.sum(dst, axis=-1, keepdims=True)

        @pl.when(i == nb - 1)
        def _():
            dk_ref[...] = (dk_s[...] * scale).astype(BF16)
            dv_ref[...] = dv_s[...].astype(BF16)
            df_ref[...] = df_s[...]

    nh = H
    row = pl.BlockSpec((None, 1, T), lambda h, j, i: (h, 0, jnp.maximum(i, j)))
    kv_out = pl.BlockSpec((T, HEAD_DIM), lambda h, j, i: (j, h))
    return _carry_call(
        body, name="attn_bwd_dkv", grid=(H, nb, nb), jobs=jobs,
        args=[zm, zm, zm, dattn, fcol, frow, lse_row, delta_row, rowsum_row],
        semantics=("parallel", "parallel", "arbitrary"),
        in_specs=[
            pl.BlockSpec((T, HEAD_DIM), lambda h, j, i: (jnp.maximum(i, j), h)),
            pl.BlockSpec((T, HEAD_DIM), lambda h, j, i: (j, nh + h)),
            pl.BlockSpec((T, HEAD_DIM), lambda h, j, i: (j, 2 * nh + h)),
            pl.BlockSpec((T, HEAD_DIM), lambda h, j, i: (jnp.maximum(i, j), h)),
            pl.BlockSpec((None, T, 1), lambda h, j, i: (h, j, 0)),
            row, row, row, row,
        ],
        out_specs=[kv_out, kv_out, pl.BlockSpec((None, T, 1), lambda h, j, i: (h, j, 0))],
        out_shape=[jax.ShapeDtypeStruct((S, H * HEAD_DIM), BF16), jax.ShapeDtypeStruct((S, H * HEAD_DIM), BF16),
                   jax.ShapeDtypeStruct((H, S, 1), F32)],
        scratch_shapes=[pltpu.VMEM((T, HEAD_DIM), F32), pltpu.VMEM((T, HEAD_DIM), F32), pltpu.VMEM((T, 1), F32)],
    )


def _ln_stats(x):
    mu = jnp.mean(x, axis=-1, keepdims=True)
    xc = x - mu
    rstd = lax.rsqrt(jnp.mean(xc * xc, axis=-1, keepdims=True) + EPS)
    return xc * rstd, rstd


def _tril_mask():
    return _iota2((CHUNK, CHUNK), 0) >= _iota2((CHUNK, CHUNK), 1)


def _gmlp_fwd(zm, ln_g, ln_b, w_s, bs_col, tr):
    S = zm.shape[0]
    H = w_s.shape[0]
    DG = H * HEAD_DIM

    def body(zu_ref, zv_ref, g_ref, b_ref, w_ref, bs_ref, out_ref):
        u = _gelu(zu_ref[...].astype(F32))
        y, _ = _ln_stats(_gelu(zv_ref[...].astype(F32)))
        v = (y * g_ref[...] + b_ref[...]).astype(BF16)
        mask = _tril_mask()
        for h in range(H):
            wc = jnp.where(mask, w_ref[h], 0.0).astype(BF16)
            cs = slice(h * HEAD_DIM, (h + 1) * HEAD_DIM)
            for c in range(tr // CHUNK):
                rs = slice(c * CHUNK, (c + 1) * CHUNK)
                mix = _dot(wc, v[rs, cs]) + bs_ref[h]
                out_ref[rs, cs] = u[rs, cs] * mix

    full = lambda a: pl.BlockSpec(a.shape, lambda i: (0,) * a.ndim)
    return pl.pallas_call(
        body, name="gmlp_fwd", grid=(S // tr,),
        in_specs=[pl.BlockSpec((tr, DG), lambda i: (i, 3)), pl.BlockSpec((tr, DG), lambda i: (i, 4)),
                  full(ln_g), full(ln_b), full(w_s), full(bs_col)],
        out_specs=pl.BlockSpec((tr, DG), lambda i: (i, 0)),
        out_shape=jax.ShapeDtypeStruct((S, DG), F32),
        compiler_params=_params(("parallel",)),
    )(zm, zm, ln_g, ln_b, w_s, bs_col)


def _gmlp_bwd(dgm, zm, ln_g, ln_b, w_s, w_st, bs_col, tr):
    S = zm.shape[0]
    H = w_s.shape[0]
    DG = H * HEAD_DIM

    def body(dg_ref, zu_ref, zv_ref, g_ref, b_ref, w_ref, wt_ref, bs_ref,
             dzu_ref, dzv_ref, dw_ref, dbs_ref, dlg_ref, dlb_ref, dv_s):
        @pl.when(pl.program_id(0) == 0)
        def _():
            dw_ref[...] = jnp.zeros_like(dw_ref)
            dbs_ref[...] = jnp.zeros_like(dbs_ref)
            dlg_ref[...] = jnp.zeros_like(dlg_ref)
            dlb_ref[...] = jnp.zeros_like(dlb_ref)

        zu = zu_ref[...].astype(F32)
        zv = zv_ref[...].astype(F32)
        u = _gelu(zu)
        y, rstd = _ln_stats(_gelu(zv))
        v = (y * g_ref[...] + b_ref[...]).astype(BF16)
        dgm_blk = dg_ref[...]
        mask = _tril_mask()
        mask_t = _iota2((CHUNK, CHUNK), 0) <= _iota2((CHUNK, CHUNK), 1)
        for h in range(H):
            wc = jnp.where(mask, w_ref[h], 0.0).astype(BF16)
            wct = jnp.where(mask_t, wt_ref[h], 0.0).astype(BF16)
            cs = slice(h * HEAD_DIM, (h + 1) * HEAD_DIM)
            dw = jnp.zeros((CHUNK, CHUNK), F32)
            dbs = jnp.zeros((CHUNK, 1), F32)
            for c in range(tr // CHUNK):
                rs = slice(c * CHUNK, (c + 1) * CHUNK)
                vch = v[rs, cs]
                mix = _dot(wc, vch) + bs_ref[h]
                dg = dgm_blk[rs, cs]
                dzu_ref[rs, cs] = (dg * mix * _gelu_grad(zu[rs, cs])).astype(BF16)
                dmix = dg * u[rs, cs]
                dbs = dbs + jnp.sum(dmix, axis=-1, keepdims=True)
                dmix_b = dmix.astype(BF16)
                dw = dw + _dot(dmix_b, vch, NT)
                dv_s[rs, cs] = _dot(wct, dmix_b)
            dw_ref[h] += jnp.where(mask, dw, 0.0)
            dbs_ref[h] += dbs
        dv = dv_s[...]
        dlg_ref[...] += jnp.sum(dv * y, axis=0, keepdims=True)
        dlb_ref[...] += jnp.sum(dv, axis=0, keepdims=True)
        dy = dv * g_ref[...]
        dgv = rstd * (dy - jnp.mean(dy, axis=-1, keepdims=True) - y * jnp.mean(dy * y, axis=-1, keepdims=True))
        dzv_ref[...] = (dgv * _gelu_grad(zv)).astype(BF16)

    full = lambda a: pl.BlockSpec(a.shape, lambda i: (0,) * a.ndim)
    rows = pl.BlockSpec((tr, DG), lambda i: (i, 0))
    return pl.pallas_call(
        body, name="gmlp_bwd", grid=(S // tr,),
        in_specs=[rows, pl.BlockSpec((tr, DG), lambda i: (i, 3)), pl.BlockSpec((tr, DG), lambda i: (i, 4)),
                  full(ln_g), full(ln_b), full(w_s), full(w_st), full(bs_col)],
        out_specs=[rows, rows, full(w_s), full(bs_col), full(ln_g), full(ln_b)],
        out_shape=[jax.ShapeDtypeStruct((S, DG), BF16), jax.ShapeDtypeStruct((S, DG), BF16),
                   jax.ShapeDtypeStruct(w_s.shape, F32), jax.ShapeDtypeStruct(bs_col.shape, F32),
                   jax.ShapeDtypeStruct(ln_g.shape, F32), jax.ShapeDtypeStruct(ln_b.shape, F32)],
        scratch_shapes=[pltpu.VMEM((tr, DG), F32)],
        compiler_params=_params(("arbitrary",)),
    )(dgm, zm, zm, ln_g, ln_b, w_s, w_st, bs_col)


def _all_gather(name, blk):
    R, C = blk.shape

    def body(x_ref, out_ref, send_sems, recv_sems, local_sem):
        x, y, c = _me()
        me, sibling = (x, y, c), (x, y, 1 - c)
        chips = [(1 - x, y), (x, 1 - y), (1 - x, 1 - y)]

        def slab(px, py, pc):
            return out_ref.at[4 * px + 2 * py + pc]

        def copy(k, block, to, src=None):
            return pltpu.make_async_remote_copy(
                src_ref=slab(*block) if src is None else src, dst_ref=slab(*block),
                send_sem=send_sems.at[k], recv_sem=recv_sems.at[k], device_id=to, device_id_type=MESH)

        mine = pltpu.make_async_copy(x_ref, slab(*me), local_sem)
        mine.start()
        first = [copy(0, me, sibling, src=x_ref)]
        first += [copy(1 + n, me, (*chip, c), src=x_ref) for n, chip in enumerate(chips)]
        for cp in first:
            cp.start()
        passed = [copy(4 + n, (*chip, c), sibling) for n, chip in enumerate(chips)]
        for n, chip in enumerate(chips):
            copy(1 + n, (*chip, c), me).wait_recv()
            passed[n].start()
        copy(0, sibling, me).wait_recv()
        for n, chip in enumerate(chips):
            copy(4 + n, (*chip, 1 - c), me).wait_recv()
        for cp in first + passed:
            cp.wait_send()
        mine.wait()

    return pl.pallas_call(
        body, name=name, out_shape=jax.ShapeDtypeStruct((N_DEV, R, C), blk.dtype),
        in_specs=[_ANY], out_specs=_ANY,
        scratch_shapes=[pltpu.SemaphoreType.DMA((7,)), pltpu.SemaphoreType.DMA((7,)), pltpu.SemaphoreType.DMA(())],
    )(blk)


def _row_tile(R, C, itemsize=4, target_bytes=2 * 1024 * 1024):
    tr = R
    while tr % 2 == 0 and tr * C * itemsize > target_bytes and (tr // 2) % 16 == 0:
        tr //= 2
    return tr


def _rs_add1(name, g4, recv, c_idx):
    _, _, R, C = g4.shape
    tr = _row_tile(R, C)

    def body(c_ref, g_ref, r_ref, h_ref, hb_ref):
        h = g_ref[...] + r_ref[...].astype(F32)
        h_ref[...] = h
        hb_ref[...] = h.astype(BF16)

    blk = pl.BlockSpec((None, tr, C), lambda p, i, c_ref: (p, i, 0))
    return pl.pallas_call(
        body, name=name,
        grid_spec=pltpu.PrefetchScalarGridSpec(
            num_scalar_prefetch=1, grid=(4, R // tr),
            in_specs=[pl.BlockSpec((None, None, tr, C), lambda p, i, c_ref: (p, c_ref[0], i, 0)), blk],
            out_specs=[blk, blk]),
        out_shape=[jax.ShapeDtypeStruct((4, R, C), F32), jax.ShapeDtypeStruct((4, R, C), BF16)],
        compiler_params=_params(("parallel", "parallel")),
    )(c_idx, g4, recv)


def _rs_add2(name, h, recv, p_idx):
    _, R, C = h.shape
    tr = _row_tile(R, C)

    def body(p_ref, h_ref, r_ref, out_ref):
        out_ref[...] = ((h_ref[...] + r_ref[0].astype(F32)) + r_ref[1].astype(F32)) + r_ref[2].astype(F32)

    return pl.pallas_call(
        body, name=name,
        grid_spec=pltpu.PrefetchScalarGridSpec(
            num_scalar_prefetch=1, grid=(R // tr,),
            in_specs=[pl.BlockSpec((None, tr, C), lambda i, p_ref: (p_ref[0], i, 0)),
                      pl.BlockSpec((3, tr, C), lambda i, p_ref: (0, i, 0))],
            out_specs=pl.BlockSpec((tr, C), lambda i, p_ref: (i, 0))),
        out_shape=jax.ShapeDtypeStruct((R, C), F32),
        compiler_params=_params(("parallel",)),
    )(p_idx, h, recv)


def _sum8(name, g):
    _, R, C = g.shape

    def body(g_ref, out_ref):
        acc = g_ref[0]
        for d in range(1, N_DEV):
            acc = acc + g_ref[d]
        out_ref[...] = acc

    return pl.pallas_call(body, name=name, out_shape=jax.ShapeDtypeStruct((R, C), F32),
                          compiler_params=_params())(g)


def _adamw(name, w, g, m, v):
    R, C = w.shape
    tr = _row_tile(R, C, target_bytes=1024 * 1024)

    def fn(w, g, m, v):
        m = ADAM_B1 * m + (1.0 - ADAM_B1) * g
        v = ADAM_B2 * v + (1.0 - ADAM_B2) * (g * g)
        m_hat = m / (1.0 - ADAM_B1 ** ADAM_STEP)
        v_hat = v / (1.0 - ADAM_B2 ** ADAM_STEP)
        delta = -ADAM_LR * (m_hat / (jnp.sqrt(v_hat) + ADAM_EPS) + ADAM_WD * w)
        return (delta, m, v), ()

    return _row_call(name, fn, [w, g, m, v], [], [(C, F32)] * 3, [], tr)


def _pack(parts):
    rows = []
    for a in parts:
        flat = a.reshape(-1).astype(F32)
        n = -(-flat.shape[0] // LANES) * LANES
        rows.append(jnp.pad(flat, (0, n - flat.shape[0])).reshape(-1, LANES))
    packed = jnp.concatenate(rows, axis=0)
    pad = -packed.shape[0] % 8
    return jnp.pad(packed, ((0, pad), (0, 0)))


def _unpack(packed, shapes):
    out = []
    r = 0
    for shp in shapes:
        n = math.prod(shp)
        nr = -(-n // LANES)
        out.append(packed[r:r + nr].reshape(-1)[:n].reshape(shp))
        r += nr
    return out


def kernel(x, norm_mix_g, w_in, b_f, gmlp_ln_g, gmlp_ln_b, w_s, b_s, attn_out_g, gmlp_out_g, w_out, norm_ffn_g, w_ff1, w_ff2, norm_final_g, loss_target, m_norm_mix_g, m_w_in, m_b_f, m_gmlp_ln_g, m_gmlp_ln_b, m_w_s, m_b_s, m_attn_out_g, m_gmlp_out_g, m_w_out, m_norm_ffn_g, m_w_ff1, m_w_ff2, m_norm_final_g, v_norm_mix_g, v_w_in, v_b_f, v_gmlp_ln_g, v_gmlp_ln_b, v_w_s, v_b_s, v_attn_out_g, v_gmlp_out_g, v_w_out, v_norm_ffn_g, v_w_ff1, v_w_ff2, v_norm_final_g):
    S, D = x.shape[1], x.shape[2]
    H = b_f.shape[1]
    DA = H * HEAD_DIM
    DG = gmlp_ln_g.shape[1]
    DQKV = 3 * DA
    DMAIN = DQKV + 2 * DG
    DIN = DMAIN + H
    DFF = w_ff1.shape[2] * N_DEV
    w_in_cols = w_in.shape[2]
    assert DIN == w_in_cols * N_DEV and DA == DG and D == DA + DG

    T_ATT = min(T_ATT_MAX, S)
    TR = min(TR_MAX, S)

    x0 = x[0]
    tgt = loss_target[0]
    g_final = norm_final_g.reshape(1, D)

    flat_rows = D * w_in_cols // LANES
    w_in_all = _all_gather("ag_w_in", w_in[0].astype(BF16).reshape(flat_rows, LANES))
    w_in_full = w_in_all.reshape(N_DEV, D, w_in_cols).transpose(1, 0, 2).reshape(D, DIN)
    w_main = jnp.concatenate([w_in_full[:, :DQKV], w_in_full[:, DQKV + H:]], axis=1)
    w_f = jnp.pad(w_in_full[:, DQKV:DQKV + H], ((0, 0), (0, LANES - H)))
    FB = DFF // N_DEV
    x_pos, y_pos, c_pos = _me()
    c_idx = jnp.reshape(c_pos, (1,)).astype(jnp.int32)
    p_idx = jnp.reshape(2 * x_pos + y_pos, (1,)).astype(jnp.int32)

    (h,), _ = _row_call("rms_mix", lambda xb, g: ((_rms_fwd(xb, g),), ()), [x0], [norm_mix_g], [(D, BF16)], [], TR)
    (zm,), ((w_out_part,),) = _mm_nn("in_proj", h, w_main, [BF16], 1024, 1024, 512,
                                     jobs=[_job_gather_chips(w_out[0].astype(BF16))])
    (zf,) = _mm_nn("in_proj_f", h, w_f, [F32], 1024, LANES, 2048)
    bf_pad = jnp.pad(b_f, ((0, 0), (0, LANES - H)))
    f_row = _fgate_fwd(zf, bf_pad)
    f_col3 = f_row.reshape(H, S, 1)
    f_row3 = f_row.reshape(H, 1, S)
    (attn, lse_col3), ((w_out_all,), (w_ff1_part,), (w_ff2_part,)) = _attn_fwd(
        zm, f_col3, f_row3, T_ATT, jobs=[_job_gather_sibling(w_out_part), _job_gather_chips(w_ff1[0].astype(BF16)),
                                         _job_gather_chips(w_ff2[0].astype(BF16))])
    w_out_full = w_out_all.reshape(D, D)
    bs_col = b_s[0].reshape(H, CHUNK, 1)
    gm = _gmlp_fwd(zm, gmlp_ln_g, gmlp_ln_b, w_s[0], bs_col, TR)

    def merge_fn(a, g, ga, gg):
        return (jnp.concatenate([_rms_fwd(a, ga), _rms_fwd(g, gg)], axis=1),), ()
    (merged,), _ = _row_call("rms_merge", merge_fn, [attn, gm], [attn_out_g, gmlp_out_g], [(D, BF16)], [], TR)

    (x1,), ((w_ff1_all,), (w_ff2_all,)) = _mm_nn(
        "out_proj", merged, w_out_full, [F32], 1024, 1024, 512, extras=[x0], epilogue=lambda acc, r: (acc + r,),
        jobs=[_job_gather_sibling(w_ff1_part), _job_gather_sibling(w_ff2_part)])
    w_ff2_full = w_ff2_all.reshape(DFF, D)
    (h2,), _ = _row_call("rms_ffn", lambda xb, g: ((_rms_fwd(xb, g),), ()), [x1], [norm_ffn_g], [(D, BF16)], [], TR)

    tm, tn, tk = min(1024, S), min(1024, FB), min(512, D)
    o_spec = pl.BlockSpec((tm, tn), lambda i, j, k: (i, j))

    def relu_sq(acc):
        a = jnp.maximum(acc, 0.0)
        return a, a * a
    nj = FB // tn
    a_act, a_sq = _mm(
        "ff1", (S // tm, DFF // tn, D // tk), h2, pl.BlockSpec((tm, tk), lambda i, j, k: (i, k)),
        w_ff1_all, pl.BlockSpec((None, tk, tn), lambda i, j, k: (j // nj, k, j % nj)), NN, (tm, tn),
        [jax.ShapeDtypeStruct((S, DFF), BF16)] * 2, [o_spec] * 2, epilogue=relu_sq)
    (x2,) = _mm_nn("ff2", a_sq, w_ff2_full, [F32], 1024, 1024, 512, extras=[x1], epilogue=lambda acc, r: (acc + r,))

    def head_fn(xb, t, g):
        rstd = lax.rsqrt(jnp.mean(xb * xb, axis=-1, keepdims=True) + EPS)
        xhat = xb * rstd
        err = xhat * g - t
        loss = 0.5 * jnp.sum(jnp.mean(err * err, axis=-1, keepdims=True), axis=0, keepdims=True)
        dy = err * (1.0 / D)
        dg = jnp.sum(dy * xhat, axis=0, keepdims=True)
        dxhat = dy * g
        dx = rstd * (dxhat - xhat * jnp.mean(dxhat * xhat, axis=-1, keepdims=True))
        return (dx, dx), (dg, jnp.broadcast_to(loss, (1, LANES)))
    (dx2, dx2_b), (dg_final, loss_part) = _row_call(
        "loss_head", head_fn, [x2, tgt], [g_final], [(D, F32), (D, BF16)], [D, LANES], TR)

    (da,) = _mm_nt("ff2_dx", dx2_b, w_ff2_full, [BF16], 1024, 1024, 512, extras=[a_act],
                   epilogue=lambda acc, a: (2.0 * a.astype(F32) * acc,))
    dw_ff2, dw_ff2_b = _mm_tn("ff2_dw", a_sq, dx2_b, [F32, BF16], 512, 2048, 512)
    tm2, tk2 = min(512, D), min(512, S)
    dw1_spec = pl.BlockSpec((None, tm2, FB), lambda i, j, k: (j, i, 0))
    (dw_ff1, dw_ff1_b), ((r1_ff2,),) = _mm(
        "ff1_dw", (D // tm2, DFF // FB, S // tk2), h2, pl.BlockSpec((tk2, tm2), lambda i, j, k: (k, i)),
        da, pl.BlockSpec((tk2, FB), lambda i, j, k: (k, j)), TN, (tm2, FB),
        [jax.ShapeDtypeStruct((N_DEV, D, FB), F32), jax.ShapeDtypeStruct((N_DEV, D, FB), BF16)], [dw1_spec] * 2,
        epilogue=lambda acc: (acc, acc), jobs=[_job_scatter_sibling(dw_ff2_b.reshape(4, 2, FB, D))])
    h_ff2, hb_ff2 = _rs_add1("rs_add1_w_ff2", dw_ff2.reshape(4, 2, FB, D), r1_ff2, c_idx)
    tkb = min(512, FB)
    nkb = FB // tkb
    tnb = min(1024, D)
    (dh2,), ((r2_ff2,), (r1_ff1,)) = _mm(
        "ff1_dx", (S // tm, D // tnb, DFF // tkb), da, pl.BlockSpec((tm, tkb), lambda i, j, k: (i, k)),
        w_ff1_all, pl.BlockSpec((None, tnb, tkb), lambda i, j, k: (k // nkb, j, k % nkb)), NT, (tm, tnb),
        [jax.ShapeDtypeStruct((S, D), F32)], [pl.BlockSpec((tm, tnb), lambda i, j, k: (i, j))],
        jobs=[_job_scatter_chips(hb_ff2), _job_scatter_sibling(dw_ff1_b.reshape(4, 2, D, FB))])
    g_w_ff2 = _rs_add2("rs_add2_w_ff2", h_ff2, r2_ff2, p_idx)
    h_ff1, hb_ff1 = _rs_add1("rs_add1_w_ff1", dw_ff1.reshape(4, 2, D, FB), r1_ff1, c_idx)

    def ffn_bwd_fn(dh, xb, dres, g):
        dx, dg = _rms_bwd(dh, xb, g)
        dx = dx + dres
        return (dx, dx), (dg,)
    (dx1, dx1_b), (dg_ffn,) = _row_call("rms_ffn_bwd", ffn_bwd_fn, [dh2, x1, dx2], [norm_ffn_g],
                                        [(D, F32), (D, BF16)], [D], TR)

    (dmerged,) = _mm_nt("out_proj_dx", dx1_b, w_out_full, [F32], 1024, 1024, 512)
    dw_out, dw_out_b = _mm_tn("out_proj_dw", merged, dx1_b, [F32, BF16], 512, 2048, 512)

    def merge_bwd_fn(dm, a, g, ga, gg):
        da_, dga = _rms_bwd(dm[:, :DA], a, ga)
        dg_, dgg = _rms_bwd(dm[:, DA:], g, gg)
        return (da_, dg_), (dga, dgg)
    (dattn, dgm), (dg_attn, dg_gmlp) = _row_call(
        "rms_merge_bwd", merge_bwd_fn, [dmerged, attn, gm], [attn_out_g, gmlp_out_g], [(DA, F32), (DG, F32)], [DA, DG], TR)

    w_st = jnp.swapaxes(w_s[0], 1, 2)
    dzu, dzv, dw_s, dbs_col, dln_g, dln_b = _gmlp_bwd(dgm, zm, gmlp_ln_g, gmlp_ln_b, w_s[0], w_st, bs_col, TR)

    delta_row = _attn_delta(dattn, attn, TR)
    lse_row3 = lse_col3.reshape(H, 1, S)
    (dq, ds_rowsum), ((r2_ff1,), (r1_out,)) = _attn_bwd_dq(
        zm, dattn, f_col3, f_row3, lse_col3, delta_row.reshape(H, S, 1), T_ATT,
        jobs=[_job_scatter_chips(hb_ff1), _job_scatter_sibling(dw_out_b.reshape(4, 2, D // N_DEV, D))])
    g_w_ff1 = _rs_add2("rs_add2_w_ff1", h_ff1, r2_ff1, p_idx)
    h_out, hb_out = _rs_add1("rs_add1_w_out", dw_out.reshape(4, 2, D // N_DEV, D), r1_out, c_idx)
    (dk, dv, df_col3), ((r2_out,),) = _attn_bwd_dkv(
        zm, dattn, f_col3, f_row3, lse_row3, delta_row.reshape(H, 1, S), ds_rowsum.reshape(H, 1, S), T_ATT,
        jobs=[_job_scatter_chips(hb_out)])
    g_w_out = _rs_add2("rs_add2_w_out", h_out, r2_out, p_idx)
    dzf, dbf = _fgate_bwd(df_col3.reshape(H, S), zf, bf_pad)

    dz_main = jnp.concatenate([dq, dk, dv, dzu, dzv], axis=1)
    (dw_main,) = _mm_tn("in_proj_dw", h, dz_main, [F32], 512, 1024, 512)
    (dw_f,) = _mm_tn("in_proj_f_dw", h, dzf, [F32], 512, LANES, 512)
    dw_in_full = jnp.concatenate([dw_main[:, :DQKV], dw_f[:, :H], dw_main[:, DQKV:]], axis=1)
    g4_in = dw_in_full.reshape(D, N_DEV, w_in_cols).transpose(1, 0, 2).reshape(4, 2, flat_rows, LANES)
    (dh_f,), ((r1_in,),) = _mm_nt("in_proj_f_dx", dzf, w_f, [F32], 1024, 1024, LANES,
                                  jobs=[_job_scatter_sibling(g4_in.astype(BF16))])
    h_in, hb_in = _rs_add1("rs_add1_w_in", g4_in, r1_in, c_idx)
    (dh,), ((r2_in,),) = _mm_nt("in_proj_dx", dz_main, w_main, [F32], 1024, 1024, 512, extras=[dh_f],
                                epilogue=lambda acc, r: (acc + r,), jobs=[_job_scatter_chips(hb_in)])
    g_w_in = _rs_add2("rs_add2_w_in", h_in, r2_in, p_idx).reshape(D, w_in_cols)

    def mix_bwd_fn(dhb, xb, dres, g):
        dx, dg = _rms_bwd(dhb, xb, g)
        return (dx + dres,), (dg,)
    (grad_x,), (dg_mix,) = _row_call("rms_mix_bwd", mix_bwd_fn, [dh, x0, dx1], [norm_mix_g], [(D, F32)], [D], TR)

    small_shapes = [norm_mix_g.shape, b_f.shape, gmlp_ln_g.shape, gmlp_ln_b.shape, w_s.shape, b_s.shape,
                    attn_out_g.shape, gmlp_out_g.shape, norm_ffn_g.shape, norm_final_g.shape]
    small_parts = [dg_mix, dbf[:, :H], dln_g, dln_b, dw_s, dbs_col, dg_attn, dg_gmlp, dg_ffn, dg_final]
    g_small = _sum8("small_sum", _all_gather("ag_small", _pack(small_parts)))
    w_small = _pack([norm_mix_g, b_f, gmlp_ln_g, gmlp_ln_b, w_s, b_s, attn_out_g, gmlp_out_g, norm_ffn_g, norm_final_g])
    m_small = _pack([m_norm_mix_g, m_b_f, m_gmlp_ln_g, m_gmlp_ln_b, m_w_s, m_b_s, m_attn_out_g, m_gmlp_out_g,
                     m_norm_ffn_g, m_norm_final_g])
    v_small = _pack([v_norm_mix_g, v_b_f, v_gmlp_ln_g, v_gmlp_ln_b, v_w_s, v_b_s, v_attn_out_g, v_gmlp_out_g,
                     v_norm_ffn_g, v_norm_final_g])
    (d_small, nm_small, nv_small), _ = _adamw("adamw_small", w_small, g_small, m_small, v_small)
    gs = _unpack(g_small, small_shapes)
    ds = _unpack(d_small, small_shapes)
    nms = _unpack(nm_small, small_shapes)
    nvs = _unpack(nv_small, small_shapes)

    big = {}
    for nm, w, g, m, v in (("w_in", w_in, g_w_in, m_w_in, v_w_in), ("w_out", w_out, g_w_out, m_w_out, v_w_out),
                           ("w_ff1", w_ff1, g_w_ff1, m_w_ff1, v_w_ff1), ("w_ff2", w_ff2, g_w_ff2, m_w_ff2, v_w_ff2)):
        (d_, m_, v_), _ = _adamw("adamw_" + nm, w[0], g, m[0], v[0])
        big[nm] = (g[None], d_[None], m_[None], v_[None])

    loss = lax.psum(loss_part[0, 0], ("x", "y", "c"))

    def leaves(n):
        sm = (gs, ds, nms, nvs)[n]
        return [sm[0], big["w_in"][n], sm[1], sm[2], sm[3], sm[4], sm[5], sm[6], sm[7], big["w_out"][n], sm[8],
                big["w_ff1"][n], big["w_ff2"][n], sm[9]]

    return (loss, grad_x[None], *leaves(0), *leaves(1), *leaves(2), *leaves(3))
```

```python
import functools
import math

import jax
import jax.numpy as jnp
from jax import lax
from jax.experimental import pallas as pl
from jax.experimental.pallas import tpu as pltpu

F32 = jnp.float32
BF16 = jnp.bfloat16
MESH = pl.DeviceIdType.MESH

HEAD_DIM = 128
CHUNK = 128
EPS = 1e-6
LANES = 128
N_DEV = 8

ADAM_LR = 0.001
ADAM_B1 = 0.9
ADAM_B2 = 0.999
ADAM_EPS = 1e-08
ADAM_WD = 0.01
ADAM_STEP = 10

VMEM_LIMIT_BYTES = 56 * 1024 * 1024
T_ATT_MAX = 512
TR_MAX = 256

NN = ((1,), (0,))
NT = ((1,), (1,))
TN = ((0,), (0,))


def _params(sem=None):
    return pltpu.CompilerParams(dimension_semantics=sem, vmem_limit_bytes=VMEM_LIMIT_BYTES)


def _dot(a, b, contract=NN):
    return lax.dot_general(a, b, (contract, ((), ())), preferred_element_type=F32)


def _dot3(x, t):
    x1 = x.astype(BF16)
    r1 = x - x1.astype(F32)
    x2 = r1.astype(BF16)
    x3 = (r1 - x2.astype(F32)).astype(BF16)
    return _dot(x1, t) + _dot(x2, t) + _dot(x3, t)


def _iota2(shape, dim):
    return lax.broadcasted_iota(jnp.int32, shape, dim)


def _row_call(name, fn, row_ins, bcast_ins, row_outs, acc_outs, tr):
    S = row_ins[0].shape[0]
    assert S % tr == 0
    n_ri, n_bi, n_ro, n_ao = len(row_ins), len(bcast_ins), len(row_outs), len(acc_outs)

    def body(*refs):
        ins = [r[...] for r in refs[:n_ri + n_bi]]
        ro_refs = refs[n_ri + n_bi:n_ri + n_bi + n_ro]
        ao_refs = refs[n_ri + n_bi + n_ro:]
        ro, ao = fn(*ins)
        for r, v in zip(ro_refs, ro):
            r[...] = v.astype(r.dtype)
        if n_ao:
            @pl.when(pl.program_id(0) == 0)
            def _():
                for r in ao_refs:
                    r[...] = jnp.zeros_like(r)
            for r, v in zip(ao_refs, ao):
                r[...] += v

    in_specs = [pl.BlockSpec((tr, a.shape[1]), lambda i: (i, 0)) for a in row_ins]
    in_specs += [pl.BlockSpec(a.shape, lambda i: (0, 0)) for a in bcast_ins]
    out_specs = [pl.BlockSpec((tr, d), lambda i: (i, 0)) for d, _ in row_outs]
    out_specs += [pl.BlockSpec((1, d), lambda i: (0, 0)) for d in acc_outs]
    out_shape = [jax.ShapeDtypeStruct((S, d), dt) for d, dt in row_outs]
    out_shape += [jax.ShapeDtypeStruct((1, d), F32) for d in acc_outs]
    outs = pl.pallas_call(
        body, name=name, grid=(S // tr,), in_specs=in_specs, out_specs=out_specs, out_shape=out_shape,
        compiler_params=_params(("arbitrary",) if n_ao else ("parallel",)),
    )(*row_ins, *bcast_ins)
    return outs[:n_ro], outs[n_ro:]


def _rms_fwd(x, g):
    rstd = lax.rsqrt(jnp.mean(x * x, axis=-1, keepdims=True) + EPS)
    return x * rstd * g


def _rms_bwd(dy, x, g):
    rstd = lax.rsqrt(jnp.mean(x * x, axis=-1, keepdims=True) + EPS)
    xhat = x * rstd
    dg = jnp.sum(dy * xhat, axis=0, keepdims=True)
    dxhat = dy * g
    dx = rstd * (dxhat - xhat * jnp.mean(dxhat * xhat, axis=-1, keepdims=True))
    return dx, dg


_GELU_C = math.sqrt(2.0 / math.pi)


def _gelu(x):
    return 0.5 * x * (1.0 + jnp.tanh(_GELU_C * (x + 0.044715 * (x * x * x))))


def _gelu_grad(x):
    t = jnp.tanh(_GELU_C * (x + 0.044715 * (x * x * x)))
    return 0.5 * (1.0 + t) + 0.5 * x * (1.0 - t * t) * (_GELU_C * (1.0 + 3.0 * 0.044715 * (x * x)))


def _me():
    return lax.axis_index("x"), lax.axis_index("y"), lax.axis_index("c")


def _other_chips(x, y):
    return [(1 - x, y), (x, 1 - y), (1 - x, 1 - y)]


_ANY = pl.BlockSpec(memory_space=pl.ANY)


class _Job:
    def __init__(self, ins, outs, n_sems, make, aliases=None):
        self.ins, self.outs, self.n_sems, self.make, self.aliases = ins, outs, n_sems, make, aliases or {}


def _job_gather_chips(blk):
    R, C = blk.shape

    def make(ins, outs, send_sems, recv_sems, base):
        (x_ref,), (out_ref,) = ins, outs
        x, y, c = _me()
        mine = 4 * x + 2 * y + c
        targets = [(x, y, 1 - c)] + [(cx, cy, c) for cx, cy in _other_chips(x, y)]

        def copy(k, slab, to):
            return pltpu.make_async_remote_copy(
                src_ref=x_ref, dst_ref=out_ref.at[slab], send_sem=send_sems.at[base + k],
                recv_sem=recv_sems.at[base + k], device_id=to, device_id_type=MESH)

        starts = [copy(k, mine, to) for k, to in enumerate(targets)]
        arrivals = [copy(k, 4 * tx + 2 * ty + tc, (tx, ty, tc)) for k, (tx, ty, tc) in enumerate(targets)]
        local = [pltpu.make_async_copy(x_ref, out_ref.at[mine], send_sems.at[base + 4])]
        return starts, arrivals, local

    return _Job([blk], [jax.ShapeDtypeStruct((N_DEV, R, C), blk.dtype)], 5, make)


def _job_gather_sibling(part):
    def make(ins, outs, send_sems, recv_sems, base):
        (out_ref,) = outs
        x, y, c = _me()

        def copy(k, slab):
            return pltpu.make_async_remote_copy(
                src_ref=out_ref.at[slab], dst_ref=out_ref.at[slab], send_sem=send_sems.at[base + k],
                recv_sem=recv_sems.at[base + k], device_id=(x, y, 1 - c), device_id_type=MESH)

        chips = _other_chips(x, y)
        starts = [copy(k, 4 * cx + 2 * cy + c) for k, (cx, cy) in enumerate(chips)]
        arrivals = [copy(k, 4 * cx + 2 * cy + (1 - c)) for k, (cx, cy) in enumerate(chips)]
        return starts, arrivals, []

    return _Job([part], [jax.ShapeDtypeStruct(part.shape, part.dtype)], 3, make, aliases={0: 0})


def _job_scatter_sibling(gb):
    _, _, R, C = gb.shape

    def make(ins, outs, send_sems, recv_sems, base):
        (g_ref,), (recv_ref,) = ins, outs
        x, y, c = _me()
        copies = [pltpu.make_async_remote_copy(
            src_ref=g_ref.at[p, 1 - c], dst_ref=recv_ref.at[p], send_sem=send_sems.at[base + p],
            recv_sem=recv_sems.at[base + p], device_id=(x, y, 1 - c), device_id_type=MESH) for p in range(4)]
        return copies, copies, []

    return _Job([gb], [jax.ShapeDtypeStruct((4, R, C), gb.dtype)], 4, make)


def _job_scatter_chips(hb):
    _, R, C = hb.shape

    def make(ins, outs, send_sems, recv_sems, base):
        (h_ref,), (recv_ref,) = ins, outs
        x, y, c = _me()
        copies = [pltpu.make_async_remote_copy(
            src_ref=h_ref.at[2 * cx + cy], dst_ref=recv_ref.at[n], send_sem=send_sems.at[base + n],
            recv_sem=recv_sems.at[base + n], device_id=(cx, cy, c), device_id_type=MESH)
            for n, (cx, cy) in enumerate(_other_chips(x, y))]
        return copies, copies, []

    return _Job([hb], [jax.ShapeDtypeStruct((3, R, C), hb.dtype)], 3, make)


def _carry_call(body, *, name, grid, in_specs, out_specs, out_shape, scratch_shapes, semantics, args, jobs=()):
    jobs = list(jobs)
    n_in, n_out, n_scr = len(in_specs), len(out_specs), len(scratch_shapes)
    j_ins = [a for j in jobs for a in j.ins]
    j_outs = [o for j in jobs for o in j.outs]
    n_sems = sum(j.n_sems for j in jobs)
    aliases = {}
    i0, o0 = n_in, n_out
    for j in jobs:
        for a, b in j.aliases.items():
            aliases[i0 + a] = o0 + b
        i0 += len(j.ins)
        o0 += len(j.outs)

    def full_body(*refs):
        ins = refs[:n_in]
        jin = refs[n_in:n_in + len(j_ins)]
        outs = refs[n_in + len(j_ins):n_in + len(j_ins) + n_out]
        jout = refs[n_in + len(j_ins) + n_out:n_in + len(j_ins) + n_out + len(j_outs)]
        scr = refs[n_in + len(j_ins) + n_out + len(j_outs):]
        if jobs:
            send_sems, recv_sems = scr[n_scr], scr[n_scr + 1]
            starts, arrivals, local = [], [], []
            base = i0 = o0 = 0
            for j in jobs:
                s, a, l = j.make(jin[i0:i0 + len(j.ins)], jout[o0:o0 + len(j.outs)], send_sems, recv_sems, base)
                starts += s
                arrivals += a
                local += l
                base += j.n_sems
                i0 += len(j.ins)
                o0 += len(j.outs)
            pids = [pl.program_id(d) for d in range(len(grid))]
            first = functools.reduce(jnp.logical_and, [p == 0 for p in pids])
            last = functools.reduce(jnp.logical_and, [p == n - 1 for p, n in zip(pids, grid)])

            @pl.when(first)
            def _():
                for cp in local + starts:
                    cp.start()

        body(*ins, *outs, *scr[:n_scr])

        if jobs:
            @pl.when(last)
            def _():
                for cp in arrivals:
                    cp.wait_recv()
                for cp in starts:
                    cp.wait_send()
                for cp in local:
                    cp.wait()

    sems = [pltpu.SemaphoreType.DMA((n_sems,)), pltpu.SemaphoreType.DMA((n_sems,))] if jobs else []
    res = pl.pallas_call(
        full_body, name=name, grid=grid,
        in_specs=list(in_specs) + [_ANY] * len(j_ins),
        out_specs=list(out_specs) + [_ANY] * len(j_outs),
        out_shape=list(out_shape) + j_outs,
        scratch_shapes=list(scratch_shapes) + sems,
        input_output_aliases=aliases,
        compiler_params=_params(("arbitrary",) * len(grid) if jobs else semantics),
    )(*args, *j_ins)
    body_res, job_res = res[:n_out], res[n_out:]
    per_job = []
    for j in jobs:
        per_job.append(job_res[:len(j.outs)])
        job_res = job_res[len(j.outs):]
    return body_res, per_job


def _mm(name, grid, a, a_spec, b, b_spec, contract, acc_shape, out_shape, out_specs, extras=(), epilogue=None, jobs=()):
    nk = grid[2]
    n_e = len(extras)
    n_o = len(out_shape)
    if epilogue is None:
        epilogue = lambda acc: (acc,)

    def body(a_ref, b_ref, *rest):
        e_refs = rest[:n_e]
        o_refs = rest[n_e:n_e + n_o]

        def finish(total):
            res = epilogue(total, *[r[...] for r in e_refs])
            for o, r in zip(o_refs, res):
                o[...] = r.astype(o.dtype)

        if nk == 1:
            finish(_dot(a_ref[...], b_ref[...], contract))
            return
        acc = rest[n_e + n_o]
        k = pl.program_id(2)

        @pl.when(k == 0)
        def _():
            acc[...] = _dot(a_ref[...], b_ref[...], contract)

        @pl.when(jnp.logical_and(k > 0, k < nk - 1))
        def _():
            acc[...] += _dot(a_ref[...], b_ref[...], contract)

        @pl.when(k == nk - 1)
        def _():
            finish(acc[...] + _dot(a_ref[...], b_ref[...], contract))

    outs, job_res = _carry_call(
        body, name=name, grid=grid, in_specs=[a_spec, b_spec] + [s for _, s in extras],
        out_specs=list(out_specs), out_shape=list(out_shape),
        scratch_shapes=[pltpu.VMEM(acc_shape, F32)] if nk > 1 else [],
        semantics=("parallel", "parallel", "arbitrary"), args=[a, b] + [e for e, _ in extras], jobs=jobs)
    return (outs, job_res) if jobs else outs


def _mm_nn(name, a, b, out_dtypes, tm, tn, tk, extras=(), epilogue=None, jobs=()):
    M, K = a.shape
    N = b.shape[1]
    tm, tn, tk = min(tm, M), min(tn, N), min(tk, K)
    o_spec = pl.BlockSpec((tm, tn), lambda i, j, k: (i, j))
    return _mm(name, (M // tm, N // tn, K // tk),
               a, pl.BlockSpec((tm, tk), lambda i, j, k: (i, k)),
               b, pl.BlockSpec((tk, tn), lambda i, j, k: (k, j)), NN, (tm, tn),
               [jax.ShapeDtypeStruct((M, N), dt) for dt in out_dtypes], [o_spec] * len(out_dtypes),
               [(e, o_spec) for e in extras], epilogue, jobs)


def _mm_nt(name, a, b, out_dtypes, tm, tn, tk, extras=(), epilogue=None, jobs=()):
    M, K = a.shape
    N = b.shape[0]
    tm, tn, tk = min(tm, M), min(tn, N), min(tk, K)
    o_spec = pl.BlockSpec((tm, tn), lambda i, j, k: (i, j))
    return _mm(name, (M // tm, N // tn, K // tk),
               a, pl.BlockSpec((tm, tk), lambda i, j, k: (i, k)),
               b, pl.BlockSpec((tn, tk), lambda i, j, k: (j, k)), NT, (tm, tn),
               [jax.ShapeDtypeStruct((M, N), dt) for dt in out_dtypes], [o_spec] * len(out_dtypes),
               [(e, o_spec) for e in extras], epilogue, jobs)


def _mm_tn(name, a, b, out_dtypes, tm, tn, tk, jobs=()):
    K, M = a.shape
    N = b.shape[1]
    tm, tn, tk = min(tm, M), min(tn, N), min(tk, K)
    o_spec = pl.BlockSpec((tm, tn), lambda i, j, k: (i, j))
    return _mm(name, (M // tm, N // tn, K // tk),
               a, pl.BlockSpec((tk, tm), lambda i, j, k: (k, i)),
               b, pl.BlockSpec((tk, tn), lambda i, j, k: (k, j)), TN, (tm, tn),
               [jax.ShapeDtypeStruct((M, N), dt) for dt in out_dtypes], [o_spec] * len(out_dtypes),
               epilogue=lambda acc: (acc,) * len(out_dtypes), jobs=jobs)


def _fgate_fwd(zf, bf):
    S = zf.shape[0]
    nc = S // CHUNK

    def body(zf_ref, bf_ref, f_ref):
        upper = (_iota2((CHUNK, CHUNK), 0) <= _iota2((CHUNK, CHUNK), 1)).astype(BF16)
        carry = jnp.zeros((8, 1), F32)
        for c in range(nc):
            t = zf_ref[c * CHUNK:(c + 1) * CHUNK, :] + bf_ref[...]
            lf = jnp.minimum(t, 0.0) - jnp.log(1.0 + jnp.exp(-jnp.abs(t)))
            lf_rows = lf.T[0:8, :]
            f_ref[:, c * CHUNK:(c + 1) * CHUNK] = _dot3(lf_rows, upper) + carry
            carry = carry + jnp.sum(lf_rows, axis=-1, keepdims=True)

    return pl.pallas_call(
        body, name="fgate_fwd", out_shape=jax.ShapeDtypeStruct((8, S), F32),
        compiler_params=_params(),
    )(zf, bf)


def _fgate_bwd(df, zf, bf):
    S = zf.shape[0]
    nc = S // CHUNK

    def body(df_ref, zf_ref, bf_ref, dzf_ref, dbf_ref):
        lower = (_iota2((CHUNK, CHUNK), 0) >= _iota2((CHUNK, CHUNK), 1)).astype(BF16)
        carry = jnp.zeros((8, 1), F32)
        dbf = jnp.zeros((1, LANES), F32)
        for c in reversed(range(nc)):
            sl = slice(c * CHUNK, (c + 1) * CHUNK)
            df = df_ref[:, sl]
            r = _dot3(df, lower) + carry
            carry = carry + jnp.sum(df, axis=-1, keepdims=True)
            r_cols = jnp.concatenate([r, jnp.zeros((CHUNK - 8, CHUNK), F32)], axis=0).T
            t = zf_ref[sl, :] + bf_ref[...]
            dz = r_cols * (1.0 / (1.0 + jnp.exp(t)))
            dzf_ref[sl, :] = dz.astype(BF16)
            dbf = dbf + jnp.sum(dz, axis=0, keepdims=True)
        dbf_ref[...] = dbf

    return pl.pallas_call(
        body, name="fgate_bwd",
        out_shape=[jax.ShapeDtypeStruct((S, LANES), BF16), jax.ShapeDtypeStruct((1, LANES), F32)],
        compiler_params=_params(),
    )(df, zf, bf)


_NEG = -1e30


def _attn_fwd(zm, fcol, frow, T, jobs=()):
    S = zm.shape[0]
    H = fcol.shape[0]
    nb = S // T
    scale = 1.0 / math.sqrt(HEAD_DIM)

    def body(q_ref, k_ref, v_ref, fq_ref, fk_ref, o_ref, lse_ref, m_s, l_s, acc_s):
        i = pl.program_id(1)
        j = pl.program_id(2)

        @pl.when(j == 0)
        def _():
            m_s[...] = jnp.full_like(m_s, _NEG)
            l_s[...] = jnp.zeros_like(l_s)
            acc_s[...] = jnp.zeros_like(acc_s)

        @pl.when(j <= i)
        def _():
            s = _dot(q_ref[...], k_ref[...], NT) * scale + (fq_ref[...] - fk_ref[...])
            keep = (_iota2((T, T), 1) + j * T) <= (_iota2((T, T), 0) + i * T)
            s = jnp.where(keep, s, _NEG)
            m_new = jnp.maximum(m_s[...], jnp.max(s, axis=-1, keepdims=True))
            alpha = jnp.exp(m_s[...] - m_new)
            p = jnp.exp(s - m_new)
            l_s[...] = alpha * l_s[...] + jnp.sum(p, axis=-1, keepdims=True)
            acc_s[...] = alpha * acc_s[...] + _dot(p.astype(BF16), v_ref[...])
            m_s[...] = m_new

        @pl.when(j == nb - 1)
        def _():
            o_ref[...] = acc_s[...] / l_s[...]
            lse_ref[...] = m_s[...] + jnp.log(l_s[...])

    nh = H
    return _carry_call(
        body, name="attn_fwd", grid=(H, nb, nb), jobs=jobs, args=[zm, zm, zm, fcol, frow],
        semantics=("parallel", "parallel", "arbitrary"),
        in_specs=[
            pl.BlockSpec((T, HEAD_DIM), lambda h, i, j: (i, h)),
            pl.BlockSpec((T, HEAD_DIM), lambda h, i, j: (jnp.minimum(j, i), nh + h)),
            pl.BlockSpec((T, HEAD_DIM), lambda h, i, j: (jnp.minimum(j, i), 2 * nh + h)),
            pl.BlockSpec((None, T, 1), lambda h, i, j: (h, i, 0)),
            pl.BlockSpec((None, 1, T), lambda h, i, j: (h, 0, jnp.minimum(j, i))),
        ],
        out_specs=[
            pl.BlockSpec((T, HEAD_DIM), lambda h, i, j: (i, h)),
            pl.BlockSpec((None, T, 1), lambda h, i, j: (h, i, 0)),
        ],
        out_shape=[jax.ShapeDtypeStruct((S, H * HEAD_DIM), F32), jax.ShapeDtypeStruct((H, S, 1), F32)],
        scratch_shapes=[pltpu.VMEM((T, 1), F32), pltpu.VMEM((T, 1), F32), pltpu.VMEM((T, HEAD_DIM), F32)],
    )


def _attn_delta(dattn, attn, tr):
    S, DA = attn.shape
    H = DA // HEAD_DIM

    def body(do_ref, o_ref, out_ref):
        lo = _iota2((DA, LANES), 1) * HEAD_DIM
        sel = ((_iota2((DA, LANES), 0) >= lo) & (_iota2((DA, LANES), 0) < lo + HEAD_DIM)).astype(BF16)
        d = _dot3(do_ref[...] * o_ref[...], sel)
        for c in range(tr // CHUNK):
            out_ref[:, c * CHUNK:(c + 1) * CHUNK] = d[c * CHUNK:(c + 1) * CHUNK, :].T[0:H, :]

    return pl.pallas_call(
        body, name="attn_delta", grid=(S // tr,),
        in_specs=[pl.BlockSpec((tr, DA), lambda i: (i, 0))] * 2,
        out_specs=pl.BlockSpec((H, tr), lambda i: (0, i)),
        out_shape=jax.ShapeDtypeStruct((H, S), F32),
        compiler_params=_params(("parallel",)),
    )(dattn, attn)


def _attn_bwd_dq(zm, dattn, fcol, frow, lse_col, delta_col, T, jobs=()):
    S = zm.shape[0]
    H = fcol.shape[0]
    nb = S // T
    scale = 1.0 / math.sqrt(HEAD_DIM)

    def body(q_ref, k_ref, v_ref, do_ref, fq_ref, fk_ref, lse_ref, dl_ref, dq_ref, rs_ref, acc_s, rs_s):
        i = pl.program_id(1)
        j = pl.program_id(2)

        @pl.when(j == 0)
        def _():
            acc_s[...] = jnp.zeros_like(acc_s)
            rs_s[...] = jnp.zeros_like(rs_s)

        @pl.when(j <= i)
        def _():
            s = _dot(q_ref[...], k_ref[...], NT) * scale + (fq_ref[...] - fk_ref[...])
            keep = (_iota2((T, T), 1) + j * T) <= (_iota2((T, T), 0) + i * T)
            p = jnp.exp(jnp.where(keep, s - lse_ref[...], _NEG))
            dp = _dot(do_ref[...].astype(BF16), v_ref[...], NT)
            ds = p * (dp - dl_ref[...])
            acc_s[...] += _dot(ds.astype(BF16), k_ref[...])
            rs_s[...] += jnp.sum(ds, axis=-1, keepdims=True)

        @pl.when(j == nb - 1)
        def _():
            dq_ref[...] = (acc_s[...] * scale).astype(BF16)
            rs_ref[...] = rs_s[...]

    nh = H
    col = pl.BlockSpec((None, T, 1), lambda h, i, j: (h, i, 0))
    return _carry_call(
        body, name="attn_bwd_dq", grid=(H, nb, nb), jobs=jobs,
        args=[zm, zm, zm, dattn, fcol, frow, lse_col, delta_col], semantics=("parallel", "parallel", "arbitrary"),
        in_specs=[
            pl.BlockSpec((T, HEAD_DIM), lambda h, i, j: (i, h)),
            pl.BlockSpec((T, HEAD_DIM), lambda h, i, j: (jnp.minimum(j, i), nh + h)),
            pl.BlockSpec((T, HEAD_DIM), lambda h, i, j: (jnp.minimum(j, i), 2 * nh + h)),
            pl.BlockSpec((T, HEAD_DIM), lambda h, i, j: (i, h)),
            col,
            pl.BlockSpec((None, 1, T), lambda h, i, j: (h, 0, jnp.minimum(j, i))),
            col, col,
        ],
        out_specs=[pl.BlockSpec((T, HEAD_DIM), lambda h, i, j: (i, h)), col],
        out_shape=[jax.ShapeDtypeStruct((S, H * HEAD_DIM), BF16), jax.ShapeDtypeStruct((H, S, 1), F32)],
        scratch_shapes=[pltpu.VMEM((T, HEAD_DIM), F32), pltpu.VMEM((T, 1), F32)],
    )


def _attn_bwd_dkv(zm, dattn, fcol, frow, lse_row, delta_row, rowsum_row, T, jobs=()):
    S = zm.shape[0]
    H = fcol.shape[0]
    nb = S // T
    scale = 1.0 / math.sqrt(HEAD_DIM)

    def body(q_ref, k_ref, v_ref, do_ref, fk_ref, fq_ref, lse_ref, dl_ref, rs_ref,
             dk_ref, dv_ref, df_ref, dk_s, dv_s, df_s):
        j = pl.program_id(1)
        i = pl.program_id(2)

        @pl.when(i == 0)
        def _():
            dk_s[...] = jnp.zeros_like(dk_s)
            dv_s[...] = jnp.zeros_like(dv_s)
            df_s[...] = jnp.zeros_like(df_s)

        @pl.when(i >= j)
        def _():
            st = _dot(k_ref[...], q_ref[...], NT) * scale + (fq_ref[...] - fk_ref[...])
            keep = (_iota2((T, T), 0) + j * T) <= (_iota2((T, T), 1) + i * T)
            pt = jnp.exp(jnp.where(keep, st - lse_ref[...], _NEG))
            do = do_ref[...].astype(BF16)
            dpt = _dot(v_ref[...], do, NT)
            dst = pt * (dpt - (dl_ref[...] + rs_ref[...]))
            dv_s[...] += _dot(pt.astype(BF16), do)
            dk_s[...] += _dot(dst.astype(BF16), q_ref[...])
            df_s[...] -= jnp.sum(dst, axis=-1, keepdims=True)

        @pl.when(i == nb - 1)
        def _():
            dk_ref[...] = (dk_s[...] * scale).astype(BF16)
            dv_ref[...] = dv_s[...].astype(BF16)
            df_ref[...] = df_s[...]

    nh = H
    row = pl.BlockSpec((None, 1, T), lambda h, j, i: (h, 0, jnp.maximum(i, j)))
    kv_out = pl.BlockSpec((T, HEAD_DIM), lambda h, j, i: (j, h))
    return _carry_call(
        body, name="attn_bwd_dkv", grid=(H, nb, nb), jobs=jobs,
        args=[zm, zm, zm, dattn, fcol, frow, lse_row, delta_row, rowsum_row],
        semantics=("parallel", "parallel", "arbitrary"),
        in_specs=[
            pl.BlockSpec((T, HEAD_DIM), lambda h, j, i: (jnp.maximum(i, j), h)),
            pl.BlockSpec((T, HEAD_DIM), lambda h, j, i: (j, nh + h)),
            pl.BlockSpec((T, HEAD_DIM), lambda h, j, i: (j, 2 * nh + h)),
            pl.BlockSpec((T, HEAD_DIM), lambda h, j, i: (jnp.maximum(i, j), h)),
            pl.BlockSpec((None, T, 1), lambda h, j, i: (h, j, 0)),
            row, row, row, row,
        ],
        out_specs=[kv_out, kv_out, pl.BlockSpec((None, T, 1), lambda h, j, i: (h, j, 0))],
        out_shape=[jax.ShapeDtypeStruct((S, H * HEAD_DIM), BF16), jax.ShapeDtypeStruct((S, H * HEAD_DIM), BF16),
                   jax.ShapeDtypeStruct((H, S, 1), F32)],
        scratch_shapes=[pltpu.VMEM((T, HEAD_DIM), F32), pltpu.VMEM((T, HEAD_DIM), F32), pltpu.VMEM((T, 1), F32)],
    )


def _ln_stats(x):
    mu = jnp.mean(x, axis=-1, keepdims=True)
    xc = x - mu
    rstd = lax.rsqrt(jnp.mean(xc * xc, axis=-1, keepdims=True) + EPS)
    return xc * rstd, rstd


def _tril_mask():
    return _iota2((CHUNK, CHUNK), 0) >= _iota2((CHUNK, CHUNK), 1)


def _gmlp_fwd(zm, ln_g, ln_b, w_s, bs_col, tr):
    S = zm.shape[0]
    H = w_s.shape[0]
    DG = H * HEAD_DIM

    def body(zu_ref, zv_ref, g_ref, b_ref, w_ref, bs_ref, out_ref):
        u = _gelu(zu_ref[...].astype(F32))
        y, _ = _ln_stats(_gelu(zv_ref[...].astype(F32)))
        v = (y * g_ref[...] + b_ref[...]).astype(BF16)
        mask = _tril_mask()
        for h in range(H):
            wc = jnp.where(mask, w_ref[h], 0.0).astype(BF16)
            cs = slice(h * HEAD_DIM, (h + 1) * HEAD_DIM)
            for c in range(tr // CHUNK):
                rs = slice(c * CHUNK, (c + 1) * CHUNK)
                mix = _dot(wc, v[rs, cs]) + bs_ref[h]
                out_ref[rs, cs] = u[rs, cs] * mix

    full = lambda a: pl.BlockSpec(a.shape, lambda i: (0,) * a.ndim)
    return pl.pallas_call(
        body, name="gmlp_fwd", grid=(S // tr,),
        in_specs=[pl.BlockSpec((tr, DG), lambda i: (i, 3)), pl.BlockSpec((tr, DG), lambda i: (i, 4)),
                  full(ln_g), full(ln_b), full(w_s), full(bs_col)],
        out_specs=pl.BlockSpec((tr, DG), lambda i: (i, 0)),
        out_shape=jax.ShapeDtypeStruct((S, DG), F32),
        compiler_params=_params(("parallel",)),
    )(zm, zm, ln_g, ln_b, w_s, bs_col)


def _gmlp_bwd(dgm, zm, ln_g, ln_b, w_s, w_st, bs_col, tr):
    S = zm.shape[0]
    H = w_s.shape[0]
    DG = H * HEAD_DIM

    def body(dg_ref, zu_ref, zv_ref, g_ref, b_ref, w_ref, wt_ref, bs_ref,
             dzu_ref, dzv_ref, dw_ref, dbs_ref, dlg_ref, dlb_ref, dv_s):
        @pl.when(pl.program_id(0) == 0)
        def _():
            dw_ref[...] = jnp.zeros_like(dw_ref)
            dbs_ref[...] = jnp.zeros_like(dbs_ref)
            dlg_ref[...] = jnp.zeros_like(dlg_ref)
            dlb_ref[...] = jnp.zeros_like(dlb_ref)

        zu = zu_ref[...].astype(F32)
        zv = zv_ref[...].astype(F32)
        u = _gelu(zu)
        y, rstd = _ln_stats(_gelu(zv))
        v = (y * g_ref[...] + b_ref[...]).astype(BF16)
        dgm_blk = dg_ref[...]
        mask = _tril_mask()
        mask_t = _iota2((CHUNK, CHUNK), 0) <= _iota2((CHUNK, CHUNK), 1)
        for h in range(H):
            wc = jnp.where(mask, w_ref[h], 0.0).astype(BF16)
            wct = jnp.where(mask_t, wt_ref[h], 0.0).astype(BF16)
            cs = slice(h * HEAD_DIM, (h + 1) * HEAD_DIM)
            dw = jnp.zeros((CHUNK, CHUNK), F32)
            dbs = jnp.zeros((CHUNK, 1), F32)
            for c in range(tr // CHUNK):
                rs = slice(c * CHUNK, (c + 1) * CHUNK)
                vch = v[rs, cs]
                mix = _dot(wc, vch) + bs_ref[h]
                dg = dgm_blk[rs, cs]
                dzu_ref[rs, cs] = (dg * mix * _gelu_grad(zu[rs, cs])).astype(BF16)
                dmix = dg * u[rs, cs]
                dbs = dbs + jnp.sum(dmix, axis=-1, keepdims=True)
                dmix_b = dmix.astype(BF16)
                dw = dw + _dot(dmix_b, vch, NT)
                dv_s[rs, cs] = _dot(wct, dmix_b)
            dw_ref[h] += jnp.where(mask, dw, 0.0)
            dbs_ref[h] += dbs
        dv = dv_s[...]
        dlg_ref[...] += jnp.sum(dv * y, axis=0, keepdims=True)
        dlb_ref[...] += jnp.sum(dv, axis=0, keepdims=True)
        dy = dv * g_ref[...]
        dgv = rstd * (dy - jnp.mean(dy, axis=-1, keepdims=True) - y * jnp.mean(dy * y, axis=-1, keepdims=True))
        dzv_ref[...] = (dgv * _gelu_grad(zv)).astype(BF16)

    full = lambda a: pl.BlockSpec(a.shape, lambda i: (0,) * a.ndim)
    rows = pl.BlockSpec((tr, DG), lambda i: (i, 0))
    return pl.pallas_call(
        body, name="gmlp_bwd", grid=(S // tr,),
        in_specs=[rows, pl.BlockSpec((tr, DG), lambda i: (i, 3)), pl.BlockSpec((tr, DG), lambda i: (i, 4)),
                  full(ln_g), full(ln_b), full(w_s), full(w_st), full(bs_col)],
        out_specs=[rows, rows, full(w_s), full(bs_col), full(ln_g), full(ln_b)],
        out_shape=[jax.ShapeDtypeStruct((S, DG), BF16), jax.ShapeDtypeStruct((S, DG), BF16),
                   jax.ShapeDtypeStruct(w_s.shape, F32), jax.ShapeDtypeStruct(bs_col.shape, F32),
                   jax.ShapeDtypeStruct(ln_g.shape, F32), jax.ShapeDtypeStruct(ln_b.shape, F32)],
        scratch_shapes=[pltpu.VMEM((tr, DG), F32)],
        compiler_params=_params(("arbitrary",)),
    )(dgm, zm, zm, ln_g, ln_b, w_s, w_st, bs_col)


def _all_gather(name, blk):
    R, C = blk.shape

    def body(x_ref, out_ref, send_sems, recv_sems, local_sem):
        x, y, c = _me()
        me, sibling = (x, y, c), (x, y, 1 - c)
        chips = [(1 - x, y), (x, 1 - y), (1 - x, 1 - y)]

        def slab(px, py, pc):
            return out_ref.at[4 * px + 2 * py + pc]

        def copy(k, block, to, src=None):
            return pltpu.make_async_remote_copy(
                src_ref=slab(*block) if src is None else src, dst_ref=slab(*block),
                send_sem=send_sems.at[k], recv_sem=recv_sems.at[k], device_id=to, device_id_type=MESH)

        mine = pltpu.make_async_copy(x_ref, slab(*me), local_sem)
        mine.start()
        first = [copy(0, me, sibling, src=x_ref)]
        first += [copy(1 + n, me, (*chip, c), src=x_ref) for n, chip in enumerate(chips)]
        for cp in first:
            cp.start()
        passed = [copy(4 + n, (*chip, c), sibling) for n, chip in enumerate(chips)]
        for n, chip in enumerate(chips):
            copy(1 + n, (*chip, c), me).wait_recv()
            passed[n].start()
        copy(0, sibling, me).wait_recv()
        for n, chip in enumerate(chips):
            copy(4 + n, (*chip, 1 - c), me).wait_recv()
        for cp in first + passed:
            cp.wait_send()
        mine.wait()

    return pl.pallas_call(
        body, name=name, out_shape=jax.ShapeDtypeStruct((N_DEV, R, C), blk.dtype),
        in_specs=[_ANY], out_specs=_ANY,
        scratch_shapes=[pltpu.SemaphoreType.DMA((7,)), pltpu.SemaphoreType.DMA((7,)), pltpu.SemaphoreType.DMA(())],
    )(blk)


def _row_tile(R, C, itemsize=4, target_bytes=2 * 1024 * 1024):
    tr = R
    while tr % 2 == 0 and tr * C * itemsize > target_bytes and (tr // 2) % 16 == 0:
        tr //= 2
    return tr


def _rs_add1(name, g4, recv, c_idx):
    _, _, R, C = g4.shape
    tr = _row_tile(R, C)

    def body(c_ref, g_ref, r_ref, h_ref, hb_ref):
        h = g_ref[...] + r_ref[...].astype(F32)
        h_ref[...] = h
        hb_ref[...] = h.astype(BF16)

    blk = pl.BlockSpec((None, tr, C), lambda p, i, c_ref: (p, i, 0))
    return pl.pallas_call(
        body, name=name,
        grid_spec=pltpu.PrefetchScalarGridSpec(
            num_scalar_prefetch=1, grid=(4, R // tr),
            in_specs=[pl.BlockSpec((None, None, tr, C), lambda p, i, c_ref: (p, c_ref[0], i, 0)), blk],
            out_specs=[blk, blk]),
        out_shape=[jax.ShapeDtypeStruct((4, R, C), F32), jax.ShapeDtypeStruct((4, R, C), BF16)],
        compiler_params=_params(("parallel", "parallel")),
    )(c_idx, g4, recv)


def _rs_add2(name, h, recv, p_idx):
    _, R, C = h.shape
    tr = _row_tile(R, C)

    def body(p_ref, h_ref, r_ref, out_ref):
        out_ref[...] = ((h_ref[...] + r_ref[0].astype(F32)) + r_ref[1].astype(F32)) + r_ref[2].astype(F32)

    return pl.pallas_call(
        body, name=name,
        grid_spec=pltpu.PrefetchScalarGridSpec(
            num_scalar_prefetch=1, grid=(R // tr,),
            in_specs=[pl.BlockSpec((None, tr, C), lambda i, p_ref: (p_ref[0], i, 0)),
                      pl.BlockSpec((3, tr, C), lambda i, p_ref: (0, i, 0))],
            out_specs=pl.BlockSpec((tr, C), lambda i, p_ref: (i, 0))),
        out_shape=jax.ShapeDtypeStruct((R, C), F32),
        compiler_params=_params(("parallel",)),
    )(p_idx, h, recv)


def _sum8(name, g):
    _, R, C = g.shape

    def body(g_ref, out_ref):
        acc = g_ref[0]
        for d in range(1, N_DEV):
            acc = acc + g_ref[d]
        out_ref[...] = acc

    return pl.pallas_call(body, name=name, out_shape=jax.ShapeDtypeStruct((R, C), F32),
                          compiler_params=_params())(g)


def _adamw(name, w, g, m, v):
    R, C = w.shape
    tr = _row_tile(R, C, target_bytes=1024 * 1024)

    def fn(w, g, m, v):
        m = ADAM_B1 * m + (1.0 - ADAM_B1) * g
        v = ADAM_B2 * v + (1.0 - ADAM_B2) * (g * g)
        m_hat = m / (1.0 - ADAM_B1 ** ADAM_STEP)
        v_hat = v / (1.0 - ADAM_B2 ** ADAM_STEP)
        delta = -ADAM_LR * (m_hat / (jnp.sqrt(v_hat) + ADAM_EPS) + ADAM_WD * w)
        return (delta, m, v), ()

    return _row_call(name, fn, [w, g, m, v], [], [(C, F32)] * 3, [], tr)


def _pack(parts):
    rows = []
    for a in parts:
        flat = a.reshape(-1).astype(F32)
        n = -(-flat.shape[0] // LANES) * LANES
        rows.append(jnp.pad(flat, (0, n - flat.shape[0])).reshape(-1, LANES))
    packed = jnp.concatenate(rows, axis=0)
    pad = -packed.shape[0] % 8
    return jnp.pad(packed, ((0, pad), (0, 0)))


def _unpack(packed, shapes):
    out = []
    r = 0
    for shp in shapes:
        n = math.prod(shp)
        nr = -(-n // LANES)
        out.append(packed[r:r + nr].reshape(-1)[:n].reshape(shp))
        r += nr
    return out


def kernel(x, norm_mix_g, w_in, b_f, gmlp_ln_g, gmlp_ln_b, w_s, b_s, attn_out_g, gmlp_out_g, w_out, norm_ffn_g, w_ff1, w_ff2, norm_final_g, loss_target, m_norm_mix_g, m_w_in, m_b_f, m_gmlp_ln_g, m_gmlp_ln_b, m_w_s, m_b_s, m_attn_out_g, m_gmlp_out_g, m_w_out, m_norm_ffn_g, m_w_ff1, m_w_ff2, m_norm_final_g, v_norm_mix_g, v_w_in, v_b_f, v_gmlp_ln_g, v_gmlp_ln_b, v_w_s, v_b_s, v_attn_out_g, v_gmlp_out_g, v_w_out, v_norm_ffn_g, v_w_ff1, v_w_ff2, v_norm_final_g):
    S, D = x.shape[1], x.shape[2]
    H = b_f.shape[1]
    DA = H * HEAD_DIM
    DG = gmlp_ln_g.shape[1]
    DQKV = 3 * DA
    DMAIN = DQKV + 2 * DG
    DIN = DMAIN + H
    DFF = w_ff1.shape[2] * N_DEV
    w_in_cols = w_in.shape[2]
    assert DIN == w_in_cols * N_DEV and DA == DG and D == DA + DG

    T_ATT = min(T_ATT_MAX, S)
    TR = min(TR_MAX, S)

    x0 = x[0]
    tgt = loss_target[0]
    g_final = norm_final_g.reshape(1, D)

    w_in_all = _all_gather("ag_w_in", w_in[0].astype(BF16))
    w_in_full = jnp.concatenate([w_in_all[n] for n in range(N_DEV)], axis=1)
    w_main = jnp.concatenate([w_in_full[:, :DQKV], w_in_full[:, DQKV + H:]], axis=1)
    w_f = jnp.pad(w_in_full[:, DQKV:DQKV + H], ((0, 0), (0, LANES - H)))
    FB = DFF // N_DEV
    x_pos, y_pos, c_pos = _me()
    c_idx = jnp.reshape(c_pos, (1,)).astype(jnp.int32)
    p_idx = jnp.reshape(2 * x_pos + y_pos, (1,)).astype(jnp.int32)

    (h,), _ = _row_call("rms_mix", lambda xb, g: ((_rms_fwd(xb, g),), ()), [x0], [norm_mix_g], [(D, BF16)], [], TR)
    (zm,), ((w_out_part,),) = _mm_nn("in_proj", h, w_main, [BF16], 1024, 1024, 2048,
                                     jobs=[_job_gather_chips(w_out[0].astype(BF16))])
    (zf,) = _mm_nn("in_proj_f", h, w_f, [F32], 1024, LANES, 2048)
    bf_pad = jnp.pad(b_f, ((0, 0), (0, LANES - H)))
    f_row = _fgate_fwd(zf, bf_pad)
    f_col3 = f_row.reshape(H, S, 1)
    f_row3 = f_row.reshape(H, 1, S)
    (attn, lse_col3), ((w_out_all,), (w_ff1_part,), (w_ff2_part,)) = _attn_fwd(
        zm, f_col3, f_row3, T_ATT, jobs=[_job_gather_sibling(w_out_part), _job_gather_chips(w_ff1[0].astype(BF16)),
                                         _job_gather_chips(w_ff2[0].astype(BF16))])
    w_out_full = w_out_all.reshape(D, D)
    bs_col = b_s[0].reshape(H, CHUNK, 1)
    gm = _gmlp_fwd(zm, gmlp_ln_g, gmlp_ln_b, w_s[0], bs_col, TR)

    def merge_fn(a, g, ga, gg):
        return (jnp.concatenate([_rms_fwd(a, ga), _rms_fwd(g, gg)], axis=1),), ()
    (merged,), _ = _row_call("rms_merge", merge_fn, [attn, gm], [attn_out_g, gmlp_out_g], [(D, BF16)], [], TR)

    (x1,), ((w_ff1_all,), (w_ff2_all,)) = _mm_nn(
        "out_proj", merged, w_out_full, [F32], 1024, 1024, 2048, extras=[x0], epilogue=lambda acc, r: (acc + r,),
        jobs=[_job_gather_sibling(w_ff1_part), _job_gather_sibling(w_ff2_part)])
    w_ff2_full = w_ff2_all.reshape(DFF, D)
    (h2,), _ = _row_call("rms_ffn", lambda xb, g: ((_rms_fwd(xb, g),), ()), [x1], [norm_ffn_g], [(D, BF16)], [], TR)

    tm, tn, tk = min(1024, S), min(1024, FB), min(2048, D)
    o_spec = pl.BlockSpec((tm, tn), lambda i, j, k: (i, j))

    def relu_sq(acc):
        a = jnp.maximum(acc, 0.0)
        return a, a * a
    nj = FB // tn
    a_act, a_sq = _mm(
        "ff1", (S // tm, DFF // tn, D // tk), h2, pl.BlockSpec((tm, tk), lambda i, j, k: (i, k)),
        w_ff1_all, pl.BlockSpec((None, tk, tn), lambda i, j, k: (j // nj, k, j % nj)), NN, (tm, tn),
        [jax.ShapeDtypeStruct((S, DFF), BF16)] * 2, [o_spec] * 2, epilogue=relu_sq)
    (x2,) = _mm_nn("ff2", a_sq, w_ff2_full, [F32], 1024, 1024, 2048, extras=[x1], epilogue=lambda acc, r: (acc + r,))

    def head_fn(xb, t, g):
        rstd = lax.rsqrt(jnp.mean(xb * xb, axis=-1, keepdims=True) + EPS)
        xhat = xb * rstd
        err = xhat * g - t
        loss = 0.5 * jnp.sum(jnp.mean(err * err, axis=-1, keepdims=True), axis=0, keepdims=True)
        dy = err * (1.0 / D)
        dg = jnp.sum(dy * xhat, axis=0, keepdims=True)
        dxhat = dy * g
        dx = rstd * (dxhat - xhat * jnp.mean(dxhat * xhat, axis=-1, keepdims=True))
        return (dx, dx), (dg, jnp.broadcast_to(loss, (1, LANES)))
    (dx2, dx2_b), (dg_final, loss_part) = _row_call(
        "loss_head", head_fn, [x2, tgt], [g_final], [(D, F32), (D, BF16)], [D, LANES], TR)

    (da,) = _mm_nt("ff2_dx", dx2_b, w_ff2_full, [BF16], 1024, 1024, 2048, extras=[a_act],
                   epilogue=lambda acc, a: (2.0 * a.astype(F32) * acc,))
    dw_ff2, dw_ff2_b = _mm_tn("ff2_dw", a_sq, dx2_b, [F32, BF16], 1024, 1024, 1024)
    tm2, tk2 = min(1024, D), min(1024, S)
    dw1_spec = pl.BlockSpec((None, tm2, FB), lambda i, j, k: (j, i, 0))
    (dw_ff1, dw_ff1_b), ((r1_ff2,),) = _mm(
        "ff1_dw", (D // tm2, DFF // FB, S // tk2), h2, pl.BlockSpec((tk2, tm2), lambda i, j, k: (k, i)),
        da, pl.BlockSpec((tk2, FB), lambda i, j, k: (k, j)), TN, (tm2, FB),
        [jax.ShapeDtypeStruct((N_DEV, D, FB), F32), jax.ShapeDtypeStruct((N_DEV, D, FB), BF16)], [dw1_spec] * 2,
        epilogue=lambda acc: (acc, acc), jobs=[_job_scatter_sibling(dw_ff2_b.reshape(4, 2, FB, D))])
    h_ff2, hb_ff2 = _rs_add1("rs_add1_w_ff2", dw_ff2.reshape(4, 2, FB, D), r1_ff2, c_idx)
    tkb = min(1024, FB)
    nkb = FB // tkb
    tnb = min(1024, D)
    (dh2,), ((r2_ff2,), (r1_ff1,)) = _mm(
        "ff1_dx", (S // tm, D // tnb, DFF // tkb), da, pl.BlockSpec((tm, tkb), lambda i, j, k: (i, k)),
        w_ff1_all, pl.BlockSpec((None, tnb, tkb), lambda i, j, k: (k // nkb, j, k % nkb)), NT, (tm, tnb),
        [jax.ShapeDtypeStruct((S, D), F32)], [pl.BlockSpec((tm, tnb), lambda i, j, k: (i, j))],
        jobs=[_job_scatter_chips(hb_ff2), _job_scatter_sibling(dw_ff1_b.reshape(4, 2, D, FB))])
    g_w_ff2 = _rs_add2("rs_add2_w_ff2", h_ff2, r2_ff2, p_idx)
    h_ff1, hb_ff1 = _rs_add1("rs_add1_w_ff1", dw_ff1.reshape(4, 2, D, FB), r1_ff1, c_idx)

    def ffn_bwd_fn(dh, xb, dres, g):
        dx, dg = _rms_bwd(dh, xb, g)
        dx = dx + dres
        return (dx, dx), (dg,)
    (dx1, dx1_b), (dg_ffn,) = _row_call("rms_ffn_bwd", ffn_bwd_fn, [dh2, x1, dx2], [norm_ffn_g],
                                        [(D, F32), (D, BF16)], [D], TR)

    (dmerged,) = _mm_nt("out_proj_dx", dx1_b, w_out_full, [F32], 1024, 1024, 2048)
    dw_out, dw_out_b = _mm_tn("out_proj_dw", merged, dx1_b, [F32, BF16], 1024, 1024, 1024)

    def merge_bwd_fn(dm, a, g, ga, gg):
        da_, dga = _rms_bwd(dm[:, :DA], a, ga)
        dg_, dgg = _rms_bwd(dm[:, DA:], g, gg)
        return (da_, dg_), (dga, dgg)
    (dattn, dgm), (dg_attn, dg_gmlp) = _row_call(
        "rms_merge_bwd", merge_bwd_fn, [dmerged, attn, gm], [attn_out_g, gmlp_out_g], [(DA, F32), (DG, F32)], [DA, DG], TR)

    w_st = jnp.swapaxes(w_s[0], 1, 2)
    dzu, dzv, dw_s, dbs_col, dln_g, dln_b = _gmlp_bwd(dgm, zm, gmlp_ln_g, gmlp_ln_b, w_s[0], w_st, bs_col, TR)

    delta_row = _attn_delta(dattn, attn, TR)
    lse_row3 = lse_col3.reshape(H, 1, S)
    (dq, ds_rowsum), ((r2_ff1,), (r1_out,)) = _attn_bwd_dq(
        zm, dattn, f_col3, f_row3, lse_col3, delta_row.reshape(H, S, 1), T_ATT,
        jobs=[_job_scatter_chips(hb_ff1), _job_scatter_sibling(dw_out_b.reshape(4, 2, D // N_DEV, D))])
    g_w_ff1 = _rs_add2("rs_add2_w_ff1", h_ff1, r2_ff1, p_idx)
    h_out, hb_out = _rs_add1("rs_add1_w_out", dw_out.reshape(4, 2, D // N_DEV, D), r1_out, c_idx)
    (dk, dv, df_col3), ((r2_out,),) = _attn_bwd_dkv(
        zm, dattn, f_col3, f_row3, lse_row3, delta_row.reshape(H, 1, S), ds_rowsum.reshape(H, 1, S), T_ATT,
        jobs=[_job_scatter_chips(hb_out)])
    g_w_out = _rs_add2("rs_add2_w_out", h_out, r2_out, p_idx)
    dzf, dbf = _fgate_bwd(df_col3.reshape(H, S), zf, bf_pad)

    dz_main = jnp.concatenate([dq, dk, dv, dzu, dzv], axis=1)
    (dw_main,) = _mm_tn("in_proj_dw", h, dz_main, [F32], 1024, 1024, 1024)
    (dw_f,) = _mm_tn("in_proj_f_dw", h, dzf, [F32], 2048, LANES, 1024)
    dw_in_full = jnp.concatenate([dw_main[:, :DQKV], dw_f[:, :H], dw_main[:, DQKV:]], axis=1)
    g4_in = jnp.stack([dw_in_full[:, n * w_in_cols:(n + 1) * w_in_cols] for n in range(N_DEV)])
    g4_in = g4_in.reshape(4, 2, D, w_in_cols)
    (dh_f,), ((r1_in,),) = _mm_nt("in_proj_f_dx", dzf, w_f, [F32], 1024, 1024, LANES,
                                  jobs=[_job_scatter_sibling(g4_in.astype(BF16))])
    h_in, hb_in = _rs_add1("rs_add1_w_in", g4_in, r1_in, c_idx)
    (dh,), ((r2_in,),) = _mm_nt("in_proj_dx", dz_main, w_main, [F32], 1024, 1024, 1024, extras=[dh_f],
                                epilogue=lambda acc, r: (acc + r,), jobs=[_job_scatter_chips(hb_in)])
    g_w_in = _rs_add2("rs_add2_w_in", h_in, r2_in, p_idx)

    def mix_bwd_fn(dhb, xb, dres, g):
        dx, dg = _rms_bwd(dhb, xb, g)
        return (dx + dres,), (dg,)
    (grad_x,), (dg_mix,) = _row_call("rms_mix_bwd", mix_bwd_fn, [dh, x0, dx1], [norm_mix_g], [(D, F32)], [D], TR)

    small_shapes = [norm_mix_g.shape, b_f.shape, gmlp_ln_g.shape, gmlp_ln_b.shape, w_s.shape, b_s.shape,
                    attn_out_g.shape, gmlp_out_g.shape, norm_ffn_g.shape, norm_final_g.shape]
    small_parts = [dg_mix, dbf[:, :H], dln_g, dln_b, dw_s, dbs_col, dg_attn, dg_gmlp, dg_ffn, dg_final]
    g_small = _sum8("small_sum", _all_gather("ag_small", _pack(small_parts)))
    w_small = _pack([norm_mix_g, b_f, gmlp_ln_g, gmlp_ln_b, w_s, b_s, attn_out_g, gmlp_out_g, norm_ffn_g, norm_final_g])
    m_small = _pack([m_norm_mix_g, m_b_f, m_gmlp_ln_g, m_gmlp_ln_b, m_w_s, m_b_s, m_attn_out_g, m_gmlp_out_g,
                     m_norm_ffn_g, m_norm_final_g])
    v_small = _pack([v_norm_mix_g, v_b_f, v_gmlp_ln_g, v_gmlp_ln_b, v_w_s, v_b_s, v_attn_out_g, v_gmlp_out_g,
                     v_norm_ffn_g, v_norm_final_g])
    (d_small, nm_small, nv_small), _ = _adamw("adamw_small", w_small, g_small, m_small, v_small)
    gs = _unpack(g_small, small_shapes)
    ds = _unpack(d_small, small_shapes)
    nms = _unpack(nm_small, small_shapes)
    nvs = _unpack(nv_small, small_shapes)

    big = {}
    for nm, w, g, m, v in (("w_in", w_in, g_w_in, m_w_in, v_w_in), ("w_out", w_out, g_w_out, m_w_out, v_w_out),
                           ("w_ff1", w_ff1, g_w_ff1, m_w_ff1, v_w_ff1), ("w_ff2", w_ff2, g_w_ff2, m_w_ff2, v_w_ff2)):
        (d_, m_, v_), _ = _adamw("adamw_" + nm, w[0], g, m[0], v[0])
        big[nm] = (g[None], d_[None], m_[None], v_[None])

    loss = lax.psum(loss_part[0, 0], ("x", "y", "c"))

    def leaves(n):
        sm = (gs, ds, nms, nvs)[n]
        return [sm[0], big["w_in"][n], sm[1], sm[2], sm[3], sm[4], sm[5], sm[6], sm[7], big["w_out"][n], sm[8],
                big["w_ff1"][n], big["w_ff2"][n], sm[9]]

    return (loss, grad_x[None], *leaves(0), *leaves(1), *leaves(2), *leaves(3))
```

```python
import functools
import math

import jax
import jax.numpy as jnp
from jax import lax
from jax.experimental import pallas as pl
from jax.experimental.pallas import tpu as pltpu

F32 = jnp.float32
BF16 = jnp.bfloat16
MESH = pl.DeviceIdType.MESH

HEAD_DIM = 128
CHUNK = 128
EPS = 1e-6
LANES = 128
N_DEV = 8

ADAM_LR = 0.001
ADAM_B1 = 0.9
ADAM_B2 = 0.999
ADAM_EPS = 1e-08
ADAM_WD = 0.01
ADAM_STEP = 10

VMEM_LIMIT_BYTES = 56 * 1024 * 1024
T_ATT_MAX = 512
TR_MAX = 256

NN = ((1,), (0,))
NT = ((1,), (1,))
TN = ((0,), (0,))


def _params(sem=None):
    return pltpu.CompilerParams(dimension_semantics=sem, vmem_limit_bytes=VMEM_LIMIT_BYTES)


def _dot(a, b, contract=NN):
    return lax.dot_general(a, b, (contract, ((), ())), preferred_element_type=F32)


def _dot3(x, t):
    x1 = x.astype(BF16)
    r1 = x - x1.astype(F32)
    x2 = r1.astype(BF16)
    x3 = (r1 - x2.astype(F32)).astype(BF16)
    return _dot(x1, t) + _dot(x2, t) + _dot(x3, t)


def _iota2(shape, dim):
    return lax.broadcasted_iota(jnp.int32, shape, dim)


def _row_call(name, fn, row_ins, bcast_ins, row_outs, acc_outs, tr):
    S = row_ins[0].shape[0]
    assert S % tr == 0
    n_ri, n_bi, n_ro, n_ao = len(row_ins), len(bcast_ins), len(row_outs), len(acc_outs)

    def body(*refs):
        ins = [r[...] for r in refs[:n_ri + n_bi]]
        ro_refs = refs[n_ri + n_bi:n_ri + n_bi + n_ro]
        ao_refs = refs[n_ri + n_bi + n_ro:]
        ro, ao = fn(*ins)
        for r, v in zip(ro_refs, ro):
            r[...] = v.astype(r.dtype)
        if n_ao:
            @pl.when(pl.program_id(0) == 0)
            def _():
                for r in ao_refs:
                    r[...] = jnp.zeros_like(r)
            for r, v in zip(ao_refs, ao):
                r[...] += v

    in_specs = [pl.BlockSpec((tr, a.shape[1]), lambda i: (i, 0)) for a in row_ins]
    in_specs += [pl.BlockSpec(a.shape, lambda i: (0, 0)) for a in bcast_ins]
    out_specs = [pl.BlockSpec((tr, d), lambda i: (i, 0)) for d, _ in row_outs]
    out_specs += [pl.BlockSpec((1, d), lambda i: (0, 0)) for d in acc_outs]
    out_shape = [jax.ShapeDtypeStruct((S, d), dt) for d, dt in row_outs]
    out_shape += [jax.ShapeDtypeStruct((1, d), F32) for d in acc_outs]
    outs = pl.pallas_call(
        body, name=name, grid=(S // tr,), in_specs=in_specs, out_specs=out_specs, out_shape=out_shape,
        compiler_params=_params(("arbitrary",) if n_ao else ("parallel",)),
    )(*row_ins, *bcast_ins)
    return outs[:n_ro], outs[n_ro:]


def _rms_fwd(x, g):
    rstd = lax.rsqrt(jnp.mean(x * x, axis=-1, keepdims=True) + EPS)
    return x * rstd * g


def _rms_bwd(dy, x, g):
    rstd = lax.rsqrt(jnp.mean(x * x, axis=-1, keepdims=True) + EPS)
    xhat = x * rstd
    dg = jnp.sum(dy * xhat, axis=0, keepdims=True)
    dxhat = dy * g
    dx = rstd * (dxhat - xhat * jnp.mean(dxhat * xhat, axis=-1, keepdims=True))
    return dx, dg


_GELU_C = math.sqrt(2.0 / math.pi)


def _gelu(x):
    return 0.5 * x * (1.0 + jnp.tanh(_GELU_C * (x + 0.044715 * (x * x * x))))


def _gelu_grad(x):
    t = jnp.tanh(_GELU_C * (x + 0.044715 * (x * x * x)))
    return 0.5 * (1.0 + t) + 0.5 * x * (1.0 - t * t) * (_GELU_C * (1.0 + 3.0 * 0.044715 * (x * x)))


def _me():
    return lax.axis_index("x"), lax.axis_index("y"), lax.axis_index("c")


def _other_chips(x, y):
    return [(1 - x, y), (x, 1 - y), (1 - x, 1 - y)]


_ANY = pl.BlockSpec(memory_space=pl.ANY)


class _Job:
    def __init__(self, ins, outs, n_sems, make, aliases=None):
        self.ins, self.outs, self.n_sems, self.make, self.aliases = ins, outs, n_sems, make, aliases or {}


def _job_gather_chips(blk):
    R, C = blk.shape

    def make(ins, outs, send_sems, recv_sems, base):
        (x_ref,), (out_ref,) = ins, outs
        x, y, c = _me()
        mine = 4 * x + 2 * y + c
        targets = [(x, y, 1 - c)] + [(cx, cy, c) for cx, cy in _other_chips(x, y)]

        def copy(k, slab, to):
            return pltpu.make_async_remote_copy(
                src_ref=x_ref, dst_ref=out_ref.at[slab], send_sem=send_sems.at[base + k],
                recv_sem=recv_sems.at[base + k], device_id=to, device_id_type=MESH)

        starts = [copy(k, mine, to) for k, to in enumerate(targets)]
        arrivals = [copy(k, 4 * tx + 2 * ty + tc, (tx, ty, tc)) for k, (tx, ty, tc) in enumerate(targets)]
        local = [pltpu.make_async_copy(x_ref, out_ref.at[mine], send_sems.at[base + 4])]
        return starts, arrivals, local

    return _Job([blk], [jax.ShapeDtypeStruct((N_DEV, R, C), blk.dtype)], 5, make)


def _job_gather_sibling(part):
    def make(ins, outs, send_sems, recv_sems, base):
        (out_ref,) = outs
        x, y, c = _me()

        def copy(k, slab):
            return pltpu.make_async_remote_copy(
                src_ref=out_ref.at[slab], dst_ref=out_ref.at[slab], send_sem=send_sems.at[base + k],
                recv_sem=recv_sems.at[base + k], device_id=(x, y, 1 - c), device_id_type=MESH)

        chips = _other_chips(x, y)
        starts = [copy(k, 4 * cx + 2 * cy + c) for k, (cx, cy) in enumerate(chips)]
        arrivals = [copy(k, 4 * cx + 2 * cy + (1 - c)) for k, (cx, cy) in enumerate(chips)]
        return starts, arrivals, []

    return _Job([part], [jax.ShapeDtypeStruct(part.shape, part.dtype)], 3, make, aliases={0: 0})


def _job_scatter_sibling(gb):
    _, _, R, C = gb.shape

    def make(ins, outs, send_sems, recv_sems, base):
        (g_ref,), (recv_ref,) = ins, outs
        x, y, c = _me()
        copies = [pltpu.make_async_remote_copy(
            src_ref=g_ref.at[p, 1 - c], dst_ref=recv_ref.at[p], send_sem=send_sems.at[base + p],
            recv_sem=recv_sems.at[base + p], device_id=(x, y, 1 - c), device_id_type=MESH) for p in range(4)]
        return copies, copies, []

    return _Job([gb], [jax.ShapeDtypeStruct((4, R, C), gb.dtype)], 4, make)


def _job_scatter_chips(hb):
    _, R, C = hb.shape

    def make(ins, outs, send_sems, recv_sems, base):
        (h_ref,), (recv_ref,) = ins, outs
        x, y, c = _me()
        copies = [pltpu.make_async_remote_copy(
            src_ref=h_ref.at[2 * cx + cy], dst_ref=recv_ref.at[n], send_sem=send_sems.at[base + n],
            recv_sem=recv_sems.at[base + n], device_id=(cx, cy, c), device_id_type=MESH)
            for n, (cx, cy) in enumerate(_other_chips(x, y))]
        return copies, copies, []

    return _Job([hb], [jax.ShapeDtypeStruct((3, R, C), hb.dtype)], 3, make)


def _carry_call(body, *, name, grid, in_specs, out_specs, out_shape, scratch_shapes, semantics, args, jobs=()):
    jobs = list(jobs)
    n_in, n_out, n_scr = len(in_specs), len(out_specs), len(scratch_shapes)
    j_ins = [a for j in jobs for a in j.ins]
    j_outs = [o for j in jobs for o in j.outs]
    n_sems = sum(j.n_sems for j in jobs)
    aliases = {}
    i0, o0 = n_in, n_out
    for j in jobs:
        for a, b in j.aliases.items():
            aliases[i0 + a] = o0 + b
        i0 += len(j.ins)
        o0 += len(j.outs)

    def full_body(*refs):
        ins = refs[:n_in]
        jin = refs[n_in:n_in + len(j_ins)]
        outs = refs[n_in + len(j_ins):n_in + len(j_ins) + n_out]
        jout = refs[n_in + len(j_ins) + n_out:n_in + len(j_ins) + n_out + len(j_outs)]
        scr = refs[n_in + len(j_ins) + n_out + len(j_outs):]
        if jobs:
            send_sems, recv_sems = scr[n_scr], scr[n_scr + 1]
            starts, arrivals, local = [], [], []
            base = i0 = o0 = 0
            for j in jobs:
                s, a, l = j.make(jin[i0:i0 + len(j.ins)], jout[o0:o0 + len(j.outs)], send_sems, recv_sems, base)
                starts += s
                arrivals += a
                local += l
                base += j.n_sems
                i0 += len(j.ins)
                o0 += len(j.outs)
            pids = [pl.program_id(d) for d in range(len(grid))]
            first = functools.reduce(jnp.logical_and, [p == 0 for p in pids])
            last = functools.reduce(jnp.logical_and, [p == n - 1 for p, n in zip(pids, grid)])

            @pl.when(first)
            def _():
                for cp in local + starts:
                    cp.start()

        body(*ins, *outs, *scr[:n_scr])

        if jobs:
            @pl.when(last)
            def _():
                for cp in arrivals:
                    cp.wait_recv()
                for cp in starts:
                    cp.wait_send()
                for cp in local:
                    cp.wait()

    sems = [pltpu.SemaphoreType.DMA((n_sems,)), pltpu.SemaphoreType.DMA((n_sems,))] if jobs else []
    res = pl.pallas_call(
        full_body, name=name, grid=grid,
        in_specs=list(in_specs) + [_ANY] * len(j_ins),
        out_specs=list(out_specs) + [_ANY] * len(j_outs),
        out_shape=list(out_shape) + j_outs,
        scratch_shapes=list(scratch_shapes) + sems,
        input_output_aliases=aliases,
        compiler_params=_params(("arbitrary",) * len(grid) if jobs else semantics),
    )(*args, *j_ins)
    body_res, job_res = res[:n_out], res[n_out:]
    per_job = []
    for j in jobs:
        per_job.append(job_res[:len(j.outs)])
        job_res = job_res[len(j.outs):]
    return body_res, per_job


def _mm(name, grid, a, a_spec, b, b_spec, contract, acc_shape, out_shape, out_specs, extras=(), epilogue=None, jobs=()):
    nk = grid[2]
    n_e = len(extras)
    n_o = len(out_shape)
    if epilogue is None:
        epilogue = lambda acc: (acc,)

    def body(a_ref, b_ref, *rest):
        e_refs = rest[:n_e]
        o_refs = rest[n_e:n_e + n_o]

        def finish(total):
            res = epilogue(total, *[r[...] for r in e_refs])
            for o, r in zip(o_refs, res):
                o[...] = r.astype(o.dtype)

        if nk == 1:
            finish(_dot(a_ref[...], b_ref[...], contract))
            return
        acc = rest[n_e + n_o]
        k = pl.program_id(2)

        @pl.when(k == 0)
        def _():
            acc[...] = _dot(a_ref[...], b_ref[...], contract)

        @pl.when(jnp.logical_and(k > 0, k < nk - 1))
        def _():
            acc[...] += _dot(a_ref[...], b_ref[...], contract)

        @pl.when(k == nk - 1)
        def _():
            finish(acc[...] + _dot(a_ref[...], b_ref[...], contract))

    outs, job_res = _carry_call(
        body, name=name, grid=grid, in_specs=[a_spec, b_spec] + [s for _, s in extras],
        out_specs=list(out_specs), out_shape=list(out_shape),
        scratch_shapes=[pltpu.VMEM(acc_shape, F32)] if nk > 1 else [],
        semantics=("parallel", "parallel", "arbitrary"), args=[a, b] + [e for e, _ in extras], jobs=jobs)
    return (outs, job_res) if jobs else outs


def _mm_nn(name, a, b, out_dtypes, tm, tn, tk, extras=(), epilogue=None, jobs=()):
    M, K = a.shape
    N = b.shape[1]
    tm, tn, tk = min(tm, M), min(tn, N), min(tk, K)
    o_spec = pl.BlockSpec((tm, tn), lambda i, j, k: (i, j))
    return _mm(name, (M // tm, N // tn, K // tk),
               a, pl.BlockSpec((tm, tk), lambda i, j, k: (i, k)),
               b, pl.BlockSpec((tk, tn), lambda i, j, k: (k, j)), NN, (tm, tn),
               [jax.ShapeDtypeStruct((M, N), dt) for dt in out_dtypes], [o_spec] * len(out_dtypes),
               [(e, o_spec) for e in extras], epilogue, jobs)


def _mm_nt(name, a, b, out_dtypes, tm, tn, tk, extras=(), epilogue=None, jobs=()):
    M, K = a.shape
    N = b.shape[0]
    tm, tn, tk = min(tm, M), min(tn, N), min(tk, K)
    o_spec = pl.BlockSpec((tm, tn), lambda i, j, k: (i, j))
    return _mm(name, (M // tm, N // tn, K // tk),
               a, pl.BlockSpec((tm, tk), lambda i, j, k: (i, k)),
               b, pl.BlockSpec((tn, tk), lambda i, j, k: (j, k)), NT, (tm, tn),
               [jax.ShapeDtypeStruct((M, N), dt) for dt in out_dtypes], [o_spec] * len(out_dtypes),
               [(e, o_spec) for e in extras], epilogue, jobs)


def _mm_tn(name, a, b, out_dtypes, tm, tn, tk, jobs=()):
    K, M = a.shape
    N = b.shape[1]
    tm, tn, tk = min(tm, M), min(tn, N), min(tk, K)
    o_spec = pl.BlockSpec((tm, tn), lambda i, j, k: (i, j))
    return _mm(name, (M // tm, N // tn, K // tk),
               a, pl.BlockSpec((tk, tm), lambda i, j, k: (k, i)),
               b, pl.BlockSpec((tk, tn), lambda i, j, k: (k, j)), TN, (tm, tn),
               [jax.ShapeDtypeStruct((M, N), dt) for dt in out_dtypes], [o_spec] * len(out_dtypes),
               epilogue=lambda acc: (acc,) * len(out_dtypes), jobs=jobs)


def _fgate_fwd(zf, bf):
    S = zf.shape[0]
    nc = S // CHUNK

    def body(zf_ref, bf_ref, f_ref):
        upper = (_iota2((CHUNK, CHUNK), 0) <= _iota2((CHUNK, CHUNK), 1)).astype(BF16)
        carry = jnp.zeros((8, 1), F32)
        for c in range(nc):
            t = zf_ref[c * CHUNK:(c + 1) * CHUNK, :] + bf_ref[...]
            lf = jnp.minimum(t, 0.0) - jnp.log(1.0 + jnp.exp(-jnp.abs(t)))
            lf_rows = lf.T[0:8, :]
            f_ref[:, c * CHUNK:(c + 1) * CHUNK] = (_dot3(lf_rows, upper) + carry) * LOG2E
            carry = carry + jnp.sum(lf_rows, axis=-1, keepdims=True)

    return pl.pallas_call(
        body, name="fgate_fwd", out_shape=jax.ShapeDtypeStruct((8, S), F32),
        compiler_params=_params(),
    )(zf, bf)


def _fgate_bwd(df, zf, bf):
    S = zf.shape[0]
    nc = S // CHUNK

    def body(df_ref, zf_ref, bf_ref, dzf_ref, dbf_ref):
        lower = (_iota2((CHUNK, CHUNK), 0) >= _iota2((CHUNK, CHUNK), 1)).astype(BF16)
        carry = jnp.zeros((8, 1), F32)
        dbf = jnp.zeros((1, LANES), F32)
        for c in reversed(range(nc)):
            sl = slice(c * CHUNK, (c + 1) * CHUNK)
            df = df_ref[:, sl]
            r = _dot3(df, lower) + carry
            carry = carry + jnp.sum(df, axis=-1, keepdims=True)
            r_cols = jnp.concatenate([r, jnp.zeros((CHUNK - 8, CHUNK), F32)], axis=0).T
            t = zf_ref[sl, :] + bf_ref[...]
            dz = r_cols * (1.0 / (1.0 + jnp.exp(t)))
            dzf_ref[sl, :] = dz.astype(BF16)
            dbf = dbf + jnp.sum(dz, axis=0, keepdims=True)
        dbf_ref[...] = dbf

    return pl.pallas_call(
        body, name="fgate_bwd",
        out_shape=[jax.ShapeDtypeStruct((S, LANES), BF16), jax.ShapeDtypeStruct((1, LANES), F32)],
        compiler_params=_params(),
    )(df, zf, bf)


_NEG = -1e30
LOG2E = 1.4426950408889634
N_SPLIT = 2


def _attn_consts(T):
    rows, cols = _iota2((T, T), 0), _iota2((T, T), 1)
    return cols <= rows, rows <= cols


def _attn2_fwd(zm, f2col, f2row, T, jobs=()):
    S = zm.shape[0]
    H = f2col.shape[0]
    nb = S // T
    c2 = LOG2E / math.sqrt(HEAD_DIM)

    def body(q_ref, k_ref, v_ref, fq_ref, fk_ref, o_ref, lse_ref, vaug_s):
        i = pl.program_id(1)

        @pl.when(i == 0)
        def _():
            vaug_s[:, :HEAD_DIM] = v_ref[...]
            vaug_s[:, HEAD_DIM:] = jnp.ones((S, HEAD_DIM), BF16)

        keep = _attn_consts(T)[0]
        TH = T // N_SPLIT

        def block(j, diagonal, state):
            r0 = pl.multiple_of(j * T, T)
            kb = k_ref[pl.ds(r0, T), :]
            vb = vaug_s[pl.ds(r0, T), :]
            fk = fk_ref[j]
            new = []
            for g, (m_old, acc) in enumerate(state):
                rows = slice(g * TH, (g + 1) * TH)
                s = _dot(q_ref[rows, :], kb, NT) * c2 + (fq_ref[rows, :] - fk)
                if diagonal:
                    s = jnp.where(keep[rows, :], s, _NEG)
                m_new = jnp.maximum(m_old, jnp.max(s, axis=-1, keepdims=True))
                p = jnp.exp2(s - m_new).astype(BF16)
                new.append((m_new, jnp.exp2(m_old - m_new) * acc + _dot(p, vb)))
            return tuple(new)

        init = tuple((jnp.full((TH, 1), _NEG, F32), jnp.zeros((TH, 2 * HEAD_DIM), F32)) for _ in range(N_SPLIT))
        state = lax.fori_loop(0, i, lambda j, st: block(j, False, st), init)
        state = block(i, True, state)
        for g, (m, acc) in enumerate(state):
            rows = slice(g * TH, (g + 1) * TH)
            o_ref[rows, :] = acc[:, :HEAD_DIM] / acc[:, HEAD_DIM:]
            lse_ref[rows, :] = m + jnp.log2(acc[:, HEAD_DIM:HEAD_DIM + 1])

    nh = H
    return _carry_call(
        body, name="attn_fwd", grid=(H, nb), jobs=jobs, args=[zm, zm, zm, f2col, f2row],
        semantics=("arbitrary", "arbitrary"),
        in_specs=[
            pl.BlockSpec((T, HEAD_DIM), lambda h, i: (i, h)),
            pl.BlockSpec((S, HEAD_DIM), lambda h, i: (0, nh + h)),
            pl.BlockSpec((S, HEAD_DIM), lambda h, i: (0, 2 * nh + h)),
            pl.BlockSpec((None, T, 1), lambda h, i: (h, i, 0)),
            pl.BlockSpec((None, nb, 1, T), lambda h, i: (h, 0, 0, 0)),
        ],
        out_specs=[pl.BlockSpec((T, HEAD_DIM), lambda h, i: (i, h)), pl.BlockSpec((None, T, 1), lambda h, i: (h, i, 0))],
        out_shape=[jax.ShapeDtypeStruct((S, H * HEAD_DIM), F32), jax.ShapeDtypeStruct((H, S, 1), F32)],
        scratch_shapes=[pltpu.VMEM((S, 2 * HEAD_DIM), BF16)],
    )


def _attn2_bwd_dq(zm, dattn, f2col, f2row, lse2_col, delta_col, T, jobs=()):
    S = zm.shape[0]
    H = f2col.shape[0]
    nb = S // T
    scale = 1.0 / math.sqrt(HEAD_DIM)
    c2 = LOG2E * scale

    def body(q_ref, k_ref, v_ref, do_ref, fq_ref, fk_ref, lse_ref, dl_ref, dq_ref, rs_ref, bias_s, do_s):
        i = pl.program_id(1)
        keep = _attn_consts(T)[0]
        TH = T // N_SPLIT
        bias_s[...] = fq_ref[...] - lse_ref[...]
        do_s[...] = do_ref[...].astype(BF16)

        def block(j, diagonal, state):
            r0 = pl.multiple_of(j * T, T)
            kb = k_ref[pl.ds(r0, T), :]
            vb = v_ref[pl.ds(r0, T), :]
            fk = fk_ref[j]
            new = []
            for g, (acc, rs) in enumerate(state):
                rows = slice(g * TH, (g + 1) * TH)
                s = _dot(q_ref[rows, :], kb, NT) * c2 + (bias_s[rows, :] - fk)
                if diagonal:
                    s = jnp.where(keep[rows, :], s, _NEG)
                ds = jnp.exp2(s) * (_dot(do_s[rows, :], vb, NT) - dl_ref[rows, :])
                new.append((acc + _dot(ds.astype(BF16), kb), rs + jnp.sum(ds, axis=-1, keepdims=True)))
            return tuple(new)

        init = tuple((jnp.zeros((TH, HEAD_DIM), F32), jnp.zeros((TH, 1), F32)) for _ in range(N_SPLIT))
        state = lax.fori_loop(0, i, lambda j, st: block(j, False, st), init)
        state = block(i, True, state)
        for g, (acc, rs) in enumerate(state):
            rows = slice(g * TH, (g + 1) * TH)
            dq_ref[rows, :] = (acc * scale).astype(BF16)
            rs_ref[rows, :] = rs

    nh = H
    col = pl.BlockSpec((None, T, 1), lambda h, i: (h, i, 0))
    blk = pl.BlockSpec((T, HEAD_DIM), lambda h, i: (i, h))
    return _carry_call(
        body, name="attn_bwd_dq", grid=(H, nb), jobs=jobs,
        args=[zm, zm, zm, dattn, f2col, f2row, lse2_col, delta_col], semantics=("arbitrary", "arbitrary"),
        in_specs=[
            blk,
            pl.BlockSpec((S, HEAD_DIM), lambda h, i: (0, nh + h)),
            pl.BlockSpec((S, HEAD_DIM), lambda h, i: (0, 2 * nh + h)),
            blk, col,
            pl.BlockSpec((None, nb, 1, T), lambda h, i: (h, 0, 0, 0)),
            col, col,
        ],
        out_specs=[blk, col],
        out_shape=[jax.ShapeDtypeStruct((S, H * HEAD_DIM), BF16), jax.ShapeDtypeStruct((H, S, 1), F32)],
        scratch_shapes=[pltpu.VMEM((T, 1), F32), pltpu.VMEM((T, HEAD_DIM), BF16)],
    )


def _attn2_bwd_dkv(zm, dattn, f2col, f2row, lse2_row, delta_row, rowsum_row, T, jobs=()):
    S = zm.shape[0]
    H = f2col.shape[0]
    nb = S // T
    scale = 1.0 / math.sqrt(HEAD_DIM)
    c2 = LOG2E * scale

    def body(q_ref, k_ref, v_ref, do_ref, fk_ref, fq_ref, lse_ref, dl_ref, rs_ref, dk_ref, dv_ref, df_ref):
        j = pl.program_id(1)
        keep = _attn_consts(T)[1]
        TH = T // N_SPLIT

        def block(i, diagonal, state):
            r0 = pl.multiple_of(i * T, T)
            qb = q_ref[pl.ds(r0, T), :]
            do = do_ref[pl.ds(r0, T), :].astype(BF16)
            bias = fq_ref[i] - lse_ref[i]
            dl = dl_ref[i] + rs_ref[i]
            new = []
            for g, (dk, dv, df) in enumerate(state):
                rows = slice(g * TH, (g + 1) * TH)
                st = _dot(k_ref[rows, :], qb, NT) * c2 + (bias - fk_ref[rows, :])
                if diagonal:
                    st = jnp.where(keep[rows, :], st, _NEG)
                pt = jnp.exp2(st)
                dst = pt * (_dot(v_ref[rows, :], do, NT) - dl)
                new.append((dk + _dot(dst.astype(BF16), qb), dv + _dot(pt.astype(BF16), do),
                            df - jnp.sum(dst, axis=-1, keepdims=True)))
            return tuple(new)

        init = tuple((jnp.zeros((TH, HEAD_DIM), F32), jnp.zeros((TH, HEAD_DIM), F32), jnp.zeros((TH, 1), F32))
                     for _ in range(N_SPLIT))
        state = block(j, True, init)
        state = lax.fori_loop(j + 1, nb, lambda i, st: block(i, False, st), state)
        for g, (dk, dv, df) in enumerate(state):
            rows = slice(g * TH, (g + 1) * TH)
            dk_ref[rows, :] = (dk * scale).astype(BF16)
            dv_ref[rows, :] = dv.astype(BF16)
            df_ref[rows, :] = df

    nh = H
    row = pl.BlockSpec((None, nb, 1, T), lambda h, j: (h, 0, 0, 0))
    whole = pl.BlockSpec((S, HEAD_DIM), lambda h, j: (0, h))
    kv_out = pl.BlockSpec((T, HEAD_DIM), lambda h, j: (j, h))
    col = pl.BlockSpec((None, T, 1), lambda h, j: (h, j, 0))
    return _carry_call(
        body, name="attn_bwd_dkv", grid=(H, nb), jobs=jobs,
        args=[zm, zm, zm, dattn, f2col, f2row, lse2_row, delta_row, rowsum_row],
        semantics=("arbitrary", "arbitrary"),
        in_specs=[
            whole,
            pl.BlockSpec((T, HEAD_DIM), lambda h, j: (j, nh + h)),
            pl.BlockSpec((T, HEAD_DIM), lambda h, j: (j, 2 * nh + h)),
            whole, col, row, row, row, row,
        ],
        out_specs=[kv_out, kv_out, col],
        out_shape=[jax.ShapeDtypeStruct((S, H * HEAD_DIM), BF16), jax.ShapeDtypeStruct((S, H * HEAD_DIM), BF16),
                   jax.ShapeDtypeStruct((H, S, 1), F32)],
        scratch_shapes=[],
    )


def _attn_fwd(zm, fcol, frow, T, jobs=()):
    S = zm.shape[0]
    H = fcol.shape[0]
    nb = S // T
    scale = 1.0 / math.sqrt(HEAD_DIM)

    def body(q_ref, k_ref, v_ref, fq_ref, fk_ref, o_ref, lse_ref, m_s, l_s, acc_s):
        i = pl.program_id(1)
        j = pl.program_id(2)

        @pl.when(j == 0)
        def _():
            m_s[...] = jnp.full_like(m_s, _NEG)
            l_s[...] = jnp.zeros_like(l_s)
            acc_s[...] = jnp.zeros_like(acc_s)

        @pl.when(j <= i)
        def _():
            s = _dot(q_ref[...], k_ref[...], NT) * scale + (fq_ref[...] - fk_ref[...])
            keep = (_iota2((T, T), 1) + j * T) <= (_iota2((T, T), 0) + i * T)
            s = jnp.where(keep, s, _NEG)
            m_new = jnp.maximum(m_s[...], jnp.max(s, axis=-1, keepdims=True))
            alpha = jnp.exp(m_s[...] - m_new)
            p = jnp.exp(s - m_new)
            l_s[...] = alpha * l_s[...] + jnp.sum(p, axis=-1, keepdims=True)
            acc_s[...] = alpha * acc_s[...] + _dot(p.astype(BF16), v_ref[...])
            m_s[...] = m_new

        @pl.when(j == nb - 1)
        def _():
            o_ref[...] = acc_s[...] / l_s[...]
            lse_ref[...] = m_s[...] + jnp.log(l_s[...])

    nh = H
    return _carry_call(
        body, name="attn_fwd", grid=(H, nb, nb), jobs=jobs, args=[zm, zm, zm, fcol, frow],
        semantics=("parallel", "parallel", "arbitrary"),
        in_specs=[
            pl.BlockSpec((T, HEAD_DIM), lambda h, i, j: (i, h)),
            pl.BlockSpec((T, HEAD_DIM), lambda h, i, j: (jnp.minimum(j, i), nh + h)),
            pl.BlockSpec((T, HEAD_DIM), lambda h, i, j: (jnp.minimum(j, i), 2 * nh + h)),
            pl.BlockSpec((None, T, 1), lambda h, i, j: (h, i, 0)),
            pl.BlockSpec((None, 1, T), lambda h, i, j: (h, 0, jnp.minimum(j, i))),
        ],
        out_specs=[
            pl.BlockSpec((T, HEAD_DIM), lambda h, i, j: (i, h)),
            pl.BlockSpec((None, T, 1), lambda h, i, j: (h, i, 0)),
        ],
        out_shape=[jax.ShapeDtypeStruct((S, H * HEAD_DIM), F32), jax.ShapeDtypeStruct((H, S, 1), F32)],
        scratch_shapes=[pltpu.VMEM((T, 1), F32), pltpu.VMEM((T, 1), F32), pltpu.VMEM((T, HEAD_DIM), F32)],
    )


def _attn_delta(dattn, attn, tr):
    S, DA = attn.shape
    H = DA // HEAD_DIM

    def body(do_ref, o_ref, out_ref):
        lo = _iota2((DA, LANES), 1) * HEAD_DIM
        sel = ((_iota2((DA, LANES), 0) >= lo) & (_iota2((DA, LANES), 0) < lo + HEAD_DIM)).astype(BF16)
        d = _dot3(do_ref[...] * o_ref[...], sel)
        for c in range(tr // CHUNK):
            out_ref[:, c * CHUNK:(c + 1) * CHUNK] = d[c * CHUNK:(c + 1) * CHUNK, :].T[0:H, :]

    return pl.pallas_call(
        body, name="attn_delta", grid=(S // tr,),
        in_specs=[pl.BlockSpec((tr, DA), lambda i: (i, 0))] * 2,
        out_specs=pl.BlockSpec((H, tr), lambda i: (0, i)),
        out_shape=jax.ShapeDtypeStruct((H, S), F32),
        compiler_params=_params(("parallel",)),
    )(dattn, attn)


def _attn_bwd_dq(zm, dattn, fcol, frow, lse_col, delta_col, T, jobs=()):
    S = zm.shape[0]
    H = fcol.shape[0]
    nb = S // T
    scale = 1.0 / math.sqrt(HEAD_DIM)

    def body(q_ref, k_ref, v_ref, do_ref, fq_ref, fk_ref, lse_ref, dl_ref, dq_ref, rs_ref, acc_s, rs_s):
        i = pl.program_id(1)
        j = pl.program_id(2)

        @pl.when(j == 0)
        def _():
            acc_s[...] = jnp.zeros_like(acc_s)
            rs_s[...] = jnp.zeros_like(rs_s)

        @pl.when(j <= i)
        def _():
            s = _dot(q_ref[...], k_ref[...], NT) * scale + (fq_ref[...] - fk_ref[...])
            keep = (_iota2((T, T), 1) + j * T) <= (_iota2((T, T), 0) + i * T)
            p = jnp.exp(jnp.where(keep, s - lse_ref[...], _NEG))
            dp = _dot(do_ref[...].astype(BF16), v_ref[...], NT)
            ds = p * (dp - dl_ref[...])
            acc_s[...] += _dot(ds.astype(BF16), k_ref[...])
            rs_s[...] += jnp.sum(ds, axis=-1, keepdims=True)

        @pl.when(j == nb - 1)
        def _():
            dq_ref[...] = (acc_s[...] * scale).astype(BF16)
            rs_ref[...] = rs_s[...]

    nh = H
    col = pl.BlockSpec((None, T, 1), lambda h, i, j: (h, i, 0))
    return _carry_call(
        body, name="attn_bwd_dq", grid=(H, nb, nb), jobs=jobs,
        args=[zm, zm, zm, dattn, fcol, frow, lse_col, delta_col], semantics=("parallel", "parallel", "arbitrary"),
        in_specs=[
            pl.BlockSpec((T, HEAD_DIM), lambda h, i, j: (i, h)),
            pl.BlockSpec((T, HEAD_DIM), lambda h, i, j: (jnp.minimum(j, i), nh + h)),
            pl.BlockSpec((T, HEAD_DIM), lambda h, i, j: (jnp.minimum(j, i), 2 * nh + h)),
            pl.BlockSpec((T, HEAD_DIM), lambda h, i, j: (i, h)),
            col,
            pl.BlockSpec((None, 1, T), lambda h, i, j: (h, 0, jnp.minimum(j, i))),
            col, col,
        ],
        out_specs=[pl.BlockSpec((T, HEAD_DIM), lambda h, i, j: (i, h)), col],
        out_shape=[jax.ShapeDtypeStruct((S, H * HEAD_DIM), BF16), jax.ShapeDtypeStruct((H, S, 1), F32)],
        scratch_shapes=[pltpu.VMEM((T, HEAD_DIM), F32), pltpu.VMEM((T, 1), F32)],
    )


def _attn_bwd_dkv(zm, dattn, fcol, frow, lse_row, delta_row, rowsum_row, T, jobs=()):
    S = zm.shape[0]
    H = fcol.shape[0]
    nb = S // T
    scale = 1.0 / math.sqrt(HEAD_DIM)

    def body(q_ref, k_ref, v_ref, do_ref, fk_ref, fq_ref, lse_ref, dl_ref, rs_ref,
             dk_ref, dv_ref, df_ref, dk_s, dv_s, df_s):
        j = pl.program_id(1)
        i = pl.program_id(2)

        @pl.when(i == 0)
        def _():
            dk_s[...] = jnp.zeros_like(dk_s)
            dv_s[...] = jnp.zeros_like(dv_s)
            df_s[...] = jnp.zeros_like(df_s)

        @pl.when(i >= j)
        def _():
            st = _dot(k_ref[...], q_ref[...], NT) * scale + (fq_ref[...] - fk_ref[...])
            keep = (_iota2((T, T), 0) + j * T) <= (_iota2((T, T), 1) + i * T)
            pt = jnp.exp(jnp.where(keep, st - lse_ref[...], _NEG))
            do = do_ref[...].astype(BF16)
            dpt = _dot(v_ref[...], do, NT)
            dst = pt * (dpt - (dl_ref[...] + rs_ref[...]))
            dv_s[...] += _dot(pt.astype(BF16), do)
            dk_s[...] += _dot(dst.astype(BF16), q_ref[...])
            df_s[...] -= jnp.sum(dst, axis=-1, keepdims=True)

        @pl.when(i == nb - 1)
        def _():
            dk_ref[...] = (dk_s[...] * scale).astype(BF16)
            dv_ref[...] = dv_s[...].astype(BF16)
            df_ref[...] = df_s[...]

    nh = H
    row = pl.BlockSpec((None, 1, T), lambda h, j, i: (h, 0, jnp.maximum(i, j)))
    kv_out = pl.BlockSpec((T, HEAD_DIM), lambda h, j, i: (j, h))
    return _carry_call(
        body, name="attn_bwd_dkv", grid=(H, nb, nb), jobs=jobs,
        args=[zm, zm, zm, dattn, fcol, frow, lse_row, delta_row, rowsum_row],
        semantics=("parallel", "parallel", "arbitrary"),
        in_specs=[
            pl.BlockSpec((T, HEAD_DIM), lambda h, j, i: (jnp.maximum(i, j), h)),
            pl.BlockSpec((T, HEAD_DIM), lambda h, j, i: (j, nh + h)),
            pl.BlockSpec((T, HEAD_DIM), lambda h, j, i: (j, 2 * nh + h)),
            pl.BlockSpec((T, HEAD_DIM), lambda h, j, i: (jnp.maximum(i, j), h)),
            pl.BlockSpec((None, T, 1), lambda h, j, i: (h, j, 0)),
            row, row, row, row,
        ],
        out_specs=[kv_out, kv_out, pl.BlockSpec((None, T, 1), lambda h, j, i: (h, j, 0))],
        out_shape=[jax.ShapeDtypeStruct((S, H * HEAD_DIM), BF16), jax.ShapeDtypeStruct((S, H * HEAD_DIM), BF16),
                   jax.ShapeDtypeStruct((H, S, 1), F32)],
        scratch_shapes=[pltpu.VMEM((T, HEAD_DIM), F32), pltpu.VMEM((T, HEAD_DIM), F32), pltpu.VMEM((T, 1), F32)],
    )


def _ln_stats(x):
    mu = jnp.mean(x, axis=-1, keepdims=True)
    xc = x - mu
    rstd = lax.rsqrt(jnp.mean(xc * xc, axis=-1, keepdims=True) + EPS)
    return xc * rstd, rstd


def _tril_mask():
    return _iota2((CHUNK, CHUNK), 0) >= _iota2((CHUNK, CHUNK), 1)


def _gmlp_fwd(zm, ln_g, ln_b, w_s, bs_col, tr):
    S = zm.shape[0]
    H = w_s.shape[0]
    DG = H * HEAD_DIM

    def body(zu_ref, zv_ref, g_ref, b_ref, w_ref, bs_ref, out_ref):
        u = _gelu(zu_ref[...].astype(F32))
        y, _ = _ln_stats(_gelu(zv_ref[...].astype(F32)))
        v = (y * g_ref[...] + b_ref[...]).astype(BF16)
        mask = _tril_mask()
        for h in range(H):
            wc = jnp.where(mask, w_ref[h], 0.0).astype(BF16)
            cs = slice(h * HEAD_DIM, (h + 1) * HEAD_DIM)
            for c in range(tr // CHUNK):
                rs = slice(c * CHUNK, (c + 1) * CHUNK)
                mix = _dot(wc, v[rs, cs]) + bs_ref[h]
                out_ref[rs, cs] = u[rs, cs] * mix

    full = lambda a: pl.BlockSpec(a.shape, lambda i: (0,) * a.ndim)
    return pl.pallas_call(
        body, name="gmlp_fwd", grid=(S // tr,),
        in_specs=[pl.BlockSpec((tr, DG), lambda i: (i, 3)), pl.BlockSpec((tr, DG), lambda i: (i, 4)),
                  full(ln_g), full(ln_b), full(w_s), full(bs_col)],
        out_specs=pl.BlockSpec((tr, DG), lambda i: (i, 0)),
        out_shape=jax.ShapeDtypeStruct((S, DG), F32),
        compiler_params=_params(("parallel",)),
    )(zm, zm, ln_g, ln_b, w_s, bs_col)


def _gmlp_bwd(dgm, zm, ln_g, ln_b, w_s, w_st, bs_col, tr):
    S = zm.shape[0]
    H = w_s.shape[0]
    DG = H * HEAD_DIM

    def body(dg_ref, zu_ref, zv_ref, g_ref, b_ref, w_ref, wt_ref, bs_ref,
             dzu_ref, dzv_ref, dw_ref, dbs_ref, dlg_ref, dlb_ref, dv_s):
        @pl.when(pl.program_id(0) == 0)
        def _():
            dw_ref[...] = jnp.zeros_like(dw_ref)
            dbs_ref[...] = jnp.zeros_like(dbs_ref)
            dlg_ref[...] = jnp.zeros_like(dlg_ref)
            dlb_ref[...] = jnp.zeros_like(dlb_ref)

        zu = zu_ref[...].astype(F32)
        zv = zv_ref[...].astype(F32)
        u = _gelu(zu)
        y, rstd = _ln_stats(_gelu(zv))
        v = (y * g_ref[...] + b_ref[...]).astype(BF16)
        dgm_blk = dg_ref[...]
        mask = _tril_mask()
        mask_t = _iota2((CHUNK, CHUNK), 0) <= _iota2((CHUNK, CHUNK), 1)
        for h in range(H):
            wc = jnp.where(mask, w_ref[h], 0.0).astype(BF16)
            wct = jnp.where(mask_t, wt_ref[h], 0.0).astype(BF16)
            cs = slice(h * HEAD_DIM, (h + 1) * HEAD_DIM)
            dw = jnp.zeros((CHUNK, CHUNK), F32)
            dbs = jnp.zeros((CHUNK, 1), F32)
            for c in range(tr // CHUNK):
                rs = slice(c * CHUNK, (c + 1) * CHUNK)
                vch = v[rs, cs]
                mix = _dot(wc, vch) + bs_ref[h]
                dg = dgm_blk[rs, cs]
                dzu_ref[rs, cs] = (dg * mix * _gelu_grad(zu[rs, cs])).astype(BF16)
                dmix = dg * u[rs, cs]
                dbs = dbs + jnp.sum(dmix, axis=-1, keepdims=True)
                dmix_b = dmix.astype(BF16)
                dw = dw + _dot(dmix_b, vch, NT)
                dv_s[rs, cs] = _dot(wct, dmix_b)
            dw_ref[h] += jnp.where(mask, dw, 0.0)
            dbs_ref[h] += dbs
        dv = dv_s[...]
        dlg_ref[...] += jnp.sum(dv * y, axis=0, keepdims=True)
        dlb_ref[...] += jnp.sum(dv, axis=0, keepdims=True)
        dy = dv * g_ref[...]
        dgv = rstd * (dy - jnp.mean(dy, axis=-1, keepdims=True) - y * jnp.mean(dy * y, axis=-1, keepdims=True))
        dzv_ref[...] = (dgv * _gelu_grad(zv)).astype(BF16)

    full = lambda a: pl.BlockSpec(a.shape, lambda i: (0,) * a.ndim)
    rows = pl.BlockSpec((tr, DG), lambda i: (i, 0))
    return pl.pallas_call(
        body, name="gmlp_bwd", grid=(S // tr,),
        in_specs=[rows, pl.BlockSpec((tr, DG), lambda i: (i, 3)), pl.BlockSpec((tr, DG), lambda i: (i, 4)),
                  full(ln_g), full(ln_b), full(w_s), full(w_st), full(bs_col)],
        out_specs=[rows, rows, full(w_s), full(bs_col), full(ln_g), full(ln_b)],
        out_shape=[jax.ShapeDtypeStruct((S, DG), BF16), jax.ShapeDtypeStruct((S, DG), BF16),
                   jax.ShapeDtypeStruct(w_s.shape, F32), jax.ShapeDtypeStruct(bs_col.shape, F32),
                   jax.ShapeDtypeStruct(ln_g.shape, F32), jax.ShapeDtypeStruct(ln_b.shape, F32)],
        scratch_shapes=[pltpu.VMEM((tr, DG), F32)],
        compiler_params=_params(("arbitrary",)),
    )(dgm, zm, zm, ln_g, ln_b, w_s, w_st, bs_col)


def _all_gather(name, blk):
    R, C = blk.shape

    def body(x_ref, out_ref, send_sems, recv_sems, local_sem):
        x, y, c = _me()
        me, sibling = (x, y, c), (x, y, 1 - c)
        chips = [(1 - x, y), (x, 1 - y), (1 - x, 1 - y)]

        def slab(px, py, pc):
            return out_ref.at[4 * px + 2 * py + pc]

        def copy(k, block, to, src=None):
            return pltpu.make_async_remote_copy(
                src_ref=slab(*block) if src is None else src, dst_ref=slab(*block),
                send_sem=send_sems.at[k], recv_sem=recv_sems.at[k], device_id=to, device_id_type=MESH)

        mine = pltpu.make_async_copy(x_ref, slab(*me), local_sem)
        mine.start()
        first = [copy(0, me, sibling, src=x_ref)]
        first += [copy(1 + n, me, (*chip, c), src=x_ref) for n, chip in enumerate(chips)]
        for cp in first:
            cp.start()
        passed = [copy(4 + n, (*chip, c), sibling) for n, chip in enumerate(chips)]
        for n, chip in enumerate(chips):
            copy(1 + n, (*chip, c), me).wait_recv()
            passed[n].start()
        copy(0, sibling, me).wait_recv()
        for n, chip in enumerate(chips):
            copy(4 + n, (*chip, 1 - c), me).wait_recv()
        for cp in first + passed:
            cp.wait_send()
        mine.wait()

    return pl.pallas_call(
        body, name=name, out_shape=jax.ShapeDtypeStruct((N_DEV, R, C), blk.dtype),
        in_specs=[_ANY], out_specs=_ANY,
        scratch_shapes=[pltpu.SemaphoreType.DMA((7,)), pltpu.SemaphoreType.DMA((7,)), pltpu.SemaphoreType.DMA(())],
    )(blk)


def _row_tile(R, C, itemsize=4, target_bytes=2 * 1024 * 1024):
    tr = R
    while tr % 2 == 0 and tr * C * itemsize > target_bytes and (tr // 2) % 16 == 0:
        tr //= 2
    return tr


def _rs_add1(name, g4, recv, c_idx):
    _, _, R, C = g4.shape
    tr = _row_tile(R, C)

    def body(c_ref, g_ref, r_ref, h_ref, hb_ref):
        h = g_ref[...] + r_ref[...].astype(F32)
        h_ref[...] = h
        hb_ref[...] = h.astype(BF16)

    blk = pl.BlockSpec((None, tr, C), lambda p, i, c_ref: (p, i, 0))
    return pl.pallas_call(
        body, name=name,
        grid_spec=pltpu.PrefetchScalarGridSpec(
            num_scalar_prefetch=1, grid=(4, R // tr),
            in_specs=[pl.BlockSpec((None, None, tr, C), lambda p, i, c_ref: (p, c_ref[0], i, 0)), blk],
            out_specs=[blk, blk]),
        out_shape=[jax.ShapeDtypeStruct((4, R, C), F32), jax.ShapeDtypeStruct((4, R, C), BF16)],
        compiler_params=_params(("parallel", "parallel")),
    )(c_idx, g4, recv)


def _rs_add2(name, h, recv, p_idx):
    _, R, C = h.shape
    tr = _row_tile(R, C)

    def body(p_ref, h_ref, r_ref, out_ref):
        out_ref[...] = ((h_ref[...] + r_ref[0].astype(F32)) + r_ref[1].astype(F32)) + r_ref[2].astype(F32)

    return pl.pallas_call(
        body, name=name,
        grid_spec=pltpu.PrefetchScalarGridSpec(
            num_scalar_prefetch=1, grid=(R // tr,),
            in_specs=[pl.BlockSpec((None, tr, C), lambda i, p_ref: (p_ref[0], i, 0)),
                      pl.BlockSpec((3, tr, C), lambda i, p_ref: (0, i, 0))],
            out_specs=pl.BlockSpec((tr, C), lambda i, p_ref: (i, 0))),
        out_shape=jax.ShapeDtypeStruct((R, C), F32),
        compiler_params=_params(("parallel",)),
    )(p_idx, h, recv)


def _sum8(name, g):
    _, R, C = g.shape

    def body(g_ref, out_ref):
        acc = g_ref[0]
        for d in range(1, N_DEV):
            acc = acc + g_ref[d]
        out_ref[...] = acc

    return pl.pallas_call(body, name=name, out_shape=jax.ShapeDtypeStruct((R, C), F32),
                          compiler_params=_params())(g)


def _adamw(name, w, g, m, v):
    R, C = w.shape
    tr = _row_tile(R, C, target_bytes=1024 * 1024)

    def fn(w, g, m, v):
        m = ADAM_B1 * m + (1.0 - ADAM_B1) * g
        v = ADAM_B2 * v + (1.0 - ADAM_B2) * (g * g)
        m_hat = m / (1.0 - ADAM_B1 ** ADAM_STEP)
        v_hat = v / (1.0 - ADAM_B2 ** ADAM_STEP)
        delta = -ADAM_LR * (m_hat / (jnp.sqrt(v_hat) + ADAM_EPS) + ADAM_WD * w)
        return (delta, m, v), ()

    return _row_call(name, fn, [w, g, m, v], [], [(C, F32)] * 3, [], tr)


def _pack(parts):
    rows = []
    for a in parts:
        flat = a.reshape(-1).astype(F32)
        n = -(-flat.shape[0] // LANES) * LANES
        rows.append(jnp.pad(flat, (0, n - flat.shape[0])).reshape(-1, LANES))
    packed = jnp.concatenate(rows, axis=0)
    pad = -packed.shape[0] % 8
    return jnp.pad(packed, ((0, pad), (0, 0)))


def _unpack(packed, shapes):
    out = []
    r = 0
    for shp in shapes:
        n = math.prod(shp)
        nr = -(-n // LANES)
        out.append(packed[r:r + nr].reshape(-1)[:n].reshape(shp))
        r += nr
    return out


def kernel(x, norm_mix_g, w_in, b_f, gmlp_ln_g, gmlp_ln_b, w_s, b_s, attn_out_g, gmlp_out_g, w_out, norm_ffn_g, w_ff1, w_ff2, norm_final_g, loss_target, m_norm_mix_g, m_w_in, m_b_f, m_gmlp_ln_g, m_gmlp_ln_b, m_w_s, m_b_s, m_attn_out_g, m_gmlp_out_g, m_w_out, m_norm_ffn_g, m_w_ff1, m_w_ff2, m_norm_final_g, v_norm_mix_g, v_w_in, v_b_f, v_gmlp_ln_g, v_gmlp_ln_b, v_w_s, v_b_s, v_attn_out_g, v_gmlp_out_g, v_w_out, v_norm_ffn_g, v_w_ff1, v_w_ff2, v_norm_final_g):
    S, D = x.shape[1], x.shape[2]
    H = b_f.shape[1]
    DA = H * HEAD_DIM
    DG = gmlp_ln_g.shape[1]
    DQKV = 3 * DA
    DMAIN = DQKV + 2 * DG
    DIN = DMAIN + H
    DFF = w_ff1.shape[2] * N_DEV
    w_in_cols = w_in.shape[2]
    assert DIN == w_in_cols * N_DEV and DA == DG and D == DA + DG

    T_ATT = min(T_ATT_MAX, S)
    TR = min(TR_MAX, S)

    x0 = x[0]
    tgt = loss_target[0]
    g_final = norm_final_g.reshape(1, D)

    w_in_all = _all_gather("ag_w_in", w_in[0].astype(BF16))
    w_in_full = jnp.concatenate([w_in_all[n] for n in range(N_DEV)], axis=1)
    w_main = jnp.concatenate([w_in_full[:, :DQKV], w_in_full[:, DQKV + H:]], axis=1)
    w_f = jnp.pad(w_in_full[:, DQKV:DQKV + H], ((0, 0), (0, LANES - H)))
    FB = DFF // N_DEV
    x_pos, y_pos, c_pos = _me()
    c_idx = jnp.reshape(c_pos, (1,)).astype(jnp.int32)
    p_idx = jnp.reshape(2 * x_pos + y_pos, (1,)).astype(jnp.int32)

    (h,), _ = _row_call("rms_mix", lambda xb, g: ((_rms_fwd(xb, g),), ()), [x0], [norm_mix_g], [(D, BF16)], [], TR)
    (zm,), ((w_out_part,),) = _mm_nn("in_proj", h, w_main, [BF16], 1024, 1024, 2048,
                                     jobs=[_job_gather_chips(w_out[0].astype(BF16))])
    (zf,) = _mm_nn("in_proj_f", h, w_f, [F32], 1024, LANES, 2048)
    bf_pad = jnp.pad(b_f, ((0, 0), (0, LANES - H)))
    f_row = _fgate_fwd(zf, bf_pad)
    NB = S // T_ATT
    f_col3 = f_row.reshape(H, S, 1)
    f_row3 = f_row.reshape(H, NB, 1, T_ATT)
    (attn, lse_col3), ((w_out_all,), (w_ff1_part,), (w_ff2_part,)) = _attn2_fwd(
        zm, f_col3, f_row3, T_ATT, jobs=[_job_gather_sibling(w_out_part), _job_gather_chips(w_ff1[0].astype(BF16)),
                                         _job_gather_chips(w_ff2[0].astype(BF16))])
    w_out_full = w_out_all.reshape(D, D)
    bs_col = b_s[0].reshape(H, CHUNK, 1)
    gm = _gmlp_fwd(zm, gmlp_ln_g, gmlp_ln_b, w_s[0], bs_col, TR)

    def merge_fn(a, g, ga, gg):
        return (jnp.concatenate([_rms_fwd(a, ga), _rms_fwd(g, gg)], axis=1),), ()
    (merged,), _ = _row_call("rms_merge", merge_fn, [attn, gm], [attn_out_g, gmlp_out_g], [(D, BF16)], [], TR)

    (x1,), ((w_ff1_all,), (w_ff2_all,)) = _mm_nn(
        "out_proj", merged, w_out_full, [F32], 1024, 1024, 2048, extras=[x0], epilogue=lambda acc, r: (acc + r,),
        jobs=[_job_gather_sibling(w_ff1_part), _job_gather_sibling(w_ff2_part)])
    w_ff2_full = w_ff2_all.reshape(DFF, D)
    (h2,), _ = _row_call("rms_ffn", lambda xb, g: ((_rms_fwd(xb, g),), ()), [x1], [norm_ffn_g], [(D, BF16)], [], TR)

    tm, tn, tk = min(1024, S), min(1024, FB), min(2048, D)
    o_spec = pl.BlockSpec((tm, tn), lambda i, j, k: (i, j))

    def relu_sq(acc):
        a = jnp.maximum(acc, 0.0)
        return a, a * a
    nj = FB // tn
    a_act, a_sq = _mm(
        "ff1", (S // tm, DFF // tn, D // tk), h2, pl.BlockSpec((tm, tk), lambda i, j, k: (i, k)),
        w_ff1_all, pl.BlockSpec((None, tk, tn), lambda i, j, k: (j // nj, k, j % nj)), NN, (tm, tn),
        [jax.ShapeDtypeStruct((S, DFF), BF16)] * 2, [o_spec] * 2, epilogue=relu_sq)
    (x2,) = _mm_nn("ff2", a_sq, w_ff2_full, [F32], 1024, 1024, 2048, extras=[x1], epilogue=lambda acc, r: (acc + r,))

    def head_fn(xb, t, g):
        rstd = lax.rsqrt(jnp.mean(xb * xb, axis=-1, keepdims=True) + EPS)
        xhat = xb * rstd
        err = xhat * g - t
        loss = 0.5 * jnp.sum(jnp.mean(err * err, axis=-1, keepdims=True), axis=0, keepdims=True)
        dy = err * (1.0 / D)
        dg = jnp.sum(dy * xhat, axis=0, keepdims=True)
        dxhat = dy * g
        dx = rstd * (dxhat - xhat * jnp.mean(dxhat * xhat, axis=-1, keepdims=True))
        return (dx, dx), (dg, jnp.broadcast_to(loss, (1, LANES)))
    (dx2, dx2_b), (dg_final, loss_part) = _row_call(
        "loss_head", head_fn, [x2, tgt], [g_final], [(D, F32), (D, BF16)], [D, LANES], TR)

    (da,) = _mm_nt("ff2_dx", dx2_b, w_ff2_full, [BF16], 1024, 1024, 2048, extras=[a_act],
                   epilogue=lambda acc, a: (2.0 * a.astype(F32) * acc,))
    dw_ff2, dw_ff2_b = _mm_tn("ff2_dw", a_sq, dx2_b, [F32, BF16], 1024, 1024, 1024)
    tm2, tk2 = min(1024, D), min(1024, S)
    dw1_spec = pl.BlockSpec((None, tm2, FB), lambda i, j, k: (j, i, 0))
    (dw_ff1, dw_ff1_b), ((r1_ff2,),) = _mm(
        "ff1_dw", (D // tm2, DFF // FB, S // tk2), h2, pl.BlockSpec((tk2, tm2), lambda i, j, k: (k, i)),
        da, pl.BlockSpec((tk2, FB), lambda i, j, k: (k, j)), TN, (tm2, FB),
        [jax.ShapeDtypeStruct((N_DEV, D, FB), F32), jax.ShapeDtypeStruct((N_DEV, D, FB), BF16)], [dw1_spec] * 2,
        epilogue=lambda acc: (acc, acc), jobs=[_job_scatter_sibling(dw_ff2_b.reshape(4, 2, FB, D))])
    h_ff2, hb_ff2 = _rs_add1("rs_add1_w_ff2", dw_ff2.reshape(4, 2, FB, D), r1_ff2, c_idx)
    tkb = min(1024, FB)
    nkb = FB // tkb
    tnb = min(1024, D)
    (dh2,), ((r2_ff2,), (r1_ff1,)) = _mm(
        "ff1_dx", (S // tm, D // tnb, DFF // tkb), da, pl.BlockSpec((tm, tkb), lambda i, j, k: (i, k)),
        w_ff1_all, pl.BlockSpec((None, tnb, tkb), lambda i, j, k: (k // nkb, j, k % nkb)), NT, (tm, tnb),
        [jax.ShapeDtypeStruct((S, D), F32)], [pl.BlockSpec((tm, tnb), lambda i, j, k: (i, j))],
        jobs=[_job_scatter_chips(hb_ff2), _job_scatter_sibling(dw_ff1_b.reshape(4, 2, D, FB))])
    g_w_ff2 = _rs_add2("rs_add2_w_ff2", h_ff2, r2_ff2, p_idx)
    h_ff1, hb_ff1 = _rs_add1("rs_add1_w_ff1", dw_ff1.reshape(4, 2, D, FB), r1_ff1, c_idx)

    def ffn_bwd_fn(dh, xb, dres, g):
        dx, dg = _rms_bwd(dh, xb, g)
        dx = dx + dres
        return (dx, dx), (dg,)
    (dx1, dx1_b), (dg_ffn,) = _row_call("rms_ffn_bwd", ffn_bwd_fn, [dh2, x1, dx2], [norm_ffn_g],
                                        [(D, F32), (D, BF16)], [D], TR)

    (dmerged,) = _mm_nt("out_proj_dx", dx1_b, w_out_full, [F32], 1024, 1024, 2048)
    dw_out, dw_out_b = _mm_tn("out_proj_dw", merged, dx1_b, [F32, BF16], 1024, 1024, 1024)

    def merge_bwd_fn(dm, a, g, ga, gg):
        da_, dga = _rms_bwd(dm[:, :DA], a, ga)
        dg_, dgg = _rms_bwd(dm[:, DA:], g, gg)
        return (da_, dg_), (dga, dgg)
    (dattn, dgm), (dg_attn, dg_gmlp) = _row_call(
        "rms_merge_bwd", merge_bwd_fn, [dmerged, attn, gm], [attn_out_g, gmlp_out_g], [(DA, F32), (DG, F32)], [DA, DG], TR)

    w_st = jnp.swapaxes(w_s[0], 1, 2)
    dzu, dzv, dw_s, dbs_col, dln_g, dln_b = _gmlp_bwd(dgm, zm, gmlp_ln_g, gmlp_ln_b, w_s[0], w_st, bs_col, TR)

    delta_row = _attn_delta(dattn, attn, TR)
    lse_row3 = lse_col3.reshape(H, NB, 1, T_ATT)
    (dq, ds_rowsum), ((r2_ff1,), (r1_out,)) = _attn2_bwd_dq(
        zm, dattn, f_col3, f_row3, lse_col3, delta_row.reshape(H, S, 1), T_ATT,
        jobs=[_job_scatter_chips(hb_ff1), _job_scatter_sibling(dw_out_b.reshape(4, 2, D // N_DEV, D))])
    g_w_ff1 = _rs_add2("rs_add2_w_ff1", h_ff1, r2_ff1, p_idx)
    h_out, hb_out = _rs_add1("rs_add1_w_out", dw_out.reshape(4, 2, D // N_DEV, D), r1_out, c_idx)
    (dk, dv, df_col3), ((r2_out,),) = _attn2_bwd_dkv(
        zm, dattn, f_col3, f_row3, lse_row3, delta_row.reshape(H, NB, 1, T_ATT),
        ds_rowsum.reshape(H, NB, 1, T_ATT), T_ATT,
        jobs=[_job_scatter_chips(hb_out)])
    g_w_out = _rs_add2("rs_add2_w_out", h_out, r2_out, p_idx)
    dzf, dbf = _fgate_bwd(df_col3.reshape(H, S), zf, bf_pad)

    dz_main = jnp.concatenate([dq, dk, dv, dzu, dzv], axis=1)
    (dw_main,) = _mm_tn("in_proj_dw", h, dz_main, [F32], 1024, 1024, 1024)
    (dw_f,) = _mm_tn("in_proj_f_dw", h, dzf, [F32], 2048, LANES, 1024)
    dw_in_full = jnp.concatenate([dw_main[:, :DQKV], dw_f[:, :H], dw_main[:, DQKV:]], axis=1)
    g4_in = jnp.stack([dw_in_full[:, n * w_in_cols:(n + 1) * w_in_cols] for n in range(N_DEV)])
    g4_in = g4_in.reshape(4, 2, D, w_in_cols)
    (dh_f,), ((r1_in,),) = _mm_nt("in_proj_f_dx", dzf, w_f, [F32], 1024, 1024, LANES,
                                  jobs=[_job_scatter_sibling(g4_in.astype(BF16))])
    h_in, hb_in = _rs_add1("rs_add1_w_in", g4_in, r1_in, c_idx)
    (dh,), ((r2_in,),) = _mm_nt("in_proj_dx", dz_main, w_main, [F32], 1024, 1024, 1024, extras=[dh_f],
                                epilogue=lambda acc, r: (acc + r,), jobs=[_job_scatter_chips(hb_in)])
    g_w_in = _rs_add2("rs_add2_w_in", h_in, r2_in, p_idx)

    def mix_bwd_fn(dhb, xb, dres, g):
        dx, dg = _rms_bwd(dhb, xb, g)
        return (dx + dres,), (dg,)
    (grad_x,), (dg_mix,) = _row_call("rms_mix_bwd", mix_bwd_fn, [dh, x0, dx1], [norm_mix_g], [(D, F32)], [D], TR)

    small_shapes = [norm_mix_g.shape, b_f.shape, gmlp_ln_g.shape, gmlp_ln_b.shape, w_s.shape, b_s.shape,
                    attn_out_g.shape, gmlp_out_g.shape, norm_ffn_g.shape, norm_final_g.shape]
    small_parts = [dg_mix, dbf[:, :H], dln_g, dln_b, dw_s, dbs_col, dg_attn, dg_gmlp, dg_ffn, dg_final]
    g_small = _sum8("small_sum", _all_gather("ag_small", _pack(small_parts)))
    w_small = _pack([norm_mix_g, b_f, gmlp_ln_g, gmlp_ln_b, w_s, b_s, attn_out_g, gmlp_out_g, norm_ffn_g, norm_final_g])
    m_small = _pack([m_norm_mix_g, m_b_f, m_gmlp_ln_g, m_gmlp_ln_b, m_w_s, m_b_s, m_attn_out_g, m_gmlp_out_g,
                     m_norm_ffn_g, m_norm_final_g])
    v_small = _pack([v_norm_mix_g, v_b_f, v_gmlp_ln_g, v_gmlp_ln_b, v_w_s, v_b_s, v_attn_out_g, v_gmlp_out_g,
                     v_norm_ffn_g, v_norm_final_g])
    (d_small, nm_small, nv_small), _ = _adamw("adamw_small", w_small, g_small, m_small, v_small)
    gs = _unpack(g_small, small_shapes)
    ds = _unpack(d_small, small_shapes)
    nms = _unpack(nm_small, small_shapes)
    nvs = _unpack(nv_small, small_shapes)

    big = {}
    for nm, w, g, m, v in (("w_in", w_in, g_w_in, m_w_in, v_w_in), ("w_out", w_out, g_w_out, m_w_out, v_w_out),
                           ("w_ff1", w_ff1, g_w_ff1, m_w_ff1, v_w_ff1), ("w_ff2", w_ff2, g_w_ff2, m_w_ff2, v_w_ff2)):
        (d_, m_, v_), _ = _adamw("adamw_" + nm, w[0], g, m[0], v[0])
        big[nm] = (g[None], d_[None], m_[None], v_[None])

    loss = lax.psum(loss_part[0, 0], ("x", "y", "c"))

    def leaves(n):
        sm = (gs, ds, nms, nvs)[n]
        return [sm[0], big["w_in"][n], sm[1], sm[2], sm[3], sm[4], sm[5], sm[6], sm[7], big["w_out"][n], sm[8],
                big["w_ff1"][n], big["w_ff2"][n], sm[9]]

    return (loss, grad_x[None], *leaves(0), *leaves(1), *leaves(2), *leaves(3))
```

```python
import functools
import math

import jax
import jax.numpy as jnp
from jax import lax
from jax.experimental import pallas as pl
from jax.experimental.pallas import tpu as pltpu

F32 = jnp.float32
BF16 = jnp.bfloat16
MESH = pl.DeviceIdType.MESH

HEAD_DIM = 128
CHUNK = 128
EPS = 1e-6
LANES = 128
N_DEV = 8

ADAM_LR = 0.001
ADAM_B1 = 0.9
ADAM_B2 = 0.999
ADAM_EPS = 1e-08
ADAM_WD = 0.01
ADAM_STEP = 10

VMEM_LIMIT_BYTES = 56 * 1024 * 1024
T_ATT_MAX = 1024
TR_MAX = 256

NN = ((1,), (0,))
NT = ((1,), (1,))
TN = ((0,), (0,))


def _params(sem=None):
    return pltpu.CompilerParams(dimension_semantics=sem, vmem_limit_bytes=VMEM_LIMIT_BYTES)


def _dot(a, b, contract=NN):
    return lax.dot_general(a, b, (contract, ((), ())), preferred_element_type=F32)


def _dot3(x, t):
    x1 = x.astype(BF16)
    r1 = x - x1.astype(F32)
    x2 = r1.astype(BF16)
    x3 = (r1 - x2.astype(F32)).astype(BF16)
    return _dot(x1, t) + _dot(x2, t) + _dot(x3, t)


def _iota2(shape, dim):
    return lax.broadcasted_iota(jnp.int32, shape, dim)


def _row_call(name, fn, row_ins, bcast_ins, row_outs, acc_outs, tr):
    S = row_ins[0].shape[0]
    assert S % tr == 0
    n_ri, n_bi, n_ro, n_ao = len(row_ins), len(bcast_ins), len(row_outs), len(acc_outs)

    def body(*refs):
        ins = [r[...] for r in refs[:n_ri + n_bi]]
        ro_refs = refs[n_ri + n_bi:n_ri + n_bi + n_ro]
        ao_refs = refs[n_ri + n_bi + n_ro:]
        ro, ao = fn(*ins)
        for r, v in zip(ro_refs, ro):
            r[...] = v.astype(r.dtype)
        if n_ao:
            @pl.when(pl.program_id(0) == 0)
            def _():
                for r in ao_refs:
                    r[...] = jnp.zeros_like(r)
            for r, v in zip(ao_refs, ao):
                r[...] += v

    in_specs = [pl.BlockSpec((tr, a.shape[1]), lambda i: (i, 0)) for a in row_ins]
    in_specs += [pl.BlockSpec(a.shape, lambda i: (0, 0)) for a in bcast_ins]
    out_specs = [pl.BlockSpec((tr, d), lambda i: (i, 0)) for d, _ in row_outs]
    out_specs += [pl.BlockSpec((1, d), lambda i: (0, 0)) for d in acc_outs]
    out_shape = [jax.ShapeDtypeStruct((S, d), dt) for d, dt in row_outs]
    out_shape += [jax.ShapeDtypeStruct((1, d), F32) for d in acc_outs]
    outs = pl.pallas_call(
        body, name=name, grid=(S // tr,), in_specs=in_specs, out_specs=out_specs, out_shape=out_shape,
        compiler_params=_params(("arbitrary",) if n_ao else ("parallel",)),
    )(*row_ins, *bcast_ins)
    return outs[:n_ro], outs[n_ro:]


def _rms_fwd(x, g):
    rstd = lax.rsqrt(jnp.mean(x * x, axis=-1, keepdims=True) + EPS)
    return x * rstd * g


def _rms_bwd(dy, x, g):
    rstd = lax.rsqrt(jnp.mean(x * x, axis=-1, keepdims=True) + EPS)
    xhat = x * rstd
    dg = jnp.sum(dy * xhat, axis=0, keepdims=True)
    dxhat = dy * g
    dx = rstd * (dxhat - xhat * jnp.mean(dxhat * xhat, axis=-1, keepdims=True))
    return dx, dg


_GELU_C = math.sqrt(2.0 / math.pi)


def _gelu(x):
    return 0.5 * x * (1.0 + jnp.tanh(_GELU_C * (x + 0.044715 * (x * x * x))))


def _gelu_grad(x):
    t = jnp.tanh(_GELU_C * (x + 0.044715 * (x * x * x)))
    return 0.5 * (1.0 + t) + 0.5 * x * (1.0 - t * t) * (_GELU_C * (1.0 + 3.0 * 0.044715 * (x * x)))


def _me():
    return lax.axis_index("x"), lax.axis_index("y"), lax.axis_index("c")


def _other_chips(x, y):
    return [(1 - x, y), (x, 1 - y), (1 - x, 1 - y)]


_ANY = pl.BlockSpec(memory_space=pl.ANY)


class _Job:
    def __init__(self, ins, outs, n_sems, make, aliases=None):
        self.ins, self.outs, self.n_sems, self.make, self.aliases = ins, outs, n_sems, make, aliases or {}


def _job_gather_chips(blk, part=(0, 1), into=None):
    R, C = blk.shape
    nr = R // part[1]
    rows = pl.ds(part[0] * nr, nr)

    def make(ins, outs, send_sems, recv_sems, base):
        x_ref, (out_ref,) = ins[0], outs
        x, y, c = _me()
        mine = 4 * x + 2 * y + c
        targets = [(x, y, 1 - c)] + [(cx, cy, c) for cx, cy in _other_chips(x, y)]

        def copy(k, slab, to):
            return pltpu.make_async_remote_copy(
                src_ref=x_ref.at[rows, :], dst_ref=out_ref.at[slab, rows, :], send_sem=send_sems.at[base + k],
                recv_sem=recv_sems.at[base + k], device_id=to, device_id_type=MESH)

        starts = [copy(k, mine, to) for k, to in enumerate(targets)]
        arrivals = [copy(k, 4 * tx + 2 * ty + tc, (tx, ty, tc)) for k, (tx, ty, tc) in enumerate(targets)]
        local = [pltpu.make_async_copy(x_ref.at[rows, :], out_ref.at[mine, rows, :], send_sems.at[base + 4])]
        return starts, arrivals, local

    out = jax.ShapeDtypeStruct((N_DEV, R, C), blk.dtype)
    if into is None:
        return _Job([blk], [out], 5, make)
    return _Job([blk, into], [out], 5, make, aliases={1: 0})


def _job_gather_sibling(part):
    def make(ins, outs, send_sems, recv_sems, base):
        (out_ref,) = outs
        x, y, c = _me()

        def copy(k, slab):
            return pltpu.make_async_remote_copy(
                src_ref=out_ref.at[slab], dst_ref=out_ref.at[slab], send_sem=send_sems.at[base + k],
                recv_sem=recv_sems.at[base + k], device_id=(x, y, 1 - c), device_id_type=MESH)

        chips = _other_chips(x, y)
        starts = [copy(k, 4 * cx + 2 * cy + c) for k, (cx, cy) in enumerate(chips)]
        arrivals = [copy(k, 4 * cx + 2 * cy + (1 - c)) for k, (cx, cy) in enumerate(chips)]
        return starts, arrivals, []

    return _Job([part], [jax.ShapeDtypeStruct(part.shape, part.dtype)], 3, make, aliases={0: 0})


def _job_scatter_sibling(gb):
    _, _, R, C = gb.shape

    def make(ins, outs, send_sems, recv_sems, base):
        (g_ref,), (recv_ref,) = ins, outs
        x, y, c = _me()
        copies = [pltpu.make_async_remote_copy(
            src_ref=g_ref.at[p, 1 - c], dst_ref=recv_ref.at[p], send_sem=send_sems.at[base + p],
            recv_sem=recv_sems.at[base + p], device_id=(x, y, 1 - c), device_id_type=MESH) for p in range(4)]
        return copies, copies, []

    return _Job([gb], [jax.ShapeDtypeStruct((4, R, C), gb.dtype)], 4, make)


def _job_scatter_chips(hb):
    _, R, C = hb.shape

    def make(ins, outs, send_sems, recv_sems, base):
        (h_ref,), (recv_ref,) = ins, outs
        x, y, c = _me()
        copies = [pltpu.make_async_remote_copy(
            src_ref=h_ref.at[2 * cx + cy], dst_ref=recv_ref.at[n], send_sem=send_sems.at[base + n],
            recv_sem=recv_sems.at[base + n], device_id=(cx, cy, c), device_id_type=MESH)
            for n, (cx, cy) in enumerate(_other_chips(x, y))]
        return copies, copies, []

    return _Job([hb], [jax.ShapeDtypeStruct((3, R, C), hb.dtype)], 3, make)


def _carry_call(body, *, name, grid, in_specs, out_specs, out_shape, scratch_shapes, semantics, args, jobs=()):
    jobs = list(jobs)
    n_in, n_out, n_scr = len(in_specs), len(out_specs), len(scratch_shapes)
    j_ins = [a for j in jobs for a in j.ins]
    j_outs = [o for j in jobs for o in j.outs]
    n_sems = sum(j.n_sems for j in jobs)
    aliases = {}
    i0, o0 = n_in, n_out
    for j in jobs:
        for a, b in j.aliases.items():
            aliases[i0 + a] = o0 + b
        i0 += len(j.ins)
        o0 += len(j.outs)

    def full_body(*refs):
        ins = refs[:n_in]
        jin = refs[n_in:n_in + len(j_ins)]
        outs = refs[n_in + len(j_ins):n_in + len(j_ins) + n_out]
        jout = refs[n_in + len(j_ins) + n_out:n_in + len(j_ins) + n_out + len(j_outs)]
        scr = refs[n_in + len(j_ins) + n_out + len(j_outs):]
        if jobs:
            send_sems, recv_sems = scr[n_scr], scr[n_scr + 1]
            starts, arrivals, local = [], [], []
            base = i0 = o0 = 0
            for j in jobs:
                s, a, l = j.make(jin[i0:i0 + len(j.ins)], jout[o0:o0 + len(j.outs)], send_sems, recv_sems, base)
                starts += s
                arrivals += a
                local += l
                base += j.n_sems
                i0 += len(j.ins)
                o0 += len(j.outs)
            pids = [pl.program_id(d) for d in range(len(grid))]
            first = functools.reduce(jnp.logical_and, [p == 0 for p in pids])
            last = functools.reduce(jnp.logical_and, [p == n - 1 for p, n in zip(pids, grid)])

            @pl.when(first)
            def _():
                for cp in local + starts:
                    cp.start()

        body(*ins, *outs, *scr[:n_scr])

        if jobs:
            @pl.when(last)
            def _():
                for cp in arrivals:
                    cp.wait_recv()
                for cp in starts:
                    cp.wait_send()
                for cp in local:
                    cp.wait()

    sems = [pltpu.SemaphoreType.DMA((n_sems,)), pltpu.SemaphoreType.DMA((n_sems,))] if jobs else []
    res = pl.pallas_call(
        full_body, name=name, grid=grid,
        in_specs=list(in_specs) + [_ANY] * len(j_ins),
        out_specs=list(out_specs) + [_ANY] * len(j_outs),
        out_shape=list(out_shape) + j_outs,
        scratch_shapes=list(scratch_shapes) + sems,
        input_output_aliases=aliases,
        compiler_params=_params(("arbitrary",) * len(grid) if jobs else semantics),
    )(*args, *j_ins)
    body_res, job_res = res[:n_out], res[n_out:]
    per_job = []
    for j in jobs:
        per_job.append(job_res[:len(j.outs)])
        job_res = job_res[len(j.outs):]
    return body_res, per_job


def _mm(name, grid, a, a_spec, b, b_spec, contract, acc_shape, out_shape, out_specs, extras=(), epilogue=None, jobs=()):
    nk = grid[2]
    n_e = len(extras)
    n_o = len(out_shape)
    if epilogue is None:
        epilogue = lambda acc: (acc,)

    def body(a_ref, b_ref, *rest):
        e_refs = rest[:n_e]
        o_refs = rest[n_e:n_e + n_o]

        def finish(total):
            res = epilogue(total, *[r[...] for r in e_refs])
            for o, r in zip(o_refs, res):
                o[...] = r.astype(o.dtype)

        if nk == 1:
            finish(_dot(a_ref[...], b_ref[...], contract))
            return
        acc = rest[n_e + n_o]
        k = pl.program_id(2)

        @pl.when(k == 0)
        def _():
            acc[...] = _dot(a_ref[...], b_ref[...], contract)

        @pl.when(jnp.logical_and(k > 0, k < nk - 1))
        def _():
            acc[...] += _dot(a_ref[...], b_ref[...], contract)

        @pl.when(k == nk - 1)
        def _():
            finish(acc[...] + _dot(a_ref[...], b_ref[...], contract))

    outs, job_res = _carry_call(
        body, name=name, grid=grid, in_specs=[a_spec, b_spec] + [s for _, s in extras],
        out_specs=list(out_specs), out_shape=list(out_shape),
        scratch_shapes=[pltpu.VMEM(acc_shape, F32)] if nk > 1 else [],
        semantics=("parallel", "parallel", "arbitrary"), args=[a, b] + [e for e, _ in extras], jobs=jobs)
    return (outs, job_res) if jobs else outs


def _mm_nn(name, a, b, out_dtypes, tm, tn, tk, extras=(), epilogue=None, jobs=()):
    M, K = a.shape
    N = b.shape[1]
    tm, tn, tk = min(tm, M), min(tn, N), min(tk, K)
    o_spec = pl.BlockSpec((tm, tn), lambda i, j, k: (i, j))
    return _mm(name, (M // tm, N // tn, K // tk),
               a, pl.BlockSpec((tm, tk), lambda i, j, k: (i, k)),
               b, pl.BlockSpec((tk, tn), lambda i, j, k: (k, j)), NN, (tm, tn),
               [jax.ShapeDtypeStruct((M, N), dt) for dt in out_dtypes], [o_spec] * len(out_dtypes),
               [(e, o_spec) for e in extras], epilogue, jobs)


def _mm_nt(name, a, b, out_dtypes, tm, tn, tk, extras=(), epilogue=None, jobs=()):
    M, K = a.shape
    N = b.shape[0]
    tm, tn, tk = min(tm, M), min(tn, N), min(tk, K)
    o_spec = pl.BlockSpec((tm, tn), lambda i, j, k: (i, j))
    return _mm(name, (M // tm, N // tn, K // tk),
               a, pl.BlockSpec((tm, tk), lambda i, j, k: (i, k)),
               b, pl.BlockSpec((tn, tk), lambda i, j, k: (j, k)), NT, (tm, tn),
               [jax.ShapeDtypeStruct((M, N), dt) for dt in out_dtypes], [o_spec] * len(out_dtypes),
               [(e, o_spec) for e in extras], epilogue, jobs)


def _mm_tn(name, a, b, out_dtypes, tm, tn, tk, jobs=()):
    K, M = a.shape
    N = b.shape[1]
    tm, tn, tk = min(tm, M), min(tn, N), min(tk, K)
    o_spec = pl.BlockSpec((tm, tn), lambda i, j, k: (i, j))
    return _mm(name, (M // tm, N // tn, K // tk),
               a, pl.BlockSpec((tk, tm), lambda i, j, k: (k, i)),
               b, pl.BlockSpec((tk, tn), lambda i, j, k: (k, j)), TN, (tm, tn),
               [jax.ShapeDtypeStruct((M, N), dt) for dt in out_dtypes], [o_spec] * len(out_dtypes),
               epilogue=lambda acc: (acc,) * len(out_dtypes), jobs=jobs)


def _fgate_fwd(zf, bf):
    S = zf.shape[0]
    nc = S // CHUNK

    def body(zf_ref, bf_ref, f_ref):
        upper = (_iota2((CHUNK, CHUNK), 0) <= _iota2((CHUNK, CHUNK), 1)).astype(BF16)
        carry = jnp.zeros((8, 1), F32)
        for c in range(nc):
            t = zf_ref[c * CHUNK:(c + 1) * CHUNK, :] + bf_ref[...]
            lf = jnp.minimum(t, 0.0) - jnp.log(1.0 + jnp.exp(-jnp.abs(t)))
            lf_rows = lf.T[0:8, :]
            f_ref[:, c * CHUNK:(c + 1) * CHUNK] = (_dot3(lf_rows, upper) + carry) * LOG2E
            carry = carry + jnp.sum(lf_rows, axis=-1, keepdims=True)

    return pl.pallas_call(
        body, name="fgate_fwd", out_shape=jax.ShapeDtypeStruct((8, S), F32),
        compiler_params=_params(),
    )(zf, bf)


def _fgate_bwd(df, zf, bf):
    S = zf.shape[0]
    nc = S // CHUNK

    def body(df_ref, zf_ref, bf_ref, dzf_ref, dbf_ref):
        lower = (_iota2((CHUNK, CHUNK), 0) >= _iota2((CHUNK, CHUNK), 1)).astype(BF16)
        carry = jnp.zeros((8, 1), F32)
        dbf = jnp.zeros((1, LANES), F32)
        for c in reversed(range(nc)):
            sl = slice(c * CHUNK, (c + 1) * CHUNK)
            df = df_ref[:, sl]
            r = _dot3(df, lower) + carry
            carry = carry + jnp.sum(df, axis=-1, keepdims=True)
            r_cols = jnp.concatenate([r, jnp.zeros((CHUNK - 8, CHUNK), F32)], axis=0).T
            t = zf_ref[sl, :] + bf_ref[...]
            dz = r_cols * (1.0 / (1.0 + jnp.exp(t)))
            dzf_ref[sl, :] = dz.astype(BF16)
            dbf = dbf + jnp.sum(dz, axis=0, keepdims=True)
        dbf_ref[...] = dbf

    return pl.pallas_call(
        body, name="fgate_bwd",
        out_shape=[jax.ShapeDtypeStruct((S, LANES), BF16), jax.ShapeDtypeStruct((1, LANES), F32)],
        compiler_params=_params(),
    )(df, zf, bf)


_NEG = -1e30
LOG2E = 1.4426950408889634
N_SPLIT = 8
N_SPLIT_DIAG = 2
DIAG_STEP = 1024


def _attn_consts(T):
    rows, cols = _iota2((T, T), 0), _iota2((T, T), 1)
    return cols <= rows, rows <= cols


def _attn2_fwd(zm, f2col, f2row, T, jobs=()):
    S = zm.shape[0]
    H = f2col.shape[0]
    nb = S // T
    c2 = LOG2E / math.sqrt(HEAD_DIM)

    def body(q_ref, k_ref, v_ref, fq_ref, fk_ref, o_ref, lse_ref, vaug_s):
        i = pl.program_id(1)

        @pl.when(i == 0)
        def _():
            vaug_s[:, :HEAD_DIM] = v_ref[...]
            vaug_s[:, HEAD_DIM:] = jnp.ones((S, HEAD_DIM), BF16)

        keep = _attn_consts(T)[0]
        TH = T // N_SPLIT

        def block(j, diagonal, state):
            r0 = pl.multiple_of(j * T, T)
            fk = fk_ref[j]
            new = []
            for g, (m_old, acc) in enumerate(state):
                rows = slice(g * TH, (g + 1) * TH)
                nk = min(T, -(-(g + 1) * TH // DIAG_STEP) * DIAG_STEP) if diagonal else T
                s = _dot(q_ref[rows, :], k_ref[pl.ds(r0, nk), :], NT) * c2 + (fq_ref[rows, :] - fk[:, :nk])
                if diagonal:
                    s = jnp.where(keep[rows, :nk], s, _NEG)
                m_new = jnp.maximum(m_old, jnp.max(s, axis=-1, keepdims=True))
                p = jnp.exp2(s - m_new).astype(BF16)
                new.append((m_new, jnp.exp2(m_old - m_new) * acc + _dot(p, vaug_s[pl.ds(r0, nk), :])))
            return tuple(new)

        init = tuple((jnp.full((TH, 1), _NEG, F32), jnp.zeros((TH, 2 * HEAD_DIM), F32)) for _ in range(N_SPLIT))
        state = lax.fori_loop(0, i, lambda j, st: block(j, False, st), init)
        state = block(i, True, state)
        for g, (m, acc) in enumerate(state):
            rows = slice(g * TH, (g + 1) * TH)
            o_ref[rows, :] = acc[:, :HEAD_DIM] / acc[:, HEAD_DIM:]
            lse_ref[rows, :] = m + jnp.log2(acc[:, HEAD_DIM:HEAD_DIM + 1])

    nh = H
    return _carry_call(
        body, name="attn_fwd", grid=(H, nb), jobs=jobs, args=[zm, zm, zm, f2col, f2row],
        semantics=("arbitrary", "arbitrary"),
        in_specs=[
            pl.BlockSpec((T, HEAD_DIM), lambda h, i: (i, h)),
            pl.BlockSpec((S, HEAD_DIM), lambda h, i: (0, nh + h)),
            pl.BlockSpec((S, HEAD_DIM), lambda h, i: (0, 2 * nh + h)),
            pl.BlockSpec((None, T, 1), lambda h, i: (h, i, 0)),
            pl.BlockSpec((None, nb, 1, T), lambda h, i: (h, 0, 0, 0)),
        ],
        out_specs=[pl.BlockSpec((T, HEAD_DIM), lambda h, i: (i, h)), pl.BlockSpec((None, T, 1), lambda h, i: (h, i, 0))],
        out_shape=[jax.ShapeDtypeStruct((S, H * HEAD_DIM), F32), jax.ShapeDtypeStruct((H, S, 1), F32)],
        scratch_shapes=[pltpu.VMEM((S, 2 * HEAD_DIM), BF16)],
    )


def _attn2_bwd_dq(zm, dattn, f2col, f2row, lse2_col, delta_col, T, jobs=()):
    S = zm.shape[0]
    H = f2col.shape[0]
    nb = S // T
    scale = 1.0 / math.sqrt(HEAD_DIM)
    c2 = LOG2E * scale

    def body(q_ref, k_ref, v_ref, do_ref, fq_ref, fk_ref, lse_ref, dl_ref, dq_ref, rs_ref, bias_s, do_s):
        i = pl.program_id(1)
        keep = _attn_consts(T)[0]
        TH = T // N_SPLIT_DIAG
        bias_s[...] = fq_ref[...] - lse_ref[...]
        do_s[...] = do_ref[...].astype(BF16)

        def part(rows, j, nk, state, masked):
            acc, rs = state
            r0 = pl.multiple_of(j * T, T)
            kb = k_ref[pl.ds(r0, nk), :]
            s = _dot(q_ref[rows, :], kb, NT) * c2 + (bias_s[rows, :] - fk_ref[j][:, :nk])
            if masked:
                s = jnp.where(keep[rows, :nk], s, _NEG)
            ds = jnp.exp2(s) * (_dot(do_s[rows, :], v_ref[pl.ds(r0, nk), :], NT) - dl_ref[rows, :])
            return acc + _dot(ds.astype(BF16), kb), rs + jnp.sum(ds, axis=-1, keepdims=True)

        def step(j, state):
            return part(slice(0, T), j, T, state, False)

        acc, rs = lax.fori_loop(0, i, step, (jnp.zeros((T, HEAD_DIM), F32), jnp.zeros((T, 1), F32)))
        for g in range(N_SPLIT_DIAG):
            rows = slice(g * TH, (g + 1) * TH)
            acc_g, rs_g = part(rows, i, (g + 1) * TH, (acc[rows, :], rs[rows, :]), True)
            dq_ref[rows, :] = (acc_g * scale).astype(BF16)
            rs_ref[rows, :] = rs_g

    nh = H
    col = pl.BlockSpec((None, T, 1), lambda h, i: (h, i, 0))
    blk = pl.BlockSpec((T, HEAD_DIM), lambda h, i: (i, h))
    return _carry_call(
        body, name="attn_bwd_dq", grid=(H, nb), jobs=jobs,
        args=[zm, zm, zm, dattn, f2col, f2row, lse2_col, delta_col], semantics=("arbitrary", "arbitrary"),
        in_specs=[
            blk,
            pl.BlockSpec((S, HEAD_DIM), lambda h, i: (0, nh + h)),
            pl.BlockSpec((S, HEAD_DIM), lambda h, i: (0, 2 * nh + h)),
            blk, col,
            pl.BlockSpec((None, nb, 1, T), lambda h, i: (h, 0, 0, 0)),
            col, col,
        ],
        out_specs=[blk, col],
        out_shape=[jax.ShapeDtypeStruct((S, H * HEAD_DIM), BF16), jax.ShapeDtypeStruct((H, S, 1), F32)],
        scratch_shapes=[pltpu.VMEM((T, 1), F32), pltpu.VMEM((T, HEAD_DIM), BF16)],
    )


def _attn2_bwd_dkv(zm, dattn, f2col, f2row, lse2_row, delta_row, rowsum_row, T, jobs=()):
    S = zm.shape[0]
    H = f2col.shape[0]
    nb = S // T
    scale = 1.0 / math.sqrt(HEAD_DIM)
    c2 = LOG2E * scale

    def body(q_ref, k_ref, v_ref, do_ref, fk_ref, fq_ref, lse_ref, dl_ref, rs_ref, dk_ref, dv_ref, df_ref):
        j = pl.program_id(1)
        keep = _attn_consts(T)[1]
        TH = T // N_SPLIT_DIAG

        def part(rows, i, c0, state, masked):
            dk, dv, df = state
            r0 = pl.multiple_of(i * T + c0, TH)
            qb = q_ref[pl.ds(r0, T - c0), :]
            do = do_ref[pl.ds(r0, T - c0), :].astype(BF16)
            bias = (fq_ref[i] - lse_ref[i])[:, c0:]
            dl = (dl_ref[i] + rs_ref[i])[:, c0:]
            st = _dot(k_ref[rows, :], qb, NT) * c2 + (bias - fk_ref[rows, :])
            if masked:
                st = jnp.where(keep[rows, c0:], st, _NEG)
            pt = jnp.exp2(st)
            dst = pt * (_dot(v_ref[rows, :], do, NT) - dl)
            return (dk + _dot(dst.astype(BF16), qb), dv + _dot(pt.astype(BF16), do),
                    df - jnp.sum(dst, axis=-1, keepdims=True))

        groups = []
        for g in range(N_SPLIT_DIAG):
            zero = (jnp.zeros((TH, HEAD_DIM), F32), jnp.zeros((TH, HEAD_DIM), F32), jnp.zeros((TH, 1), F32))
            groups.append(part(slice(g * TH, (g + 1) * TH), j, g * TH, zero, True))
        state = tuple(jnp.concatenate([grp[n] for grp in groups], axis=0) for n in range(3))
        dk, dv, df = lax.fori_loop(j + 1, nb, lambda i, st: part(slice(0, T), i, 0, st, False), state)
        dk_ref[...] = (dk * scale).astype(BF16)
        dv_ref[...] = dv.astype(BF16)
        df_ref[...] = df

    nh = H
    row = pl.BlockSpec((None, nb, 1, T), lambda h, j: (h, 0, 0, 0))
    whole = pl.BlockSpec((S, HEAD_DIM), lambda h, j: (0, h))
    kv_out = pl.BlockSpec((T, HEAD_DIM), lambda h, j: (j, h))
    col = pl.BlockSpec((None, T, 1), lambda h, j: (h, j, 0))
    return _carry_call(
        body, name="attn_bwd_dkv", grid=(H, nb), jobs=jobs,
        args=[zm, zm, zm, dattn, f2col, f2row, lse2_row, delta_row, rowsum_row],
        semantics=("arbitrary", "arbitrary"),
        in_specs=[
            whole,
            pl.BlockSpec((T, HEAD_DIM), lambda h, j: (j, nh + h)),
            pl.BlockSpec((T, HEAD_DIM), lambda h, j: (j, 2 * nh + h)),
            whole, col, row, row, row, row,
        ],
        out_specs=[kv_out, kv_out, col],
        out_shape=[jax.ShapeDtypeStruct((S, H * HEAD_DIM), BF16), jax.ShapeDtypeStruct((S, H * HEAD_DIM), BF16),
                   jax.ShapeDtypeStruct((H, S, 1), F32)],
        scratch_shapes=[],
    )


def _attn_fwd(zm, fcol, frow, T, jobs=()):
    S = zm.shape[0]
    H = fcol.shape[0]
    nb = S // T
    scale = 1.0 / math.sqrt(HEAD_DIM)

    def body(q_ref, k_ref, v_ref, fq_ref, fk_ref, o_ref, lse_ref, m_s, l_s, acc_s):
        i = pl.program_id(1)
        j = pl.program_id(2)

        @pl.when(j == 0)
        def _():
            m_s[...] = jnp.full_like(m_s, _NEG)
            l_s[...] = jnp.zeros_like(l_s)
            acc_s[...] = jnp.zeros_like(acc_s)

        @pl.when(j <= i)
        def _():
            s = _dot(q_ref[...], k_ref[...], NT) * scale + (fq_ref[...] - fk_ref[...])
            keep = (_iota2((T, T), 1) + j * T) <= (_iota2((T, T), 0) + i * T)
            s = jnp.where(keep, s, _NEG)
            m_new = jnp.maximum(m_s[...], jnp.max(s, axis=-1, keepdims=True))
            alpha = jnp.exp(m_s[...] - m_new)
            p = jnp.exp(s - m_new)
            l_s[...] = alpha * l_s[...] + jnp.sum(p, axis=-1, keepdims=True)
            acc_s[...] = alpha * acc_s[...] + _dot(p.astype(BF16), v_ref[...])
            m_s[...] = m_new

        @pl.when(j == nb - 1)
        def _():
            o_ref[...] = acc_s[...] / l_s[...]
            lse_ref[...] = m_s[...] + jnp.log(l_s[...])

    nh = H
    return _carry_call(
        body, name="attn_fwd", grid=(H, nb, nb), jobs=jobs, args=[zm, zm, zm, fcol, frow],
        semantics=("parallel", "parallel", "arbitrary"),
        in_specs=[
            pl.BlockSpec((T, HEAD_DIM), lambda h, i, j: (i, h)),
            pl.BlockSpec((T, HEAD_DIM), lambda h, i, j: (jnp.minimum(j, i), nh + h)),
            pl.BlockSpec((T, HEAD_DIM), lambda h, i, j: (jnp.minimum(j, i), 2 * nh + h)),
            pl.BlockSpec((None, T, 1), lambda h, i, j: (h, i, 0)),
            pl.BlockSpec((None, 1, T), lambda h, i, j: (h, 0, jnp.minimum(j, i))),
        ],
        out_specs=[
            pl.BlockSpec((T, HEAD_DIM), lambda h, i, j: (i, h)),
            pl.BlockSpec((None, T, 1), lambda h, i, j: (h, i, 0)),
        ],
        out_shape=[jax.ShapeDtypeStruct((S, H * HEAD_DIM), F32), jax.ShapeDtypeStruct((H, S, 1), F32)],
        scratch_shapes=[pltpu.VMEM((T, 1), F32), pltpu.VMEM((T, 1), F32), pltpu.VMEM((T, HEAD_DIM), F32)],
    )


def _attn_delta(dattn, attn, tr):
    S, DA = attn.shape
    H = DA // HEAD_DIM

    def body(do_ref, o_ref, out_ref):
        lo = _iota2((DA, LANES), 1) * HEAD_DIM
        sel = ((_iota2((DA, LANES), 0) >= lo) & (_iota2((DA, LANES), 0) < lo + HEAD_DIM)).astype(BF16)
        d = _dot3(do_ref[...] * o_ref[...], sel)
        for c in range(tr // CHUNK):
            out_ref[:, c * CHUNK:(c + 1) * CHUNK] = d[c * CHUNK:(c + 1) * CHUNK, :].T[0:H, :]

    return pl.pallas_call(
        body, name="attn_delta", grid=(S // tr,),
        in_specs=[pl.BlockSpec((tr, DA), lambda i: (i, 0))] * 2,
        out_specs=pl.BlockSpec((H, tr), lambda i: (0, i)),
        out_shape=jax.ShapeDtypeStruct((H, S), F32),
        compiler_params=_params(("parallel",)),
    )(dattn, attn)


def _attn_bwd_dq(zm, dattn, fcol, frow, lse_col, delta_col, T, jobs=()):
    S = zm.shape[0]
    H = fcol.shape[0]
    nb = S // T
    scale = 1.0 / math.sqrt(HEAD_DIM)

    def body(q_ref, k_ref, v_ref, do_ref, fq_ref, fk_ref, lse_ref, dl_ref, dq_ref, rs_ref, acc_s, rs_s):
        i = pl.program_id(1)
        j = pl.program_id(2)

        @pl.when(j == 0)
        def _():
            acc_s[...] = jnp.zeros_like(acc_s)
            rs_s[...] = jnp.zeros_like(rs_s)

        @pl.when(j <= i)
        def _():
            s = _dot(q_ref[...], k_ref[...], NT) * scale + (fq_ref[...] - fk_ref[...])
            keep = (_iota2((T, T), 1) + j * T) <= (_iota2((T, T), 0) + i * T)
            p = jnp.exp(jnp.where(keep, s - lse_ref[...], _NEG))
            dp = _dot(do_ref[...].astype(BF16), v_ref[...], NT)
            ds = p * (dp - dl_ref[...])
            acc_s[...] += _dot(ds.astype(BF16), k_ref[...])
            rs_s[...] += jnp.sum(ds, axis=-1, keepdims=True)

        @pl.when(j == nb - 1)
        def _():
            dq_ref[...] = (acc_s[...] * scale).astype(BF16)
            rs_ref[...] = rs_s[...]

    nh = H
    col = pl.BlockSpec((None, T, 1), lambda h, i, j: (h, i, 0))
    return _carry_call(
        body, name="attn_bwd_dq", grid=(H, nb, nb), jobs=jobs,
        args=[zm, zm, zm, dattn, fcol, frow, lse_col, delta_col], semantics=("parallel", "parallel", "arbitrary"),
        in_specs=[
            pl.BlockSpec((T, HEAD_DIM), lambda h, i, j: (i, h)),
            pl.BlockSpec((T, HEAD_DIM), lambda h, i, j: (jnp.minimum(j, i), nh + h)),
            pl.BlockSpec((T, HEAD_DIM), lambda h, i, j: (jnp.minimum(j, i), 2 * nh + h)),
            pl.BlockSpec((T, HEAD_DIM), lambda h, i, j: (i, h)),
            col,
            pl.BlockSpec((None, 1, T), lambda h, i, j: (h, 0, jnp.minimum(j, i))),
            col, col,
        ],
        out_specs=[pl.BlockSpec((T, HEAD_DIM), lambda h, i, j: (i, h)), col],
        out_shape=[jax.ShapeDtypeStruct((S, H * HEAD_DIM), BF16), jax.ShapeDtypeStruct((H, S, 1), F32)],
        scratch_shapes=[pltpu.VMEM((T, HEAD_DIM), F32), pltpu.VMEM((T, 1), F32)],
    )


def _attn_bwd_dkv(zm, dattn, fcol, frow, lse_row, delta_row, rowsum_row, T, jobs=()):
    S = zm.shape[0]
    H = fcol.shape[0]
    nb = S // T
    scale = 1.0 / math.sqrt(HEAD_DIM)

    def body(q_ref, k_ref, v_ref, do_ref, fk_ref, fq_ref, lse_ref, dl_ref, rs_ref,
             dk_ref, dv_ref, df_ref, dk_s, dv_s, df_s):
        j = pl.program_id(1)
        i = pl.program_id(2)

        @pl.when(i == 0)
        def _():
            dk_s[...] = jnp.zeros_like(dk_s)
            dv_s[...] = jnp.zeros_like(dv_s)
            df_s[...] = jnp.zeros_like(df_s)

        @pl.when(i >= j)
        def _():
            st = _dot(k_ref[...], q_ref[...], NT) * scale + (fq_ref[...] - fk_ref[...])
            keep = (_iota2((T, T), 0) + j * T) <= (_iota2((T, T), 1) + i * T)
            pt = jnp.exp(jnp.where(keep, st - lse_ref[...], _NEG))
            do = do_ref[...].astype(BF16)
            dpt = _dot(v_ref[...], do, NT)
            dst = pt * (dpt - (dl_ref[...] + rs_ref[...]))
            dv_s[...] += _dot(pt.astype(BF16), do)
            dk_s[...] += _dot(dst.astype(BF16), q_ref[...])
            df_s[...] -= jnp.sum(dst, axis=-1, keepdims=True)

        @pl.when(i == nb - 1)
        def _():
            dk_ref[...] = (dk_s[...] * scale).astype(BF16)
            dv_ref[...] = dv_s[...].astype(BF16)
            df_ref[...] = df_s[...]

    nh = H
    row = pl.BlockSpec((None, 1, T), lambda h, j, i: (h, 0, jnp.maximum(i, j)))
    kv_out = pl.BlockSpec((T, HEAD_DIM), lambda h, j, i: (j, h))
    return _carry_call(
        body, name="attn_bwd_dkv", grid=(H, nb, nb), jobs=jobs,
        args=[zm, zm, zm, dattn, fcol, frow, lse_row, delta_row, rowsum_row],
        semantics=("parallel", "parallel", "arbitrary"),
        in_specs=[
            pl.BlockSpec((T, HEAD_DIM), lambda h, j, i: (jnp.maximum(i, j), h)),
            pl.BlockSpec((T, HEAD_DIM), lambda h, j, i: (j, nh + h)),
            pl.BlockSpec((T, HEAD_DIM), lambda h, j, i: (j, 2 * nh + h)),
            pl.BlockSpec((T, HEAD_DIM), lambda h, j, i: (jnp.maximum(i, j), h)),
            pl.BlockSpec((None, T, 1), lambda h, j, i: (h, j, 0)),
            row, row, row, row,
        ],
        out_specs=[kv_out, kv_out, pl.BlockSpec((None, T, 1), lambda h, j, i: (h, j, 0))],
        out_shape=[jax.ShapeDtypeStruct((S, H * HEAD_DIM), BF16), jax.ShapeDtypeStruct((S, H * HEAD_DIM), BF16),
                   jax.ShapeDtypeStruct((H, S, 1), F32)],
        scratch_shapes=[pltpu.VMEM((T, HEAD_DIM), F32), pltpu.VMEM((T, HEAD_DIM), F32), pltpu.VMEM((T, 1), F32)],
    )


def _ln_stats(x):
    mu = jnp.mean(x, axis=-1, keepdims=True)
    xc = x - mu
    rstd = lax.rsqrt(jnp.mean(xc * xc, axis=-1, keepdims=True) + EPS)
    return xc * rstd, rstd


def _tril_mask():
    return _iota2((CHUNK, CHUNK), 0) >= _iota2((CHUNK, CHUNK), 1)


def _gmlp_fwd(zm, ln_g, ln_b, w_s, bs_col, tr):
    S = zm.shape[0]
    H = w_s.shape[0]
    DG = H * HEAD_DIM

    def body(zu_ref, zv_ref, g_ref, b_ref, w_ref, bs_ref, out_ref):
        u = _gelu(zu_ref[...].astype(F32))
        y, _ = _ln_stats(_gelu(zv_ref[...].astype(F32)))
        v = (y * g_ref[...] + b_ref[...]).astype(BF16)
        mask = _tril_mask()
        for h in range(H):
            wc = jnp.where(mask, w_ref[h], 0.0).astype(BF16)
            cs = slice(h * HEAD_DIM, (h + 1) * HEAD_DIM)
            for c in range(tr // CHUNK):
                rs = slice(c * CHUNK, (c + 1) * CHUNK)
                mix = _dot(wc, v[rs, cs]) + bs_ref[h]
                out_ref[rs, cs] = u[rs, cs] * mix

    full = lambda a: pl.BlockSpec(a.shape, lambda i: (0,) * a.ndim)
    return pl.pallas_call(
        body, name="gmlp_fwd", grid=(S // tr,),
        in_specs=[pl.BlockSpec((tr, DG), lambda i: (i, 3)), pl.BlockSpec((tr, DG), lambda i: (i, 4)),
                  full(ln_g), full(ln_b), full(w_s), full(bs_col)],
        out_specs=pl.BlockSpec((tr, DG), lambda i: (i, 0)),
        out_shape=jax.ShapeDtypeStruct((S, DG), F32),
        compiler_params=_params(("parallel",)),
    )(zm, zm, ln_g, ln_b, w_s, bs_col)


def _gmlp_bwd(dgm, zm, ln_g, ln_b, w_s, w_st, bs_col, tr):
    S = zm.shape[0]
    H = w_s.shape[0]
    DG = H * HEAD_DIM

    def body(dg_ref, zu_ref, zv_ref, g_ref, b_ref, w_ref, wt_ref, bs_ref,
             dzu_ref, dzv_ref, dw_ref, dbs_ref, dlg_ref, dlb_ref, dv_s):
        @pl.when(pl.program_id(0) == 0)
        def _():
            dw_ref[...] = jnp.zeros_like(dw_ref)
            dbs_ref[...] = jnp.zeros_like(dbs_ref)
            dlg_ref[...] = jnp.zeros_like(dlg_ref)
            dlb_ref[...] = jnp.zeros_like(dlb_ref)

        zu = zu_ref[...].astype(F32)
        zv = zv_ref[...].astype(F32)
        u = _gelu(zu)
        y, rstd = _ln_stats(_gelu(zv))
        v = (y * g_ref[...] + b_ref[...]).astype(BF16)
        dgm_blk = dg_ref[...]
        mask = _tril_mask()
        mask_t = _iota2((CHUNK, CHUNK), 0) <= _iota2((CHUNK, CHUNK), 1)
        for h in range(H):
            wc = jnp.where(mask, w_ref[h], 0.0).astype(BF16)
            wct = jnp.where(mask_t, wt_ref[h], 0.0).astype(BF16)
            cs = slice(h * HEAD_DIM, (h + 1) * HEAD_DIM)
            dw = jnp.zeros((CHUNK, CHUNK), F32)
            dbs = jnp.zeros((CHUNK, 1), F32)
            for c in range(tr // CHUNK):
                rs = slice(c * CHUNK, (c + 1) * CHUNK)
                vch = v[rs, cs]
                mix = _dot(wc, vch) + bs_ref[h]
                dg = dgm_blk[rs, cs]
                dzu_ref[rs, cs] = (dg * mix * _gelu_grad(zu[rs, cs])).astype(BF16)
                dmix = dg * u[rs, cs]
                dbs = dbs + jnp.sum(dmix, axis=-1, keepdims=True)
                dmix_b = dmix.astype(BF16)
                dw = dw + _dot(dmix_b, vch, NT)
                dv_s[rs, cs] = _dot(wct, dmix_b)
            dw_ref[h] += jnp.where(mask, dw, 0.0)
            dbs_ref[h] += dbs
        dv = dv_s[...]
        dlg_ref[...] += jnp.sum(dv * y, axis=0, keepdims=True)
        dlb_ref[...] += jnp.sum(dv, axis=0, keepdims=True)
        dy = dv * g_ref[...]
        dgv = rstd * (dy - jnp.mean(dy, axis=-1, keepdims=True) - y * jnp.mean(dy * y, axis=-1, keepdims=True))
        dzv_ref[...] = (dgv * _gelu_grad(zv)).astype(BF16)

    full = lambda a: pl.BlockSpec(a.shape, lambda i: (0,) * a.ndim)
    rows = pl.BlockSpec((tr, DG), lambda i: (i, 0))
    return pl.pallas_call(
        body, name="gmlp_bwd", grid=(S // tr,),
        in_specs=[rows, pl.BlockSpec((tr, DG), lambda i: (i, 3)), pl.BlockSpec((tr, DG), lambda i: (i, 4)),
                  full(ln_g), full(ln_b), full(w_s), full(w_st), full(bs_col)],
        out_specs=[rows, rows, full(w_s), full(bs_col), full(ln_g), full(ln_b)],
        out_shape=[jax.ShapeDtypeStruct((S, DG), BF16), jax.ShapeDtypeStruct((S, DG), BF16),
                   jax.ShapeDtypeStruct(w_s.shape, F32), jax.ShapeDtypeStruct(bs_col.shape, F32),
                   jax.ShapeDtypeStruct(ln_g.shape, F32), jax.ShapeDtypeStruct(ln_b.shape, F32)],
        scratch_shapes=[pltpu.VMEM((tr, DG), F32)],
        compiler_params=_params(("arbitrary",)),
    )(dgm, zm, zm, ln_g, ln_b, w_s, w_st, bs_col)


def _all_gather(name, blk):
    R, C = blk.shape

    def body(x_ref, out_ref, send_sems, recv_sems, local_sem):
        x, y, c = _me()
        me, sibling = (x, y, c), (x, y, 1 - c)
        chips = [(1 - x, y), (x, 1 - y), (1 - x, 1 - y)]

        def slab(px, py, pc):
            return out_ref.at[4 * px + 2 * py + pc]

        def copy(k, block, to, src=None):
            return pltpu.make_async_remote_copy(
                src_ref=slab(*block) if src is None else src, dst_ref=slab(*block),
                send_sem=send_sems.at[k], recv_sem=recv_sems.at[k], device_id=to, device_id_type=MESH)

        mine = pltpu.make_async_copy(x_ref, slab(*me), local_sem)
        mine.start()
        first = [copy(0, me, sibling, src=x_ref)]
        first += [copy(1 + n, me, (*chip, c), src=x_ref) for n, chip in enumerate(chips)]
        for cp in first:
            cp.start()
        passed = [copy(4 + n, (*chip, c), sibling) for n, chip in enumerate(chips)]
        for n, chip in enumerate(chips):
            copy(1 + n, (*chip, c), me).wait_recv()
            passed[n].start()
        copy(0, sibling, me).wait_recv()
        for n, chip in enumerate(chips):
            copy(4 + n, (*chip, 1 - c), me).wait_recv()
        for cp in first + passed:
            cp.wait_send()
        mine.wait()

    return pl.pallas_call(
        body, name=name, out_shape=jax.ShapeDtypeStruct((N_DEV, R, C), blk.dtype),
        in_specs=[_ANY], out_specs=_ANY,
        scratch_shapes=[pltpu.SemaphoreType.DMA((7,)), pltpu.SemaphoreType.DMA((7,)), pltpu.SemaphoreType.DMA(())],
    )(blk)


def _row_tile(R, C, itemsize=4, target_bytes=2 * 1024 * 1024):
    tr = R
    while tr % 2 == 0 and tr * C * itemsize > target_bytes and (tr // 2) % 16 == 0:
        tr //= 2
    return tr


def _rs_add1(name, g4, recv, c_idx):
    _, _, R, C = g4.shape
    tr = _row_tile(R, C)

    def body(c_ref, g_ref, r_ref, h_ref, hb_ref):
        h = g_ref[...] + r_ref[...].astype(F32)
        h_ref[...] = h
        hb_ref[...] = h.astype(BF16)

    blk = pl.BlockSpec((None, tr, C), lambda p, i, c_ref: (p, i, 0))
    return pl.pallas_call(
        body, name=name,
        grid_spec=pltpu.PrefetchScalarGridSpec(
            num_scalar_prefetch=1, grid=(4, R // tr),
            in_specs=[pl.BlockSpec((None, None, tr, C), lambda p, i, c_ref: (p, c_ref[0], i, 0)), blk],
            out_specs=[blk, blk]),
        out_shape=[jax.ShapeDtypeStruct((4, R, C), F32), jax.ShapeDtypeStruct((4, R, C), BF16)],
        compiler_params=_params(("parallel", "parallel")),
    )(c_idx, g4, recv)


def _rs_add2(name, h, recv, p_idx):
    _, R, C = h.shape
    tr = _row_tile(R, C)

    def body(p_ref, h_ref, r_ref, out_ref):
        out_ref[...] = ((h_ref[...] + r_ref[0].astype(F32)) + r_ref[1].astype(F32)) + r_ref[2].astype(F32)

    return pl.pallas_call(
        body, name=name,
        grid_spec=pltpu.PrefetchScalarGridSpec(
            num_scalar_prefetch=1, grid=(R // tr,),
            in_specs=[pl.BlockSpec((None, tr, C), lambda i, p_ref: (p_ref[0], i, 0)),
                      pl.BlockSpec((3, tr, C), lambda i, p_ref: (0, i, 0))],
            out_specs=pl.BlockSpec((tr, C), lambda i, p_ref: (i, 0))),
        out_shape=jax.ShapeDtypeStruct((R, C), F32),
        compiler_params=_params(("parallel",)),
    )(p_idx, h, recv)


def _sum8(name, g):
    _, R, C = g.shape

    def body(g_ref, out_ref):
        acc = g_ref[0]
        for d in range(1, N_DEV):
            acc = acc + g_ref[d]
        out_ref[...] = acc

    return pl.pallas_call(body, name=name, out_shape=jax.ShapeDtypeStruct((R, C), F32),
                          compiler_params=_params())(g)


def _adamw(name, w, g, m, v):
    R, C = w.shape
    tr = _row_tile(R, C, target_bytes=1024 * 1024)

    def fn(w, g, m, v):
        m = ADAM_B1 * m + (1.0 - ADAM_B1) * g
        v = ADAM_B2 * v + (1.0 - ADAM_B2) * (g * g)
        m_hat = m / (1.0 - ADAM_B1 ** ADAM_STEP)
        v_hat = v / (1.0 - ADAM_B2 ** ADAM_STEP)
        delta = -ADAM_LR * (m_hat / (jnp.sqrt(v_hat) + ADAM_EPS) + ADAM_WD * w)
        return (delta, m, v), ()

    return _row_call(name, fn, [w, g, m, v], [], [(C, F32)] * 3, [], tr)


def _pack(parts):
    flat = []
    total = 0
    for a in parts:
        n = math.prod(a.shape)
        flat.append(a.reshape(-1).astype(F32))
        if n % LANES:
            flat.append(jnp.zeros((-n % LANES,), F32))
        total += n + (-n % LANES)
    if total % (8 * LANES):
        flat.append(jnp.zeros((-total % (8 * LANES),), F32))
    return jnp.concatenate(flat).reshape(-1, LANES)


def _unpack(packed, shapes):
    out = []
    r = 0
    for shp in shapes:
        n = math.prod(shp)
        nr = -(-n // LANES)
        out.append(packed[r:r + nr].reshape(-1)[:n].reshape(shp))
        r += nr
    return out


def kernel(x, norm_mix_g, w_in, b_f, gmlp_ln_g, gmlp_ln_b, w_s, b_s, attn_out_g, gmlp_out_g, w_out, norm_ffn_g, w_ff1, w_ff2, norm_final_g, loss_target, m_norm_mix_g, m_w_in, m_b_f, m_gmlp_ln_g, m_gmlp_ln_b, m_w_s, m_b_s, m_attn_out_g, m_gmlp_out_g, m_w_out, m_norm_ffn_g, m_w_ff1, m_w_ff2, m_norm_final_g, v_norm_mix_g, v_w_in, v_b_f, v_gmlp_ln_g, v_gmlp_ln_b, v_w_s, v_b_s, v_attn_out_g, v_gmlp_out_g, v_w_out, v_norm_ffn_g, v_w_ff1, v_w_ff2, v_norm_final_g):
    S, D = x.shape[1], x.shape[2]
    H = b_f.shape[1]
    DA = H * HEAD_DIM
    DG = gmlp_ln_g.shape[1]
    DQKV = 3 * DA
    DMAIN = DQKV + 2 * DG
    DIN = DMAIN + H
    DFF = w_ff1.shape[2] * N_DEV
    w_in_cols = w_in.shape[2]
    assert DIN == w_in_cols * N_DEV and DA == DG and D == DA + DG

    T_ATT = min(T_ATT_MAX, S)
    TR = min(TR_MAX, S)

    x0 = x[0]
    tgt = loss_target[0]
    g_final = norm_final_g.reshape(1, D)

    w_in_all = _all_gather("ag_w_in", w_in[0].astype(BF16))
    w_in_full = jnp.concatenate([w_in_all[n] for n in range(N_DEV)], axis=1)
    w_main = jnp.concatenate([w_in_full[:, :DQKV], w_in_full[:, DQKV + H:]], axis=1)
    w_f = jnp.pad(w_in_full[:, DQKV:DQKV + H], ((0, 0), (0, LANES - H)))
    FB = DFF // N_DEV
    x_pos, y_pos, c_pos = _me()
    c_idx = jnp.reshape(c_pos, (1,)).astype(jnp.int32)
    p_idx = jnp.reshape(2 * x_pos + y_pos, (1,)).astype(jnp.int32)

    (h,), _ = _row_call("rms_mix", lambda xb, g: ((_rms_fwd(xb, g),), ()), [x0], [norm_mix_g], [(D, BF16)], [], TR)
    w_ff1_b = w_ff1[0].astype(BF16)
    (zm,), ((w_out_part,), (w_ff1_half,)) = _mm_nn(
        "in_proj", h, w_main, [BF16], 1024, 1024, 2048,
        jobs=[_job_gather_chips(w_out[0].astype(BF16)), _job_gather_chips(w_ff1_b, part=(0, 2))])
    (zf,) = _mm_nn("in_proj_f", h, w_f, [F32], 1024, LANES, 2048)
    bf_pad = jnp.pad(b_f, ((0, 0), (0, LANES - H)))
    f_row = _fgate_fwd(zf, bf_pad)
    NB = S // T_ATT
    f_col3 = f_row.reshape(H, S, 1)
    f_row3 = f_row.reshape(H, NB, 1, T_ATT)
    (attn, lse_col3), ((w_out_all,), (w_ff1_part,), (w_ff2_part,)) = _attn2_fwd(
        zm, f_col3, f_row3, T_ATT, jobs=[_job_gather_sibling(w_out_part),
                                         _job_gather_chips(w_ff1_b, part=(1, 2), into=w_ff1_half),
                                         _job_gather_chips(w_ff2[0].astype(BF16))])
    w_out_full = w_out_all.reshape(D, D)
    bs_col = b_s[0].reshape(H, CHUNK, 1)
    gm = _gmlp_fwd(zm, gmlp_ln_g, gmlp_ln_b, w_s[0], bs_col, TR)

    def merge_fn(a, g, ga, gg):
        return (jnp.concatenate([_rms_fwd(a, ga), _rms_fwd(g, gg)], axis=1),), ()
    (merged,), _ = _row_call("rms_merge", merge_fn, [attn, gm], [attn_out_g, gmlp_out_g], [(D, BF16)], [], TR)

    (x1,), ((w_ff1_all,), (w_ff2_all,)) = _mm_nn(
        "out_proj", merged, w_out_full, [F32], 1024, 1024, 2048, extras=[x0], epilogue=lambda acc, r: (acc + r,),
        jobs=[_job_gather_sibling(w_ff1_part), _job_gather_sibling(w_ff2_part)])
    w_ff2_full = w_ff2_all.reshape(DFF, D)
    (h2,), _ = _row_call("rms_ffn", lambda xb, g: ((_rms_fwd(xb, g),), ()), [x1], [norm_ffn_g], [(D, BF16)], [], TR)

    tm, tn, tk = min(1024, S), min(1024, FB), min(2048, D)
    o_spec = pl.BlockSpec((tm, tn), lambda i, j, k: (i, j))

    def relu_sq(acc):
        a = jnp.maximum(acc, 0.0)
        return a, a * a
    nj = FB // tn
    a_act, a_sq = _mm(
        "ff1", (S // tm, DFF // tn, D // tk), h2, pl.BlockSpec((tm, tk), lambda i, j, k: (i, k)),
        w_ff1_all, pl.BlockSpec((None, tk, tn), lambda i, j, k: (j // nj, k, j % nj)), NN, (tm, tn),
        [jax.ShapeDtypeStruct((S, DFF), BF16)] * 2, [o_spec] * 2, epilogue=relu_sq)
    (x2,) = _mm_nn("ff2", a_sq, w_ff2_full, [F32], 1024, 1024, 2048, extras=[x1], epilogue=lambda acc, r: (acc + r,))

    def head_fn(xb, t, g):
        rstd = lax.rsqrt(jnp.mean(xb * xb, axis=-1, keepdims=True) + EPS)
        xhat = xb * rstd
        err = xhat * g - t
        loss = 0.5 * jnp.sum(jnp.mean(err * err, axis=-1, keepdims=True), axis=0, keepdims=True)
        dy = err * (1.0 / D)
        dg = jnp.sum(dy * xhat, axis=0, keepdims=True)
        dxhat = dy * g
        dx = rstd * (dxhat - xhat * jnp.mean(dxhat * xhat, axis=-1, keepdims=True))
        return (dx, dx), (dg, jnp.broadcast_to(loss, (1, LANES)))
    (dx2, dx2_b), (dg_final, loss_part) = _row_call(
        "loss_head", head_fn, [x2, tgt], [g_final], [(D, F32), (D, BF16)], [D, LANES], TR)

    (da,) = _mm_nt("ff2_dx", dx2_b, w_ff2_full, [BF16], 1024, 1024, 2048, extras=[a_act],
                   epilogue=lambda acc, a: (2.0 * a.astype(F32) * acc,))
    dw_ff2, dw_ff2_b = _mm_tn("ff2_dw", a_sq, dx2_b, [F32, BF16], 1024, 1024, 1024)
    tm2, tk2 = min(1024, D), min(1024, S)
    dw1_spec = pl.BlockSpec((None, tm2, FB), lambda i, j, k: (j, i, 0))
    (dw_ff1, dw_ff1_b), ((r1_ff2,),) = _mm(
        "ff1_dw", (D // tm2, DFF // FB, S // tk2), h2, pl.BlockSpec((tk2, tm2), lambda i, j, k: (k, i)),
        da, pl.BlockSpec((tk2, FB), lambda i, j, k: (k, j)), TN, (tm2, FB),
        [jax.ShapeDtypeStruct((N_DEV, D, FB), F32), jax.ShapeDtypeStruct((N_DEV, D, FB), BF16)], [dw1_spec] * 2,
        epilogue=lambda acc: (acc, acc), jobs=[_job_scatter_sibling(dw_ff2_b.reshape(4, 2, FB, D))])
    h_ff2, hb_ff2 = _rs_add1("rs_add1_w_ff2", dw_ff2.reshape(4, 2, FB, D), r1_ff2, c_idx)
    tkb = min(1024, FB)
    nkb = FB // tkb
    tnb = min(1024, D)
    (dh2,), ((r2_ff2,), (r1_ff1,)) = _mm(
        "ff1_dx", (S // tm, D // tnb, DFF // tkb), da, pl.BlockSpec((tm, tkb), lambda i, j, k: (i, k)),
        w_ff1_all, pl.BlockSpec((None, tnb, tkb), lambda i, j, k: (k // nkb, j, k % nkb)), NT, (tm, tnb),
        [jax.ShapeDtypeStruct((S, D), F32)], [pl.BlockSpec((tm, tnb), lambda i, j, k: (i, j))],
        jobs=[_job_scatter_chips(hb_ff2), _job_scatter_sibling(dw_ff1_b.reshape(4, 2, D, FB))])
    g_w_ff2 = _rs_add2("rs_add2_w_ff2", h_ff2, r2_ff2, p_idx)
    h_ff1, hb_ff1 = _rs_add1("rs_add1_w_ff1", dw_ff1.reshape(4, 2, D, FB), r1_ff1, c_idx)

    def ffn_bwd_fn(dh, xb, dres, g):
        dx, dg = _rms_bwd(dh, xb, g)
        dx = dx + dres
        return (dx, dx), (dg,)
    (dx1, dx1_b), (dg_ffn,) = _row_call("rms_ffn_bwd", ffn_bwd_fn, [dh2, x1, dx2], [norm_ffn_g],
                                        [(D, F32), (D, BF16)], [D], TR)

    (dmerged,) = _mm_nt("out_proj_dx", dx1_b, w_out_full, [F32], 1024, 1024, 2048)
    dw_out, dw_out_b = _mm_tn("out_proj_dw", merged, dx1_b, [F32, BF16], 1024, 1024, 1024)

    def merge_bwd_fn(dm, a, g, ga, gg):
        da_, dga = _rms_bwd(dm[:, :DA], a, ga)
        dg_, dgg = _rms_bwd(dm[:, DA:], g, gg)
        return (da_, dg_), (dga, dgg)
    (dattn, dgm), (dg_attn, dg_gmlp) = _row_call(
        "rms_merge_bwd", merge_bwd_fn, [dmerged, attn, gm], [attn_out_g, gmlp_out_g], [(DA, F32), (DG, F32)], [DA, DG], TR)

    w_st = jnp.swapaxes(w_s[0], 1, 2)
    dzu, dzv, dw_s, dbs_col, dln_g, dln_b = _gmlp_bwd(dgm, zm, gmlp_ln_g, gmlp_ln_b, w_s[0], w_st, bs_col, TR)

    delta_row = _attn_delta(dattn, attn, TR)
    lse_row3 = lse_col3.reshape(H, NB, 1, T_ATT)
    (dq, ds_rowsum), ((r2_ff1,), (r1_out,)) = _attn2_bwd_dq(
        zm, dattn, f_col3, f_row3, lse_col3, delta_row.reshape(H, S, 1), T_ATT,
        jobs=[_job_scatter_chips(hb_ff1), _job_scatter_sibling(dw_out_b.reshape(4, 2, D // N_DEV, D))])
    g_w_ff1 = _rs_add2("rs_add2_w_ff1", h_ff1, r2_ff1, p_idx)
    h_out, hb_out = _rs_add1("rs_add1_w_out", dw_out.reshape(4, 2, D // N_DEV, D), r1_out, c_idx)
    (dk, dv, df_col3), ((r2_out,),) = _attn2_bwd_dkv(
        zm, dattn, f_col3, f_row3, lse_row3, delta_row.reshape(H, NB, 1, T_ATT),
        ds_rowsum.reshape(H, NB, 1, T_ATT), T_ATT,
        jobs=[_job_scatter_chips(hb_out)])
    g_w_out = _rs_add2("rs_add2_w_out", h_out, r2_out, p_idx)
    dzf, dbf = _fgate_bwd(df_col3.reshape(H, S), zf, bf_pad)

    dz_main = jnp.concatenate([dq, dk, dv, dzu, dzv], axis=1)
    (dw_main,) = _mm_tn("in_proj_dw", h, dz_main, [F32], 1024, 1024, 1024)
    (dw_f,) = _mm_tn("in_proj_f_dw", h, dzf, [F32], 2048, LANES, 1024)
    dw_in_full = jnp.concatenate([dw_main[:, :DQKV], dw_f[:, :H], dw_main[:, DQKV:]], axis=1)
    g4_in = jnp.stack([dw_in_full[:, n * w_in_cols:(n + 1) * w_in_cols] for n in range(N_DEV)])
    g4_in = g4_in.reshape(4, 2, D, w_in_cols)
    (dh_f,), ((r1_in,),) = _mm_nt("in_proj_f_dx", dzf, w_f, [F32], 1024, 1024, LANES,
                                  jobs=[_job_scatter_sibling(g4_in.astype(BF16))])
    h_in, hb_in = _rs_add1("rs_add1_w_in", g4_in, r1_in, c_idx)
    (dh,), ((r2_in,),) = _mm_nt("in_proj_dx", dz_main, w_main, [F32], 1024, 1024, 1024, extras=[dh_f],
                                epilogue=lambda acc, r: (acc + r,), jobs=[_job_scatter_chips(hb_in)])
    g_w_in = _rs_add2("rs_add2_w_in", h_in, r2_in, p_idx)

    def mix_bwd_fn(dhb, xb, dres, g):
        dx, dg = _rms_bwd(dhb, xb, g)
        return (dx + dres,), (dg,)
    (grad_x,), (dg_mix,) = _row_call("rms_mix_bwd", mix_bwd_fn, [dh, x0, dx1], [norm_mix_g], [(D, F32)], [D], TR)

    small_shapes = [norm_mix_g.shape, b_f.shape, gmlp_ln_g.shape, gmlp_ln_b.shape, w_s.shape, b_s.shape,
                    attn_out_g.shape, gmlp_out_g.shape, norm_ffn_g.shape, norm_final_g.shape]
    small_parts = [dg_mix, dbf[:, :H], dln_g, dln_b, dw_s, dbs_col, dg_attn, dg_gmlp, dg_ffn, dg_final]
    g_small = _sum8("small_sum", _all_gather("ag_small", _pack(small_parts)))
    w_small = _pack([norm_mix_g, b_f, gmlp_ln_g, gmlp_ln_b, w_s, b_s, attn_out_g, gmlp_out_g, norm_ffn_g, norm_final_g])
    m_small = _pack([m_norm_mix_g, m_b_f, m_gmlp_ln_g, m_gmlp_ln_b, m_w_s, m_b_s, m_attn_out_g, m_gmlp_out_g,
                     m_norm_ffn_g, m_norm_final_g])
    v_small = _pack([v_norm_mix_g, v_b_f, v_gmlp_ln_g, v_gmlp_ln_b, v_w_s, v_b_s, v_attn_out_g, v_gmlp_out_g,
                     v_norm_ffn_g, v_norm_final_g])
    (d_small, nm_small, nv_small), _ = _adamw("adamw_small", w_small, g_small, m_small, v_small)
    gs = _unpack(g_small, small_shapes)
    ds = _unpack(d_small, small_shapes)
    nms = _unpack(nm_small, small_shapes)
    nvs = _unpack(nv_small, small_shapes)

    big = {}
    for nm, w, g, m, v in (("w_in", w_in, g_w_in, m_w_in, v_w_in), ("w_out", w_out, g_w_out, m_w_out, v_w_out),
                           ("w_ff1", w_ff1, g_w_ff1, m_w_ff1, v_w_ff1), ("w_ff2", w_ff2, g_w_ff2, m_w_ff2, v_w_ff2)):
        (d_, m_, v_), _ = _adamw("adamw_" + nm, w[0], g, m[0], v[0])
        big[nm] = (g[None], d_[None], m_[None], v_[None])

    loss = lax.psum(loss_part[0, 0], ("x", "y", "c"))

    def leaves(n):
        sm = (gs, ds, nms, nvs)[n]
        return [sm[0], big["w_in"][n], sm[1], sm[2], sm[3], sm[4], sm[5], sm[6], sm[7], big["w_out"][n], sm[8],
                big["w_ff1"][n], big["w_ff2"][n], sm[9]]

    return (loss, grad_x[None], *leaves(0), *leaves(1), *leaves(2), *leaves(3))
```

```python
import functools
import math

import jax
import jax.numpy as jnp
from jax import lax
from jax.experimental import pallas as pl
from jax.experimental.pallas import tpu as pltpu

F32 = jnp.float32
BF16 = jnp.bfloat16
MESH = pl.DeviceIdType.MESH

HEAD_DIM = 128
CHUNK = 128
EPS = 1e-6
LANES = 128
N_DEV = 8

ADAM_LR = 0.001
ADAM_B1 = 0.9
ADAM_B2 = 0.999
ADAM_EPS = 1e-08
ADAM_WD = 0.01
ADAM_STEP = 10

VMEM_LIMIT_BYTES = 56 * 1024 * 1024
T_ATT_MAX = 1024
TR_MAX = 256

NN = ((1,), (0,))
NT = ((1,), (1,))
TN = ((0,), (0,))


def _params(sem=None):
    return pltpu.CompilerParams(dimension_semantics=sem, vmem_limit_bytes=VMEM_LIMIT_BYTES)


def _dot(a, b, contract=NN):
    return lax.dot_general(a, b, (contract, ((), ())), preferred_element_type=F32)


def _dot3(x, t):
    x1 = x.astype(BF16)
    r1 = x - x1.astype(F32)
    x2 = r1.astype(BF16)
    x3 = (r1 - x2.astype(F32)).astype(BF16)
    return _dot(x1, t) + _dot(x2, t) + _dot(x3, t)


def _iota2(shape, dim):
    return lax.broadcasted_iota(jnp.int32, shape, dim)


def _row_call(name, fn, row_ins, bcast_ins, row_outs, acc_outs, tr):
    S = row_ins[0].shape[0]
    assert S % tr == 0
    n_ri, n_bi, n_ro, n_ao = len(row_ins), len(bcast_ins), len(row_outs), len(acc_outs)

    def body(*refs):
        ins = [r[...] for r in refs[:n_ri + n_bi]]
        ro_refs = refs[n_ri + n_bi:n_ri + n_bi + n_ro]
        ao_refs = refs[n_ri + n_bi + n_ro:]
        ro, ao = fn(*ins)
        for r, v in zip(ro_refs, ro):
            r[...] = v.astype(r.dtype)
        if n_ao:
            @pl.when(pl.program_id(0) == 0)
            def _():
                for r in ao_refs:
                    r[...] = jnp.zeros_like(r)
            for r, v in zip(ao_refs, ao):
                r[...] += v

    in_specs = [pl.BlockSpec((tr, a.shape[1]), lambda i: (i, 0)) for a in row_ins]
    in_specs += [pl.BlockSpec(a.shape, lambda i: (0, 0)) for a in bcast_ins]
    out_specs = [pl.BlockSpec((tr, d), lambda i: (i, 0)) for d, _ in row_outs]
    out_specs += [pl.BlockSpec((1, d), lambda i: (0, 0)) for d in acc_outs]
    out_shape = [jax.ShapeDtypeStruct((S, d), dt) for d, dt in row_outs]
    out_shape += [jax.ShapeDtypeStruct((1, d), F32) for d in acc_outs]
    outs = pl.pallas_call(
        body, name=name, grid=(S // tr,), in_specs=in_specs, out_specs=out_specs, out_shape=out_shape,
        compiler_params=_params(("arbitrary",) if n_ao else ("parallel",)),
    )(*row_ins, *bcast_ins)
    return outs[:n_ro], outs[n_ro:]


def _rms_fwd(x, g):
    rstd = lax.rsqrt(jnp.mean(x * x, axis=-1, keepdims=True) + EPS)
    return x * rstd * g


def _rms_bwd(dy, x, g):
    rstd = lax.rsqrt(jnp.mean(x * x, axis=-1, keepdims=True) + EPS)
    xhat = x * rstd
    dg = jnp.sum(dy * xhat, axis=0, keepdims=True)
    dxhat = dy * g
    dx = rstd * (dxhat - xhat * jnp.mean(dxhat * xhat, axis=-1, keepdims=True))
    return dx, dg


_GELU_C = math.sqrt(2.0 / math.pi)


def _gelu(x):
    return 0.5 * x * (1.0 + jnp.tanh(_GELU_C * (x + 0.044715 * (x * x * x))))


def _gelu_grad(x):
    t = jnp.tanh(_GELU_C * (x + 0.044715 * (x * x * x)))
    return 0.5 * (1.0 + t) + 0.5 * x * (1.0 - t * t) * (_GELU_C * (1.0 + 3.0 * 0.044715 * (x * x)))


def _me():
    return lax.axis_index("x"), lax.axis_index("y"), lax.axis_index("c")


def _other_chips(x, y):
    return [(1 - x, y), (x, 1 - y), (1 - x, 1 - y)]


_ANY = pl.BlockSpec(memory_space=pl.ANY)


class _Job:
    def __init__(self, ins, outs, n_sems, make, aliases=None):
        self.ins, self.outs, self.n_sems, self.make, self.aliases = ins, outs, n_sems, make, aliases or {}


def _job_gather_chips(blk, part=(0, 1, 1), into=None):
    R, C = blk.shape
    nr = R // part[2]
    rows = pl.ds(part[0] * nr, (part[1] - part[0]) * nr)

    def make(ins, outs, send_sems, recv_sems, base):
        x_ref, (out_ref,) = ins[0], outs
        x, y, c = _me()
        mine = 4 * x + 2 * y + c
        targets = [(x, y, 1 - c)] + [(cx, cy, c) for cx, cy in _other_chips(x, y)]

        def copy(k, slab, to):
            return pltpu.make_async_remote_copy(
                src_ref=x_ref.at[rows, :], dst_ref=out_ref.at[slab, rows, :], send_sem=send_sems.at[base + k],
                recv_sem=recv_sems.at[base + k], device_id=to, device_id_type=MESH)

        starts = [copy(k, mine, to) for k, to in enumerate(targets)]
        arrivals = [copy(k, 4 * tx + 2 * ty + tc, (tx, ty, tc)) for k, (tx, ty, tc) in enumerate(targets)]
        local = [pltpu.make_async_copy(x_ref.at[rows, :], out_ref.at[mine, rows, :], send_sems.at[base + 4])]
        return starts, arrivals, local

    out = jax.ShapeDtypeStruct((N_DEV, R, C), blk.dtype)
    if into is None:
        return _Job([blk], [out], 5, make)
    return _Job([blk, into], [out], 5, make, aliases={1: 0})


def _job_gather_sibling(part):
    def make(ins, outs, send_sems, recv_sems, base):
        (out_ref,) = outs
        x, y, c = _me()

        def copy(k, slab):
            return pltpu.make_async_remote_copy(
                src_ref=out_ref.at[slab], dst_ref=out_ref.at[slab], send_sem=send_sems.at[base + k],
                recv_sem=recv_sems.at[base + k], device_id=(x, y, 1 - c), device_id_type=MESH)

        chips = _other_chips(x, y)
        starts = [copy(k, 4 * cx + 2 * cy + c) for k, (cx, cy) in enumerate(chips)]
        arrivals = [copy(k, 4 * cx + 2 * cy + (1 - c)) for k, (cx, cy) in enumerate(chips)]
        return starts, arrivals, []

    return _Job([part], [jax.ShapeDtypeStruct(part.shape, part.dtype)], 3, make, aliases={0: 0})


def _job_scatter_sibling(gb):
    _, _, R, C = gb.shape

    def make(ins, outs, send_sems, recv_sems, base):
        (g_ref,), (recv_ref,) = ins, outs
        x, y, c = _me()
        copies = [pltpu.make_async_remote_copy(
            src_ref=g_ref.at[p, 1 - c], dst_ref=recv_ref.at[p], send_sem=send_sems.at[base + p],
            recv_sem=recv_sems.at[base + p], device_id=(x, y, 1 - c), device_id_type=MESH) for p in range(4)]
        return copies, copies, []

    return _Job([gb], [jax.ShapeDtypeStruct((4, R, C), gb.dtype)], 4, make)


def _job_scatter_sibling_windows(gb, starts, width):
    R, _ = gb.shape

    def make(ins, outs, send_sems, recv_sems, base):
        (g_ref,), (recv_ref,) = ins, outs
        x, y, c = _me()
        copies = []
        for p in range(4):
            start = pl.multiple_of(jnp.where(c == 0, starts[2 * p + 1], starts[2 * p]), LANES)
            copies.append(pltpu.make_async_remote_copy(
                src_ref=g_ref.at[:, pl.ds(start, width)], dst_ref=recv_ref.at[p], send_sem=send_sems.at[base + p],
                recv_sem=recv_sems.at[base + p], device_id=(x, y, 1 - c), device_id_type=MESH))
        return copies, copies, []

    return _Job([gb], [jax.ShapeDtypeStruct((4, R, width), gb.dtype)], 4, make)


def _job_scatter_chips(hb):
    _, R, C = hb.shape

    def make(ins, outs, send_sems, recv_sems, base):
        (h_ref,), (recv_ref,) = ins, outs
        x, y, c = _me()
        copies = [pltpu.make_async_remote_copy(
            src_ref=h_ref.at[2 * cx + cy], dst_ref=recv_ref.at[n], send_sem=send_sems.at[base + n],
            recv_sem=recv_sems.at[base + n], device_id=(cx, cy, c), device_id_type=MESH)
            for n, (cx, cy) in enumerate(_other_chips(x, y))]
        return copies, copies, []

    return _Job([hb], [jax.ShapeDtypeStruct((3, R, C), hb.dtype)], 3, make)


def _carry_call(body, *, name, grid, in_specs, out_specs, out_shape, scratch_shapes, semantics, args, jobs=()):
    jobs = list(jobs)
    n_in, n_out, n_scr = len(in_specs), len(out_specs), len(scratch_shapes)
    j_ins = [a for j in jobs for a in j.ins]
    j_outs = [o for j in jobs for o in j.outs]
    n_sems = sum(j.n_sems for j in jobs)
    aliases = {}
    i0, o0 = n_in, n_out
    for j in jobs:
        for a, b in j.aliases.items():
            aliases[i0 + a] = o0 + b
        i0 += len(j.ins)
        o0 += len(j.outs)

    def full_body(*refs):
        ins = refs[:n_in]
        jin = refs[n_in:n_in + len(j_ins)]
        outs = refs[n_in + len(j_ins):n_in + len(j_ins) + n_out]
        jout = refs[n_in + len(j_ins) + n_out:n_in + len(j_ins) + n_out + len(j_outs)]
        scr = refs[n_in + len(j_ins) + n_out + len(j_outs):]
        if jobs:
            send_sems, recv_sems = scr[n_scr], scr[n_scr + 1]
            starts, arrivals, local = [], [], []
            base = i0 = o0 = 0
            for j in jobs:
                s, a, l = j.make(jin[i0:i0 + len(j.ins)], jout[o0:o0 + len(j.outs)], send_sems, recv_sems, base)
                starts += s
                arrivals += a
                local += l
                base += j.n_sems
                i0 += len(j.ins)
                o0 += len(j.outs)
            pids = [pl.program_id(d) for d in range(len(grid))]
            first = functools.reduce(jnp.logical_and, [p == 0 for p in pids])
            last = functools.reduce(jnp.logical_and, [p == n - 1 for p, n in zip(pids, grid)])

            @pl.when(first)
            def _():
                for cp in local + starts:
                    cp.start()

        body(*ins, *outs, *scr[:n_scr])

        if jobs:
            @pl.when(last)
            def _():
                for cp in arrivals:
                    cp.wait_recv()
                for cp in starts:
                    cp.wait_send()
                for cp in local:
                    cp.wait()

    sems = [pltpu.SemaphoreType.DMA((n_sems,)), pltpu.SemaphoreType.DMA((n_sems,))] if jobs else []
    res = pl.pallas_call(
        full_body, name=name, grid=grid,
        in_specs=list(in_specs) + [_ANY] * len(j_ins),
        out_specs=list(out_specs) + [_ANY] * len(j_outs),
        out_shape=list(out_shape) + j_outs,
        scratch_shapes=list(scratch_shapes) + sems,
        input_output_aliases=aliases,
        compiler_params=_params(("arbitrary",) * len(grid) if jobs else semantics),
    )(*args, *j_ins)
    body_res, job_res = res[:n_out], res[n_out:]
    per_job = []
    for j in jobs:
        per_job.append(job_res[:len(j.outs)])
        job_res = job_res[len(j.outs):]
    return body_res, per_job


def _run_jobs(name, jobs):
    def body(done_ref):
        done_ref[...] = jnp.zeros_like(done_ref)

    return _carry_call(body, name=name, grid=(1,), in_specs=[], out_specs=[pl.BlockSpec((8, LANES), lambda i: (0, 0))],
                       out_shape=[jax.ShapeDtypeStruct((8, LANES), F32)], scratch_shapes=[], semantics=("arbitrary",),
                       args=[], jobs=jobs)[1]


def _mm(name, grid, a, a_spec, b, b_spec, contract, acc_shape, out_shape, out_specs, extras=(), epilogue=None, jobs=()):
    nk = grid[2]
    n_e = len(extras)
    n_o = len(out_shape)
    if epilogue is None:
        epilogue = lambda acc: (acc,)

    def body(a_ref, b_ref, *rest):
        e_refs = rest[:n_e]
        o_refs = rest[n_e:n_e + n_o]

        def finish(total):
            res = epilogue(total, *[r[...] for r in e_refs])
            for o, r in zip(o_refs, res):
                o[...] = r.astype(o.dtype)

        if nk == 1:
            finish(_dot(a_ref[...], b_ref[...], contract))
            return
        acc = rest[n_e + n_o]
        k = pl.program_id(2)

        @pl.when(k == 0)
        def _():
            acc[...] = _dot(a_ref[...], b_ref[...], contract)

        @pl.when(jnp.logical_and(k > 0, k < nk - 1))
        def _():
            acc[...] += _dot(a_ref[...], b_ref[...], contract)

        @pl.when(k == nk - 1)
        def _():
            finish(acc[...] + _dot(a_ref[...], b_ref[...], contract))

    outs, job_res = _carry_call(
        body, name=name, grid=grid, in_specs=[a_spec, b_spec] + [s for _, s in extras],
        out_specs=list(out_specs), out_shape=list(out_shape),
        scratch_shapes=[pltpu.VMEM(acc_shape, F32)] if nk > 1 else [],
        semantics=("parallel", "parallel", "arbitrary"), args=[a, b] + [e for e, _ in extras], jobs=jobs)
    return (outs, job_res) if jobs else outs


def _mm_nn(name, a, b, out_dtypes, tm, tn, tk, extras=(), epilogue=None, jobs=()):
    M, K = a.shape
    N = b.shape[1]
    tm, tn, tk = min(tm, M), min(tn, N), min(tk, K)
    o_spec = pl.BlockSpec((tm, tn), lambda i, j, k: (i, j))
    return _mm(name, (M // tm, N // tn, K // tk),
               a, pl.BlockSpec((tm, tk), lambda i, j, k: (i, k)),
               b, pl.BlockSpec((tk, tn), lambda i, j, k: (k, j)), NN, (tm, tn),
               [jax.ShapeDtypeStruct((M, N), dt) for dt in out_dtypes], [o_spec] * len(out_dtypes),
               [(e, o_spec) for e in extras], epilogue, jobs)


def _mm_nt(name, a, b, out_dtypes, tm, tn, tk, extras=(), epilogue=None, jobs=()):
    M, K = a.shape
    N = b.shape[0]
    tm, tn, tk = min(tm, M), min(tn, N), min(tk, K)
    o_spec = pl.BlockSpec((tm, tn), lambda i, j, k: (i, j))
    return _mm(name, (M // tm, N // tn, K // tk),
               a, pl.BlockSpec((tm, tk), lambda i, j, k: (i, k)),
               b, pl.BlockSpec((tn, tk), lambda i, j, k: (j, k)), NT, (tm, tn),
               [jax.ShapeDtypeStruct((M, N), dt) for dt in out_dtypes], [o_spec] * len(out_dtypes),
               [(e, o_spec) for e in extras], epilogue, jobs)


def _mm_tn(name, a, b, out_dtypes, tm, tn, tk, jobs=()):
    K, M = a.shape
    N = b.shape[1]
    tm, tn, tk = min(tm, M), min(tn, N), min(tk, K)
    o_spec = pl.BlockSpec((tm, tn), lambda i, j, k: (i, j))
    return _mm(name, (M // tm, N // tn, K // tk),
               a, pl.BlockSpec((tk, tm), lambda i, j, k: (k, i)),
               b, pl.BlockSpec((tk, tn), lambda i, j, k: (k, j)), TN, (tm, tn),
               [jax.ShapeDtypeStruct((M, N), dt) for dt in out_dtypes], [o_spec] * len(out_dtypes),
               epilogue=lambda acc: (acc,) * len(out_dtypes), jobs=jobs)


def _fgate_fwd(zf, bf):
    S = zf.shape[0]
    nc = S // CHUNK

    def body(zf_ref, bf_ref, f_ref):
        upper = (_iota2((CHUNK, CHUNK), 0) <= _iota2((CHUNK, CHUNK), 1)).astype(BF16)
        carry = jnp.zeros((8, 1), F32)
        for c in range(nc):
            t = zf_ref[c * CHUNK:(c + 1) * CHUNK, :] + bf_ref[...]
            lf = jnp.minimum(t, 0.0) - jnp.log(1.0 + jnp.exp(-jnp.abs(t)))
            lf_rows = lf.T[0:8, :]
            f_ref[:, c * CHUNK:(c + 1) * CHUNK] = (_dot3(lf_rows, upper) + carry) * LOG2E
            carry = carry + jnp.sum(lf_rows, axis=-1, keepdims=True)

    return pl.pallas_call(
        body, name="fgate_fwd", out_shape=jax.ShapeDtypeStruct((8, S), F32),
        compiler_params=_params(),
    )(zf, bf)


def _fgate_bwd(df, zf, bf):
    S = zf.shape[0]
    nc = S // CHUNK

    def body(df_ref, zf_ref, bf_ref, dzf_ref, dbf_ref):
        lower = (_iota2((CHUNK, CHUNK), 0) >= _iota2((CHUNK, CHUNK), 1)).astype(BF16)
        carry = jnp.zeros((8, 1), F32)
        dbf = jnp.zeros((1, LANES), F32)
        for c in reversed(range(nc)):
            sl = slice(c * CHUNK, (c + 1) * CHUNK)
            df = df_ref[:, sl]
            r = _dot3(df, lower) + carry
            carry = carry + jnp.sum(df, axis=-1, keepdims=True)
            r_cols = jnp.concatenate([r, jnp.zeros((CHUNK - 8, CHUNK), F32)], axis=0).T
            t = zf_ref[sl, :] + bf_ref[...]
            dz = r_cols * (1.0 / (1.0 + jnp.exp(t)))
            dzf_ref[sl, :] = dz.astype(BF16)
            dbf = dbf + jnp.sum(dz, axis=0, keepdims=True)
        dbf_ref[...] = dbf

    return pl.pallas_call(
        body, name="fgate_bwd",
        out_shape=[jax.ShapeDtypeStruct((S, LANES), BF16), jax.ShapeDtypeStruct((1, LANES), F32)],
        compiler_params=_params(),
    )(df, zf, bf)


_NEG = -1e30
LOG2E = 1.4426950408889634
N_SPLIT = 8
N_SPLIT_DIAG = 2
DIAG_STEP = 1024


def _attn_consts(T):
    rows, cols = _iota2((T, T), 0), _iota2((T, T), 1)
    return cols <= rows, rows <= cols


def _attn2_fwd(zm, f2col, f2row, T, jobs=()):
    S = zm.shape[0]
    H = f2col.shape[0]
    nb = S // T
    c2 = LOG2E / math.sqrt(HEAD_DIM)

    def body(q_ref, k_ref, v_ref, fq_ref, fk_ref, o_ref, lse_ref, vaug_s):
        i = pl.program_id(1)

        @pl.when(i == 0)
        def _():
            vaug_s[:, :HEAD_DIM] = v_ref[...]
            vaug_s[:, HEAD_DIM:] = jnp.ones((S, HEAD_DIM), BF16)

        keep = _attn_consts(T)[0]
        TH = T // N_SPLIT

        def block(j, diagonal, state):
            r0 = pl.multiple_of(j * T, T)
            fk = fk_ref[j]
            new = []
            for g, (m_old, acc) in enumerate(state):
                rows = slice(g * TH, (g + 1) * TH)
                nk = min(T, -(-(g + 1) * TH // DIAG_STEP) * DIAG_STEP) if diagonal else T
                s = _dot(q_ref[rows, :], k_ref[pl.ds(r0, nk), :], NT) * c2 + (fq_ref[rows, :] - fk[:, :nk])
                if diagonal:
                    s = jnp.where(keep[rows, :nk], s, _NEG)
                m_new = jnp.maximum(m_old, jnp.max(s, axis=-1, keepdims=True))
                p = jnp.exp2(s - m_new).astype(BF16)
                new.append((m_new, jnp.exp2(m_old - m_new) * acc + _dot(p, vaug_s[pl.ds(r0, nk), :])))
            return tuple(new)

        init = tuple((jnp.full((TH, 1), _NEG, F32), jnp.zeros((TH, 2 * HEAD_DIM), F32)) for _ in range(N_SPLIT))
        state = lax.fori_loop(0, i, lambda j, st: block(j, False, st), init)
        state = block(i, True, state)
        for g, (m, acc) in enumerate(state):
            rows = slice(g * TH, (g + 1) * TH)
            o_ref[rows, :] = acc[:, :HEAD_DIM] / acc[:, HEAD_DIM:]
            lse_ref[rows, :] = m + jnp.log2(acc[:, HEAD_DIM:HEAD_DIM + 1])

    nh = H
    return _carry_call(
        body, name="attn_fwd", grid=(H, nb), jobs=jobs, args=[zm, zm, zm, f2col, f2row],
        semantics=("arbitrary", "arbitrary"),
        in_specs=[
            pl.BlockSpec((T, HEAD_DIM), lambda h, i: (i, h)),
            pl.BlockSpec((S, HEAD_DIM), lambda h, i: (0, nh + h)),
            pl.BlockSpec((S, HEAD_DIM), lambda h, i: (0, 2 * nh + h)),
            pl.BlockSpec((None, T, 1), lambda h, i: (h, i, 0)),
            pl.BlockSpec((None, nb, 1, T), lambda h, i: (h, 0, 0, 0)),
        ],
        out_specs=[pl.BlockSpec((T, HEAD_DIM), lambda h, i: (i, h)), pl.BlockSpec((None, T, 1), lambda h, i: (h, i, 0))],
        out_shape=[jax.ShapeDtypeStruct((S, H * HEAD_DIM), F32), jax.ShapeDtypeStruct((H, S, 1), F32)],
        scratch_shapes=[pltpu.VMEM((S, 2 * HEAD_DIM), BF16)],
    )


def _attn2_bwd_dq(zm, dattn, f2col, f2row, lse2_col, delta_col, T, jobs=()):
    S = zm.shape[0]
    H = f2col.shape[0]
    nb = S // T
    scale = 1.0 / math.sqrt(HEAD_DIM)
    c2 = LOG2E * scale

    def body(q_ref, k_ref, v_ref, do_ref, fq_ref, fk_ref, lse_ref, dl_ref, dq_ref, rs_ref, bias_s, do_s):
        i = pl.program_id(1)
        keep = _attn_consts(T)[0]
        TH = T // N_SPLIT_DIAG
        bias_s[...] = fq_ref[...] - lse_ref[...]
        do_s[...] = do_ref[...].astype(BF16)

        def part(rows, j, nk, state, masked):
            acc, rs = state
            r0 = pl.multiple_of(j * T, T)
            kb = k_ref[pl.ds(r0, nk), :]
            s = _dot(q_ref[rows, :], kb, NT) * c2 + (bias_s[rows, :] - fk_ref[j][:, :nk])
            if masked:
                s = jnp.where(keep[rows, :nk], s, _NEG)
            ds = jnp.exp2(s) * (_dot(do_s[rows, :], v_ref[pl.ds(r0, nk), :], NT) - dl_ref[rows, :])
            return acc + _dot(ds.astype(BF16), kb), rs + jnp.sum(ds, axis=-1, keepdims=True)

        def step(j, state):
            return part(slice(0, T), j, T, state, False)

        acc, rs = lax.fori_loop(0, i, step, (jnp.zeros((T, HEAD_DIM), F32), jnp.zeros((T, 1), F32)))
        for g in range(N_SPLIT_DIAG):
            rows = slice(g * TH, (g + 1) * TH)
            acc_g, rs_g = part(rows, i, (g + 1) * TH, (acc[rows, :], rs[rows, :]), True)
            dq_ref[rows, :] = (acc_g * scale).astype(BF16)
            rs_ref[rows, :] = rs_g

    nh = H
    col = pl.BlockSpec((None, T, 1), lambda h, i: (h, i, 0))
    blk = pl.BlockSpec((T, HEAD_DIM), lambda h, i: (i, h))
    return _carry_call(
        body, name="attn_bwd_dq", grid=(H, nb), jobs=jobs,
        args=[zm, zm, zm, dattn, f2col, f2row, lse2_col, delta_col], semantics=("arbitrary", "arbitrary"),
        in_specs=[
            blk,
            pl.BlockSpec((S, HEAD_DIM), lambda h, i: (0, nh + h)),
            pl.BlockSpec((S, HEAD_DIM), lambda h, i: (0, 2 * nh + h)),
            blk, col,
            pl.BlockSpec((None, nb, 1, T), lambda h, i: (h, 0, 0, 0)),
            col, col,
        ],
        out_specs=[blk, col],
        out_shape=[jax.ShapeDtypeStruct((S, H * HEAD_DIM), BF16), jax.ShapeDtypeStruct((H, S, 1), F32)],
        scratch_shapes=[pltpu.VMEM((T, 1), F32), pltpu.VMEM((T, HEAD_DIM), BF16)],
    )


def _attn2_bwd_dkv(zm, dattn, f2col, f2row, lse2_row, delta_row, rowsum_row, T, jobs=()):
    S = zm.shape[0]
    H = f2col.shape[0]
    nb = S // T
    scale = 1.0 / math.sqrt(HEAD_DIM)
    c2 = LOG2E * scale

    def body(q_ref, k_ref, v_ref, do_ref, fk_ref, fq_ref, lse_ref, dl_ref, rs_ref, dk_ref, dv_ref, df_ref):
        j = pl.program_id(1)
        keep = _attn_consts(T)[1]
        TH = T // N_SPLIT_DIAG

        def part(rows, i, c0, state, masked):
            dk, dv, df = state
            r0 = pl.multiple_of(i * T + c0, TH)
            qb = q_ref[pl.ds(r0, T - c0), :]
            do = do_ref[pl.ds(r0, T - c0), :].astype(BF16)
            bias = (fq_ref[i] - lse_ref[i])[:, c0:]
            dl = (dl_ref[i] + rs_ref[i])[:, c0:]
            st = _dot(k_ref[rows, :], qb, NT) * c2 + (bias - fk_ref[rows, :])
            if masked:
                st = jnp.where(keep[rows, c0:], st, _NEG)
            pt = jnp.exp2(st)
            dst = pt * (_dot(v_ref[rows, :], do, NT) - dl)
            return (dk + _dot(dst.astype(BF16), qb), dv + _dot(pt.astype(BF16), do),
                    df - jnp.sum(dst, axis=-1, keepdims=True))

        groups = []
        for g in range(N_SPLIT_DIAG):
            zero = (jnp.zeros((TH, HEAD_DIM), F32), jnp.zeros((TH, HEAD_DIM), F32), jnp.zeros((TH, 1), F32))
            groups.append(part(slice(g * TH, (g + 1) * TH), j, g * TH, zero, True))
        state = tuple(jnp.concatenate([grp[n] for grp in groups], axis=0) for n in range(3))
        dk, dv, df = lax.fori_loop(j + 1, nb, lambda i, st: part(slice(0, T), i, 0, st, False), state)
        dk_ref[...] = (dk * scale).astype(BF16)
        dv_ref[...] = dv.astype(BF16)
        df_ref[...] = df

    nh = H
    row = pl.BlockSpec((None, nb, 1, T), lambda h, j: (h, 0, 0, 0))
    whole = pl.BlockSpec((S, HEAD_DIM), lambda h, j: (0, h))
    kv_out = pl.BlockSpec((T, HEAD_DIM), lambda h, j: (j, h))
    col = pl.BlockSpec((None, T, 1), lambda h, j: (h, j, 0))
    return _carry_call(
        body, name="attn_bwd_dkv", grid=(H, nb), jobs=jobs,
        args=[zm, zm, zm, dattn, f2col, f2row, lse2_row, delta_row, rowsum_row],
        semantics=("arbitrary", "arbitrary"),
        in_specs=[
            whole,
            pl.BlockSpec((T, HEAD_DIM), lambda h, j: (j, nh + h)),
            pl.BlockSpec((T, HEAD_DIM), lambda h, j: (j, 2 * nh + h)),
            whole, col, row, row, row, row,
        ],
        out_specs=[kv_out, kv_out, col],
        out_shape=[jax.ShapeDtypeStruct((S, H * HEAD_DIM), BF16), jax.ShapeDtypeStruct((S, H * HEAD_DIM), BF16),
                   jax.ShapeDtypeStruct((H, S, 1), F32)],
        scratch_shapes=[],
    )


def _attn_fwd(zm, fcol, frow, T, jobs=()):
    S = zm.shape[0]
    H = fcol.shape[0]
    nb = S // T
    scale = 1.0 / math.sqrt(HEAD_DIM)

    def body(q_ref, k_ref, v_ref, fq_ref, fk_ref, o_ref, lse_ref, m_s, l_s, acc_s):
        i = pl.program_id(1)
        j = pl.program_id(2)

        @pl.when(j == 0)
        def _():
            m_s[...] = jnp.full_like(m_s, _NEG)
            l_s[...] = jnp.zeros_like(l_s)
            acc_s[...] = jnp.zeros_like(acc_s)

        @pl.when(j <= i)
        def _():
            s = _dot(q_ref[...], k_ref[...], NT) * scale + (fq_ref[...] - fk_ref[...])
            keep = (_iota2((T, T), 1) + j * T) <= (_iota2((T, T), 0) + i * T)
            s = jnp.where(keep, s, _NEG)
            m_new = jnp.maximum(m_s[...], jnp.max(s, axis=-1, keepdims=True))
            alpha = jnp.exp(m_s[...] - m_new)
            p = jnp.exp(s - m_new)
            l_s[...] = alpha * l_s[...] + jnp.sum(p, axis=-1, keepdims=True)
            acc_s[...] = alpha * acc_s[...] + _dot(p.astype(BF16), v_ref[...])
            m_s[...] = m_new

        @pl.when(j == nb - 1)
        def _():
            o_ref[...] = acc_s[...] / l_s[...]
            lse_ref[...] = m_s[...] + jnp.log(l_s[...])

    nh = H
    return _carry_call(
        body, name="attn_fwd", grid=(H, nb, nb), jobs=jobs, args=[zm, zm, zm, fcol, frow],
        semantics=("parallel", "parallel", "arbitrary"),
        in_specs=[
            pl.BlockSpec((T, HEAD_DIM), lambda h, i, j: (i, h)),
            pl.BlockSpec((T, HEAD_DIM), lambda h, i, j: (jnp.minimum(j, i), nh + h)),
            pl.BlockSpec((T, HEAD_DIM), lambda h, i, j: (jnp.minimum(j, i), 2 * nh + h)),
            pl.BlockSpec((None, T, 1), lambda h, i, j: (h, i, 0)),
            pl.BlockSpec((None, 1, T), lambda h, i, j: (h, 0, jnp.minimum(j, i))),
        ],
        out_specs=[
            pl.BlockSpec((T, HEAD_DIM), lambda h, i, j: (i, h)),
            pl.BlockSpec((None, T, 1), lambda h, i, j: (h, i, 0)),
        ],
        out_shape=[jax.ShapeDtypeStruct((S, H * HEAD_DIM), F32), jax.ShapeDtypeStruct((H, S, 1), F32)],
        scratch_shapes=[pltpu.VMEM((T, 1), F32), pltpu.VMEM((T, 1), F32), pltpu.VMEM((T, HEAD_DIM), F32)],
    )


def _attn_delta(dattn, attn, tr):
    S, DA = attn.shape
    H = DA // HEAD_DIM

    def body(do_ref, o_ref, out_ref):
        lo = _iota2((DA, LANES), 1) * HEAD_DIM
        sel = ((_iota2((DA, LANES), 0) >= lo) & (_iota2((DA, LANES), 0) < lo + HEAD_DIM)).astype(BF16)
        d = _dot3(do_ref[...] * o_ref[...], sel)
        for c in range(tr // CHUNK):
            out_ref[:, c * CHUNK:(c + 1) * CHUNK] = d[c * CHUNK:(c + 1) * CHUNK, :].T[0:H, :]

    return pl.pallas_call(
        body, name="attn_delta", grid=(S // tr,),
        in_specs=[pl.BlockSpec((tr, DA), lambda i: (i, 0))] * 2,
        out_specs=pl.BlockSpec((H, tr), lambda i: (0, i)),
        out_shape=jax.ShapeDtypeStruct((H, S), F32),
        compiler_params=_params(("parallel",)),
    )(dattn, attn)


def _attn_bwd_dq(zm, dattn, fcol, frow, lse_col, delta_col, T, jobs=()):
    S = zm.shape[0]
    H = fcol.shape[0]
    nb = S // T
    scale = 1.0 / math.sqrt(HEAD_DIM)

    def body(q_ref, k_ref, v_ref, do_ref, fq_ref, fk_ref, lse_ref, dl_ref, dq_ref, rs_ref, acc_s, rs_s):
        i = pl.program_id(1)
        j = pl.program_id(2)

        @pl.when(j == 0)
        def _():
            acc_s[...] = jnp.zeros_like(acc_s)
            rs_s[...] = jnp.zeros_like(rs_s)

        @pl.when(j <= i)
        def _():
            s = _dot(q_ref[...], k_ref[...], NT) * scale + (fq_ref[...] - fk_ref[...])
            keep = (_iota2((T, T), 1) + j * T) <= (_iota2((T, T), 0) + i * T)
            p = jnp.exp(jnp.where(keep, s - lse_ref[...], _NEG))
            dp = _dot(do_ref[...].astype(BF16), v_ref[...], NT)
            ds = p * (dp - dl_ref[...])
            acc_s[...] += _dot(ds.astype(BF16), k_ref[...])
            rs_s[...] += jnp.sum(ds, axis=-1, keepdims=True)

        @pl.when(j == nb - 1)
        def _():
            dq_ref[...] = (acc_s[...] * scale).astype(BF16)
            rs_ref[...] = rs_s[...]

    nh = H
    col = pl.BlockSpec((None, T, 1), lambda h, i, j: (h, i, 0))
    return _carry_call(
        body, name="attn_bwd_dq", grid=(H, nb, nb), jobs=jobs,
        args=[zm, zm, zm, dattn, fcol, frow, lse_col, delta_col], semantics=("parallel", "parallel", "arbitrary"),
        in_specs=[
            pl.BlockSpec((T, HEAD_DIM), lambda h, i, j: (i, h)),
            pl.BlockSpec((T, HEAD_DIM), lambda h, i, j: (jnp.minimum(j, i), nh + h)),
            pl.BlockSpec((T, HEAD_DIM), lambda h, i, j: (jnp.minimum(j, i), 2 * nh + h)),
            pl.BlockSpec((T, HEAD_DIM), lambda h, i, j: (i, h)),
            col,
            pl.BlockSpec((None, 1, T), lambda h, i, j: (h, 0, jnp.minimum(j, i))),
            col, col,
        ],
        out_specs=[pl.BlockSpec((T, HEAD_DIM), lambda h, i, j: (i, h)), col],
        out_shape=[jax.ShapeDtypeStruct((S, H * HEAD_DIM), BF16), jax.ShapeDtypeStruct((H, S, 1), F32)],
        scratch_shapes=[pltpu.VMEM((T, HEAD_DIM), F32), pltpu.VMEM((T, 1), F32)],
    )


def _attn_bwd_dkv(zm, dattn, fcol, frow, lse_row, delta_row, rowsum_row, T, jobs=()):
    S = zm.shape[0]
    H = fcol.shape[0]
    nb = S // T
    scale = 1.0 / math.sqrt(HEAD_DIM)

    def body(q_ref, k_ref, v_ref, do_ref, fk_ref, fq_ref, lse_ref, dl_ref, rs_ref,
             dk_ref, dv_ref, df_ref, dk_s, dv_s, df_s):
        j = pl.program_id(1)
        i = pl.program_id(2)

        @pl.when(i == 0)
        def _():
            dk_s[...] = jnp.zeros_like(dk_s)
            dv_s[...] = jnp.zeros_like(dv_s)
            df_s[...] = jnp.zeros_like(df_s)

        @pl.when(i >= j)
        def _():
            st = _dot(k_ref[...], q_ref[...], NT) * scale + (fq_ref[...] - fk_ref[...])
            keep = (_iota2((T, T), 0) + j * T) <= (_iota2((T, T), 1) + i * T)
            pt = jnp.exp(jnp.where(keep, st - lse_ref[...], _NEG))
            do = do_ref[...].astype(BF16)
            dpt = _dot(v_ref[...], do, NT)
            dst = pt * (dpt - (dl_ref[...] + rs_ref[...]))
            dv_s[...] += _dot(pt.astype(BF16), do)
            dk_s[...] += _dot(dst.astype(BF16), q_ref[...])
            df_s[...] -= jnp.sum(dst, axis=-1, keepdims=True)

        @pl.when(i == nb - 1)
        def _():
            dk_ref[...] = (dk_s[...] * scale).astype(BF16)
            dv_ref[...] = dv_s[...].astype(BF16)
            df_ref[...] = df_s[...]

    nh = H
    row = pl.BlockSpec((None, 1, T), lambda h, j, i: (h, 0, jnp.maximum(i, j)))
    kv_out = pl.BlockSpec((T, HEAD_DIM), lambda h, j, i: (j, h))
    return _carry_call(
        body, name="attn_bwd_dkv", grid=(H, nb, nb), jobs=jobs,
        args=[zm, zm, zm, dattn, fcol, frow, lse_row, delta_row, rowsum_row],
        semantics=("parallel", "parallel", "arbitrary"),
        in_specs=[
            pl.BlockSpec((T, HEAD_DIM), lambda h, j, i: (jnp.maximum(i, j), h)),
            pl.BlockSpec((T, HEAD_DIM), lambda h, j, i: (j, nh + h)),
            pl.BlockSpec((T, HEAD_DIM), lambda h, j, i: (j, 2 * nh + h)),
            pl.BlockSpec((T, HEAD_DIM), lambda h, j, i: (jnp.maximum(i, j), h)),
            pl.BlockSpec((None, T, 1), lambda h, j, i: (h, j, 0)),
            row, row, row, row,
        ],
        out_specs=[kv_out, kv_out, pl.BlockSpec((None, T, 1), lambda h, j, i: (h, j, 0))],
        out_shape=[jax.ShapeDtypeStruct((S, H * HEAD_DIM), BF16), jax.ShapeDtypeStruct((S, H * HEAD_DIM), BF16),
                   jax.ShapeDtypeStruct((H, S, 1), F32)],
        scratch_shapes=[pltpu.VMEM((T, HEAD_DIM), F32), pltpu.VMEM((T, HEAD_DIM), F32), pltpu.VMEM((T, 1), F32)],
    )


def _ln_stats(x):
    mu = jnp.mean(x, axis=-1, keepdims=True)
    xc = x - mu
    rstd = lax.rsqrt(jnp.mean(xc * xc, axis=-1, keepdims=True) + EPS)
    return xc * rstd, rstd


def _tril_mask():
    return _iota2((CHUNK, CHUNK), 0) >= _iota2((CHUNK, CHUNK), 1)


def _gmlp_fwd(zm, ln_g, ln_b, w_s, bs_col, tr):
    S = zm.shape[0]
    H = w_s.shape[0]
    DG = H * HEAD_DIM

    def body(zu_ref, zv_ref, g_ref, b_ref, w_ref, bs_ref, out_ref):
        u = _gelu(zu_ref[...].astype(F32))
        y, _ = _ln_stats(_gelu(zv_ref[...].astype(F32)))
        v = (y * g_ref[...] + b_ref[...]).astype(BF16)
        mask = _tril_mask()
        for h in range(H):
            wc = jnp.where(mask, w_ref[h], 0.0).astype(BF16)
            cs = slice(h * HEAD_DIM, (h + 1) * HEAD_DIM)
            for c in range(tr // CHUNK):
                rs = slice(c * CHUNK, (c + 1) * CHUNK)
                mix = _dot(wc, v[rs, cs]) + bs_ref[h]
                out_ref[rs, cs] = u[rs, cs] * mix

    full = lambda a: pl.BlockSpec(a.shape, lambda i: (0,) * a.ndim)
    return pl.pallas_call(
        body, name="gmlp_fwd", grid=(S // tr,),
        in_specs=[pl.BlockSpec((tr, DG), lambda i: (i, 3)), pl.BlockSpec((tr, DG), lambda i: (i, 4)),
                  full(ln_g), full(ln_b), full(w_s), full(bs_col)],
        out_specs=pl.BlockSpec((tr, DG), lambda i: (i, 0)),
        out_shape=jax.ShapeDtypeStruct((S, DG), F32),
        compiler_params=_params(("parallel",)),
    )(zm, zm, ln_g, ln_b, w_s, bs_col)


def _gmlp_bwd(dgm, zm, ln_g, ln_b, w_s, w_st, bs_col, tr):
    S = zm.shape[0]
    H = w_s.shape[0]
    DG = H * HEAD_DIM

    def body(dg_ref, zu_ref, zv_ref, g_ref, b_ref, w_ref, wt_ref, bs_ref,
             dzu_ref, dzv_ref, dw_ref, dbs_ref, dlg_ref, dlb_ref, dv_s):
        @pl.when(pl.program_id(0) == 0)
        def _():
            dw_ref[...] = jnp.zeros_like(dw_ref)
            dbs_ref[...] = jnp.zeros_like(dbs_ref)
            dlg_ref[...] = jnp.zeros_like(dlg_ref)
            dlb_ref[...] = jnp.zeros_like(dlb_ref)

        zu = zu_ref[...].astype(F32)
        zv = zv_ref[...].astype(F32)
        u = _gelu(zu)
        y, rstd = _ln_stats(_gelu(zv))
        v = (y * g_ref[...] + b_ref[...]).astype(BF16)
        dgm_blk = dg_ref[...]
        mask = _tril_mask()
        mask_t = _iota2((CHUNK, CHUNK), 0) <= _iota2((CHUNK, CHUNK), 1)
        for h in range(H):
            wc = jnp.where(mask, w_ref[h], 0.0).astype(BF16)
            wct = jnp.where(mask_t, wt_ref[h], 0.0).astype(BF16)
            cs = slice(h * HEAD_DIM, (h + 1) * HEAD_DIM)
            dw = jnp.zeros((CHUNK, CHUNK), F32)
            dbs = jnp.zeros((CHUNK, 1), F32)
            for c in range(tr // CHUNK):
                rs = slice(c * CHUNK, (c + 1) * CHUNK)
                vch = v[rs, cs]
                mix = _dot(wc, vch) + bs_ref[h]
                dg = dgm_blk[rs, cs]
                dzu_ref[rs, cs] = (dg * mix * _gelu_grad(zu[rs, cs])).astype(BF16)
                dmix = dg * u[rs, cs]
                dbs = dbs + jnp.sum(dmix, axis=-1, keepdims=True)
                dmix_b = dmix.astype(BF16)
                dw = dw + _dot(dmix_b, vch, NT)
                dv_s[rs, cs] = _dot(wct, dmix_b)
            dw_ref[h] += jnp.where(mask, dw, 0.0)
            dbs_ref[h] += dbs
        dv = dv_s[...]
        dlg_ref[...] += jnp.sum(dv * y, axis=0, keepdims=True)
        dlb_ref[...] += jnp.sum(dv, axis=0, keepdims=True)
        dy = dv * g_ref[...]
        dgv = rstd * (dy - jnp.mean(dy, axis=-1, keepdims=True) - y * jnp.mean(dy * y, axis=-1, keepdims=True))
        dzv_ref[...] = (dgv * _gelu_grad(zv)).astype(BF16)

    full = lambda a: pl.BlockSpec(a.shape, lambda i: (0,) * a.ndim)
    rows = pl.BlockSpec((tr, DG), lambda i: (i, 0))
    return pl.pallas_call(
        body, name="gmlp_bwd", grid=(S // tr,),
        in_specs=[rows, pl.BlockSpec((tr, DG), lambda i: (i, 3)), pl.BlockSpec((tr, DG), lambda i: (i, 4)),
                  full(ln_g), full(ln_b), full(w_s), full(w_st), full(bs_col)],
        out_specs=[rows, rows, full(w_s), full(bs_col), full(ln_g), full(ln_b)],
        out_shape=[jax.ShapeDtypeStruct((S, DG), BF16), jax.ShapeDtypeStruct((S, DG), BF16),
                   jax.ShapeDtypeStruct(w_s.shape, F32), jax.ShapeDtypeStruct(bs_col.shape, F32),
                   jax.ShapeDtypeStruct(ln_g.shape, F32), jax.ShapeDtypeStruct(ln_b.shape, F32)],
        scratch_shapes=[pltpu.VMEM((tr, DG), F32)],
        compiler_params=_params(("arbitrary",)),
    )(dgm, zm, zm, ln_g, ln_b, w_s, w_st, bs_col)


def _all_gather(name, blk):
    R, C = blk.shape

    def body(x_ref, out_ref, send_sems, recv_sems, local_sem):
        x, y, c = _me()
        me, sibling = (x, y, c), (x, y, 1 - c)
        chips = [(1 - x, y), (x, 1 - y), (1 - x, 1 - y)]

        def slab(px, py, pc):
            return out_ref.at[4 * px + 2 * py + pc]

        def copy(k, block, to, src=None):
            return pltpu.make_async_remote_copy(
                src_ref=slab(*block) if src is None else src, dst_ref=slab(*block),
                send_sem=send_sems.at[k], recv_sem=recv_sems.at[k], device_id=to, device_id_type=MESH)

        mine = pltpu.make_async_copy(x_ref, slab(*me), local_sem)
        mine.start()
        first = [copy(0, me, sibling, src=x_ref)]
        first += [copy(1 + n, me, (*chip, c), src=x_ref) for n, chip in enumerate(chips)]
        for cp in first:
            cp.start()
        passed = [copy(4 + n, (*chip, c), sibling) for n, chip in enumerate(chips)]
        for n, chip in enumerate(chips):
            copy(1 + n, (*chip, c), me).wait_recv()
            passed[n].start()
        copy(0, sibling, me).wait_recv()
        for n, chip in enumerate(chips):
            copy(4 + n, (*chip, 1 - c), me).wait_recv()
        for cp in first + passed:
            cp.wait_send()
        mine.wait()

    return pl.pallas_call(
        body, name=name, out_shape=jax.ShapeDtypeStruct((N_DEV, R, C), blk.dtype),
        in_specs=[_ANY], out_specs=_ANY,
        scratch_shapes=[pltpu.SemaphoreType.DMA((7,)), pltpu.SemaphoreType.DMA((7,)), pltpu.SemaphoreType.DMA(())],
    )(blk)


def _row_tile(R, C, itemsize=4, target_bytes=2 * 1024 * 1024):
    tr = R
    while tr % 2 == 0 and tr * C * itemsize > target_bytes and (tr // 2) % 16 == 0:
        tr //= 2
    return tr


def _rs_add1(name, g4, recv, c_idx):
    _, _, R, C = g4.shape
    tr = _row_tile(R, C)

    def body(c_ref, g_ref, r_ref, h_ref, hb_ref):
        h = g_ref[...] + r_ref[...].astype(F32)
        h_ref[...] = h
        hb_ref[...] = h.astype(BF16)

    blk = pl.BlockSpec((None, tr, C), lambda p, i, c_ref: (p, i, 0))
    return pl.pallas_call(
        body, name=name,
        grid_spec=pltpu.PrefetchScalarGridSpec(
            num_scalar_prefetch=1, grid=(4, R // tr),
            in_specs=[pl.BlockSpec((None, None, tr, C), lambda p, i, c_ref: (p, c_ref[0], i, 0)), blk],
            out_specs=[blk, blk]),
        out_shape=[jax.ShapeDtypeStruct((4, R, C), F32), jax.ShapeDtypeStruct((4, R, C), BF16)],
        compiler_params=_params(("parallel", "parallel")),
    )(c_idx, g4, recv)


def _rs_add1_windows(name, g, recv, first_blocks):
    _, R, W = recv.shape

    def body(t_ref, g_ref, r_ref, h_ref, hb_ref):
        h = g_ref[...] + r_ref[...].astype(F32)
        h_ref[...] = h
        hb_ref[...] = h.astype(BF16)

    blk = pl.BlockSpec((None, R, LANES), lambda p, l, t_ref: (p, 0, l))
    return pl.pallas_call(
        body, name=name,
        grid_spec=pltpu.PrefetchScalarGridSpec(
            num_scalar_prefetch=1, grid=(4, W // LANES),
            in_specs=[pl.BlockSpec((R, LANES), lambda p, l, t_ref: (0, t_ref[p] + l)), blk],
            out_specs=[blk, blk]),
        out_shape=[jax.ShapeDtypeStruct((4, R, W), F32), jax.ShapeDtypeStruct((4, R, W), BF16)],
        compiler_params=_params(("parallel", "parallel")),
    )(first_blocks, g, recv)


def _rs_add2(name, h, recv, p_idx):
    _, R, C = h.shape
    tr = _row_tile(R, C)

    def body(p_ref, h_ref, r_ref, out_ref):
        out_ref[...] = ((h_ref[...] + r_ref[0].astype(F32)) + r_ref[1].astype(F32)) + r_ref[2].astype(F32)

    return pl.pallas_call(
        body, name=name,
        grid_spec=pltpu.PrefetchScalarGridSpec(
            num_scalar_prefetch=1, grid=(R // tr,),
            in_specs=[pl.BlockSpec((None, tr, C), lambda i, p_ref: (p_ref[0], i, 0)),
                      pl.BlockSpec((3, tr, C), lambda i, p_ref: (0, i, 0))],
            out_specs=pl.BlockSpec((tr, C), lambda i, p_ref: (i, 0))),
        out_shape=jax.ShapeDtypeStruct((R, C), F32),
        compiler_params=_params(("parallel",)),
    )(p_idx, h, recv)


def _sum8(name, g):
    _, R, C = g.shape

    def body(g_ref, out_ref):
        acc = g_ref[0]
        for d in range(1, N_DEV):
            acc = acc + g_ref[d]
        out_ref[...] = acc

    return pl.pallas_call(body, name=name, out_shape=jax.ShapeDtypeStruct((R, C), F32),
                          compiler_params=_params())(g)


def _adamw(name, w, g, m, v):
    R, C = w.shape
    tr = _row_tile(R, C, target_bytes=1024 * 1024)

    def fn(w, g, m, v):
        m = ADAM_B1 * m + (1.0 - ADAM_B1) * g
        v = ADAM_B2 * v + (1.0 - ADAM_B2) * (g * g)
        m_hat = m / (1.0 - ADAM_B1 ** ADAM_STEP)
        v_hat = v / (1.0 - ADAM_B2 ** ADAM_STEP)
        delta = -ADAM_LR * (m_hat / (jnp.sqrt(v_hat) + ADAM_EPS) + ADAM_WD * w)
        return (delta, m, v), ()

    return _row_call(name, fn, [w, g, m, v], [], [(C, F32)] * 3, [], tr)


def _pack(parts):
    flat = []
    total = 0
    for a in parts:
        n = math.prod(a.shape)
        flat.append(a.reshape(-1).astype(F32))
        if n % LANES:
            flat.append(jnp.zeros((-n % LANES,), F32))
        total += n + (-n % LANES)
    if total % (8 * LANES):
        flat.append(jnp.zeros((-total % (8 * LANES),), F32))
    return jnp.concatenate(flat).reshape(-1, LANES)


def _unpack(packed, shapes):
    out = []
    r = 0
    for shp in shapes:
        n = math.prod(shp)
        nr = -(-n // LANES)
        out.append(packed[r:r + nr].reshape(-1)[:n].reshape(shp))
        r += nr
    return out


def kernel(x, norm_mix_g, w_in, b_f, gmlp_ln_g, gmlp_ln_b, w_s, b_s, attn_out_g, gmlp_out_g, w_out, norm_ffn_g, w_ff1, w_ff2, norm_final_g, loss_target, m_norm_mix_g, m_w_in, m_b_f, m_gmlp_ln_g, m_gmlp_ln_b, m_w_s, m_b_s, m_attn_out_g, m_gmlp_out_g, m_w_out, m_norm_ffn_g, m_w_ff1, m_w_ff2, m_norm_final_g, v_norm_mix_g, v_w_in, v_b_f, v_gmlp_ln_g, v_gmlp_ln_b, v_w_s, v_b_s, v_attn_out_g, v_gmlp_out_g, v_w_out, v_norm_ffn_g, v_w_ff1, v_w_ff2, v_norm_final_g):
    S, D = x.shape[1], x.shape[2]
    H = b_f.shape[1]
    DA = H * HEAD_DIM
    DG = gmlp_ln_g.shape[1]
    DQKV = 3 * DA
    DMAIN = DQKV + 2 * DG
    DIN = DMAIN + H
    DFF = w_ff1.shape[2] * N_DEV
    w_in_cols = w_in.shape[2]
    assert DIN == w_in_cols * N_DEV and DA == DG and D == DA + DG

    T_ATT = min(T_ATT_MAX, S)
    TR = min(TR_MAX, S)

    x0 = x[0]
    tgt = loss_target[0]
    g_final = norm_final_g.reshape(1, D)

    w_in_all = _all_gather("ag_w_in", w_in[0].astype(BF16))
    w_in_full = jnp.concatenate([w_in_all[n] for n in range(N_DEV)], axis=1)
    w_main = jnp.concatenate([w_in_full[:, :DQKV], w_in_full[:, DQKV + H:]], axis=1)
    w_f = jnp.pad(w_in_full[:, DQKV:DQKV + H], ((0, 0), (0, LANES - H)))
    FB = DFF // N_DEV
    x_pos, y_pos, c_pos = _me()
    c_idx = jnp.reshape(c_pos, (1,)).astype(jnp.int32)
    p_idx = jnp.reshape(2 * x_pos + y_pos, (1,)).astype(jnp.int32)

    (h,), _ = _row_call("rms_mix", lambda xb, g: ((_rms_fwd(xb, g),), ()), [x0], [norm_mix_g], [(D, BF16)], [], TR)
    (zm,), ((w_out_part,),) = _mm_nn("in_proj", h, w_main, [BF16], 1024, 1024, 2048,
                                     jobs=[_job_gather_chips(w_out[0].astype(BF16))])
    (zf,) = _mm_nn("in_proj_f", h, w_f, [F32], 1024, LANES, 2048)
    bf_pad = jnp.pad(b_f, ((0, 0), (0, LANES - H)))
    f_row = _fgate_fwd(zf, bf_pad)
    NB = S // T_ATT
    f_col3 = f_row.reshape(H, S, 1)
    f_row3 = f_row.reshape(H, NB, 1, T_ATT)
    (attn, lse_col3), ((w_out_all,), (w_ff1_part,)) = _attn2_fwd(
        zm, f_col3, f_row3, T_ATT, jobs=[_job_gather_sibling(w_out_part), _job_gather_chips(w_ff1[0].astype(BF16))])
    w_out_full = w_out_all.reshape(D, D)
    bs_col = b_s[0].reshape(H, CHUNK, 1)
    gm = _gmlp_fwd(zm, gmlp_ln_g, gmlp_ln_b, w_s[0], bs_col, TR)

    def merge_fn(a, g, ga, gg):
        return (jnp.concatenate([_rms_fwd(a, ga), _rms_fwd(g, gg)], axis=1),), ()
    (merged,), _ = _row_call("rms_merge", merge_fn, [attn, gm], [attn_out_g, gmlp_out_g], [(D, BF16)], [], TR)

    w_ff2_b = w_ff2[0].astype(BF16)
    (x1,), ((w_ff1_all,), (w_ff2_q1,)) = _mm_nn(
        "out_proj", merged, w_out_full, [F32], 1024, 1024, 2048, extras=[x0], epilogue=lambda acc, r: (acc + r,),
        jobs=[_job_gather_sibling(w_ff1_part), _job_gather_chips(w_ff2_b, part=(0, 1, 4))])
    (h2,), _ = _row_call("rms_ffn", lambda xb, g: ((_rms_fwd(xb, g),), ()), [x1], [norm_ffn_g], [(D, BF16)], [], TR)

    tm, tn, tk = min(1024, S), min(1024, FB), min(2048, D)
    o_spec = pl.BlockSpec((tm, tn), lambda i, j, k: (i, j))

    def relu_sq(acc):
        a = jnp.maximum(acc, 0.0)
        return a, a * a
    nj = FB // tn
    ff2_rest = [_job_gather_chips(w_ff2_b, part=(1, 4, 4), into=w_ff2_q1)]
    (a_act, a_sq), ((w_ff2_q2,),) = _mm(
        "ff1", (S // tm, DFF // tn, D // tk), h2, pl.BlockSpec((tm, tk), lambda i, j, k: (i, k)),
        w_ff1_all, pl.BlockSpec((None, tk, tn), lambda i, j, k: (j // nj, k, j % nj)), NN, (tm, tn),
        [jax.ShapeDtypeStruct((S, DFF), BF16)] * 2, [o_spec] * 2, epilogue=relu_sq, jobs=ff2_rest)
    (w_ff2_all,) = _run_jobs("ag_w_ff2_sibling", [_job_gather_sibling(w_ff2_q2)])[0]
    w_ff2_full = w_ff2_all.reshape(DFF, D)
    (x2,) = _mm_nn("ff2", a_sq, w_ff2_full, [F32], 1024, 1024, 2048, extras=[x1], epilogue=lambda acc, r: (acc + r,))

    def head_fn(xb, t, g):
        rstd = lax.rsqrt(jnp.mean(xb * xb, axis=-1, keepdims=True) + EPS)
        xhat = xb * rstd
        err = xhat * g - t
        loss = 0.5 * jnp.sum(jnp.mean(err * err, axis=-1, keepdims=True), axis=0, keepdims=True)
        dy = err * (1.0 / D)
        dg = jnp.sum(dy * xhat, axis=0, keepdims=True)
        dxhat = dy * g
        dx = rstd * (dxhat - xhat * jnp.mean(dxhat * xhat, axis=-1, keepdims=True))
        return (dx, dx), (dg, jnp.broadcast_to(loss, (1, LANES)))
    (dx2, dx2_b), (dg_final, loss_part) = _row_call(
        "loss_head", head_fn, [x2, tgt], [g_final], [(D, F32), (D, BF16)], [D, LANES], TR)

    (da,) = _mm_nt("ff2_dx", dx2_b, w_ff2_full, [BF16], 1024, 1024, 2048, extras=[a_act],
                   epilogue=lambda acc, a: (2.0 * a.astype(F32) * acc,))
    dw_ff2, dw_ff2_b = _mm_tn("ff2_dw", a_sq, dx2_b, [F32, BF16], 1024, 1024, 1024)
    tm2, tk2 = min(1024, D), min(1024, S)
    dw1_spec = pl.BlockSpec((None, tm2, FB), lambda i, j, k: (j, i, 0))
    (dw_ff1, dw_ff1_b), ((r1_ff2,),) = _mm(
        "ff1_dw", (D // tm2, DFF // FB, S // tk2), h2, pl.BlockSpec((tk2, tm2), lambda i, j, k: (k, i)),
        da, pl.BlockSpec((tk2, FB), lambda i, j, k: (k, j)), TN, (tm2, FB),
        [jax.ShapeDtypeStruct((N_DEV, D, FB), F32), jax.ShapeDtypeStruct((N_DEV, D, FB), BF16)], [dw1_spec] * 2,
        epilogue=lambda acc: (acc, acc), jobs=[_job_scatter_sibling(dw_ff2_b.reshape(4, 2, FB, D))])
    h_ff2, hb_ff2 = _rs_add1("rs_add1_w_ff2", dw_ff2.reshape(4, 2, FB, D), r1_ff2, c_idx)
    tkb = min(1024, FB)
    nkb = FB // tkb
    tnb = min(1024, D)
    (dh2,), ((r2_ff2,), (r1_ff1,)) = _mm(
        "ff1_dx", (S // tm, D // tnb, DFF // tkb), da, pl.BlockSpec((tm, tkb), lambda i, j, k: (i, k)),
        w_ff1_all, pl.BlockSpec((None, tnb, tkb), lambda i, j, k: (k // nkb, j, k % nkb)), NT, (tm, tnb),
        [jax.ShapeDtypeStruct((S, D), F32)], [pl.BlockSpec((tm, tnb), lambda i, j, k: (i, j))],
        jobs=[_job_scatter_chips(hb_ff2), _job_scatter_sibling(dw_ff1_b.reshape(4, 2, D, FB))])
    g_w_ff2 = _rs_add2("rs_add2_w_ff2", h_ff2, r2_ff2, p_idx)
    h_ff1, hb_ff1 = _rs_add1("rs_add1_w_ff1", dw_ff1.reshape(4, 2, D, FB), r1_ff1, c_idx)

    def ffn_bwd_fn(dh, xb, dres, g):
        dx, dg = _rms_bwd(dh, xb, g)
        dx = dx + dres
        return (dx, dx), (dg,)
    (dx1, dx1_b), (dg_ffn,) = _row_call("rms_ffn_bwd", ffn_bwd_fn, [dh2, x1, dx2], [norm_ffn_g],
                                        [(D, F32), (D, BF16)], [D], TR)

    (dmerged,) = _mm_nt("out_proj_dx", dx1_b, w_out_full, [F32], 1024, 1024, 2048)
    dw_out, dw_out_b = _mm_tn("out_proj_dw", merged, dx1_b, [F32, BF16], 1024, 1024, 1024)

    def merge_bwd_fn(dm, a, g, ga, gg):
        da_, dga = _rms_bwd(dm[:, :DA], a, ga)
        dg_, dgg = _rms_bwd(dm[:, DA:], g, gg)
        return (da_, dg_), (dga, dgg)
    (dattn, dgm), (dg_attn, dg_gmlp) = _row_call(
        "rms_merge_bwd", merge_bwd_fn, [dmerged, attn, gm], [attn_out_g, gmlp_out_g], [(DA, F32), (DG, F32)], [DA, DG], TR)

    w_st = jnp.swapaxes(w_s[0], 1, 2)
    dzu, dzv, dw_s, dbs_col, dln_g, dln_b = _gmlp_bwd(dgm, zm, gmlp_ln_g, gmlp_ln_b, w_s[0], w_st, bs_col, TR)

    delta_row = _attn_delta(dattn, attn, TR)
    lse_row3 = lse_col3.reshape(H, NB, 1, T_ATT)
    (dq, ds_rowsum), ((r2_ff1,), (r1_out,)) = _attn2_bwd_dq(
        zm, dattn, f_col3, f_row3, lse_col3, delta_row.reshape(H, S, 1), T_ATT,
        jobs=[_job_scatter_chips(hb_ff1), _job_scatter_sibling(dw_out_b.reshape(4, 2, D // N_DEV, D))])
    g_w_ff1 = _rs_add2("rs_add2_w_ff1", h_ff1, r2_ff1, p_idx)
    h_out, hb_out = _rs_add1("rs_add1_w_out", dw_out.reshape(4, 2, D // N_DEV, D), r1_out, c_idx)
    (dk, dv, df_col3), ((r2_out,),) = _attn2_bwd_dkv(
        zm, dattn, f_col3, f_row3, lse_row3, delta_row.reshape(H, NB, 1, T_ATT),
        ds_rowsum.reshape(H, NB, 1, T_ATT), T_ATT,
        jobs=[_job_scatter_chips(hb_out)])
    g_w_out = _rs_add2("rs_add2_w_out", h_out, r2_out, p_idx)
    dzf, dbf = _fgate_bwd(df_col3.reshape(H, S), zf, bf_pad)

    dz_main = jnp.concatenate([dq, dk, dv, dzu, dzv], axis=1)
    dw_main, dw_main_b = _mm_tn("in_proj_dw", h, dz_main, [F32, BF16], 1024, 1024, 1024)
    (dw_f,) = _mm_tn("in_proj_f_dw", h, dzf, [F32], 2048, LANES, 1024)
    WW = -(-(w_in_cols + LANES - 1) // LANES) * LANES
    to_main = lambda col: col if col <= DQKV else max(DQKV, col - H)
    lo = [to_main(n * w_in_cols) for n in range(N_DEV)]
    hi = [to_main((n + 1) * w_in_cols) for n in range(N_DEV)]
    starts = [v // LANES * LANES for v in lo]
    gate_dev = DQKV // w_in_cols
    assert all(hi[n] <= starts[n] + WW <= DMAIN for n in range(N_DEV))
    assert gate_dev * w_in_cols <= DQKV and DQKV + H <= (gate_dev + 1) * w_in_cols
    (dh_f,), ((r1_in,),) = _mm_nt("in_proj_f_dx", dzf, w_f, [F32], 1024, 1024, LANES,
                                  jobs=[_job_scatter_sibling_windows(dw_main_b, starts, WW)])
    first_blocks = jnp.stack([jnp.where(c_pos == 0, starts[2 * p], starts[2 * p + 1]) // LANES
                              for p in range(4)]).astype(jnp.int32)
    h_in, hb_in = _rs_add1_windows("rs_add1_w_in", dw_main, r1_in, first_blocks)
    (dh,), ((r2_in,),) = _mm_nt("in_proj_dx", dz_main, w_main, [F32], 1024, 1024, 1024, extras=[dh_f],
                                epilogue=lambda acc, r: (acc + r,), jobs=[_job_scatter_chips(hb_in)])
    g_window = _rs_add2("rs_add2_w_in", h_in, r2_in, p_idx)

    def mix_bwd_fn(dhb, xb, dres, g):
        dx, dg = _rms_bwd(dhb, xb, g)
        return (dx + dres,), (dg,)
    (grad_x,), (dg_mix,) = _row_call("rms_mix_bwd", mix_bwd_fn, [dh, x0, dx1], [norm_mix_g], [(D, F32)], [D], TR)

    small_shapes = [norm_mix_g.shape, b_f.shape, gmlp_ln_g.shape, gmlp_ln_b.shape, w_s.shape, b_s.shape,
                    attn_out_g.shape, gmlp_out_g.shape, norm_ffn_g.shape, norm_final_g.shape]
    small_parts = [dg_mix, dbf[:, :H], dln_g, dln_b, dw_s, dbs_col, dg_attn, dg_gmlp, dg_ffn, dg_final]
    g_small = _sum8("small_sum", _all_gather("ag_small", _pack(small_parts + [dw_f[:, :H]])))
    no_gate = jnp.zeros((D, H), F32)
    w_small = _pack([norm_mix_g, b_f, gmlp_ln_g, gmlp_ln_b, w_s, b_s, attn_out_g, gmlp_out_g, norm_ffn_g, norm_final_g,
                     no_gate])
    m_small = _pack([m_norm_mix_g, m_b_f, m_gmlp_ln_g, m_gmlp_ln_b, m_w_s, m_b_s, m_attn_out_g, m_gmlp_out_g,
                     m_norm_ffn_g, m_norm_final_g, no_gate])
    v_small = _pack([v_norm_mix_g, v_b_f, v_gmlp_ln_g, v_gmlp_ln_b, v_w_s, v_b_s, v_attn_out_g, v_gmlp_out_g,
                     v_norm_ffn_g, v_norm_final_g, no_gate])
    (d_small, nm_small, nv_small), _ = _adamw("adamw_small", w_small, g_small, m_small, v_small)
    g_gate = _unpack(g_small, small_shapes + [(D, H)])[-1]

    me_idx = 4 * x_pos + 2 * y_pos + c_pos
    off = sum(jnp.where(me_idx == n, lo[n] - starts[n], 0) for n in range(N_DEV))
    plain = lax.dynamic_slice(g_window, (0, off), (D, w_in_cols))
    g0 = lo[gate_dev] - starts[gate_dev]
    n_before = DQKV - gate_dev * w_in_cols
    with_gate = jnp.concatenate([g_window[:, g0:g0 + n_before], g_gate,
                                 g_window[:, g0 + n_before:g0 + w_in_cols - H]], axis=1)
    g_w_in = jnp.where(me_idx == gate_dev, with_gate, plain)
    gs = _unpack(g_small, small_shapes)
    ds = _unpack(d_small, small_shapes)
    nms = _unpack(nm_small, small_shapes)
    nvs = _unpack(nv_small, small_shapes)

    big = {}
    for nm, w, g, m, v in (("w_in", w_in, g_w_in, m_w_in, v_w_in), ("w_out", w_out, g_w_out, m_w_out, v_w_out),
                           ("w_ff1", w_ff1, g_w_ff1, m_w_ff1, v_w_ff1), ("w_ff2", w_ff2, g_w_ff2, m_w_ff2, v_w_ff2)):
        (d_, m_, v_), _ = _adamw("adamw_" + nm, w[0], g, m[0], v[0])
        big[nm] = (g[None], d_[None], m_[None], v_[None])

    loss = lax.psum(loss_part[0, 0], ("x", "y", "c"))

    def leaves(n):
        sm = (gs, ds, nms, nvs)[n]
        return [sm[0], big["w_in"][n], sm[1], sm[2], sm[3], sm[4], sm[5], sm[6], sm[7], big["w_out"][n], sm[8],
                big["w_ff1"][n], big["w_ff2"][n], sm[9]]

    return (loss, grad_x[None], *leaves(0), *leaves(1), *leaves(2), *leaves(3))
```

```python
import functools
import math

import jax
import jax.numpy as jnp
from jax import lax
from jax.experimental import pallas as pl
from jax.experimental.pallas import tpu as pltpu

F32 = jnp.float32
BF16 = jnp.bfloat16
MESH = pl.DeviceIdType.MESH

HEAD_DIM = 128
CHUNK = 128
EPS = 1e-6
LANES = 128
N_DEV = 8

ADAM_LR = 0.001
ADAM_B1 = 0.9
ADAM_B2 = 0.999
ADAM_EPS = 1e-08
ADAM_WD = 0.01
ADAM_STEP = 10

VMEM_LIMIT_BYTES = 56 * 1024 * 1024
T_ATT_MAX = 1024
TR_MAX = 256

NN = ((1,), (0,))
NT = ((1,), (1,))
TN = ((0,), (0,))


def _params(sem=None):
    return pltpu.CompilerParams(dimension_semantics=sem, vmem_limit_bytes=VMEM_LIMIT_BYTES)


def _dot(a, b, contract=NN):
    return lax.dot_general(a, b, (contract, ((), ())), preferred_element_type=F32)


def _dot3(x, t):
    x1 = x.astype(BF16)
    r1 = x - x1.astype(F32)
    x2 = r1.astype(BF16)
    x3 = (r1 - x2.astype(F32)).astype(BF16)
    return _dot(x1, t) + _dot(x2, t) + _dot(x3, t)


def _iota2(shape, dim):
    return lax.broadcasted_iota(jnp.int32, shape, dim)


def _row_call(name, fn, row_ins, bcast_ins, row_outs, acc_outs, tr):
    S = row_ins[0].shape[0]
    assert S % tr == 0
    n_ri, n_bi, n_ro, n_ao = len(row_ins), len(bcast_ins), len(row_outs), len(acc_outs)

    def body(*refs):
        ins = [r[...] for r in refs[:n_ri + n_bi]]
        ro_refs = refs[n_ri + n_bi:n_ri + n_bi + n_ro]
        ao_refs = refs[n_ri + n_bi + n_ro:]
        ro, ao = fn(*ins)
        for r, v in zip(ro_refs, ro):
            r[...] = v.astype(r.dtype)
        if n_ao:
            @pl.when(pl.program_id(0) == 0)
            def _():
                for r in ao_refs:
                    r[...] = jnp.zeros_like(r)
            for r, v in zip(ao_refs, ao):
                r[...] += v

    in_specs = [pl.BlockSpec((tr, a.shape[1]), lambda i: (i, 0)) for a in row_ins]
    in_specs += [pl.BlockSpec(a.shape, lambda i: (0, 0)) for a in bcast_ins]
    out_specs = [pl.BlockSpec((tr, d), lambda i: (i, 0)) for d, _ in row_outs]
    out_specs += [pl.BlockSpec((1, d), lambda i: (0, 0)) for d in acc_outs]
    out_shape = [jax.ShapeDtypeStruct((S, d), dt) for d, dt in row_outs]
    out_shape += [jax.ShapeDtypeStruct((1, d), F32) for d in acc_outs]
    outs = pl.pallas_call(
        body, name=name, grid=(S // tr,), in_specs=in_specs, out_specs=out_specs, out_shape=out_shape,
        compiler_params=_params(("arbitrary",) if n_ao else ("parallel",)),
    )(*row_ins, *bcast_ins)
    return outs[:n_ro], outs[n_ro:]


def _rms_fwd(x, g):
    rstd = lax.rsqrt(jnp.mean(x * x, axis=-1, keepdims=True) + EPS)
    return x * rstd * g


def _rms_bwd(dy, x, g):
    rstd = lax.rsqrt(jnp.mean(x * x, axis=-1, keepdims=True) + EPS)
    xhat = x * rstd
    dg = jnp.sum(dy * xhat, axis=0, keepdims=True)
    dxhat = dy * g
    dx = rstd * (dxhat - xhat * jnp.mean(dxhat * xhat, axis=-1, keepdims=True))
    return dx, dg


_GELU_C = math.sqrt(2.0 / math.pi)


def _gelu(x):
    return 0.5 * x * (1.0 + jnp.tanh(_GELU_C * (x + 0.044715 * (x * x * x))))


def _gelu_grad(x):
    t = jnp.tanh(_GELU_C * (x + 0.044715 * (x * x * x)))
    return 0.5 * (1.0 + t) + 0.5 * x * (1.0 - t * t) * (_GELU_C * (1.0 + 3.0 * 0.044715 * (x * x)))


def _me():
    return lax.axis_index("x"), lax.axis_index("y"), lax.axis_index("c")


def _other_chips(x, y):
    return [(1 - x, y), (x, 1 - y), (1 - x, 1 - y)]


_ANY = pl.BlockSpec(memory_space=pl.ANY)


class _Job:
    def __init__(self, ins, outs, n_sems, make, aliases=None):
        self.ins, self.outs, self.n_sems, self.make, self.aliases = ins, outs, n_sems, make, aliases or {}


def _job_gather_chips(blk, part=(0, 1, 1), into=None):
    R, C = blk.shape
    nr = R // part[2]
    rows = pl.ds(part[0] * nr, (part[1] - part[0]) * nr)

    def make(ins, outs, send_sems, recv_sems, base):
        x_ref, (out_ref,) = ins[0], outs
        x, y, c = _me()
        mine = 4 * x + 2 * y + c
        targets = [(x, y, 1 - c)] + [(cx, cy, c) for cx, cy in _other_chips(x, y)]

        def copy(k, slab, to):
            return pltpu.make_async_remote_copy(
                src_ref=x_ref.at[rows, :], dst_ref=out_ref.at[slab, rows, :], send_sem=send_sems.at[base + k],
                recv_sem=recv_sems.at[base + k], device_id=to, device_id_type=MESH)

        starts = [copy(k, mine, to) for k, to in enumerate(targets)]
        arrivals = [copy(k, 4 * tx + 2 * ty + tc, (tx, ty, tc)) for k, (tx, ty, tc) in enumerate(targets)]
        local = [pltpu.make_async_copy(x_ref.at[rows, :], out_ref.at[mine, rows, :], send_sems.at[base + 4])]
        return starts, arrivals, local

    out = jax.ShapeDtypeStruct((N_DEV, R, C), blk.dtype)
    if into is None:
        return _Job([blk], [out], 5, make)
    return _Job([blk, into], [out], 5, make, aliases={1: 0})


def _job_gather_sibling(part):
    def make(ins, outs, send_sems, recv_sems, base):
        (out_ref,) = outs
        x, y, c = _me()

        def copy(k, slab):
            return pltpu.make_async_remote_copy(
                src_ref=out_ref.at[slab], dst_ref=out_ref.at[slab], send_sem=send_sems.at[base + k],
                recv_sem=recv_sems.at[base + k], device_id=(x, y, 1 - c), device_id_type=MESH)

        chips = _other_chips(x, y)
        starts = [copy(k, 4 * cx + 2 * cy + c) for k, (cx, cy) in enumerate(chips)]
        arrivals = [copy(k, 4 * cx + 2 * cy + (1 - c)) for k, (cx, cy) in enumerate(chips)]
        return starts, arrivals, []

    return _Job([part], [jax.ShapeDtypeStruct(part.shape, part.dtype)], 3, make, aliases={0: 0})


def _job_scatter_sibling(gb):
    _, _, R, C = gb.shape

    def make(ins, outs, send_sems, recv_sems, base):
        (g_ref,), (recv_ref,) = ins, outs
        x, y, c = _me()
        copies = [pltpu.make_async_remote_copy(
            src_ref=g_ref.at[p, 1 - c], dst_ref=recv_ref.at[p], send_sem=send_sems.at[base + p],
            recv_sem=recv_sems.at[base + p], device_id=(x, y, 1 - c), device_id_type=MESH) for p in range(4)]
        return copies, copies, []

    return _Job([gb], [jax.ShapeDtypeStruct((4, R, C), gb.dtype)], 4, make)


def _job_scatter_sibling_windows(gb, starts, width):
    R, _ = gb.shape

    def make(ins, outs, send_sems, recv_sems, base):
        (g_ref,), (recv_ref,) = ins, outs
        x, y, c = _me()
        copies = []
        for p in range(4):
            start = pl.multiple_of(jnp.where(c == 0, starts[2 * p + 1], starts[2 * p]), LANES)
            copies.append(pltpu.make_async_remote_copy(
                src_ref=g_ref.at[:, pl.ds(start, width)], dst_ref=recv_ref.at[p], send_sem=send_sems.at[base + p],
                recv_sem=recv_sems.at[base + p], device_id=(x, y, 1 - c), device_id_type=MESH))
        return copies, copies, []

    return _Job([gb], [jax.ShapeDtypeStruct((4, R, width), gb.dtype)], 4, make)


def _job_scatter_chips(hb):
    _, R, C = hb.shape

    def make(ins, outs, send_sems, recv_sems, base):
        (h_ref,), (recv_ref,) = ins, outs
        x, y, c = _me()
        copies = [pltpu.make_async_remote_copy(
            src_ref=h_ref.at[2 * cx + cy], dst_ref=recv_ref.at[n], send_sem=send_sems.at[base + n],
            recv_sem=recv_sems.at[base + n], device_id=(cx, cy, c), device_id_type=MESH)
            for n, (cx, cy) in enumerate(_other_chips(x, y))]
        return copies, copies, []

    return _Job([hb], [jax.ShapeDtypeStruct((3, R, C), hb.dtype)], 3, make)


def _carry_call(body, *, name, grid, in_specs, out_specs, out_shape, scratch_shapes, semantics, args, jobs=()):
    jobs = list(jobs)
    n_in, n_out, n_scr = len(in_specs), len(out_specs), len(scratch_shapes)
    j_ins = [a for j in jobs for a in j.ins]
    j_outs = [o for j in jobs for o in j.outs]
    n_sems = sum(j.n_sems for j in jobs)
    aliases = {}
    i0, o0 = n_in, n_out
    for j in jobs:
        for a, b in j.aliases.items():
            aliases[i0 + a] = o0 + b
        i0 += len(j.ins)
        o0 += len(j.outs)

    def full_body(*refs):
        ins = refs[:n_in]
        jin = refs[n_in:n_in + len(j_ins)]
        outs = refs[n_in + len(j_ins):n_in + len(j_ins) + n_out]
        jout = refs[n_in + len(j_ins) + n_out:n_in + len(j_ins) + n_out + len(j_outs)]
        scr = refs[n_in + len(j_ins) + n_out + len(j_outs):]
        if jobs:
            send_sems, recv_sems = scr[n_scr], scr[n_scr + 1]
            starts, arrivals, local = [], [], []
            base = i0 = o0 = 0
            for j in jobs:
                s, a, l = j.make(jin[i0:i0 + len(j.ins)], jout[o0:o0 + len(j.outs)], send_sems, recv_sems, base)
                starts += s
                arrivals += a
                local += l
                base += j.n_sems
                i0 += len(j.ins)
                o0 += len(j.outs)
            pids = [pl.program_id(d) for d in range(len(grid))]
            first = functools.reduce(jnp.logical_and, [p == 0 for p in pids])
            last = functools.reduce(jnp.logical_and, [p == n - 1 for p, n in zip(pids, grid)])

            @pl.when(first)
            def _():
                for cp in local + starts:
                    cp.start()

        body(*ins, *outs, *scr[:n_scr])

        if jobs:
            @pl.when(last)
            def _():
                for cp in arrivals:
                    cp.wait_recv()
                for cp in starts:
                    cp.wait_send()
                for cp in local:
                    cp.wait()

    sems = [pltpu.SemaphoreType.DMA((n_sems,)), pltpu.SemaphoreType.DMA((n_sems,))] if jobs else []
    res = pl.pallas_call(
        full_body, name=name, grid=grid,
        in_specs=list(in_specs) + [_ANY] * len(j_ins),
        out_specs=list(out_specs) + [_ANY] * len(j_outs),
        out_shape=list(out_shape) + j_outs,
        scratch_shapes=list(scratch_shapes) + sems,
        input_output_aliases=aliases,
        compiler_params=_params(("arbitrary",) * len(grid) if jobs else semantics),
    )(*args, *j_ins)
    body_res, job_res = res[:n_out], res[n_out:]
    per_job = []
    for j in jobs:
        per_job.append(job_res[:len(j.outs)])
        job_res = job_res[len(j.outs):]
    return body_res, per_job


def _run_jobs(name, jobs):
    def body(done_ref):
        done_ref[...] = jnp.zeros_like(done_ref)

    return _carry_call(body, name=name, grid=(1,), in_specs=[], out_specs=[pl.BlockSpec((8, LANES), lambda i: (0, 0))],
                       out_shape=[jax.ShapeDtypeStruct((8, LANES), F32)], scratch_shapes=[], semantics=("arbitrary",),
                       args=[], jobs=jobs)[1]


def _mm(name, grid, a, a_spec, b, b_spec, contract, acc_shape, out_shape, out_specs, extras=(), epilogue=None, jobs=()):
    nk = grid[2]
    n_e = len(extras)
    n_o = len(out_shape)
    if epilogue is None:
        epilogue = lambda acc: (acc,)

    def body(a_ref, b_ref, *rest):
        e_refs = rest[:n_e]
        o_refs = rest[n_e:n_e + n_o]

        def finish(total):
            res = epilogue(total, *[r[...] for r in e_refs])
            for o, r in zip(o_refs, res):
                o[...] = r.astype(o.dtype)

        if nk == 1:
            finish(_dot(a_ref[...], b_ref[...], contract))
            return
        acc = rest[n_e + n_o]
        k = pl.program_id(2)

        @pl.when(k == 0)
        def _():
            acc[...] = _dot(a_ref[...], b_ref[...], contract)

        @pl.when(jnp.logical_and(k > 0, k < nk - 1))
        def _():
            acc[...] += _dot(a_ref[...], b_ref[...], contract)

        @pl.when(k == nk - 1)
        def _():
            finish(acc[...] + _dot(a_ref[...], b_ref[...], contract))

    outs, job_res = _carry_call(
        body, name=name, grid=grid, in_specs=[a_spec, b_spec] + [s for _, s in extras],
        out_specs=list(out_specs), out_shape=list(out_shape),
        scratch_shapes=[pltpu.VMEM(acc_shape, F32)] if nk > 1 else [],
        semantics=("parallel", "parallel", "arbitrary"), args=[a, b] + [e for e, _ in extras], jobs=jobs)
    return (outs, job_res) if jobs else outs


def _mm_nn(name, a, b, out_dtypes, tm, tn, tk, extras=(), epilogue=None, jobs=()):
    M, K = a.shape
    N = b.shape[1]
    tm, tn, tk = min(tm, M), min(tn, N), min(tk, K)
    o_spec = pl.BlockSpec((tm, tn), lambda i, j, k: (i, j))
    return _mm(name, (M // tm, N // tn, K // tk),
               a, pl.BlockSpec((tm, tk), lambda i, j, k: (i, k)),
               b, pl.BlockSpec((tk, tn), lambda i, j, k: (k, j)), NN, (tm, tn),
               [jax.ShapeDtypeStruct((M, N), dt) for dt in out_dtypes], [o_spec] * len(out_dtypes),
               [(e, o_spec) for e in extras], epilogue, jobs)


def _mm_nt(name, a, b, out_dtypes, tm, tn, tk, extras=(), epilogue=None, jobs=()):
    M, K = a.shape
    N = b.shape[0]
    tm, tn, tk = min(tm, M), min(tn, N), min(tk, K)
    o_spec = pl.BlockSpec((tm, tn), lambda i, j, k: (i, j))
    return _mm(name, (M // tm, N // tn, K // tk),
               a, pl.BlockSpec((tm, tk), lambda i, j, k: (i, k)),
               b, pl.BlockSpec((tn, tk), lambda i, j, k: (j, k)), NT, (tm, tn),
               [jax.ShapeDtypeStruct((M, N), dt) for dt in out_dtypes], [o_spec] * len(out_dtypes),
               [(e, o_spec) for e in extras], epilogue, jobs)


def _mm_tn(name, a, b, out_dtypes, tm, tn, tk, jobs=()):
    K, M = a.shape
    N = b.shape[1]
    tm, tn, tk = min(tm, M), min(tn, N), min(tk, K)
    o_spec = pl.BlockSpec((tm, tn), lambda i, j, k: (i, j))
    return _mm(name, (M // tm, N // tn, K // tk),
               a, pl.BlockSpec((tk, tm), lambda i, j, k: (k, i)),
               b, pl.BlockSpec((tk, tn), lambda i, j, k: (k, j)), TN, (tm, tn),
               [jax.ShapeDtypeStruct((M, N), dt) for dt in out_dtypes], [o_spec] * len(out_dtypes),
               epilogue=lambda acc: (acc,) * len(out_dtypes), jobs=jobs)


def _fgate_fwd(zf, bf):
    S = zf.shape[0]
    nc = S // CHUNK

    def body(zf_ref, bf_ref, f_ref):
        upper = (_iota2((CHUNK, CHUNK), 0) <= _iota2((CHUNK, CHUNK), 1)).astype(BF16)
        carry = jnp.zeros((8, 1), F32)
        for c in range(nc):
            t = zf_ref[c * CHUNK:(c + 1) * CHUNK, :] + bf_ref[...]
            lf = jnp.minimum(t, 0.0) - jnp.log(1.0 + jnp.exp(-jnp.abs(t)))
            lf_rows = lf.T[0:8, :]
            f_ref[:, c * CHUNK:(c + 1) * CHUNK] = (_dot3(lf_rows, upper) + carry) * LOG2E
            carry = carry + jnp.sum(lf_rows, axis=-1, keepdims=True)

    return pl.pallas_call(
        body, name="fgate_fwd", out_shape=jax.ShapeDtypeStruct((8, S), F32),
        compiler_params=_params(),
    )(zf, bf)


def _fgate_bwd(df, zf, bf):
    S = zf.shape[0]
    nc = S // CHUNK

    def body(df_ref, zf_ref, bf_ref, dzf_ref, dbf_ref):
        lower = (_iota2((CHUNK, CHUNK), 0) >= _iota2((CHUNK, CHUNK), 1)).astype(BF16)
        carry = jnp.zeros((8, 1), F32)
        dbf = jnp.zeros((1, LANES), F32)
        for c in reversed(range(nc)):
            sl = slice(c * CHUNK, (c + 1) * CHUNK)
            df = df_ref[:, sl]
            r = _dot3(df, lower) + carry
            carry = carry + jnp.sum(df, axis=-1, keepdims=True)
            r_cols = jnp.concatenate([r, jnp.zeros((CHUNK - 8, CHUNK), F32)], axis=0).T
            t = zf_ref[sl, :] + bf_ref[...]
            dz = r_cols * (1.0 / (1.0 + jnp.exp(t)))
            dzf_ref[sl, :] = dz.astype(BF16)
            dbf = dbf + jnp.sum(dz, axis=0, keepdims=True)
        dbf_ref[...] = dbf

    return pl.pallas_call(
        body, name="fgate_bwd",
        out_shape=[jax.ShapeDtypeStruct((S, LANES), BF16), jax.ShapeDtypeStruct((1, LANES), F32)],
        compiler_params=_params(),
    )(df, zf, bf)


_NEG = -1e30
LOG2E = 1.4426950408889634
N_SPLIT = 8
N_SPLIT_DIAG = 2
DIAG_STEP = 1024


def _attn_consts(T):
    rows, cols = _iota2((T, T), 0), _iota2((T, T), 1)
    return cols <= rows, rows <= cols


def _attn2_fwd(zm, f2col, f2row, T, jobs=()):
    S = zm.shape[0]
    H = f2col.shape[0]
    nb = S // T
    c2 = LOG2E / math.sqrt(HEAD_DIM)

    def body(q_ref, k_ref, v_ref, fq_ref, fk_ref, o_ref, lse_ref, vaug_s):
        i = pl.program_id(1)

        @pl.when(i == 0)
        def _():
            vaug_s[:, :HEAD_DIM] = v_ref[...]
            vaug_s[:, HEAD_DIM:] = jnp.ones((S, HEAD_DIM), BF16)

        keep = _attn_consts(T)[0]
        TH = T // N_SPLIT

        def block(j, diagonal, state):
            r0 = pl.multiple_of(j * T, T)
            fk = fk_ref[j]
            new = []
            for g, (m_old, acc) in enumerate(state):
                rows = slice(g * TH, (g + 1) * TH)
                nk = min(T, -(-(g + 1) * TH // DIAG_STEP) * DIAG_STEP) if diagonal else T
                s = _dot(q_ref[rows, :], k_ref[pl.ds(r0, nk), :], NT) * c2 + (fq_ref[rows, :] - fk[:, :nk])
                if diagonal:
                    s = jnp.where(keep[rows, :nk], s, _NEG)
                m_new = jnp.maximum(m_old, jnp.max(s, axis=-1, keepdims=True))
                p = jnp.exp2(s - m_new).astype(BF16)
                new.append((m_new, jnp.exp2(m_old - m_new) * acc + _dot(p, vaug_s[pl.ds(r0, nk), :])))
            return tuple(new)

        init = tuple((jnp.full((TH, 1), _NEG, F32), jnp.zeros((TH, 2 * HEAD_DIM), F32)) for _ in range(N_SPLIT))
        state = lax.fori_loop(0, i, lambda j, st: block(j, False, st), init)
        state = block(i, True, state)
        for g, (m, acc) in enumerate(state):
            rows = slice(g * TH, (g + 1) * TH)
            o_ref[rows, :] = acc[:, :HEAD_DIM] / acc[:, HEAD_DIM:]
            lse_ref[rows, :] = m + jnp.log2(acc[:, HEAD_DIM:HEAD_DIM + 1])

    nh = H
    return _carry_call(
        body, name="attn_fwd", grid=(H, nb), jobs=jobs, args=[zm, zm, zm, f2col, f2row],
        semantics=("arbitrary", "arbitrary"),
        in_specs=[
            pl.BlockSpec((T, HEAD_DIM), lambda h, i: (i, h)),
            pl.BlockSpec((S, HEAD_DIM), lambda h, i: (0, nh + h)),
            pl.BlockSpec((S, HEAD_DIM), lambda h, i: (0, 2 * nh + h)),
            pl.BlockSpec((None, T, 1), lambda h, i: (h, i, 0)),
            pl.BlockSpec((None, nb, 1, T), lambda h, i: (h, 0, 0, 0)),
        ],
        out_specs=[pl.BlockSpec((T, HEAD_DIM), lambda h, i: (i, h)), pl.BlockSpec((None, T, 1), lambda h, i: (h, i, 0))],
        out_shape=[jax.ShapeDtypeStruct((S, H * HEAD_DIM), F32), jax.ShapeDtypeStruct((H, S, 1), F32)],
        scratch_shapes=[pltpu.VMEM((S, 2 * HEAD_DIM), BF16)],
    )


def _attn2_bwd_dq(zm, dattn, f2col, f2row, lse2_col, delta_col, T, jobs=()):
    S = zm.shape[0]
    H = f2col.shape[0]
    nb = S // T
    scale = 1.0 / math.sqrt(HEAD_DIM)
    c2 = LOG2E * scale

    def body(q_ref, k_ref, v_ref, do_ref, fq_ref, fk_ref, lse_ref, dl_ref, dq_ref, rs_ref, bias_s, do_s):
        i = pl.program_id(1)
        keep = _attn_consts(T)[0]
        TH = T // N_SPLIT_DIAG
        bias_s[...] = fq_ref[...] - lse_ref[...]
        do_s[...] = do_ref[...].astype(BF16)

        def part(rows, j, nk, state, masked):
            acc, rs = state
            r0 = pl.multiple_of(j * T, T)
            kb = k_ref[pl.ds(r0, nk), :]
            s = _dot(q_ref[rows, :], kb, NT) * c2 + (bias_s[rows, :] - fk_ref[j][:, :nk])
            if masked:
                s = jnp.where(keep[rows, :nk], s, _NEG)
            ds = jnp.exp2(s) * (_dot(do_s[rows, :], v_ref[pl.ds(r0, nk), :], NT) - dl_ref[rows, :])
            return acc + _dot(ds.astype(BF16), kb), rs + jnp.sum(ds, axis=-1, keepdims=True)

        def step(j, state):
            return part(slice(0, T), j, T, state, False)

        acc, rs = lax.fori_loop(0, i, step, (jnp.zeros((T, HEAD_DIM), F32), jnp.zeros((T, 1), F32)))
        for g in range(N_SPLIT_DIAG):
            rows = slice(g * TH, (g + 1) * TH)
            acc_g, rs_g = part(rows, i, (g + 1) * TH, (acc[rows, :], rs[rows, :]), True)
            dq_ref[rows, :] = (acc_g * scale).astype(BF16)
            rs_ref[rows, :] = rs_g

    nh = H
    col = pl.BlockSpec((None, T, 1), lambda h, i: (h, i, 0))
    blk = pl.BlockSpec((T, HEAD_DIM), lambda h, i: (i, h))
    return _carry_call(
        body, name="attn_bwd_dq", grid=(H, nb), jobs=jobs,
        args=[zm, zm, zm, dattn, f2col, f2row, lse2_col, delta_col], semantics=("arbitrary", "arbitrary"),
        in_specs=[
            blk,
            pl.BlockSpec((S, HEAD_DIM), lambda h, i: (0, nh + h)),
            pl.BlockSpec((S, HEAD_DIM), lambda h, i: (0, 2 * nh + h)),
            blk, col,
            pl.BlockSpec((None, nb, 1, T), lambda h, i: (h, 0, 0, 0)),
            col, col,
        ],
        out_specs=[blk, col],
        out_shape=[jax.ShapeDtypeStruct((S, H * HEAD_DIM), BF16), jax.ShapeDtypeStruct((H, S, 1), F32)],
        scratch_shapes=[pltpu.VMEM((T, 1), F32), pltpu.VMEM((T, HEAD_DIM), BF16)],
    )


def _attn2_bwd_dkv(zm, dattn, f2col, f2row, lse2_row, delta_row, rowsum_row, T, jobs=()):
    S = zm.shape[0]
    H = f2col.shape[0]
    nb = S // T
    scale = 1.0 / math.sqrt(HEAD_DIM)
    c2 = LOG2E * scale

    def body(q_ref, k_ref, v_ref, do_ref, fk_ref, fq_ref, lse_ref, dl_ref, rs_ref, dk_ref, dv_ref, df_ref):
        j = pl.program_id(1)
        keep = _attn_consts(T)[1]
        TH = T // N_SPLIT_DIAG

        def part(rows, i, c0, state, masked):
            dk, dv, df = state
            r0 = pl.multiple_of(i * T + c0, TH)
            qb = q_ref[pl.ds(r0, T - c0), :]
            do = do_ref[pl.ds(r0, T - c0), :].astype(BF16)
            bias = (fq_ref[i] - lse_ref[i])[:, c0:]
            dl = (dl_ref[i] + rs_ref[i])[:, c0:]
            st = _dot(k_ref[rows, :], qb, NT) * c2 + (bias - fk_ref[rows, :])
            if masked:
                st = jnp.where(keep[rows, c0:], st, _NEG)
            pt = jnp.exp2(st)
            dst = pt * (_dot(v_ref[rows, :], do, NT) - dl)
            return (dk + _dot(dst.astype(BF16), qb), dv + _dot(pt.astype(BF16), do),
                    df - jnp.sum(dst, axis=-1, keepdims=True))

        groups = []
        for g in range(N_SPLIT_DIAG):
            zero = (jnp.zeros((TH, HEAD_DIM), F32), jnp.zeros((TH, HEAD_DIM), F32), jnp.zeros((TH, 1), F32))
            groups.append(part(slice(g * TH, (g + 1) * TH), j, g * TH, zero, True))
        state = tuple(jnp.concatenate([grp[n] for grp in groups], axis=0) for n in range(3))
        dk, dv, df = lax.fori_loop(j + 1, nb, lambda i, st: part(slice(0, T), i, 0, st, False), state)
        dk_ref[...] = (dk * scale).astype(BF16)
        dv_ref[...] = dv.astype(BF16)
        df_ref[...] = df

    nh = H
    row = pl.BlockSpec((None, nb, 1, T), lambda h, j: (h, 0, 0, 0))
    whole = pl.BlockSpec((S, HEAD_DIM), lambda h, j: (0, h))
    kv_out = pl.BlockSpec((T, HEAD_DIM), lambda h, j: (j, h))
    col = pl.BlockSpec((None, T, 1), lambda h, j: (h, j, 0))
    return _carry_call(
        body, name="attn_bwd_dkv", grid=(H, nb), jobs=jobs,
        args=[zm, zm, zm, dattn, f2col, f2row, lse2_row, delta_row, rowsum_row],
        semantics=("arbitrary", "arbitrary"),
        in_specs=[
            whole,
            pl.BlockSpec((T, HEAD_DIM), lambda h, j: (j, nh + h)),
            pl.BlockSpec((T, HEAD_DIM), lambda h, j: (j, 2 * nh + h)),
            whole, col, row, row, row, row,
        ],
        out_specs=[kv_out, kv_out, col],
        out_shape=[jax.ShapeDtypeStruct((S, H * HEAD_DIM), BF16), jax.ShapeDtypeStruct((S, H * HEAD_DIM), BF16),
                   jax.ShapeDtypeStruct((H, S, 1), F32)],
        scratch_shapes=[],
    )


def _attn_fwd(zm, fcol, frow, T, jobs=()):
    S = zm.shape[0]
    H = fcol.shape[0]
    nb = S // T
    scale = 1.0 / math.sqrt(HEAD_DIM)

    def body(q_ref, k_ref, v_ref, fq_ref, fk_ref, o_ref, lse_ref, m_s, l_s, acc_s):
        i = pl.program_id(1)
        j = pl.program_id(2)

        @pl.when(j == 0)
        def _():
            m_s[...] = jnp.full_like(m_s, _NEG)
            l_s[...] = jnp.zeros_like(l_s)
            acc_s[...] = jnp.zeros_like(acc_s)

        @pl.when(j <= i)
        def _():
            s = _dot(q_ref[...], k_ref[...], NT) * scale + (fq_ref[...] - fk_ref[...])
            keep = (_iota2((T, T), 1) + j * T) <= (_iota2((T, T), 0) + i * T)
            s = jnp.where(keep, s, _NEG)
            m_new = jnp.maximum(m_s[...], jnp.max(s, axis=-1, keepdims=True))
            alpha = jnp.exp(m_s[...] - m_new)
            p = jnp.exp(s - m_new)
            l_s[...] = alpha * l_s[...] + jnp.sum(p, axis=-1, keepdims=True)
            acc_s[...] = alpha * acc_s[...] + _dot(p.astype(BF16), v_ref[...])
            m_s[...] = m_new

        @pl.when(j == nb - 1)
        def _():
            o_ref[...] = acc_s[...] / l_s[...]
            lse_ref[...] = m_s[...] + jnp.log(l_s[...])

    nh = H
    return _carry_call(
        body, name="attn_fwd", grid=(H, nb, nb), jobs=jobs, args=[zm, zm, zm, fcol, frow],
        semantics=("parallel", "parallel", "arbitrary"),
        in_specs=[
            pl.BlockSpec((T, HEAD_DIM), lambda h, i, j: (i, h)),
            pl.BlockSpec((T, HEAD_DIM), lambda h, i, j: (jnp.minimum(j, i), nh + h)),
            pl.BlockSpec((T, HEAD_DIM), lambda h, i, j: (jnp.minimum(j, i), 2 * nh + h)),
            pl.BlockSpec((None, T, 1), lambda h, i, j: (h, i, 0)),
            pl.BlockSpec((None, 1, T), lambda h, i, j: (h, 0, jnp.minimum(j, i))),
        ],
        out_specs=[
            pl.BlockSpec((T, HEAD_DIM), lambda h, i, j: (i, h)),
            pl.BlockSpec((None, T, 1), lambda h, i, j: (h, i, 0)),
        ],
        out_shape=[jax.ShapeDtypeStruct((S, H * HEAD_DIM), F32), jax.ShapeDtypeStruct((H, S, 1), F32)],
        scratch_shapes=[pltpu.VMEM((T, 1), F32), pltpu.VMEM((T, 1), F32), pltpu.VMEM((T, HEAD_DIM), F32)],
    )


def _attn_delta(dattn, attn, tr):
    S, DA = attn.shape
    H = DA // HEAD_DIM

    def body(do_ref, o_ref, out_ref):
        lo = _iota2((DA, LANES), 1) * HEAD_DIM
        sel = ((_iota2((DA, LANES), 0) >= lo) & (_iota2((DA, LANES), 0) < lo + HEAD_DIM)).astype(BF16)
        d = _dot3(do_ref[...] * o_ref[...], sel)
        for c in range(tr // CHUNK):
            out_ref[:, c * CHUNK:(c + 1) * CHUNK] = d[c * CHUNK:(c + 1) * CHUNK, :].T[0:H, :]

    return pl.pallas_call(
        body, name="attn_delta", grid=(S // tr,),
        in_specs=[pl.BlockSpec((tr, DA), lambda i: (i, 0))] * 2,
        out_specs=pl.BlockSpec((H, tr), lambda i: (0, i)),
        out_shape=jax.ShapeDtypeStruct((H, S), F32),
        compiler_params=_params(("parallel",)),
    )(dattn, attn)


def _attn_bwd_dq(zm, dattn, fcol, frow, lse_col, delta_col, T, jobs=()):
    S = zm.shape[0]
    H = fcol.shape[0]
    nb = S // T
    scale = 1.0 / math.sqrt(HEAD_DIM)

    def body(q_ref, k_ref, v_ref, do_ref, fq_ref, fk_ref, lse_ref, dl_ref, dq_ref, rs_ref, acc_s, rs_s):
        i = pl.program_id(1)
        j = pl.program_id(2)

        @pl.when(j == 0)
        def _():
            acc_s[...] = jnp.zeros_like(acc_s)
            rs_s[...] = jnp.zeros_like(rs_s)

        @pl.when(j <= i)
        def _():
            s = _dot(q_ref[...], k_ref[...], NT) * scale + (fq_ref[...] - fk_ref[...])
            keep = (_iota2((T, T), 1) + j * T) <= (_iota2((T, T), 0) + i * T)
            p = jnp.exp(jnp.where(keep, s - lse_ref[...], _NEG))
            dp = _dot(do_ref[...].astype(BF16), v_ref[...], NT)
            ds = p * (dp - dl_ref[...])
            acc_s[...] += _dot(ds.astype(BF16), k_ref[...])
            rs_s[...] += jnp.sum(ds, axis=-1, keepdims=True)

        @pl.when(j == nb - 1)
        def _():
            dq_ref[...] = (acc_s[...] * scale).astype(BF16)
            rs_ref[...] = rs_s[...]

    nh = H
    col = pl.BlockSpec((None, T, 1), lambda h, i, j: (h, i, 0))
    return _carry_call(
        body, name="attn_bwd_dq", grid=(H, nb, nb), jobs=jobs,
        args=[zm, zm, zm, dattn, fcol, frow, lse_col, delta_col], semantics=("parallel", "parallel", "arbitrary"),
        in_specs=[
            pl.BlockSpec((T, HEAD_DIM), lambda h, i, j: (i, h)),
            pl.BlockSpec((T, HEAD_DIM), lambda h, i, j: (jnp.minimum(j, i), nh + h)),
            pl.BlockSpec((T, HEAD_DIM), lambda h, i, j: (jnp.minimum(j, i), 2 * nh + h)),
            pl.BlockSpec((T, HEAD_DIM), lambda h, i, j: (i, h)),
            col,
            pl.BlockSpec((None, 1, T), lambda h, i, j: (h, 0, jnp.minimum(j, i))),
            col, col,
        ],
        out_specs=[pl.BlockSpec((T, HEAD_DIM), lambda h, i, j: (i, h)), col],
        out_shape=[jax.ShapeDtypeStruct((S, H * HEAD_DIM), BF16), jax.ShapeDtypeStruct((H, S, 1), F32)],
        scratch_shapes=[pltpu.VMEM((T, HEAD_DIM), F32), pltpu.VMEM((T, 1), F32)],
    )


def _attn_bwd_dkv(zm, dattn, fcol, frow, lse_row, delta_row, rowsum_row, T, jobs=()):
    S = zm.shape[0]
    H = fcol.shape[0]
    nb = S // T
    scale = 1.0 / math.sqrt(HEAD_DIM)

    def body(q_ref, k_ref, v_ref, do_ref, fk_ref, fq_ref, lse_ref, dl_ref, rs_ref,
             dk_ref, dv_ref, df_ref, dk_s, dv_s, df_s):
        j = pl.program_id(1)
        i = pl.program_id(2)

        @pl.when(i == 0)
        def _():
            dk_s[...] = jnp.zeros_like(dk_s)
            dv_s[...] = jnp.zeros_like(dv_s)
            df_s[...] = jnp.zeros_like(df_s)

        @pl.when(i >= j)
        def _():
            st = _dot(k_ref[...], q_ref[...], NT) * scale + (fq_ref[...] - fk_ref[...])
            keep = (_iota2((T, T), 0) + j * T) <= (_iota2((T, T), 1) + i * T)
            pt = jnp.exp(jnp.where(keep, st - lse_ref[...], _NEG))
            do = do_ref[...].astype(BF16)
            dpt = _dot(v_ref[...], do, NT)
            dst = pt * (dpt - (dl_ref[...] + rs_ref[...]))
            dv_s[...] += _dot(pt.astype(BF16), do)
            dk_s[...] += _dot(dst.astype(BF16), q_ref[...])
            df_s[...] -= jnp.sum(dst, axis=-1, keepdims=True)

        @pl.when(i == nb - 1)
        def _():
            dk_ref[...] = (dk_s[...] * scale).astype(BF16)
            dv_ref[...] = dv_s[...].astype(BF16)
            df_ref[...] = df_s[...]

    nh = H
    row = pl.BlockSpec((None, 1, T), lambda h, j, i: (h, 0, jnp.maximum(i, j)))
    kv_out = pl.BlockSpec((T, HEAD_DIM), lambda h, j, i: (j, h))
    return _carry_call(
        body, name="attn_bwd_dkv", grid=(H, nb, nb), jobs=jobs,
        args=[zm, zm, zm, dattn, fcol, frow, lse_row, delta_row, rowsum_row],
        semantics=("parallel", "parallel", "arbitrary"),
        in_specs=[
            pl.BlockSpec((T, HEAD_DIM), lambda h, j, i: (jnp.maximum(i, j), h)),
            pl.BlockSpec((T, HEAD_DIM), lambda h, j, i: (j, nh + h)),
            pl.BlockSpec((T, HEAD_DIM), lambda h, j, i: (j, 2 * nh + h)),
            pl.BlockSpec((T, HEAD_DIM), lambda h, j, i: (jnp.maximum(i, j), h)),
            pl.BlockSpec((None, T, 1), lambda h, j, i: (h, j, 0)),
            row, row, row, row,
        ],
        out_specs=[kv_out, kv_out, pl.BlockSpec((None, T, 1), lambda h, j, i: (h, j, 0))],
        out_shape=[jax.ShapeDtypeStruct((S, H * HEAD_DIM), BF16), jax.ShapeDtypeStruct((S, H * HEAD_DIM), BF16),
                   jax.ShapeDtypeStruct((H, S, 1), F32)],
        scratch_shapes=[pltpu.VMEM((T, HEAD_DIM), F32), pltpu.VMEM((T, HEAD_DIM), F32), pltpu.VMEM((T, 1), F32)],
    )


def _ln_stats(x):
    mu = jnp.mean(x, axis=-1, keepdims=True)
    xc = x - mu
    rstd = lax.rsqrt(jnp.mean(xc * xc, axis=-1, keepdims=True) + EPS)
    return xc * rstd, rstd


def _tril_mask():
    return _iota2((CHUNK, CHUNK), 0) >= _iota2((CHUNK, CHUNK), 1)


def _gmlp_fwd(zm, ln_g, ln_b, w_s, bs_col, tr):
    S = zm.shape[0]
    H = w_s.shape[0]
    DG = H * HEAD_DIM

    def body(zu_ref, zv_ref, g_ref, b_ref, w_ref, bs_ref, out_ref):
        u = _gelu(zu_ref[...].astype(F32))
        y, _ = _ln_stats(_gelu(zv_ref[...].astype(F32)))
        v = (y * g_ref[...] + b_ref[...]).astype(BF16)
        mask = _tril_mask()
        for h in range(H):
            wc = jnp.where(mask, w_ref[h], 0.0).astype(BF16)
            cs = slice(h * HEAD_DIM, (h + 1) * HEAD_DIM)
            for c in range(tr // CHUNK):
                rs = slice(c * CHUNK, (c + 1) * CHUNK)
                mix = _dot(wc, v[rs, cs]) + bs_ref[h]
                out_ref[rs, cs] = u[rs, cs] * mix

    full = lambda a: pl.BlockSpec(a.shape, lambda i: (0,) * a.ndim)
    return pl.pallas_call(
        body, name="gmlp_fwd", grid=(S // tr,),
        in_specs=[pl.BlockSpec((tr, DG), lambda i: (i, 3)), pl.BlockSpec((tr, DG), lambda i: (i, 4)),
                  full(ln_g), full(ln_b), full(w_s), full(bs_col)],
        out_specs=pl.BlockSpec((tr, DG), lambda i: (i, 0)),
        out_shape=jax.ShapeDtypeStruct((S, DG), F32),
        compiler_params=_params(("parallel",)),
    )(zm, zm, ln_g, ln_b, w_s, bs_col)


def _gmlp_bwd(dgm, zm, ln_g, ln_b, w_s, w_st, bs_col, tr):
    S = zm.shape[0]
    H = w_s.shape[0]
    DG = H * HEAD_DIM

    def body(dg_ref, zu_ref, zv_ref, g_ref, b_ref, w_ref, wt_ref, bs_ref,
             dzu_ref, dzv_ref, dw_ref, dbs_ref, dlg_ref, dlb_ref, dv_s):
        @pl.when(pl.program_id(0) == 0)
        def _():
            dw_ref[...] = jnp.zeros_like(dw_ref)
            dbs_ref[...] = jnp.zeros_like(dbs_ref)
            dlg_ref[...] = jnp.zeros_like(dlg_ref)
            dlb_ref[...] = jnp.zeros_like(dlb_ref)

        zu = zu_ref[...].astype(F32)
        zv = zv_ref[...].astype(F32)
        u = _gelu(zu)
        y, rstd = _ln_stats(_gelu(zv))
        v = (y * g_ref[...] + b_ref[...]).astype(BF16)
        dgm_blk = dg_ref[...]
        mask = _tril_mask()
        mask_t = _iota2((CHUNK, CHUNK), 0) <= _iota2((CHUNK, CHUNK), 1)
        for h in range(H):
            wc = jnp.where(mask, w_ref[h], 0.0).astype(BF16)
            wct = jnp.where(mask_t, wt_ref[h], 0.0).astype(BF16)
            cs = slice(h * HEAD_DIM, (h + 1) * HEAD_DIM)
            dw = jnp.zeros((CHUNK, CHUNK), F32)
            dbs = jnp.zeros((CHUNK, 1), F32)
            for c in range(tr // CHUNK):
                rs = slice(c * CHUNK, (c + 1) * CHUNK)
                vch = v[rs, cs]
                mix = _dot(wc, vch) + bs_ref[h]
                dg = dgm_blk[rs, cs]
                dzu_ref[rs, cs] = (dg * mix * _gelu_grad(zu[rs, cs])).astype(BF16)
                dmix = dg * u[rs, cs]
                dbs = dbs + jnp.sum(dmix, axis=-1, keepdims=True)
                dmix_b = dmix.astype(BF16)
                dw = dw + _dot(dmix_b, vch, NT)
                dv_s[rs, cs] = _dot(wct, dmix_b)
            dw_ref[h] += jnp.where(mask, dw, 0.0)
            dbs_ref[h] += dbs
        dv = dv_s[...]
        dlg_ref[...] += jnp.sum(dv * y, axis=0, keepdims=True)
        dlb_ref[...] += jnp.sum(dv, axis=0, keepdims=True)
        dy = dv * g_ref[...]
        dgv = rstd * (dy - jnp.mean(dy, axis=-1, keepdims=True) - y * jnp.mean(dy * y, axis=-1, keepdims=True))
        dzv_ref[...] = (dgv * _gelu_grad(zv)).astype(BF16)

    full = lambda a: pl.BlockSpec(a.shape, lambda i: (0,) * a.ndim)
    rows = pl.BlockSpec((tr, DG), lambda i: (i, 0))
    return pl.pallas_call(
        body, name="gmlp_bwd", grid=(S // tr,),
        in_specs=[rows, pl.BlockSpec((tr, DG), lambda i: (i, 3)), pl.BlockSpec((tr, DG), lambda i: (i, 4)),
                  full(ln_g), full(ln_b), full(w_s), full(w_st), full(bs_col)],
        out_specs=[rows, rows, full(w_s), full(bs_col), full(ln_g), full(ln_b)],
        out_shape=[jax.ShapeDtypeStruct((S, DG), BF16), jax.ShapeDtypeStruct((S, DG), BF16),
                   jax.ShapeDtypeStruct(w_s.shape, F32), jax.ShapeDtypeStruct(bs_col.shape, F32),
                   jax.ShapeDtypeStruct(ln_g.shape, F32), jax.ShapeDtypeStruct(ln_b.shape, F32)],
        scratch_shapes=[pltpu.VMEM((tr, DG), F32)],
        compiler_params=_params(("arbitrary",)),
    )(dgm, zm, zm, ln_g, ln_b, w_s, w_st, bs_col)


def _all_gather(name, blk):
    R, C = blk.shape

    def body(x_ref, out_ref, send_sems, recv_sems, local_sem):
        x, y, c = _me()
        me, sibling = (x, y, c), (x, y, 1 - c)
        chips = [(1 - x, y), (x, 1 - y), (1 - x, 1 - y)]

        def slab(px, py, pc):
            return out_ref.at[4 * px + 2 * py + pc]

        def copy(k, block, to, src=None):
            return pltpu.make_async_remote_copy(
                src_ref=slab(*block) if src is None else src, dst_ref=slab(*block),
                send_sem=send_sems.at[k], recv_sem=recv_sems.at[k], device_id=to, device_id_type=MESH)

        mine = pltpu.make_async_copy(x_ref, slab(*me), local_sem)
        mine.start()
        first = [copy(0, me, sibling, src=x_ref)]
        first += [copy(1 + n, me, (*chip, c), src=x_ref) for n, chip in enumerate(chips)]
        for cp in first:
            cp.start()
        passed = [copy(4 + n, (*chip, c), sibling) for n, chip in enumerate(chips)]
        for n, chip in enumerate(chips):
            copy(1 + n, (*chip, c), me).wait_recv()
            passed[n].start()
        copy(0, sibling, me).wait_recv()
        for n, chip in enumerate(chips):
            copy(4 + n, (*chip, 1 - c), me).wait_recv()
        for cp in first + passed:
            cp.wait_send()
        mine.wait()

    return pl.pallas_call(
        body, name=name, out_shape=jax.ShapeDtypeStruct((N_DEV, R, C), blk.dtype),
        in_specs=[_ANY], out_specs=_ANY,
        scratch_shapes=[pltpu.SemaphoreType.DMA((7,)), pltpu.SemaphoreType.DMA((7,)), pltpu.SemaphoreType.DMA(())],
    )(blk)


def _row_tile(R, C, itemsize=4, target_bytes=2 * 1024 * 1024):
    tr = R
    while tr % 2 == 0 and tr * C * itemsize > target_bytes and (tr // 2) % 16 == 0:
        tr //= 2
    return tr


def _rs_add1(name, g4, recv, c_idx):
    _, _, R, C = g4.shape
    tr = _row_tile(R, C)

    def body(c_ref, g_ref, r_ref, h_ref, hb_ref):
        h = g_ref[...] + r_ref[...].astype(F32)
        h_ref[...] = h
        hb_ref[...] = h.astype(BF16)

    blk = pl.BlockSpec((None, tr, C), lambda p, i, c_ref: (p, i, 0))
    return pl.pallas_call(
        body, name=name,
        grid_spec=pltpu.PrefetchScalarGridSpec(
            num_scalar_prefetch=1, grid=(4, R // tr),
            in_specs=[pl.BlockSpec((None, None, tr, C), lambda p, i, c_ref: (p, c_ref[0], i, 0)), blk],
            out_specs=[blk, blk]),
        out_shape=[jax.ShapeDtypeStruct((4, R, C), F32), jax.ShapeDtypeStruct((4, R, C), BF16)],
        compiler_params=_params(("parallel", "parallel")),
    )(c_idx, g4, recv)


def _rs_add1_windows(name, g, recv, first_blocks):
    _, R, W = recv.shape

    def body(t_ref, g_ref, r_ref, h_ref, hb_ref):
        h = g_ref[...] + r_ref[...].astype(F32)
        h_ref[...] = h
        hb_ref[...] = h.astype(BF16)

    blk = pl.BlockSpec((None, R, LANES), lambda p, l, t_ref: (p, 0, l))
    return pl.pallas_call(
        body, name=name,
        grid_spec=pltpu.PrefetchScalarGridSpec(
            num_scalar_prefetch=1, grid=(4, W // LANES),
            in_specs=[pl.BlockSpec((R, LANES), lambda p, l, t_ref: (0, t_ref[p] + l)), blk],
            out_specs=[blk, blk]),
        out_shape=[jax.ShapeDtypeStruct((4, R, W), F32), jax.ShapeDtypeStruct((4, R, W), BF16)],
        compiler_params=_params(("parallel", "parallel")),
    )(first_blocks, g, recv)


def _rs_add2(name, h, recv, p_idx):
    _, R, C = h.shape
    tr = _row_tile(R, C)

    def body(p_ref, h_ref, r_ref, out_ref):
        out_ref[...] = ((h_ref[...] + r_ref[0].astype(F32)) + r_ref[1].astype(F32)) + r_ref[2].astype(F32)

    return pl.pallas_call(
        body, name=name,
        grid_spec=pltpu.PrefetchScalarGridSpec(
            num_scalar_prefetch=1, grid=(R // tr,),
            in_specs=[pl.BlockSpec((None, tr, C), lambda i, p_ref: (p_ref[0], i, 0)),
                      pl.BlockSpec((3, tr, C), lambda i, p_ref: (0, i, 0))],
            out_specs=pl.BlockSpec((tr, C), lambda i, p_ref: (i, 0))),
        out_shape=jax.ShapeDtypeStruct((R, C), F32),
        compiler_params=_params(("parallel",)),
    )(p_idx, h, recv)


def _sum8(name, g):
    _, R, C = g.shape

    def body(g_ref, out_ref):
        acc = g_ref[0]
        for d in range(1, N_DEV):
            acc = acc + g_ref[d]
        out_ref[...] = acc

    return pl.pallas_call(body, name=name, out_shape=jax.ShapeDtypeStruct((R, C), F32),
                          compiler_params=_params())(g)


def _adamw(name, w, g, m, v):
    R, C = w.shape
    tr = _row_tile(R, C, target_bytes=1024 * 1024)

    def fn(w, g, m, v):
        m = ADAM_B1 * m + (1.0 - ADAM_B1) * g
        v = ADAM_B2 * v + (1.0 - ADAM_B2) * (g * g)
        m_hat = m / (1.0 - ADAM_B1 ** ADAM_STEP)
        v_hat = v / (1.0 - ADAM_B2 ** ADAM_STEP)
        delta = -ADAM_LR * (m_hat / (jnp.sqrt(v_hat) + ADAM_EPS) + ADAM_WD * w)
        return (delta, m, v), ()

    return _row_call(name, fn, [w, g, m, v], [], [(C, F32)] * 3, [], tr)


def _pack(parts):
    flat = []
    total = 0
    for a in parts:
        n = math.prod(a.shape)
        flat.append(a.reshape(-1).astype(F32))
        if n % LANES:
            flat.append(jnp.zeros((-n % LANES,), F32))
        total += n + (-n % LANES)
    if total % (8 * LANES):
        flat.append(jnp.zeros((-total % (8 * LANES),), F32))
    return jnp.concatenate(flat).reshape(-1, LANES)


def _unpack(packed, shapes):
    out = []
    r = 0
    for shp in shapes:
        n = math.prod(shp)
        nr = -(-n // LANES)
        out.append(packed[r:r + nr].reshape(-1)[:n].reshape(shp))
        r += nr
    return out


def kernel(x, norm_mix_g, w_in, b_f, gmlp_ln_g, gmlp_ln_b, w_s, b_s, attn_out_g, gmlp_out_g, w_out, norm_ffn_g, w_ff1, w_ff2, norm_final_g, loss_target, m_norm_mix_g, m_w_in, m_b_f, m_gmlp_ln_g, m_gmlp_ln_b, m_w_s, m_b_s, m_attn_out_g, m_gmlp_out_g, m_w_out, m_norm_ffn_g, m_w_ff1, m_w_ff2, m_norm_final_g, v_norm_mix_g, v_w_in, v_b_f, v_gmlp_ln_g, v_gmlp_ln_b, v_w_s, v_b_s, v_attn_out_g, v_gmlp_out_g, v_w_out, v_norm_ffn_g, v_w_ff1, v_w_ff2, v_norm_final_g):
    S, D = x.shape[1], x.shape[2]
    H = b_f.shape[1]
    DA = H * HEAD_DIM
    DG = gmlp_ln_g.shape[1]
    DQKV = 3 * DA
    DMAIN = DQKV + 2 * DG
    DIN = DMAIN + H
    DFF = w_ff1.shape[2] * N_DEV
    w_in_cols = w_in.shape[2]
    assert DIN == w_in_cols * N_DEV and DA == DG and D == DA + DG

    T_ATT = min(T_ATT_MAX, S)
    TR = min(TR_MAX, S)

    x0 = x[0]
    tgt = loss_target[0]
    g_final = norm_final_g.reshape(1, D)

    FB = DFF // N_DEV
    x_pos, y_pos, c_pos = _me()
    me_idx = 4 * x_pos + 2 * y_pos + c_pos

    WW = -(-(w_in_cols + LANES - 1) // LANES) * LANES
    to_main = lambda col: col if col <= DQKV else max(DQKV, col - H)
    lo = [to_main(n * w_in_cols) for n in range(N_DEV)]
    hi = [to_main((n + 1) * w_in_cols) for n in range(N_DEV)]
    starts = [v // LANES * LANES for v in lo]
    gate_dev = DQKV // w_in_cols
    n_before = DQKV - gate_dev * w_in_cols
    g0 = lo[gate_dev] - starts[gate_dev]
    stash = -(-(g0 + w_in_cols - H) // LANES) * LANES
    assert all(hi[n] <= starts[n] + WW <= DMAIN for n in range(N_DEV))
    assert gate_dev * w_in_cols <= DQKV and DQKV + H <= (gate_dev + 1) * w_in_cols and stash + LANES <= WW
    shard = w_in[0].astype(BF16)
    off = sum(jnp.where(me_idx == n, lo[n] - starts[n], 0) for n in range(N_DEV))
    plain = lax.dynamic_update_slice(jnp.zeros((D, WW), BF16), shard, (0, off))
    with_gate = jnp.concatenate([
        jnp.zeros((D, g0), BF16), shard[:, :n_before], shard[:, n_before + H:],
        jnp.zeros((D, stash - g0 - (w_in_cols - H)), BF16), shard[:, n_before:n_before + H],
        jnp.zeros((D, WW - stash - H), BF16)], axis=1)
    windows = _all_gather("ag_w_in", jnp.where(me_idx == gate_dev, with_gate, plain))
    pieces = []
    for blk in range(DMAIN // LANES):
        c0 = blk * LANES
        owners = [n for n in range(N_DEV) if lo[n] < c0 + LANES and hi[n] > c0]
        piece = windows[owners[0], :, c0 - starts[owners[0]]:c0 - starts[owners[0]] + LANES]
        for n in owners[1:]:
            piece = piece + windows[n, :, c0 - starts[n]:c0 - starts[n] + LANES]
        pieces.append(piece)
    w_main = jnp.concatenate(pieces, axis=1)
    w_f = windows[gate_dev, :, stash:stash + LANES]
    c_idx = jnp.reshape(c_pos, (1,)).astype(jnp.int32)
    p_idx = jnp.reshape(2 * x_pos + y_pos, (1,)).astype(jnp.int32)

    (h,), _ = _row_call("rms_mix", lambda xb, g: ((_rms_fwd(xb, g),), ()), [x0], [norm_mix_g], [(D, BF16)], [], TR)
    (zm,), ((w_out_part,),) = _mm_nn("in_proj", h, w_main, [BF16], 2048, 1024, 2048,
                                     jobs=[_job_gather_chips(w_out[0].astype(BF16))])
    (zf,) = _mm_nn("in_proj_f", h, w_f, [F32], 1024, LANES, 2048)
    bf_pad = jnp.pad(b_f, ((0, 0), (0, LANES - H)))
    f_row = _fgate_fwd(zf, bf_pad)
    NB = S // T_ATT
    f_col3 = f_row.reshape(H, S, 1)
    f_row3 = f_row.reshape(H, NB, 1, T_ATT)
    (attn, lse_col3), ((w_out_all,), (w_ff1_part,)) = _attn2_fwd(
        zm, f_col3, f_row3, T_ATT, jobs=[_job_gather_sibling(w_out_part), _job_gather_chips(w_ff1[0].astype(BF16))])
    w_out_full = w_out_all.reshape(D, D)
    bs_col = b_s[0].reshape(H, CHUNK, 1)
    gm = _gmlp_fwd(zm, gmlp_ln_g, gmlp_ln_b, w_s[0], bs_col, TR)

    def merge_fn(a, g, ga, gg):
        return (jnp.concatenate([_rms_fwd(a, ga), _rms_fwd(g, gg)], axis=1),), ()
    (merged,), _ = _row_call("rms_merge", merge_fn, [attn, gm], [attn_out_g, gmlp_out_g], [(D, BF16)], [], TR)

    w_ff2_b = w_ff2[0].astype(BF16)
    (x1,), ((w_ff1_all,), (w_ff2_q1,)) = _mm_nn(
        "out_proj", merged, w_out_full, [F32], 1024, 1024, 2048, extras=[x0], epilogue=lambda acc, r: (acc + r,),
        jobs=[_job_gather_sibling(w_ff1_part), _job_gather_chips(w_ff2_b, part=(0, 1, 4))])
    (h2,), _ = _row_call("rms_ffn", lambda xb, g: ((_rms_fwd(xb, g),), ()), [x1], [norm_ffn_g], [(D, BF16)], [], TR)

    tm, tn, tk = min(1024, S), min(1024, FB), min(2048, D)
    tm1 = min(2048, S)
    o_spec = pl.BlockSpec((tm1, tn), lambda i, j, k: (i, j))

    def relu_sq(acc):
        a = jnp.maximum(acc, 0.0)
        return a, a * a
    nj = FB // tn
    ff2_rest = [_job_gather_chips(w_ff2_b, part=(1, 4, 4), into=w_ff2_q1)]
    (a_act, a_sq), ((w_ff2_q2,),) = _mm(
        "ff1", (S // tm1, DFF // tn, D // tk), h2, pl.BlockSpec((tm1, tk), lambda i, j, k: (i, k)),
        w_ff1_all, pl.BlockSpec((None, tk, tn), lambda i, j, k: (j // nj, k, j % nj)), NN, (tm1, tn),
        [jax.ShapeDtypeStruct((S, DFF), BF16)] * 2, [o_spec] * 2, epilogue=relu_sq, jobs=ff2_rest)
    (w_ff2_all,) = _run_jobs("ag_w_ff2_sibling", [_job_gather_sibling(w_ff2_q2)])[0]
    w_ff2_full = w_ff2_all.reshape(DFF, D)
    (x2,) = _mm_nn("ff2", a_sq, w_ff2_full, [F32], 1024, 1024, 2048, extras=[x1], epilogue=lambda acc, r: (acc + r,))

    def head_fn(xb, t, g):
        rstd = lax.rsqrt(jnp.mean(xb * xb, axis=-1, keepdims=True) + EPS)
        xhat = xb * rstd
        err = xhat * g - t
        loss = 0.5 * jnp.sum(jnp.mean(err * err, axis=-1, keepdims=True), axis=0, keepdims=True)
        dy = err * (1.0 / D)
        dg = jnp.sum(dy * xhat, axis=0, keepdims=True)
        dxhat = dy * g
        dx = rstd * (dxhat - xhat * jnp.mean(dxhat * xhat, axis=-1, keepdims=True))
        return (dx, dx), (dg, jnp.broadcast_to(loss, (1, LANES)))
    (dx2, dx2_b), (dg_final, loss_part) = _row_call(
        "loss_head", head_fn, [x2, tgt], [g_final], [(D, F32), (D, BF16)], [D, LANES], TR)

    (da,) = _mm_nt("ff2_dx", dx2_b, w_ff2_full, [BF16], 2048, 1024, 2048, extras=[a_act],
                   epilogue=lambda acc, a: (2.0 * a.astype(F32) * acc,))
    dw_ff2, dw_ff2_b = _mm_tn("ff2_dw", a_sq, dx2_b, [F32, BF16], 1024, 2048, 1024)
    tm2, tk2 = min(2048, D), min(1024, S)
    dw1_spec = pl.BlockSpec((None, tm2, FB), lambda i, j, k: (j, i, 0))
    (dw_ff1, dw_ff1_b), ((r1_ff2,),) = _mm(
        "ff1_dw", (D // tm2, DFF // FB, S // tk2), h2, pl.BlockSpec((tk2, tm2), lambda i, j, k: (k, i)),
        da, pl.BlockSpec((tk2, FB), lambda i, j, k: (k, j)), TN, (tm2, FB),
        [jax.ShapeDtypeStruct((N_DEV, D, FB), F32), jax.ShapeDtypeStruct((N_DEV, D, FB), BF16)], [dw1_spec] * 2,
        epilogue=lambda acc: (acc, acc), jobs=[_job_scatter_sibling(dw_ff2_b.reshape(4, 2, FB, D))])
    h_ff2, hb_ff2 = _rs_add1("rs_add1_w_ff2", dw_ff2.reshape(4, 2, FB, D), r1_ff2, c_idx)
    tkb = min(1024, FB)
    nkb = FB // tkb
    tnb = min(1024, D)
    (dh2,), ((r2_ff2,), (r1_ff1,)) = _mm(
        "ff1_dx", (S // tm, D // tnb, DFF // tkb), da, pl.BlockSpec((tm, tkb), lambda i, j, k: (i, k)),
        w_ff1_all, pl.BlockSpec((None, tnb, tkb), lambda i, j, k: (k // nkb, j, k % nkb)), NT, (tm, tnb),
        [jax.ShapeDtypeStruct((S, D), F32)], [pl.BlockSpec((tm, tnb), lambda i, j, k: (i, j))],
        jobs=[_job_scatter_chips(hb_ff2), _job_scatter_sibling(dw_ff1_b.reshape(4, 2, D, FB))])
    g_w_ff2 = _rs_add2("rs_add2_w_ff2", h_ff2, r2_ff2, p_idx)
    h_ff1, hb_ff1 = _rs_add1("rs_add1_w_ff1", dw_ff1.reshape(4, 2, D, FB), r1_ff1, c_idx)

    def ffn_bwd_fn(dh, xb, dres, g):
        dx, dg = _rms_bwd(dh, xb, g)
        dx = dx + dres
        return (dx, dx), (dg,)
    (dx1, dx1_b), (dg_ffn,) = _row_call("rms_ffn_bwd", ffn_bwd_fn, [dh2, x1, dx2], [norm_ffn_g],
                                        [(D, F32), (D, BF16)], [D], TR)

    (dmerged,) = _mm_nt("out_proj_dx", dx1_b, w_out_full, [F32], 1024, 1024, 2048)
    dw_out, dw_out_b = _mm_tn("out_proj_dw", merged, dx1_b, [F32, BF16], 2048, 1024, 1024)

    def merge_bwd_fn(dm, a, g, ga, gg):
        da_, dga = _rms_bwd(dm[:, :DA], a, ga)
        dg_, dgg = _rms_bwd(dm[:, DA:], g, gg)
        return (da_, dg_), (dga, dgg)
    (dattn, dgm), (dg_attn, dg_gmlp) = _row_call(
        "rms_merge_bwd", merge_bwd_fn, [dmerged, attn, gm], [attn_out_g, gmlp_out_g], [(DA, F32), (DG, F32)], [DA, DG], TR)

    w_st = jnp.swapaxes(w_s[0], 1, 2)
    dzu, dzv, dw_s, dbs_col, dln_g, dln_b = _gmlp_bwd(dgm, zm, gmlp_ln_g, gmlp_ln_b, w_s[0], w_st, bs_col, TR)

    delta_row = _attn_delta(dattn, attn, TR)
    lse_row3 = lse_col3.reshape(H, NB, 1, T_ATT)
    (dq, ds_rowsum), ((r2_ff1,), (r1_out,)) = _attn2_bwd_dq(
        zm, dattn, f_col3, f_row3, lse_col3, delta_row.reshape(H, S, 1), T_ATT,
        jobs=[_job_scatter_chips(hb_ff1), _job_scatter_sibling(dw_out_b.reshape(4, 2, D // N_DEV, D))])
    g_w_ff1 = _rs_add2("rs_add2_w_ff1", h_ff1, r2_ff1, p_idx)
    h_out, hb_out = _rs_add1("rs_add1_w_out", dw_out.reshape(4, 2, D // N_DEV, D), r1_out, c_idx)
    (dk, dv, df_col3), ((r2_out,),) = _attn2_bwd_dkv(
        zm, dattn, f_col3, f_row3, lse_row3, delta_row.reshape(H, NB, 1, T_ATT),
        ds_rowsum.reshape(H, NB, 1, T_ATT), T_ATT,
        jobs=[_job_scatter_chips(hb_out)])
    g_w_out = _rs_add2("rs_add2_w_out", h_out, r2_out, p_idx)
    dzf, dbf = _fgate_bwd(df_col3.reshape(H, S), zf, bf_pad)

    dz_main = jnp.concatenate([dq, dk, dv, dzu, dzv], axis=1)
    dw_main, dw_main_b = _mm_tn("in_proj_dw", h, dz_main, [F32, BF16], 2048, 1024, 1024)
    (dw_f,) = _mm_tn("in_proj_f_dw", h, dzf, [F32], 2048, LANES, 1024)
    (dh_f,), ((r1_in,),) = _mm_nt("in_proj_f_dx", dzf, w_f, [F32], 1024, 1024, LANES,
                                  jobs=[_job_scatter_sibling_windows(dw_main_b, starts, WW)])
    first_blocks = jnp.stack([jnp.where(c_pos == 0, starts[2 * p], starts[2 * p + 1]) // LANES
                              for p in range(4)]).astype(jnp.int32)
    h_in, hb_in = _rs_add1_windows("rs_add1_w_in", dw_main, r1_in, first_blocks)
    (dh,), ((r2_in,),) = _mm_nt("in_proj_dx", dz_main, w_main, [F32], 1024, 1024, 1024, extras=[dh_f],
                                epilogue=lambda acc, r: (acc + r,), jobs=[_job_scatter_chips(hb_in)])
    g_window = _rs_add2("rs_add2_w_in", h_in, r2_in, p_idx)

    def mix_bwd_fn(dhb, xb, dres, g):
        dx, dg = _rms_bwd(dhb, xb, g)
        return (dx + dres,), (dg,)
    (grad_x,), (dg_mix,) = _row_call("rms_mix_bwd", mix_bwd_fn, [dh, x0, dx1], [norm_mix_g], [(D, F32)], [D], TR)

    small_shapes = [norm_mix_g.shape, b_f.shape, gmlp_ln_g.shape, gmlp_ln_b.shape, w_s.shape, b_s.shape,
                    attn_out_g.shape, gmlp_out_g.shape, norm_ffn_g.shape, norm_final_g.shape]
    small_parts = [dg_mix, dbf[:, :H], dln_g, dln_b, dw_s, dbs_col, dg_attn, dg_gmlp, dg_ffn, dg_final]
    g_small = _sum8("small_sum", _all_gather("ag_small", _pack(small_parts + [dw_f[:, :H]])))
    no_gate = jnp.zeros((D, H), F32)
    w_small = _pack([norm_mix_g, b_f, gmlp_ln_g, gmlp_ln_b, w_s, b_s, attn_out_g, gmlp_out_g, norm_ffn_g, norm_final_g,
                     no_gate])
    m_small = _pack([m_norm_mix_g, m_b_f, m_gmlp_ln_g, m_gmlp_ln_b, m_w_s, m_b_s, m_attn_out_g, m_gmlp_out_g,
                     m_norm_ffn_g, m_norm_final_g, no_gate])
    v_small = _pack([v_norm_mix_g, v_b_f, v_gmlp_ln_g, v_gmlp_ln_b, v_w_s, v_b_s, v_attn_out_g, v_gmlp_out_g,
                     v_norm_ffn_g, v_norm_final_g, no_gate])
    (d_small, nm_small, nv_small), _ = _adamw("adamw_small", w_small, g_small, m_small, v_small)
    g_gate = _unpack(g_small, small_shapes + [(D, H)])[-1]

    g_plain = lax.dynamic_slice(g_window, (0, off), (D, w_in_cols))
    g_with_gate = jnp.concatenate([g_window[:, g0:g0 + n_before], g_gate,
                                   g_window[:, g0 + n_before:g0 + w_in_cols - H]], axis=1)
    g_w_in = jnp.where(me_idx == gate_dev, g_with_gate, g_plain)
    gs = _unpack(g_small, small_shapes)
    ds = _unpack(d_small, small_shapes)
    nms = _unpack(nm_small, small_shapes)
    nvs = _unpack(nv_small, small_shapes)

    big = {}
    for nm, w, g, m, v in (("w_in", w_in, g_w_in, m_w_in, v_w_in), ("w_out", w_out, g_w_out, m_w_out, v_w_out),
                           ("w_ff1", w_ff1, g_w_ff1, m_w_ff1, v_w_ff1), ("w_ff2", w_ff2, g_w_ff2, m_w_ff2, v_w_ff2)):
        (d_, m_, v_), _ = _adamw("adamw_" + nm, w[0], g, m[0], v[0])
        big[nm] = (g[None], d_[None], m_[None], v_[None])

    loss = lax.psum(loss_part[0, 0], ("x", "y", "c"))

    def leaves(n):
        sm = (gs, ds, nms, nvs)[n]
        return [sm[0], big["w_in"][n], sm[1], sm[2], sm[3], sm[4], sm[5], sm[6], sm[7], big["w_out"][n], sm[8],
                big["w_ff1"][n], big["w_ff2"][n], sm[9]]

    return (loss, grad_x[None], *leaves(0), *leaves(1), *leaves(2), *leaves(3))
```

```python
import functools
import math

import jax
import jax.numpy as jnp
from jax import lax
from jax.experimental import pallas as pl
from jax.experimental.pallas import tpu as pltpu

F32 = jnp.float32
BF16 = jnp.bfloat16
MESH = pl.DeviceIdType.MESH

HEAD_DIM = 128
CHUNK = 128
EPS = 1e-6
LANES = 128
N_DEV = 8

ADAM_LR = 0.001
ADAM_B1 = 0.9
ADAM_B2 = 0.999
ADAM_EPS = 1e-08
ADAM_WD = 0.01
ADAM_STEP = 10

VMEM_LIMIT_BYTES = 56 * 1024 * 1024
T_ATT_MAX = 1024
TR_MAX = 256

NN = ((1,), (0,))
NT = ((1,), (1,))
TN = ((0,), (0,))


def _params(sem=None):
    return pltpu.CompilerParams(dimension_semantics=sem, vmem_limit_bytes=VMEM_LIMIT_BYTES)


def _dot(a, b, contract=NN):
    return lax.dot_general(a, b, (contract, ((), ())), preferred_element_type=F32)


def _dot3(x, t):
    x1 = x.astype(BF16)
    r1 = x - x1.astype(F32)
    x2 = r1.astype(BF16)
    x3 = (r1 - x2.astype(F32)).astype(BF16)
    return _dot(x1, t) + _dot(x2, t) + _dot(x3, t)


def _iota2(shape, dim):
    return lax.broadcasted_iota(jnp.int32, shape, dim)


def _row_call(name, fn, row_ins, bcast_ins, row_outs, acc_outs, tr):
    S = row_ins[0].shape[0]
    assert S % tr == 0
    n_ri, n_bi, n_ro, n_ao = len(row_ins), len(bcast_ins), len(row_outs), len(acc_outs)

    def body(*refs):
        ins = [r[...] for r in refs[:n_ri + n_bi]]
        ro_refs = refs[n_ri + n_bi:n_ri + n_bi + n_ro]
        ao_refs = refs[n_ri + n_bi + n_ro:]
        ro, ao = fn(*ins)
        for r, v in zip(ro_refs, ro):
            r[...] = v.astype(r.dtype)
        if n_ao:
            @pl.when(pl.program_id(0) == 0)
            def _():
                for r in ao_refs:
                    r[...] = jnp.zeros_like(r)
            for r, v in zip(ao_refs, ao):
                r[...] += v

    in_specs = [pl.BlockSpec((tr, a.shape[1]), lambda i: (i, 0)) for a in row_ins]
    in_specs += [pl.BlockSpec(a.shape, lambda i: (0, 0)) for a in bcast_ins]
    out_specs = [pl.BlockSpec((tr, d), lambda i: (i, 0)) for d, _ in row_outs]
    out_specs += [pl.BlockSpec((1, d), lambda i: (0, 0)) for d in acc_outs]
    out_shape = [jax.ShapeDtypeStruct((S, d), dt) for d, dt in row_outs]
    out_shape += [jax.ShapeDtypeStruct((1, d), F32) for d in acc_outs]
    outs = pl.pallas_call(
        body, name=name, grid=(S // tr,), in_specs=in_specs, out_specs=out_specs, out_shape=out_shape,
        compiler_params=_params(("arbitrary",) if n_ao else ("parallel",)),
    )(*row_ins, *bcast_ins)
    return outs[:n_ro], outs[n_ro:]


def _rms_fwd(x, g):
    rstd = lax.rsqrt(jnp.mean(x * x, axis=-1, keepdims=True) + EPS)
    return x * rstd * g


def _rms_bwd(dy, x, g):
    rstd = lax.rsqrt(jnp.mean(x * x, axis=-1, keepdims=True) + EPS)
    xhat = x * rstd
    dg = jnp.sum(dy * xhat, axis=0, keepdims=True)
    dxhat = dy * g
    dx = rstd * (dxhat - xhat * jnp.mean(dxhat * xhat, axis=-1, keepdims=True))
    return dx, dg


_GELU_C = math.sqrt(2.0 / math.pi)


def _gelu(x):
    return 0.5 * x * (1.0 + jnp.tanh(_GELU_C * (x + 0.044715 * (x * x * x))))


def _gelu_grad(x):
    t = jnp.tanh(_GELU_C * (x + 0.044715 * (x * x * x)))
    return 0.5 * (1.0 + t) + 0.5 * x * (1.0 - t * t) * (_GELU_C * (1.0 + 3.0 * 0.044715 * (x * x)))


def _me():
    return lax.axis_index("x"), lax.axis_index("y"), lax.axis_index("c")


def _other_chips(x, y):
    return [(1 - x, y), (x, 1 - y), (1 - x, 1 - y)]


_ANY = pl.BlockSpec(memory_space=pl.ANY)


class _Job:
    def __init__(self, ins, outs, n_sems, make, aliases=None):
        self.ins, self.outs, self.n_sems, self.make, self.aliases = ins, outs, n_sems, make, aliases or {}


def _job_gather_chips(blk, part=(0, 1, 1), into=None):
    R, C = blk.shape
    nr = R // part[2]
    rows = pl.ds(part[0] * nr, (part[1] - part[0]) * nr)

    def make(ins, outs, send_sems, recv_sems, base):
        x_ref, (out_ref,) = ins[0], outs
        x, y, c = _me()
        mine = 4 * x + 2 * y + c
        targets = [(x, y, 1 - c)] + [(cx, cy, c) for cx, cy in _other_chips(x, y)]

        def copy(k, slab, to):
            return pltpu.make_async_remote_copy(
                src_ref=x_ref.at[rows, :], dst_ref=out_ref.at[slab, rows, :], send_sem=send_sems.at[base + k],
                recv_sem=recv_sems.at[base + k], device_id=to, device_id_type=MESH)

        starts = [copy(k, mine, to) for k, to in enumerate(targets)]
        arrivals = [copy(k, 4 * tx + 2 * ty + tc, (tx, ty, tc)) for k, (tx, ty, tc) in enumerate(targets)]
        local = [pltpu.make_async_copy(x_ref.at[rows, :], out_ref.at[mine, rows, :], send_sems.at[base + 4])]
        return starts, arrivals, local

    out = jax.ShapeDtypeStruct((N_DEV, R, C), blk.dtype)
    if into is None:
        return _Job([blk], [out], 5, make)
    return _Job([blk, into], [out], 5, make, aliases={1: 0})


def _job_gather_sibling(part):
    def make(ins, outs, send_sems, recv_sems, base):
        (out_ref,) = outs
        x, y, c = _me()

        def copy(k, slab):
            return pltpu.make_async_remote_copy(
                src_ref=out_ref.at[slab], dst_ref=out_ref.at[slab], send_sem=send_sems.at[base + k],
                recv_sem=recv_sems.at[base + k], device_id=(x, y, 1 - c), device_id_type=MESH)

        chips = _other_chips(x, y)
        starts = [copy(k, 4 * cx + 2 * cy + c) for k, (cx, cy) in enumerate(chips)]
        arrivals = [copy(k, 4 * cx + 2 * cy + (1 - c)) for k, (cx, cy) in enumerate(chips)]
        return starts, arrivals, []

    return _Job([part], [jax.ShapeDtypeStruct(part.shape, part.dtype)], 3, make, aliases={0: 0})


def _job_scatter_sibling(gb):
    _, _, R, C = gb.shape

    def make(ins, outs, send_sems, recv_sems, base):
        (g_ref,), (recv_ref,) = ins, outs
        x, y, c = _me()
        copies = [pltpu.make_async_remote_copy(
            src_ref=g_ref.at[p, 1 - c], dst_ref=recv_ref.at[p], send_sem=send_sems.at[base + p],
            recv_sem=recv_sems.at[base + p], device_id=(x, y, 1 - c), device_id_type=MESH) for p in range(4)]
        return copies, copies, []

    return _Job([gb], [jax.ShapeDtypeStruct((4, R, C), gb.dtype)], 4, make)


def _job_scatter_sibling_windows(gb, starts, width):
    R, _ = gb.shape

    def make(ins, outs, send_sems, recv_sems, base):
        (g_ref,), (recv_ref,) = ins, outs
        x, y, c = _me()
        copies = []
        for p in range(4):
            start = pl.multiple_of(jnp.where(c == 0, starts[2 * p + 1], starts[2 * p]), LANES)
            copies.append(pltpu.make_async_remote_copy(
                src_ref=g_ref.at[:, pl.ds(start, width)], dst_ref=recv_ref.at[p], send_sem=send_sems.at[base + p],
                recv_sem=recv_sems.at[base + p], device_id=(x, y, 1 - c), device_id_type=MESH))
        return copies, copies, []

    return _Job([gb], [jax.ShapeDtypeStruct((4, R, width), gb.dtype)], 4, make)


def _job_scatter_chips(hb):
    _, R, C = hb.shape

    def make(ins, outs, send_sems, recv_sems, base):
        (h_ref,), (recv_ref,) = ins, outs
        x, y, c = _me()
        copies = [pltpu.make_async_remote_copy(
            src_ref=h_ref.at[2 * cx + cy], dst_ref=recv_ref.at[n], send_sem=send_sems.at[base + n],
            recv_sem=recv_sems.at[base + n], device_id=(cx, cy, c), device_id_type=MESH)
            for n, (cx, cy) in enumerate(_other_chips(x, y))]
        return copies, copies, []

    return _Job([hb], [jax.ShapeDtypeStruct((3, R, C), hb.dtype)], 3, make)


def _carry_call(body, *, name, grid, in_specs, out_specs, out_shape, scratch_shapes, semantics, args, jobs=()):
    jobs = list(jobs)
    n_in, n_out, n_scr = len(in_specs), len(out_specs), len(scratch_shapes)
    j_ins = [a for j in jobs for a in j.ins]
    j_outs = [o for j in jobs for o in j.outs]
    n_sems = sum(j.n_sems for j in jobs)
    aliases = {}
    i0, o0 = n_in, n_out
    for j in jobs:
        for a, b in j.aliases.items():
            aliases[i0 + a] = o0 + b
        i0 += len(j.ins)
        o0 += len(j.outs)

    def full_body(*refs):
        ins = refs[:n_in]
        jin = refs[n_in:n_in + len(j_ins)]
        outs = refs[n_in + len(j_ins):n_in + len(j_ins) + n_out]
        jout = refs[n_in + len(j_ins) + n_out:n_in + len(j_ins) + n_out + len(j_outs)]
        scr = refs[n_in + len(j_ins) + n_out + len(j_outs):]
        if jobs:
            send_sems, recv_sems = scr[n_scr], scr[n_scr + 1]
            starts, arrivals, local = [], [], []
            base = i0 = o0 = 0
            for j in jobs:
                s, a, l = j.make(jin[i0:i0 + len(j.ins)], jout[o0:o0 + len(j.outs)], send_sems, recv_sems, base)
                starts += s
                arrivals += a
                local += l
                base += j.n_sems
                i0 += len(j.ins)
                o0 += len(j.outs)
            pids = [pl.program_id(d) for d in range(len(grid))]
            first = functools.reduce(jnp.logical_and, [p == 0 for p in pids])
            last = functools.reduce(jnp.logical_and, [p == n - 1 for p, n in zip(pids, grid)])

            @pl.when(first)
            def _():
                for cp in local + starts:
                    cp.start()

        body(*ins, *outs, *scr[:n_scr])

        if jobs:
            @pl.when(last)
            def _():
                for cp in arrivals:
                    cp.wait_recv()
                for cp in starts:
                    cp.wait_send()
                for cp in local:
                    cp.wait()

    sems = [pltpu.SemaphoreType.DMA((n_sems,)), pltpu.SemaphoreType.DMA((n_sems,))] if jobs else []
    res = pl.pallas_call(
        full_body, name=name, grid=grid,
        in_specs=list(in_specs) + [_ANY] * len(j_ins),
        out_specs=list(out_specs) + [_ANY] * len(j_outs),
        out_shape=list(out_shape) + j_outs,
        scratch_shapes=list(scratch_shapes) + sems,
        input_output_aliases=aliases,
        compiler_params=_params(("arbitrary",) * len(grid) if jobs else semantics),
    )(*args, *j_ins)
    body_res, job_res = res[:n_out], res[n_out:]
    per_job = []
    for j in jobs:
        per_job.append(job_res[:len(j.outs)])
        job_res = job_res[len(j.outs):]
    return body_res, per_job


def _run_jobs(name, jobs):
    def body(done_ref):
        done_ref[...] = jnp.zeros_like(done_ref)

    return _carry_call(body, name=name, grid=(1,), in_specs=[], out_specs=[pl.BlockSpec((8, LANES), lambda i: (0, 0))],
                       out_shape=[jax.ShapeDtypeStruct((8, LANES), F32)], scratch_shapes=[], semantics=("arbitrary",),
                       args=[], jobs=jobs)[1]


def _mm(name, grid, a, a_spec, b, b_spec, contract, acc_shape, out_shape, out_specs, extras=(), epilogue=None, jobs=()):
    nk = grid[2]
    n_e = len(extras)
    n_o = len(out_shape)
    if epilogue is None:
        epilogue = lambda acc: (acc,)

    def body(a_ref, b_ref, *rest):
        e_refs = rest[:n_e]
        o_refs = rest[n_e:n_e + n_o]

        def finish(total):
            res = epilogue(total, *[r[...] for r in e_refs])
            for o, r in zip(o_refs, res):
                o[...] = r.astype(o.dtype)

        if nk == 1:
            finish(_dot(a_ref[...], b_ref[...], contract))
            return
        acc = rest[n_e + n_o]
        k = pl.program_id(2)

        @pl.when(k == 0)
        def _():
            acc[...] = _dot(a_ref[...], b_ref[...], contract)

        @pl.when(jnp.logical_and(k > 0, k < nk - 1))
        def _():
            acc[...] += _dot(a_ref[...], b_ref[...], contract)

        @pl.when(k == nk - 1)
        def _():
            finish(acc[...] + _dot(a_ref[...], b_ref[...], contract))

    outs, job_res = _carry_call(
        body, name=name, grid=grid, in_specs=[a_spec, b_spec] + [s for _, s in extras],
        out_specs=list(out_specs), out_shape=list(out_shape),
        scratch_shapes=[pltpu.VMEM(acc_shape, F32)] if nk > 1 else [],
        semantics=("parallel", "parallel", "arbitrary"), args=[a, b] + [e for e, _ in extras], jobs=jobs)
    return (outs, job_res) if jobs else outs


def _mm_nn(name, a, b, out_dtypes, tm, tn, tk, extras=(), epilogue=None, jobs=()):
    M, K = a.shape
    N = b.shape[1]
    tm, tn, tk = min(tm, M), min(tn, N), min(tk, K)
    o_spec = pl.BlockSpec((tm, tn), lambda i, j, k: (i, j))
    return _mm(name, (M // tm, N // tn, K // tk),
               a, pl.BlockSpec((tm, tk), lambda i, j, k: (i, k)),
               b, pl.BlockSpec((tk, tn), lambda i, j, k: (k, j)), NN, (tm, tn),
               [jax.ShapeDtypeStruct((M, N), dt) for dt in out_dtypes], [o_spec] * len(out_dtypes),
               [(e, o_spec) for e in extras], epilogue, jobs)


def _mm_nt(name, a, b, out_dtypes, tm, tn, tk, extras=(), epilogue=None, jobs=()):
    M, K = a.shape
    N = b.shape[0]
    tm, tn, tk = min(tm, M), min(tn, N), min(tk, K)
    o_spec = pl.BlockSpec((tm, tn), lambda i, j, k: (i, j))
    return _mm(name, (M // tm, N // tn, K // tk),
               a, pl.BlockSpec((tm, tk), lambda i, j, k: (i, k)),
               b, pl.BlockSpec((tn, tk), lambda i, j, k: (j, k)), NT, (tm, tn),
               [jax.ShapeDtypeStruct((M, N), dt) for dt in out_dtypes], [o_spec] * len(out_dtypes),
               [(e, o_spec) for e in extras], epilogue, jobs)


def _mm_tn(name, a, b, out_dtypes, tm, tn, tk, jobs=()):
    K, M = a.shape
    N = b.shape[1]
    tm, tn, tk = min(tm, M), min(tn, N), min(tk, K)
    o_spec = pl.BlockSpec((tm, tn), lambda i, j, k: (i, j))
    return _mm(name, (M // tm, N // tn, K // tk),
               a, pl.BlockSpec((tk, tm), lambda i, j, k: (k, i)),
               b, pl.BlockSpec((tk, tn), lambda i, j, k: (k, j)), TN, (tm, tn),
               [jax.ShapeDtypeStruct((M, N), dt) for dt in out_dtypes], [o_spec] * len(out_dtypes),
               epilogue=lambda acc: (acc,) * len(out_dtypes), jobs=jobs)


def _fgate_fwd(zf, bf):
    S = zf.shape[0]
    nc = S // CHUNK

    def body(zf_ref, bf_ref, f_ref):
        upper = (_iota2((CHUNK, CHUNK), 0) <= _iota2((CHUNK, CHUNK), 1)).astype(BF16)
        carry = jnp.zeros((8, 1), F32)
        for c in range(nc):
            t = zf_ref[c * CHUNK:(c + 1) * CHUNK, :] + bf_ref[...]
            lf = jnp.minimum(t, 0.0) - jnp.log(1.0 + jnp.exp(-jnp.abs(t)))
            lf_rows = lf.T[0:8, :]
            f_ref[:, c * CHUNK:(c + 1) * CHUNK] = (_dot3(lf_rows, upper) + carry) * LOG2E
            carry = carry + jnp.sum(lf_rows, axis=-1, keepdims=True)

    return pl.pallas_call(
        body, name="fgate_fwd", out_shape=jax.ShapeDtypeStruct((8, S), F32),
        compiler_params=_params(),
    )(zf, bf)


def _fgate_bwd(df, zf, bf):
    S = zf.shape[0]
    nc = S // CHUNK

    def body(df_ref, zf_ref, bf_ref, dzf_ref, dbf_ref):
        lower = (_iota2((CHUNK, CHUNK), 0) >= _iota2((CHUNK, CHUNK), 1)).astype(BF16)
        carry = jnp.zeros((8, 1), F32)
        dbf = jnp.zeros((1, LANES), F32)
        for c in reversed(range(nc)):
            sl = slice(c * CHUNK, (c + 1) * CHUNK)
            df = df_ref[:, sl]
            r = _dot3(df, lower) + carry
            carry = carry + jnp.sum(df, axis=-1, keepdims=True)
            r_cols = jnp.concatenate([r, jnp.zeros((CHUNK - 8, CHUNK), F32)], axis=0).T
            t = zf_ref[sl, :] + bf_ref[...]
            dz = r_cols * (1.0 / (1.0 + jnp.exp(t)))
            dzf_ref[sl, :] = dz.astype(BF16)
            dbf = dbf + jnp.sum(dz, axis=0, keepdims=True)
        dbf_ref[...] = dbf

    return pl.pallas_call(
        body, name="fgate_bwd",
        out_shape=[jax.ShapeDtypeStruct((S, LANES), BF16), jax.ShapeDtypeStruct((1, LANES), F32)],
        compiler_params=_params(),
    )(df, zf, bf)


_NEG = -1e30
LOG2E = 1.4426950408889634
N_SPLIT = 8
N_SPLIT_DIAG = 2
DIAG_STEP = 1024


def _attn_consts(T):
    rows, cols = _iota2((T, T), 0), _iota2((T, T), 1)
    return cols <= rows, rows <= cols


def _attn2_fwd(zm, f2col, f2row, T, jobs=()):
    S = zm.shape[0]
    H = f2col.shape[0]
    nb = S // T
    c2 = LOG2E / math.sqrt(HEAD_DIM)

    def body(q_ref, k_ref, v_ref, fq_ref, fk_ref, o_ref, lse_ref, vaug_s):
        i = pl.program_id(1)

        @pl.when(i == 0)
        def _():
            vaug_s[:, :HEAD_DIM] = v_ref[...]
            vaug_s[:, HEAD_DIM:] = jnp.ones((S, HEAD_DIM), BF16)

        keep = _attn_consts(T)[0]
        TH = T // N_SPLIT

        def block(j, diagonal, state):
            r0 = pl.multiple_of(j * T, T)
            fk = fk_ref[j]
            new = []
            for g, (m_old, acc) in enumerate(state):
                rows = slice(g * TH, (g + 1) * TH)
                nk = min(T, -(-(g + 1) * TH // DIAG_STEP) * DIAG_STEP) if diagonal else T
                s = _dot(q_ref[rows, :], k_ref[pl.ds(r0, nk), :], NT) * c2 + (fq_ref[rows, :] - fk[:, :nk])
                if diagonal:
                    s = jnp.where(keep[rows, :nk], s, _NEG)
                m_new = jnp.maximum(m_old, jnp.max(s, axis=-1, keepdims=True))
                p = jnp.exp2(s - m_new).astype(BF16)
                new.append((m_new, jnp.exp2(m_old - m_new) * acc + _dot(p, vaug_s[pl.ds(r0, nk), :])))
            return tuple(new)

        init = tuple((jnp.full((TH, 1), _NEG, F32), jnp.zeros((TH, 2 * HEAD_DIM), F32)) for _ in range(N_SPLIT))
        state = lax.fori_loop(0, i, lambda j, st: block(j, False, st), init)
        state = block(i, True, state)
        for g, (m, acc) in enumerate(state):
            rows = slice(g * TH, (g + 1) * TH)
            o_ref[rows, :] = acc[:, :HEAD_DIM] / acc[:, HEAD_DIM:]
            lse_ref[rows, :] = m + jnp.log2(acc[:, HEAD_DIM:HEAD_DIM + 1])

    nh = H
    return _carry_call(
        body, name="attn_fwd", grid=(H, nb), jobs=jobs, args=[zm, zm, zm, f2col, f2row],
        semantics=("arbitrary", "arbitrary"),
        in_specs=[
            pl.BlockSpec((T, HEAD_DIM), lambda h, i: (i, h)),
            pl.BlockSpec((S, HEAD_DIM), lambda h, i: (0, nh + h)),
            pl.BlockSpec((S, HEAD_DIM), lambda h, i: (0, 2 * nh + h)),
            pl.BlockSpec((None, T, 1), lambda h, i: (h, i, 0)),
            pl.BlockSpec((None, nb, 1, T), lambda h, i: (h, 0, 0, 0)),
        ],
        out_specs=[pl.BlockSpec((T, HEAD_DIM), lambda h, i: (i, h)), pl.BlockSpec((None, T, 1), lambda h, i: (h, i, 0))],
        out_shape=[jax.ShapeDtypeStruct((S, H * HEAD_DIM), F32), jax.ShapeDtypeStruct((H, S, 1), F32)],
        scratch_shapes=[pltpu.VMEM((S, 2 * HEAD_DIM), BF16)],
    )


def _attn2_bwd_dq(zm, dattn, f2col, f2row, lse2_col, delta_col, T, jobs=()):
    S = zm.shape[0]
    H = f2col.shape[0]
    nb = S // T
    scale = 1.0 / math.sqrt(HEAD_DIM)
    c2 = LOG2E * scale

    def body(q_ref, k_ref, v_ref, do_ref, fq_ref, fk_ref, lse_ref, dl_ref, dq_ref, rs_ref, bias_s, do_s):
        i = pl.program_id(1)
        keep = _attn_consts(T)[0]
        TH = T // N_SPLIT_DIAG
        bias_s[...] = fq_ref[...] - lse_ref[...]
        do_s[...] = do_ref[...].astype(BF16)

        def part(rows, j, nk, state, masked):
            acc, rs = state
            r0 = pl.multiple_of(j * T, T)
            kb = k_ref[pl.ds(r0, nk), :]
            s = _dot(q_ref[rows, :], kb, NT) * c2 + (bias_s[rows, :] - fk_ref[j][:, :nk])
            if masked:
                s = jnp.where(keep[rows, :nk], s, _NEG)
            ds = jnp.exp2(s) * (_dot(do_s[rows, :], v_ref[pl.ds(r0, nk), :], NT) - dl_ref[rows, :])
            return acc + _dot(ds.astype(BF16), kb), rs + jnp.sum(ds, axis=-1, keepdims=True)

        def step(j, state):
            return part(slice(0, T), j, T, state, False)

        acc, rs = lax.fori_loop(0, i, step, (jnp.zeros((T, HEAD_DIM), F32), jnp.zeros((T, 1), F32)))
        for g in range(N_SPLIT_DIAG):
            rows = slice(g * TH, (g + 1) * TH)
            acc_g, rs_g = part(rows, i, (g + 1) * TH, (acc[rows, :], rs[rows, :]), True)
            dq_ref[rows, :] = (acc_g * scale).astype(BF16)
            rs_ref[rows, :] = rs_g

    nh = H
    col = pl.BlockSpec((None, T, 1), lambda h, i: (h, i, 0))
    blk = pl.BlockSpec((T, HEAD_DIM), lambda h, i: (i, h))
    return _carry_call(
        body, name="attn_bwd_dq", grid=(H, nb), jobs=jobs,
        args=[zm, zm, zm, dattn, f2col, f2row, lse2_col, delta_col], semantics=("arbitrary", "arbitrary"),
        in_specs=[
            blk,
            pl.BlockSpec((S, HEAD_DIM), lambda h, i: (0, nh + h)),
            pl.BlockSpec((S, HEAD_DIM), lambda h, i: (0, 2 * nh + h)),
            blk, col,
            pl.BlockSpec((None, nb, 1, T), lambda h, i: (h, 0, 0, 0)),
            col, col,
        ],
        out_specs=[blk, col],
        out_shape=[jax.ShapeDtypeStruct((S, H * HEAD_DIM), BF16), jax.ShapeDtypeStruct((H, S, 1), F32)],
        scratch_shapes=[pltpu.VMEM((T, 1), F32), pltpu.VMEM((T, HEAD_DIM), BF16)],
    )


def _attn2_bwd_dkv(zm, dattn, f2col, f2row, lse2_row, delta_row, rowsum_row, T, jobs=()):
    S = zm.shape[0]
    H = f2col.shape[0]
    nb = S // T
    scale = 1.0 / math.sqrt(HEAD_DIM)
    c2 = LOG2E * scale

    def body(q_ref, k_ref, v_ref, do_ref, fk_ref, fq_ref, lse_ref, dl_ref, rs_ref, dk_ref, dv_ref, df_ref):
        j = pl.program_id(1)
        keep = _attn_consts(T)[1]
        TH = T // N_SPLIT_DIAG

        def part(rows, i, c0, state, masked):
            dk, dv, df = state
            r0 = pl.multiple_of(i * T + c0, TH)
            qb = q_ref[pl.ds(r0, T - c0), :]
            do = do_ref[pl.ds(r0, T - c0), :].astype(BF16)
            bias = (fq_ref[i] - lse_ref[i])[:, c0:]
            dl = (dl_ref[i] + rs_ref[i])[:, c0:]
            st = _dot(k_ref[rows, :], qb, NT) * c2 + (bias - fk_ref[rows, :])
            if masked:
                st = jnp.where(keep[rows, c0:], st, _NEG)
            pt = jnp.exp2(st)
            dst = pt * (_dot(v_ref[rows, :], do, NT) - dl)
            return (dk + _dot(dst.astype(BF16), qb), dv + _dot(pt.astype(BF16), do),
                    df - jnp.sum(dst, axis=-1, keepdims=True))

        groups = []
        for g in range(N_SPLIT_DIAG):
            zero = (jnp.zeros((TH, HEAD_DIM), F32), jnp.zeros((TH, HEAD_DIM), F32), jnp.zeros((TH, 1), F32))
            groups.append(part(slice(g * TH, (g + 1) * TH), j, g * TH, zero, True))
        state = tuple(jnp.concatenate([grp[n] for grp in groups], axis=0) for n in range(3))
        dk, dv, df = lax.fori_loop(j + 1, nb, lambda i, st: part(slice(0, T), i, 0, st, False), state)
        dk_ref[...] = (dk * scale).astype(BF16)
        dv_ref[...] = dv.astype(BF16)
        df_ref[...] = df

    nh = H
    row = pl.BlockSpec((None, nb, 1, T), lambda h, j: (h, 0, 0, 0))
    whole = pl.BlockSpec((S, HEAD_DIM), lambda h, j: (0, h))
    kv_out = pl.BlockSpec((T, HEAD_DIM), lambda h, j: (j, h))
    col = pl.BlockSpec((None, T, 1), lambda h, j: (h, j, 0))
    return _carry_call(
        body, name="attn_bwd_dkv", grid=(H, nb), jobs=jobs,
        args=[zm, zm, zm, dattn, f2col, f2row, lse2_row, delta_row, rowsum_row],
        semantics=("arbitrary", "arbitrary"),
        in_specs=[
            whole,
            pl.BlockSpec((T, HEAD_DIM), lambda h, j: (j, nh + h)),
            pl.BlockSpec((T, HEAD_DIM), lambda h, j: (j, 2 * nh + h)),
            whole, col, row, row, row, row,
        ],
        out_specs=[kv_out, kv_out, col],
        out_shape=[jax.ShapeDtypeStruct((S, H * HEAD_DIM), BF16), jax.ShapeDtypeStruct((S, H * HEAD_DIM), BF16),
                   jax.ShapeDtypeStruct((H, S, 1), F32)],
        scratch_shapes=[],
    )


def _attn_fwd(zm, fcol, frow, T, jobs=()):
    S = zm.shape[0]
    H = fcol.shape[0]
    nb = S // T
    scale = 1.0 / math.sqrt(HEAD_DIM)

    def body(q_ref, k_ref, v_ref, fq_ref, fk_ref, o_ref, lse_ref, m_s, l_s, acc_s):
        i = pl.program_id(1)
        j = pl.program_id(2)

        @pl.when(j == 0)
        def _():
            m_s[...] = jnp.full_like(m_s, _NEG)
            l_s[...] = jnp.zeros_like(l_s)
            acc_s[...] = jnp.zeros_like(acc_s)

        @pl.when(j <= i)
        def _():
            s = _dot(q_ref[...], k_ref[...], NT) * scale + (fq_ref[...] - fk_ref[...])
            keep = (_iota2((T, T), 1) + j * T) <= (_iota2((T, T), 0) + i * T)
            s = jnp.where(keep, s, _NEG)
            m_new = jnp.maximum(m_s[...], jnp.max(s, axis=-1, keepdims=True))
            alpha = jnp.exp(m_s[...] - m_new)
            p = jnp.exp(s - m_new)
            l_s[...] = alpha * l_s[...] + jnp.sum(p, axis=-1, keepdims=True)
            acc_s[...] = alpha * acc_s[...] + _dot(p.astype(BF16), v_ref[...])
            m_s[...] = m_new

        @pl.when(j == nb - 1)
        def _():
            o_ref[...] = acc_s[...] / l_s[...]
            lse_ref[...] = m_s[...] + jnp.log(l_s[...])

    nh = H
    return _carry_call(
        body, name="attn_fwd", grid=(H, nb, nb), jobs=jobs, args=[zm, zm, zm, fcol, frow],
        semantics=("parallel", "parallel", "arbitrary"),
        in_specs=[
            pl.BlockSpec((T, HEAD_DIM), lambda h, i, j: (i, h)),
            pl.BlockSpec((T, HEAD_DIM), lambda h, i, j: (jnp.minimum(j, i), nh + h)),
            pl.BlockSpec((T, HEAD_DIM), lambda h, i, j: (jnp.minimum(j, i), 2 * nh + h)),
            pl.BlockSpec((None, T, 1), lambda h, i, j: (h, i, 0)),
            pl.BlockSpec((None, 1, T), lambda h, i, j: (h, 0, jnp.minimum(j, i))),
        ],
        out_specs=[
            pl.BlockSpec((T, HEAD_DIM), lambda h, i, j: (i, h)),
            pl.BlockSpec((None, T, 1), lambda h, i, j: (h, i, 0)),
        ],
        out_shape=[jax.ShapeDtypeStruct((S, H * HEAD_DIM), F32), jax.ShapeDtypeStruct((H, S, 1), F32)],
        scratch_shapes=[pltpu.VMEM((T, 1), F32), pltpu.VMEM((T, 1), F32), pltpu.VMEM((T, HEAD_DIM), F32)],
    )


def _attn_delta(dattn, attn, tr):
    S, DA = attn.shape
    H = DA // HEAD_DIM

    def body(do_ref, o_ref, out_ref):
        lo = _iota2((DA, LANES), 1) * HEAD_DIM
        sel = ((_iota2((DA, LANES), 0) >= lo) & (_iota2((DA, LANES), 0) < lo + HEAD_DIM)).astype(BF16)
        d = _dot3(do_ref[...] * o_ref[...], sel)
        for c in range(tr // CHUNK):
            out_ref[:, c * CHUNK:(c + 1) * CHUNK] = d[c * CHUNK:(c + 1) * CHUNK, :].T[0:H, :]

    return pl.pallas_call(
        body, name="attn_delta", grid=(S // tr,),
        in_specs=[pl.BlockSpec((tr, DA), lambda i: (i, 0))] * 2,
        out_specs=pl.BlockSpec((H, tr), lambda i: (0, i)),
        out_shape=jax.ShapeDtypeStruct((H, S), F32),
        compiler_params=_params(("parallel",)),
    )(dattn, attn)


def _attn_bwd_dq(zm, dattn, fcol, frow, lse_col, delta_col, T, jobs=()):
    S = zm.shape[0]
    H = fcol.shape[0]
    nb = S // T
    scale = 1.0 / math.sqrt(HEAD_DIM)

    def body(q_ref, k_ref, v_ref, do_ref, fq_ref, fk_ref, lse_ref, dl_ref, dq_ref, rs_ref, acc_s, rs_s):
        i = pl.program_id(1)
        j = pl.program_id(2)

        @pl.when(j == 0)
        def _():
            acc_s[...] = jnp.zeros_like(acc_s)
            rs_s[...] = jnp.zeros_like(rs_s)

        @pl.when(j <= i)
        def _():
            s = _dot(q_ref[...], k_ref[...], NT) * scale + (fq_ref[...] - fk_ref[...])
            keep = (_iota2((T, T), 1) + j * T) <= (_iota2((T, T), 0) + i * T)
            p = jnp.exp(jnp.where(keep, s - lse_ref[...], _NEG))
            dp = _dot(do_ref[...].astype(BF16), v_ref[...], NT)
            ds = p * (dp - dl_ref[...])
            acc_s[...] += _dot(ds.astype(BF16), k_ref[...])
            rs_s[...] += jnp.sum(ds, axis=-1, keepdims=True)

        @pl.when(j == nb - 1)
        def _():
            dq_ref[...] = (acc_s[...] * scale).astype(BF16)
            rs_ref[...] = rs_s[...]

    nh = H
    col = pl.BlockSpec((None, T, 1), lambda h, i, j: (h, i, 0))
    return _carry_call(
        body, name="attn_bwd_dq", grid=(H, nb, nb), jobs=jobs,
        args=[zm, zm, zm, dattn, fcol, frow, lse_col, delta_col], semantics=("parallel", "parallel", "arbitrary"),
        in_specs=[
            pl.BlockSpec((T, HEAD_DIM), lambda h, i, j: (i, h)),
            pl.BlockSpec((T, HEAD_DIM), lambda h, i, j: (jnp.minimum(j, i), nh + h)),
            pl.BlockSpec((T, HEAD_DIM), lambda h, i, j: (jnp.minimum(j, i), 2 * nh + h)),
            pl.BlockSpec((T, HEAD_DIM), lambda h, i, j: (i, h)),
            col,
            pl.BlockSpec((None, 1, T), lambda h, i, j: (h, 0, jnp.minimum(j, i))),
            col, col,
        ],
        out_specs=[pl.BlockSpec((T, HEAD_DIM), lambda h, i, j: (i, h)), col],
        out_shape=[jax.ShapeDtypeStruct((S, H * HEAD_DIM), BF16), jax.ShapeDtypeStruct((H, S, 1), F32)],
        scratch_shapes=[pltpu.VMEM((T, HEAD_DIM), F32), pltpu.VMEM((T, 1), F32)],
    )


def _attn_bwd_dkv(zm, dattn, fcol, frow, lse_row, delta_row, rowsum_row, T, jobs=()):
    S = zm.shape[0]
    H = fcol.shape[0]
    nb = S // T
    scale = 1.0 / math.sqrt(HEAD_DIM)

    def body(q_ref, k_ref, v_ref, do_ref, fk_ref, fq_ref, lse_ref, dl_ref, rs_ref,
             dk_ref, dv_ref, df_ref, dk_s, dv_s, df_s):
        j = pl.program_id(1)
        i = pl.program_id(2)

        @pl.when(i == 0)
        def _():
            dk_s[...] = jnp.zeros_like(dk_s)
            dv_s[...] = jnp.zeros_like(dv_s)
            df_s[...] = jnp.zeros_like(df_s)

        @pl.when(i >= j)
        def _():
            st = _dot(k_ref[...], q_ref[...], NT) * scale + (fq_ref[...] - fk_ref[...])
            keep = (_iota2((T, T), 0) + j * T) <= (_iota2((T, T), 1) + i * T)
            pt = jnp.exp(jnp.where(keep, st - lse_ref[...], _NEG))
            do = do_ref[...].astype(BF16)
            dpt = _dot(v_ref[...], do, NT)
            dst = pt * (dpt - (dl_ref[...] + rs_ref[...]))
            dv_s[...] += _dot(pt.astype(BF16), do)
            dk_s[...] += _dot(dst.astype(BF16), q_ref[...])
            df_s[...] -= jnp.sum(dst, axis=-1, keepdims=True)

        @pl.when(i == nb - 1)
        def _():
            dk_ref[...] = (dk_s[...] * scale).astype(BF16)
            dv_ref[...] = dv_s[...].astype(BF16)
            df_ref[...] = df_s[...]

    nh = H
    row = pl.BlockSpec((None, 1, T), lambda h, j, i: (h, 0, jnp.maximum(i, j)))
    kv_out = pl.BlockSpec((T, HEAD_DIM), lambda h, j, i: (j, h))
    return _carry_call(
        body, name="attn_bwd_dkv", grid=(H, nb, nb), jobs=jobs,
        args=[zm, zm, zm, dattn, fcol, frow, lse_row, delta_row, rowsum_row],
        semantics=("parallel", "parallel", "arbitrary"),
        in_specs=[
            pl.BlockSpec((T, HEAD_DIM), lambda h, j, i: (jnp.maximum(i, j), h)),
            pl.BlockSpec((T, HEAD_DIM), lambda h, j, i: (j, nh + h)),
            pl.BlockSpec((T, HEAD_DIM), lambda h, j, i: (j, 2 * nh + h)),
            pl.BlockSpec((T, HEAD_DIM), lambda h, j, i: (jnp.maximum(i, j), h)),
            pl.BlockSpec((None, T, 1), lambda h, j, i: (h, j, 0)),
            row, row, row, row,
        ],
        out_specs=[kv_out, kv_out, pl.BlockSpec((None, T, 1), lambda h, j, i: (h, j, 0))],
        out_shape=[jax.ShapeDtypeStruct((S, H * HEAD_DIM), BF16), jax.ShapeDtypeStruct((S, H * HEAD_DIM), BF16),
                   jax.ShapeDtypeStruct((H, S, 1), F32)],
        scratch_shapes=[pltpu.VMEM((T, HEAD_DIM), F32), pltpu.VMEM((T, HEAD_DIM), F32), pltpu.VMEM((T, 1), F32)],
    )


def _ln_stats(x):
    mu = jnp.mean(x, axis=-1, keepdims=True)
    xc = x - mu
    rstd = lax.rsqrt(jnp.mean(xc * xc, axis=-1, keepdims=True) + EPS)
    return xc * rstd, rstd


def _tril_mask():
    return _iota2((CHUNK, CHUNK), 0) >= _iota2((CHUNK, CHUNK), 1)


def _gmlp_fwd(zm, ln_g, ln_b, w_s, bs_col, tr):
    S = zm.shape[0]
    H = w_s.shape[0]
    DG = H * HEAD_DIM

    def body(zu_ref, zv_ref, g_ref, b_ref, w_ref, bs_ref, out_ref):
        u = _gelu(zu_ref[...].astype(F32))
        y, _ = _ln_stats(_gelu(zv_ref[...].astype(F32)))
        v = (y * g_ref[...] + b_ref[...]).astype(BF16)
        mask = _tril_mask()
        for h in range(H):
            wc = jnp.where(mask, w_ref[h], 0.0).astype(BF16)
            cs = slice(h * HEAD_DIM, (h + 1) * HEAD_DIM)
            for c in range(tr // CHUNK):
                rs = slice(c * CHUNK, (c + 1) * CHUNK)
                mix = _dot(wc, v[rs, cs]) + bs_ref[h]
                out_ref[rs, cs] = u[rs, cs] * mix

    full = lambda a: pl.BlockSpec(a.shape, lambda i: (0,) * a.ndim)
    return pl.pallas_call(
        body, name="gmlp_fwd", grid=(S // tr,),
        in_specs=[pl.BlockSpec((tr, DG), lambda i: (i, 3)), pl.BlockSpec((tr, DG), lambda i: (i, 4)),
                  full(ln_g), full(ln_b), full(w_s), full(bs_col)],
        out_specs=pl.BlockSpec((tr, DG), lambda i: (i, 0)),
        out_shape=jax.ShapeDtypeStruct((S, DG), F32),
        compiler_params=_params(("parallel",)),
    )(zm, zm, ln_g, ln_b, w_s, bs_col)


def _gmlp_bwd(dgm, zm, ln_g, ln_b, w_s, w_st, bs_col, tr):
    S = zm.shape[0]
    H = w_s.shape[0]
    DG = H * HEAD_DIM

    def body(dg_ref, zu_ref, zv_ref, g_ref, b_ref, w_ref, wt_ref, bs_ref,
             dzu_ref, dzv_ref, dw_ref, dbs_ref, dlg_ref, dlb_ref, dv_s):
        @pl.when(pl.program_id(0) == 0)
        def _():
            dw_ref[...] = jnp.zeros_like(dw_ref)
            dbs_ref[...] = jnp.zeros_like(dbs_ref)
            dlg_ref[...] = jnp.zeros_like(dlg_ref)
            dlb_ref[...] = jnp.zeros_like(dlb_ref)

        zu = zu_ref[...].astype(F32)
        zv = zv_ref[...].astype(F32)
        u = _gelu(zu)
        y, rstd = _ln_stats(_gelu(zv))
        v = (y * g_ref[...] + b_ref[...]).astype(BF16)
        dgm_blk = dg_ref[...]
        mask = _tril_mask()
        mask_t = _iota2((CHUNK, CHUNK), 0) <= _iota2((CHUNK, CHUNK), 1)
        for h in range(H):
            wc = jnp.where(mask, w_ref[h], 0.0).astype(BF16)
            wct = jnp.where(mask_t, wt_ref[h], 0.0).astype(BF16)
            cs = slice(h * HEAD_DIM, (h + 1) * HEAD_DIM)
            dw = jnp.zeros((CHUNK, CHUNK), F32)
            dbs = jnp.zeros((CHUNK, 1), F32)
            for c in range(tr // CHUNK):
                rs = slice(c * CHUNK, (c + 1) * CHUNK)
                vch = v[rs, cs]
                mix = _dot(wc, vch) + bs_ref[h]
                dg = dgm_blk[rs, cs]
                dzu_ref[rs, cs] = (dg * mix * _gelu_grad(zu[rs, cs])).astype(BF16)
                dmix = dg * u[rs, cs]
                dbs = dbs + jnp.sum(dmix, axis=-1, keepdims=True)
                dmix_b = dmix.astype(BF16)
                dw = dw + _dot(dmix_b, vch, NT)
                dv_s[rs, cs] = _dot(wct, dmix_b)
            dw_ref[h] += jnp.where(mask, dw, 0.0)
            dbs_ref[h] += dbs
        dv = dv_s[...]
        dlg_ref[...] += jnp.sum(dv * y, axis=0, keepdims=True)
        dlb_ref[...] += jnp.sum(dv, axis=0, keepdims=True)
        dy = dv * g_ref[...]
        dgv = rstd * (dy - jnp.mean(dy, axis=-1, keepdims=True) - y * jnp.mean(dy * y, axis=-1, keepdims=True))
        dzv_ref[...] = (dgv * _gelu_grad(zv)).astype(BF16)

    full = lambda a: pl.BlockSpec(a.shape, lambda i: (0,) * a.ndim)
    rows = pl.BlockSpec((tr, DG), lambda i: (i, 0))
    return pl.pallas_call(
        body, name="gmlp_bwd", grid=(S // tr,),
        in_specs=[rows, pl.BlockSpec((tr, DG), lambda i: (i, 3)), pl.BlockSpec((tr, DG), lambda i: (i, 4)),
                  full(ln_g), full(ln_b), full(w_s), full(w_st), full(bs_col)],
        out_specs=[rows, rows, full(w_s), full(bs_col), full(ln_g), full(ln_b)],
        out_shape=[jax.ShapeDtypeStruct((S, DG), BF16), jax.ShapeDtypeStruct((S, DG), BF16),
                   jax.ShapeDtypeStruct(w_s.shape, F32), jax.ShapeDtypeStruct(bs_col.shape, F32),
                   jax.ShapeDtypeStruct(ln_g.shape, F32), jax.ShapeDtypeStruct(ln_b.shape, F32)],
        scratch_shapes=[pltpu.VMEM((tr, DG), F32)],
        compiler_params=_params(("arbitrary",)),
    )(dgm, zm, zm, ln_g, ln_b, w_s, w_st, bs_col)


def _all_gather(name, blk):
    R, C = blk.shape

    def body(x_ref, out_ref, send_sems, recv_sems, local_sem):
        x, y, c = _me()
        me, sibling = (x, y, c), (x, y, 1 - c)
        chips = [(1 - x, y), (x, 1 - y), (1 - x, 1 - y)]

        def slab(px, py, pc):
            return out_ref.at[4 * px + 2 * py + pc]

        def copy(k, block, to, src=None):
            return pltpu.make_async_remote_copy(
                src_ref=slab(*block) if src is None else src, dst_ref=slab(*block),
                send_sem=send_sems.at[k], recv_sem=recv_sems.at[k], device_id=to, device_id_type=MESH)

        mine = pltpu.make_async_copy(x_ref, slab(*me), local_sem)
        mine.start()
        first = [copy(0, me, sibling, src=x_ref)]
        first += [copy(1 + n, me, (*chip, c), src=x_ref) for n, chip in enumerate(chips)]
        for cp in first:
            cp.start()
        passed = [copy(4 + n, (*chip, c), sibling) for n, chip in enumerate(chips)]
        for n, chip in enumerate(chips):
            copy(1 + n, (*chip, c), me).wait_recv()
            passed[n].start()
        copy(0, sibling, me).wait_recv()
        for n, chip in enumerate(chips):
            copy(4 + n, (*chip, 1 - c), me).wait_recv()
        for cp in first + passed:
            cp.wait_send()
        mine.wait()

    return pl.pallas_call(
        body, name=name, out_shape=jax.ShapeDtypeStruct((N_DEV, R, C), blk.dtype),
        in_specs=[_ANY], out_specs=_ANY,
        scratch_shapes=[pltpu.SemaphoreType.DMA((7,)), pltpu.SemaphoreType.DMA((7,)), pltpu.SemaphoreType.DMA(())],
    )(blk)


def _row_tile(R, C, itemsize=4, target_bytes=2 * 1024 * 1024):
    tr = R
    while tr % 2 == 0 and tr * C * itemsize > target_bytes and (tr // 2) % 16 == 0:
        tr //= 2
    return tr


def _rs_add1(name, g4, recv, c_idx):
    _, _, R, C = g4.shape
    tr = _row_tile(R, C)

    def body(c_ref, g_ref, r_ref, h_ref, hb_ref):
        h = g_ref[...] + r_ref[...].astype(F32)
        h_ref[...] = h
        hb_ref[...] = h.astype(BF16)

    blk = pl.BlockSpec((None, tr, C), lambda p, i, c_ref: (p, i, 0))
    return pl.pallas_call(
        body, name=name,
        grid_spec=pltpu.PrefetchScalarGridSpec(
            num_scalar_prefetch=1, grid=(4, R // tr),
            in_specs=[pl.BlockSpec((None, None, tr, C), lambda p, i, c_ref: (p, c_ref[0], i, 0)), blk],
            out_specs=[blk, blk]),
        out_shape=[jax.ShapeDtypeStruct((4, R, C), F32), jax.ShapeDtypeStruct((4, R, C), BF16)],
        compiler_params=_params(("parallel", "parallel")),
    )(c_idx, g4, recv)


def _rs_add1_windows(name, g, recv, first_blocks):
    _, R, W = recv.shape

    def body(t_ref, g_ref, r_ref, h_ref, hb_ref):
        h = g_ref[...] + r_ref[...].astype(F32)
        h_ref[...] = h
        hb_ref[...] = h.astype(BF16)

    blk = pl.BlockSpec((None, R, LANES), lambda p, l, t_ref: (p, 0, l))
    return pl.pallas_call(
        body, name=name,
        grid_spec=pltpu.PrefetchScalarGridSpec(
            num_scalar_prefetch=1, grid=(4, W // LANES),
            in_specs=[pl.BlockSpec((R, LANES), lambda p, l, t_ref: (0, t_ref[p] + l)), blk],
            out_specs=[blk, blk]),
        out_shape=[jax.ShapeDtypeStruct((4, R, W), F32), jax.ShapeDtypeStruct((4, R, W), BF16)],
        compiler_params=_params(("parallel", "parallel")),
    )(first_blocks, g, recv)


def _add_windows(name, windows, first, second, n_blocks):
    _, R, W = windows.shape
    dev1 = jnp.asarray([d for d, _ in first], jnp.int32)
    blk1 = jnp.asarray([b for _, b in first], jnp.int32)
    dev2 = jnp.asarray([max(d, 0) for d, _ in second], jnp.int32)
    blk2 = jnp.asarray([b for _, b in second], jnp.int32)
    two = jnp.asarray([int(d >= 0) for d, _ in second], jnp.int32)

    def body(d1_ref, b1_ref, d2_ref, b2_ref, two_ref, a_ref, b_ref, out_ref):
        k = pl.program_id(0)

        @pl.when(two_ref[k] == 0)
        def _():
            out_ref[...] = a_ref[...]

        @pl.when(two_ref[k] != 0)
        def _():
            out_ref[...] = a_ref[...] + b_ref[...]

    return pl.pallas_call(
        body, name=name,
        grid_spec=pltpu.PrefetchScalarGridSpec(
            num_scalar_prefetch=5, grid=(n_blocks,),
            in_specs=[pl.BlockSpec((None, R, LANES), lambda k, d1, b1, d2, b2, t: (d1[k], 0, b1[k])),
                      pl.BlockSpec((None, R, LANES), lambda k, d1, b1, d2, b2, t: (d2[k], 0, b2[k]))],
            out_specs=pl.BlockSpec((R, LANES), lambda k, d1, b1, d2, b2, t: (0, k))),
        out_shape=jax.ShapeDtypeStruct((R, n_blocks * LANES), windows.dtype),
        compiler_params=_params(("parallel",)),
    )(dev1, blk1, dev2, blk2, two, windows, windows)


def _rs_add2(name, h, recv, p_idx):
    _, R, C = h.shape
    tr = _row_tile(R, C)

    def body(p_ref, h_ref, r_ref, out_ref):
        out_ref[...] = ((h_ref[...] + r_ref[0].astype(F32)) + r_ref[1].astype(F32)) + r_ref[2].astype(F32)

    return pl.pallas_call(
        body, name=name,
        grid_spec=pltpu.PrefetchScalarGridSpec(
            num_scalar_prefetch=1, grid=(R // tr,),
            in_specs=[pl.BlockSpec((None, tr, C), lambda i, p_ref: (p_ref[0], i, 0)),
                      pl.BlockSpec((3, tr, C), lambda i, p_ref: (0, i, 0))],
            out_specs=pl.BlockSpec((tr, C), lambda i, p_ref: (i, 0))),
        out_shape=jax.ShapeDtypeStruct((R, C), F32),
        compiler_params=_params(("parallel",)),
    )(p_idx, h, recv)


def _sum8(name, g):
    _, R, C = g.shape

    def body(g_ref, out_ref):
        acc = g_ref[0]
        for d in range(1, N_DEV):
            acc = acc + g_ref[d]
        out_ref[...] = acc

    return pl.pallas_call(body, name=name, out_shape=jax.ShapeDtypeStruct((R, C), F32),
                          compiler_params=_params())(g)


def _adamw_math(w, g, m, v):
    m = ADAM_B1 * m + (1.0 - ADAM_B1) * g
    v = ADAM_B2 * v + (1.0 - ADAM_B2) * (g * g)
    m_hat = m / (1.0 - ADAM_B1 ** ADAM_STEP)
    v_hat = v / (1.0 - ADAM_B2 ** ADAM_STEP)
    delta = -ADAM_LR * (m_hat / (jnp.sqrt(v_hat) + ADAM_EPS) + ADAM_WD * w)
    return delta, m, v


def _adamw(name, w, g, m, v):
    R, C = w.shape
    tr = _row_tile(R, C, target_bytes=1024 * 1024)
    return _row_call(name, lambda *a: (_adamw_math(*a), ()), [w, g, m, v], [], [(C, F32)] * 3, [], tr)


def _adamw_many(name, ws, gs, ms, vs):
    n = len(ws)

    def body(*refs):
        ins, outs = refs[:4 * n], refs[4 * n:]
        for k in range(n):
            res = _adamw_math(ins[k][...], ins[n + k][...], ins[2 * n + k][...], ins[3 * n + k][...])
            for t in range(3):
                outs[t * n + k][...] = res[t]

    out = pl.pallas_call(
        body, name=name, out_shape=[jax.ShapeDtypeStruct(w.shape, F32) for _ in range(3) for w in ws],
        compiler_params=_params(),
    )(*ws, *gs, *ms, *vs)
    return out[:n], out[n:2 * n], out[2 * n:]


def _pack(parts):
    flat = []
    total = 0
    for a in parts:
        n = math.prod(a.shape)
        flat.append(a.reshape(-1).astype(F32))
        if n % LANES:
            flat.append(jnp.zeros((-n % LANES,), F32))
        total += n + (-n % LANES)
    if total % (8 * LANES):
        flat.append(jnp.zeros((-total % (8 * LANES),), F32))
    return jnp.concatenate(flat).reshape(-1, LANES)


def _unpack(packed, shapes):
    out = []
    r = 0
    for shp in shapes:
        n = math.prod(shp)
        nr = -(-n // LANES)
        out.append(packed[r:r + nr].reshape(-1)[:n].reshape(shp))
        r += nr
    return out


def kernel(x, norm_mix_g, w_in, b_f, gmlp_ln_g, gmlp_ln_b, w_s, b_s, attn_out_g, gmlp_out_g, w_out, norm_ffn_g, w_ff1, w_ff2, norm_final_g, loss_target, m_norm_mix_g, m_w_in, m_b_f, m_gmlp_ln_g, m_gmlp_ln_b, m_w_s, m_b_s, m_attn_out_g, m_gmlp_out_g, m_w_out, m_norm_ffn_g, m_w_ff1, m_w_ff2, m_norm_final_g, v_norm_mix_g, v_w_in, v_b_f, v_gmlp_ln_g, v_gmlp_ln_b, v_w_s, v_b_s, v_attn_out_g, v_gmlp_out_g, v_w_out, v_norm_ffn_g, v_w_ff1, v_w_ff2, v_norm_final_g):
    S, D = x.shape[1], x.shape[2]
    H = b_f.shape[1]
    DA = H * HEAD_DIM
    DG = gmlp_ln_g.shape[1]
    DQKV = 3 * DA
    DMAIN = DQKV + 2 * DG
    DIN = DMAIN + H
    DFF = w_ff1.shape[2] * N_DEV
    w_in_cols = w_in.shape[2]
    assert DIN == w_in_cols * N_DEV and DA == DG and D == DA + DG

    T_ATT = min(T_ATT_MAX, S)
    TR = min(TR_MAX, S)

    x0 = x[0]
    tgt = loss_target[0]
    g_final = norm_final_g.reshape(1, D)

    FB = DFF // N_DEV
    x_pos, y_pos, c_pos = _me()
    me_idx = 4 * x_pos + 2 * y_pos + c_pos

    WW = -(-(w_in_cols + LANES - 1) // LANES) * LANES
    to_main = lambda col: col if col <= DQKV else max(DQKV, col - H)
    lo = [to_main(n * w_in_cols) for n in range(N_DEV)]
    hi = [to_main((n + 1) * w_in_cols) for n in range(N_DEV)]
    starts = [v // LANES * LANES for v in lo]
    gate_dev = DQKV // w_in_cols
    n_before = DQKV - gate_dev * w_in_cols
    g0 = lo[gate_dev] - starts[gate_dev]
    stash = -(-(g0 + w_in_cols - H) // LANES) * LANES
    assert all(hi[n] <= starts[n] + WW <= DMAIN for n in range(N_DEV))
    assert gate_dev * w_in_cols <= DQKV and DQKV + H <= (gate_dev + 1) * w_in_cols and stash + LANES <= WW
    shard = w_in[0].astype(BF16)

    def my_window(n):
        if n != gate_dev:
            return lambda s: jnp.pad(s, ((0, 0), (lo[n] - starts[n], WW - w_in_cols - (lo[n] - starts[n]))))
        return lambda s: jnp.concatenate([
            jnp.zeros((D, g0), BF16), s[:, :n_before], s[:, n_before + H:],
            jnp.zeros((D, stash - g0 - (w_in_cols - H)), BF16), s[:, n_before:n_before + H],
            jnp.zeros((D, WW - stash - H), BF16)], axis=1)
    windows = _all_gather("ag_w_in", lax.switch(me_idx, [my_window(n) for n in range(N_DEV)], shard))
    first, second = [], []
    for blk in range(DMAIN // LANES):
        c0 = blk * LANES
        owners = [(n, (c0 - starts[n]) // LANES) for n in range(N_DEV) if lo[n] < c0 + LANES and hi[n] > c0]
        assert 1 <= len(owners) <= 2
        first.append(owners[0])
        second.append(owners[1] if len(owners) == 2 else (-1, 0))
    w_main = _add_windows("w_in_windows", windows, first, second, DMAIN // LANES)
    w_f = windows[gate_dev, :, stash:stash + LANES]
    c_idx = jnp.reshape(c_pos, (1,)).astype(jnp.int32)
    p_idx = jnp.reshape(2 * x_pos + y_pos, (1,)).astype(jnp.int32)

    (h,), _ = _row_call("rms_mix", lambda xb, g: ((_rms_fwd(xb, g),), ()), [x0], [norm_mix_g], [(D, BF16)], [], TR)
    (zm,), ((w_out_part,),) = _mm_nn("in_proj", h, w_main, [BF16], 2048, 1024, 2048,
                                     jobs=[_job_gather_chips(w_out[0].astype(BF16))])
    (zf,) = _mm_nn("in_proj_f", h, w_f, [F32], 1024, LANES, 2048)
    bf_pad = jnp.pad(b_f, ((0, 0), (0, LANES - H)))
    f_row = _fgate_fwd(zf, bf_pad)
    NB = S // T_ATT
    f_col3 = f_row.reshape(H, S, 1)
    f_row3 = f_row.reshape(H, NB, 1, T_ATT)
    (attn, lse_col3), ((w_out_all,), (w_ff1_part,)) = _attn2_fwd(
        zm, f_col3, f_row3, T_ATT, jobs=[_job_gather_sibling(w_out_part), _job_gather_chips(w_ff1[0].astype(BF16))])
    w_out_full = w_out_all.reshape(D, D)
    bs_col = b_s[0].reshape(H, CHUNK, 1)
    gm = _gmlp_fwd(zm, gmlp_ln_g, gmlp_ln_b, w_s[0], bs_col, TR)

    def merge_fn(a, g, ga, gg):
        return (jnp.concatenate([_rms_fwd(a, ga), _rms_fwd(g, gg)], axis=1),), ()
    (merged,), _ = _row_call("rms_merge", merge_fn, [attn, gm], [attn_out_g, gmlp_out_g], [(D, BF16)], [], TR)

    w_ff2_b = w_ff2[0].astype(BF16)
    (x1,), ((w_ff1_all,), (w_ff2_q1,)) = _mm_nn(
        "out_proj", merged, w_out_full, [F32], 1024, 1024, 2048, extras=[x0], epilogue=lambda acc, r: (acc + r,),
        jobs=[_job_gather_sibling(w_ff1_part), _job_gather_chips(w_ff2_b, part=(0, 1, 4))])
    (h2,), _ = _row_call("rms_ffn", lambda xb, g: ((_rms_fwd(xb, g),), ()), [x1], [norm_ffn_g], [(D, BF16)], [], TR)

    tm, tn, tk = min(1024, S), min(1024, FB), min(2048, D)
    tm1 = min(2048, S)
    o_spec = pl.BlockSpec((tm1, tn), lambda i, j, k: (i, j))

    def relu_sq(acc):
        a = jnp.maximum(acc, 0.0)
        return a, a * a
    nj = FB // tn
    ff2_rest = [_job_gather_chips(w_ff2_b, part=(1, 4, 4), into=w_ff2_q1)]
    (a_act, a_sq), ((w_ff2_q2,),) = _mm(
        "ff1", (S // tm1, DFF // tn, D // tk), h2, pl.BlockSpec((tm1, tk), lambda i, j, k: (i, k)),
        w_ff1_all, pl.BlockSpec((None, tk, tn), lambda i, j, k: (j // nj, k, j % nj)), NN, (tm1, tn),
        [jax.ShapeDtypeStruct((S, DFF), BF16)] * 2, [o_spec] * 2, epilogue=relu_sq, jobs=ff2_rest)
    (w_ff2_all,) = _run_jobs("ag_w_ff2_sibling", [_job_gather_sibling(w_ff2_q2)])[0]
    w_ff2_full = w_ff2_all.reshape(DFF, D)
    (x2,) = _mm_nn("ff2", a_sq, w_ff2_full, [F32], 1024, 1024, 2048, extras=[x1], epilogue=lambda acc, r: (acc + r,))

    def head_fn(xb, t, g):
        rstd = lax.rsqrt(jnp.mean(xb * xb, axis=-1, keepdims=True) + EPS)
        xhat = xb * rstd
        err = xhat * g - t
        loss = 0.5 * jnp.sum(jnp.mean(err * err, axis=-1, keepdims=True), axis=0, keepdims=True)
        dy = err * (1.0 / D)
        dg = jnp.sum(dy * xhat, axis=0, keepdims=True)
        dxhat = dy * g
        dx = rstd * (dxhat - xhat * jnp.mean(dxhat * xhat, axis=-1, keepdims=True))
        return (dx, dx), (dg, jnp.broadcast_to(loss, (1, LANES)))
    (dx2, dx2_b), (dg_final, loss_part) = _row_call(
        "loss_head", head_fn, [x2, tgt], [g_final], [(D, F32), (D, BF16)], [D, LANES], TR)

    (da,) = _mm_nt("ff2_dx", dx2_b, w_ff2_full, [BF16], 2048, 1024, 2048, extras=[a_act],
                   epilogue=lambda acc, a: (2.0 * a.astype(F32) * acc,))
    dw_ff2, dw_ff2_b = _mm_tn("ff2_dw", a_sq, dx2_b, [F32, BF16], 1024, 2048, 1024)
    tm2, tk2 = min(2048, D), min(1024, S)
    dw1_spec = pl.BlockSpec((None, tm2, FB), lambda i, j, k: (j, i, 0))
    (dw_ff1, dw_ff1_b), ((r1_ff2,),) = _mm(
        "ff1_dw", (D // tm2, DFF // FB, S // tk2), h2, pl.BlockSpec((tk2, tm2), lambda i, j, k: (k, i)),
        da, pl.BlockSpec((tk2, FB), lambda i, j, k: (k, j)), TN, (tm2, FB),
        [jax.ShapeDtypeStruct((N_DEV, D, FB), F32), jax.ShapeDtypeStruct((N_DEV, D, FB), BF16)], [dw1_spec] * 2,
        epilogue=lambda acc: (acc, acc), jobs=[_job_scatter_sibling(dw_ff2_b.reshape(4, 2, FB, D))])
    h_ff2, hb_ff2 = _rs_add1("rs_add1_w_ff2", dw_ff2.reshape(4, 2, FB, D), r1_ff2, c_idx)
    tkb = min(1024, FB)
    nkb = FB // tkb
    tnb = min(1024, D)
    (dh2,), ((r2_ff2,), (r1_ff1,)) = _mm(
        "ff1_dx", (S // tm, D // tnb, DFF // tkb), da, pl.BlockSpec((tm, tkb), lambda i, j, k: (i, k)),
        w_ff1_all, pl.BlockSpec((None, tnb, tkb), lambda i, j, k: (k // nkb, j, k % nkb)), NT, (tm, tnb),
        [jax.ShapeDtypeStruct((S, D), F32)], [pl.BlockSpec((tm, tnb), lambda i, j, k: (i, j))],
        jobs=[_job_scatter_chips(hb_ff2), _job_scatter_sibling(dw_ff1_b.reshape(4, 2, D, FB))])
    g_w_ff2 = _rs_add2("rs_add2_w_ff2", h_ff2, r2_ff2, p_idx)
    h_ff1, hb_ff1 = _rs_add1("rs_add1_w_ff1", dw_ff1.reshape(4, 2, D, FB), r1_ff1, c_idx)

    def ffn_bwd_fn(dh, xb, dres, g):
        dx, dg = _rms_bwd(dh, xb, g)
        dx = dx + dres
        return (dx, dx), (dg,)
    (dx1, dx1_b), (dg_ffn,) = _row_call("rms_ffn_bwd", ffn_bwd_fn, [dh2, x1, dx2], [norm_ffn_g],
                                        [(D, F32), (D, BF16)], [D], TR)

    (dmerged,) = _mm_nt("out_proj_dx", dx1_b, w_out_full, [F32], 1024, 1024, 2048)
    dw_out, dw_out_b = _mm_tn("out_proj_dw", merged, dx1_b, [F32, BF16], 2048, 1024, 1024)

    def merge_bwd_fn(dm, a, g, ga, gg):
        da_, dga = _rms_bwd(dm[:, :DA], a, ga)
        dg_, dgg = _rms_bwd(dm[:, DA:], g, gg)
        return (da_, dg_), (dga, dgg)
    (dattn, dgm), (dg_attn, dg_gmlp) = _row_call(
        "rms_merge_bwd", merge_bwd_fn, [dmerged, attn, gm], [attn_out_g, gmlp_out_g], [(DA, F32), (DG, F32)], [DA, DG], TR)

    w_st = jnp.swapaxes(w_s[0], 1, 2)
    dzu, dzv, dw_s, dbs_col, dln_g, dln_b = _gmlp_bwd(dgm, zm, gmlp_ln_g, gmlp_ln_b, w_s[0], w_st, bs_col, TR)

    delta_row = _attn_delta(dattn, attn, TR)
    lse_row3 = lse_col3.reshape(H, NB, 1, T_ATT)
    (dq, ds_rowsum), ((r2_ff1,), (r1_out,)) = _attn2_bwd_dq(
        zm, dattn, f_col3, f_row3, lse_col3, delta_row.reshape(H, S, 1), T_ATT,
        jobs=[_job_scatter_chips(hb_ff1), _job_scatter_sibling(dw_out_b.reshape(4, 2, D // N_DEV, D))])
    g_w_ff1 = _rs_add2("rs_add2_w_ff1", h_ff1, r2_ff1, p_idx)
    h_out, hb_out = _rs_add1("rs_add1_w_out", dw_out.reshape(4, 2, D // N_DEV, D), r1_out, c_idx)
    (dk, dv, df_col3), ((r2_out,),) = _attn2_bwd_dkv(
        zm, dattn, f_col3, f_row3, lse_row3, delta_row.reshape(H, NB, 1, T_ATT),
        ds_rowsum.reshape(H, NB, 1, T_ATT), T_ATT,
        jobs=[_job_scatter_chips(hb_out)])
    g_w_out = _rs_add2("rs_add2_w_out", h_out, r2_out, p_idx)
    dzf, dbf = _fgate_bwd(df_col3.reshape(H, S), zf, bf_pad)

    dz_main = jnp.concatenate([dq, dk, dv, dzu, dzv], axis=1)
    dw_main, dw_main_b = _mm_tn("in_proj_dw", h, dz_main, [F32, BF16], 2048, 1024, 1024)
    (dw_f,) = _mm_tn("in_proj_f_dw", h, dzf, [F32], 2048, LANES, 1024)
    (dh_f,), ((r1_in,),) = _mm_nt("in_proj_f_dx", dzf, w_f, [F32], 1024, 1024, LANES,
                                  jobs=[_job_scatter_sibling_windows(dw_main_b, starts, WW)])
    first_blocks = jnp.stack([jnp.where(c_pos == 0, starts[2 * p], starts[2 * p + 1]) // LANES
                              for p in range(4)]).astype(jnp.int32)
    h_in, hb_in = _rs_add1_windows("rs_add1_w_in", dw_main, r1_in, first_blocks)
    (dh,), ((r2_in,),) = _mm_nt("in_proj_dx", dz_main, w_main, [F32], 1024, 1024, 1024, extras=[dh_f],
                                epilogue=lambda acc, r: (acc + r,), jobs=[_job_scatter_chips(hb_in)])
    g_window = _rs_add2("rs_add2_w_in", h_in, r2_in, p_idx)

    def mix_bwd_fn(dhb, xb, dres, g):
        dx, dg = _rms_bwd(dhb, xb, g)
        return (dx + dres,), (dg,)
    (grad_x,), (dg_mix,) = _row_call("rms_mix_bwd", mix_bwd_fn, [dh, x0, dx1], [norm_mix_g], [(D, F32)], [D], TR)

    small_shapes = [norm_mix_g.shape, b_f.shape, gmlp_ln_g.shape, gmlp_ln_b.shape, w_s.shape, b_s.shape,
                    attn_out_g.shape, gmlp_out_g.shape, norm_ffn_g.shape, norm_final_g.shape]
    small_parts = [dg_mix, dbf[:, :H], dln_g, dln_b, dw_s, dbs_col, dg_attn, dg_gmlp, dg_ffn, dg_final]
    g_small = _sum8("small_sum", _all_gather("ag_small", _pack(small_parts + [dw_f[:, :H]])))
    *gs, g_gate = _unpack(g_small, small_shapes + [(D, H)])
    two_d = lambda a: a.reshape(1, -1) if a.ndim == 1 else a
    ds, nms, nvs = _adamw_many(
        "adamw_small",
        [two_d(a) for a in (norm_mix_g, b_f, gmlp_ln_g, gmlp_ln_b, w_s, b_s, attn_out_g, gmlp_out_g, norm_ffn_g,
                            norm_final_g)],
        [two_d(a) for a in gs],
        [two_d(a) for a in (m_norm_mix_g, m_b_f, m_gmlp_ln_g, m_gmlp_ln_b, m_w_s, m_b_s, m_attn_out_g, m_gmlp_out_g,
                            m_norm_ffn_g, m_norm_final_g)],
        [two_d(a) for a in (v_norm_mix_g, v_b_f, v_gmlp_ln_g, v_gmlp_ln_b, v_w_s, v_b_s, v_attn_out_g, v_gmlp_out_g,
                            v_norm_ffn_g, v_norm_final_g)])
    ds, nms, nvs = [[a.reshape(s) for a, s in zip(lst, small_shapes)] for lst in (ds, nms, nvs)]

    def my_columns(n):
        if n != gate_dev:
            return lambda win, gate: win[:, lo[n] - starts[n]:lo[n] - starts[n] + w_in_cols]
        return lambda win, gate: jnp.concatenate([win[:, g0:g0 + n_before], gate,
                                                  win[:, g0 + n_before:g0 + w_in_cols - H]], axis=1)
    g_w_in = lax.switch(me_idx, [my_columns(n) for n in range(N_DEV)], g_window, g_gate)

    big = {}
    for nm, w, g, m, v in (("w_in", w_in, g_w_in, m_w_in, v_w_in), ("w_out", w_out, g_w_out, m_w_out, v_w_out),
                           ("w_ff1", w_ff1, g_w_ff1, m_w_ff1, v_w_ff1), ("w_ff2", w_ff2, g_w_ff2, m_w_ff2, v_w_ff2)):
        (d_, m_, v_), _ = _adamw("adamw_" + nm, w[0], g, m[0], v[0])
        big[nm] = (g[None], d_[None], m_[None], v_[None])

    loss = lax.psum(loss_part[0, 0], ("x", "y", "c"))

    def leaves(n):
        sm = (gs, ds, nms, nvs)[n]
        return [sm[0], big["w_in"][n], sm[1], sm[2], sm[3], sm[4], sm[5], sm[6], sm[7], big["w_out"][n], sm[8],
                big["w_ff1"][n], big["w_ff2"][n], sm[9]]

    return (loss, grad_x[None], *leaves(0), *leaves(1), *leaves(2), *leaves(3))
```

```python
import functools
import math

import jax
import jax.numpy as jnp
from jax import lax
from jax.experimental import pallas as pl
from jax.experimental.pallas import tpu as pltpu

F32 = jnp.float32
BF16 = jnp.bfloat16
MESH = pl.DeviceIdType.MESH

HEAD_DIM = 128
CHUNK = 128
EPS = 1e-6
LANES = 128
N_DEV = 8

ADAM_LR = 0.001
ADAM_B1 = 0.9
ADAM_B2 = 0.999
ADAM_EPS = 1e-08
ADAM_WD = 0.01
ADAM_STEP = 10

VMEM_LIMIT_BYTES = 56 * 1024 * 1024
T_ATT_MAX = 1024
TR_MAX = 256

NN = ((1,), (0,))
NT = ((1,), (1,))
TN = ((0,), (0,))


def _params(sem=None):
    return pltpu.CompilerParams(dimension_semantics=sem, vmem_limit_bytes=VMEM_LIMIT_BYTES)


def _dot(a, b, contract=NN):
    return lax.dot_general(a, b, (contract, ((), ())), preferred_element_type=F32)


def _dot3(x, t):
    x1 = x.astype(BF16)
    r1 = x - x1.astype(F32)
    x2 = r1.astype(BF16)
    x3 = (r1 - x2.astype(F32)).astype(BF16)
    return _dot(x1, t) + _dot(x2, t) + _dot(x3, t)


def _iota2(shape, dim):
    return lax.broadcasted_iota(jnp.int32, shape, dim)


def _row_call(name, fn, row_ins, bcast_ins, row_outs, acc_outs, tr):
    S = row_ins[0].shape[0]
    assert S % tr == 0
    n_ri, n_bi, n_ro, n_ao = len(row_ins), len(bcast_ins), len(row_outs), len(acc_outs)

    def body(*refs):
        ins = [r[...] for r in refs[:n_ri + n_bi]]
        ro_refs = refs[n_ri + n_bi:n_ri + n_bi + n_ro]
        ao_refs = refs[n_ri + n_bi + n_ro:]
        ro, ao = fn(*ins)
        for r, v in zip(ro_refs, ro):
            r[...] = v.astype(r.dtype)
        if n_ao:
            @pl.when(pl.program_id(0) == 0)
            def _():
                for r in ao_refs:
                    r[...] = jnp.zeros_like(r)
            for r, v in zip(ao_refs, ao):
                r[...] += v

    in_specs = [pl.BlockSpec((tr, a.shape[1]), lambda i: (i, 0)) for a in row_ins]
    in_specs += [pl.BlockSpec(a.shape, lambda i: (0, 0)) for a in bcast_ins]
    out_specs = [pl.BlockSpec((tr, d), lambda i: (i, 0)) for d, _ in row_outs]
    out_specs += [pl.BlockSpec((1, d), lambda i: (0, 0)) for d in acc_outs]
    out_shape = [jax.ShapeDtypeStruct((S, d), dt) for d, dt in row_outs]
    out_shape += [jax.ShapeDtypeStruct((1, d), F32) for d in acc_outs]
    outs = pl.pallas_call(
        body, name=name, grid=(S // tr,), in_specs=in_specs, out_specs=out_specs, out_shape=out_shape,
        compiler_params=_params(("arbitrary",) if n_ao else ("parallel",)),
    )(*row_ins, *bcast_ins)
    return outs[:n_ro], outs[n_ro:]


def _rms_fwd(x, g):
    rstd = lax.rsqrt(jnp.mean(x * x, axis=-1, keepdims=True) + EPS)
    return x * rstd * g


def _rms_bwd(dy, x, g):
    rstd = lax.rsqrt(jnp.mean(x * x, axis=-1, keepdims=True) + EPS)
    xhat = x * rstd
    dg = jnp.sum(dy * xhat, axis=0, keepdims=True)
    dxhat = dy * g
    dx = rstd * (dxhat - xhat * jnp.mean(dxhat * xhat, axis=-1, keepdims=True))
    return dx, dg


_GELU_C = math.sqrt(2.0 / math.pi)


def _gelu(x):
    return 0.5 * x * (1.0 + jnp.tanh(_GELU_C * (x + 0.044715 * (x * x * x))))


def _gelu_grad(x):
    t = jnp.tanh(_GELU_C * (x + 0.044715 * (x * x * x)))
    return 0.5 * (1.0 + t) + 0.5 * x * (1.0 - t * t) * (_GELU_C * (1.0 + 3.0 * 0.044715 * (x * x)))


def _me():
    return lax.axis_index("x"), lax.axis_index("y"), lax.axis_index("c")


def _other_chips(x, y):
    return [(1 - x, y), (x, 1 - y), (1 - x, 1 - y)]


_ANY = pl.BlockSpec(memory_space=pl.ANY)


class _Job:
    def __init__(self, ins, outs, n_sems, make, aliases=None):
        self.ins, self.outs, self.n_sems, self.make, self.aliases = ins, outs, n_sems, make, aliases or {}


def _job_gather_chips(blk, part=(0, 1, 1), into=None):
    R, C = blk.shape
    nr = R // part[2]
    rows = pl.ds(part[0] * nr, (part[1] - part[0]) * nr)

    def make(ins, outs, send_sems, recv_sems, base):
        x_ref, (out_ref,) = ins[0], outs
        x, y, c = _me()
        mine = 4 * x + 2 * y + c
        targets = [(x, y, 1 - c)] + [(cx, cy, c) for cx, cy in _other_chips(x, y)]

        def copy(k, slab, to):
            return pltpu.make_async_remote_copy(
                src_ref=x_ref.at[rows, :], dst_ref=out_ref.at[slab, rows, :], send_sem=send_sems.at[base + k],
                recv_sem=recv_sems.at[base + k], device_id=to, device_id_type=MESH)

        starts = [copy(k, mine, to) for k, to in enumerate(targets)]
        arrivals = [copy(k, 4 * tx + 2 * ty + tc, (tx, ty, tc)) for k, (tx, ty, tc) in enumerate(targets)]
        local = [pltpu.make_async_copy(x_ref.at[rows, :], out_ref.at[mine, rows, :], send_sems.at[base + 4])]
        return starts, arrivals, local

    out = jax.ShapeDtypeStruct((N_DEV, R, C), blk.dtype)
    if into is None:
        return _Job([blk], [out], 5, make)
    return _Job([blk, into], [out], 5, make, aliases={1: 0})


def _job_gather_sibling(part):
    def make(ins, outs, send_sems, recv_sems, base):
        (out_ref,) = outs
        x, y, c = _me()

        def copy(k, slab):
            return pltpu.make_async_remote_copy(
                src_ref=out_ref.at[slab], dst_ref=out_ref.at[slab], send_sem=send_sems.at[base + k],
                recv_sem=recv_sems.at[base + k], device_id=(x, y, 1 - c), device_id_type=MESH)

        chips = _other_chips(x, y)
        starts = [copy(k, 4 * cx + 2 * cy + c) for k, (cx, cy) in enumerate(chips)]
        arrivals = [copy(k, 4 * cx + 2 * cy + (1 - c)) for k, (cx, cy) in enumerate(chips)]
        return starts, arrivals, []

    return _Job([part], [jax.ShapeDtypeStruct(part.shape, part.dtype)], 3, make, aliases={0: 0})


def _job_scatter_sibling(gb):
    _, _, R, C = gb.shape

    def make(ins, outs, send_sems, recv_sems, base):
        (g_ref,), (recv_ref,) = ins, outs
        x, y, c = _me()
        copies = [pltpu.make_async_remote_copy(
            src_ref=g_ref.at[p, 1 - c], dst_ref=recv_ref.at[p], send_sem=send_sems.at[base + p],
            recv_sem=recv_sems.at[base + p], device_id=(x, y, 1 - c), device_id_type=MESH) for p in range(4)]
        return copies, copies, []

    return _Job([gb], [jax.ShapeDtypeStruct((4, R, C), gb.dtype)], 4, make)


def _job_scatter_sibling_windows(gb, starts, width):
    R, _ = gb.shape

    def make(ins, outs, send_sems, recv_sems, base):
        (g_ref,), (recv_ref,) = ins, outs
        x, y, c = _me()
        copies = []
        for p in range(4):
            start = pl.multiple_of(jnp.where(c == 0, starts[2 * p + 1], starts[2 * p]), LANES)
            copies.append(pltpu.make_async_remote_copy(
                src_ref=g_ref.at[:, pl.ds(start, width)], dst_ref=recv_ref.at[p], send_sem=send_sems.at[base + p],
                recv_sem=recv_sems.at[base + p], device_id=(x, y, 1 - c), device_id_type=MESH))
        return copies, copies, []

    return _Job([gb], [jax.ShapeDtypeStruct((4, R, width), gb.dtype)], 4, make)


def _job_scatter_chips(hb):
    _, R, C = hb.shape

    def make(ins, outs, send_sems, recv_sems, base):
        (h_ref,), (recv_ref,) = ins, outs
        x, y, c = _me()
        copies = [pltpu.make_async_remote_copy(
            src_ref=h_ref.at[2 * cx + cy], dst_ref=recv_ref.at[n], send_sem=send_sems.at[base + n],
            recv_sem=recv_sems.at[base + n], device_id=(cx, cy, c), device_id_type=MESH)
            for n, (cx, cy) in enumerate(_other_chips(x, y))]
        return copies, copies, []

    return _Job([hb], [jax.ShapeDtypeStruct((3, R, C), hb.dtype)], 3, make)


def _carry_call(body, *, name, grid, in_specs, out_specs, out_shape, scratch_shapes, semantics, args, jobs=()):
    jobs = list(jobs)
    n_in, n_out, n_scr = len(in_specs), len(out_specs), len(scratch_shapes)
    j_ins = [a for j in jobs for a in j.ins]
    j_outs = [o for j in jobs for o in j.outs]
    n_sems = sum(j.n_sems for j in jobs)
    aliases = {}
    i0, o0 = n_in, n_out
    for j in jobs:
        for a, b in j.aliases.items():
            aliases[i0 + a] = o0 + b
        i0 += len(j.ins)
        o0 += len(j.outs)

    def full_body(*refs):
        ins = refs[:n_in]
        jin = refs[n_in:n_in + len(j_ins)]
        outs = refs[n_in + len(j_ins):n_in + len(j_ins) + n_out]
        jout = refs[n_in + len(j_ins) + n_out:n_in + len(j_ins) + n_out + len(j_outs)]
        scr = refs[n_in + len(j_ins) + n_out + len(j_outs):]
        if jobs:
            send_sems, recv_sems = scr[n_scr], scr[n_scr + 1]
            starts, arrivals, local = [], [], []
            base = i0 = o0 = 0
            for j in jobs:
                s, a, l = j.make(jin[i0:i0 + len(j.ins)], jout[o0:o0 + len(j.outs)], send_sems, recv_sems, base)
                starts += s
                arrivals += a
                local += l
                base += j.n_sems
                i0 += len(j.ins)
                o0 += len(j.outs)
            pids = [pl.program_id(d) for d in range(len(grid))]
            first = functools.reduce(jnp.logical_and, [p == 0 for p in pids])
            last = functools.reduce(jnp.logical_and, [p == n - 1 for p, n in zip(pids, grid)])

            @pl.when(first)
            def _():
                for cp in local + starts:
                    cp.start()

        body(*ins, *outs, *scr[:n_scr])

        if jobs:
            @pl.when(last)
            def _():
                for cp in arrivals:
                    cp.wait_recv()
                for cp in starts:
                    cp.wait_send()
                for cp in local:
                    cp.wait()

    sems = [pltpu.SemaphoreType.DMA((n_sems,)), pltpu.SemaphoreType.DMA((n_sems,))] if jobs else []
    res = pl.pallas_call(
        full_body, name=name, grid=grid,
        in_specs=list(in_specs) + [_ANY] * len(j_ins),
        out_specs=list(out_specs) + [_ANY] * len(j_outs),
        out_shape=list(out_shape) + j_outs,
        scratch_shapes=list(scratch_shapes) + sems,
        input_output_aliases=aliases,
        compiler_params=_params(("arbitrary",) * len(grid) if jobs else semantics),
    )(*args, *j_ins)
    body_res, job_res = res[:n_out], res[n_out:]
    per_job = []
    for j in jobs:
        per_job.append(job_res[:len(j.outs)])
        job_res = job_res[len(j.outs):]
    return body_res, per_job


def _run_jobs(name, jobs):
    def body(done_ref):
        done_ref[...] = jnp.zeros_like(done_ref)

    return _carry_call(body, name=name, grid=(1,), in_specs=[], out_specs=[pl.BlockSpec((8, LANES), lambda i: (0, 0))],
                       out_shape=[jax.ShapeDtypeStruct((8, LANES), F32)], scratch_shapes=[], semantics=("arbitrary",),
                       args=[], jobs=jobs)[1]


def _mm(name, grid, a, a_spec, b, b_spec, contract, acc_shape, out_shape, out_specs, extras=(), epilogue=None, jobs=()):
    nk = grid[2]
    n_e = len(extras)
    n_o = len(out_shape)
    if epilogue is None:
        epilogue = lambda acc: (acc,)

    def body(a_ref, b_ref, *rest):
        e_refs = rest[:n_e]
        o_refs = rest[n_e:n_e + n_o]

        def finish(total):
            res = epilogue(total, *[r[...] for r in e_refs])
            for o, r in zip(o_refs, res):
                o[...] = r.astype(o.dtype)

        if nk == 1:
            finish(_dot(a_ref[...], b_ref[...], contract))
            return
        acc = rest[n_e + n_o]
        k = pl.program_id(2)

        @pl.when(k == 0)
        def _():
            acc[...] = _dot(a_ref[...], b_ref[...], contract)

        @pl.when(jnp.logical_and(k > 0, k < nk - 1))
        def _():
            acc[...] += _dot(a_ref[...], b_ref[...], contract)

        @pl.when(k == nk - 1)
        def _():
            finish(acc[...] + _dot(a_ref[...], b_ref[...], contract))

    outs, job_res = _carry_call(
        body, name=name, grid=grid, in_specs=[a_spec, b_spec] + [s for _, s in extras],
        out_specs=list(out_specs), out_shape=list(out_shape),
        scratch_shapes=[pltpu.VMEM(acc_shape, F32)] if nk > 1 else [],
        semantics=("parallel", "parallel", "arbitrary"), args=[a, b] + [e for e, _ in extras], jobs=jobs)
    return (outs, job_res) if jobs else outs


def _mm_rows(name, grid, a, a_spec, b, b_spec, contract, tm, n, row_extras, bcast, row_outs, acc_outs, epilogue, jobs=()):
    nk = grid[1]
    M = grid[0] * tm
    n_x, n_b, n_ro, n_ao = len(row_extras), len(bcast), len(row_outs), len(acc_outs)

    def body(a_ref, b_ref, *rest):
        x_refs = rest[:n_x + n_b]
        ro_refs = rest[n_x + n_b:n_x + n_b + n_ro]
        ao_refs = rest[n_x + n_b + n_ro:n_x + n_b + n_ro + n_ao]
        i = pl.program_id(0)

        def finish(total):
            ro, ao = epilogue(total, *[r[...] for r in x_refs])
            for r, v in zip(ro_refs, ro):
                r[...] = v.astype(r.dtype)
            if n_ao:
                @pl.when(i == 0)
                def _():
                    for r, v in zip(ao_refs, ao):
                        r[...] = v

                @pl.when(i > 0)
                def _():
                    for r, v in zip(ao_refs, ao):
                        r[...] += v

        if nk == 1:
            finish(_dot(a_ref[...], b_ref[...], contract))
            return
        acc = rest[n_x + n_b + n_ro + n_ao]
        k = pl.program_id(1)

        @pl.when(k == 0)
        def _():
            acc[...] = _dot(a_ref[...], b_ref[...], contract)

        @pl.when(jnp.logical_and(k > 0, k < nk - 1))
        def _():
            acc[...] += _dot(a_ref[...], b_ref[...], contract)

        @pl.when(k == nk - 1)
        def _():
            finish(acc[...] + _dot(a_ref[...], b_ref[...], contract))

    in_specs = [a_spec, b_spec] + [pl.BlockSpec((tm, x.shape[1]), lambda i, k: (i, 0)) for x in row_extras]
    in_specs += [pl.BlockSpec(x.shape, lambda i, k: (0,) * x.ndim) for x in bcast]
    out_specs = [pl.BlockSpec((tm, w), lambda i, k: (i, 0)) for w, _ in row_outs]
    out_specs += [pl.BlockSpec((1, w), lambda i, k: (0, 0)) for w in acc_outs]
    out_shape = [jax.ShapeDtypeStruct((M, w), dt) for w, dt in row_outs]
    out_shape += [jax.ShapeDtypeStruct((1, w), F32) for w in acc_outs]
    outs, job_res = _carry_call(
        body, name=name, grid=grid, in_specs=in_specs, out_specs=out_specs, out_shape=out_shape,
        scratch_shapes=[pltpu.VMEM((tm, n), F32)] if nk > 1 else [],
        semantics=("arbitrary", "arbitrary"), args=[a, b] + list(row_extras) + list(bcast), jobs=jobs)
    res = (outs[:n_ro], outs[n_ro:])
    return (res, job_res) if jobs else res


def _mm_nn(name, a, b, out_dtypes, tm, tn, tk, extras=(), epilogue=None, jobs=()):
    M, K = a.shape
    N = b.shape[1]
    tm, tn, tk = min(tm, M), min(tn, N), min(tk, K)
    o_spec = pl.BlockSpec((tm, tn), lambda i, j, k: (i, j))
    return _mm(name, (M // tm, N // tn, K // tk),
               a, pl.BlockSpec((tm, tk), lambda i, j, k: (i, k)),
               b, pl.BlockSpec((tk, tn), lambda i, j, k: (k, j)), NN, (tm, tn),
               [jax.ShapeDtypeStruct((M, N), dt) for dt in out_dtypes], [o_spec] * len(out_dtypes),
               [(e, o_spec) for e in extras], epilogue, jobs)


def _mm_nt(name, a, b, out_dtypes, tm, tn, tk, extras=(), epilogue=None, jobs=()):
    M, K = a.shape
    N = b.shape[0]
    tm, tn, tk = min(tm, M), min(tn, N), min(tk, K)
    o_spec = pl.BlockSpec((tm, tn), lambda i, j, k: (i, j))
    return _mm(name, (M // tm, N // tn, K // tk),
               a, pl.BlockSpec((tm, tk), lambda i, j, k: (i, k)),
               b, pl.BlockSpec((tn, tk), lambda i, j, k: (j, k)), NT, (tm, tn),
               [jax.ShapeDtypeStruct((M, N), dt) for dt in out_dtypes], [o_spec] * len(out_dtypes),
               [(e, o_spec) for e in extras], epilogue, jobs)


def _mm_tn(name, a, b, out_dtypes, tm, tn, tk, jobs=()):
    K, M = a.shape
    N = b.shape[1]
    tm, tn, tk = min(tm, M), min(tn, N), min(tk, K)
    o_spec = pl.BlockSpec((tm, tn), lambda i, j, k: (i, j))
    return _mm(name, (M // tm, N // tn, K // tk),
               a, pl.BlockSpec((tk, tm), lambda i, j, k: (k, i)),
               b, pl.BlockSpec((tk, tn), lambda i, j, k: (k, j)), TN, (tm, tn),
               [jax.ShapeDtypeStruct((M, N), dt) for dt in out_dtypes], [o_spec] * len(out_dtypes),
               epilogue=lambda acc: (acc,) * len(out_dtypes), jobs=jobs)


def _fgate_fwd(zf, bf):
    S = zf.shape[0]
    nc = S // CHUNK

    def body(zf_ref, bf_ref, f_ref):
        upper = (_iota2((CHUNK, CHUNK), 0) <= _iota2((CHUNK, CHUNK), 1)).astype(BF16)
        carry = jnp.zeros((8, 1), F32)
        for c in range(nc):
            t = zf_ref[c * CHUNK:(c + 1) * CHUNK, :] + bf_ref[...]
            lf = jnp.minimum(t, 0.0) - jnp.log(1.0 + jnp.exp(-jnp.abs(t)))
            lf_rows = lf.T[0:8, :]
            f_ref[:, c * CHUNK:(c + 1) * CHUNK] = (_dot3(lf_rows, upper) + carry) * LOG2E
            carry = carry + jnp.sum(lf_rows, axis=-1, keepdims=True)

    return pl.pallas_call(
        body, name="fgate_fwd", out_shape=jax.ShapeDtypeStruct((8, S), F32),
        compiler_params=_params(),
    )(zf, bf)


def _fgate_bwd(df, zf, bf):
    S = zf.shape[0]
    nc = S // CHUNK

    def body(df_ref, zf_ref, bf_ref, dzf_ref, dbf_ref):
        lower = (_iota2((CHUNK, CHUNK), 0) >= _iota2((CHUNK, CHUNK), 1)).astype(BF16)
        carry = jnp.zeros((8, 1), F32)
        dbf = jnp.zeros((1, LANES), F32)
        for c in reversed(range(nc)):
            sl = slice(c * CHUNK, (c + 1) * CHUNK)
            df = df_ref[:, sl]
            r = _dot3(df, lower) + carry
            carry = carry + jnp.sum(df, axis=-1, keepdims=True)
            r_cols = jnp.concatenate([r, jnp.zeros((CHUNK - 8, CHUNK), F32)], axis=0).T
            t = zf_ref[sl, :] + bf_ref[...]
            dz = r_cols * (1.0 / (1.0 + jnp.exp(t)))
            dzf_ref[sl, :] = dz.astype(BF16)
            dbf = dbf + jnp.sum(dz, axis=0, keepdims=True)
        dbf_ref[...] = dbf

    return pl.pallas_call(
        body, name="fgate_bwd",
        out_shape=[jax.ShapeDtypeStruct((S, LANES), BF16), jax.ShapeDtypeStruct((1, LANES), F32)],
        compiler_params=_params(),
    )(df, zf, bf)


_NEG = -1e30
LOG2E = 1.4426950408889634
N_SPLIT = 8
N_SPLIT_DIAG = 2
DIAG_STEP = 1024


def _attn_consts(T):
    rows, cols = _iota2((T, T), 0), _iota2((T, T), 1)
    return cols <= rows, rows <= cols


def _attn2_fwd(zm, f2col, f2row, T, jobs=()):
    S = zm.shape[0]
    H = f2col.shape[0]
    nb = S // T
    c2 = LOG2E / math.sqrt(HEAD_DIM)

    def body(q_ref, k_ref, v_ref, fq_ref, fk_ref, o_ref, lse_ref, vaug_s):
        i = pl.program_id(1)

        @pl.when(i == 0)
        def _():
            vaug_s[:, :HEAD_DIM] = v_ref[...]
            vaug_s[:, HEAD_DIM:] = jnp.ones((S, HEAD_DIM), BF16)

        keep = _attn_consts(T)[0]
        TH = T // N_SPLIT

        def block(j, diagonal, state):
            r0 = pl.multiple_of(j * T, T)
            fk = fk_ref[j]
            new = []
            for g, (m_old, acc) in enumerate(state):
                rows = slice(g * TH, (g + 1) * TH)
                nk = min(T, -(-(g + 1) * TH // DIAG_STEP) * DIAG_STEP) if diagonal else T
                s = _dot(q_ref[rows, :], k_ref[pl.ds(r0, nk), :], NT) * c2 + (fq_ref[rows, :] - fk[:, :nk])
                if diagonal:
                    s = jnp.where(keep[rows, :nk], s, _NEG)
                m_new = jnp.maximum(m_old, jnp.max(s, axis=-1, keepdims=True))
                p = jnp.exp2(s - m_new).astype(BF16)
                new.append((m_new, jnp.exp2(m_old - m_new) * acc + _dot(p, vaug_s[pl.ds(r0, nk), :])))
            return tuple(new)

        init = tuple((jnp.full((TH, 1), _NEG, F32), jnp.zeros((TH, 2 * HEAD_DIM), F32)) for _ in range(N_SPLIT))
        state = lax.fori_loop(0, i, lambda j, st: block(j, False, st), init)
        state = block(i, True, state)
        for g, (m, acc) in enumerate(state):
            rows = slice(g * TH, (g + 1) * TH)
            o_ref[rows, :] = acc[:, :HEAD_DIM] / acc[:, HEAD_DIM:]
            lse_ref[rows, :] = m + jnp.log2(acc[:, HEAD_DIM:HEAD_DIM + 1])

    nh = H
    return _carry_call(
        body, name="attn_fwd", grid=(H, nb), jobs=jobs, args=[zm, zm, zm, f2col, f2row],
        semantics=("arbitrary", "arbitrary"),
        in_specs=[
            pl.BlockSpec((T, HEAD_DIM), lambda h, i: (i, h)),
            pl.BlockSpec((S, HEAD_DIM), lambda h, i: (0, nh + h)),
            pl.BlockSpec((S, HEAD_DIM), lambda h, i: (0, 2 * nh + h)),
            pl.BlockSpec((None, T, 1), lambda h, i: (h, i, 0)),
            pl.BlockSpec((None, nb, 1, T), lambda h, i: (h, 0, 0, 0)),
        ],
        out_specs=[pl.BlockSpec((T, HEAD_DIM), lambda h, i: (i, h)), pl.BlockSpec((None, T, 1), lambda h, i: (h, i, 0))],
        out_shape=[jax.ShapeDtypeStruct((S, H * HEAD_DIM), F32), jax.ShapeDtypeStruct((H, S, 1), F32)],
        scratch_shapes=[pltpu.VMEM((S, 2 * HEAD_DIM), BF16)],
    )


def _attn2_bwd_dq(zm, dattn, f2col, f2row, lse2_col, delta_col, T, jobs=()):
    S = zm.shape[0]
    H = f2col.shape[0]
    nb = S // T
    scale = 1.0 / math.sqrt(HEAD_DIM)
    c2 = LOG2E * scale

    def body(q_ref, k_ref, v_ref, do_ref, fq_ref, fk_ref, lse_ref, dl_ref, dq_ref, rs_ref, bias_s, do_s):
        i = pl.program_id(1)
        keep = _attn_consts(T)[0]
        TH = T // N_SPLIT_DIAG
        bias_s[...] = fq_ref[...] - lse_ref[...]
        do_s[...] = do_ref[...].astype(BF16)

        def part(rows, j, nk, state, masked):
            acc, rs = state
            r0 = pl.multiple_of(j * T, T)
            kb = k_ref[pl.ds(r0, nk), :]
            s = _dot(q_ref[rows, :], kb, NT) * c2 + (bias_s[rows, :] - fk_ref[j][:, :nk])
            if masked:
                s = jnp.where(keep[rows, :nk], s, _NEG)
            ds = jnp.exp2(s) * (_dot(do_s[rows, :], v_ref[pl.ds(r0, nk), :], NT) - dl_ref[rows, :])
            return acc + _dot(ds.astype(BF16), kb), rs + jnp.sum(ds, axis=-1, keepdims=True)

        def step(j, state):
            return part(slice(0, T), j, T, state, False)

        acc, rs = lax.fori_loop(0, i, step, (jnp.zeros((T, HEAD_DIM), F32), jnp.zeros((T, 1), F32)))
        for g in range(N_SPLIT_DIAG):
            rows = slice(g * TH, (g + 1) * TH)
            acc_g, rs_g = part(rows, i, (g + 1) * TH, (acc[rows, :], rs[rows, :]), True)
            dq_ref[rows, :] = (acc_g * scale).astype(BF16)
            rs_ref[rows, :] = rs_g

    nh = H
    col = pl.BlockSpec((None, T, 1), lambda h, i: (h, i, 0))
    blk = pl.BlockSpec((T, HEAD_DIM), lambda h, i: (i, h))
    return _carry_call(
        body, name="attn_bwd_dq", grid=(H, nb), jobs=jobs,
        args=[zm, zm, zm, dattn, f2col, f2row, lse2_col, delta_col], semantics=("arbitrary", "arbitrary"),
        in_specs=[
            blk,
            pl.BlockSpec((S, HEAD_DIM), lambda h, i: (0, nh + h)),
            pl.BlockSpec((S, HEAD_DIM), lambda h, i: (0, 2 * nh + h)),
            blk, col,
            pl.BlockSpec((None, nb, 1, T), lambda h, i: (h, 0, 0, 0)),
            col, col,
        ],
        out_specs=[blk, col],
        out_shape=[jax.ShapeDtypeStruct((S, H * HEAD_DIM), BF16), jax.ShapeDtypeStruct((H, S, 1), F32)],
        scratch_shapes=[pltpu.VMEM((T, 1), F32), pltpu.VMEM((T, HEAD_DIM), BF16)],
    )


def _attn2_bwd_dkv(zm, dattn, f2col, f2row, lse2_row, delta_row, rowsum_row, T, jobs=()):
    S = zm.shape[0]
    H = f2col.shape[0]
    nb = S // T
    scale = 1.0 / math.sqrt(HEAD_DIM)
    c2 = LOG2E * scale

    def body(q_ref, k_ref, v_ref, do_ref, fk_ref, fq_ref, lse_ref, dl_ref, rs_ref, dk_ref, dv_ref, df_ref):
        j = pl.program_id(1)
        keep = _attn_consts(T)[1]
        TH = T // N_SPLIT_DIAG

        def part(rows, i, c0, state, masked):
            dk, dv, df = state
            r0 = pl.multiple_of(i * T + c0, TH)
            qb = q_ref[pl.ds(r0, T - c0), :]
            do = do_ref[pl.ds(r0, T - c0), :].astype(BF16)
            bias = (fq_ref[i] - lse_ref[i])[:, c0:]
            dl = (dl_ref[i] + rs_ref[i])[:, c0:]
            st = _dot(k_ref[rows, :], qb, NT) * c2 + (bias - fk_ref[rows, :])
            if masked:
                st = jnp.where(keep[rows, c0:], st, _NEG)
            pt = jnp.exp2(st)
            dst = pt * (_dot(v_ref[rows, :], do, NT) - dl)
            return (dk + _dot(dst.astype(BF16), qb), dv + _dot(pt.astype(BF16), do),
                    df - jnp.sum(dst, axis=-1, keepdims=True))

        groups = []
        for g in range(N_SPLIT_DIAG):
            zero = (jnp.zeros((TH, HEAD_DIM), F32), jnp.zeros((TH, HEAD_DIM), F32), jnp.zeros((TH, 1), F32))
            groups.append(part(slice(g * TH, (g + 1) * TH), j, g * TH, zero, True))
        state = tuple(jnp.concatenate([grp[n] for grp in groups], axis=0) for n in range(3))
        dk, dv, df = lax.fori_loop(j + 1, nb, lambda i, st: part(slice(0, T), i, 0, st, False), state)
        dk_ref[...] = (dk * scale).astype(BF16)
        dv_ref[...] = dv.astype(BF16)
        df_ref[...] = df

    nh = H
    row = pl.BlockSpec((None, nb, 1, T), lambda h, j: (h, 0, 0, 0))
    whole = pl.BlockSpec((S, HEAD_DIM), lambda h, j: (0, h))
    kv_out = pl.BlockSpec((T, HEAD_DIM), lambda h, j: (j, h))
    col = pl.BlockSpec((None, T, 1), lambda h, j: (h, j, 0))
    return _carry_call(
        body, name="attn_bwd_dkv", grid=(H, nb), jobs=jobs,
        args=[zm, zm, zm, dattn, f2col, f2row, lse2_row, delta_row, rowsum_row],
        semantics=("arbitrary", "arbitrary"),
        in_specs=[
            whole,
            pl.BlockSpec((T, HEAD_DIM), lambda h, j: (j, nh + h)),
            pl.BlockSpec((T, HEAD_DIM), lambda h, j: (j, 2 * nh + h)),
            whole, col, row, row, row, row,
        ],
        out_specs=[kv_out, kv_out, col],
        out_shape=[jax.ShapeDtypeStruct((S, H * HEAD_DIM), BF16), jax.ShapeDtypeStruct((S, H * HEAD_DIM), BF16),
                   jax.ShapeDtypeStruct((H, S, 1), F32)],
        scratch_shapes=[],
    )


def _attn_fwd(zm, fcol, frow, T, jobs=()):
    S = zm.shape[0]
    H = fcol.shape[0]
    nb = S // T
    scale = 1.0 / math.sqrt(HEAD_DIM)

    def body(q_ref, k_ref, v_ref, fq_ref, fk_ref, o_ref, lse_ref, m_s, l_s, acc_s):
        i = pl.program_id(1)
        j = pl.program_id(2)

        @pl.when(j == 0)
        def _():
            m_s[...] = jnp.full_like(m_s, _NEG)
            l_s[...] = jnp.zeros_like(l_s)
            acc_s[...] = jnp.zeros_like(acc_s)

        @pl.when(j <= i)
        def _():
            s = _dot(q_ref[...], k_ref[...], NT) * scale + (fq_ref[...] - fk_ref[...])
            keep = (_iota2((T, T), 1) + j * T) <= (_iota2((T, T), 0) + i * T)
            s = jnp.where(keep, s, _NEG)
            m_new = jnp.maximum(m_s[...], jnp.max(s, axis=-1, keepdims=True))
            alpha = jnp.exp(m_s[...] - m_new)
            p = jnp.exp(s - m_new)
            l_s[...] = alpha * l_s[...] + jnp.sum(p, axis=-1, keepdims=True)
            acc_s[...] = alpha * acc_s[...] + _dot(p.astype(BF16), v_ref[...])
            m_s[...] = m_new

        @pl.when(j == nb - 1)
        def _():
            o_ref[...] = acc_s[...] / l_s[...]
            lse_ref[...] = m_s[...] + jnp.log(l_s[...])

    nh = H
    return _carry_call(
        body, name="attn_fwd", grid=(H, nb, nb), jobs=jobs, args=[zm, zm, zm, fcol, frow],
        semantics=("parallel", "parallel", "arbitrary"),
        in_specs=[
            pl.BlockSpec((T, HEAD_DIM), lambda h, i, j: (i, h)),
            pl.BlockSpec((T, HEAD_DIM), lambda h, i, j: (jnp.minimum(j, i), nh + h)),
            pl.BlockSpec((T, HEAD_DIM), lambda h, i, j: (jnp.minimum(j, i), 2 * nh + h)),
            pl.BlockSpec((None, T, 1), lambda h, i, j: (h, i, 0)),
            pl.BlockSpec((None, 1, T), lambda h, i, j: (h, 0, jnp.minimum(j, i))),
        ],
        out_specs=[
            pl.BlockSpec((T, HEAD_DIM), lambda h, i, j: (i, h)),
            pl.BlockSpec((None, T, 1), lambda h, i, j: (h, i, 0)),
        ],
        out_shape=[jax.ShapeDtypeStruct((S, H * HEAD_DIM), F32), jax.ShapeDtypeStruct((H, S, 1), F32)],
        scratch_shapes=[pltpu.VMEM((T, 1), F32), pltpu.VMEM((T, 1), F32), pltpu.VMEM((T, HEAD_DIM), F32)],
    )


def _attn_delta(dattn, attn, tr):
    S, DA = attn.shape
    H = DA // HEAD_DIM

    def body(do_ref, o_ref, out_ref):
        lo = _iota2((DA, LANES), 1) * HEAD_DIM
        sel = ((_iota2((DA, LANES), 0) >= lo) & (_iota2((DA, LANES), 0) < lo + HEAD_DIM)).astype(BF16)
        d = _dot3(do_ref[...] * o_ref[...], sel)
        for c in range(tr // CHUNK):
            out_ref[:, c * CHUNK:(c + 1) * CHUNK] = d[c * CHUNK:(c + 1) * CHUNK, :].T[0:H, :]

    return pl.pallas_call(
        body, name="attn_delta", grid=(S // tr,),
        in_specs=[pl.BlockSpec((tr, DA), lambda i: (i, 0))] * 2,
        out_specs=pl.BlockSpec((H, tr), lambda i: (0, i)),
        out_shape=jax.ShapeDtypeStruct((H, S), F32),
        compiler_params=_params(("parallel",)),
    )(dattn, attn)


def _attn_bwd_dq(zm, dattn, fcol, frow, lse_col, delta_col, T, jobs=()):
    S = zm.shape[0]
    H = fcol.shape[0]
    nb = S // T
    scale = 1.0 / math.sqrt(HEAD_DIM)

    def body(q_ref, k_ref, v_ref, do_ref, fq_ref, fk_ref, lse_ref, dl_ref, dq_ref, rs_ref, acc_s, rs_s):
        i = pl.program_id(1)
        j = pl.program_id(2)

        @pl.when(j == 0)
        def _():
            acc_s[...] = jnp.zeros_like(acc_s)
            rs_s[...] = jnp.zeros_like(rs_s)

        @pl.when(j <= i)
        def _():
            s = _dot(q_ref[...], k_ref[...], NT) * scale + (fq_ref[...] - fk_ref[...])
            keep = (_iota2((T, T), 1) + j * T) <= (_iota2((T, T), 0) + i * T)
            p = jnp.exp(jnp.where(keep, s - lse_ref[...], _NEG))
            dp = _dot(do_ref[...].astype(BF16), v_ref[...], NT)
            ds = p * (dp - dl_ref[...])
            acc_s[...] += _dot(ds.astype(BF16), k_ref[...])
            rs_s[...] += jnp.sum(ds, axis=-1, keepdims=True)

        @pl.when(j == nb - 1)
        def _():
            dq_ref[...] = (acc_s[...] * scale).astype(BF16)
            rs_ref[...] = rs_s[...]

    nh = H
    col = pl.BlockSpec((None, T, 1), lambda h, i, j: (h, i, 0))
    return _carry_call(
        body, name="attn_bwd_dq", grid=(H, nb, nb), jobs=jobs,
        args=[zm, zm, zm, dattn, fcol, frow, lse_col, delta_col], semantics=("parallel", "parallel", "arbitrary"),
        in_specs=[
            pl.BlockSpec((T, HEAD_DIM), lambda h, i, j: (i, h)),
            pl.BlockSpec((T, HEAD_DIM), lambda h, i, j: (jnp.minimum(j, i), nh + h)),
            pl.BlockSpec((T, HEAD_DIM), lambda h, i, j: (jnp.minimum(j, i), 2 * nh + h)),
            pl.BlockSpec((T, HEAD_DIM), lambda h, i, j: (i, h)),
            col,
            pl.BlockSpec((None, 1, T), lambda h, i, j: (h, 0, jnp.minimum(j, i))),
            col, col,
        ],
        out_specs=[pl.BlockSpec((T, HEAD_DIM), lambda h, i, j: (i, h)), col],
        out_shape=[jax.ShapeDtypeStruct((S, H * HEAD_DIM), BF16), jax.ShapeDtypeStruct((H, S, 1), F32)],
        scratch_shapes=[pltpu.VMEM((T, HEAD_DIM), F32), pltpu.VMEM((T, 1), F32)],
    )


def _attn_bwd_dkv(zm, dattn, fcol, frow, lse_row, delta_row, rowsum_row, T, jobs=()):
    S = zm.shape[0]
    H = fcol.shape[0]
    nb = S // T
    scale = 1.0 / math.sqrt(HEAD_DIM)

    def body(q_ref, k_ref, v_ref, do_ref, fk_ref, fq_ref, lse_ref, dl_ref, rs_ref,
             dk_ref, dv_ref, df_ref, dk_s, dv_s, df_s):
        j = pl.program_id(1)
        i = pl.program_id(2)

        @pl.when(i == 0)
        def _():
            dk_s[...] = jnp.zeros_like(dk_s)
            dv_s[...] = jnp.zeros_like(dv_s)
            df_s[...] = jnp.zeros_like(df_s)

        @pl.when(i >= j)
        def _():
            st = _dot(k_ref[...], q_ref[...], NT) * scale + (fq_ref[...] - fk_ref[...])
            keep = (_iota2((T, T), 0) + j * T) <= (_iota2((T, T), 1) + i * T)
            pt = jnp.exp(jnp.where(keep, st - lse_ref[...], _NEG))
            do = do_ref[...].astype(BF16)
            dpt = _dot(v_ref[...], do, NT)
            dst = pt * (dpt - (dl_ref[...] + rs_ref[...]))
            dv_s[...] += _dot(pt.astype(BF16), do)
            dk_s[...] += _dot(dst.astype(BF16), q_ref[...])
            df_s[...] -= jnp.sum(dst, axis=-1, keepdims=True)

        @pl.when(i == nb - 1)
        def _():
            dk_ref[...] = (dk_s[...] * scale).astype(BF16)
            dv_ref[...] = dv_s[...].astype(BF16)
            df_ref[...] = df_s[...]

    nh = H
    row = pl.BlockSpec((None, 1, T), lambda h, j, i: (h, 0, jnp.maximum(i, j)))
    kv_out = pl.BlockSpec((T, HEAD_DIM), lambda h, j, i: (j, h))
    return _carry_call(
        body, name="attn_bwd_dkv", grid=(H, nb, nb), jobs=jobs,
        args=[zm, zm, zm, dattn, fcol, frow, lse_row, delta_row, rowsum_row],
        semantics=("parallel", "parallel", "arbitrary"),
        in_specs=[
            pl.BlockSpec((T, HEAD_DIM), lambda h, j, i: (jnp.maximum(i, j), h)),
            pl.BlockSpec((T, HEAD_DIM), lambda h, j, i: (j, nh + h)),
            pl.BlockSpec((T, HEAD_DIM), lambda h, j, i: (j, 2 * nh + h)),
            pl.BlockSpec((T, HEAD_DIM), lambda h, j, i: (jnp.maximum(i, j), h)),
            pl.BlockSpec((None, T, 1), lambda h, j, i: (h, j, 0)),
            row, row, row, row,
        ],
        out_specs=[kv_out, kv_out, pl.BlockSpec((None, T, 1), lambda h, j, i: (h, j, 0))],
        out_shape=[jax.ShapeDtypeStruct((S, H * HEAD_DIM), BF16), jax.ShapeDtypeStruct((S, H * HEAD_DIM), BF16),
                   jax.ShapeDtypeStruct((H, S, 1), F32)],
        scratch_shapes=[pltpu.VMEM((T, HEAD_DIM), F32), pltpu.VMEM((T, HEAD_DIM), F32), pltpu.VMEM((T, 1), F32)],
    )


def _ln_stats(x):
    mu = jnp.mean(x, axis=-1, keepdims=True)
    xc = x - mu
    rstd = lax.rsqrt(jnp.mean(xc * xc, axis=-1, keepdims=True) + EPS)
    return xc * rstd, rstd


def _tril_mask():
    return _iota2((CHUNK, CHUNK), 0) >= _iota2((CHUNK, CHUNK), 1)


def _gmlp_fwd(zm, ln_g, ln_b, w_s, bs_col, tr):
    S = zm.shape[0]
    H = w_s.shape[0]
    DG = H * HEAD_DIM

    def body(zu_ref, zv_ref, g_ref, b_ref, w_ref, bs_ref, out_ref):
        u = _gelu(zu_ref[...].astype(F32))
        y, _ = _ln_stats(_gelu(zv_ref[...].astype(F32)))
        v = (y * g_ref[...] + b_ref[...]).astype(BF16)
        mask = _tril_mask()
        for h in range(H):
            wc = jnp.where(mask, w_ref[h], 0.0).astype(BF16)
            cs = slice(h * HEAD_DIM, (h + 1) * HEAD_DIM)
            for c in range(tr // CHUNK):
                rs = slice(c * CHUNK, (c + 1) * CHUNK)
                mix = _dot(wc, v[rs, cs]) + bs_ref[h]
                out_ref[rs, cs] = u[rs, cs] * mix

    full = lambda a: pl.BlockSpec(a.shape, lambda i: (0,) * a.ndim)
    return pl.pallas_call(
        body, name="gmlp_fwd", grid=(S // tr,),
        in_specs=[pl.BlockSpec((tr, DG), lambda i: (i, 3)), pl.BlockSpec((tr, DG), lambda i: (i, 4)),
                  full(ln_g), full(ln_b), full(w_s), full(bs_col)],
        out_specs=pl.BlockSpec((tr, DG), lambda i: (i, 0)),
        out_shape=jax.ShapeDtypeStruct((S, DG), F32),
        compiler_params=_params(("parallel",)),
    )(zm, zm, ln_g, ln_b, w_s, bs_col)


def _gmlp_bwd(dgm, zm, ln_g, ln_b, w_s, w_st, bs_col, tr):
    S = zm.shape[0]
    H = w_s.shape[0]
    DG = H * HEAD_DIM

    def body(dg_ref, zu_ref, zv_ref, g_ref, b_ref, w_ref, wt_ref, bs_ref,
             dzu_ref, dzv_ref, dw_ref, dbs_ref, dlg_ref, dlb_ref, dv_s):
        @pl.when(pl.program_id(0) == 0)
        def _():
            dw_ref[...] = jnp.zeros_like(dw_ref)
            dbs_ref[...] = jnp.zeros_like(dbs_ref)
            dlg_ref[...] = jnp.zeros_like(dlg_ref)
            dlb_ref[...] = jnp.zeros_like(dlb_ref)

        zu = zu_ref[...].astype(F32)
        zv = zv_ref[...].astype(F32)
        u = _gelu(zu)
        y, rstd = _ln_stats(_gelu(zv))
        v = (y * g_ref[...] + b_ref[...]).astype(BF16)
        dgm_blk = dg_ref[...]
        mask = _tril_mask()
        mask_t = _iota2((CHUNK, CHUNK), 0) <= _iota2((CHUNK, CHUNK), 1)
        for h in range(H):
            wc = jnp.where(mask, w_ref[h], 0.0).astype(BF16)
            wct = jnp.where(mask_t, wt_ref[h], 0.0).astype(BF16)
            cs = slice(h * HEAD_DIM, (h + 1) * HEAD_DIM)
            dw = jnp.zeros((CHUNK, CHUNK), F32)
            dbs = jnp.zeros((CHUNK, 1), F32)
            for c in range(tr // CHUNK):
                rs = slice(c * CHUNK, (c + 1) * CHUNK)
                vch = v[rs, cs]
                mix = _dot(wc, vch) + bs_ref[h]
                dg = dgm_blk[rs, cs]
                dzu_ref[rs, cs] = (dg * mix * _gelu_grad(zu[rs, cs])).astype(BF16)
                dmix = dg * u[rs, cs]
                dbs = dbs + jnp.sum(dmix, axis=-1, keepdims=True)
                dmix_b = dmix.astype(BF16)
                dw = dw + _dot(dmix_b, vch, NT)
                dv_s[rs, cs] = _dot(wct, dmix_b)
            dw_ref[h] += jnp.where(mask, dw, 0.0)
            dbs_ref[h] += dbs
        dv = dv_s[...]
        dlg_ref[...] += jnp.sum(dv * y, axis=0, keepdims=True)
        dlb_ref[...] += jnp.sum(dv, axis=0, keepdims=True)
        dy = dv * g_ref[...]
        dgv = rstd * (dy - jnp.mean(dy, axis=-1, keepdims=True) - y * jnp.mean(dy * y, axis=-1, keepdims=True))
        dzv_ref[...] = (dgv * _gelu_grad(zv)).astype(BF16)

    full = lambda a: pl.BlockSpec(a.shape, lambda i: (0,) * a.ndim)
    rows = pl.BlockSpec((tr, DG), lambda i: (i, 0))
    return pl.pallas_call(
        body, name="gmlp_bwd", grid=(S // tr,),
        in_specs=[rows, pl.BlockSpec((tr, DG), lambda i: (i, 3)), pl.BlockSpec((tr, DG), lambda i: (i, 4)),
                  full(ln_g), full(ln_b), full(w_s), full(w_st), full(bs_col)],
        out_specs=[rows, rows, full(w_s), full(bs_col), full(ln_g), full(ln_b)],
        out_shape=[jax.ShapeDtypeStruct((S, DG), BF16), jax.ShapeDtypeStruct((S, DG), BF16),
                   jax.ShapeDtypeStruct(w_s.shape, F32), jax.ShapeDtypeStruct(bs_col.shape, F32),
                   jax.ShapeDtypeStruct(ln_g.shape, F32), jax.ShapeDtypeStruct(ln_b.shape, F32)],
        scratch_shapes=[pltpu.VMEM((tr, DG), F32)],
        compiler_params=_params(("arbitrary",)),
    )(dgm, zm, zm, ln_g, ln_b, w_s, w_st, bs_col)


def _all_gather(name, blk):
    R, C = blk.shape

    def body(x_ref, out_ref, send_sems, recv_sems, local_sem):
        x, y, c = _me()
        me, sibling = (x, y, c), (x, y, 1 - c)
        chips = [(1 - x, y), (x, 1 - y), (1 - x, 1 - y)]

        def slab(px, py, pc):
            return out_ref.at[4 * px + 2 * py + pc]

        def copy(k, block, to, src=None):
            return pltpu.make_async_remote_copy(
                src_ref=slab(*block) if src is None else src, dst_ref=slab(*block),
                send_sem=send_sems.at[k], recv_sem=recv_sems.at[k], device_id=to, device_id_type=MESH)

        mine = pltpu.make_async_copy(x_ref, slab(*me), local_sem)
        mine.start()
        first = [copy(0, me, sibling, src=x_ref)]
        first += [copy(1 + n, me, (*chip, c), src=x_ref) for n, chip in enumerate(chips)]
        for cp in first:
            cp.start()
        passed = [copy(4 + n, (*chip, c), sibling) for n, chip in enumerate(chips)]
        for n, chip in enumerate(chips):
            copy(1 + n, (*chip, c), me).wait_recv()
            passed[n].start()
        copy(0, sibling, me).wait_recv()
        for n, chip in enumerate(chips):
            copy(4 + n, (*chip, 1 - c), me).wait_recv()
        for cp in first + passed:
            cp.wait_send()
        mine.wait()

    return pl.pallas_call(
        body, name=name, out_shape=jax.ShapeDtypeStruct((N_DEV, R, C), blk.dtype),
        in_specs=[_ANY], out_specs=_ANY,
        scratch_shapes=[pltpu.SemaphoreType.DMA((7,)), pltpu.SemaphoreType.DMA((7,)), pltpu.SemaphoreType.DMA(())],
    )(blk)


def _row_tile(R, C, itemsize=4, target_bytes=2 * 1024 * 1024):
    tr = R
    while tr % 2 == 0 and tr * C * itemsize > target_bytes and (tr // 2) % 16 == 0:
        tr //= 2
    return tr


def _rs_add1(name, g4, recv, c_idx):
    _, _, R, C = g4.shape
    tr = _row_tile(R, C)

    def body(c_ref, g_ref, r_ref, h_ref, hb_ref):
        h = g_ref[...] + r_ref[...].astype(F32)
        h_ref[...] = h
        hb_ref[...] = h.astype(BF16)

    blk = pl.BlockSpec((None, tr, C), lambda p, i, c_ref: (p, i, 0))
    return pl.pallas_call(
        body, name=name,
        grid_spec=pltpu.PrefetchScalarGridSpec(
            num_scalar_prefetch=1, grid=(4, R // tr),
            in_specs=[pl.BlockSpec((None, None, tr, C), lambda p, i, c_ref: (p, c_ref[0], i, 0)), blk],
            out_specs=[blk, blk]),
        out_shape=[jax.ShapeDtypeStruct((4, R, C), F32), jax.ShapeDtypeStruct((4, R, C), BF16)],
        compiler_params=_params(("parallel", "parallel")),
    )(c_idx, g4, recv)


def _rs_add1_windows(name, g, recv, first_blocks):
    _, R, W = recv.shape

    def body(t_ref, g_ref, r_ref, h_ref, hb_ref):
        h = g_ref[...] + r_ref[...].astype(F32)
        h_ref[...] = h
        hb_ref[...] = h.astype(BF16)

    blk = pl.BlockSpec((None, R, LANES), lambda p, l, t_ref: (p, 0, l))
    return pl.pallas_call(
        body, name=name,
        grid_spec=pltpu.PrefetchScalarGridSpec(
            num_scalar_prefetch=1, grid=(4, W // LANES),
            in_specs=[pl.BlockSpec((R, LANES), lambda p, l, t_ref: (0, t_ref[p] + l)), blk],
            out_specs=[blk, blk]),
        out_shape=[jax.ShapeDtypeStruct((4, R, W), F32), jax.ShapeDtypeStruct((4, R, W), BF16)],
        compiler_params=_params(("parallel", "parallel")),
    )(first_blocks, g, recv)


def _add_windows(name, windows, first, second, n_blocks):
    _, R, W = windows.shape
    dev1 = jnp.asarray([d for d, _ in first], jnp.int32)
    blk1 = jnp.asarray([b for _, b in first], jnp.int32)
    dev2 = jnp.asarray([max(d, 0) for d, _ in second], jnp.int32)
    blk2 = jnp.asarray([b for _, b in second], jnp.int32)
    two = jnp.asarray([int(d >= 0) for d, _ in second], jnp.int32)

    def body(d1_ref, b1_ref, d2_ref, b2_ref, two_ref, a_ref, b_ref, out_ref):
        k = pl.program_id(0)

        @pl.when(two_ref[k] == 0)
        def _():
            out_ref[...] = a_ref[...]

        @pl.when(two_ref[k] != 0)
        def _():
            out_ref[...] = a_ref[...] + b_ref[...]

    return pl.pallas_call(
        body, name=name,
        grid_spec=pltpu.PrefetchScalarGridSpec(
            num_scalar_prefetch=5, grid=(n_blocks,),
            in_specs=[pl.BlockSpec((None, R, LANES), lambda k, d1, b1, d2, b2, t: (d1[k], 0, b1[k])),
                      pl.BlockSpec((None, R, LANES), lambda k, d1, b1, d2, b2, t: (d2[k], 0, b2[k]))],
            out_specs=pl.BlockSpec((R, LANES), lambda k, d1, b1, d2, b2, t: (0, k))),
        out_shape=jax.ShapeDtypeStruct((R, n_blocks * LANES), windows.dtype),
        compiler_params=_params(("parallel",)),
    )(dev1, blk1, dev2, blk2, two, windows, windows)


def _rs_add2(name, h, recv, p_idx):
    _, R, C = h.shape
    tr = _row_tile(R, C)

    def body(p_ref, h_ref, r_ref, out_ref):
        out_ref[...] = ((h_ref[...] + r_ref[0].astype(F32)) + r_ref[1].astype(F32)) + r_ref[2].astype(F32)

    return pl.pallas_call(
        body, name=name,
        grid_spec=pltpu.PrefetchScalarGridSpec(
            num_scalar_prefetch=1, grid=(R // tr,),
            in_specs=[pl.BlockSpec((None, tr, C), lambda i, p_ref: (p_ref[0], i, 0)),
                      pl.BlockSpec((3, tr, C), lambda i, p_ref: (0, i, 0))],
            out_specs=pl.BlockSpec((tr, C), lambda i, p_ref: (i, 0))),
        out_shape=jax.ShapeDtypeStruct((R, C), F32),
        compiler_params=_params(("parallel",)),
    )(p_idx, h, recv)


def _sum8(name, g):
    _, R, C = g.shape

    def body(g_ref, out_ref):
        acc = g_ref[0]
        for d in range(1, N_DEV):
            acc = acc + g_ref[d]
        out_ref[...] = acc

    return pl.pallas_call(body, name=name, out_shape=jax.ShapeDtypeStruct((R, C), F32),
                          compiler_params=_params())(g)


def _adamw_math(w, g, m, v):
    m = ADAM_B1 * m + (1.0 - ADAM_B1) * g
    v = ADAM_B2 * v + (1.0 - ADAM_B2) * (g * g)
    m_hat = m / (1.0 - ADAM_B1 ** ADAM_STEP)
    v_hat = v / (1.0 - ADAM_B2 ** ADAM_STEP)
    delta = -ADAM_LR * (m_hat / (jnp.sqrt(v_hat) + ADAM_EPS) + ADAM_WD * w)
    return delta, m, v


def _adamw(name, w, g, m, v):
    R, C = w.shape
    tr = _row_tile(R, C, target_bytes=1024 * 1024)
    return _row_call(name, lambda *a: (_adamw_math(*a), ()), [w, g, m, v], [], [(C, F32)] * 3, [], tr)


def _adamw_many(name, ws, gs, ms, vs):
    n = len(ws)

    def body(*refs):
        ins, outs = refs[:4 * n], refs[4 * n:]
        for k in range(n):
            res = _adamw_math(ins[k][...], ins[n + k][...], ins[2 * n + k][...], ins[3 * n + k][...])
            for t in range(3):
                outs[t * n + k][...] = res[t]

    out = pl.pallas_call(
        body, name=name, out_shape=[jax.ShapeDtypeStruct(w.shape, F32) for _ in range(3) for w in ws],
        compiler_params=_params(),
    )(*ws, *gs, *ms, *vs)
    return out[:n], out[n:2 * n], out[2 * n:]


def _pack(parts):
    flat = []
    total = 0
    for a in parts:
        n = math.prod(a.shape)
        flat.append(a.reshape(-1).astype(F32))
        if n % LANES:
            flat.append(jnp.zeros((-n % LANES,), F32))
        total += n + (-n % LANES)
    if total % (8 * LANES):
        flat.append(jnp.zeros((-total % (8 * LANES),), F32))
    return jnp.concatenate(flat).reshape(-1, LANES)


def _unpack(packed, shapes):
    out = []
    r = 0
    for shp in shapes:
        n = math.prod(shp)
        nr = -(-n // LANES)
        out.append(packed[r:r + nr].reshape(-1)[:n].reshape(shp))
        r += nr
    return out


def kernel(x, norm_mix_g, w_in, b_f, gmlp_ln_g, gmlp_ln_b, w_s, b_s, attn_out_g, gmlp_out_g, w_out, norm_ffn_g, w_ff1, w_ff2, norm_final_g, loss_target, m_norm_mix_g, m_w_in, m_b_f, m_gmlp_ln_g, m_gmlp_ln_b, m_w_s, m_b_s, m_attn_out_g, m_gmlp_out_g, m_w_out, m_norm_ffn_g, m_w_ff1, m_w_ff2, m_norm_final_g, v_norm_mix_g, v_w_in, v_b_f, v_gmlp_ln_g, v_gmlp_ln_b, v_w_s, v_b_s, v_attn_out_g, v_gmlp_out_g, v_w_out, v_norm_ffn_g, v_w_ff1, v_w_ff2, v_norm_final_g):
    S, D = x.shape[1], x.shape[2]
    H = b_f.shape[1]
    DA = H * HEAD_DIM
    DG = gmlp_ln_g.shape[1]
    DQKV = 3 * DA
    DMAIN = DQKV + 2 * DG
    DIN = DMAIN + H
    DFF = w_ff1.shape[2] * N_DEV
    w_in_cols = w_in.shape[2]
    assert DIN == w_in_cols * N_DEV and DA == DG and D == DA + DG

    T_ATT = min(T_ATT_MAX, S)
    TR = min(TR_MAX, S)

    x0 = x[0]
    tgt = loss_target[0]
    g_final = norm_final_g.reshape(1, D)

    FB = DFF // N_DEV
    x_pos, y_pos, c_pos = _me()
    me_idx = 4 * x_pos + 2 * y_pos + c_pos

    WW = -(-(w_in_cols + LANES - 1) // LANES) * LANES
    to_main = lambda col: col if col <= DQKV else max(DQKV, col - H)
    lo = [to_main(n * w_in_cols) for n in range(N_DEV)]
    hi = [to_main((n + 1) * w_in_cols) for n in range(N_DEV)]
    starts = [v // LANES * LANES for v in lo]
    gate_dev = DQKV // w_in_cols
    n_before = DQKV - gate_dev * w_in_cols
    g0 = lo[gate_dev] - starts[gate_dev]
    stash = -(-(g0 + w_in_cols - H) // LANES) * LANES
    assert all(hi[n] <= starts[n] + WW <= DMAIN for n in range(N_DEV))
    assert gate_dev * w_in_cols <= DQKV and DQKV + H <= (gate_dev + 1) * w_in_cols and stash + LANES <= WW
    shard = w_in[0].astype(BF16)

    def my_window(n):
        if n != gate_dev:
            return lambda s: jnp.pad(s, ((0, 0), (lo[n] - starts[n], WW - w_in_cols - (lo[n] - starts[n]))))
        return lambda s: jnp.concatenate([
            jnp.zeros((D, g0), BF16), s[:, :n_before], s[:, n_before + H:],
            jnp.zeros((D, stash - g0 - (w_in_cols - H)), BF16), s[:, n_before:n_before + H],
            jnp.zeros((D, WW - stash - H), BF16)], axis=1)
    windows = _all_gather("ag_w_in", lax.switch(me_idx, [my_window(n) for n in range(N_DEV)], shard))
    first, second = [], []
    for blk in range(DMAIN // LANES):
        c0 = blk * LANES
        owners = [(n, (c0 - starts[n]) // LANES) for n in range(N_DEV) if lo[n] < c0 + LANES and hi[n] > c0]
        assert 1 <= len(owners) <= 2
        first.append(owners[0])
        second.append(owners[1] if len(owners) == 2 else (-1, 0))
    w_main = _add_windows("w_in_windows", windows, first, second, DMAIN // LANES)
    w_f = windows[gate_dev, :, stash:stash + LANES]
    c_idx = jnp.reshape(c_pos, (1,)).astype(jnp.int32)
    p_idx = jnp.reshape(2 * x_pos + y_pos, (1,)).astype(jnp.int32)

    (h,), _ = _row_call("rms_mix", lambda xb, g: ((_rms_fwd(xb, g),), ()), [x0], [norm_mix_g], [(D, BF16)], [], TR)
    (zm,), ((w_out_part,),) = _mm_nn("in_proj", h, w_main, [BF16], 2048, 1024, 2048,
                                     jobs=[_job_gather_chips(w_out[0].astype(BF16))])
    (zf,) = _mm_nn("in_proj_f", h, w_f, [F32], 1024, LANES, 2048)
    bf_pad = jnp.pad(b_f, ((0, 0), (0, LANES - H)))
    f_row = _fgate_fwd(zf, bf_pad)
    NB = S // T_ATT
    f_col3 = f_row.reshape(H, S, 1)
    f_row3 = f_row.reshape(H, NB, 1, T_ATT)
    (attn, lse_col3), ((w_out_all,), (w_ff1_part,)) = _attn2_fwd(
        zm, f_col3, f_row3, T_ATT, jobs=[_job_gather_sibling(w_out_part), _job_gather_chips(w_ff1[0].astype(BF16))])
    w_out_full = w_out_all.reshape(D, D)
    bs_col = b_s[0].reshape(H, CHUNK, 1)
    gm = _gmlp_fwd(zm, gmlp_ln_g, gmlp_ln_b, w_s[0], bs_col, TR)

    def merge_fn(a, g, ga, gg):
        return (jnp.concatenate([_rms_fwd(a, ga), _rms_fwd(g, gg)], axis=1),), ()
    (merged,), _ = _row_call("rms_merge", merge_fn, [attn, gm], [attn_out_g, gmlp_out_g], [(D, BF16)], [], TR)

    w_ff2_b = w_ff2[0].astype(BF16)
    TMR = min(512, S)

    def out_proj_fn(acc, res, g):
        xb = acc + res
        return (xb, _rms_fwd(xb, g)), ()
    ((x1, h2), _), ((w_ff1_all,), (w_ff2_q1,)) = _mm_rows(
        "out_proj", (S // TMR, 1), merged, pl.BlockSpec((TMR, D), lambda i, k: (i, 0)),
        w_out_full, pl.BlockSpec((D, D), lambda i, k: (0, 0)), NN, TMR, D, [x0], [norm_ffn_g],
        [(D, F32), (D, BF16)], [], out_proj_fn,
        jobs=[_job_gather_sibling(w_ff1_part), _job_gather_chips(w_ff2_b, part=(0, 1, 4))])

    tm, tn, tk = min(1024, S), min(1024, FB), min(2048, D)
    tm1 = min(2048, S)
    o_spec = pl.BlockSpec((tm1, tn), lambda i, j, k: (i, j))

    def relu_sq(acc):
        a = jnp.maximum(acc, 0.0)
        return a, a * a
    nj = FB // tn
    ff2_rest = [_job_gather_chips(w_ff2_b, part=(1, 4, 4), into=w_ff2_q1)]
    (a_act, a_sq), ((w_ff2_q2,),) = _mm(
        "ff1", (S // tm1, DFF // tn, D // tk), h2, pl.BlockSpec((tm1, tk), lambda i, j, k: (i, k)),
        w_ff1_all, pl.BlockSpec((None, tk, tn), lambda i, j, k: (j // nj, k, j % nj)), NN, (tm1, tn),
        [jax.ShapeDtypeStruct((S, DFF), BF16)] * 2, [o_spec] * 2, epilogue=relu_sq, jobs=ff2_rest)
    (w_ff2_all,) = _run_jobs("ag_w_ff2_sibling", [_job_gather_sibling(w_ff2_q2)])[0]
    w_ff2_full = w_ff2_all.reshape(DFF, D)
    def head_fn(acc, res, t, g):
        xb = acc + res
        rstd = lax.rsqrt(jnp.mean(xb * xb, axis=-1, keepdims=True) + EPS)
        xhat = xb * rstd
        err = xhat * g - t
        loss = 0.5 * jnp.sum(jnp.mean(err * err, axis=-1, keepdims=True), axis=0, keepdims=True)
        dy = err * (1.0 / D)
        dg = jnp.sum(dy * xhat, axis=0, keepdims=True)
        dxhat = dy * g
        dx = rstd * (dxhat - xhat * jnp.mean(dxhat * xhat, axis=-1, keepdims=True))
        return (dx, dx), (dg, jnp.broadcast_to(loss, (1, LANES)))
    tk_ff2 = min(1024, DFF)
    (dx2, dx2_b), (dg_final, loss_part) = _mm_rows(
        "ff2", (S // TMR, DFF // tk_ff2), a_sq, pl.BlockSpec((TMR, tk_ff2), lambda i, k: (i, k)),
        w_ff2_full, pl.BlockSpec((tk_ff2, D), lambda i, k: (k, 0)), NN, TMR, D, [x1, tgt], [g_final],
        [(D, F32), (D, BF16)], [D, LANES], head_fn)

    (da,) = _mm_nt("ff2_dx", dx2_b, w_ff2_full, [BF16], 2048, 1024, 2048, extras=[a_act],
                   epilogue=lambda acc, a: (2.0 * a.astype(F32) * acc,))
    dw_ff2, dw_ff2_b = _mm_tn("ff2_dw", a_sq, dx2_b, [F32, BF16], 1024, 2048, 1024)
    tm2, tk2 = min(2048, D), min(1024, S)
    dw1_spec = pl.BlockSpec((None, tm2, FB), lambda i, j, k: (j, i, 0))
    (dw_ff1, dw_ff1_b), ((r1_ff2,),) = _mm(
        "ff1_dw", (D // tm2, DFF // FB, S // tk2), h2, pl.BlockSpec((tk2, tm2), lambda i, j, k: (k, i)),
        da, pl.BlockSpec((tk2, FB), lambda i, j, k: (k, j)), TN, (tm2, FB),
        [jax.ShapeDtypeStruct((N_DEV, D, FB), F32), jax.ShapeDtypeStruct((N_DEV, D, FB), BF16)], [dw1_spec] * 2,
        epilogue=lambda acc: (acc, acc), jobs=[_job_scatter_sibling(dw_ff2_b.reshape(4, 2, FB, D))])
    h_ff2, hb_ff2 = _rs_add1("rs_add1_w_ff2", dw_ff2.reshape(4, 2, FB, D), r1_ff2, c_idx)
    def ffn_bwd_fn(dh, xb, dres, g):
        dx, dg = _rms_bwd(dh, xb, g)
        dx = dx + dres
        return (dx, dx), (dg,)
    tkb = min(1024, FB)
    nkb = FB // tkb
    ((dx1, dx1_b), (dg_ffn,)), ((r2_ff2,), (r1_ff1,)) = _mm_rows(
        "ff1_dx", (S // TMR, DFF // tkb), da, pl.BlockSpec((TMR, tkb), lambda i, k: (i, k)),
        w_ff1_all, pl.BlockSpec((None, D, tkb), lambda i, k: (k // nkb, 0, k % nkb)), NT, TMR, D, [x1, dx2],
        [norm_ffn_g], [(D, F32), (D, BF16)], [D], ffn_bwd_fn,
        jobs=[_job_scatter_chips(hb_ff2), _job_scatter_sibling(dw_ff1_b.reshape(4, 2, D, FB))])
    g_w_ff2 = _rs_add2("rs_add2_w_ff2", h_ff2, r2_ff2, p_idx)
    h_ff1, hb_ff1 = _rs_add1("rs_add1_w_ff1", dw_ff1.reshape(4, 2, D, FB), r1_ff1, c_idx)

    def merge_bwd_fn(dm, a, g, ga, gg):
        da_, dga = _rms_bwd(dm[:, :DA], a, ga)
        dg_, dgg = _rms_bwd(dm[:, DA:], g, gg)
        return (da_, dg_), (dga, dgg)
    (dattn, dgm), (dg_attn, dg_gmlp) = _mm_rows(
        "out_proj_dx", (S // TMR, 1), dx1_b, pl.BlockSpec((TMR, D), lambda i, k: (i, 0)),
        w_out_full, pl.BlockSpec((D, D), lambda i, k: (0, 0)), NT, TMR, D, [attn, gm], [attn_out_g, gmlp_out_g],
        [(DA, F32), (DG, F32)], [DA, DG], merge_bwd_fn)
    dw_out, dw_out_b = _mm_tn("out_proj_dw", merged, dx1_b, [F32, BF16], 2048, 1024, 1024)

    w_st = jnp.swapaxes(w_s[0], 1, 2)
    dzu, dzv, dw_s, dbs_col, dln_g, dln_b = _gmlp_bwd(dgm, zm, gmlp_ln_g, gmlp_ln_b, w_s[0], w_st, bs_col, TR)

    delta_row = _attn_delta(dattn, attn, TR)
    lse_row3 = lse_col3.reshape(H, NB, 1, T_ATT)
    (dq, ds_rowsum), ((r2_ff1,), (r1_out,)) = _attn2_bwd_dq(
        zm, dattn, f_col3, f_row3, lse_col3, delta_row.reshape(H, S, 1), T_ATT,
        jobs=[_job_scatter_chips(hb_ff1), _job_scatter_sibling(dw_out_b.reshape(4, 2, D // N_DEV, D))])
    g_w_ff1 = _rs_add2("rs_add2_w_ff1", h_ff1, r2_ff1, p_idx)
    h_out, hb_out = _rs_add1("rs_add1_w_out", dw_out.reshape(4, 2, D // N_DEV, D), r1_out, c_idx)
    (dk, dv, df_col3), ((r2_out,),) = _attn2_bwd_dkv(
        zm, dattn, f_col3, f_row3, lse_row3, delta_row.reshape(H, NB, 1, T_ATT),
        ds_rowsum.reshape(H, NB, 1, T_ATT), T_ATT,
        jobs=[_job_scatter_chips(hb_out)])
    g_w_out = _rs_add2("rs_add2_w_out", h_out, r2_out, p_idx)
    dzf, dbf = _fgate_bwd(df_col3.reshape(H, S), zf, bf_pad)

    dz_main = jnp.concatenate([dq, dk, dv, dzu, dzv], axis=1)
    dw_main, dw_main_b = _mm_tn("in_proj_dw", h, dz_main, [F32, BF16], 2048, 1024, 1024)
    (dw_f,) = _mm_tn("in_proj_f_dw", h, dzf, [F32], 2048, LANES, 1024)
    (dh_f,), ((r1_in,),) = _mm_nt("in_proj_f_dx", dzf, w_f, [F32], 1024, 1024, LANES,
                                  jobs=[_job_scatter_sibling_windows(dw_main_b, starts, WW)])
    first_blocks = jnp.stack([jnp.where(c_pos == 0, starts[2 * p], starts[2 * p + 1]) // LANES
                              for p in range(4)]).astype(jnp.int32)
    h_in, hb_in = _rs_add1_windows("rs_add1_w_in", dw_main, r1_in, first_blocks)
    def mix_bwd_fn(dh_main, dh_gate, xb, dres, g):
        dx, dg = _rms_bwd(dh_main + dh_gate, xb, g)
        return (dx + dres,), (dg,)
    tk_in = min(1024, DMAIN)
    ((grad_x,), (dg_mix,)), ((r2_in,),) = _mm_rows(
        "in_proj_dx", (S // TMR, DMAIN // tk_in), dz_main, pl.BlockSpec((TMR, tk_in), lambda i, k: (i, k)),
        w_main, pl.BlockSpec((D, tk_in), lambda i, k: (0, k)), NT, TMR, D, [dh_f, x0, dx1], [norm_mix_g],
        [(D, F32)], [D], mix_bwd_fn, jobs=[_job_scatter_chips(hb_in)])
    g_window = _rs_add2("rs_add2_w_in", h_in, r2_in, p_idx)

    small_shapes = [norm_mix_g.shape, b_f.shape, gmlp_ln_g.shape, gmlp_ln_b.shape, w_s.shape, b_s.shape,
                    attn_out_g.shape, gmlp_out_g.shape, norm_ffn_g.shape, norm_final_g.shape]
    small_parts = [dg_mix, dbf[:, :H], dln_g, dln_b, dw_s, dbs_col, dg_attn, dg_gmlp, dg_ffn, dg_final]
    g_small = _sum8("small_sum", _all_gather("ag_small", _pack(small_parts + [dw_f[:, :H], loss_part])))
    *gs, g_gate, loss_sum = _unpack(g_small, small_shapes + [(D, H), (1, LANES)])
    two_d = lambda a: a.reshape(1, -1) if a.ndim == 1 else a
    ds, nms, nvs = _adamw_many(
        "adamw_small",
        [two_d(a) for a in (norm_mix_g, b_f, gmlp_ln_g, gmlp_ln_b, w_s, b_s, attn_out_g, gmlp_out_g, norm_ffn_g,
                            norm_final_g)],
        [two_d(a) for a in gs],
        [two_d(a) for a in (m_norm_mix_g, m_b_f, m_gmlp_ln_g, m_gmlp_ln_b, m_w_s, m_b_s, m_attn_out_g, m_gmlp_out_g,
                            m_norm_ffn_g, m_norm_final_g)],
        [two_d(a) for a in (v_norm_mix_g, v_b_f, v_gmlp_ln_g, v_gmlp_ln_b, v_w_s, v_b_s, v_attn_out_g, v_gmlp_out_g,
                            v_norm_ffn_g, v_norm_final_g)])
    ds, nms, nvs = [[a.reshape(s) for a, s in zip(lst, small_shapes)] for lst in (ds, nms, nvs)]

    def my_columns(n):
        if n != gate_dev:
            return lambda win, gate: win[:, lo[n] - starts[n]:lo[n] - starts[n] + w_in_cols]
        return lambda win, gate: jnp.concatenate([win[:, g0:g0 + n_before], gate,
                                                  win[:, g0 + n_before:g0 + w_in_cols - H]], axis=1)
    g_w_in = lax.switch(me_idx, [my_columns(n) for n in range(N_DEV)], g_window, g_gate)

    big = {}
    for nm, w, g, m, v in (("w_in", w_in, g_w_in, m_w_in, v_w_in), ("w_out", w_out, g_w_out, m_w_out, v_w_out),
                           ("w_ff1", w_ff1, g_w_ff1, m_w_ff1, v_w_ff1), ("w_ff2", w_ff2, g_w_ff2, m_w_ff2, v_w_ff2)):
        (d_, m_, v_), _ = _adamw("adamw_" + nm, w[0], g, m[0], v[0])
        big[nm] = (g[None], d_[None], m_[None], v_[None])

    loss = loss_sum[0, 0]

    def leaves(n):
        sm = (gs, ds, nms, nvs)[n]
        return [sm[0], big["w_in"][n], sm[1], sm[2], sm[3], sm[4], sm[5], sm[6], sm[7], big["w_out"][n], sm[8],
                big["w_ff1"][n], big["w_ff2"][n], sm[9]]

    return (loss, grad_x[None], *leaves(0), *leaves(1), *leaves(2), *leaves(3))
```

```python
import functools
import math

import jax
import jax.numpy as jnp
from jax import lax
from jax.experimental import pallas as pl
from jax.experimental.pallas import tpu as pltpu

F32 = jnp.float32
BF16 = jnp.bfloat16
MESH = pl.DeviceIdType.MESH

HEAD_DIM = 128
CHUNK = 128
EPS = 1e-6
LANES = 128
N_DEV = 8

ADAM_LR = 0.001
ADAM_B1 = 0.9
ADAM_B2 = 0.999
ADAM_EPS = 1e-08
ADAM_WD = 0.01
ADAM_STEP = 10

VMEM_LIMIT_BYTES = 56 * 1024 * 1024
T_ATT_MAX = 1024
TR_MAX = 256

NN = ((1,), (0,))
NT = ((1,), (1,))
TN = ((0,), (0,))


def _params(sem=None):
    return pltpu.CompilerParams(dimension_semantics=sem, vmem_limit_bytes=VMEM_LIMIT_BYTES)


def _dot(a, b, contract=NN):
    return lax.dot_general(a, b, (contract, ((), ())), preferred_element_type=F32)


def _dot3(x, t):
    x1 = x.astype(BF16)
    r1 = x - x1.astype(F32)
    x2 = r1.astype(BF16)
    x3 = (r1 - x2.astype(F32)).astype(BF16)
    return _dot(x1, t) + _dot(x2, t) + _dot(x3, t)


def _iota2(shape, dim):
    return lax.broadcasted_iota(jnp.int32, shape, dim)


def _row_call(name, fn, row_ins, bcast_ins, row_outs, acc_outs, tr):
    S = row_ins[0].shape[0]
    assert S % tr == 0
    n_ri, n_bi, n_ro, n_ao = len(row_ins), len(bcast_ins), len(row_outs), len(acc_outs)

    def body(*refs):
        ins = [r[...] for r in refs[:n_ri + n_bi]]
        ro_refs = refs[n_ri + n_bi:n_ri + n_bi + n_ro]
        ao_refs = refs[n_ri + n_bi + n_ro:]
        ro, ao = fn(*ins)
        for r, v in zip(ro_refs, ro):
            r[...] = v.astype(r.dtype)
        if n_ao:
            @pl.when(pl.program_id(0) == 0)
            def _():
                for r in ao_refs:
                    r[...] = jnp.zeros_like(r)
            for r, v in zip(ao_refs, ao):
                r[...] += v

    in_specs = [pl.BlockSpec((tr, a.shape[1]), lambda i: (i, 0)) for a in row_ins]
    in_specs += [pl.BlockSpec(a.shape, lambda i: (0, 0)) for a in bcast_ins]
    out_specs = [pl.BlockSpec((tr, d), lambda i: (i, 0)) for d, _ in row_outs]
    out_specs += [pl.BlockSpec((1, d), lambda i: (0, 0)) for d in acc_outs]
    out_shape = [jax.ShapeDtypeStruct((S, d), dt) for d, dt in row_outs]
    out_shape += [jax.ShapeDtypeStruct((1, d), F32) for d in acc_outs]
    outs = pl.pallas_call(
        body, name=name, grid=(S // tr,), in_specs=in_specs, out_specs=out_specs, out_shape=out_shape,
        compiler_params=_params(("arbitrary",) if n_ao else ("parallel",)),
    )(*row_ins, *bcast_ins)
    return outs[:n_ro], outs[n_ro:]


def _rms_fwd(x, g):
    rstd = lax.rsqrt(jnp.mean(x * x, axis=-1, keepdims=True) + EPS)
    return x * rstd * g


def _rms_bwd(dy, x, g):
    rstd = lax.rsqrt(jnp.mean(x * x, axis=-1, keepdims=True) + EPS)
    xhat = x * rstd
    dg = jnp.sum(dy * xhat, axis=0, keepdims=True)
    dxhat = dy * g
    dx = rstd * (dxhat - xhat * jnp.mean(dxhat * xhat, axis=-1, keepdims=True))
    return dx, dg


_GELU_C = math.sqrt(2.0 / math.pi)


def _gelu(x):
    return 0.5 * x * (1.0 + jnp.tanh(_GELU_C * (x + 0.044715 * (x * x * x))))


def _gelu_grad(x):
    t = jnp.tanh(_GELU_C * (x + 0.044715 * (x * x * x)))
    return 0.5 * (1.0 + t) + 0.5 * x * (1.0 - t * t) * (_GELU_C * (1.0 + 3.0 * 0.044715 * (x * x)))


def _me():
    return lax.axis_index("x"), lax.axis_index("y"), lax.axis_index("c")


def _other_chips(x, y):
    return [(1 - x, y), (x, 1 - y), (1 - x, 1 - y)]


_ANY = pl.BlockSpec(memory_space=pl.ANY)


class _Job:
    def __init__(self, ins, outs, n_sems, make, aliases=None):
        self.ins, self.outs, self.n_sems, self.make, self.aliases = ins, outs, n_sems, make, aliases or {}


def _job_gather_chips(blk, part=(0, 1, 1), into=None):
    R, C = blk.shape
    nr = R // part[2]
    rows = pl.ds(part[0] * nr, (part[1] - part[0]) * nr)

    def make(ins, outs, send_sems, recv_sems, base):
        x_ref, (out_ref,) = ins[0], outs
        x, y, c = _me()
        mine = 4 * x + 2 * y + c
        targets = [(x, y, 1 - c)] + [(cx, cy, c) for cx, cy in _other_chips(x, y)]

        def copy(k, slab, to):
            return pltpu.make_async_remote_copy(
                src_ref=x_ref.at[rows, :], dst_ref=out_ref.at[slab, rows, :], send_sem=send_sems.at[base + k],
                recv_sem=recv_sems.at[base + k], device_id=to, device_id_type=MESH)

        starts = [copy(k, mine, to) for k, to in enumerate(targets)]
        arrivals = [copy(k, 4 * tx + 2 * ty + tc, (tx, ty, tc)) for k, (tx, ty, tc) in enumerate(targets)]
        local = [pltpu.make_async_copy(x_ref.at[rows, :], out_ref.at[mine, rows, :], send_sems.at[base + 4])]
        return starts, arrivals, local

    out = jax.ShapeDtypeStruct((N_DEV, R, C), blk.dtype)
    if into is None:
        return _Job([blk], [out], 5, make)
    return _Job([blk, into], [out], 5, make, aliases={1: 0})


def _job_gather_sibling(part):
    def make(ins, outs, send_sems, recv_sems, base):
        (out_ref,) = outs
        x, y, c = _me()

        def copy(k, slab):
            return pltpu.make_async_remote_copy(
                src_ref=out_ref.at[slab], dst_ref=out_ref.at[slab], send_sem=send_sems.at[base + k],
                recv_sem=recv_sems.at[base + k], device_id=(x, y, 1 - c), device_id_type=MESH)

        chips = _other_chips(x, y)
        starts = [copy(k, 4 * cx + 2 * cy + c) for k, (cx, cy) in enumerate(chips)]
        arrivals = [copy(k, 4 * cx + 2 * cy + (1 - c)) for k, (cx, cy) in enumerate(chips)]
        return starts, arrivals, []

    return _Job([part], [jax.ShapeDtypeStruct(part.shape, part.dtype)], 3, make, aliases={0: 0})


def _job_scatter_sibling(gb):
    _, _, R, C = gb.shape

    def make(ins, outs, send_sems, recv_sems, base):
        (g_ref,), (recv_ref,) = ins, outs
        x, y, c = _me()
        copies = [pltpu.make_async_remote_copy(
            src_ref=g_ref.at[p, 1 - c], dst_ref=recv_ref.at[p], send_sem=send_sems.at[base + p],
            recv_sem=recv_sems.at[base + p], device_id=(x, y, 1 - c), device_id_type=MESH) for p in range(4)]
        return copies, copies, []

    return _Job([gb], [jax.ShapeDtypeStruct((4, R, C), gb.dtype)], 4, make)


def _job_scatter_sibling_windows(gb, starts, width):
    R, _ = gb.shape

    def make(ins, outs, send_sems, recv_sems, base):
        (g_ref,), (recv_ref,) = ins, outs
        x, y, c = _me()
        copies = []
        for p in range(4):
            start = pl.multiple_of(jnp.where(c == 0, starts[2 * p + 1], starts[2 * p]), LANES)
            copies.append(pltpu.make_async_remote_copy(
                src_ref=g_ref.at[:, pl.ds(start, width)], dst_ref=recv_ref.at[p], send_sem=send_sems.at[base + p],
                recv_sem=recv_sems.at[base + p], device_id=(x, y, 1 - c), device_id_type=MESH))
        return copies, copies, []

    return _Job([gb], [jax.ShapeDtypeStruct((4, R, width), gb.dtype)], 4, make)


def _job_scatter_chips(hb):
    _, R, C = hb.shape

    def make(ins, outs, send_sems, recv_sems, base):
        (h_ref,), (recv_ref,) = ins, outs
        x, y, c = _me()
        copies = [pltpu.make_async_remote_copy(
            src_ref=h_ref.at[2 * cx + cy], dst_ref=recv_ref.at[n], send_sem=send_sems.at[base + n],
            recv_sem=recv_sems.at[base + n], device_id=(cx, cy, c), device_id_type=MESH)
            for n, (cx, cy) in enumerate(_other_chips(x, y))]
        return copies, copies, []

    return _Job([hb], [jax.ShapeDtypeStruct((3, R, C), hb.dtype)], 3, make)


def _carry_call(body, *, name, grid, in_specs, out_specs, out_shape, scratch_shapes, semantics, args, jobs=()):
    jobs = list(jobs)
    n_in, n_out, n_scr = len(in_specs), len(out_specs), len(scratch_shapes)
    j_ins = [a for j in jobs for a in j.ins]
    j_outs = [o for j in jobs for o in j.outs]
    n_sems = sum(j.n_sems for j in jobs)
    aliases = {}
    i0, o0 = n_in, n_out
    for j in jobs:
        for a, b in j.aliases.items():
            aliases[i0 + a] = o0 + b
        i0 += len(j.ins)
        o0 += len(j.outs)

    def full_body(*refs):
        ins = refs[:n_in]
        jin = refs[n_in:n_in + len(j_ins)]
        outs = refs[n_in + len(j_ins):n_in + len(j_ins) + n_out]
        jout = refs[n_in + len(j_ins) + n_out:n_in + len(j_ins) + n_out + len(j_outs)]
        scr = refs[n_in + len(j_ins) + n_out + len(j_outs):]
        if jobs:
            send_sems, recv_sems = scr[n_scr], scr[n_scr + 1]
            starts, arrivals, local = [], [], []
            base = i0 = o0 = 0
            for j in jobs:
                s, a, l = j.make(jin[i0:i0 + len(j.ins)], jout[o0:o0 + len(j.outs)], send_sems, recv_sems, base)
                starts += s
                arrivals += a
                local += l
                base += j.n_sems
                i0 += len(j.ins)
                o0 += len(j.outs)
            pids = [pl.program_id(d) for d in range(len(grid))]
            first = functools.reduce(jnp.logical_and, [p == 0 for p in pids])
            last = functools.reduce(jnp.logical_and, [p == n - 1 for p, n in zip(pids, grid)])

            @pl.when(first)
            def _():
                for cp in local + starts:
                    cp.start()

        body(*ins, *outs, *scr[:n_scr])

        if jobs:
            @pl.when(last)
            def _():
                for cp in arrivals:
                    cp.wait_recv()
                for cp in starts:
                    cp.wait_send()
                for cp in local:
                    cp.wait()

    sems = [pltpu.SemaphoreType.DMA((n_sems,)), pltpu.SemaphoreType.DMA((n_sems,))] if jobs else []
    res = pl.pallas_call(
        full_body, name=name, grid=grid,
        in_specs=list(in_specs) + [_ANY] * len(j_ins),
        out_specs=list(out_specs) + [_ANY] * len(j_outs),
        out_shape=list(out_shape) + j_outs,
        scratch_shapes=list(scratch_shapes) + sems,
        input_output_aliases=aliases,
        compiler_params=_params(("arbitrary",) * len(grid) if jobs else semantics),
    )(*args, *j_ins)
    body_res, job_res = res[:n_out], res[n_out:]
    per_job = []
    for j in jobs:
        per_job.append(job_res[:len(j.outs)])
        job_res = job_res[len(j.outs):]
    return body_res, per_job


def _run_jobs(name, jobs):
    def body(done_ref):
        done_ref[...] = jnp.zeros_like(done_ref)

    return _carry_call(body, name=name, grid=(1,), in_specs=[], out_specs=[pl.BlockSpec((8, LANES), lambda i: (0, 0))],
                       out_shape=[jax.ShapeDtypeStruct((8, LANES), F32)], scratch_shapes=[], semantics=("arbitrary",),
                       args=[], jobs=jobs)[1]


def _mm(name, grid, a, a_spec, b, b_spec, contract, acc_shape, out_shape, out_specs, extras=(), epilogue=None, jobs=()):
    nk = grid[2]
    n_e = len(extras)
    n_o = len(out_shape)
    if epilogue is None:
        epilogue = lambda acc: (acc,)

    def body(a_ref, b_ref, *rest):
        e_refs = rest[:n_e]
        o_refs = rest[n_e:n_e + n_o]

        def finish(total):
            res = epilogue(total, *[r[...] for r in e_refs])
            for o, r in zip(o_refs, res):
                o[...] = r.astype(o.dtype)

        if nk == 1:
            finish(_dot(a_ref[...], b_ref[...], contract))
            return
        acc = rest[n_e + n_o]
        k = pl.program_id(2)

        @pl.when(k == 0)
        def _():
            acc[...] = _dot(a_ref[...], b_ref[...], contract)

        @pl.when(jnp.logical_and(k > 0, k < nk - 1))
        def _():
            acc[...] += _dot(a_ref[...], b_ref[...], contract)

        @pl.when(k == nk - 1)
        def _():
            finish(acc[...] + _dot(a_ref[...], b_ref[...], contract))

    outs, job_res = _carry_call(
        body, name=name, grid=grid, in_specs=[a_spec, b_spec] + [s for _, s in extras],
        out_specs=list(out_specs), out_shape=list(out_shape),
        scratch_shapes=[pltpu.VMEM(acc_shape, F32)] if nk > 1 else [],
        semantics=("parallel", "parallel", "arbitrary"), args=[a, b] + [e for e, _ in extras], jobs=jobs)
    return (outs, job_res) if jobs else outs


def _mm_rows(name, grid, a, a_spec, b, b_spec, contract, tm, n, row_extras, bcast, row_outs, acc_outs, epilogue, jobs=()):
    nk = grid[1]
    M = grid[0] * tm
    n_x, n_b, n_ro, n_ao = len(row_extras), len(bcast), len(row_outs), len(acc_outs)

    def body(a_ref, b_ref, *rest):
        x_refs = rest[:n_x + n_b]
        ro_refs = rest[n_x + n_b:n_x + n_b + n_ro]
        ao_refs = rest[n_x + n_b + n_ro:n_x + n_b + n_ro + n_ao]
        i = pl.program_id(0)

        def finish(total):
            ro, ao = epilogue(total, *[r[...] for r in x_refs])
            for r, v in zip(ro_refs, ro):
                r[...] = v.astype(r.dtype)
            if n_ao:
                @pl.when(i == 0)
                def _():
                    for r, v in zip(ao_refs, ao):
                        r[...] = v

                @pl.when(i > 0)
                def _():
                    for r, v in zip(ao_refs, ao):
                        r[...] += v

        if nk == 1:
            finish(_dot(a_ref[...], b_ref[...], contract))
            return
        acc = rest[n_x + n_b + n_ro + n_ao]
        k = pl.program_id(1)

        @pl.when(k == 0)
        def _():
            acc[...] = _dot(a_ref[...], b_ref[...], contract)

        @pl.when(jnp.logical_and(k > 0, k < nk - 1))
        def _():
            acc[...] += _dot(a_ref[...], b_ref[...], contract)

        @pl.when(k == nk - 1)
        def _():
            finish(acc[...] + _dot(a_ref[...], b_ref[...], contract))

    in_specs = [a_spec, b_spec] + [pl.BlockSpec((tm, x.shape[1]), lambda i, k: (i, 0)) for x in row_extras]
    in_specs += [pl.BlockSpec(x.shape, lambda i, k: (0,) * x.ndim) for x in bcast]
    out_specs = [pl.BlockSpec((tm, w), lambda i, k: (i, 0)) for w, _ in row_outs]
    out_specs += [pl.BlockSpec((1, w), lambda i, k: (0, 0)) for w in acc_outs]
    out_shape = [jax.ShapeDtypeStruct((M, w), dt) for w, dt in row_outs]
    out_shape += [jax.ShapeDtypeStruct((1, w), F32) for w in acc_outs]
    outs, job_res = _carry_call(
        body, name=name, grid=grid, in_specs=in_specs, out_specs=out_specs, out_shape=out_shape,
        scratch_shapes=[pltpu.VMEM((tm, n), F32)] if nk > 1 else [],
        semantics=("arbitrary", "arbitrary"), args=[a, b] + list(row_extras) + list(bcast), jobs=jobs)
    res = (outs[:n_ro], outs[n_ro:])
    return (res, job_res) if jobs else res


def _mm_nn(name, a, b, out_dtypes, tm, tn, tk, extras=(), epilogue=None, jobs=()):
    M, K = a.shape
    N = b.shape[1]
    tm, tn, tk = min(tm, M), min(tn, N), min(tk, K)
    o_spec = pl.BlockSpec((tm, tn), lambda i, j, k: (i, j))
    return _mm(name, (M // tm, N // tn, K // tk),
               a, pl.BlockSpec((tm, tk), lambda i, j, k: (i, k)),
               b, pl.BlockSpec((tk, tn), lambda i, j, k: (k, j)), NN, (tm, tn),
               [jax.ShapeDtypeStruct((M, N), dt) for dt in out_dtypes], [o_spec] * len(out_dtypes),
               [(e, o_spec) for e in extras], epilogue, jobs)


def _mm_nt(name, a, b, out_dtypes, tm, tn, tk, extras=(), epilogue=None, jobs=()):
    M, K = a.shape
    N = b.shape[0]
    tm, tn, tk = min(tm, M), min(tn, N), min(tk, K)
    o_spec = pl.BlockSpec((tm, tn), lambda i, j, k: (i, j))
    return _mm(name, (M // tm, N // tn, K // tk),
               a, pl.BlockSpec((tm, tk), lambda i, j, k: (i, k)),
               b, pl.BlockSpec((tn, tk), lambda i, j, k: (j, k)), NT, (tm, tn),
               [jax.ShapeDtypeStruct((M, N), dt) for dt in out_dtypes], [o_spec] * len(out_dtypes),
               [(e, o_spec) for e in extras], epilogue, jobs)


def _mm_tn(name, a, b, out_dtypes, tm, tn, tk, jobs=()):
    K, M = a.shape
    N = b.shape[1]
    tm, tn, tk = min(tm, M), min(tn, N), min(tk, K)
    o_spec = pl.BlockSpec((tm, tn), lambda i, j, k: (i, j))
    return _mm(name, (M // tm, N // tn, K // tk),
               a, pl.BlockSpec((tk, tm), lambda i, j, k: (k, i)),
               b, pl.BlockSpec((tk, tn), lambda i, j, k: (k, j)), TN, (tm, tn),
               [jax.ShapeDtypeStruct((M, N), dt) for dt in out_dtypes], [o_spec] * len(out_dtypes),
               epilogue=lambda acc: (acc,) * len(out_dtypes), jobs=jobs)


def _fgate_fwd(zf, bf):
    S = zf.shape[0]
    nc = S // CHUNK

    def body(zf_ref, bf_ref, f_ref):
        upper = (_iota2((CHUNK, CHUNK), 0) <= _iota2((CHUNK, CHUNK), 1)).astype(BF16)
        carry = jnp.zeros((8, 1), F32)
        for c in range(nc):
            t = zf_ref[c * CHUNK:(c + 1) * CHUNK, :] + bf_ref[...]
            lf = jnp.minimum(t, 0.0) - jnp.log(1.0 + jnp.exp(-jnp.abs(t)))
            lf_rows = lf.T[0:8, :]
            f_ref[:, c * CHUNK:(c + 1) * CHUNK] = (_dot3(lf_rows, upper) + carry) * LOG2E
            carry = carry + jnp.sum(lf_rows, axis=-1, keepdims=True)

    return pl.pallas_call(
        body, name="fgate_fwd", out_shape=jax.ShapeDtypeStruct((8, S), F32),
        compiler_params=_params(),
    )(zf, bf)


def _fgate_bwd(df, zf, bf):
    S = zf.shape[0]
    nc = S // CHUNK

    def body(df_ref, zf_ref, bf_ref, dzf_ref, dbf_ref):
        lower = (_iota2((CHUNK, CHUNK), 0) >= _iota2((CHUNK, CHUNK), 1)).astype(BF16)
        carry = jnp.zeros((8, 1), F32)
        dbf = jnp.zeros((1, LANES), F32)
        for c in reversed(range(nc)):
            sl = slice(c * CHUNK, (c + 1) * CHUNK)
            df = df_ref[:, sl]
            r = _dot3(df, lower) + carry
            carry = carry + jnp.sum(df, axis=-1, keepdims=True)
            r_cols = jnp.concatenate([r, jnp.zeros((CHUNK - 8, CHUNK), F32)], axis=0).T
            t = zf_ref[sl, :] + bf_ref[...]
            dz = r_cols * (1.0 / (1.0 + jnp.exp(t)))
            dzf_ref[sl, :] = dz.astype(BF16)
            dbf = dbf + jnp.sum(dz, axis=0, keepdims=True)
        dbf_ref[...] = dbf

    return pl.pallas_call(
        body, name="fgate_bwd",
        out_shape=[jax.ShapeDtypeStruct((S, LANES), BF16), jax.ShapeDtypeStruct((1, LANES), F32)],
        compiler_params=_params(),
    )(df, zf, bf)


_NEG = -1e30
LOG2E = 1.4426950408889634
N_SPLIT = 8
N_SPLIT_DIAG = 4
DIAG_STEP = 1024


def _attn_consts(T):
    rows, cols = _iota2((T, T), 0), _iota2((T, T), 1)
    return cols <= rows, rows <= cols


def _attn2_fwd(zm, f2col, f2row, T, jobs=()):
    S = zm.shape[0]
    H = f2col.shape[0]
    nb = S // T
    c2 = LOG2E / math.sqrt(HEAD_DIM)

    def body(q_ref, k_ref, v_ref, fq_ref, fk_ref, o_ref, lse_ref, vaug_s):
        i = pl.program_id(1)

        @pl.when(i == 0)
        def _():
            vaug_s[:, :HEAD_DIM] = v_ref[...]
            vaug_s[:, HEAD_DIM:] = jnp.ones((S, HEAD_DIM), BF16)

        keep = _attn_consts(T)[0]
        TH = T // N_SPLIT

        def block(j, diagonal, state):
            r0 = pl.multiple_of(j * T, T)
            fk = fk_ref[j]
            new = []
            for g, (m_old, acc) in enumerate(state):
                rows = slice(g * TH, (g + 1) * TH)
                nk = min(T, -(-(g + 1) * TH // DIAG_STEP) * DIAG_STEP) if diagonal else T
                s = _dot(q_ref[rows, :], k_ref[pl.ds(r0, nk), :], NT) * c2 + (fq_ref[rows, :] - fk[:, :nk])
                if diagonal:
                    s = jnp.where(keep[rows, :nk], s, _NEG)
                m_new = jnp.maximum(m_old, jnp.max(s, axis=-1, keepdims=True))
                p = jnp.exp2(s - m_new).astype(BF16)
                new.append((m_new, jnp.exp2(m_old - m_new) * acc + _dot(p, vaug_s[pl.ds(r0, nk), :])))
            return tuple(new)

        init = tuple((jnp.full((TH, 1), _NEG, F32), jnp.zeros((TH, 2 * HEAD_DIM), F32)) for _ in range(N_SPLIT))
        state = lax.fori_loop(0, i, lambda j, st: block(j, False, st), init)
        state = block(i, True, state)
        for g, (m, acc) in enumerate(state):
            rows = slice(g * TH, (g + 1) * TH)
            o_ref[rows, :] = acc[:, :HEAD_DIM] / acc[:, HEAD_DIM:]
            lse_ref[rows, :] = m + jnp.log2(acc[:, HEAD_DIM:HEAD_DIM + 1])

    nh = H
    return _carry_call(
        body, name="attn_fwd", grid=(H, nb), jobs=jobs, args=[zm, zm, zm, f2col, f2row],
        semantics=("arbitrary", "arbitrary"),
        in_specs=[
            pl.BlockSpec((T, HEAD_DIM), lambda h, i: (i, h)),
            pl.BlockSpec((S, HEAD_DIM), lambda h, i: (0, nh + h)),
            pl.BlockSpec((S, HEAD_DIM), lambda h, i: (0, 2 * nh + h)),
            pl.BlockSpec((None, T, 1), lambda h, i: (h, i, 0)),
            pl.BlockSpec((None, nb, 1, T), lambda h, i: (h, 0, 0, 0)),
        ],
        out_specs=[pl.BlockSpec((T, HEAD_DIM), lambda h, i: (i, h)), pl.BlockSpec((None, T, 1), lambda h, i: (h, i, 0))],
        out_shape=[jax.ShapeDtypeStruct((S, H * HEAD_DIM), F32), jax.ShapeDtypeStruct((H, S, 1), F32)],
        scratch_shapes=[pltpu.VMEM((S, 2 * HEAD_DIM), BF16)],
    )


def _attn2_bwd_dq(zm, dattn, f2col, f2row, lse2_col, delta_col, T, jobs=()):
    S = zm.shape[0]
    H = f2col.shape[0]
    nb = S // T
    scale = 1.0 / math.sqrt(HEAD_DIM)
    c2 = LOG2E * scale

    def body(q_ref, k_ref, v_ref, do_ref, fq_ref, fk_ref, lse_ref, dl_ref, dq_ref, rs_ref, bias_s, do_s):
        i = pl.program_id(1)
        keep = _attn_consts(T)[0]
        TH = T // N_SPLIT_DIAG
        bias_s[...] = fq_ref[...] - lse_ref[...]
        do_s[...] = do_ref[...].astype(BF16)

        def part(rows, j, nk, state, masked):
            acc, rs = state
            r0 = pl.multiple_of(j * T, T)
            kb = k_ref[pl.ds(r0, nk), :]
            s = _dot(q_ref[rows, :], kb, NT) * c2 + (bias_s[rows, :] - fk_ref[j][:, :nk])
            if masked:
                s = jnp.where(keep[rows, :nk], s, _NEG)
            ds = jnp.exp2(s) * (_dot(do_s[rows, :], v_ref[pl.ds(r0, nk), :], NT) - dl_ref[rows, :])
            return acc + _dot(ds.astype(BF16), kb), rs + jnp.sum(ds, axis=-1, keepdims=True)

        def step(j, state):
            return part(slice(0, T), j, T, state, False)

        acc, rs = lax.fori_loop(0, i, step, (jnp.zeros((T, HEAD_DIM), F32), jnp.zeros((T, 1), F32)))
        for g in range(N_SPLIT_DIAG):
            rows = slice(g * TH, (g + 1) * TH)
            acc_g, rs_g = part(rows, i, (g + 1) * TH, (acc[rows, :], rs[rows, :]), True)
            dq_ref[rows, :] = (acc_g * scale).astype(BF16)
            rs_ref[rows, :] = rs_g

    nh = H
    col = pl.BlockSpec((None, T, 1), lambda h, i: (h, i, 0))
    blk = pl.BlockSpec((T, HEAD_DIM), lambda h, i: (i, h))
    return _carry_call(
        body, name="attn_bwd_dq", grid=(H, nb), jobs=jobs,
        args=[zm, zm, zm, dattn, f2col, f2row, lse2_col, delta_col], semantics=("arbitrary", "arbitrary"),
        in_specs=[
            blk,
            pl.BlockSpec((S, HEAD_DIM), lambda h, i: (0, nh + h)),
            pl.BlockSpec((S, HEAD_DIM), lambda h, i: (0, 2 * nh + h)),
            blk, col,
            pl.BlockSpec((None, nb, 1, T), lambda h, i: (h, 0, 0, 0)),
            col, col,
        ],
        out_specs=[blk, col],
        out_shape=[jax.ShapeDtypeStruct((S, H * HEAD_DIM), BF16), jax.ShapeDtypeStruct((H, S, 1), F32)],
        scratch_shapes=[pltpu.VMEM((T, 1), F32), pltpu.VMEM((T, HEAD_DIM), BF16)],
    )


def _attn2_bwd_dkv(zm, dattn, f2col, f2row, lse2_row, delta_row, rowsum_row, T, jobs=()):
    S = zm.shape[0]
    H = f2col.shape[0]
    nb = S // T
    scale = 1.0 / math.sqrt(HEAD_DIM)
    c2 = LOG2E * scale

    def body(q_ref, k_ref, v_ref, do_ref, fk_ref, fq_ref, lse_ref, dl_ref, rs_ref, dk_ref, dv_ref, df_ref):
        j = pl.program_id(1)
        keep = _attn_consts(T)[1]
        TH = T // N_SPLIT_DIAG

        def part(rows, i, c0, state, masked):
            dk, dv, df = state
            r0 = pl.multiple_of(i * T + c0, TH)
            qb = q_ref[pl.ds(r0, T - c0), :]
            do = do_ref[pl.ds(r0, T - c0), :].astype(BF16)
            bias = (fq_ref[i] - lse_ref[i])[:, c0:]
            dl = (dl_ref[i] + rs_ref[i])[:, c0:]
            st = _dot(k_ref[rows, :], qb, NT) * c2 + (bias - fk_ref[rows, :])
            if masked:
                st = jnp.where(keep[rows, c0:], st, _NEG)
            pt = jnp.exp2(st)
            dst = pt * (_dot(v_ref[rows, :], do, NT) - dl)
            return (dk + _dot(dst.astype(BF16), qb), dv + _dot(pt.astype(BF16), do),
                    df - jnp.sum(dst, axis=-1, keepdims=True))

        groups = []
        for g in range(N_SPLIT_DIAG):
            zero = (jnp.zeros((TH, HEAD_DIM), F32), jnp.zeros((TH, HEAD_DIM), F32), jnp.zeros((TH, 1), F32))
            groups.append(part(slice(g * TH, (g + 1) * TH), j, g * TH, zero, True))
        state = tuple(jnp.concatenate([grp[n] for grp in groups], axis=0) for n in range(3))
        dk, dv, df = lax.fori_loop(j + 1, nb, lambda i, st: part(slice(0, T), i, 0, st, False), state)
        dk_ref[...] = (dk * scale).astype(BF16)
        dv_ref[...] = dv.astype(BF16)
        df_ref[...] = df

    nh = H
    row = pl.BlockSpec((None, nb, 1, T), lambda h, j: (h, 0, 0, 0))
    whole = pl.BlockSpec((S, HEAD_DIM), lambda h, j: (0, h))
    kv_out = pl.BlockSpec((T, HEAD_DIM), lambda h, j: (j, h))
    col = pl.BlockSpec((None, T, 1), lambda h, j: (h, j, 0))
    return _carry_call(
        body, name="attn_bwd_dkv", grid=(H, nb), jobs=jobs,
        args=[zm, zm, zm, dattn, f2col, f2row, lse2_row, delta_row, rowsum_row],
        semantics=("arbitrary", "arbitrary"),
        in_specs=[
            whole,
            pl.BlockSpec((T, HEAD_DIM), lambda h, j: (j, nh + h)),
            pl.BlockSpec((T, HEAD_DIM), lambda h, j: (j, 2 * nh + h)),
            whole, col, row, row, row, row,
        ],
        out_specs=[kv_out, kv_out, col],
        out_shape=[jax.ShapeDtypeStruct((S, H * HEAD_DIM), BF16), jax.ShapeDtypeStruct((S, H * HEAD_DIM), BF16),
                   jax.ShapeDtypeStruct((H, S, 1), F32)],
        scratch_shapes=[],
    )


def _attn_fwd(zm, fcol, frow, T, jobs=()):
    S = zm.shape[0]
    H = fcol.shape[0]
    nb = S // T
    scale = 1.0 / math.sqrt(HEAD_DIM)

    def body(q_ref, k_ref, v_ref, fq_ref, fk_ref, o_ref, lse_ref, m_s, l_s, acc_s):
        i = pl.program_id(1)
        j = pl.program_id(2)

        @pl.when(j == 0)
        def _():
            m_s[...] = jnp.full_like(m_s, _NEG)
            l_s[...] = jnp.zeros_like(l_s)
            acc_s[...] = jnp.zeros_like(acc_s)

        @pl.when(j <= i)
        def _():
            s = _dot(q_ref[...], k_ref[...], NT) * scale + (fq_ref[...] - fk_ref[...])
            keep = (_iota2((T, T), 1) + j * T) <= (_iota2((T, T), 0) + i * T)
            s = jnp.where(keep, s, _NEG)
            m_new = jnp.maximum(m_s[...], jnp.max(s, axis=-1, keepdims=True))
            alpha = jnp.exp(m_s[...] - m_new)
            p = jnp.exp(s - m_new)
            l_s[...] = alpha * l_s[...] + jnp.sum(p, axis=-1, keepdims=True)
            acc_s[...] = alpha * acc_s[...] + _dot(p.astype(BF16), v_ref[...])
            m_s[...] = m_new

        @pl.when(j == nb - 1)
        def _():
            o_ref[...] = acc_s[...] / l_s[...]
            lse_ref[...] = m_s[...] + jnp.log(l_s[...])

    nh = H
    return _carry_call(
        body, name="attn_fwd", grid=(H, nb, nb), jobs=jobs, args=[zm, zm, zm, fcol, frow],
        semantics=("parallel", "parallel", "arbitrary"),
        in_specs=[
            pl.BlockSpec((T, HEAD_DIM), lambda h, i, j: (i, h)),
            pl.BlockSpec((T, HEAD_DIM), lambda h, i, j: (jnp.minimum(j, i), nh + h)),
            pl.BlockSpec((T, HEAD_DIM), lambda h, i, j: (jnp.minimum(j, i), 2 * nh + h)),
            pl.BlockSpec((None, T, 1), lambda h, i, j: (h, i, 0)),
            pl.BlockSpec((None, 1, T), lambda h, i, j: (h, 0, jnp.minimum(j, i))),
        ],
        out_specs=[
            pl.BlockSpec((T, HEAD_DIM), lambda h, i, j: (i, h)),
            pl.BlockSpec((None, T, 1), lambda h, i, j: (h, i, 0)),
        ],
        out_shape=[jax.ShapeDtypeStruct((S, H * HEAD_DIM), F32), jax.ShapeDtypeStruct((H, S, 1), F32)],
        scratch_shapes=[pltpu.VMEM((T, 1), F32), pltpu.VMEM((T, 1), F32), pltpu.VMEM((T, HEAD_DIM), F32)],
    )


def _attn_delta(dattn, attn, tr):
    S, DA = attn.shape
    H = DA // HEAD_DIM

    def body(do_ref, o_ref, out_ref):
        lo = _iota2((DA, LANES), 1) * HEAD_DIM
        sel = ((_iota2((DA, LANES), 0) >= lo) & (_iota2((DA, LANES), 0) < lo + HEAD_DIM)).astype(BF16)
        d = _dot3(do_ref[...] * o_ref[...], sel)
        for c in range(tr // CHUNK):
            out_ref[:, c * CHUNK:(c + 1) * CHUNK] = d[c * CHUNK:(c + 1) * CHUNK, :].T[0:H, :]

    return pl.pallas_call(
        body, name="attn_delta", grid=(S // tr,),
        in_specs=[pl.BlockSpec((tr, DA), lambda i: (i, 0))] * 2,
        out_specs=pl.BlockSpec((H, tr), lambda i: (0, i)),
        out_shape=jax.ShapeDtypeStruct((H, S), F32),
        compiler_params=_params(("parallel",)),
    )(dattn, attn)


def _attn_bwd_dq(zm, dattn, fcol, frow, lse_col, delta_col, T, jobs=()):
    S = zm.shape[0]
    H = fcol.shape[0]
    nb = S // T
    scale = 1.0 / math.sqrt(HEAD_DIM)

    def body(q_ref, k_ref, v_ref, do_ref, fq_ref, fk_ref, lse_ref, dl_ref, dq_ref, rs_ref, acc_s, rs_s):
        i = pl.program_id(1)
        j = pl.program_id(2)

        @pl.when(j == 0)
        def _():
            acc_s[...] = jnp.zeros_like(acc_s)
            rs_s[...] = jnp.zeros_like(rs_s)

        @pl.when(j <= i)
        def _():
            s = _dot(q_ref[...], k_ref[...], NT) * scale + (fq_ref[...] - fk_ref[...])
            keep = (_iota2((T, T), 1) + j * T) <= (_iota2((T, T), 0) + i * T)
            p = jnp.exp(jnp.where(keep, s - lse_ref[...], _NEG))
            dp = _dot(do_ref[...].astype(BF16), v_ref[...], NT)
            ds = p * (dp - dl_ref[...])
            acc_s[...] += _dot(ds.astype(BF16), k_ref[...])
            rs_s[...] += jnp.sum(ds, axis=-1, keepdims=True)

        @pl.when(j == nb - 1)
        def _():
            dq_ref[...] = (acc_s[...] * scale).astype(BF16)
            rs_ref[...] = rs_s[...]

    nh = H
    col = pl.BlockSpec((None, T, 1), lambda h, i, j: (h, i, 0))
    return _carry_call(
        body, name="attn_bwd_dq", grid=(H, nb, nb), jobs=jobs,
        args=[zm, zm, zm, dattn, fcol, frow, lse_col, delta_col], semantics=("parallel", "parallel", "arbitrary"),
        in_specs=[
            pl.BlockSpec((T, HEAD_DIM), lambda h, i, j: (i, h)),
            pl.BlockSpec((T, HEAD_DIM), lambda h, i, j: (jnp.minimum(j, i), nh + h)),
            pl.BlockSpec((T, HEAD_DIM), lambda h, i, j: (jnp.minimum(j, i), 2 * nh + h)),
            pl.BlockSpec((T, HEAD_DIM), lambda h, i, j: (i, h)),
            col,
            pl.BlockSpec((None, 1, T), lambda h, i, j: (h, 0, jnp.minimum(j, i))),
            col, col,
        ],
        out_specs=[pl.BlockSpec((T, HEAD_DIM), lambda h, i, j: (i, h)), col],
        out_shape=[jax.ShapeDtypeStruct((S, H * HEAD_DIM), BF16), jax.ShapeDtypeStruct((H, S, 1), F32)],
        scratch_shapes=[pltpu.VMEM((T, HEAD_DIM), F32), pltpu.VMEM((T, 1), F32)],
    )


def _attn_bwd_dkv(zm, dattn, fcol, frow, lse_row, delta_row, rowsum_row, T, jobs=()):
    S = zm.shape[0]
    H = fcol.shape[0]
    nb = S // T
    scale = 1.0 / math.sqrt(HEAD_DIM)

    def body(q_ref, k_ref, v_ref, do_ref, fk_ref, fq_ref, lse_ref, dl_ref, rs_ref,
             dk_ref, dv_ref, df_ref, dk_s, dv_s, df_s):
        j = pl.program_id(1)
        i = pl.program_id(2)

        @pl.when(i == 0)
        def _():
            dk_s[...] = jnp.zeros_like(dk_s)
            dv_s[...] = jnp.zeros_like(dv_s)
            df_s[...] = jnp.zeros_like(df_s)

        @pl.when(i >= j)
        def _():
            st = _dot(k_ref[...], q_ref[...], NT) * scale + (fq_ref[...] - fk_ref[...])
            keep = (_iota2((T, T), 0) + j * T) <= (_iota2((T, T), 1) + i * T)
            pt = jnp.exp(jnp.where(keep, st - lse_ref[...], _NEG))
            do = do_ref[...].astype(BF16)
            dpt = _dot(v_ref[...], do, NT)
            dst = pt * (dpt - (dl_ref[...] + rs_ref[...]))
            dv_s[...] += _dot(pt.astype(BF16), do)
            dk_s[...] += _dot(dst.astype(BF16), q_ref[...])
            df_s[...] -= jnp.sum(dst, axis=-1, keepdims=True)

        @pl.when(i == nb - 1)
        def _():
            dk_ref[...] = (dk_s[...] * scale).astype(BF16)
            dv_ref[...] = dv_s[...].astype(BF16)
            df_ref[...] = df_s[...]

    nh = H
    row = pl.BlockSpec((None, 1, T), lambda h, j, i: (h, 0, jnp.maximum(i, j)))
    kv_out = pl.BlockSpec((T, HEAD_DIM), lambda h, j, i: (j, h))
    return _carry_call(
        body, name="attn_bwd_dkv", grid=(H, nb, nb), jobs=jobs,
        args=[zm, zm, zm, dattn, fcol, frow, lse_row, delta_row, rowsum_row],
        semantics=("parallel", "parallel", "arbitrary"),
        in_specs=[
            pl.BlockSpec((T, HEAD_DIM), lambda h, j, i: (jnp.maximum(i, j), h)),
            pl.BlockSpec((T, HEAD_DIM), lambda h, j, i: (j, nh + h)),
            pl.BlockSpec((T, HEAD_DIM), lambda h, j, i: (j, 2 * nh + h)),
            pl.BlockSpec((T, HEAD_DIM), lambda h, j, i: (jnp.maximum(i, j), h)),
            pl.BlockSpec((None, T, 1), lambda h, j, i: (h, j, 0)),
            row, row, row, row,
        ],
        out_specs=[kv_out, kv_out, pl.BlockSpec((None, T, 1), lambda h, j, i: (h, j, 0))],
        out_shape=[jax.ShapeDtypeStruct((S, H * HEAD_DIM), BF16), jax.ShapeDtypeStruct((S, H * HEAD_DIM), BF16),
                   jax.ShapeDtypeStruct((H, S, 1), F32)],
        scratch_shapes=[pltpu.VMEM((T, HEAD_DIM), F32), pltpu.VMEM((T, HEAD_DIM), F32), pltpu.VMEM((T, 1), F32)],
    )


def _ln_stats(x):
    mu = jnp.mean(x, axis=-1, keepdims=True)
    xc = x - mu
    rstd = lax.rsqrt(jnp.mean(xc * xc, axis=-1, keepdims=True) + EPS)
    return xc * rstd, rstd


def _tril_mask():
    return _iota2((CHUNK, CHUNK), 0) >= _iota2((CHUNK, CHUNK), 1)


def _gmlp_fwd(zm, ln_g, ln_b, w_s, bs_col, tr):
    S = zm.shape[0]
    H = w_s.shape[0]
    DG = H * HEAD_DIM

    def body(zu_ref, zv_ref, g_ref, b_ref, w_ref, bs_ref, out_ref):
        u = _gelu(zu_ref[...].astype(F32))
        y, _ = _ln_stats(_gelu(zv_ref[...].astype(F32)))
        v = (y * g_ref[...] + b_ref[...]).astype(BF16)
        mask = _tril_mask()
        for h in range(H):
            wc = jnp.where(mask, w_ref[h], 0.0).astype(BF16)
            cs = slice(h * HEAD_DIM, (h + 1) * HEAD_DIM)
            for c in range(tr // CHUNK):
                rs = slice(c * CHUNK, (c + 1) * CHUNK)
                mix = _dot(wc, v[rs, cs]) + bs_ref[h]
                out_ref[rs, cs] = u[rs, cs] * mix

    full = lambda a: pl.BlockSpec(a.shape, lambda i: (0,) * a.ndim)
    return pl.pallas_call(
        body, name="gmlp_fwd", grid=(S // tr,),
        in_specs=[pl.BlockSpec((tr, DG), lambda i: (i, 3)), pl.BlockSpec((tr, DG), lambda i: (i, 4)),
                  full(ln_g), full(ln_b), full(w_s), full(bs_col)],
        out_specs=pl.BlockSpec((tr, DG), lambda i: (i, 0)),
        out_shape=jax.ShapeDtypeStruct((S, DG), F32),
        compiler_params=_params(("parallel",)),
    )(zm, zm, ln_g, ln_b, w_s, bs_col)


def _gmlp_bwd(dgm, zm, ln_g, ln_b, w_s, w_st, bs_col, tr):
    S = zm.shape[0]
    H = w_s.shape[0]
    DG = H * HEAD_DIM

    def body(dg_ref, zu_ref, zv_ref, g_ref, b_ref, w_ref, wt_ref, bs_ref,
             dzu_ref, dzv_ref, dw_ref, dbs_ref, dlg_ref, dlb_ref, dv_s):
        @pl.when(pl.program_id(0) == 0)
        def _():
            dw_ref[...] = jnp.zeros_like(dw_ref)
            dbs_ref[...] = jnp.zeros_like(dbs_ref)
            dlg_ref[...] = jnp.zeros_like(dlg_ref)
            dlb_ref[...] = jnp.zeros_like(dlb_ref)

        zu = zu_ref[...].astype(F32)
        zv = zv_ref[...].astype(F32)
        u = _gelu(zu)
        y, rstd = _ln_stats(_gelu(zv))
        v = (y * g_ref[...] + b_ref[...]).astype(BF16)
        dgm_blk = dg_ref[...]
        mask = _tril_mask()
        mask_t = _iota2((CHUNK, CHUNK), 0) <= _iota2((CHUNK, CHUNK), 1)
        for h in range(H):
            wc = jnp.where(mask, w_ref[h], 0.0).astype(BF16)
            wct = jnp.where(mask_t, wt_ref[h], 0.0).astype(BF16)
            cs = slice(h * HEAD_DIM, (h + 1) * HEAD_DIM)
            dw = jnp.zeros((CHUNK, CHUNK), F32)
            dbs = jnp.zeros((CHUNK, 1), F32)
            for c in range(tr // CHUNK):
                rs = slice(c * CHUNK, (c + 1) * CHUNK)
                vch = v[rs, cs]
                mix = _dot(wc, vch) + bs_ref[h]
                dg = dgm_blk[rs, cs]
                dzu_ref[rs, cs] = (dg * mix * _gelu_grad(zu[rs, cs])).astype(BF16)
                dmix = dg * u[rs, cs]
                dbs = dbs + jnp.sum(dmix, axis=-1, keepdims=True)
                dmix_b = dmix.astype(BF16)
                dw = dw + _dot(dmix_b, vch, NT)
                dv_s[rs, cs] = _dot(wct, dmix_b)
            dw_ref[h] += jnp.where(mask, dw, 0.0)
            dbs_ref[h] += dbs
        dv = dv_s[...]
        dlg_ref[...] += jnp.sum(dv * y, axis=0, keepdims=True)
        dlb_ref[...] += jnp.sum(dv, axis=0, keepdims=True)
        dy = dv * g_ref[...]
        dgv = rstd * (dy - jnp.mean(dy, axis=-1, keepdims=True) - y * jnp.mean(dy * y, axis=-1, keepdims=True))
        dzv_ref[...] = (dgv * _gelu_grad(zv)).astype(BF16)

    full = lambda a: pl.BlockSpec(a.shape, lambda i: (0,) * a.ndim)
    rows = pl.BlockSpec((tr, DG), lambda i: (i, 0))
    return pl.pallas_call(
        body, name="gmlp_bwd", grid=(S // tr,),
        in_specs=[rows, pl.BlockSpec((tr, DG), lambda i: (i, 3)), pl.BlockSpec((tr, DG), lambda i: (i, 4)),
                  full(ln_g), full(ln_b), full(w_s), full(w_st), full(bs_col)],
        out_specs=[rows, rows, full(w_s), full(bs_col), full(ln_g), full(ln_b)],
        out_shape=[jax.ShapeDtypeStruct((S, DG), BF16), jax.ShapeDtypeStruct((S, DG), BF16),
                   jax.ShapeDtypeStruct(w_s.shape, F32), jax.ShapeDtypeStruct(bs_col.shape, F32),
                   jax.ShapeDtypeStruct(ln_g.shape, F32), jax.ShapeDtypeStruct(ln_b.shape, F32)],
        scratch_shapes=[pltpu.VMEM((tr, DG), F32)],
        compiler_params=_params(("arbitrary",)),
    )(dgm, zm, zm, ln_g, ln_b, w_s, w_st, bs_col)


def _all_gather(name, blk):
    R, C = blk.shape

    def body(x_ref, out_ref, send_sems, recv_sems, local_sem):
        x, y, c = _me()
        me, sibling = (x, y, c), (x, y, 1 - c)
        chips = [(1 - x, y), (x, 1 - y), (1 - x, 1 - y)]

        def slab(px, py, pc):
            return out_ref.at[4 * px + 2 * py + pc]

        def copy(k, block, to, src=None):
            return pltpu.make_async_remote_copy(
                src_ref=slab(*block) if src is None else src, dst_ref=slab(*block),
                send_sem=send_sems.at[k], recv_sem=recv_sems.at[k], device_id=to, device_id_type=MESH)

        mine = pltpu.make_async_copy(x_ref, slab(*me), local_sem)
        mine.start()
        first = [copy(0, me, sibling, src=x_ref)]
        first += [copy(1 + n, me, (*chip, c), src=x_ref) for n, chip in enumerate(chips)]
        for cp in first:
            cp.start()
        passed = [copy(4 + n, (*chip, c), sibling) for n, chip in enumerate(chips)]
        for n, chip in enumerate(chips):
            copy(1 + n, (*chip, c), me).wait_recv()
            passed[n].start()
        copy(0, sibling, me).wait_recv()
        for n, chip in enumerate(chips):
            copy(4 + n, (*chip, 1 - c), me).wait_recv()
        for cp in first + passed:
            cp.wait_send()
        mine.wait()

    return pl.pallas_call(
        body, name=name, out_shape=jax.ShapeDtypeStruct((N_DEV, R, C), blk.dtype),
        in_specs=[_ANY], out_specs=_ANY,
        scratch_shapes=[pltpu.SemaphoreType.DMA((7,)), pltpu.SemaphoreType.DMA((7,)), pltpu.SemaphoreType.DMA(())],
    )(blk)


def _row_tile(R, C, itemsize=4, target_bytes=2 * 1024 * 1024):
    tr = R
    while tr % 2 == 0 and tr * C * itemsize > target_bytes and (tr // 2) % 16 == 0:
        tr //= 2
    return tr


def _rs_add1(name, g4, recv, c_idx):
    _, _, R, C = g4.shape
    tr = _row_tile(R, C)

    def body(c_ref, g_ref, r_ref, hb_ref):
        hb_ref[...] = (g_ref[...] + r_ref[...].astype(F32)).astype(BF16)

    blk = pl.BlockSpec((None, tr, C), lambda p, i, c_ref: (p, i, 0))
    return pl.pallas_call(
        body, name=name,
        grid_spec=pltpu.PrefetchScalarGridSpec(
            num_scalar_prefetch=1, grid=(4, R // tr),
            in_specs=[pl.BlockSpec((None, None, tr, C), lambda p, i, c_ref: (p, c_ref[0], i, 0)), blk],
            out_specs=blk),
        out_shape=jax.ShapeDtypeStruct((4, R, C), BF16),
        compiler_params=_params(("parallel", "parallel")),
    )(c_idx, g4, recv)


def _rs_add2_own(name, g4, recv1, recv2, c_idx, p_idx):
    _, _, R, C = g4.shape
    tr = _row_tile(R, C)

    def body(c_ref, p_ref, g_ref, r1_ref, r2_ref, out_ref):
        h = g_ref[...] + r1_ref[...].astype(F32)
        out_ref[...] = ((h + r2_ref[0].astype(F32)) + r2_ref[1].astype(F32)) + r2_ref[2].astype(F32)

    return pl.pallas_call(
        body, name=name,
        grid_spec=pltpu.PrefetchScalarGridSpec(
            num_scalar_prefetch=2, grid=(R // tr,),
            in_specs=[pl.BlockSpec((None, None, tr, C), lambda i, c_ref, p_ref: (p_ref[0], c_ref[0], i, 0)),
                      pl.BlockSpec((None, tr, C), lambda i, c_ref, p_ref: (p_ref[0], i, 0)),
                      pl.BlockSpec((3, tr, C), lambda i, c_ref, p_ref: (0, i, 0))],
            out_specs=pl.BlockSpec((tr, C), lambda i, c_ref, p_ref: (i, 0))),
        out_shape=jax.ShapeDtypeStruct((R, C), F32),
        compiler_params=_params(("parallel",)),
    )(c_idx, p_idx, g4, recv1, recv2)


def _rs_add1_windows(name, g, recv, first_blocks):
    _, R, W = recv.shape

    def body(t_ref, g_ref, r_ref, h_ref, hb_ref):
        h = g_ref[...] + r_ref[...].astype(F32)
        h_ref[...] = h
        hb_ref[...] = h.astype(BF16)

    blk = pl.BlockSpec((None, R, LANES), lambda p, l, t_ref: (p, 0, l))
    return pl.pallas_call(
        body, name=name,
        grid_spec=pltpu.PrefetchScalarGridSpec(
            num_scalar_prefetch=1, grid=(4, W // LANES),
            in_specs=[pl.BlockSpec((R, LANES), lambda p, l, t_ref: (0, t_ref[p] + l)), blk],
            out_specs=[blk, blk]),
        out_shape=[jax.ShapeDtypeStruct((4, R, W), F32), jax.ShapeDtypeStruct((4, R, W), BF16)],
        compiler_params=_params(("parallel", "parallel")),
    )(first_blocks, g, recv)


def _add_windows(name, windows, first, second, n_blocks):
    _, R, W = windows.shape
    dev1 = jnp.asarray([d for d, _ in first], jnp.int32)
    blk1 = jnp.asarray([b for _, b in first], jnp.int32)
    dev2 = jnp.asarray([max(d, 0) for d, _ in second], jnp.int32)
    blk2 = jnp.asarray([b for _, b in second], jnp.int32)
    two = jnp.asarray([int(d >= 0) for d, _ in second], jnp.int32)

    def body(d1_ref, b1_ref, d2_ref, b2_ref, two_ref, a_ref, b_ref, out_ref):
        k = pl.program_id(0)

        @pl.when(two_ref[k] == 0)
        def _():
            out_ref[...] = a_ref[...]

        @pl.when(two_ref[k] != 0)
        def _():
            out_ref[...] = a_ref[...] + b_ref[...]

    return pl.pallas_call(
        body, name=name,
        grid_spec=pltpu.PrefetchScalarGridSpec(
            num_scalar_prefetch=5, grid=(n_blocks,),
            in_specs=[pl.BlockSpec((None, R, LANES), lambda k, d1, b1, d2, b2, t: (d1[k], 0, b1[k])),
                      pl.BlockSpec((None, R, LANES), lambda k, d1, b1, d2, b2, t: (d2[k], 0, b2[k]))],
            out_specs=pl.BlockSpec((R, LANES), lambda k, d1, b1, d2, b2, t: (0, k))),
        out_shape=jax.ShapeDtypeStruct((R, n_blocks * LANES), windows.dtype),
        compiler_params=_params(("parallel",)),
    )(dev1, blk1, dev2, blk2, two, windows, windows)


def _rs_add2(name, h, recv, p_idx):
    _, R, C = h.shape
    tr = _row_tile(R, C)

    def body(p_ref, h_ref, r_ref, out_ref):
        out_ref[...] = ((h_ref[...] + r_ref[0].astype(F32)) + r_ref[1].astype(F32)) + r_ref[2].astype(F32)

    return pl.pallas_call(
        body, name=name,
        grid_spec=pltpu.PrefetchScalarGridSpec(
            num_scalar_prefetch=1, grid=(R // tr,),
            in_specs=[pl.BlockSpec((None, tr, C), lambda i, p_ref: (p_ref[0], i, 0)),
                      pl.BlockSpec((3, tr, C), lambda i, p_ref: (0, i, 0))],
            out_specs=pl.BlockSpec((tr, C), lambda i, p_ref: (i, 0))),
        out_shape=jax.ShapeDtypeStruct((R, C), F32),
        compiler_params=_params(("parallel",)),
    )(p_idx, h, recv)


def _sum8(name, g):
    _, R, C = g.shape

    def body(g_ref, out_ref):
        acc = g_ref[0]
        for d in range(1, N_DEV):
            acc = acc + g_ref[d]
        out_ref[...] = acc

    return pl.pallas_call(body, name=name, out_shape=jax.ShapeDtypeStruct((R, C), F32),
                          compiler_params=_params())(g)


def _adamw_math(w, g, m, v):
    m = ADAM_B1 * m + (1.0 - ADAM_B1) * g
    v = ADAM_B2 * v + (1.0 - ADAM_B2) * (g * g)
    m_hat = m / (1.0 - ADAM_B1 ** ADAM_STEP)
    v_hat = v / (1.0 - ADAM_B2 ** ADAM_STEP)
    delta = -ADAM_LR * (m_hat / (jnp.sqrt(v_hat) + ADAM_EPS) + ADAM_WD * w)
    return delta, m, v


def _adamw(name, w, g, m, v):
    R, C = w.shape
    tr = _row_tile(R, C, target_bytes=1024 * 1024)
    return _row_call(name, lambda *a: (_adamw_math(*a), ()), [w, g, m, v], [], [(C, F32)] * 3, [], tr)


def _adamw_many(name, ws, gs, ms, vs):
    n = len(ws)

    def body(*refs):
        ins, outs = refs[:4 * n], refs[4 * n:]
        for k in range(n):
            res = _adamw_math(ins[k][...], ins[n + k][...], ins[2 * n + k][...], ins[3 * n + k][...])
            for t in range(3):
                outs[t * n + k][...] = res[t]

    out = pl.pallas_call(
        body, name=name, out_shape=[jax.ShapeDtypeStruct(w.shape, F32) for _ in range(3) for w in ws],
        compiler_params=_params(),
    )(*ws, *gs, *ms, *vs)
    return out[:n], out[n:2 * n], out[2 * n:]


def _pack(parts):
    flat = []
    total = 0
    for a in parts:
        n = math.prod(a.shape)
        flat.append(a.reshape(-1).astype(F32))
        if n % LANES:
            flat.append(jnp.zeros((-n % LANES,), F32))
        total += n + (-n % LANES)
    if total % (8 * LANES):
        flat.append(jnp.zeros((-total % (8 * LANES),), F32))
    return jnp.concatenate(flat).reshape(-1, LANES)


def _unpack(packed, shapes):
    out = []
    r = 0
    for shp in shapes:
        n = math.prod(shp)
        nr = -(-n // LANES)
        out.append(packed[r:r + nr].reshape(-1)[:n].reshape(shp))
        r += nr
    return out


def kernel(x, norm_mix_g, w_in, b_f, gmlp_ln_g, gmlp_ln_b, w_s, b_s, attn_out_g, gmlp_out_g, w_out, norm_ffn_g, w_ff1, w_ff2, norm_final_g, loss_target, m_norm_mix_g, m_w_in, m_b_f, m_gmlp_ln_g, m_gmlp_ln_b, m_w_s, m_b_s, m_attn_out_g, m_gmlp_out_g, m_w_out, m_norm_ffn_g, m_w_ff1, m_w_ff2, m_norm_final_g, v_norm_mix_g, v_w_in, v_b_f, v_gmlp_ln_g, v_gmlp_ln_b, v_w_s, v_b_s, v_attn_out_g, v_gmlp_out_g, v_w_out, v_norm_ffn_g, v_w_ff1, v_w_ff2, v_norm_final_g):
    S, D = x.shape[1], x.shape[2]
    H = b_f.shape[1]
    DA = H * HEAD_DIM
    DG = gmlp_ln_g.shape[1]
    DQKV = 3 * DA
    DMAIN = DQKV + 2 * DG
    DIN = DMAIN + H
    DFF = w_ff1.shape[2] * N_DEV
    w_in_cols = w_in.shape[2]
    assert DIN == w_in_cols * N_DEV and DA == DG and D == DA + DG

    T_ATT = min(T_ATT_MAX, S)
    TR = min(TR_MAX, S)

    x0 = x[0]
    tgt = loss_target[0]
    g_final = norm_final_g.reshape(1, D)

    FB = DFF // N_DEV
    x_pos, y_pos, c_pos = _me()
    me_idx = 4 * x_pos + 2 * y_pos + c_pos

    WW = -(-(w_in_cols + LANES - 1) // LANES) * LANES
    to_main = lambda col: col if col <= DQKV else max(DQKV, col - H)
    lo = [to_main(n * w_in_cols) for n in range(N_DEV)]
    hi = [to_main((n + 1) * w_in_cols) for n in range(N_DEV)]
    starts = [v // LANES * LANES for v in lo]
    gate_dev = DQKV // w_in_cols
    n_before = DQKV - gate_dev * w_in_cols
    g0 = lo[gate_dev] - starts[gate_dev]
    stash = -(-(g0 + w_in_cols - H) // LANES) * LANES
    assert all(hi[n] <= starts[n] + WW <= DMAIN for n in range(N_DEV))
    assert gate_dev * w_in_cols <= DQKV and DQKV + H <= (gate_dev + 1) * w_in_cols and stash + LANES <= WW
    shard = w_in[0].astype(BF16)

    def my_window(n):
        if n != gate_dev:
            return lambda s: jnp.pad(s, ((0, 0), (lo[n] - starts[n], WW - w_in_cols - (lo[n] - starts[n]))))
        return lambda s: jnp.concatenate([
            jnp.zeros((D, g0), BF16), s[:, :n_before], s[:, n_before + H:],
            jnp.zeros((D, stash - g0 - (w_in_cols - H)), BF16), s[:, n_before:n_before + H],
            jnp.zeros((D, WW - stash - H), BF16)], axis=1)
    windows = _all_gather("ag_w_in", lax.switch(me_idx, [my_window(n) for n in range(N_DEV)], shard))
    first, second = [], []
    for blk in range(DMAIN // LANES):
        c0 = blk * LANES
        owners = [(n, (c0 - starts[n]) // LANES) for n in range(N_DEV) if lo[n] < c0 + LANES and hi[n] > c0]
        assert 1 <= len(owners) <= 2
        first.append(owners[0])
        second.append(owners[1] if len(owners) == 2 else (-1, 0))
    w_main = _add_windows("w_in_windows", windows, first, second, DMAIN // LANES)
    w_f = windows[gate_dev, :, stash:stash + LANES]
    c_idx = jnp.reshape(c_pos, (1,)).astype(jnp.int32)
    p_idx = jnp.reshape(2 * x_pos + y_pos, (1,)).astype(jnp.int32)

    (h,), _ = _row_call("rms_mix", lambda xb, g: ((_rms_fwd(xb, g),), ()), [x0], [norm_mix_g], [(D, BF16)], [], TR)
    (zm,), ((w_out_part,),) = _mm_nn("in_proj", h, w_main, [BF16], 2048, 1024, 2048,
                                     jobs=[_job_gather_chips(w_out[0].astype(BF16))])
    (zf,) = _mm_nn("in_proj_f", h, w_f, [F32], 1024, LANES, 2048)
    bf_pad = jnp.pad(b_f, ((0, 0), (0, LANES - H)))
    f_row = _fgate_fwd(zf, bf_pad)
    NB = S // T_ATT
    f_col3 = f_row.reshape(H, S, 1)
    f_row3 = f_row.reshape(H, NB, 1, T_ATT)
    (attn, lse_col3), ((w_out_all,), (w_ff1_part,)) = _attn2_fwd(
        zm, f_col3, f_row3, T_ATT, jobs=[_job_gather_sibling(w_out_part), _job_gather_chips(w_ff1[0].astype(BF16))])
    w_out_full = w_out_all.reshape(D, D)
    bs_col = b_s[0].reshape(H, CHUNK, 1)
    gm = _gmlp_fwd(zm, gmlp_ln_g, gmlp_ln_b, w_s[0], bs_col, TR)

    def merge_fn(a, g, ga, gg):
        return (jnp.concatenate([_rms_fwd(a, ga), _rms_fwd(g, gg)], axis=1),), ()
    (merged,), _ = _row_call("rms_merge", merge_fn, [attn, gm], [attn_out_g, gmlp_out_g], [(D, BF16)], [], TR)

    w_ff2_b = w_ff2[0].astype(BF16)
    TMR = min(512, S)

    def out_proj_fn(acc, res, g):
        xb = acc + res
        return (xb, _rms_fwd(xb, g)), ()
    ((x1, h2), _), ((w_ff1_all,), (w_ff2_q1,)) = _mm_rows(
        "out_proj", (S // TMR, 1), merged, pl.BlockSpec((TMR, D), lambda i, k: (i, 0)),
        w_out_full, pl.BlockSpec((D, D), lambda i, k: (0, 0)), NN, TMR, D, [x0], [norm_ffn_g],
        [(D, F32), (D, BF16)], [], out_proj_fn,
        jobs=[_job_gather_sibling(w_ff1_part), _job_gather_chips(w_ff2_b, part=(0, 1, 4))])

    tm, tn, tk = min(1024, S), min(1024, FB), min(2048, D)
    tm1 = min(2048, S)
    o_spec = pl.BlockSpec((tm1, tn), lambda i, j, k: (i, j))

    def relu_sq(acc):
        a = jnp.maximum(acc, 0.0)
        return a, a * a
    nj = FB // tn
    ff2_rest = [_job_gather_chips(w_ff2_b, part=(1, 4, 4), into=w_ff2_q1)]
    (a_act, a_sq), ((w_ff2_q2,),) = _mm(
        "ff1", (S // tm1, DFF // tn, D // tk), h2, pl.BlockSpec((tm1, tk), lambda i, j, k: (i, k)),
        w_ff1_all, pl.BlockSpec((None, tk, tn), lambda i, j, k: (j // nj, k, j % nj)), NN, (tm1, tn),
        [jax.ShapeDtypeStruct((S, DFF), BF16)] * 2, [o_spec] * 2, epilogue=relu_sq, jobs=ff2_rest)
    (w_ff2_all,) = _run_jobs("ag_w_ff2_sibling", [_job_gather_sibling(w_ff2_q2)])[0]
    w_ff2_full = w_ff2_all.reshape(DFF, D)
    def head_fn(acc, res, t, g):
        xb = acc + res
        rstd = lax.rsqrt(jnp.mean(xb * xb, axis=-1, keepdims=True) + EPS)
        xhat = xb * rstd
        err = xhat * g - t
        loss = 0.5 * jnp.sum(jnp.mean(err * err, axis=-1, keepdims=True), axis=0, keepdims=True)
        dy = err * (1.0 / D)
        dg = jnp.sum(dy * xhat, axis=0, keepdims=True)
        dxhat = dy * g
        dx = rstd * (dxhat - xhat * jnp.mean(dxhat * xhat, axis=-1, keepdims=True))
        return (dx, dx), (dg, jnp.broadcast_to(loss, (1, LANES)))
    tk_ff2 = min(1024, DFF)
    (dx2, dx2_b), (dg_final, loss_part) = _mm_rows(
        "ff2", (S // TMR, DFF // tk_ff2), a_sq, pl.BlockSpec((TMR, tk_ff2), lambda i, k: (i, k)),
        w_ff2_full, pl.BlockSpec((tk_ff2, D), lambda i, k: (k, 0)), NN, TMR, D, [x1, tgt], [g_final],
        [(D, F32), (D, BF16)], [D, LANES], head_fn)

    (da,) = _mm_nt("ff2_dx", dx2_b, w_ff2_full, [BF16], 2048, 1024, 2048, extras=[a_act],
                   epilogue=lambda acc, a: (2.0 * a.astype(F32) * acc,))
    dw_ff2, dw_ff2_b = _mm_tn("ff2_dw", a_sq, dx2_b, [F32, BF16], 1024, 2048, 1024)
    tm2, tk2 = min(2048, D), min(1024, S)
    dw1_spec = pl.BlockSpec((None, tm2, FB), lambda i, j, k: (j, i, 0))
    (dw_ff1, dw_ff1_b), ((r1_ff2,),) = _mm(
        "ff1_dw", (D // tm2, DFF // FB, S // tk2), h2, pl.BlockSpec((tk2, tm2), lambda i, j, k: (k, i)),
        da, pl.BlockSpec((tk2, FB), lambda i, j, k: (k, j)), TN, (tm2, FB),
        [jax.ShapeDtypeStruct((N_DEV, D, FB), F32), jax.ShapeDtypeStruct((N_DEV, D, FB), BF16)], [dw1_spec] * 2,
        epilogue=lambda acc: (acc, acc), jobs=[_job_scatter_sibling(dw_ff2_b.reshape(4, 2, FB, D))])
    hb_ff2 = _rs_add1("rs_add1_w_ff2", dw_ff2.reshape(4, 2, FB, D), r1_ff2, c_idx)
    def ffn_bwd_fn(dh, xb, dres, g):
        dx, dg = _rms_bwd(dh, xb, g)
        dx = dx + dres
        return (dx, dx), (dg,)
    tkb = min(1024, FB)
    nkb = FB // tkb
    ((dx1, dx1_b), (dg_ffn,)), ((r2_ff2,), (r1_ff1,)) = _mm_rows(
        "ff1_dx", (S // TMR, DFF // tkb), da, pl.BlockSpec((TMR, tkb), lambda i, k: (i, k)),
        w_ff1_all, pl.BlockSpec((None, D, tkb), lambda i, k: (k // nkb, 0, k % nkb)), NT, TMR, D, [x1, dx2],
        [norm_ffn_g], [(D, F32), (D, BF16)], [D], ffn_bwd_fn,
        jobs=[_job_scatter_chips(hb_ff2), _job_scatter_sibling(dw_ff1_b.reshape(4, 2, D, FB))])
    g_w_ff2 = _rs_add2_own("rs_add2_w_ff2", dw_ff2.reshape(4, 2, FB, D), r1_ff2, r2_ff2, c_idx, p_idx)
    hb_ff1 = _rs_add1("rs_add1_w_ff1", dw_ff1.reshape(4, 2, D, FB), r1_ff1, c_idx)

    def merge_bwd_fn(dm, a, g, ga, gg):
        da_, dga = _rms_bwd(dm[:, :DA], a, ga)
        dg_, dgg = _rms_bwd(dm[:, DA:], g, gg)
        return (da_, dg_), (dga, dgg)
    (dattn, dgm), (dg_attn, dg_gmlp) = _mm_rows(
        "out_proj_dx", (S // TMR, 1), dx1_b, pl.BlockSpec((TMR, D), lambda i, k: (i, 0)),
        w_out_full, pl.BlockSpec((D, D), lambda i, k: (0, 0)), NT, TMR, D, [attn, gm], [attn_out_g, gmlp_out_g],
        [(DA, F32), (DG, F32)], [DA, DG], merge_bwd_fn)
    dw_out, dw_out_b = _mm_tn("out_proj_dw", merged, dx1_b, [F32, BF16], 2048, 1024, 1024)

    w_st = jnp.swapaxes(w_s[0], 1, 2)
    dzu, dzv, dw_s, dbs_col, dln_g, dln_b = _gmlp_bwd(dgm, zm, gmlp_ln_g, gmlp_ln_b, w_s[0], w_st, bs_col, TR)

    delta_row = _attn_delta(dattn, attn, TR)
    lse_row3 = lse_col3.reshape(H, NB, 1, T_ATT)
    (dq, ds_rowsum), ((r2_ff1,), (r1_out,)) = _attn2_bwd_dq(
        zm, dattn, f_col3, f_row3, lse_col3, delta_row.reshape(H, S, 1), T_ATT,
        jobs=[_job_scatter_chips(hb_ff1), _job_scatter_sibling(dw_out_b.reshape(4, 2, D // N_DEV, D))])
    g_w_ff1 = _rs_add2_own("rs_add2_w_ff1", dw_ff1.reshape(4, 2, D, FB), r1_ff1, r2_ff1, c_idx, p_idx)
    hb_out = _rs_add1("rs_add1_w_out", dw_out.reshape(4, 2, D // N_DEV, D), r1_out, c_idx)
    (dk, dv, df_col3), ((r2_out,),) = _attn2_bwd_dkv(
        zm, dattn, f_col3, f_row3, lse_row3, delta_row.reshape(H, NB, 1, T_ATT),
        ds_rowsum.reshape(H, NB, 1, T_ATT), T_ATT,
        jobs=[_job_scatter_chips(hb_out)])
    g_w_out = _rs_add2_own("rs_add2_w_out", dw_out.reshape(4, 2, D // N_DEV, D), r1_out, r2_out, c_idx, p_idx)
    dzf, dbf = _fgate_bwd(df_col3.reshape(H, S), zf, bf_pad)

    dz_main = jnp.concatenate([dq, dk, dv, dzu, dzv], axis=1)
    dw_main, dw_main_b = _mm_tn("in_proj_dw", h, dz_main, [F32, BF16], 2048, 1024, 1024)
    (dw_f,), ((r1_in,),) = _mm_tn("in_proj_f_dw", h, dzf, [F32], 2048, LANES, 1024,
                                  jobs=[_job_scatter_sibling_windows(dw_main_b, starts, WW)])
    first_blocks = jnp.stack([jnp.where(c_pos == 0, starts[2 * p], starts[2 * p + 1]) // LANES
                              for p in range(4)]).astype(jnp.int32)
    h_in, hb_in = _rs_add1_windows("rs_add1_w_in", dw_main, r1_in, first_blocks)

    def mix_bwd_fn(dh_main, dz_gate, xb, dres, g, w_gate):
        dx, dg = _rms_bwd(dh_main + _dot(dz_gate, w_gate, NT), xb, g)
        return (dx + dres,), (dg,)
    tk_in = min(1024, DMAIN)
    ((grad_x,), (dg_mix,)), ((r2_in,),) = _mm_rows(
        "in_proj_dx", (S // TMR, DMAIN // tk_in), dz_main, pl.BlockSpec((TMR, tk_in), lambda i, k: (i, k)),
        w_main, pl.BlockSpec((D, tk_in), lambda i, k: (0, k)), NT, TMR, D, [dzf, x0, dx1], [norm_mix_g, w_f],
        [(D, F32)], [D], mix_bwd_fn, jobs=[_job_scatter_chips(hb_in)])
    g_window = _rs_add2("rs_add2_w_in", h_in, r2_in, p_idx)

    small_shapes = [norm_mix_g.shape, b_f.shape, gmlp_ln_g.shape, gmlp_ln_b.shape, w_s.shape, b_s.shape,
                    attn_out_g.shape, gmlp_out_g.shape, norm_ffn_g.shape, norm_final_g.shape]
    small_parts = [dg_mix, dbf[:, :H], dln_g, dln_b, dw_s, dbs_col, dg_attn, dg_gmlp, dg_ffn, dg_final]
    g_small = _sum8("small_sum", _all_gather("ag_small", _pack(small_parts + [dw_f[:, :H], loss_part])))
    *gs, g_gate, loss_sum = _unpack(g_small, small_shapes + [(D, H), (1, LANES)])
    two_d = lambda a: a.reshape(1, -1) if a.ndim == 1 else a
    ds, nms, nvs = _adamw_many(
        "adamw_small",
        [two_d(a) for a in (norm_mix_g, b_f, gmlp_ln_g, gmlp_ln_b, w_s, b_s, attn_out_g, gmlp_out_g, norm_ffn_g,
                            norm_final_g)],
        [two_d(a) for a in gs],
        [two_d(a) for a in (m_norm_mix_g, m_b_f, m_gmlp_ln_g, m_gmlp_ln_b, m_w_s, m_b_s, m_attn_out_g, m_gmlp_out_g,
                            m_norm_ffn_g, m_norm_final_g)],
        [two_d(a) for a in (v_norm_mix_g, v_b_f, v_gmlp_ln_g, v_gmlp_ln_b, v_w_s, v_b_s, v_attn_out_g, v_gmlp_out_g,
                            v_norm_ffn_g, v_norm_final_g)])
    ds, nms, nvs = [[a.reshape(s) for a, s in zip(lst, small_shapes)] for lst in (ds, nms, nvs)]

    def my_columns(n):
        if n != gate_dev:
            return lambda win, gate: win[:, lo[n] - starts[n]:lo[n] - starts[n] + w_in_cols]
        return lambda win, gate: jnp.concatenate([win[:, g0:g0 + n_before], gate,
                                                  win[:, g0 + n_before:g0 + w_in_cols - H]], axis=1)
    g_w_in = lax.switch(me_idx, [my_columns(n) for n in range(N_DEV)], g_window, g_gate)

    big = {}
    for nm, w, g, m, v in (("w_in", w_in, g_w_in, m_w_in, v_w_in), ("w_out", w_out, g_w_out, m_w_out, v_w_out),
                           ("w_ff1", w_ff1, g_w_ff1, m_w_ff1, v_w_ff1), ("w_ff2", w_ff2, g_w_ff2, m_w_ff2, v_w_ff2)):
        (d_, m_, v_), _ = _adamw("adamw_" + nm, w[0], g, m[0], v[0])
        big[nm] = (g[None], d_[None], m_[None], v_[None])

    loss = loss_sum[0, 0]

    def leaves(n):
        sm = (gs, ds, nms, nvs)[n]
        return [sm[0], big["w_in"][n], sm[1], sm[2], sm[3], sm[4], sm[5], sm[6], sm[7], big["w_out"][n], sm[8],
                big["w_ff1"][n], big["w_ff2"][n], sm[9]]

    return (loss, grad_x[None], *leaves(0), *leaves(1), *leaves(2), *leaves(3))
```

```python
import functools
import math

import jax
import jax.numpy as jnp
from jax import lax
from jax.experimental import pallas as pl
from jax.experimental.pallas import tpu as pltpu

F32 = jnp.float32
BF16 = jnp.bfloat16
MESH = pl.DeviceIdType.MESH

HEAD_DIM = 128
CHUNK = 128
EPS = 1e-6
LANES = 128
N_DEV = 8

ADAM_LR = 0.001
ADAM_B1 = 0.9
ADAM_B2 = 0.999
ADAM_EPS = 1e-08
ADAM_WD = 0.01
ADAM_STEP = 10

VMEM_LIMIT_BYTES = 56 * 1024 * 1024
T_ATT_MAX = 1024
TR_MAX = 512

NN = ((1,), (0,))
NT = ((1,), (1,))
TN = ((0,), (0,))


def _params(sem=None):
    return pltpu.CompilerParams(dimension_semantics=sem, vmem_limit_bytes=VMEM_LIMIT_BYTES)


def _dot(a, b, contract=NN):
    return lax.dot_general(a, b, (contract, ((), ())), preferred_element_type=F32)


def _dot3(x, t):
    x1 = x.astype(BF16)
    r1 = x - x1.astype(F32)
    x2 = r1.astype(BF16)
    x3 = (r1 - x2.astype(F32)).astype(BF16)
    return _dot(x1, t) + _dot(x2, t) + _dot(x3, t)


def _iota2(shape, dim):
    return lax.broadcasted_iota(jnp.int32, shape, dim)


def _row_call(name, fn, row_ins, bcast_ins, row_outs, acc_outs, tr):
    S = row_ins[0].shape[0]
    assert S % tr == 0
    n_ri, n_bi, n_ro, n_ao = len(row_ins), len(bcast_ins), len(row_outs), len(acc_outs)

    def body(*refs):
        ins = [r[...] for r in refs[:n_ri + n_bi]]
        ro_refs = refs[n_ri + n_bi:n_ri + n_bi + n_ro]
        ao_refs = refs[n_ri + n_bi + n_ro:]
        ro, ao = fn(*ins)
        for r, v in zip(ro_refs, ro):
            r[...] = v.astype(r.dtype)
        if n_ao:
            @pl.when(pl.program_id(0) == 0)
            def _():
                for r in ao_refs:
                    r[...] = jnp.zeros_like(r)
            for r, v in zip(ao_refs, ao):
                r[...] += v

    in_specs = [pl.BlockSpec((tr, a.shape[1]), lambda i: (i, 0)) for a in row_ins]
    in_specs += [pl.BlockSpec(a.shape, lambda i: (0, 0)) for a in bcast_ins]
    out_specs = [pl.BlockSpec((tr, d), lambda i: (i, 0)) for d, _ in row_outs]
    out_specs += [pl.BlockSpec((1, d), lambda i: (0, 0)) for d in acc_outs]
    out_shape = [jax.ShapeDtypeStruct((S, d), dt) for d, dt in row_outs]
    out_shape += [jax.ShapeDtypeStruct((1, d), F32) for d in acc_outs]
    outs = pl.pallas_call(
        body, name=name, grid=(S // tr,), in_specs=in_specs, out_specs=out_specs, out_shape=out_shape,
        compiler_params=_params(("arbitrary",) if n_ao else ("parallel",)),
    )(*row_ins, *bcast_ins)
    return outs[:n_ro], outs[n_ro:]


def _rms_fwd(x, g):
    rstd = lax.rsqrt(jnp.mean(x * x, axis=-1, keepdims=True) + EPS)
    return x * rstd * g


def _rms_bwd(dy, x, g):
    rstd = lax.rsqrt(jnp.mean(x * x, axis=-1, keepdims=True) + EPS)
    xhat = x * rstd
    dg = jnp.sum(dy * xhat, axis=0, keepdims=True)
    dxhat = dy * g
    dx = rstd * (dxhat - xhat * jnp.mean(dxhat * xhat, axis=-1, keepdims=True))
    return dx, dg


_GELU_C = math.sqrt(2.0 / math.pi)


def _gelu(x):
    return 0.5 * x * (1.0 + jnp.tanh(_GELU_C * (x + 0.044715 * (x * x * x))))


def _gelu_grad(x):
    t = jnp.tanh(_GELU_C * (x + 0.044715 * (x * x * x)))
    return 0.5 * (1.0 + t) + 0.5 * x * (1.0 - t * t) * (_GELU_C * (1.0 + 3.0 * 0.044715 * (x * x)))


def _me():
    return lax.axis_index("x"), lax.axis_index("y"), lax.axis_index("c")


def _other_chips(x, y):
    return [(1 - x, y), (x, 1 - y), (1 - x, 1 - y)]


_ANY = pl.BlockSpec(memory_space=pl.ANY)


class _Job:
    def __init__(self, ins, outs, n_sems, make, aliases=None):
        self.ins, self.outs, self.n_sems, self.make, self.aliases = ins, outs, n_sems, make, aliases or {}


def _job_gather_chips(blk, part=(0, 1, 1), into=None):
    R, C = blk.shape
    nr = R // part[2]
    rows = pl.ds(part[0] * nr, (part[1] - part[0]) * nr)

    def make(ins, outs, send_sems, recv_sems, base):
        x_ref, (out_ref,) = ins[0], outs
        x, y, c = _me()
        mine = 4 * x + 2 * y + c
        targets = [(x, y, 1 - c)] + [(cx, cy, c) for cx, cy in _other_chips(x, y)]

        def copy(k, slab, to):
            return pltpu.make_async_remote_copy(
                src_ref=x_ref.at[rows, :], dst_ref=out_ref.at[slab, rows, :], send_sem=send_sems.at[base + k],
                recv_sem=recv_sems.at[base + k], device_id=to, device_id_type=MESH)

        starts = [copy(k, mine, to) for k, to in enumerate(targets)]
        arrivals = [copy(k, 4 * tx + 2 * ty + tc, (tx, ty, tc)) for k, (tx, ty, tc) in enumerate(targets)]
        local = [pltpu.make_async_copy(x_ref.at[rows, :], out_ref.at[mine, rows, :], send_sems.at[base + 4])]
        return starts, arrivals, local

    out = jax.ShapeDtypeStruct((N_DEV, R, C), blk.dtype)
    if into is None:
        return _Job([blk], [out], 5, make)
    return _Job([blk, into], [out], 5, make, aliases={1: 0})


def _job_gather_sibling(part):
    def make(ins, outs, send_sems, recv_sems, base):
        (out_ref,) = outs
        x, y, c = _me()

        def copy(k, slab):
            return pltpu.make_async_remote_copy(
                src_ref=out_ref.at[slab], dst_ref=out_ref.at[slab], send_sem=send_sems.at[base + k],
                recv_sem=recv_sems.at[base + k], device_id=(x, y, 1 - c), device_id_type=MESH)

        chips = _other_chips(x, y)
        starts = [copy(k, 4 * cx + 2 * cy + c) for k, (cx, cy) in enumerate(chips)]
        arrivals = [copy(k, 4 * cx + 2 * cy + (1 - c)) for k, (cx, cy) in enumerate(chips)]
        return starts, arrivals, []

    return _Job([part], [jax.ShapeDtypeStruct(part.shape, part.dtype)], 3, make, aliases={0: 0})


def _job_scatter_sibling(gb):
    _, _, R, C = gb.shape

    def make(ins, outs, send_sems, recv_sems, base):
        (g_ref,), (recv_ref,) = ins, outs
        x, y, c = _me()
        copies = [pltpu.make_async_remote_copy(
            src_ref=g_ref.at[p, 1 - c], dst_ref=recv_ref.at[p], send_sem=send_sems.at[base + p],
            recv_sem=recv_sems.at[base + p], device_id=(x, y, 1 - c), device_id_type=MESH) for p in range(4)]
        return copies, copies, []

    return _Job([gb], [jax.ShapeDtypeStruct((4, R, C), gb.dtype)], 4, make)


def _job_scatter_sibling_windows(gb, starts, width):
    R, _ = gb.shape

    def make(ins, outs, send_sems, recv_sems, base):
        (g_ref,), (recv_ref,) = ins, outs
        x, y, c = _me()
        copies = []
        for p in range(4):
            start = pl.multiple_of(jnp.where(c == 0, starts[2 * p + 1], starts[2 * p]), LANES)
            copies.append(pltpu.make_async_remote_copy(
                src_ref=g_ref.at[:, pl.ds(start, width)], dst_ref=recv_ref.at[p], send_sem=send_sems.at[base + p],
                recv_sem=recv_sems.at[base + p], device_id=(x, y, 1 - c), device_id_type=MESH))
        return copies, copies, []

    return _Job([gb], [jax.ShapeDtypeStruct((4, R, width), gb.dtype)], 4, make)


def _job_scatter_chips(hb):
    _, R, C = hb.shape

    def make(ins, outs, send_sems, recv_sems, base):
        (h_ref,), (recv_ref,) = ins, outs
        x, y, c = _me()
        copies = [pltpu.make_async_remote_copy(
            src_ref=h_ref.at[2 * cx + cy], dst_ref=recv_ref.at[n], send_sem=send_sems.at[base + n],
            recv_sem=recv_sems.at[base + n], device_id=(cx, cy, c), device_id_type=MESH)
            for n, (cx, cy) in enumerate(_other_chips(x, y))]
        return copies, copies, []

    return _Job([hb], [jax.ShapeDtypeStruct((3, R, C), hb.dtype)], 3, make)


def _carry_call(body, *, name, grid, in_specs, out_specs, out_shape, scratch_shapes, semantics, args, jobs=()):
    jobs = list(jobs)
    n_in, n_out, n_scr = len(in_specs), len(out_specs), len(scratch_shapes)
    j_ins = [a for j in jobs for a in j.ins]
    j_outs = [o for j in jobs for o in j.outs]
    n_sems = sum(j.n_sems for j in jobs)
    aliases = {}
    i0, o0 = n_in, n_out
    for j in jobs:
        for a, b in j.aliases.items():
            aliases[i0 + a] = o0 + b
        i0 += len(j.ins)
        o0 += len(j.outs)

    def full_body(*refs):
        ins = refs[:n_in]
        jin = refs[n_in:n_in + len(j_ins)]
        outs = refs[n_in + len(j_ins):n_in + len(j_ins) + n_out]
        jout = refs[n_in + len(j_ins) + n_out:n_in + len(j_ins) + n_out + len(j_outs)]
        scr = refs[n_in + len(j_ins) + n_out + len(j_outs):]
        if jobs:
            send_sems, recv_sems = scr[n_scr], scr[n_scr + 1]
            starts, arrivals, local = [], [], []
            base = i0 = o0 = 0
            for j in jobs:
                s, a, l = j.make(jin[i0:i0 + len(j.ins)], jout[o0:o0 + len(j.outs)], send_sems, recv_sems, base)
                starts += s
                arrivals += a
                local += l
                base += j.n_sems
                i0 += len(j.ins)
                o0 += len(j.outs)
            pids = [pl.program_id(d) for d in range(len(grid))]
            first = functools.reduce(jnp.logical_and, [p == 0 for p in pids])
            last = functools.reduce(jnp.logical_and, [p == n - 1 for p, n in zip(pids, grid)])

            @pl.when(first)
            def _():
                for cp in local + starts:
                    cp.start()

        body(*ins, *outs, *scr[:n_scr])

        if jobs:
            @pl.when(last)
            def _():
                for cp in arrivals:
                    cp.wait_recv()
                for cp in starts:
                    cp.wait_send()
                for cp in local:
                    cp.wait()

    sems = [pltpu.SemaphoreType.DMA((n_sems,)), pltpu.SemaphoreType.DMA((n_sems,))] if jobs else []
    res = pl.pallas_call(
        full_body, name=name, grid=grid,
        in_specs=list(in_specs) + [_ANY] * len(j_ins),
        out_specs=list(out_specs) + [_ANY] * len(j_outs),
        out_shape=list(out_shape) + j_outs,
        scratch_shapes=list(scratch_shapes) + sems,
        input_output_aliases=aliases,
        compiler_params=_params(("arbitrary",) * len(grid) if jobs else semantics),
    )(*args, *j_ins)
    body_res, job_res = res[:n_out], res[n_out:]
    per_job = []
    for j in jobs:
        per_job.append(job_res[:len(j.outs)])
        job_res = job_res[len(j.outs):]
    return body_res, per_job


def _run_jobs(name, jobs):
    def body(done_ref):
        done_ref[...] = jnp.zeros_like(done_ref)

    return _carry_call(body, name=name, grid=(1,), in_specs=[], out_specs=[pl.BlockSpec((8, LANES), lambda i: (0, 0))],
                       out_shape=[jax.ShapeDtypeStruct((8, LANES), F32)], scratch_shapes=[], semantics=("arbitrary",),
                       args=[], jobs=jobs)[1]


def _mm(name, grid, a, a_spec, b, b_spec, contract, acc_shape, out_shape, out_specs, extras=(), epilogue=None, jobs=()):
    nk = grid[2]
    n_e = len(extras)
    n_o = len(out_shape)
    if epilogue is None:
        epilogue = lambda acc: (acc,)

    def body(a_ref, b_ref, *rest):
        e_refs = rest[:n_e]
        o_refs = rest[n_e:n_e + n_o]

        def finish(total):
            res = epilogue(total, *[r[...] for r in e_refs])
            for o, r in zip(o_refs, res):
                o[...] = r.astype(o.dtype)

        if nk == 1:
            finish(_dot(a_ref[...], b_ref[...], contract))
            return
        acc = rest[n_e + n_o]
        k = pl.program_id(2)

        @pl.when(k == 0)
        def _():
            acc[...] = _dot(a_ref[...], b_ref[...], contract)

        @pl.when(jnp.logical_and(k > 0, k < nk - 1))
        def _():
            acc[...] += _dot(a_ref[...], b_ref[...], contract)

        @pl.when(k == nk - 1)
        def _():
            finish(acc[...] + _dot(a_ref[...], b_ref[...], contract))

    outs, job_res = _carry_call(
        body, name=name, grid=grid, in_specs=[a_spec, b_spec] + [s for _, s in extras],
        out_specs=list(out_specs), out_shape=list(out_shape),
        scratch_shapes=[pltpu.VMEM(acc_shape, F32)] if nk > 1 else [],
        semantics=("parallel", "parallel", "arbitrary"), args=[a, b] + [e for e, _ in extras], jobs=jobs)
    return (outs, job_res) if jobs else outs


def _mm_rows(name, grid, a, a_spec, b, b_spec, contract, tm, n, row_extras, bcast, row_outs, acc_outs, epilogue, jobs=()):
    nk = grid[1]
    M = grid[0] * tm
    n_x, n_b, n_ro, n_ao = len(row_extras), len(bcast), len(row_outs), len(acc_outs)

    def body(a_ref, b_ref, *rest):
        x_refs = rest[:n_x + n_b]
        ro_refs = rest[n_x + n_b:n_x + n_b + n_ro]
        ao_refs = rest[n_x + n_b + n_ro:n_x + n_b + n_ro + n_ao]
        i = pl.program_id(0)

        def finish(total):
            ro, ao = epilogue(total, *[r[...] for r in x_refs])
            for r, v in zip(ro_refs, ro):
                r[...] = v.astype(r.dtype)
            if n_ao:
                @pl.when(i == 0)
                def _():
                    for r, v in zip(ao_refs, ao):
                        r[...] = v

                @pl.when(i > 0)
                def _():
                    for r, v in zip(ao_refs, ao):
                        r[...] += v

        if nk == 1:
            finish(_dot(a_ref[...], b_ref[...], contract))
            return
        acc = rest[n_x + n_b + n_ro + n_ao]
        k = pl.program_id(1)

        @pl.when(k == 0)
        def _():
            acc[...] = _dot(a_ref[...], b_ref[...], contract)

        @pl.when(jnp.logical_and(k > 0, k < nk - 1))
        def _():
            acc[...] += _dot(a_ref[...], b_ref[...], contract)

        @pl.when(k == nk - 1)
        def _():
            finish(acc[...] + _dot(a_ref[...], b_ref[...], contract))

    in_specs = [a_spec, b_spec] + [pl.BlockSpec((tm, x.shape[1]), lambda i, k: (i, 0)) for x in row_extras]
    in_specs += [pl.BlockSpec(x.shape, lambda i, k: (0,) * x.ndim) for x in bcast]
    out_specs = [pl.BlockSpec((tm, w), lambda i, k: (i, 0)) for w, _ in row_outs]
    out_specs += [pl.BlockSpec((1, w), lambda i, k: (0, 0)) for w in acc_outs]
    out_shape = [jax.ShapeDtypeStruct((M, w), dt) for w, dt in row_outs]
    out_shape += [jax.ShapeDtypeStruct((1, w), F32) for w in acc_outs]
    outs, job_res = _carry_call(
        body, name=name, grid=grid, in_specs=in_specs, out_specs=out_specs, out_shape=out_shape,
        scratch_shapes=[pltpu.VMEM((tm, n), F32)] if nk > 1 else [],
        semantics=("arbitrary", "arbitrary"), args=[a, b] + list(row_extras) + list(bcast), jobs=jobs)
    res = (outs[:n_ro], outs[n_ro:])
    return (res, job_res) if jobs else res


def _mm_nn(name, a, b, out_dtypes, tm, tn, tk, extras=(), epilogue=None, jobs=()):
    M, K = a.shape
    N = b.shape[1]
    tm, tn, tk = min(tm, M), min(tn, N), min(tk, K)
    o_spec = pl.BlockSpec((tm, tn), lambda i, j, k: (i, j))
    return _mm(name, (M // tm, N // tn, K // tk),
               a, pl.BlockSpec((tm, tk), lambda i, j, k: (i, k)),
               b, pl.BlockSpec((tk, tn), lambda i, j, k: (k, j)), NN, (tm, tn),
               [jax.ShapeDtypeStruct((M, N), dt) for dt in out_dtypes], [o_spec] * len(out_dtypes),
               [(e, o_spec) for e in extras], epilogue, jobs)


def _mm_nt(name, a, b, out_dtypes, tm, tn, tk, extras=(), epilogue=None, jobs=()):
    M, K = a.shape
    N = b.shape[0]
    tm, tn, tk = min(tm, M), min(tn, N), min(tk, K)
    o_spec = pl.BlockSpec((tm, tn), lambda i, j, k: (i, j))
    return _mm(name, (M // tm, N // tn, K // tk),
               a, pl.BlockSpec((tm, tk), lambda i, j, k: (i, k)),
               b, pl.BlockSpec((tn, tk), lambda i, j, k: (j, k)), NT, (tm, tn),
               [jax.ShapeDtypeStruct((M, N), dt) for dt in out_dtypes], [o_spec] * len(out_dtypes),
               [(e, o_spec) for e in extras], epilogue, jobs)


def _mm_tn(name, a, b, out_dtypes, tm, tn, tk, jobs=()):
    K, M = a.shape
    N = b.shape[1]
    tm, tn, tk = min(tm, M), min(tn, N), min(tk, K)
    o_spec = pl.BlockSpec((tm, tn), lambda i, j, k: (i, j))
    return _mm(name, (M // tm, N // tn, K // tk),
               a, pl.BlockSpec((tk, tm), lambda i, j, k: (k, i)),
               b, pl.BlockSpec((tk, tn), lambda i, j, k: (k, j)), TN, (tm, tn),
               [jax.ShapeDtypeStruct((M, N), dt) for dt in out_dtypes], [o_spec] * len(out_dtypes),
               epilogue=lambda acc: (acc,) * len(out_dtypes), jobs=jobs)


def _fgate_fwd(zf, bf):
    S = zf.shape[0]
    nc = S // CHUNK

    def body(zf_ref, bf_ref, f_ref):
        upper = (_iota2((CHUNK, CHUNK), 0) <= _iota2((CHUNK, CHUNK), 1)).astype(BF16)
        carry = jnp.zeros((8, 1), F32)
        for c in range(nc):
            t = zf_ref[c * CHUNK:(c + 1) * CHUNK, :] + bf_ref[...]
            lf = jnp.minimum(t, 0.0) - jnp.log(1.0 + jnp.exp(-jnp.abs(t)))
            lf_rows = lf.T[0:8, :]
            f_ref[:, c * CHUNK:(c + 1) * CHUNK] = (_dot3(lf_rows, upper) + carry) * LOG2E
            carry = carry + jnp.sum(lf_rows, axis=-1, keepdims=True)

    return pl.pallas_call(
        body, name="fgate_fwd", out_shape=jax.ShapeDtypeStruct((8, S), F32),
        compiler_params=_params(),
    )(zf, bf)


def _fgate_bwd(df, zf, bf):
    S = zf.shape[0]
    nc = S // CHUNK

    def body(df_ref, zf_ref, bf_ref, dzf_ref, dbf_ref):
        lower = (_iota2((CHUNK, CHUNK), 0) >= _iota2((CHUNK, CHUNK), 1)).astype(BF16)
        carry = jnp.zeros((8, 1), F32)
        dbf = jnp.zeros((1, LANES), F32)
        for c in reversed(range(nc)):
            sl = slice(c * CHUNK, (c + 1) * CHUNK)
            df = df_ref[:, sl]
            r = _dot3(df, lower) + carry
            carry = carry + jnp.sum(df, axis=-1, keepdims=True)
            r_cols = jnp.concatenate([r, jnp.zeros((CHUNK - 8, CHUNK), F32)], axis=0).T
            t = zf_ref[sl, :] + bf_ref[...]
            dz = r_cols * (1.0 / (1.0 + jnp.exp(t)))
            dzf_ref[sl, :] = dz.astype(BF16)
            dbf = dbf + jnp.sum(dz, axis=0, keepdims=True)
        dbf_ref[...] = dbf

    return pl.pallas_call(
        body, name="fgate_bwd",
        out_shape=[jax.ShapeDtypeStruct((S, LANES), BF16), jax.ShapeDtypeStruct((1, LANES), F32)],
        compiler_params=_params(),
    )(df, zf, bf)


_NEG = -1e30
LOG2E = 1.4426950408889634
N_SPLIT = 8
N_SPLIT_DIAG = 4
DIAG_STEP = 1024


def _attn_consts(T):
    rows, cols = _iota2((T, T), 0), _iota2((T, T), 1)
    return cols <= rows, rows <= cols


def _attn2_fwd(zm, f2col, f2row, T, jobs=()):
    S = zm.shape[0]
    H = f2col.shape[0]
    nb = S // T
    c2 = LOG2E / math.sqrt(HEAD_DIM)

    def body(q_ref, k_ref, v_ref, fq_ref, fk_ref, o_ref, lse_ref, vaug_s):
        i = pl.program_id(1)

        @pl.when(i == 0)
        def _():
            vaug_s[:, :HEAD_DIM] = v_ref[...]
            vaug_s[:, HEAD_DIM:] = jnp.ones((S, HEAD_DIM), BF16)

        keep = _attn_consts(T)[0]
        TH = T // N_SPLIT

        def block(j, diagonal, state):
            r0 = pl.multiple_of(j * T, T)
            fk = fk_ref[j]
            new = []
            for g, (m_old, acc) in enumerate(state):
                rows = slice(g * TH, (g + 1) * TH)
                nk = min(T, -(-(g + 1) * TH // DIAG_STEP) * DIAG_STEP) if diagonal else T
                s = _dot(q_ref[rows, :], k_ref[pl.ds(r0, nk), :], NT) * c2 + (fq_ref[rows, :] - fk[:, :nk])
                if diagonal:
                    s = jnp.where(keep[rows, :nk], s, _NEG)
                m_new = jnp.maximum(m_old, jnp.max(s, axis=-1, keepdims=True))
                p = jnp.exp2(s - m_new).astype(BF16)
                new.append((m_new, jnp.exp2(m_old - m_new) * acc + _dot(p, vaug_s[pl.ds(r0, nk), :])))
            return tuple(new)

        init = tuple((jnp.full((TH, 1), _NEG, F32), jnp.zeros((TH, 2 * HEAD_DIM), F32)) for _ in range(N_SPLIT))
        state = lax.fori_loop(0, i, lambda j, st: block(j, False, st), init)
        state = block(i, True, state)
        for g, (m, acc) in enumerate(state):
            rows = slice(g * TH, (g + 1) * TH)
            o_ref[rows, :] = acc[:, :HEAD_DIM] / acc[:, HEAD_DIM:]
            lse_ref[rows, :] = m + jnp.log2(acc[:, HEAD_DIM:HEAD_DIM + 1])

    nh = H
    return _carry_call(
        body, name="attn_fwd", grid=(H, nb), jobs=jobs, args=[zm, zm, zm, f2col, f2row],
        semantics=("arbitrary", "arbitrary"),
        in_specs=[
            pl.BlockSpec((T, HEAD_DIM), lambda h, i: (i, h)),
            pl.BlockSpec((S, HEAD_DIM), lambda h, i: (0, nh + h)),
            pl.BlockSpec((S, HEAD_DIM), lambda h, i: (0, 2 * nh + h)),
            pl.BlockSpec((None, T, 1), lambda h, i: (h, i, 0)),
            pl.BlockSpec((None, nb, 1, T), lambda h, i: (h, 0, 0, 0)),
        ],
        out_specs=[pl.BlockSpec((T, HEAD_DIM), lambda h, i: (i, h)), pl.BlockSpec((None, T, 1), lambda h, i: (h, i, 0))],
        out_shape=[jax.ShapeDtypeStruct((S, H * HEAD_DIM), F32), jax.ShapeDtypeStruct((H, S, 1), F32)],
        scratch_shapes=[pltpu.VMEM((S, 2 * HEAD_DIM), BF16)],
    )


def _attn2_bwd_dq(zm, dattn, f2col, f2row, lse2_col, delta_col, T, jobs=()):
    S = zm.shape[0]
    H = f2col.shape[0]
    nb = S // T
    scale = 1.0 / math.sqrt(HEAD_DIM)
    c2 = LOG2E * scale

    def body(q_ref, k_ref, v_ref, do_ref, fq_ref, fk_ref, lse_ref, dl_ref, dq_ref, rs_ref, bias_s, do_s):
        i = pl.program_id(1)
        keep = _attn_consts(T)[0]
        TH = T // N_SPLIT_DIAG
        bias_s[...] = fq_ref[...] - lse_ref[...]
        do_s[...] = do_ref[...].astype(BF16)

        def part(rows, j, nk, state, masked):
            acc, rs = state
            r0 = pl.multiple_of(j * T, T)
            kb = k_ref[pl.ds(r0, nk), :]
            s = _dot(q_ref[rows, :], kb, NT) * c2 + (bias_s[rows, :] - fk_ref[j][:, :nk])
            if masked:
                s = jnp.where(keep[rows, :nk], s, _NEG)
            ds = jnp.exp2(s) * (_dot(do_s[rows, :], v_ref[pl.ds(r0, nk), :], NT) - dl_ref[rows, :])
            return acc + _dot(ds.astype(BF16), kb), rs + jnp.sum(ds, axis=-1, keepdims=True)

        def step(j, state):
            return part(slice(0, T), j, T, state, False)

        acc, rs = lax.fori_loop(0, i, step, (jnp.zeros((T, HEAD_DIM), F32), jnp.zeros((T, 1), F32)))
        for g in range(N_SPLIT_DIAG):
            rows = slice(g * TH, (g + 1) * TH)
            acc_g, rs_g = part(rows, i, (g + 1) * TH, (acc[rows, :], rs[rows, :]), True)
            dq_ref[rows, :] = (acc_g * scale).astype(BF16)
            rs_ref[rows, :] = rs_g

    nh = H
    col = pl.BlockSpec((None, T, 1), lambda h, i: (h, i, 0))
    blk = pl.BlockSpec((T, HEAD_DIM), lambda h, i: (i, h))
    return _carry_call(
        body, name="attn_bwd_dq", grid=(H, nb), jobs=jobs,
        args=[zm, zm, zm, dattn, f2col, f2row, lse2_col, delta_col], semantics=("arbitrary", "arbitrary"),
        in_specs=[
            blk,
            pl.BlockSpec((S, HEAD_DIM), lambda h, i: (0, nh + h)),
            pl.BlockSpec((S, HEAD_DIM), lambda h, i: (0, 2 * nh + h)),
            blk, col,
            pl.BlockSpec((None, nb, 1, T), lambda h, i: (h, 0, 0, 0)),
            col, col,
        ],
        out_specs=[blk, col],
        out_shape=[jax.ShapeDtypeStruct((S, H * HEAD_DIM), BF16), jax.ShapeDtypeStruct((H, S, 1), F32)],
        scratch_shapes=[pltpu.VMEM((T, 1), F32), pltpu.VMEM((T, HEAD_DIM), BF16)],
    )


def _attn2_bwd_dkv(zm, dattn, f2col, f2row, lse2_row, delta_row, rowsum_row, T, jobs=()):
    S = zm.shape[0]
    H = f2col.shape[0]
    nb = S // T
    scale = 1.0 / math.sqrt(HEAD_DIM)
    c2 = LOG2E * scale

    def body(q_ref, k_ref, v_ref, do_ref, fk_ref, fq_ref, lse_ref, dl_ref, rs_ref, dk_ref, dv_ref, df_ref):
        j = pl.program_id(1)
        keep = _attn_consts(T)[1]
        TH = T // N_SPLIT_DIAG

        def part(rows, i, c0, state, masked):
            dk, dv, df = state
            r0 = pl.multiple_of(i * T + c0, TH)
            qb = q_ref[pl.ds(r0, T - c0), :]
            do = do_ref[pl.ds(r0, T - c0), :].astype(BF16)
            bias = (fq_ref[i] - lse_ref[i])[:, c0:]
            dl = (dl_ref[i] + rs_ref[i])[:, c0:]
            st = _dot(k_ref[rows, :], qb, NT) * c2 + (bias - fk_ref[rows, :])
            if masked:
                st = jnp.where(keep[rows, c0:], st, _NEG)
            pt = jnp.exp2(st)
            dst = pt * (_dot(v_ref[rows, :], do, NT) - dl)
            return (dk + _dot(dst.astype(BF16), qb), dv + _dot(pt.astype(BF16), do),
                    df - jnp.sum(dst, axis=-1, keepdims=True))

        groups = []
        for g in range(N_SPLIT_DIAG):
            zero = (jnp.zeros((TH, HEAD_DIM), F32), jnp.zeros((TH, HEAD_DIM), F32), jnp.zeros((TH, 1), F32))
            groups.append(part(slice(g * TH, (g + 1) * TH), j, g * TH, zero, True))
        state = tuple(jnp.concatenate([grp[n] for grp in groups], axis=0) for n in range(3))
        dk, dv, df = lax.fori_loop(j + 1, nb, lambda i, st: part(slice(0, T), i, 0, st, False), state)
        dk_ref[...] = (dk * scale).astype(BF16)
        dv_ref[...] = dv.astype(BF16)
        df_ref[...] = df

    nh = H
    row = pl.BlockSpec((None, nb, 1, T), lambda h, j: (h, 0, 0, 0))
    whole = pl.BlockSpec((S, HEAD_DIM), lambda h, j: (0, h))
    kv_out = pl.BlockSpec((T, HEAD_DIM), lambda h, j: (j, h))
    col = pl.BlockSpec((None, T, 1), lambda h, j: (h, j, 0))
    return _carry_call(
        body, name="attn_bwd_dkv", grid=(H, nb), jobs=jobs,
        args=[zm, zm, zm, dattn, f2col, f2row, lse2_row, delta_row, rowsum_row],
        semantics=("arbitrary", "arbitrary"),
        in_specs=[
            whole,
            pl.BlockSpec((T, HEAD_DIM), lambda h, j: (j, nh + h)),
            pl.BlockSpec((T, HEAD_DIM), lambda h, j: (j, 2 * nh + h)),
            whole, col, row, row, row, row,
        ],
        out_specs=[kv_out, kv_out, col],
        out_shape=[jax.ShapeDtypeStruct((S, H * HEAD_DIM), BF16), jax.ShapeDtypeStruct((S, H * HEAD_DIM), BF16),
                   jax.ShapeDtypeStruct((H, S, 1), F32)],
        scratch_shapes=[],
    )


def _attn_fwd(zm, fcol, frow, T, jobs=()):
    S = zm.shape[0]
    H = fcol.shape[0]
    nb = S // T
    scale = 1.0 / math.sqrt(HEAD_DIM)

    def body(q_ref, k_ref, v_ref, fq_ref, fk_ref, o_ref, lse_ref, m_s, l_s, acc_s):
        i = pl.program_id(1)
        j = pl.program_id(2)

        @pl.when(j == 0)
        def _():
            m_s[...] = jnp.full_like(m_s, _NEG)
            l_s[...] = jnp.zeros_like(l_s)
            acc_s[...] = jnp.zeros_like(acc_s)

        @pl.when(j <= i)
        def _():
            s = _dot(q_ref[...], k_ref[...], NT) * scale + (fq_ref[...] - fk_ref[...])
            keep = (_iota2((T, T), 1) + j * T) <= (_iota2((T, T), 0) + i * T)
            s = jnp.where(keep, s, _NEG)
            m_new = jnp.maximum(m_s[...], jnp.max(s, axis=-1, keepdims=True))
            alpha = jnp.exp(m_s[...] - m_new)
            p = jnp.exp(s - m_new)
            l_s[...] = alpha * l_s[...] + jnp.sum(p, axis=-1, keepdims=True)
            acc_s[...] = alpha * acc_s[...] + _dot(p.astype(BF16), v_ref[...])
            m_s[...] = m_new

        @pl.when(j == nb - 1)
        def _():
            o_ref[...] = acc_s[...] / l_s[...]
            lse_ref[...] = m_s[...] + jnp.log(l_s[...])

    nh = H
    return _carry_call(
        body, name="attn_fwd", grid=(H, nb, nb), jobs=jobs, args=[zm, zm, zm, fcol, frow],
        semantics=("parallel", "parallel", "arbitrary"),
        in_specs=[
            pl.BlockSpec((T, HEAD_DIM), lambda h, i, j: (i, h)),
            pl.BlockSpec((T, HEAD_DIM), lambda h, i, j: (jnp.minimum(j, i), nh + h)),
            pl.BlockSpec((T, HEAD_DIM), lambda h, i, j: (jnp.minimum(j, i), 2 * nh + h)),
            pl.BlockSpec((None, T, 1), lambda h, i, j: (h, i, 0)),
            pl.BlockSpec((None, 1, T), lambda h, i, j: (h, 0, jnp.minimum(j, i))),
        ],
        out_specs=[
            pl.BlockSpec((T, HEAD_DIM), lambda h, i, j: (i, h)),
            pl.BlockSpec((None, T, 1), lambda h, i, j: (h, i, 0)),
        ],
        out_shape=[jax.ShapeDtypeStruct((S, H * HEAD_DIM), F32), jax.ShapeDtypeStruct((H, S, 1), F32)],
        scratch_shapes=[pltpu.VMEM((T, 1), F32), pltpu.VMEM((T, 1), F32), pltpu.VMEM((T, HEAD_DIM), F32)],
    )


def _attn_delta(dattn, attn, tr):
    S, DA = attn.shape
    H = DA // HEAD_DIM

    def body(do_ref, o_ref, out_ref):
        lo = _iota2((DA, LANES), 1) * HEAD_DIM
        sel = ((_iota2((DA, LANES), 0) >= lo) & (_iota2((DA, LANES), 0) < lo + HEAD_DIM)).astype(BF16)
        d = _dot3(do_ref[...] * o_ref[...], sel)
        for c in range(tr // CHUNK):
            out_ref[:, c * CHUNK:(c + 1) * CHUNK] = d[c * CHUNK:(c + 1) * CHUNK, :].T[0:H, :]

    return pl.pallas_call(
        body, name="attn_delta", grid=(S // tr,),
        in_specs=[pl.BlockSpec((tr, DA), lambda i: (i, 0))] * 2,
        out_specs=pl.BlockSpec((H, tr), lambda i: (0, i)),
        out_shape=jax.ShapeDtypeStruct((H, S), F32),
        compiler_params=_params(("parallel",)),
    )(dattn, attn)


def _attn_bwd_dq(zm, dattn, fcol, frow, lse_col, delta_col, T, jobs=()):
    S = zm.shape[0]
    H = fcol.shape[0]
    nb = S // T
    scale = 1.0 / math.sqrt(HEAD_DIM)

    def body(q_ref, k_ref, v_ref, do_ref, fq_ref, fk_ref, lse_ref, dl_ref, dq_ref, rs_ref, acc_s, rs_s):
        i = pl.program_id(1)
        j = pl.program_id(2)

        @pl.when(j == 0)
        def _():
            acc_s[...] = jnp.zeros_like(acc_s)
            rs_s[...] = jnp.zeros_like(rs_s)

        @pl.when(j <= i)
        def _():
            s = _dot(q_ref[...], k_ref[...], NT) * scale + (fq_ref[...] - fk_ref[...])
            keep = (_iota2((T, T), 1) + j * T) <= (_iota2((T, T), 0) + i * T)
            p = jnp.exp(jnp.where(keep, s - lse_ref[...], _NEG))
            dp = _dot(do_ref[...].astype(BF16), v_ref[...], NT)
            ds = p * (dp - dl_ref[...])
            acc_s[...] += _dot(ds.astype(BF16), k_ref[...])
            rs_s[...] += jnp.sum(ds, axis=-1, keepdims=True)

        @pl.when(j == nb - 1)
        def _():
            dq_ref[...] = (acc_s[...] * scale).astype(BF16)
            rs_ref[...] = rs_s[...]

    nh = H
    col = pl.BlockSpec((None, T, 1), lambda h, i, j: (h, i, 0))
    return _carry_call(
        body, name="attn_bwd_dq", grid=(H, nb, nb), jobs=jobs,
        args=[zm, zm, zm, dattn, fcol, frow, lse_col, delta_col], semantics=("parallel", "parallel", "arbitrary"),
        in_specs=[
            pl.BlockSpec((T, HEAD_DIM), lambda h, i, j: (i, h)),
            pl.BlockSpec((T, HEAD_DIM), lambda h, i, j: (jnp.minimum(j, i), nh + h)),
            pl.BlockSpec((T, HEAD_DIM), lambda h, i, j: (jnp.minimum(j, i), 2 * nh + h)),
            pl.BlockSpec((T, HEAD_DIM), lambda h, i, j: (i, h)),
            col,
            pl.BlockSpec((None, 1, T), lambda h, i, j: (h, 0, jnp.minimum(j, i))),
            col, col,
        ],
        out_specs=[pl.BlockSpec((T, HEAD_DIM), lambda h, i, j: (i, h)), col],
        out_shape=[jax.ShapeDtypeStruct((S, H * HEAD_DIM), BF16), jax.ShapeDtypeStruct((H, S, 1), F32)],
        scratch_shapes=[pltpu.VMEM((T, HEAD_DIM), F32), pltpu.VMEM((T, 1), F32)],
    )


def _attn_bwd_dkv(zm, dattn, fcol, frow, lse_row, delta_row, rowsum_row, T, jobs=()):
    S = zm.shape[0]
    H = fcol.shape[0]
    nb = S // T
    scale = 1.0 / math.sqrt(HEAD_DIM)

    def body(q_ref, k_ref, v_ref, do_ref, fk_ref, fq_ref, lse_ref, dl_ref, rs_ref,
             dk_ref, dv_ref, df_ref, dk_s, dv_s, df_s):
        j = pl.program_id(1)
        i = pl.program_id(2)

        @pl.when(i == 0)
        def _():
            dk_s[...] = jnp.zeros_like(dk_s)
            dv_s[...] = jnp.zeros_like(dv_s)
            df_s[...] = jnp.zeros_like(df_s)

        @pl.when(i >= j)
        def _():
            st = _dot(k_ref[...], q_ref[...], NT) * scale + (fq_ref[...] - fk_ref[...])
            keep = (_iota2((T, T), 0) + j * T) <= (_iota2((T, T), 1) + i * T)
            pt = jnp.exp(jnp.where(keep, st - lse_ref[...], _NEG))
            do = do_ref[...].astype(BF16)
            dpt = _dot(v_ref[...], do, NT)
            dst = pt * (dpt - (dl_ref[...] + rs_ref[...]))
            dv_s[...] += _dot(pt.astype(BF16), do)
            dk_s[...] += _dot(dst.astype(BF16), q_ref[...])
            df_s[...] -= jnp.sum(dst, axis=-1, keepdims=True)

        @pl.when(i == nb - 1)
        def _():
            dk_ref[...] = (dk_s[...] * scale).astype(BF16)
            dv_ref[...] = dv_s[...].astype(BF16)
            df_ref[...] = df_s[...]

    nh = H
    row = pl.BlockSpec((None, 1, T), lambda h, j, i: (h, 0, jnp.maximum(i, j)))
    kv_out = pl.BlockSpec((T, HEAD_DIM), lambda h, j, i: (j, h))
    return _carry_call(
        body, name="attn_bwd_dkv", grid=(H, nb, nb), jobs=jobs,
        args=[zm, zm, zm, dattn, fcol, frow, lse_row, delta_row, rowsum_row],
        semantics=("parallel", "parallel", "arbitrary"),
        in_specs=[
            pl.BlockSpec((T, HEAD_DIM), lambda h, j, i: (jnp.maximum(i, j), h)),
            pl.BlockSpec((T, HEAD_DIM), lambda h, j, i: (j, nh + h)),
            pl.BlockSpec((T, HEAD_DIM), lambda h, j, i: (j, 2 * nh + h)),
            pl.BlockSpec((T, HEAD_DIM), lambda h, j, i: (jnp.maximum(i, j), h)),
            pl.BlockSpec((None, T, 1), lambda h, j, i: (h, j, 0)),
            row, row, row, row,
        ],
        out_specs=[kv_out, kv_out, pl.BlockSpec((None, T, 1), lambda h, j, i: (h, j, 0))],
        out_shape=[jax.ShapeDtypeStruct((S, H * HEAD_DIM), BF16), jax.ShapeDtypeStruct((S, H * HEAD_DIM), BF16),
                   jax.ShapeDtypeStruct((H, S, 1), F32)],
        scratch_shapes=[pltpu.VMEM((T, HEAD_DIM), F32), pltpu.VMEM((T, HEAD_DIM), F32), pltpu.VMEM((T, 1), F32)],
    )


def _ln_stats(x):
    mu = jnp.mean(x, axis=-1, keepdims=True)
    xc = x - mu
    rstd = lax.rsqrt(jnp.mean(xc * xc, axis=-1, keepdims=True) + EPS)
    return xc * rstd, rstd


def _tril_mask():
    return _iota2((CHUNK, CHUNK), 0) >= _iota2((CHUNK, CHUNK), 1)


def _gmlp_fwd(zm, ln_g, ln_b, w_s, bs_col, tr):
    S = zm.shape[0]
    H = w_s.shape[0]
    DG = H * HEAD_DIM

    def body(zu_ref, zv_ref, g_ref, b_ref, w_ref, bs_ref, out_ref):
        u = _gelu(zu_ref[...].astype(F32))
        y, _ = _ln_stats(_gelu(zv_ref[...].astype(F32)))
        v = (y * g_ref[...] + b_ref[...]).astype(BF16)
        mask = _tril_mask()
        for h in range(H):
            wc = jnp.where(mask, w_ref[h], 0.0).astype(BF16)
            cs = slice(h * HEAD_DIM, (h + 1) * HEAD_DIM)
            for c in range(tr // CHUNK):
                rs = slice(c * CHUNK, (c + 1) * CHUNK)
                mix = _dot(wc, v[rs, cs]) + bs_ref[h]
                out_ref[rs, cs] = u[rs, cs] * mix

    full = lambda a: pl.BlockSpec(a.shape, lambda i: (0,) * a.ndim)
    return pl.pallas_call(
        body, name="gmlp_fwd", grid=(S // tr,),
        in_specs=[pl.BlockSpec((tr, DG), lambda i: (i, 3)), pl.BlockSpec((tr, DG), lambda i: (i, 4)),
                  full(ln_g), full(ln_b), full(w_s), full(bs_col)],
        out_specs=pl.BlockSpec((tr, DG), lambda i: (i, 0)),
        out_shape=jax.ShapeDtypeStruct((S, DG), F32),
        compiler_params=_params(("parallel",)),
    )(zm, zm, ln_g, ln_b, w_s, bs_col)


def _gmlp_bwd(dgm, zm, ln_g, ln_b, w_s, w_st, bs_col, tr):
    S = zm.shape[0]
    H = w_s.shape[0]
    DG = H * HEAD_DIM

    def body(dg_ref, zu_ref, zv_ref, g_ref, b_ref, w_ref, wt_ref, bs_ref,
             dzu_ref, dzv_ref, dw_ref, dbs_ref, dlg_ref, dlb_ref, dv_s):
        @pl.when(pl.program_id(0) == 0)
        def _():
            dw_ref[...] = jnp.zeros_like(dw_ref)
            dbs_ref[...] = jnp.zeros_like(dbs_ref)
            dlg_ref[...] = jnp.zeros_like(dlg_ref)
            dlb_ref[...] = jnp.zeros_like(dlb_ref)

        zu = zu_ref[...].astype(F32)
        zv = zv_ref[...].astype(F32)
        u = _gelu(zu)
        y, rstd = _ln_stats(_gelu(zv))
        v = (y * g_ref[...] + b_ref[...]).astype(BF16)
        dgm_blk = dg_ref[...]
        mask = _tril_mask()
        mask_t = _iota2((CHUNK, CHUNK), 0) <= _iota2((CHUNK, CHUNK), 1)
        for h in range(H):
            wc = jnp.where(mask, w_ref[h], 0.0).astype(BF16)
            wct = jnp.where(mask_t, wt_ref[h], 0.0).astype(BF16)
            cs = slice(h * HEAD_DIM, (h + 1) * HEAD_DIM)
            dw = jnp.zeros((CHUNK, CHUNK), F32)
            dbs = jnp.zeros((CHUNK, 1), F32)
            for c in range(tr // CHUNK):
                rs = slice(c * CHUNK, (c + 1) * CHUNK)
                vch = v[rs, cs]
                mix = _dot(wc, vch) + bs_ref[h]
                dg = dgm_blk[rs, cs]
                dzu_ref[rs, cs] = (dg * mix * _gelu_grad(zu[rs, cs])).astype(BF16)
                dmix = dg * u[rs, cs]
                dbs = dbs + jnp.sum(dmix, axis=-1, keepdims=True)
                dmix_b = dmix.astype(BF16)
                dw = dw + _dot(dmix_b, vch, NT)
                dv_s[rs, cs] = _dot(wct, dmix_b)
            dw_ref[h] += jnp.where(mask, dw, 0.0)
            dbs_ref[h] += dbs
        dv = dv_s[...]
        dlg_ref[...] += jnp.sum(dv * y, axis=0, keepdims=True)
        dlb_ref[...] += jnp.sum(dv, axis=0, keepdims=True)
        dy = dv * g_ref[...]
        dgv = rstd * (dy - jnp.mean(dy, axis=-1, keepdims=True) - y * jnp.mean(dy * y, axis=-1, keepdims=True))
        dzv_ref[...] = (dgv * _gelu_grad(zv)).astype(BF16)

    full = lambda a: pl.BlockSpec(a.shape, lambda i: (0,) * a.ndim)
    rows = pl.BlockSpec((tr, DG), lambda i: (i, 0))
    return pl.pallas_call(
        body, name="gmlp_bwd", grid=(S // tr,),
        in_specs=[rows, pl.BlockSpec((tr, DG), lambda i: (i, 3)), pl.BlockSpec((tr, DG), lambda i: (i, 4)),
                  full(ln_g), full(ln_b), full(w_s), full(w_st), full(bs_col)],
        out_specs=[rows, rows, full(w_s), full(bs_col), full(ln_g), full(ln_b)],
        out_shape=[jax.ShapeDtypeStruct((S, DG), BF16), jax.ShapeDtypeStruct((S, DG), BF16),
                   jax.ShapeDtypeStruct(w_s.shape, F32), jax.ShapeDtypeStruct(bs_col.shape, F32),
                   jax.ShapeDtypeStruct(ln_g.shape, F32), jax.ShapeDtypeStruct(ln_b.shape, F32)],
        scratch_shapes=[pltpu.VMEM((tr, DG), F32)],
        compiler_params=_params(("arbitrary",)),
    )(dgm, zm, zm, ln_g, ln_b, w_s, w_st, bs_col)


def _all_gather(name, blk):
    R, C = blk.shape

    def body(x_ref, out_ref, send_sems, recv_sems, local_sem):
        x, y, c = _me()
        me, sibling = (x, y, c), (x, y, 1 - c)
        chips = [(1 - x, y), (x, 1 - y), (1 - x, 1 - y)]

        def slab(px, py, pc):
            return out_ref.at[4 * px + 2 * py + pc]

        def copy(k, block, to, src=None):
            return pltpu.make_async_remote_copy(
                src_ref=slab(*block) if src is None else src, dst_ref=slab(*block),
                send_sem=send_sems.at[k], recv_sem=recv_sems.at[k], device_id=to, device_id_type=MESH)

        mine = pltpu.make_async_copy(x_ref, slab(*me), local_sem)
        mine.start()
        first = [copy(0, me, sibling, src=x_ref)]
        first += [copy(1 + n, me, (*chip, c), src=x_ref) for n, chip in enumerate(chips)]
        for cp in first:
            cp.start()
        passed = [copy(4 + n, (*chip, c), sibling) for n, chip in enumerate(chips)]
        for n, chip in enumerate(chips):
            copy(1 + n, (*chip, c), me).wait_recv()
            passed[n].start()
        copy(0, sibling, me).wait_recv()
        for n, chip in enumerate(chips):
            copy(4 + n, (*chip, 1 - c), me).wait_recv()
        for cp in first + passed:
            cp.wait_send()
        mine.wait()

    return pl.pallas_call(
        body, name=name, out_shape=jax.ShapeDtypeStruct((N_DEV, R, C), blk.dtype),
        in_specs=[_ANY], out_specs=_ANY,
        scratch_shapes=[pltpu.SemaphoreType.DMA((7,)), pltpu.SemaphoreType.DMA((7,)), pltpu.SemaphoreType.DMA(())],
    )(blk)


def _row_tile(R, C, itemsize=4, target_bytes=2 * 1024 * 1024):
    tr = R
    while tr % 2 == 0 and tr * C * itemsize > target_bytes and (tr // 2) % 16 == 0:
        tr //= 2
    return tr


def _rs_add1(name, g4, recv, c_idx):
    _, _, R, C = g4.shape
    tr = _row_tile(R, C)

    def body(c_ref, g_ref, r_ref, hb_ref):
        hb_ref[...] = (g_ref[...] + r_ref[...].astype(F32)).astype(BF16)

    blk = pl.BlockSpec((None, tr, C), lambda p, i, c_ref: (p, i, 0))
    return pl.pallas_call(
        body, name=name,
        grid_spec=pltpu.PrefetchScalarGridSpec(
            num_scalar_prefetch=1, grid=(4, R // tr),
            in_specs=[pl.BlockSpec((None, None, tr, C), lambda p, i, c_ref: (p, c_ref[0], i, 0)), blk],
            out_specs=blk),
        out_shape=jax.ShapeDtypeStruct((4, R, C), BF16),
        compiler_params=_params(("parallel", "parallel")),
    )(c_idx, g4, recv)


def _rs_add2_own(name, g4, recv1, recv2, c_idx, p_idx):
    _, _, R, C = g4.shape
    tr = _row_tile(R, C)

    def body(c_ref, p_ref, g_ref, r1_ref, r2_ref, out_ref):
        h = g_ref[...] + r1_ref[...].astype(F32)
        out_ref[...] = ((h + r2_ref[0].astype(F32)) + r2_ref[1].astype(F32)) + r2_ref[2].astype(F32)

    return pl.pallas_call(
        body, name=name,
        grid_spec=pltpu.PrefetchScalarGridSpec(
            num_scalar_prefetch=2, grid=(R // tr,),
            in_specs=[pl.BlockSpec((None, None, tr, C), lambda i, c_ref, p_ref: (p_ref[0], c_ref[0], i, 0)),
                      pl.BlockSpec((None, tr, C), lambda i, c_ref, p_ref: (p_ref[0], i, 0)),
                      pl.BlockSpec((3, tr, C), lambda i, c_ref, p_ref: (0, i, 0))],
            out_specs=pl.BlockSpec((tr, C), lambda i, c_ref, p_ref: (i, 0))),
        out_shape=jax.ShapeDtypeStruct((R, C), F32),
        compiler_params=_params(("parallel",)),
    )(c_idx, p_idx, g4, recv1, recv2)


def _rs_add1_windows(name, g, recv, first_blocks):
    _, R, W = recv.shape

    def body(t_ref, g_ref, r_ref, h_ref, hb_ref):
        h = g_ref[...] + r_ref[...].astype(F32)
        h_ref[...] = h
        hb_ref[...] = h.astype(BF16)

    blk = pl.BlockSpec((None, R, LANES), lambda p, l, t_ref: (p, 0, l))
    return pl.pallas_call(
        body, name=name,
        grid_spec=pltpu.PrefetchScalarGridSpec(
            num_scalar_prefetch=1, grid=(4, W // LANES),
            in_specs=[pl.BlockSpec((R, LANES), lambda p, l, t_ref: (0, t_ref[p] + l)), blk],
            out_specs=[blk, blk]),
        out_shape=[jax.ShapeDtypeStruct((4, R, W), F32), jax.ShapeDtypeStruct((4, R, W), BF16)],
        compiler_params=_params(("parallel", "parallel")),
    )(first_blocks, g, recv)


def _add_windows(name, windows, first, second, n_blocks):
    _, R, W = windows.shape
    dev1 = jnp.asarray([d for d, _ in first], jnp.int32)
    blk1 = jnp.asarray([b for _, b in first], jnp.int32)
    dev2 = jnp.asarray([max(d, 0) for d, _ in second], jnp.int32)
    blk2 = jnp.asarray([b for _, b in second], jnp.int32)
    two = jnp.asarray([int(d >= 0) for d, _ in second], jnp.int32)

    def body(d1_ref, b1_ref, d2_ref, b2_ref, two_ref, a_ref, b_ref, out_ref):
        k = pl.program_id(0)

        @pl.when(two_ref[k] == 0)
        def _():
            out_ref[...] = a_ref[...]

        @pl.when(two_ref[k] != 0)
        def _():
            out_ref[...] = a_ref[...] + b_ref[...]

    return pl.pallas_call(
        body, name=name,
        grid_spec=pltpu.PrefetchScalarGridSpec(
            num_scalar_prefetch=5, grid=(n_blocks,),
            in_specs=[pl.BlockSpec((None, R, LANES), lambda k, d1, b1, d2, b2, t: (d1[k], 0, b1[k])),
                      pl.BlockSpec((None, R, LANES), lambda k, d1, b1, d2, b2, t: (d2[k], 0, b2[k]))],
            out_specs=pl.BlockSpec((R, LANES), lambda k, d1, b1, d2, b2, t: (0, k))),
        out_shape=jax.ShapeDtypeStruct((R, n_blocks * LANES), windows.dtype),
        compiler_params=_params(("parallel",)),
    )(dev1, blk1, dev2, blk2, two, windows, windows)


def _rs_add2(name, h, recv, p_idx):
    _, R, C = h.shape
    tr = _row_tile(R, C)

    def body(p_ref, h_ref, r_ref, out_ref):
        out_ref[...] = ((h_ref[...] + r_ref[0].astype(F32)) + r_ref[1].astype(F32)) + r_ref[2].astype(F32)

    return pl.pallas_call(
        body, name=name,
        grid_spec=pltpu.PrefetchScalarGridSpec(
            num_scalar_prefetch=1, grid=(R // tr,),
            in_specs=[pl.BlockSpec((None, tr, C), lambda i, p_ref: (p_ref[0], i, 0)),
                      pl.BlockSpec((3, tr, C), lambda i, p_ref: (0, i, 0))],
            out_specs=pl.BlockSpec((tr, C), lambda i, p_ref: (i, 0))),
        out_shape=jax.ShapeDtypeStruct((R, C), F32),
        compiler_params=_params(("parallel",)),
    )(p_idx, h, recv)


def _sum8(name, g):
    _, R, C = g.shape

    def body(g_ref, out_ref):
        acc = g_ref[0]
        for d in range(1, N_DEV):
            acc = acc + g_ref[d]
        out_ref[...] = acc

    return pl.pallas_call(body, name=name, out_shape=jax.ShapeDtypeStruct((R, C), F32),
                          compiler_params=_params())(g)


def _adamw_math(w, g, m, v):
    m = ADAM_B1 * m + (1.0 - ADAM_B1) * g
    v = ADAM_B2 * v + (1.0 - ADAM_B2) * (g * g)
    m_hat = m / (1.0 - ADAM_B1 ** ADAM_STEP)
    v_hat = v / (1.0 - ADAM_B2 ** ADAM_STEP)
    delta = -ADAM_LR * (m_hat / (jnp.sqrt(v_hat) + ADAM_EPS) + ADAM_WD * w)
    return delta, m, v


def _adamw(name, w, g, m, v):
    R, C = w.shape
    tr = _row_tile(R, C, target_bytes=1024 * 1024)
    return _row_call(name, lambda *a: (_adamw_math(*a), ()), [w, g, m, v], [], [(C, F32)] * 3, [], tr)


def _adamw_from_window(name, w, m, v, window, gate, where):
    R, C = w.shape
    W = window.shape[1]
    tr = _row_tile(R, C, target_bytes=1024 * 1024)

    def body(p_ref, w_ref, m_ref, v_ref, win_ref, gate_ref, g_out, d_out, m_out, v_out):
        off, nb, hg = p_ref[0], p_ref[1], p_ref[2]
        r, c = _iota2((W, C), 0), _iota2((W, C), 1)
        pick = jnp.logical_or(jnp.logical_and(c < nb, r == c + off),
                              jnp.logical_and(c >= nb + hg, r == c - hg + off)).astype(BF16)
        r2, c2 = _iota2((LANES, C), 0), _iota2((LANES, C), 1)
        pick_gate = jnp.logical_and(r2 < hg, c2 == nb + r2).astype(BF16)
        g = _dot3(win_ref[...], pick) + _dot3(gate_ref[...], pick_gate)
        g_out[...] = g
        d_out[...], m_out[...], v_out[...] = _adamw_math(w_ref[...], g, m_ref[...], v_ref[...])

    blk = pl.BlockSpec((tr, C), lambda i, p: (i, 0))
    return pl.pallas_call(
        body, name=name,
        grid_spec=pltpu.PrefetchScalarGridSpec(
            num_scalar_prefetch=1, grid=(R // tr,),
            in_specs=[blk, blk, blk, pl.BlockSpec((tr, W), lambda i, p: (i, 0)),
                      pl.BlockSpec((tr, LANES), lambda i, p: (i, 0))],
            out_specs=[blk] * 4),
        out_shape=[jax.ShapeDtypeStruct((R, C), F32)] * 4,
        compiler_params=_params(("parallel",)),
    )(where, w, m, v, window, gate)


def _adamw_many(name, ws, gs, ms, vs):
    n = len(ws)

    def body(*refs):
        ins, outs = refs[:4 * n], refs[4 * n:]
        for k in range(n):
            res = _adamw_math(ins[k][...], ins[n + k][...], ins[2 * n + k][...], ins[3 * n + k][...])
            for t in range(3):
                outs[t * n + k][...] = res[t]

    out = pl.pallas_call(
        body, name=name, out_shape=[jax.ShapeDtypeStruct(w.shape, F32) for _ in range(3) for w in ws],
        compiler_params=_params(),
    )(*ws, *gs, *ms, *vs)
    return out[:n], out[n:2 * n], out[2 * n:]


def _pack(parts):
    flat = []
    total = 0
    for a in parts:
        n = math.prod(a.shape)
        flat.append(a.reshape(-1).astype(F32))
        if n % LANES:
            flat.append(jnp.zeros((-n % LANES,), F32))
        total += n + (-n % LANES)
    if total % (8 * LANES):
        flat.append(jnp.zeros((-total % (8 * LANES),), F32))
    return jnp.concatenate(flat).reshape(-1, LANES)


def _unpack(packed, shapes):
    out = []
    r = 0
    for shp in shapes:
        n = math.prod(shp)
        nr = -(-n // LANES)
        out.append(packed[r:r + nr].reshape(-1)[:n].reshape(shp))
        r += nr
    return out


def kernel(x, norm_mix_g, w_in, b_f, gmlp_ln_g, gmlp_ln_b, w_s, b_s, attn_out_g, gmlp_out_g, w_out, norm_ffn_g, w_ff1, w_ff2, norm_final_g, loss_target, m_norm_mix_g, m_w_in, m_b_f, m_gmlp_ln_g, m_gmlp_ln_b, m_w_s, m_b_s, m_attn_out_g, m_gmlp_out_g, m_w_out, m_norm_ffn_g, m_w_ff1, m_w_ff2, m_norm_final_g, v_norm_mix_g, v_w_in, v_b_f, v_gmlp_ln_g, v_gmlp_ln_b, v_w_s, v_b_s, v_attn_out_g, v_gmlp_out_g, v_w_out, v_norm_ffn_g, v_w_ff1, v_w_ff2, v_norm_final_g):
    S, D = x.shape[1], x.shape[2]
    H = b_f.shape[1]
    DA = H * HEAD_DIM
    DG = gmlp_ln_g.shape[1]
    DQKV = 3 * DA
    DMAIN = DQKV + 2 * DG
    DIN = DMAIN + H
    DFF = w_ff1.shape[2] * N_DEV
    w_in_cols = w_in.shape[2]
    assert DIN == w_in_cols * N_DEV and DA == DG and D == DA + DG

    T_ATT = min(T_ATT_MAX, S)
    TR = min(TR_MAX, S)

    x0 = x[0]
    tgt = loss_target[0]
    g_final = norm_final_g.reshape(1, D)

    FB = DFF // N_DEV
    x_pos, y_pos, c_pos = _me()
    me_idx = 4 * x_pos + 2 * y_pos + c_pos

    WW = -(-(w_in_cols + LANES - 1) // LANES) * LANES
    to_main = lambda col: col if col <= DQKV else max(DQKV, col - H)
    lo = [to_main(n * w_in_cols) for n in range(N_DEV)]
    hi = [to_main((n + 1) * w_in_cols) for n in range(N_DEV)]
    starts = [v // LANES * LANES for v in lo]
    gate_dev = DQKV // w_in_cols
    n_before = DQKV - gate_dev * w_in_cols
    g0 = lo[gate_dev] - starts[gate_dev]
    stash = -(-(g0 + w_in_cols - H) // LANES) * LANES
    assert all(hi[n] <= starts[n] + WW <= DMAIN for n in range(N_DEV))
    assert gate_dev * w_in_cols <= DQKV and DQKV + H <= (gate_dev + 1) * w_in_cols and stash + LANES <= WW
    shard = w_in[0].astype(BF16)

    def my_window(n):
        if n != gate_dev:
            return lambda s: jnp.pad(s, ((0, 0), (lo[n] - starts[n], WW - w_in_cols - (lo[n] - starts[n]))))
        return lambda s: jnp.concatenate([
            jnp.zeros((D, g0), BF16), s[:, :n_before], s[:, n_before + H:],
            jnp.zeros((D, stash - g0 - (w_in_cols - H)), BF16), s[:, n_before:n_before + H],
            jnp.zeros((D, WW - stash - H), BF16)], axis=1)
    windows = _all_gather("ag_w_in", lax.switch(me_idx, [my_window(n) for n in range(N_DEV)], shard))
    first, second = [], []
    for blk in range(DMAIN // LANES):
        c0 = blk * LANES
        owners = [(n, (c0 - starts[n]) // LANES) for n in range(N_DEV) if lo[n] < c0 + LANES and hi[n] > c0]
        assert 1 <= len(owners) <= 2
        first.append(owners[0])
        second.append(owners[1] if len(owners) == 2 else (-1, 0))
    w_main = _add_windows("w_in_windows", windows, first, second, DMAIN // LANES)
    w_f = windows[gate_dev, :, stash:stash + LANES]
    c_idx = jnp.reshape(c_pos, (1,)).astype(jnp.int32)
    p_idx = jnp.reshape(2 * x_pos + y_pos, (1,)).astype(jnp.int32)

    (h,), _ = _row_call("rms_mix", lambda xb, g: ((_rms_fwd(xb, g),), ()), [x0], [norm_mix_g], [(D, BF16)], [], TR)
    (zm,), ((w_out_part,),) = _mm_nn("in_proj", h, w_main, [BF16], 2048, 1024, 2048,
                                     jobs=[_job_gather_chips(w_out[0].astype(BF16))])
    (zf,) = _mm_nn("in_proj_f", h, w_f, [F32], 1024, LANES, 2048)
    bf_pad = jnp.pad(b_f, ((0, 0), (0, LANES - H)))
    f_row = _fgate_fwd(zf, bf_pad)
    NB = S // T_ATT
    f_col3 = f_row.reshape(H, S, 1)
    f_row3 = f_row.reshape(H, NB, 1, T_ATT)
    (attn, lse_col3), ((w_out_all,), (w_ff1_part,)) = _attn2_fwd(
        zm, f_col3, f_row3, T_ATT, jobs=[_job_gather_sibling(w_out_part), _job_gather_chips(w_ff1[0].astype(BF16))])
    w_out_full = w_out_all.reshape(D, D)
    bs_col = b_s[0].reshape(H, CHUNK, 1)
    gm = _gmlp_fwd(zm, gmlp_ln_g, gmlp_ln_b, w_s[0], bs_col, TR)

    def merge_fn(a, g, ga, gg):
        return (jnp.concatenate([_rms_fwd(a, ga), _rms_fwd(g, gg)], axis=1),), ()
    (merged,), _ = _row_call("rms_merge", merge_fn, [attn, gm], [attn_out_g, gmlp_out_g], [(D, BF16)], [], TR)

    w_ff2_b = w_ff2[0].astype(BF16)
    TMR = min(512, S)

    def out_proj_fn(acc, res, g):
        xb = acc + res
        return (xb, _rms_fwd(xb, g)), ()
    ((x1, h2), _), ((w_ff1_all,), (w_ff2_q1,)) = _mm_rows(
        "out_proj", (S // TMR, 1), merged, pl.BlockSpec((TMR, D), lambda i, k: (i, 0)),
        w_out_full, pl.BlockSpec((D, D), lambda i, k: (0, 0)), NN, TMR, D, [x0], [norm_ffn_g],
        [(D, F32), (D, BF16)], [], out_proj_fn,
        jobs=[_job_gather_sibling(w_ff1_part), _job_gather_chips(w_ff2_b, part=(0, 1, 4))])

    tm, tn, tk = min(1024, S), min(1024, FB), min(2048, D)
    tm1 = min(2048, S)
    o_spec = pl.BlockSpec((tm1, tn), lambda i, j, k: (i, j))

    def relu_sq(acc):
        a = jnp.maximum(acc, 0.0)
        return a, a * a
    nj = FB // tn
    ff2_rest = [_job_gather_chips(w_ff2_b, part=(1, 4, 4), into=w_ff2_q1)]
    (a_act, a_sq), ((w_ff2_q2,),) = _mm(
        "ff1", (S // tm1, DFF // tn, D // tk), h2, pl.BlockSpec((tm1, tk), lambda i, j, k: (i, k)),
        w_ff1_all, pl.BlockSpec((None, tk, tn), lambda i, j, k: (j // nj, k, j % nj)), NN, (tm1, tn),
        [jax.ShapeDtypeStruct((S, DFF), BF16)] * 2, [o_spec] * 2, epilogue=relu_sq, jobs=ff2_rest)
    (w_ff2_all,) = _run_jobs("ag_w_ff2_sibling", [_job_gather_sibling(w_ff2_q2)])[0]
    w_ff2_full = w_ff2_all.reshape(DFF, D)
    def head_fn(acc, res, t, g):
        xb = acc + res
        rstd = lax.rsqrt(jnp.mean(xb * xb, axis=-1, keepdims=True) + EPS)
        xhat = xb * rstd
        err = xhat * g - t
        loss = 0.5 * jnp.sum(jnp.mean(err * err, axis=-1, keepdims=True), axis=0, keepdims=True)
        dy = err * (1.0 / D)
        dg = jnp.sum(dy * xhat, axis=0, keepdims=True)
        dxhat = dy * g
        dx = rstd * (dxhat - xhat * jnp.mean(dxhat * xhat, axis=-1, keepdims=True))
        return (dx, dx), (dg, jnp.broadcast_to(loss, (1, LANES)))
    tk_ff2 = min(1024, DFF)
    (dx2, dx2_b), (dg_final, loss_part) = _mm_rows(
        "ff2", (S // TMR, DFF // tk_ff2), a_sq, pl.BlockSpec((TMR, tk_ff2), lambda i, k: (i, k)),
        w_ff2_full, pl.BlockSpec((tk_ff2, D), lambda i, k: (k, 0)), NN, TMR, D, [x1, tgt], [g_final],
        [(D, F32), (D, BF16)], [D, LANES], head_fn)

    (da,) = _mm_nt("ff2_dx", dx2_b, w_ff2_full, [BF16], 2048, 1024, 2048, extras=[a_act],
                   epilogue=lambda acc, a: (2.0 * a.astype(F32) * acc,))
    dw_ff2, dw_ff2_b = _mm_tn("ff2_dw", a_sq, dx2_b, [F32, BF16], 1024, 2048, 1024)
    tm2, tk2 = min(2048, D), min(1024, S)
    dw1_spec = pl.BlockSpec((None, tm2, FB), lambda i, j, k: (j, i, 0))
    (dw_ff1, dw_ff1_b), ((r1_ff2,),) = _mm(
        "ff1_dw", (D // tm2, DFF // FB, S // tk2), h2, pl.BlockSpec((tk2, tm2), lambda i, j, k: (k, i)),
        da, pl.BlockSpec((tk2, FB), lambda i, j, k: (k, j)), TN, (tm2, FB),
        [jax.ShapeDtypeStruct((N_DEV, D, FB), F32), jax.ShapeDtypeStruct((N_DEV, D, FB), BF16)], [dw1_spec] * 2,
        epilogue=lambda acc: (acc, acc), jobs=[_job_scatter_sibling(dw_ff2_b.reshape(4, 2, FB, D))])
    hb_ff2 = _rs_add1("rs_add1_w_ff2", dw_ff2.reshape(4, 2, FB, D), r1_ff2, c_idx)
    def ffn_bwd_fn(dh, xb, dres, g):
        dx, dg = _rms_bwd(dh, xb, g)
        dx = dx + dres
        return (dx, dx), (dg,)
    tkb = min(1024, FB)
    nkb = FB // tkb
    ((dx1, dx1_b), (dg_ffn,)), ((r2_ff2,), (r1_ff1,)) = _mm_rows(
        "ff1_dx", (S // TMR, DFF // tkb), da, pl.BlockSpec((TMR, tkb), lambda i, k: (i, k)),
        w_ff1_all, pl.BlockSpec((None, D, tkb), lambda i, k: (k // nkb, 0, k % nkb)), NT, TMR, D, [x1, dx2],
        [norm_ffn_g], [(D, F32), (D, BF16)], [D], ffn_bwd_fn,
        jobs=[_job_scatter_chips(hb_ff2), _job_scatter_sibling(dw_ff1_b.reshape(4, 2, D, FB))])
    g_w_ff2 = _rs_add2_own("rs_add2_w_ff2", dw_ff2.reshape(4, 2, FB, D), r1_ff2, r2_ff2, c_idx, p_idx)
    hb_ff1 = _rs_add1("rs_add1_w_ff1", dw_ff1.reshape(4, 2, D, FB), r1_ff1, c_idx)

    def merge_bwd_fn(dm, a, g, ga, gg):
        da_, dga = _rms_bwd(dm[:, :DA], a, ga)
        dg_, dgg = _rms_bwd(dm[:, DA:], g, gg)
        return (da_, dg_), (dga, dgg)
    (dattn, dgm), (dg_attn, dg_gmlp) = _mm_rows(
        "out_proj_dx", (S // TMR, 1), dx1_b, pl.BlockSpec((TMR, D), lambda i, k: (i, 0)),
        w_out_full, pl.BlockSpec((D, D), lambda i, k: (0, 0)), NT, TMR, D, [attn, gm], [attn_out_g, gmlp_out_g],
        [(DA, F32), (DG, F32)], [DA, DG], merge_bwd_fn)
    dw_out, dw_out_b = _mm_tn("out_proj_dw", merged, dx1_b, [F32, BF16], 2048, 1024, 1024)

    w_st = jnp.swapaxes(w_s[0], 1, 2)
    dzu, dzv, dw_s, dbs_col, dln_g, dln_b = _gmlp_bwd(dgm, zm, gmlp_ln_g, gmlp_ln_b, w_s[0], w_st, bs_col, TR)

    delta_row = _attn_delta(dattn, attn, TR)
    lse_row3 = lse_col3.reshape(H, NB, 1, T_ATT)
    (dq, ds_rowsum), ((r2_ff1,), (r1_out,)) = _attn2_bwd_dq(
        zm, dattn, f_col3, f_row3, lse_col3, delta_row.reshape(H, S, 1), T_ATT,
        jobs=[_job_scatter_chips(hb_ff1), _job_scatter_sibling(dw_out_b.reshape(4, 2, D // N_DEV, D))])
    g_w_ff1 = _rs_add2_own("rs_add2_w_ff1", dw_ff1.reshape(4, 2, D, FB), r1_ff1, r2_ff1, c_idx, p_idx)
    hb_out = _rs_add1("rs_add1_w_out", dw_out.reshape(4, 2, D // N_DEV, D), r1_out, c_idx)
    (dk, dv, df_col3), ((r2_out,),) = _attn2_bwd_dkv(
        zm, dattn, f_col3, f_row3, lse_row3, delta_row.reshape(H, NB, 1, T_ATT),
        ds_rowsum.reshape(H, NB, 1, T_ATT), T_ATT,
        jobs=[_job_scatter_chips(hb_out)])
    g_w_out = _rs_add2_own("rs_add2_w_out", dw_out.reshape(4, 2, D // N_DEV, D), r1_out, r2_out, c_idx, p_idx)
    dzf, dbf = _fgate_bwd(df_col3.reshape(H, S), zf, bf_pad)

    dz_main = jnp.concatenate([dq, dk, dv, dzu, dzv], axis=1)
    dw_main, dw_main_b = _mm_tn("in_proj_dw", h, dz_main, [F32, BF16], 2048, 1024, 1024)
    (dw_f,), ((r1_in,),) = _mm_tn("in_proj_f_dw", h, dzf, [F32], 2048, LANES, 1024,
                                  jobs=[_job_scatter_sibling_windows(dw_main_b, starts, WW)])
    first_blocks = jnp.stack([jnp.where(c_pos == 0, starts[2 * p], starts[2 * p + 1]) // LANES
                              for p in range(4)]).astype(jnp.int32)
    h_in, hb_in = _rs_add1_windows("rs_add1_w_in", dw_main, r1_in, first_blocks)

    def mix_bwd_fn(dh_main, dz_gate, xb, dres, g, w_gate):
        dx, dg = _rms_bwd(dh_main + _dot(dz_gate, w_gate, NT), xb, g)
        return (dx + dres,), (dg,)
    tk_in = min(1024, DMAIN)
    ((grad_x,), (dg_mix,)), ((r2_in,),) = _mm_rows(
        "in_proj_dx", (S // TMR, DMAIN // tk_in), dz_main, pl.BlockSpec((TMR, tk_in), lambda i, k: (i, k)),
        w_main, pl.BlockSpec((D, tk_in), lambda i, k: (0, k)), NT, TMR, D, [dzf, x0, dx1], [norm_mix_g, w_f],
        [(D, F32)], [D], mix_bwd_fn, jobs=[_job_scatter_chips(hb_in)])
    g_window = _rs_add2("rs_add2_w_in", h_in, r2_in, p_idx)

    small_shapes = [norm_mix_g.shape, b_f.shape, gmlp_ln_g.shape, gmlp_ln_b.shape, w_s.shape, b_s.shape,
                    attn_out_g.shape, gmlp_out_g.shape, norm_ffn_g.shape, norm_final_g.shape]
    small_parts = [dg_mix, dbf[:, :H], dln_g, dln_b, dw_s, dbs_col, dg_attn, dg_gmlp, dg_ffn, dg_final]
    g_small = _sum8("small_sum", _all_gather("ag_small", _pack(small_parts + [dw_f[:, :H], loss_part])))
    *gs, g_gate, loss_sum = _unpack(g_small, small_shapes + [(D, H), (1, LANES)])
    two_d = lambda a: a.reshape(1, -1) if a.ndim == 1 else a
    ds, nms, nvs = _adamw_many(
        "adamw_small",
        [two_d(a) for a in (norm_mix_g, b_f, gmlp_ln_g, gmlp_ln_b, w_s, b_s, attn_out_g, gmlp_out_g, norm_ffn_g,
                            norm_final_g)],
        [two_d(a) for a in gs],
        [two_d(a) for a in (m_norm_mix_g, m_b_f, m_gmlp_ln_g, m_gmlp_ln_b, m_w_s, m_b_s, m_attn_out_g, m_gmlp_out_g,
                            m_norm_ffn_g, m_norm_final_g)],
        [two_d(a) for a in (v_norm_mix_g, v_b_f, v_gmlp_ln_g, v_gmlp_ln_b, v_w_s, v_b_s, v_attn_out_g, v_gmlp_out_g,
                            v_norm_ffn_g, v_norm_final_g)])
    ds, nms, nvs = [[a.reshape(s) for a, s in zip(lst, small_shapes)] for lst in (ds, nms, nvs)]

    is_gate_dev = me_idx == gate_dev
    where = jnp.stack([sum(jnp.where(me_idx == n, lo[n] - starts[n], 0) for n in range(N_DEV)),
                       jnp.where(is_gate_dev, n_before, w_in_cols), jnp.where(is_gate_dev, H, 0)]).astype(jnp.int32)
    big = {"w_in": tuple(a[None] for a in _adamw_from_window(
        "adamw_w_in", w_in[0], m_w_in[0], v_w_in[0], g_window, jnp.pad(g_gate, ((0, 0), (0, LANES - H))), where))}
    for nm, w, g, m, v in (("w_out", w_out, g_w_out, m_w_out, v_w_out),
                           ("w_ff1", w_ff1, g_w_ff1, m_w_ff1, v_w_ff1), ("w_ff2", w_ff2, g_w_ff2, m_w_ff2, v_w_ff2)):
        (d_, m_, v_), _ = _adamw("adamw_" + nm, w[0], g, m[0], v[0])
        big[nm] = (g[None], d_[None], m_[None], v_[None])

    loss = loss_sum[0, 0]

    def leaves(n):
        sm = (gs, ds, nms, nvs)[n]
        return [sm[0], big["w_in"][n], sm[1], sm[2], sm[3], sm[4], sm[5], sm[6], sm[7], big["w_out"][n], sm[8],
                big["w_ff1"][n], big["w_ff2"][n], sm[9]]

    return (loss, grad_x[None], *leaves(0), *leaves(1), *leaves(2), *leaves(3))
```

```python
import functools
import math

import jax
import jax.numpy as jnp
from jax import lax
from jax.experimental import pallas as pl
from jax.experimental.pallas import tpu as pltpu

F32 = jnp.float32
BF16 = jnp.bfloat16
MESH = pl.DeviceIdType.MESH

HEAD_DIM = 128
CHUNK = 128
EPS = 1e-6
LANES = 128
N_DEV = 8

ADAM_LR = 0.001
ADAM_B1 = 0.9
ADAM_B2 = 0.999
ADAM_EPS = 1e-08
ADAM_WD = 0.01
ADAM_STEP = 10

VMEM_LIMIT_BYTES = 56 * 1024 * 1024
T_ATT_MAX = 1024
TR_MAX = 512

NN = ((1,), (0,))
NT = ((1,), (1,))
TN = ((0,), (0,))


def _params(sem=None):
    return pltpu.CompilerParams(dimension_semantics=sem, vmem_limit_bytes=VMEM_LIMIT_BYTES)


def _dot(a, b, contract=NN):
    return lax.dot_general(a, b, (contract, ((), ())), preferred_element_type=F32)


def _dot3(x, t):
    x1 = x.astype(BF16)
    r1 = x - x1.astype(F32)
    x2 = r1.astype(BF16)
    x3 = (r1 - x2.astype(F32)).astype(BF16)
    return _dot(x1, t) + _dot(x2, t) + _dot(x3, t)


def _iota2(shape, dim):
    return lax.broadcasted_iota(jnp.int32, shape, dim)


def _row_call(name, fn, row_ins, bcast_ins, row_outs, acc_outs, tr):
    S = row_ins[0].shape[0]
    assert S % tr == 0
    n_ri, n_bi, n_ro, n_ao = len(row_ins), len(bcast_ins), len(row_outs), len(acc_outs)

    def body(*refs):
        ins = [r[...] for r in refs[:n_ri + n_bi]]
        ro_refs = refs[n_ri + n_bi:n_ri + n_bi + n_ro]
        ao_refs = refs[n_ri + n_bi + n_ro:]
        ro, ao = fn(*ins)
        for r, v in zip(ro_refs, ro):
            r[...] = v.astype(r.dtype)
        if n_ao:
            @pl.when(pl.program_id(0) == 0)
            def _():
                for r in ao_refs:
                    r[...] = jnp.zeros_like(r)
            for r, v in zip(ao_refs, ao):
                r[...] += v

    in_specs = [pl.BlockSpec((tr, a.shape[1]), lambda i: (i, 0)) for a in row_ins]
    in_specs += [pl.BlockSpec(a.shape, lambda i: (0, 0)) for a in bcast_ins]
    out_specs = [pl.BlockSpec((tr, d), lambda i: (i, 0)) for d, _ in row_outs]
    out_specs += [pl.BlockSpec((1, d), lambda i: (0, 0)) for d in acc_outs]
    out_shape = [jax.ShapeDtypeStruct((S, d), dt) for d, dt in row_outs]
    out_shape += [jax.ShapeDtypeStruct((1, d), F32) for d in acc_outs]
    outs = pl.pallas_call(
        body, name=name, grid=(S // tr,), in_specs=in_specs, out_specs=out_specs, out_shape=out_shape,
        compiler_params=_params(("arbitrary",) if n_ao else ("parallel",)),
    )(*row_ins, *bcast_ins)
    return outs[:n_ro], outs[n_ro:]


def _rms_fwd(x, g):
    rstd = lax.rsqrt(jnp.mean(x * x, axis=-1, keepdims=True) + EPS)
    return x * rstd * g


def _rms_bwd(dy, x, g):
    rstd = lax.rsqrt(jnp.mean(x * x, axis=-1, keepdims=True) + EPS)
    xhat = x * rstd
    dg = jnp.sum(dy * xhat, axis=0, keepdims=True)
    dxhat = dy * g
    dx = rstd * (dxhat - xhat * jnp.mean(dxhat * xhat, axis=-1, keepdims=True))
    return dx, dg


_GELU_C = math.sqrt(2.0 / math.pi)


def _gelu(x):
    return 0.5 * x * (1.0 + jnp.tanh(_GELU_C * (x + 0.044715 * (x * x * x))))


def _gelu_grad(x):
    t = jnp.tanh(_GELU_C * (x + 0.044715 * (x * x * x)))
    return 0.5 * (1.0 + t) + 0.5 * x * (1.0 - t * t) * (_GELU_C * (1.0 + 3.0 * 0.044715 * (x * x)))


def _me():
    return lax.axis_index("x"), lax.axis_index("y"), lax.axis_index("c")


def _other_chips(x, y):
    return [(1 - x, y), (x, 1 - y), (1 - x, 1 - y)]


_ANY = pl.BlockSpec(memory_space=pl.ANY)


class _Job:
    def __init__(self, ins, outs, n_sems, make, aliases=None):
        self.ins, self.outs, self.n_sems, self.make, self.aliases = ins, outs, n_sems, make, aliases or {}


def _job_gather_chips(blk, part=(0, 1, 1), into=None):
    R, C = blk.shape
    nr = R // part[2]
    rows = pl.ds(part[0] * nr, (part[1] - part[0]) * nr)

    def make(ins, outs, send_sems, recv_sems, base):
        x_ref, (out_ref,) = ins[0], outs
        x, y, c = _me()
        mine = 4 * x + 2 * y + c
        targets = [(x, y, 1 - c)] + [(cx, cy, c) for cx, cy in _other_chips(x, y)]

        def copy(k, slab, to):
            return pltpu.make_async_remote_copy(
                src_ref=x_ref.at[rows, :], dst_ref=out_ref.at[slab, rows, :], send_sem=send_sems.at[base + k],
                recv_sem=recv_sems.at[base + k], device_id=to, device_id_type=MESH)

        starts = [copy(k, mine, to) for k, to in enumerate(targets)]
        arrivals = [copy(k, 4 * tx + 2 * ty + tc, (tx, ty, tc)) for k, (tx, ty, tc) in enumerate(targets)]
        local = [pltpu.make_async_copy(x_ref.at[rows, :], out_ref.at[mine, rows, :], send_sems.at[base + 4])]
        return starts, arrivals, local

    out = jax.ShapeDtypeStruct((N_DEV, R, C), blk.dtype)
    if into is None:
        return _Job([blk], [out], 5, make)
    return _Job([blk, into], [out], 5, make, aliases={1: 0})


def _job_gather_sibling(part):
    def make(ins, outs, send_sems, recv_sems, base):
        (out_ref,) = outs
        x, y, c = _me()

        def copy(k, slab):
            return pltpu.make_async_remote_copy(
                src_ref=out_ref.at[slab], dst_ref=out_ref.at[slab], send_sem=send_sems.at[base + k],
                recv_sem=recv_sems.at[base + k], device_id=(x, y, 1 - c), device_id_type=MESH)

        chips = _other_chips(x, y)
        starts = [copy(k, 4 * cx + 2 * cy + c) for k, (cx, cy) in enumerate(chips)]
        arrivals = [copy(k, 4 * cx + 2 * cy + (1 - c)) for k, (cx, cy) in enumerate(chips)]
        return starts, arrivals, []

    return _Job([part], [jax.ShapeDtypeStruct(part.shape, part.dtype)], 3, make, aliases={0: 0})


def _job_scatter_sibling(gb):
    _, _, R, C = gb.shape

    def make(ins, outs, send_sems, recv_sems, base):
        (g_ref,), (recv_ref,) = ins, outs
        x, y, c = _me()
        copies = [pltpu.make_async_remote_copy(
            src_ref=g_ref.at[p, 1 - c], dst_ref=recv_ref.at[p], send_sem=send_sems.at[base + p],
            recv_sem=recv_sems.at[base + p], device_id=(x, y, 1 - c), device_id_type=MESH) for p in range(4)]
        return copies, copies, []

    return _Job([gb], [jax.ShapeDtypeStruct((4, R, C), gb.dtype)], 4, make)


def _job_scatter_sibling_windows(gb, starts, width):
    R, _ = gb.shape

    def make(ins, outs, send_sems, recv_sems, base):
        (g_ref,), (recv_ref,) = ins, outs
        x, y, c = _me()
        copies = []
        for p in range(4):
            start = pl.multiple_of(jnp.where(c == 0, starts[2 * p + 1], starts[2 * p]), LANES)
            copies.append(pltpu.make_async_remote_copy(
                src_ref=g_ref.at[:, pl.ds(start, width)], dst_ref=recv_ref.at[p], send_sem=send_sems.at[base + p],
                recv_sem=recv_sems.at[base + p], device_id=(x, y, 1 - c), device_id_type=MESH))
        return copies, copies, []

    return _Job([gb], [jax.ShapeDtypeStruct((4, R, width), gb.dtype)], 4, make)


def _job_scatter_chips(hb, part=(0, 1, 1), into=None):
    _, R, C = hb.shape
    nr = R // part[2]
    rows = pl.ds(part[0] * nr, (part[1] - part[0]) * nr)

    def make(ins, outs, send_sems, recv_sems, base):
        h_ref, (recv_ref,) = ins[0], outs
        x, y, c = _me()
        copies = [pltpu.make_async_remote_copy(
            src_ref=h_ref.at[2 * cx + cy, rows, :], dst_ref=recv_ref.at[n, rows, :], send_sem=send_sems.at[base + n],
            recv_sem=recv_sems.at[base + n], device_id=(cx, cy, c), device_id_type=MESH)
            for n, (cx, cy) in enumerate(_other_chips(x, y))]
        return copies, copies, []

    out = jax.ShapeDtypeStruct((3, R, C), hb.dtype)
    if into is None:
        return _Job([hb], [out], 3, make)
    return _Job([hb, into], [out], 3, make, aliases={1: 0})


def _carry_call(body, *, name, grid, in_specs, out_specs, out_shape, scratch_shapes, semantics, args, jobs=()):
    jobs = list(jobs)
    n_in, n_out, n_scr = len(in_specs), len(out_specs), len(scratch_shapes)
    j_ins = [a for j in jobs for a in j.ins]
    j_outs = [o for j in jobs for o in j.outs]
    n_sems = sum(j.n_sems for j in jobs)
    aliases = {}
    i0, o0 = n_in, n_out
    for j in jobs:
        for a, b in j.aliases.items():
            aliases[i0 + a] = o0 + b
        i0 += len(j.ins)
        o0 += len(j.outs)

    def full_body(*refs):
        ins = refs[:n_in]
        jin = refs[n_in:n_in + len(j_ins)]
        outs = refs[n_in + len(j_ins):n_in + len(j_ins) + n_out]
        jout = refs[n_in + len(j_ins) + n_out:n_in + len(j_ins) + n_out + len(j_outs)]
        scr = refs[n_in + len(j_ins) + n_out + len(j_outs):]
        if jobs:
            send_sems, recv_sems = scr[n_scr], scr[n_scr + 1]
            starts, arrivals, local = [], [], []
            base = i0 = o0 = 0
            for j in jobs:
                s, a, l = j.make(jin[i0:i0 + len(j.ins)], jout[o0:o0 + len(j.outs)], send_sems, recv_sems, base)
                starts += s
                arrivals += a
                local += l
                base += j.n_sems
                i0 += len(j.ins)
                o0 += len(j.outs)
            pids = [pl.program_id(d) for d in range(len(grid))]
            first = functools.reduce(jnp.logical_and, [p == 0 for p in pids])
            last = functools.reduce(jnp.logical_and, [p == n - 1 for p, n in zip(pids, grid)])

            @pl.when(first)
            def _():
                for cp in local + starts:
                    cp.start()

        body(*ins, *outs, *scr[:n_scr])

        if jobs:
            @pl.when(last)
            def _():
                for cp in arrivals:
                    cp.wait_recv()
                for cp in starts:
                    cp.wait_send()
                for cp in local:
                    cp.wait()

    sems = [pltpu.SemaphoreType.DMA((n_sems,)), pltpu.SemaphoreType.DMA((n_sems,))] if jobs else []
    res = pl.pallas_call(
        full_body, name=name, grid=grid,
        in_specs=list(in_specs) + [_ANY] * len(j_ins),
        out_specs=list(out_specs) + [_ANY] * len(j_outs),
        out_shape=list(out_shape) + j_outs,
        scratch_shapes=list(scratch_shapes) + sems,
        input_output_aliases=aliases,
        compiler_params=_params(("arbitrary",) * len(grid) if jobs else semantics),
    )(*args, *j_ins)
    body_res, job_res = res[:n_out], res[n_out:]
    per_job = []
    for j in jobs:
        per_job.append(job_res[:len(j.outs)])
        job_res = job_res[len(j.outs):]
    return body_res, per_job


def _run_jobs(name, jobs):
    def body(done_ref):
        done_ref[...] = jnp.zeros_like(done_ref)

    return _carry_call(body, name=name, grid=(1,), in_specs=[], out_specs=[pl.BlockSpec((8, LANES), lambda i: (0, 0))],
                       out_shape=[jax.ShapeDtypeStruct((8, LANES), F32)], scratch_shapes=[], semantics=("arbitrary",),
                       args=[], jobs=jobs)[1]


def _mm(name, grid, a, a_spec, b, b_spec, contract, acc_shape, out_shape, out_specs, extras=(), epilogue=None, jobs=()):
    nk = grid[2]
    n_e = len(extras)
    n_o = len(out_shape)
    if epilogue is None:
        epilogue = lambda acc: (acc,)

    def body(a_ref, b_ref, *rest):
        e_refs = rest[:n_e]
        o_refs = rest[n_e:n_e + n_o]

        def finish(total):
            res = epilogue(total, *[r[...] for r in e_refs])
            for o, r in zip(o_refs, res):
                o[...] = r.astype(o.dtype)

        if nk == 1:
            finish(_dot(a_ref[...], b_ref[...], contract))
            return
        acc = rest[n_e + n_o]
        k = pl.program_id(2)

        @pl.when(k == 0)
        def _():
            acc[...] = _dot(a_ref[...], b_ref[...], contract)

        @pl.when(jnp.logical_and(k > 0, k < nk - 1))
        def _():
            acc[...] += _dot(a_ref[...], b_ref[...], contract)

        @pl.when(k == nk - 1)
        def _():
            finish(acc[...] + _dot(a_ref[...], b_ref[...], contract))

    outs, job_res = _carry_call(
        body, name=name, grid=grid, in_specs=[a_spec, b_spec] + [s for _, s in extras],
        out_specs=list(out_specs), out_shape=list(out_shape),
        scratch_shapes=[pltpu.VMEM(acc_shape, F32)] if nk > 1 else [],
        semantics=("parallel", "parallel", "arbitrary"), args=[a, b] + [e for e, _ in extras], jobs=jobs)
    return (outs, job_res) if jobs else outs


def _mm_rows(name, grid, a, a_spec, b, b_spec, contract, tm, n, row_extras, bcast, row_outs, acc_outs, epilogue, jobs=()):
    nk = grid[1]
    M = grid[0] * tm
    n_x, n_b, n_ro, n_ao = len(row_extras), len(bcast), len(row_outs), len(acc_outs)

    def body(a_ref, b_ref, *rest):
        x_refs = rest[:n_x + n_b]
        ro_refs = rest[n_x + n_b:n_x + n_b + n_ro]
        ao_refs = rest[n_x + n_b + n_ro:n_x + n_b + n_ro + n_ao]
        i = pl.program_id(0)

        def finish(total):
            ro, ao = epilogue(total, *[r[...] for r in x_refs])
            for r, v in zip(ro_refs, ro):
                r[...] = v.astype(r.dtype)
            if n_ao:
                @pl.when(i == 0)
                def _():
                    for r, v in zip(ao_refs, ao):
                        r[...] = v

                @pl.when(i > 0)
                def _():
                    for r, v in zip(ao_refs, ao):
                        r[...] += v

        if nk == 1:
            finish(_dot(a_ref[...], b_ref[...], contract))
            return
        acc = rest[n_x + n_b + n_ro + n_ao]
        k = pl.program_id(1)

        @pl.when(k == 0)
        def _():
            acc[...] = _dot(a_ref[...], b_ref[...], contract)

        @pl.when(jnp.logical_and(k > 0, k < nk - 1))
        def _():
            acc[...] += _dot(a_ref[...], b_ref[...], contract)

        @pl.when(k == nk - 1)
        def _():
            finish(acc[...] + _dot(a_ref[...], b_ref[...], contract))

    in_specs = [a_spec, b_spec] + [pl.BlockSpec((tm, x.shape[1]), lambda i, k: (i, 0)) for x in row_extras]
    in_specs += [pl.BlockSpec(x.shape, lambda i, k: (0,) * x.ndim) for x in bcast]
    out_specs = [pl.BlockSpec((tm, w), lambda i, k: (i, 0)) for w, _ in row_outs]
    out_specs += [pl.BlockSpec((1, w), lambda i, k: (0, 0)) for w in acc_outs]
    out_shape = [jax.ShapeDtypeStruct((M, w), dt) for w, dt in row_outs]
    out_shape += [jax.ShapeDtypeStruct((1, w), F32) for w in acc_outs]
    outs, job_res = _carry_call(
        body, name=name, grid=grid, in_specs=in_specs, out_specs=out_specs, out_shape=out_shape,
        scratch_shapes=[pltpu.VMEM((tm, n), F32)] if nk > 1 else [],
        semantics=("arbitrary", "arbitrary"), args=[a, b] + list(row_extras) + list(bcast), jobs=jobs)
    res = (outs[:n_ro], outs[n_ro:])
    return (res, job_res) if jobs else res


def _mm_nn(name, a, b, out_dtypes, tm, tn, tk, extras=(), epilogue=None, jobs=()):
    M, K = a.shape
    N = b.shape[1]
    tm, tn, tk = min(tm, M), min(tn, N), min(tk, K)
    o_spec = pl.BlockSpec((tm, tn), lambda i, j, k: (i, j))
    return _mm(name, (M // tm, N // tn, K // tk),
               a, pl.BlockSpec((tm, tk), lambda i, j, k: (i, k)),
               b, pl.BlockSpec((tk, tn), lambda i, j, k: (k, j)), NN, (tm, tn),
               [jax.ShapeDtypeStruct((M, N), dt) for dt in out_dtypes], [o_spec] * len(out_dtypes),
               [(e, o_spec) for e in extras], epilogue, jobs)


def _mm_nt(name, a, b, out_dtypes, tm, tn, tk, extras=(), epilogue=None, jobs=()):
    M, K = a.shape
    N = b.shape[0]
    tm, tn, tk = min(tm, M), min(tn, N), min(tk, K)
    o_spec = pl.BlockSpec((tm, tn), lambda i, j, k: (i, j))
    return _mm(name, (M // tm, N // tn, K // tk),
               a, pl.BlockSpec((tm, tk), lambda i, j, k: (i, k)),
               b, pl.BlockSpec((tn, tk), lambda i, j, k: (j, k)), NT, (tm, tn),
               [jax.ShapeDtypeStruct((M, N), dt) for dt in out_dtypes], [o_spec] * len(out_dtypes),
               [(e, o_spec) for e in extras], epilogue, jobs)


def _mm_tn(name, a, b, out_dtypes, tm, tn, tk, jobs=()):
    K, M = a.shape
    N = b.shape[1]
    tm, tn, tk = min(tm, M), min(tn, N), min(tk, K)
    o_spec = pl.BlockSpec((tm, tn), lambda i, j, k: (i, j))
    return _mm(name, (M // tm, N // tn, K // tk),
               a, pl.BlockSpec((tk, tm), lambda i, j, k: (k, i)),
               b, pl.BlockSpec((tk, tn), lambda i, j, k: (k, j)), TN, (tm, tn),
               [jax.ShapeDtypeStruct((M, N), dt) for dt in out_dtypes], [o_spec] * len(out_dtypes),
               epilogue=lambda acc: (acc,) * len(out_dtypes), jobs=jobs)


def _fgate_fwd(zf, bf):
    S = zf.shape[0]
    nc = S // CHUNK

    def body(zf_ref, bf_ref, f_ref):
        upper = (_iota2((CHUNK, CHUNK), 0) <= _iota2((CHUNK, CHUNK), 1)).astype(BF16)
        carry = jnp.zeros((8, 1), F32)
        for c in range(nc):
            t = zf_ref[c * CHUNK:(c + 1) * CHUNK, :] + bf_ref[...]
            lf = jnp.minimum(t, 0.0) - jnp.log(1.0 + jnp.exp(-jnp.abs(t)))
            lf_rows = lf.T[0:8, :]
            f_ref[:, c * CHUNK:(c + 1) * CHUNK] = (_dot3(lf_rows, upper) + carry) * LOG2E
            carry = carry + jnp.sum(lf_rows, axis=-1, keepdims=True)

    return pl.pallas_call(
        body, name="fgate_fwd", out_shape=jax.ShapeDtypeStruct((8, S), F32),
        compiler_params=_params(),
    )(zf, bf)


def _fgate_bwd(df, zf, bf):
    S = zf.shape[0]
    nc = S // CHUNK

    def body(df_ref, zf_ref, bf_ref, dzf_ref, dbf_ref):
        lower = (_iota2((CHUNK, CHUNK), 0) >= _iota2((CHUNK, CHUNK), 1)).astype(BF16)
        carry = jnp.zeros((8, 1), F32)
        dbf = jnp.zeros((1, LANES), F32)
        for c in reversed(range(nc)):
            sl = slice(c * CHUNK, (c + 1) * CHUNK)
            df = df_ref[:, sl]
            r = _dot3(df, lower) + carry
            carry = carry + jnp.sum(df, axis=-1, keepdims=True)
            r_cols = jnp.concatenate([r, jnp.zeros((CHUNK - 8, CHUNK), F32)], axis=0).T
            t = zf_ref[sl, :] + bf_ref[...]
            dz = r_cols * (1.0 / (1.0 + jnp.exp(t)))
            dzf_ref[sl, :] = dz.astype(BF16)
            dbf = dbf + jnp.sum(dz, axis=0, keepdims=True)
        dbf_ref[...] = dbf

    return pl.pallas_call(
        body, name="fgate_bwd",
        out_shape=[jax.ShapeDtypeStruct((S, LANES), BF16), jax.ShapeDtypeStruct((1, LANES), F32)],
        compiler_params=_params(),
    )(df, zf, bf)


_NEG = -1e30
LOG2E = 1.4426950408889634
N_SPLIT = 8
N_SPLIT_DIAG = 4
DIAG_STEP = 1024


def _attn_consts(T):
    rows, cols = _iota2((T, T), 0), _iota2((T, T), 1)
    return cols <= rows, rows <= cols


def _attn2_fwd(zm, f2col, f2row, T, jobs=()):
    S = zm.shape[0]
    H = f2col.shape[0]
    nb = S // T
    c2 = LOG2E / math.sqrt(HEAD_DIM)

    def body(q_ref, k_ref, v_ref, fq_ref, fk_ref, o_ref, lse_ref, vaug_s):
        i = pl.program_id(1)

        @pl.when(i == 0)
        def _():
            vaug_s[:, :HEAD_DIM] = v_ref[...]
            vaug_s[:, HEAD_DIM:] = jnp.ones((S, HEAD_DIM), BF16)

        keep = _attn_consts(T)[0]
        TH = T // N_SPLIT

        def block(j, diagonal, state):
            r0 = pl.multiple_of(j * T, T)
            fk = fk_ref[j]
            new = []
            for g, (m_old, acc) in enumerate(state):
                rows = slice(g * TH, (g + 1) * TH)
                nk = min(T, -(-(g + 1) * TH // DIAG_STEP) * DIAG_STEP) if diagonal else T
                s = _dot(q_ref[rows, :], k_ref[pl.ds(r0, nk), :], NT) * c2 + (fq_ref[rows, :] - fk[:, :nk])
                if diagonal:
                    s = jnp.where(keep[rows, :nk], s, _NEG)
                m_new = jnp.maximum(m_old, jnp.max(s, axis=-1, keepdims=True))
                p = jnp.exp2(s - m_new).astype(BF16)
                new.append((m_new, jnp.exp2(m_old - m_new) * acc + _dot(p, vaug_s[pl.ds(r0, nk), :])))
            return tuple(new)

        init = tuple((jnp.full((TH, 1), _NEG, F32), jnp.zeros((TH, 2 * HEAD_DIM), F32)) for _ in range(N_SPLIT))
        state = lax.fori_loop(0, i, lambda j, st: block(j, False, st), init)
        state = block(i, True, state)
        for g, (m, acc) in enumerate(state):
            rows = slice(g * TH, (g + 1) * TH)
            o_ref[rows, :] = acc[:, :HEAD_DIM] / acc[:, HEAD_DIM:]
            lse_ref[rows, :] = m + jnp.log2(acc[:, HEAD_DIM:HEAD_DIM + 1])

    nh = H
    return _carry_call(
        body, name="attn_fwd", grid=(H, nb), jobs=jobs, args=[zm, zm, zm, f2col, f2row],
        semantics=("arbitrary", "arbitrary"),
        in_specs=[
            pl.BlockSpec((T, HEAD_DIM), lambda h, i: (i, h)),
            pl.BlockSpec((S, HEAD_DIM), lambda h, i: (0, nh + h)),
            pl.BlockSpec((S, HEAD_DIM), lambda h, i: (0, 2 * nh + h)),
            pl.BlockSpec((None, T, 1), lambda h, i: (h, i, 0)),
            pl.BlockSpec((None, nb, 1, T), lambda h, i: (h, 0, 0, 0)),
        ],
        out_specs=[pl.BlockSpec((T, HEAD_DIM), lambda h, i: (i, h)), pl.BlockSpec((None, T, 1), lambda h, i: (h, i, 0))],
        out_shape=[jax.ShapeDtypeStruct((S, H * HEAD_DIM), F32), jax.ShapeDtypeStruct((H, S, 1), F32)],
        scratch_shapes=[pltpu.VMEM((S, 2 * HEAD_DIM), BF16)],
    )


def _attn2_bwd_dq(zm, dattn, f2col, f2row, lse2_col, delta_col, T, jobs=()):
    S = zm.shape[0]
    H = f2col.shape[0]
    nb = S // T
    scale = 1.0 / math.sqrt(HEAD_DIM)
    c2 = LOG2E * scale

    def body(q_ref, k_ref, v_ref, do_ref, fq_ref, fk_ref, lse_ref, dl_ref, dq_ref, rs_ref, bias_s, do_s):
        i = pl.program_id(1)
        keep = _attn_consts(T)[0]
        TH = T // N_SPLIT_DIAG
        bias_s[...] = fq_ref[...] - lse_ref[...]
        do_s[...] = do_ref[...].astype(BF16)

        def part(rows, j, nk, state, masked):
            acc, rs = state
            r0 = pl.multiple_of(j * T, T)
            kb = k_ref[pl.ds(r0, nk), :]
            s = _dot(q_ref[rows, :], kb, NT) * c2 + (bias_s[rows, :] - fk_ref[j][:, :nk])
            if masked:
                s = jnp.where(keep[rows, :nk], s, _NEG)
            ds = jnp.exp2(s) * (_dot(do_s[rows, :], v_ref[pl.ds(r0, nk), :], NT) - dl_ref[rows, :])
            return acc + _dot(ds.astype(BF16), kb), rs + jnp.sum(ds, axis=-1, keepdims=True)

        def step(j, state):
            return part(slice(0, T), j, T, state, False)

        acc, rs = lax.fori_loop(0, i, step, (jnp.zeros((T, HEAD_DIM), F32), jnp.zeros((T, 1), F32)))
        for g in range(N_SPLIT_DIAG):
            rows = slice(g * TH, (g + 1) * TH)
            acc_g, rs_g = part(rows, i, (g + 1) * TH, (acc[rows, :], rs[rows, :]), True)
            dq_ref[rows, :] = (acc_g * scale).astype(BF16)
            rs_ref[rows, :] = rs_g

    nh = H
    col = pl.BlockSpec((None, T, 1), lambda h, i: (h, i, 0))
    blk = pl.BlockSpec((T, HEAD_DIM), lambda h, i: (i, h))
    return _carry_call(
        body, name="attn_bwd_dq", grid=(H, nb), jobs=jobs,
        args=[zm, zm, zm, dattn, f2col, f2row, lse2_col, delta_col], semantics=("arbitrary", "arbitrary"),
        in_specs=[
            blk,
            pl.BlockSpec((S, HEAD_DIM), lambda h, i: (0, nh + h)),
            pl.BlockSpec((S, HEAD_DIM), lambda h, i: (0, 2 * nh + h)),
            blk, col,
            pl.BlockSpec((None, nb, 1, T), lambda h, i: (h, 0, 0, 0)),
            col, col,
        ],
        out_specs=[blk, col],
        out_shape=[jax.ShapeDtypeStruct((S, H * HEAD_DIM), BF16), jax.ShapeDtypeStruct((H, S, 1), F32)],
        scratch_shapes=[pltpu.VMEM((T, 1), F32), pltpu.VMEM((T, HEAD_DIM), BF16)],
    )


def _attn2_bwd_dkv(zm, dattn, f2col, f2row, lse2_row, delta_row, rowsum_row, T, jobs=()):
    S = zm.shape[0]
    H = f2col.shape[0]
    nb = S // T
    scale = 1.0 / math.sqrt(HEAD_DIM)
    c2 = LOG2E * scale

    def body(q_ref, k_ref, v_ref, do_ref, fk_ref, fq_ref, lse_ref, dl_ref, rs_ref, dk_ref, dv_ref, df_ref):
        j = pl.program_id(1)
        keep = _attn_consts(T)[1]
        TH = T // N_SPLIT_DIAG

        def part(rows, i, c0, state, masked):
            dk, dv, df = state
            r0 = pl.multiple_of(i * T + c0, TH)
            qb = q_ref[pl.ds(r0, T - c0), :]
            do = do_ref[pl.ds(r0, T - c0), :].astype(BF16)
            bias = (fq_ref[i] - lse_ref[i])[:, c0:]
            dl = (dl_ref[i] + rs_ref[i])[:, c0:]
            st = _dot(k_ref[rows, :], qb, NT) * c2 + (bias - fk_ref[rows, :])
            if masked:
                st = jnp.where(keep[rows, c0:], st, _NEG)
            pt = jnp.exp2(st)
            dst = pt * (_dot(v_ref[rows, :], do, NT) - dl)
            return (dk + _dot(dst.astype(BF16), qb), dv + _dot(pt.astype(BF16), do),
                    df - jnp.sum(dst, axis=-1, keepdims=True))

        groups = []
        for g in range(N_SPLIT_DIAG):
            zero = (jnp.zeros((TH, HEAD_DIM), F32), jnp.zeros((TH, HEAD_DIM), F32), jnp.zeros((TH, 1), F32))
            groups.append(part(slice(g * TH, (g + 1) * TH), j, g * TH, zero, True))
        state = tuple(jnp.concatenate([grp[n] for grp in groups], axis=0) for n in range(3))
        dk, dv, df = lax.fori_loop(j + 1, nb, lambda i, st: part(slice(0, T), i, 0, st, False), state)
        dk_ref[...] = (dk * scale).astype(BF16)
        dv_ref[...] = dv.astype(BF16)
        df_ref[...] = df

    nh = H
    row = pl.BlockSpec((None, nb, 1, T), lambda h, j: (h, 0, 0, 0))
    whole = pl.BlockSpec((S, HEAD_DIM), lambda h, j: (0, h))
    kv_out = pl.BlockSpec((T, HEAD_DIM), lambda h, j: (j, h))
    col = pl.BlockSpec((None, T, 1), lambda h, j: (h, j, 0))
    return _carry_call(
        body, name="attn_bwd_dkv", grid=(H, nb), jobs=jobs,
        args=[zm, zm, zm, dattn, f2col, f2row, lse2_row, delta_row, rowsum_row],
        semantics=("arbitrary", "arbitrary"),
        in_specs=[
            whole,
            pl.BlockSpec((T, HEAD_DIM), lambda h, j: (j, nh + h)),
            pl.BlockSpec((T, HEAD_DIM), lambda h, j: (j, 2 * nh + h)),
            whole, col, row, row, row, row,
        ],
        out_specs=[kv_out, kv_out, col],
        out_shape=[jax.ShapeDtypeStruct((S, H * HEAD_DIM), BF16), jax.ShapeDtypeStruct((S, H * HEAD_DIM), BF16),
                   jax.ShapeDtypeStruct((H, S, 1), F32)],
        scratch_shapes=[],
    )


def _attn_fwd(zm, fcol, frow, T, jobs=()):
    S = zm.shape[0]
    H = fcol.shape[0]
    nb = S // T
    scale = 1.0 / math.sqrt(HEAD_DIM)

    def body(q_ref, k_ref, v_ref, fq_ref, fk_ref, o_ref, lse_ref, m_s, l_s, acc_s):
        i = pl.program_id(1)
        j = pl.program_id(2)

        @pl.when(j == 0)
        def _():
            m_s[...] = jnp.full_like(m_s, _NEG)
            l_s[...] = jnp.zeros_like(l_s)
            acc_s[...] = jnp.zeros_like(acc_s)

        @pl.when(j <= i)
        def _():
            s = _dot(q_ref[...], k_ref[...], NT) * scale + (fq_ref[...] - fk_ref[...])
            keep = (_iota2((T, T), 1) + j * T) <= (_iota2((T, T), 0) + i * T)
            s = jnp.where(keep, s, _NEG)
            m_new = jnp.maximum(m_s[...], jnp.max(s, axis=-1, keepdims=True))
            alpha = jnp.exp(m_s[...] - m_new)
            p = jnp.exp(s - m_new)
            l_s[...] = alpha * l_s[...] + jnp.sum(p, axis=-1, keepdims=True)
            acc_s[...] = alpha * acc_s[...] + _dot(p.astype(BF16), v_ref[...])
            m_s[...] = m_new

        @pl.when(j == nb - 1)
        def _():
            o_ref[...] = acc_s[...] / l_s[...]
            lse_ref[...] = m_s[...] + jnp.log(l_s[...])

    nh = H
    return _carry_call(
        body, name="attn_fwd", grid=(H, nb, nb), jobs=jobs, args=[zm, zm, zm, fcol, frow],
        semantics=("parallel", "parallel", "arbitrary"),
        in_specs=[
            pl.BlockSpec((T, HEAD_DIM), lambda h, i, j: (i, h)),
            pl.BlockSpec((T, HEAD_DIM), lambda h, i, j: (jnp.minimum(j, i), nh + h)),
            pl.BlockSpec((T, HEAD_DIM), lambda h, i, j: (jnp.minimum(j, i), 2 * nh + h)),
            pl.BlockSpec((None, T, 1), lambda h, i, j: (h, i, 0)),
            pl.BlockSpec((None, 1, T), lambda h, i, j: (h, 0, jnp.minimum(j, i))),
        ],
        out_specs=[
            pl.BlockSpec((T, HEAD_DIM), lambda h, i, j: (i, h)),
            pl.BlockSpec((None, T, 1), lambda h, i, j: (h, i, 0)),
        ],
        out_shape=[jax.ShapeDtypeStruct((S, H * HEAD_DIM), F32), jax.ShapeDtypeStruct((H, S, 1), F32)],
        scratch_shapes=[pltpu.VMEM((T, 1), F32), pltpu.VMEM((T, 1), F32), pltpu.VMEM((T, HEAD_DIM), F32)],
    )


def _attn_delta(dattn, attn, tr):
    S, DA = attn.shape
    H = DA // HEAD_DIM

    def body(do_ref, o_ref, out_ref):
        lo = _iota2((DA, LANES), 1) * HEAD_DIM
        sel = ((_iota2((DA, LANES), 0) >= lo) & (_iota2((DA, LANES), 0) < lo + HEAD_DIM)).astype(BF16)
        d = _dot3(do_ref[...] * o_ref[...], sel)
        for c in range(tr // CHUNK):
            out_ref[:, c * CHUNK:(c + 1) * CHUNK] = d[c * CHUNK:(c + 1) * CHUNK, :].T[0:H, :]

    return pl.pallas_call(
        body, name="attn_delta", grid=(S // tr,),
        in_specs=[pl.BlockSpec((tr, DA), lambda i: (i, 0))] * 2,
        out_specs=pl.BlockSpec((H, tr), lambda i: (0, i)),
        out_shape=jax.ShapeDtypeStruct((H, S), F32),
        compiler_params=_params(("parallel",)),
    )(dattn, attn)


def _attn_bwd_dq(zm, dattn, fcol, frow, lse_col, delta_col, T, jobs=()):
    S = zm.shape[0]
    H = fcol.shape[0]
    nb = S // T
    scale = 1.0 / math.sqrt(HEAD_DIM)

    def body(q_ref, k_ref, v_ref, do_ref, fq_ref, fk_ref, lse_ref, dl_ref, dq_ref, rs_ref, acc_s, rs_s):
        i = pl.program_id(1)
        j = pl.program_id(2)

        @pl.when(j == 0)
        def _():
            acc_s[...] = jnp.zeros_like(acc_s)
            rs_s[...] = jnp.zeros_like(rs_s)

        @pl.when(j <= i)
        def _():
            s = _dot(q_ref[...], k_ref[...], NT) * scale + (fq_ref[...] - fk_ref[...])
            keep = (_iota2((T, T), 1) + j * T) <= (_iota2((T, T), 0) + i * T)
            p = jnp.exp(jnp.where(keep, s - lse_ref[...], _NEG))
            dp = _dot(do_ref[...].astype(BF16), v_ref[...], NT)
            ds = p * (dp - dl_ref[...])
            acc_s[...] += _dot(ds.astype(BF16), k_ref[...])
            rs_s[...] += jnp.sum(ds, axis=-1, keepdims=True)

        @pl.when(j == nb - 1)
        def _():
            dq_ref[...] = (acc_s[...] * scale).astype(BF16)
            rs_ref[...] = rs_s[...]

    nh = H
    col = pl.BlockSpec((None, T, 1), lambda h, i, j: (h, i, 0))
    return _carry_call(
        body, name="attn_bwd_dq", grid=(H, nb, nb), jobs=jobs,
        args=[zm, zm, zm, dattn, fcol, frow, lse_col, delta_col], semantics=("parallel", "parallel", "arbitrary"),
        in_specs=[
            pl.BlockSpec((T, HEAD_DIM), lambda h, i, j: (i, h)),
            pl.BlockSpec((T, HEAD_DIM), lambda h, i, j: (jnp.minimum(j, i), nh + h)),
            pl.BlockSpec((T, HEAD_DIM), lambda h, i, j: (jnp.minimum(j, i), 2 * nh + h)),
            pl.BlockSpec((T, HEAD_DIM), lambda h, i, j: (i, h)),
            col,
            pl.BlockSpec((None, 1, T), lambda h, i, j: (h, 0, jnp.minimum(j, i))),
            col, col,
        ],
        out_specs=[pl.BlockSpec((T, HEAD_DIM), lambda h, i, j: (i, h)), col],
        out_shape=[jax.ShapeDtypeStruct((S, H * HEAD_DIM), BF16), jax.ShapeDtypeStruct((H, S, 1), F32)],
        scratch_shapes=[pltpu.VMEM((T, HEAD_DIM), F32), pltpu.VMEM((T, 1), F32)],
    )


def _attn_bwd_dkv(zm, dattn, fcol, frow, lse_row, delta_row, rowsum_row, T, jobs=()):
    S = zm.shape[0]
    H = fcol.shape[0]
    nb = S // T
    scale = 1.0 / math.sqrt(HEAD_DIM)

    def body(q_ref, k_ref, v_ref, do_ref, fk_ref, fq_ref, lse_ref, dl_ref, rs_ref,
             dk_ref, dv_ref, df_ref, dk_s, dv_s, df_s):
        j = pl.program_id(1)
        i = pl.program_id(2)

        @pl.when(i == 0)
        def _():
            dk_s[...] = jnp.zeros_like(dk_s)
            dv_s[...] = jnp.zeros_like(dv_s)
            df_s[...] = jnp.zeros_like(df_s)

        @pl.when(i >= j)
        def _():
            st = _dot(k_ref[...], q_ref[...], NT) * scale + (fq_ref[...] - fk_ref[...])
            keep = (_iota2((T, T), 0) + j * T) <= (_iota2((T, T), 1) + i * T)
            pt = jnp.exp(jnp.where(keep, st - lse_ref[...], _NEG))
            do = do_ref[...].astype(BF16)
            dpt = _dot(v_ref[...], do, NT)
            dst = pt * (dpt - (dl_ref[...] + rs_ref[...]))
            dv_s[...] += _dot(pt.astype(BF16), do)
            dk_s[...] += _dot(dst.astype(BF16), q_ref[...])
            df_s[...] -= jnp.sum(dst, axis=-1, keepdims=True)

        @pl.when(i == nb - 1)
        def _():
            dk_ref[...] = (dk_s[...] * scale).astype(BF16)
            dv_ref[...] = dv_s[...].astype(BF16)
            df_ref[...] = df_s[...]

    nh = H
    row = pl.BlockSpec((None, 1, T), lambda h, j, i: (h, 0, jnp.maximum(i, j)))
    kv_out = pl.BlockSpec((T, HEAD_DIM), lambda h, j, i: (j, h))
    return _carry_call(
        body, name="attn_bwd_dkv", grid=(H, nb, nb), jobs=jobs,
        args=[zm, zm, zm, dattn, fcol, frow, lse_row, delta_row, rowsum_row],
        semantics=("parallel", "parallel", "arbitrary"),
        in_specs=[
            pl.BlockSpec((T, HEAD_DIM), lambda h, j, i: (jnp.maximum(i, j), h)),
            pl.BlockSpec((T, HEAD_DIM), lambda h, j, i: (j, nh + h)),
            pl.BlockSpec((T, HEAD_DIM), lambda h, j, i: (j, 2 * nh + h)),
            pl.BlockSpec((T, HEAD_DIM), lambda h, j, i: (jnp.maximum(i, j), h)),
            pl.BlockSpec((None, T, 1), lambda h, j, i: (h, j, 0)),
            row, row, row, row,
        ],
        out_specs=[kv_out, kv_out, pl.BlockSpec((None, T, 1), lambda h, j, i: (h, j, 0))],
        out_shape=[jax.ShapeDtypeStruct((S, H * HEAD_DIM), BF16), jax.ShapeDtypeStruct((S, H * HEAD_DIM), BF16),
                   jax.ShapeDtypeStruct((H, S, 1), F32)],
        scratch_shapes=[pltpu.VMEM((T, HEAD_DIM), F32), pltpu.VMEM((T, HEAD_DIM), F32), pltpu.VMEM((T, 1), F32)],
    )


def _ln_stats(x):
    mu = jnp.mean(x, axis=-1, keepdims=True)
    xc = x - mu
    rstd = lax.rsqrt(jnp.mean(xc * xc, axis=-1, keepdims=True) + EPS)
    return xc * rstd, rstd


def _tril_mask():
    return _iota2((CHUNK, CHUNK), 0) >= _iota2((CHUNK, CHUNK), 1)


def _gmlp_fwd(zm, ln_g, ln_b, w_s, bs_col, tr):
    S = zm.shape[0]
    H = w_s.shape[0]
    DG = H * HEAD_DIM

    def body(zu_ref, zv_ref, g_ref, b_ref, w_ref, bs_ref, out_ref):
        u = _gelu(zu_ref[...].astype(F32))
        y, _ = _ln_stats(_gelu(zv_ref[...].astype(F32)))
        v = (y * g_ref[...] + b_ref[...]).astype(BF16)
        mask = _tril_mask()
        for h in range(H):
            wc = jnp.where(mask, w_ref[h], 0.0).astype(BF16)
            cs = slice(h * HEAD_DIM, (h + 1) * HEAD_DIM)
            for c in range(tr // CHUNK):
                rs = slice(c * CHUNK, (c + 1) * CHUNK)
                mix = _dot(wc, v[rs, cs]) + bs_ref[h]
                out_ref[rs, cs] = u[rs, cs] * mix

    full = lambda a: pl.BlockSpec(a.shape, lambda i: (0,) * a.ndim)
    return pl.pallas_call(
        body, name="gmlp_fwd", grid=(S // tr,),
        in_specs=[pl.BlockSpec((tr, DG), lambda i: (i, 3)), pl.BlockSpec((tr, DG), lambda i: (i, 4)),
                  full(ln_g), full(ln_b), full(w_s), full(bs_col)],
        out_specs=pl.BlockSpec((tr, DG), lambda i: (i, 0)),
        out_shape=jax.ShapeDtypeStruct((S, DG), F32),
        compiler_params=_params(("parallel",)),
    )(zm, zm, ln_g, ln_b, w_s, bs_col)


def _gmlp_bwd(dgm, zm, ln_g, ln_b, w_s, w_st, bs_col, tr):
    S = zm.shape[0]
    H = w_s.shape[0]
    DG = H * HEAD_DIM

    def body(dg_ref, zu_ref, zv_ref, g_ref, b_ref, w_ref, wt_ref, bs_ref,
             dzu_ref, dzv_ref, dw_ref, dbs_ref, dlg_ref, dlb_ref, dv_s):
        @pl.when(pl.program_id(0) == 0)
        def _():
            dw_ref[...] = jnp.zeros_like(dw_ref)
            dbs_ref[...] = jnp.zeros_like(dbs_ref)
            dlg_ref[...] = jnp.zeros_like(dlg_ref)
            dlb_ref[...] = jnp.zeros_like(dlb_ref)

        zu = zu_ref[...].astype(F32)
        zv = zv_ref[...].astype(F32)
        u = _gelu(zu)
        y, rstd = _ln_stats(_gelu(zv))
        v = (y * g_ref[...] + b_ref[...]).astype(BF16)
        dgm_blk = dg_ref[...]
        mask = _tril_mask()
        mask_t = _iota2((CHUNK, CHUNK), 0) <= _iota2((CHUNK, CHUNK), 1)
        for h in range(H):
            wc = jnp.where(mask, w_ref[h], 0.0).astype(BF16)
            wct = jnp.where(mask_t, wt_ref[h], 0.0).astype(BF16)
            cs = slice(h * HEAD_DIM, (h + 1) * HEAD_DIM)
            dw = jnp.zeros((CHUNK, CHUNK), F32)
            dbs = jnp.zeros((CHUNK, 1), F32)
            for c in range(tr // CHUNK):
                rs = slice(c * CHUNK, (c + 1) * CHUNK)
                vch = v[rs, cs]
                mix = _dot(wc, vch) + bs_ref[h]
                dg = dgm_blk[rs, cs]
                dzu_ref[rs, cs] = (dg * mix * _gelu_grad(zu[rs, cs])).astype(BF16)
                dmix = dg * u[rs, cs]
                dbs = dbs + jnp.sum(dmix, axis=-1, keepdims=True)
                dmix_b = dmix.astype(BF16)
                dw = dw + _dot(dmix_b, vch, NT)
                dv_s[rs, cs] = _dot(wct, dmix_b)
            dw_ref[h] += jnp.where(mask, dw, 0.0)
            dbs_ref[h] += dbs
        dv = dv_s[...]
        dlg_ref[...] += jnp.sum(dv * y, axis=0, keepdims=True)
        dlb_ref[...] += jnp.sum(dv, axis=0, keepdims=True)
        dy = dv * g_ref[...]
        dgv = rstd * (dy - jnp.mean(dy, axis=-1, keepdims=True) - y * jnp.mean(dy * y, axis=-1, keepdims=True))
        dzv_ref[...] = (dgv * _gelu_grad(zv)).astype(BF16)

    full = lambda a: pl.BlockSpec(a.shape, lambda i: (0,) * a.ndim)
    rows = pl.BlockSpec((tr, DG), lambda i: (i, 0))
    return pl.pallas_call(
        body, name="gmlp_bwd", grid=(S // tr,),
        in_specs=[rows, pl.BlockSpec((tr, DG), lambda i: (i, 3)), pl.BlockSpec((tr, DG), lambda i: (i, 4)),
                  full(ln_g), full(ln_b), full(w_s), full(w_st), full(bs_col)],
        out_specs=[rows, rows, full(w_s), full(bs_col), full(ln_g), full(ln_b)],
        out_shape=[jax.ShapeDtypeStruct((S, DG), BF16), jax.ShapeDtypeStruct((S, DG), BF16),
                   jax.ShapeDtypeStruct(w_s.shape, F32), jax.ShapeDtypeStruct(bs_col.shape, F32),
                   jax.ShapeDtypeStruct(ln_g.shape, F32), jax.ShapeDtypeStruct(ln_b.shape, F32)],
        scratch_shapes=[pltpu.VMEM((tr, DG), F32)],
        compiler_params=_params(("arbitrary",)),
    )(dgm, zm, zm, ln_g, ln_b, w_s, w_st, bs_col)


def _all_gather(name, blk):
    R, C = blk.shape

    def body(x_ref, out_ref, send_sems, recv_sems, local_sem):
        x, y, c = _me()
        me, sibling = (x, y, c), (x, y, 1 - c)
        chips = [(1 - x, y), (x, 1 - y), (1 - x, 1 - y)]

        def slab(px, py, pc):
            return out_ref.at[4 * px + 2 * py + pc]

        def copy(k, block, to, src=None):
            return pltpu.make_async_remote_copy(
                src_ref=slab(*block) if src is None else src, dst_ref=slab(*block),
                send_sem=send_sems.at[k], recv_sem=recv_sems.at[k], device_id=to, device_id_type=MESH)

        mine = pltpu.make_async_copy(x_ref, slab(*me), local_sem)
        mine.start()
        first = [copy(0, me, sibling, src=x_ref)]
        first += [copy(1 + n, me, (*chip, c), src=x_ref) for n, chip in enumerate(chips)]
        for cp in first:
            cp.start()
        passed = [copy(4 + n, (*chip, c), sibling) for n, chip in enumerate(chips)]
        for n, chip in enumerate(chips):
            copy(1 + n, (*chip, c), me).wait_recv()
            passed[n].start()
        copy(0, sibling, me).wait_recv()
        for n, chip in enumerate(chips):
            copy(4 + n, (*chip, 1 - c), me).wait_recv()
        for cp in first + passed:
            cp.wait_send()
        mine.wait()

    return pl.pallas_call(
        body, name=name, out_shape=jax.ShapeDtypeStruct((N_DEV, R, C), blk.dtype),
        in_specs=[_ANY], out_specs=_ANY,
        scratch_shapes=[pltpu.SemaphoreType.DMA((7,)), pltpu.SemaphoreType.DMA((7,)), pltpu.SemaphoreType.DMA(())],
    )(blk)


def _row_tile(R, C, itemsize=4, target_bytes=2 * 1024 * 1024):
    tr = R
    while tr % 2 == 0 and tr * C * itemsize > target_bytes and (tr // 2) % 16 == 0:
        tr //= 2
    return tr


def _rs_add1(name, g4, recv, c_idx):
    _, _, R, C = g4.shape
    tr = _row_tile(R, C)

    def body(c_ref, g_ref, r_ref, hb_ref):
        hb_ref[...] = (g_ref[...] + r_ref[...].astype(F32)).astype(BF16)

    blk = pl.BlockSpec((None, tr, C), lambda p, i, c_ref: (p, i, 0))
    return pl.pallas_call(
        body, name=name,
        grid_spec=pltpu.PrefetchScalarGridSpec(
            num_scalar_prefetch=1, grid=(4, R // tr),
            in_specs=[pl.BlockSpec((None, None, tr, C), lambda p, i, c_ref: (p, c_ref[0], i, 0)), blk],
            out_specs=blk),
        out_shape=jax.ShapeDtypeStruct((4, R, C), BF16),
        compiler_params=_params(("parallel", "parallel")),
    )(c_idx, g4, recv)


def _rs_add2_own(name, g4, recv1, recv2, c_idx, p_idx):
    _, _, R, C = g4.shape
    tr = _row_tile(R, C)

    def body(c_ref, p_ref, g_ref, r1_ref, r2_ref, out_ref):
        h = g_ref[...] + r1_ref[...].astype(F32)
        out_ref[...] = ((h + r2_ref[0].astype(F32)) + r2_ref[1].astype(F32)) + r2_ref[2].astype(F32)

    return pl.pallas_call(
        body, name=name,
        grid_spec=pltpu.PrefetchScalarGridSpec(
            num_scalar_prefetch=2, grid=(R // tr,),
            in_specs=[pl.BlockSpec((None, None, tr, C), lambda i, c_ref, p_ref: (p_ref[0], c_ref[0], i, 0)),
                      pl.BlockSpec((None, tr, C), lambda i, c_ref, p_ref: (p_ref[0], i, 0)),
                      pl.BlockSpec((3, tr, C), lambda i, c_ref, p_ref: (0, i, 0))],
            out_specs=pl.BlockSpec((tr, C), lambda i, c_ref, p_ref: (i, 0))),
        out_shape=jax.ShapeDtypeStruct((R, C), F32),
        compiler_params=_params(("parallel",)),
    )(c_idx, p_idx, g4, recv1, recv2)


def _rs_add1_windows(name, g, recv, first_blocks):
    _, R, W = recv.shape

    def body(t_ref, g_ref, r_ref, h_ref, hb_ref):
        h = g_ref[...] + r_ref[...].astype(F32)
        h_ref[...] = h
        hb_ref[...] = h.astype(BF16)

    blk = pl.BlockSpec((None, R, LANES), lambda p, l, t_ref: (p, 0, l))
    return pl.pallas_call(
        body, name=name,
        grid_spec=pltpu.PrefetchScalarGridSpec(
            num_scalar_prefetch=1, grid=(4, W // LANES),
            in_specs=[pl.BlockSpec((R, LANES), lambda p, l, t_ref: (0, t_ref[p] + l)), blk],
            out_specs=[blk, blk]),
        out_shape=[jax.ShapeDtypeStruct((4, R, W), F32), jax.ShapeDtypeStruct((4, R, W), BF16)],
        compiler_params=_params(("parallel", "parallel")),
    )(first_blocks, g, recv)


def _add_windows(name, windows, first, second, n_blocks):
    _, R, W = windows.shape
    dev1 = jnp.asarray([d for d, _ in first], jnp.int32)
    blk1 = jnp.asarray([b for _, b in first], jnp.int32)
    dev2 = jnp.asarray([max(d, 0) for d, _ in second], jnp.int32)
    blk2 = jnp.asarray([b for _, b in second], jnp.int32)
    two = jnp.asarray([int(d >= 0) for d, _ in second], jnp.int32)

    def body(d1_ref, b1_ref, d2_ref, b2_ref, two_ref, a_ref, b_ref, out_ref):
        k = pl.program_id(0)

        @pl.when(two_ref[k] == 0)
        def _():
            out_ref[...] = a_ref[...]

        @pl.when(two_ref[k] != 0)
        def _():
            out_ref[...] = a_ref[...] + b_ref[...]

    return pl.pallas_call(
        body, name=name,
        grid_spec=pltpu.PrefetchScalarGridSpec(
            num_scalar_prefetch=5, grid=(n_blocks,),
            in_specs=[pl.BlockSpec((None, R, LANES), lambda k, d1, b1, d2, b2, t: (d1[k], 0, b1[k])),
                      pl.BlockSpec((None, R, LANES), lambda k, d1, b1, d2, b2, t: (d2[k], 0, b2[k]))],
            out_specs=pl.BlockSpec((R, LANES), lambda k, d1, b1, d2, b2, t: (0, k))),
        out_shape=jax.ShapeDtypeStruct((R, n_blocks * LANES), windows.dtype),
        compiler_params=_params(("parallel",)),
    )(dev1, blk1, dev2, blk2, two, windows, windows)


def _rs_add2(name, h, recv, p_idx):
    _, R, C = h.shape
    tr = _row_tile(R, C)

    def body(p_ref, h_ref, r_ref, out_ref):
        out_ref[...] = ((h_ref[...] + r_ref[0].astype(F32)) + r_ref[1].astype(F32)) + r_ref[2].astype(F32)

    return pl.pallas_call(
        body, name=name,
        grid_spec=pltpu.PrefetchScalarGridSpec(
            num_scalar_prefetch=1, grid=(R // tr,),
            in_specs=[pl.BlockSpec((None, tr, C), lambda i, p_ref: (p_ref[0], i, 0)),
                      pl.BlockSpec((3, tr, C), lambda i, p_ref: (0, i, 0))],
            out_specs=pl.BlockSpec((tr, C), lambda i, p_ref: (i, 0))),
        out_shape=jax.ShapeDtypeStruct((R, C), F32),
        compiler_params=_params(("parallel",)),
    )(p_idx, h, recv)


def _sum8(name, g):
    _, R, C = g.shape

    def body(g_ref, out_ref):
        acc = g_ref[0]
        for d in range(1, N_DEV):
            acc = acc + g_ref[d]
        out_ref[...] = acc

    return pl.pallas_call(body, name=name, out_shape=jax.ShapeDtypeStruct((R, C), F32),
                          compiler_params=_params())(g)


def _adamw_math(w, g, m, v):
    m = ADAM_B1 * m + (1.0 - ADAM_B1) * g
    v = ADAM_B2 * v + (1.0 - ADAM_B2) * (g * g)
    m_hat = m / (1.0 - ADAM_B1 ** ADAM_STEP)
    v_hat = v / (1.0 - ADAM_B2 ** ADAM_STEP)
    delta = -ADAM_LR * (m_hat / (jnp.sqrt(v_hat) + ADAM_EPS) + ADAM_WD * w)
    return delta, m, v


def _adamw(name, w, g, m, v):
    R, C = w.shape
    tr = _row_tile(R, C, target_bytes=1024 * 1024)
    return _row_call(name, lambda *a: (_adamw_math(*a), ()), [w, g, m, v], [], [(C, F32)] * 3, [], tr)


def _adamw_from_window(name, w, m, v, window, gate, where):
    R, C = w.shape
    W = window.shape[1]
    tr = _row_tile(R, C, target_bytes=1024 * 1024)

    def body(p_ref, w_ref, m_ref, v_ref, win_ref, gate_ref, g_out, d_out, m_out, v_out):
        off, nb, hg = p_ref[0], p_ref[1], p_ref[2]
        r, c = _iota2((W, C), 0), _iota2((W, C), 1)
        pick = jnp.logical_or(jnp.logical_and(c < nb, r == c + off),
                              jnp.logical_and(c >= nb + hg, r == c - hg + off)).astype(BF16)
        r2, c2 = _iota2((LANES, C), 0), _iota2((LANES, C), 1)
        pick_gate = jnp.logical_and(r2 < hg, c2 == nb + r2).astype(BF16)
        g = _dot3(win_ref[...], pick) + _dot3(gate_ref[...], pick_gate)
        g_out[...] = g
        d_out[...], m_out[...], v_out[...] = _adamw_math(w_ref[...], g, m_ref[...], v_ref[...])

    blk = pl.BlockSpec((tr, C), lambda i, p: (i, 0))
    return pl.pallas_call(
        body, name=name,
        grid_spec=pltpu.PrefetchScalarGridSpec(
            num_scalar_prefetch=1, grid=(R // tr,),
            in_specs=[blk, blk, blk, pl.BlockSpec((tr, W), lambda i, p: (i, 0)),
                      pl.BlockSpec((tr, LANES), lambda i, p: (i, 0))],
            out_specs=[blk] * 4),
        out_shape=[jax.ShapeDtypeStruct((R, C), F32)] * 4,
        compiler_params=_params(("parallel",)),
    )(where, w, m, v, window, gate)


def _adamw_many(name, ws, gs, ms, vs):
    n = len(ws)

    def body(*refs):
        ins, outs = refs[:4 * n], refs[4 * n:]
        for k in range(n):
            res = _adamw_math(ins[k][...], ins[n + k][...], ins[2 * n + k][...], ins[3 * n + k][...])
            for t in range(3):
                outs[t * n + k][...] = res[t]

    out = pl.pallas_call(
        body, name=name, out_shape=[jax.ShapeDtypeStruct(w.shape, F32) for _ in range(3) for w in ws],
        compiler_params=_params(),
    )(*ws, *gs, *ms, *vs)
    return out[:n], out[n:2 * n], out[2 * n:]


def _pack(parts):
    flat = []
    total = 0
    for a in parts:
        n = math.prod(a.shape)
        flat.append(a.reshape(-1).astype(F32))
        if n % LANES:
            flat.append(jnp.zeros((-n % LANES,), F32))
        total += n + (-n % LANES)
    if total % (8 * LANES):
        flat.append(jnp.zeros((-total % (8 * LANES),), F32))
    return jnp.concatenate(flat).reshape(-1, LANES)


def _unpack(packed, shapes):
    out = []
    r = 0
    for shp in shapes:
        n = math.prod(shp)
        nr = -(-n // LANES)
        out.append(packed[r:r + nr].reshape(-1)[:n].reshape(shp))
        r += nr
    return out


def kernel(x, norm_mix_g, w_in, b_f, gmlp_ln_g, gmlp_ln_b, w_s, b_s, attn_out_g, gmlp_out_g, w_out, norm_ffn_g, w_ff1, w_ff2, norm_final_g, loss_target, m_norm_mix_g, m_w_in, m_b_f, m_gmlp_ln_g, m_gmlp_ln_b, m_w_s, m_b_s, m_attn_out_g, m_gmlp_out_g, m_w_out, m_norm_ffn_g, m_w_ff1, m_w_ff2, m_norm_final_g, v_norm_mix_g, v_w_in, v_b_f, v_gmlp_ln_g, v_gmlp_ln_b, v_w_s, v_b_s, v_attn_out_g, v_gmlp_out_g, v_w_out, v_norm_ffn_g, v_w_ff1, v_w_ff2, v_norm_final_g):
    S, D = x.shape[1], x.shape[2]
    H = b_f.shape[1]
    DA = H * HEAD_DIM
    DG = gmlp_ln_g.shape[1]
    DQKV = 3 * DA
    DMAIN = DQKV + 2 * DG
    DIN = DMAIN + H
    DFF = w_ff1.shape[2] * N_DEV
    w_in_cols = w_in.shape[2]
    assert DIN == w_in_cols * N_DEV and DA == DG and D == DA + DG

    T_ATT = min(T_ATT_MAX, S)
    TR = min(TR_MAX, S)

    x0 = x[0]
    tgt = loss_target[0]
    g_final = norm_final_g.reshape(1, D)

    FB = DFF // N_DEV
    x_pos, y_pos, c_pos = _me()
    me_idx = 4 * x_pos + 2 * y_pos + c_pos

    WW = -(-(w_in_cols + LANES - 1) // LANES) * LANES
    to_main = lambda col: col if col <= DQKV else max(DQKV, col - H)
    lo = [to_main(n * w_in_cols) for n in range(N_DEV)]
    hi = [to_main((n + 1) * w_in_cols) for n in range(N_DEV)]
    starts = [v // LANES * LANES for v in lo]
    gate_dev = DQKV // w_in_cols
    n_before = DQKV - gate_dev * w_in_cols
    g0 = lo[gate_dev] - starts[gate_dev]
    stash = -(-(g0 + w_in_cols - H) // LANES) * LANES
    assert all(hi[n] <= starts[n] + WW <= DMAIN for n in range(N_DEV))
    assert gate_dev * w_in_cols <= DQKV and DQKV + H <= (gate_dev + 1) * w_in_cols and stash + LANES <= WW
    shard = w_in[0].astype(BF16)

    def my_window(n):
        if n != gate_dev:
            return lambda s: jnp.pad(s, ((0, 0), (lo[n] - starts[n], WW - w_in_cols - (lo[n] - starts[n]))))
        return lambda s: jnp.concatenate([
            jnp.zeros((D, g0), BF16), s[:, :n_before], s[:, n_before + H:],
            jnp.zeros((D, stash - g0 - (w_in_cols - H)), BF16), s[:, n_before:n_before + H],
            jnp.zeros((D, WW - stash - H), BF16)], axis=1)
    windows = _all_gather("ag_w_in", lax.switch(me_idx, [my_window(n) for n in range(N_DEV)], shard))
    first, second = [], []
    for blk in range(DMAIN // LANES):
        c0 = blk * LANES
        owners = [(n, (c0 - starts[n]) // LANES) for n in range(N_DEV) if lo[n] < c0 + LANES and hi[n] > c0]
        assert 1 <= len(owners) <= 2
        first.append(owners[0])
        second.append(owners[1] if len(owners) == 2 else (-1, 0))
    w_main = _add_windows("w_in_windows", windows, first, second, DMAIN // LANES)
    w_f = windows[gate_dev, :, stash:stash + LANES]
    c_idx = jnp.reshape(c_pos, (1,)).astype(jnp.int32)
    p_idx = jnp.reshape(2 * x_pos + y_pos, (1,)).astype(jnp.int32)

    (h,), _ = _row_call("rms_mix", lambda xb, g: ((_rms_fwd(xb, g),), ()), [x0], [norm_mix_g], [(D, BF16)], [], TR)
    w_ff1_b = w_ff1[0].astype(BF16)
    (zm,), ((w_out_part,), (w_ff1_q1,)) = _mm_nn(
        "in_proj", h, w_main, [BF16], 2048, 1024, 2048,
        jobs=[_job_gather_chips(w_out[0].astype(BF16)), _job_gather_chips(w_ff1_b, part=(0, 1, 4))])
    (zf,) = _mm_nn("in_proj_f", h, w_f, [F32], 1024, LANES, 2048)
    bf_pad = jnp.pad(b_f, ((0, 0), (0, LANES - H)))
    f_row = _fgate_fwd(zf, bf_pad)
    NB = S // T_ATT
    f_col3 = f_row.reshape(H, S, 1)
    f_row3 = f_row.reshape(H, NB, 1, T_ATT)
    (attn, lse_col3), ((w_out_all,), (w_ff1_part,)) = _attn2_fwd(
        zm, f_col3, f_row3, T_ATT, jobs=[_job_gather_sibling(w_out_part),
                                         _job_gather_chips(w_ff1_b, part=(1, 4, 4), into=w_ff1_q1)])
    w_out_full = w_out_all.reshape(D, D)
    bs_col = b_s[0].reshape(H, CHUNK, 1)
    gm = _gmlp_fwd(zm, gmlp_ln_g, gmlp_ln_b, w_s[0], bs_col, TR)

    def merge_fn(a, g, ga, gg):
        return (jnp.concatenate([_rms_fwd(a, ga), _rms_fwd(g, gg)], axis=1),), ()
    (merged,), _ = _row_call("rms_merge", merge_fn, [attn, gm], [attn_out_g, gmlp_out_g], [(D, BF16)], [], TR)

    w_ff2_b = w_ff2[0].astype(BF16)
    TMR = min(512, S)

    def out_proj_fn(acc, res, g):
        xb = acc + res
        return (xb, _rms_fwd(xb, g)), ()
    ((x1, h2), _), ((w_ff1_all,), (w_ff2_q1,)) = _mm_rows(
        "out_proj", (S // TMR, 1), merged, pl.BlockSpec((TMR, D), lambda i, k: (i, 0)),
        w_out_full, pl.BlockSpec((D, D), lambda i, k: (0, 0)), NN, TMR, D, [x0], [norm_ffn_g],
        [(D, F32), (D, BF16)], [], out_proj_fn,
        jobs=[_job_gather_sibling(w_ff1_part), _job_gather_chips(w_ff2_b, part=(0, 1, 4))])

    tm, tn, tk = min(1024, S), min(1024, FB), min(2048, D)
    tm1 = min(2048, S)
    o_spec = pl.BlockSpec((tm1, tn), lambda i, j, k: (i, j))

    def relu_sq(acc):
        a = jnp.maximum(acc, 0.0)
        return a, a * a
    nj = FB // tn
    ff2_rest = [_job_gather_chips(w_ff2_b, part=(1, 4, 4), into=w_ff2_q1)]
    (a_act, a_sq), ((w_ff2_q2,),) = _mm(
        "ff1", (S // tm1, DFF // tn, D // tk), h2, pl.BlockSpec((tm1, tk), lambda i, j, k: (i, k)),
        w_ff1_all, pl.BlockSpec((None, tk, tn), lambda i, j, k: (j // nj, k, j % nj)), NN, (tm1, tn),
        [jax.ShapeDtypeStruct((S, DFF), BF16)] * 2, [o_spec] * 2, epilogue=relu_sq, jobs=ff2_rest)
    (w_ff2_all,) = _run_jobs("ag_w_ff2_sibling", [_job_gather_sibling(w_ff2_q2)])[0]
    w_ff2_full = w_ff2_all.reshape(DFF, D)
    def head_fn(acc, res, t, g):
        xb = acc + res
        rstd = lax.rsqrt(jnp.mean(xb * xb, axis=-1, keepdims=True) + EPS)
        xhat = xb * rstd
        err = xhat * g - t
        loss = 0.5 * jnp.sum(jnp.mean(err * err, axis=-1, keepdims=True), axis=0, keepdims=True)
        dy = err * (1.0 / D)
        dg = jnp.sum(dy * xhat, axis=0, keepdims=True)
        dxhat = dy * g
        dx = rstd * (dxhat - xhat * jnp.mean(dxhat * xhat, axis=-1, keepdims=True))
        return (dx, dx), (dg, jnp.broadcast_to(loss, (1, LANES)))
    tk_ff2 = min(1024, DFF)
    (dx2, dx2_b), (dg_final, loss_part) = _mm_rows(
        "ff2", (S // TMR, DFF // tk_ff2), a_sq, pl.BlockSpec((TMR, tk_ff2), lambda i, k: (i, k)),
        w_ff2_full, pl.BlockSpec((tk_ff2, D), lambda i, k: (k, 0)), NN, TMR, D, [x1, tgt], [g_final],
        [(D, F32), (D, BF16)], [D, LANES], head_fn)

    (da,) = _mm_nt("ff2_dx", dx2_b, w_ff2_full, [BF16], 2048, 1024, 2048, extras=[a_act],
                   epilogue=lambda acc, a: (2.0 * a.astype(F32) * acc,))
    dw_ff2, dw_ff2_b = _mm_tn("ff2_dw", a_sq, dx2_b, [F32, BF16], 1024, 2048, 1024)
    tm2, tk2 = min(2048, D), min(1024, S)
    dw1_spec = pl.BlockSpec((None, tm2, FB), lambda i, j, k: (j, i, 0))
    (dw_ff1, dw_ff1_b), ((r1_ff2,),) = _mm(
        "ff1_dw", (D // tm2, DFF // FB, S // tk2), h2, pl.BlockSpec((tk2, tm2), lambda i, j, k: (k, i)),
        da, pl.BlockSpec((tk2, FB), lambda i, j, k: (k, j)), TN, (tm2, FB),
        [jax.ShapeDtypeStruct((N_DEV, D, FB), F32), jax.ShapeDtypeStruct((N_DEV, D, FB), BF16)], [dw1_spec] * 2,
        epilogue=lambda acc: (acc, acc), jobs=[_job_scatter_sibling(dw_ff2_b.reshape(4, 2, FB, D))])
    hb_ff2 = _rs_add1("rs_add1_w_ff2", dw_ff2.reshape(4, 2, FB, D), r1_ff2, c_idx)
    def ffn_bwd_fn(dh, xb, dres, g):
        dx, dg = _rms_bwd(dh, xb, g)
        dx = dx + dres
        return (dx, dx), (dg,)
    tkb = min(1024, FB)
    nkb = FB // tkb
    ((dx1, dx1_b), (dg_ffn,)), ((r2_ff2,), (r1_ff1,)) = _mm_rows(
        "ff1_dx", (S // TMR, DFF // tkb), da, pl.BlockSpec((TMR, tkb), lambda i, k: (i, k)),
        w_ff1_all, pl.BlockSpec((None, D, tkb), lambda i, k: (k // nkb, 0, k % nkb)), NT, TMR, D, [x1, dx2],
        [norm_ffn_g], [(D, F32), (D, BF16)], [D], ffn_bwd_fn,
        jobs=[_job_scatter_chips(hb_ff2), _job_scatter_sibling(dw_ff1_b.reshape(4, 2, D, FB))])
    g_w_ff2 = _rs_add2_own("rs_add2_w_ff2", dw_ff2.reshape(4, 2, FB, D), r1_ff2, r2_ff2, c_idx, p_idx)
    hb_ff1 = _rs_add1("rs_add1_w_ff1", dw_ff1.reshape(4, 2, D, FB), r1_ff1, c_idx)

    def merge_bwd_fn(dm, a, g, ga, gg):
        da_, dga = _rms_bwd(dm[:, :DA], a, ga)
        dg_, dgg = _rms_bwd(dm[:, DA:], g, gg)
        return (da_, dg_), (dga, dgg)
    (dattn, dgm), (dg_attn, dg_gmlp) = _mm_rows(
        "out_proj_dx", (S // TMR, 1), dx1_b, pl.BlockSpec((TMR, D), lambda i, k: (i, 0)),
        w_out_full, pl.BlockSpec((D, D), lambda i, k: (0, 0)), NT, TMR, D, [attn, gm], [attn_out_g, gmlp_out_g],
        [(DA, F32), (DG, F32)], [DA, DG], merge_bwd_fn)
    dw_out, dw_out_b = _mm_tn("out_proj_dw", merged, dx1_b, [F32, BF16], 2048, 1024, 1024)

    w_st = jnp.swapaxes(w_s[0], 1, 2)
    dzu, dzv, dw_s, dbs_col, dln_g, dln_b = _gmlp_bwd(dgm, zm, gmlp_ln_g, gmlp_ln_b, w_s[0], w_st, bs_col, TR)

    delta_row = _attn_delta(dattn, attn, TR)
    lse_row3 = lse_col3.reshape(H, NB, 1, T_ATT)
    (dq, ds_rowsum), ((r2_ff1_a,), (r1_out,)) = _attn2_bwd_dq(
        zm, dattn, f_col3, f_row3, lse_col3, delta_row.reshape(H, S, 1), T_ATT,
        jobs=[_job_scatter_chips(hb_ff1, part=(0, 5, 8)),
              _job_scatter_sibling(dw_out_b.reshape(4, 2, D // N_DEV, D))])
    hb_out = _rs_add1("rs_add1_w_out", dw_out.reshape(4, 2, D // N_DEV, D), r1_out, c_idx)
    (dk, dv, df_col3), ((r2_ff1,), (r2_out,)) = _attn2_bwd_dkv(
        zm, dattn, f_col3, f_row3, lse_row3, delta_row.reshape(H, NB, 1, T_ATT),
        ds_rowsum.reshape(H, NB, 1, T_ATT), T_ATT,
        jobs=[_job_scatter_chips(hb_ff1, part=(5, 8, 8), into=r2_ff1_a), _job_scatter_chips(hb_out)])
    g_w_ff1 = _rs_add2_own("rs_add2_w_ff1", dw_ff1.reshape(4, 2, D, FB), r1_ff1, r2_ff1, c_idx, p_idx)
    g_w_out = _rs_add2_own("rs_add2_w_out", dw_out.reshape(4, 2, D // N_DEV, D), r1_out, r2_out, c_idx, p_idx)
    dzf, dbf = _fgate_bwd(df_col3.reshape(H, S), zf, bf_pad)

    dz_main = jnp.concatenate([dq, dk, dv, dzu, dzv], axis=1)
    dw_main, dw_main_b = _mm_tn("in_proj_dw", h, dz_main, [F32, BF16], 2048, 1024, 1024)
    (dw_f,), ((r1_in,),) = _mm_tn("in_proj_f_dw", h, dzf, [F32], 2048, LANES, 1024,
                                  jobs=[_job_scatter_sibling_windows(dw_main_b, starts, WW)])
    first_blocks = jnp.stack([jnp.where(c_pos == 0, starts[2 * p], starts[2 * p + 1]) // LANES
                              for p in range(4)]).astype(jnp.int32)
    h_in, hb_in = _rs_add1_windows("rs_add1_w_in", dw_main, r1_in, first_blocks)

    def mix_bwd_fn(dh_main, dz_gate, xb, dres, g, w_gate):
        dx, dg = _rms_bwd(dh_main + _dot(dz_gate, w_gate, NT), xb, g)
        return (dx + dres,), (dg,)
    tk_in = min(1024, DMAIN)
    ((grad_x,), (dg_mix,)), ((r2_in,),) = _mm_rows(
        "in_proj_dx", (S // TMR, DMAIN // tk_in), dz_main, pl.BlockSpec((TMR, tk_in), lambda i, k: (i, k)),
        w_main, pl.BlockSpec((D, tk_in), lambda i, k: (0, k)), NT, TMR, D, [dzf, x0, dx1], [norm_mix_g, w_f],
        [(D, F32)], [D], mix_bwd_fn, jobs=[_job_scatter_chips(hb_in)])
    g_window = _rs_add2("rs_add2_w_in", h_in, r2_in, p_idx)

    small_shapes = [norm_mix_g.shape, b_f.shape, gmlp_ln_g.shape, gmlp_ln_b.shape, w_s.shape, b_s.shape,
                    attn_out_g.shape, gmlp_out_g.shape, norm_ffn_g.shape, norm_final_g.shape]
    small_parts = [dg_mix, dbf[:, :H], dln_g, dln_b, dw_s, dbs_col, dg_attn, dg_gmlp, dg_ffn, dg_final]
    g_small = _sum8("small_sum", _all_gather("ag_small", _pack(small_parts + [dw_f[:, :H], loss_part])))
    *gs, g_gate, loss_sum = _unpack(g_small, small_shapes + [(D, H), (1, LANES)])
    two_d = lambda a: a.reshape(1, -1) if a.ndim == 1 else a
    ds, nms, nvs = _adamw_many(
        "adamw_small",
        [two_d(a) for a in (norm_mix_g, b_f, gmlp_ln_g, gmlp_ln_b, w_s, b_s, attn_out_g, gmlp_out_g, norm_ffn_g,
                            norm_final_g)],
        [two_d(a) for a in gs],
        [two_d(a) for a in (m_norm_mix_g, m_b_f, m_gmlp_ln_g, m_gmlp_ln_b, m_w_s, m_b_s, m_attn_out_g, m_gmlp_out_g,
                            m_norm_ffn_g, m_norm_final_g)],
        [two_d(a) for a in (v_norm_mix_g, v_b_f, v_gmlp_ln_g, v_gmlp_ln_b, v_w_s, v_b_s, v_attn_out_g, v_gmlp_out_g,
                            v_norm_ffn_g, v_norm_final_g)])
    ds, nms, nvs = [[a.reshape(s) for a, s in zip(lst, small_shapes)] for lst in (ds, nms, nvs)]

    is_gate_dev = me_idx == gate_dev
    where = jnp.stack([sum(jnp.where(me_idx == n, lo[n] - starts[n], 0) for n in range(N_DEV)),
                       jnp.where(is_gate_dev, n_before, w_in_cols), jnp.where(is_gate_dev, H, 0)]).astype(jnp.int32)
    big = {"w_in": tuple(a[None] for a in _adamw_from_window(
        "adamw_w_in", w_in[0], m_w_in[0], v_w_in[0], g_window, jnp.pad(g_gate, ((0, 0), (0, LANES - H))), where))}
    for nm, w, g, m, v in (("w_out", w_out, g_w_out, m_w_out, v_w_out),
                           ("w_ff1", w_ff1, g_w_ff1, m_w_ff1, v_w_ff1), ("w_ff2", w_ff2, g_w_ff2, m_w_ff2, v_w_ff2)):
        (d_, m_, v_), _ = _adamw("adamw_" + nm, w[0], g, m[0], v[0])
        big[nm] = (g[None], d_[None], m_[None], v_[None])

    loss = loss_sum[0, 0]

    def leaves(n):
        sm = (gs, ds, nms, nvs)[n]
        return [sm[0], big["w_in"][n], sm[1], sm[2], sm[3], sm[4], sm[5], sm[6], sm[7], big["w_out"][n], sm[8],
                big["w_ff1"][n], big["w_ff2"][n], sm[9]]

    return (loss, grad_x[None], *leaves(0), *leaves(1), *leaves(2), *leaves(3))
```

```python
import functools
import math

import jax
import jax.numpy as jnp
from jax import lax
from jax.experimental import pallas as pl
from jax.experimental.pallas import tpu as pltpu

F32 = jnp.float32
BF16 = jnp.bfloat16
MESH = pl.DeviceIdType.MESH

HEAD_DIM = 128
CHUNK = 128
EPS = 1e-6
LANES = 128
N_DEV = 8

ADAM_LR = 0.001
ADAM_B1 = 0.9
ADAM_B2 = 0.999
ADAM_EPS = 1e-08
ADAM_WD = 0.01
ADAM_STEP = 10

VMEM_LIMIT_BYTES = 56 * 1024 * 1024
T_ATT_MAX = 1024
TR_MAX = 512

NN = ((1,), (0,))
NT = ((1,), (1,))
TN = ((0,), (0,))


def _params(sem=None):
    return pltpu.CompilerParams(dimension_semantics=sem, vmem_limit_bytes=VMEM_LIMIT_BYTES)


def _dot(a, b, contract=NN):
    return lax.dot_general(a, b, (contract, ((), ())), preferred_element_type=F32)


def _dot3(x, t):
    x1 = x.astype(BF16)
    r1 = x - x1.astype(F32)
    x2 = r1.astype(BF16)
    x3 = (r1 - x2.astype(F32)).astype(BF16)
    return _dot(x1, t) + _dot(x2, t) + _dot(x3, t)


def _iota2(shape, dim):
    return lax.broadcasted_iota(jnp.int32, shape, dim)


def _row_call(name, fn, row_ins, bcast_ins, row_outs, acc_outs, tr, jobs=()):
    S = row_ins[0].shape[0]
    assert S % tr == 0
    n_ri, n_bi, n_ro, n_ao = len(row_ins), len(bcast_ins), len(row_outs), len(acc_outs)

    def body(*refs):
        ins = [r[...] for r in refs[:n_ri + n_bi]]
        ro_refs = refs[n_ri + n_bi:n_ri + n_bi + n_ro]
        ao_refs = refs[n_ri + n_bi + n_ro:]
        ro, ao = fn(*ins)
        for r, v in zip(ro_refs, ro):
            r[...] = v.astype(r.dtype)
        if n_ao:
            @pl.when(pl.program_id(0) == 0)
            def _():
                for r in ao_refs:
                    r[...] = jnp.zeros_like(r)
            for r, v in zip(ao_refs, ao):
                r[...] += v

    in_specs = [pl.BlockSpec((tr, a.shape[1]), lambda i: (i, 0)) for a in row_ins]
    in_specs += [pl.BlockSpec(a.shape, lambda i: (0, 0)) for a in bcast_ins]
    out_specs = [pl.BlockSpec((tr, d), lambda i: (i, 0)) for d, _ in row_outs]
    out_specs += [pl.BlockSpec((1, d), lambda i: (0, 0)) for d in acc_outs]
    out_shape = [jax.ShapeDtypeStruct((S, d), dt) for d, dt in row_outs]
    out_shape += [jax.ShapeDtypeStruct((1, d), F32) for d in acc_outs]
    outs, job_res = _carry_call(
        body, name=name, grid=(S // tr,), in_specs=in_specs, out_specs=out_specs, out_shape=out_shape,
        scratch_shapes=[], semantics=("arbitrary",) if n_ao else ("parallel",), args=list(row_ins) + list(bcast_ins),
        jobs=jobs)
    res = (outs[:n_ro], outs[n_ro:])
    return (res, job_res) if jobs else res


def _rms_fwd(x, g):
    rstd = lax.rsqrt(jnp.mean(x * x, axis=-1, keepdims=True) + EPS)
    return x * rstd * g


def _rms_bwd(dy, x, g):
    rstd = lax.rsqrt(jnp.mean(x * x, axis=-1, keepdims=True) + EPS)
    xhat = x * rstd
    dg = jnp.sum(dy * xhat, axis=0, keepdims=True)
    dxhat = dy * g
    dx = rstd * (dxhat - xhat * jnp.mean(dxhat * xhat, axis=-1, keepdims=True))
    return dx, dg


_GELU_C = math.sqrt(2.0 / math.pi)


def _gelu(x):
    return 0.5 * x * (1.0 + jnp.tanh(_GELU_C * (x + 0.044715 * (x * x * x))))


def _gelu_grad(x):
    t = jnp.tanh(_GELU_C * (x + 0.044715 * (x * x * x)))
    return 0.5 * (1.0 + t) + 0.5 * x * (1.0 - t * t) * (_GELU_C * (1.0 + 3.0 * 0.044715 * (x * x)))


def _me():
    return lax.axis_index("x"), lax.axis_index("y"), lax.axis_index("c")


def _other_chips(x, y):
    return [(1 - x, y), (x, 1 - y), (1 - x, 1 - y)]


_ANY = pl.BlockSpec(memory_space=pl.ANY)


class _Job:
    def __init__(self, ins, outs, n_sems, make, aliases=None):
        self.ins, self.outs, self.n_sems, self.make, self.aliases = ins, outs, n_sems, make, aliases or {}


def _job_gather_chips(blk, part=(0, 1, 1), into=None):
    R, C = blk.shape
    nr = R // part[2]
    rows = pl.ds(part[0] * nr, (part[1] - part[0]) * nr)

    def make(ins, outs, send_sems, recv_sems, base):
        x_ref, (out_ref,) = ins[0], outs
        x, y, c = _me()
        mine = 4 * x + 2 * y + c
        targets = [(x, y, 1 - c)] + [(cx, cy, c) for cx, cy in _other_chips(x, y)]

        def copy(k, slab, to):
            return pltpu.make_async_remote_copy(
                src_ref=x_ref.at[rows, :], dst_ref=out_ref.at[slab, rows, :], send_sem=send_sems.at[base + k],
                recv_sem=recv_sems.at[base + k], device_id=to, device_id_type=MESH)

        starts = [copy(k, mine, to) for k, to in enumerate(targets)]
        arrivals = [copy(k, 4 * tx + 2 * ty + tc, (tx, ty, tc)) for k, (tx, ty, tc) in enumerate(targets)]
        local = [pltpu.make_async_copy(x_ref.at[rows, :], out_ref.at[mine, rows, :], send_sems.at[base + 4])]
        return starts, arrivals, local

    out = jax.ShapeDtypeStruct((N_DEV, R, C), blk.dtype)
    if into is None:
        return _Job([blk], [out], 5, make)
    return _Job([blk, into], [out], 5, make, aliases={1: 0})


def _job_gather_sibling(part):
    def make(ins, outs, send_sems, recv_sems, base):
        (out_ref,) = outs
        x, y, c = _me()

        def copy(k, slab):
            return pltpu.make_async_remote_copy(
                src_ref=out_ref.at[slab], dst_ref=out_ref.at[slab], send_sem=send_sems.at[base + k],
                recv_sem=recv_sems.at[base + k], device_id=(x, y, 1 - c), device_id_type=MESH)

        chips = _other_chips(x, y)
        starts = [copy(k, 4 * cx + 2 * cy + c) for k, (cx, cy) in enumerate(chips)]
        arrivals = [copy(k, 4 * cx + 2 * cy + (1 - c)) for k, (cx, cy) in enumerate(chips)]
        return starts, arrivals, []

    return _Job([part], [jax.ShapeDtypeStruct(part.shape, part.dtype)], 3, make, aliases={0: 0})


def _job_scatter_sibling(gb):
    _, _, R, C = gb.shape

    def make(ins, outs, send_sems, recv_sems, base):
        (g_ref,), (recv_ref,) = ins, outs
        x, y, c = _me()
        copies = [pltpu.make_async_remote_copy(
            src_ref=g_ref.at[p, 1 - c], dst_ref=recv_ref.at[p], send_sem=send_sems.at[base + p],
            recv_sem=recv_sems.at[base + p], device_id=(x, y, 1 - c), device_id_type=MESH) for p in range(4)]
        return copies, copies, []

    return _Job([gb], [jax.ShapeDtypeStruct((4, R, C), gb.dtype)], 4, make)


def _job_scatter_sibling_windows(gb, starts, width):
    R, _ = gb.shape

    def make(ins, outs, send_sems, recv_sems, base):
        (g_ref,), (recv_ref,) = ins, outs
        x, y, c = _me()
        copies = []
        for p in range(4):
            start = pl.multiple_of(jnp.where(c == 0, starts[2 * p + 1], starts[2 * p]), LANES)
            copies.append(pltpu.make_async_remote_copy(
                src_ref=g_ref.at[:, pl.ds(start, width)], dst_ref=recv_ref.at[p], send_sem=send_sems.at[base + p],
                recv_sem=recv_sems.at[base + p], device_id=(x, y, 1 - c), device_id_type=MESH))
        return copies, copies, []

    return _Job([gb], [jax.ShapeDtypeStruct((4, R, width), gb.dtype)], 4, make)


def _job_scatter_chips(hb, part=(0, 1, 1), into=None):
    _, R, C = hb.shape
    nr = R // part[2]
    rows = pl.ds(part[0] * nr, (part[1] - part[0]) * nr)

    def make(ins, outs, send_sems, recv_sems, base):
        h_ref, (recv_ref,) = ins[0], outs
        x, y, c = _me()
        copies = [pltpu.make_async_remote_copy(
            src_ref=h_ref.at[2 * cx + cy, rows, :], dst_ref=recv_ref.at[n, rows, :], send_sem=send_sems.at[base + n],
            recv_sem=recv_sems.at[base + n], device_id=(cx, cy, c), device_id_type=MESH)
            for n, (cx, cy) in enumerate(_other_chips(x, y))]
        return copies, copies, []

    out = jax.ShapeDtypeStruct((3, R, C), hb.dtype)
    if into is None:
        return _Job([hb], [out], 3, make)
    return _Job([hb, into], [out], 3, make, aliases={1: 0})


def _carry_call(body, *, name, grid, in_specs, out_specs, out_shape, scratch_shapes, semantics, args, jobs=()):
    jobs = list(jobs)
    n_in, n_out, n_scr = len(in_specs), len(out_specs), len(scratch_shapes)
    j_ins = [a for j in jobs for a in j.ins]
    j_outs = [o for j in jobs for o in j.outs]
    n_sems = sum(j.n_sems for j in jobs)
    aliases = {}
    i0, o0 = n_in, n_out
    for j in jobs:
        for a, b in j.aliases.items():
            aliases[i0 + a] = o0 + b
        i0 += len(j.ins)
        o0 += len(j.outs)

    def full_body(*refs):
        ins = refs[:n_in]
        jin = refs[n_in:n_in + len(j_ins)]
        outs = refs[n_in + len(j_ins):n_in + len(j_ins) + n_out]
        jout = refs[n_in + len(j_ins) + n_out:n_in + len(j_ins) + n_out + len(j_outs)]
        scr = refs[n_in + len(j_ins) + n_out + len(j_outs):]
        if jobs:
            send_sems, recv_sems = scr[n_scr], scr[n_scr + 1]
            starts, arrivals, local = [], [], []
            base = i0 = o0 = 0
            for j in jobs:
                s, a, l = j.make(jin[i0:i0 + len(j.ins)], jout[o0:o0 + len(j.outs)], send_sems, recv_sems, base)
                starts += s
                arrivals += a
                local += l
                base += j.n_sems
                i0 += len(j.ins)
                o0 += len(j.outs)
            pids = [pl.program_id(d) for d in range(len(grid))]
            first = functools.reduce(jnp.logical_and, [p == 0 for p in pids])
            last = functools.reduce(jnp.logical_and, [p == n - 1 for p, n in zip(pids, grid)])

            @pl.when(first)
            def _():
                for cp in local + starts:
                    cp.start()

        body(*ins, *outs, *scr[:n_scr])

        if jobs:
            @pl.when(last)
            def _():
                for cp in arrivals:
                    cp.wait_recv()
                for cp in starts:
                    cp.wait_send()
                for cp in local:
                    cp.wait()

    sems = [pltpu.SemaphoreType.DMA((n_sems,)), pltpu.SemaphoreType.DMA((n_sems,))] if jobs else []
    res = pl.pallas_call(
        full_body, name=name, grid=grid,
        in_specs=list(in_specs) + [_ANY] * len(j_ins),
        out_specs=list(out_specs) + [_ANY] * len(j_outs),
        out_shape=list(out_shape) + j_outs,
        scratch_shapes=list(scratch_shapes) + sems,
        input_output_aliases=aliases,
        compiler_params=_params(("arbitrary",) * len(grid) if jobs else semantics),
    )(*args, *j_ins)
    body_res, job_res = res[:n_out], res[n_out:]
    per_job = []
    for j in jobs:
        per_job.append(job_res[:len(j.outs)])
        job_res = job_res[len(j.outs):]
    return body_res, per_job


def _run_jobs(name, jobs):
    def body(done_ref):
        done_ref[...] = jnp.zeros_like(done_ref)

    return _carry_call(body, name=name, grid=(1,), in_specs=[], out_specs=[pl.BlockSpec((8, LANES), lambda i: (0, 0))],
                       out_shape=[jax.ShapeDtypeStruct((8, LANES), F32)], scratch_shapes=[], semantics=("arbitrary",),
                       args=[], jobs=jobs)[1]


def _mm(name, grid, a, a_spec, b, b_spec, contract, acc_shape, out_shape, out_specs, extras=(), epilogue=None, jobs=()):
    nk = grid[2]
    n_e = len(extras)
    n_o = len(out_shape)
    if epilogue is None:
        epilogue = lambda acc: (acc,)

    def body(a_ref, b_ref, *rest):
        e_refs = rest[:n_e]
        o_refs = rest[n_e:n_e + n_o]

        def finish(total):
            res = epilogue(total, *[r[...] for r in e_refs])
            for o, r in zip(o_refs, res):
                o[...] = r.astype(o.dtype)

        if nk == 1:
            finish(_dot(a_ref[...], b_ref[...], contract))
            return
        acc = rest[n_e + n_o]
        k = pl.program_id(2)

        @pl.when(k == 0)
        def _():
            acc[...] = _dot(a_ref[...], b_ref[...], contract)

        @pl.when(jnp.logical_and(k > 0, k < nk - 1))
        def _():
            acc[...] += _dot(a_ref[...], b_ref[...], contract)

        @pl.when(k == nk - 1)
        def _():
            finish(acc[...] + _dot(a_ref[...], b_ref[...], contract))

    outs, job_res = _carry_call(
        body, name=name, grid=grid, in_specs=[a_spec, b_spec] + [s for _, s in extras],
        out_specs=list(out_specs), out_shape=list(out_shape),
        scratch_shapes=[pltpu.VMEM(acc_shape, F32)] if nk > 1 else [],
        semantics=("parallel", "parallel", "arbitrary"), args=[a, b] + [e for e, _ in extras], jobs=jobs)
    return (outs, job_res) if jobs else outs


def _mm_rows(name, grid, a, a_spec, b, b_spec, contract, tm, n, row_extras, bcast, row_outs, acc_outs, epilogue, jobs=()):
    nk = grid[1]
    M = grid[0] * tm
    n_x, n_b, n_ro, n_ao = len(row_extras), len(bcast), len(row_outs), len(acc_outs)

    def body(a_ref, b_ref, *rest):
        x_refs = rest[:n_x + n_b]
        ro_refs = rest[n_x + n_b:n_x + n_b + n_ro]
        ao_refs = rest[n_x + n_b + n_ro:n_x + n_b + n_ro + n_ao]
        i = pl.program_id(0)

        def finish(total):
            ro, ao = epilogue(total, *[r[...] for r in x_refs])
            for r, v in zip(ro_refs, ro):
                r[...] = v.astype(r.dtype)
            if n_ao:
                @pl.when(i == 0)
                def _():
                    for r, v in zip(ao_refs, ao):
                        r[...] = v

                @pl.when(i > 0)
                def _():
                    for r, v in zip(ao_refs, ao):
                        r[...] += v

        if nk == 1:
            finish(_dot(a_ref[...], b_ref[...], contract))
            return
        acc = rest[n_x + n_b + n_ro + n_ao]
        k = pl.program_id(1)

        @pl.when(k == 0)
        def _():
            acc[...] = _dot(a_ref[...], b_ref[...], contract)

        @pl.when(jnp.logical_and(k > 0, k < nk - 1))
        def _():
            acc[...] += _dot(a_ref[...], b_ref[...], contract)

        @pl.when(k == nk - 1)
        def _():
            finish(acc[...] + _dot(a_ref[...], b_ref[...], contract))

    in_specs = [a_spec, b_spec] + [pl.BlockSpec((tm, x.shape[1]), lambda i, k: (i, 0)) for x in row_extras]
    in_specs += [pl.BlockSpec(x.shape, lambda i, k: (0,) * x.ndim) for x in bcast]
    out_specs = [pl.BlockSpec((tm, w), lambda i, k: (i, 0)) for w, _ in row_outs]
    out_specs += [pl.BlockSpec((1, w), lambda i, k: (0, 0)) for w in acc_outs]
    out_shape = [jax.ShapeDtypeStruct((M, w), dt) for w, dt in row_outs]
    out_shape += [jax.ShapeDtypeStruct((1, w), F32) for w in acc_outs]
    outs, job_res = _carry_call(
        body, name=name, grid=grid, in_specs=in_specs, out_specs=out_specs, out_shape=out_shape,
        scratch_shapes=[pltpu.VMEM((tm, n), F32)] if nk > 1 else [],
        semantics=("arbitrary", "arbitrary"), args=[a, b] + list(row_extras) + list(bcast), jobs=jobs)
    res = (outs[:n_ro], outs[n_ro:])
    return (res, job_res) if jobs else res


def _mm_nn(name, a, b, out_dtypes, tm, tn, tk, extras=(), epilogue=None, jobs=()):
    M, K = a.shape
    N = b.shape[1]
    tm, tn, tk = min(tm, M), min(tn, N), min(tk, K)
    o_spec = pl.BlockSpec((tm, tn), lambda i, j, k: (i, j))
    return _mm(name, (M // tm, N // tn, K // tk),
               a, pl.BlockSpec((tm, tk), lambda i, j, k: (i, k)),
               b, pl.BlockSpec((tk, tn), lambda i, j, k: (k, j)), NN, (tm, tn),
               [jax.ShapeDtypeStruct((M, N), dt) for dt in out_dtypes], [o_spec] * len(out_dtypes),
               [(e, o_spec) for e in extras], epilogue, jobs)


def _mm_nt(name, a, b, out_dtypes, tm, tn, tk, extras=(), epilogue=None, jobs=()):
    M, K = a.shape
    N = b.shape[0]
    tm, tn, tk = min(tm, M), min(tn, N), min(tk, K)
    o_spec = pl.BlockSpec((tm, tn), lambda i, j, k: (i, j))
    return _mm(name, (M // tm, N // tn, K // tk),
               a, pl.BlockSpec((tm, tk), lambda i, j, k: (i, k)),
               b, pl.BlockSpec((tn, tk), lambda i, j, k: (j, k)), NT, (tm, tn),
               [jax.ShapeDtypeStruct((M, N), dt) for dt in out_dtypes], [o_spec] * len(out_dtypes),
               [(e, o_spec) for e in extras], epilogue, jobs)


def _mm_tn(name, a, b, out_dtypes, tm, tn, tk, jobs=()):
    K, M = a.shape
    N = b.shape[1]
    tm, tn, tk = min(tm, M), min(tn, N), min(tk, K)
    o_spec = pl.BlockSpec((tm, tn), lambda i, j, k: (i, j))
    return _mm(name, (M // tm, N // tn, K // tk),
               a, pl.BlockSpec((tk, tm), lambda i, j, k: (k, i)),
               b, pl.BlockSpec((tk, tn), lambda i, j, k: (k, j)), TN, (tm, tn),
               [jax.ShapeDtypeStruct((M, N), dt) for dt in out_dtypes], [o_spec] * len(out_dtypes),
               epilogue=lambda acc: (acc,) * len(out_dtypes), jobs=jobs)


def _fgate_fwd(zf, bf):
    S = zf.shape[0]
    nc = S // CHUNK

    def body(zf_ref, bf_ref, f_ref):
        upper = (_iota2((CHUNK, CHUNK), 0) <= _iota2((CHUNK, CHUNK), 1)).astype(BF16)
        carry = jnp.zeros((8, 1), F32)
        for c in range(nc):
            t = zf_ref[c * CHUNK:(c + 1) * CHUNK, :] + bf_ref[...]
            lf = jnp.minimum(t, 0.0) - jnp.log(1.0 + jnp.exp(-jnp.abs(t)))
            lf_rows = lf.T[0:8, :]
            f_ref[:, c * CHUNK:(c + 1) * CHUNK] = (_dot3(lf_rows, upper) + carry) * LOG2E
            carry = carry + jnp.sum(lf_rows, axis=-1, keepdims=True)

    return pl.pallas_call(
        body, name="fgate_fwd", out_shape=jax.ShapeDtypeStruct((8, S), F32),
        compiler_params=_params(),
    )(zf, bf)


def _fgate_bwd(df, zf, bf):
    S = zf.shape[0]
    nc = S // CHUNK

    def body(df_ref, zf_ref, bf_ref, dzf_ref, dbf_ref):
        lower = (_iota2((CHUNK, CHUNK), 0) >= _iota2((CHUNK, CHUNK), 1)).astype(BF16)
        carry = jnp.zeros((8, 1), F32)
        dbf = jnp.zeros((1, LANES), F32)
        for c in reversed(range(nc)):
            sl = slice(c * CHUNK, (c + 1) * CHUNK)
            df = df_ref[:, sl]
            r = _dot3(df, lower) + carry
            carry = carry + jnp.sum(df, axis=-1, keepdims=True)
            r_cols = jnp.concatenate([r, jnp.zeros((CHUNK - 8, CHUNK), F32)], axis=0).T
            t = zf_ref[sl, :] + bf_ref[...]
            dz = r_cols * (1.0 / (1.0 + jnp.exp(t)))
            dzf_ref[sl, :] = dz.astype(BF16)
            dbf = dbf + jnp.sum(dz, axis=0, keepdims=True)
        dbf_ref[...] = dbf

    return pl.pallas_call(
        body, name="fgate_bwd",
        out_shape=[jax.ShapeDtypeStruct((S, LANES), BF16), jax.ShapeDtypeStruct((1, LANES), F32)],
        compiler_params=_params(),
    )(df, zf, bf)


_NEG = -1e30
LOG2E = 1.4426950408889634
N_SPLIT = 8
N_SPLIT_DIAG = 4
DIAG_STEP = 1024


def _attn_consts(T):
    rows, cols = _iota2((T, T), 0), _iota2((T, T), 1)
    return cols <= rows, rows <= cols


def _attn2_fwd(zm, f2col, f2row, T, jobs=()):
    S = zm.shape[0]
    H = f2col.shape[0]
    nb = S // T
    c2 = LOG2E / math.sqrt(HEAD_DIM)

    def body(q_ref, k_ref, v_ref, fq_ref, fk_ref, o_ref, lse_ref, vaug_s):
        i = pl.program_id(1)

        @pl.when(i == 0)
        def _():
            vaug_s[:, :HEAD_DIM] = v_ref[...]
            vaug_s[:, HEAD_DIM:] = jnp.ones((S, HEAD_DIM), BF16)

        keep = _attn_consts(T)[0]
        TH = T // N_SPLIT

        def block(j, diagonal, state):
            r0 = pl.multiple_of(j * T, T)
            fk = fk_ref[j]
            new = []
            for g, (m_old, acc) in enumerate(state):
                rows = slice(g * TH, (g + 1) * TH)
                nk = min(T, -(-(g + 1) * TH // DIAG_STEP) * DIAG_STEP) if diagonal else T
                s = _dot(q_ref[rows, :], k_ref[pl.ds(r0, nk), :], NT) * c2 + (fq_ref[rows, :] - fk[:, :nk])
                if diagonal:
                    s = jnp.where(keep[rows, :nk], s, _NEG)
                m_new = jnp.maximum(m_old, jnp.max(s, axis=-1, keepdims=True))
                p = jnp.exp2(s - m_new).astype(BF16)
                new.append((m_new, jnp.exp2(m_old - m_new) * acc + _dot(p, vaug_s[pl.ds(r0, nk), :])))
            return tuple(new)

        init = tuple((jnp.full((TH, 1), _NEG, F32), jnp.zeros((TH, 2 * HEAD_DIM), F32)) for _ in range(N_SPLIT))
        state = lax.fori_loop(0, i, lambda j, st: block(j, False, st), init)
        state = block(i, True, state)
        for g, (m, acc) in enumerate(state):
            rows = slice(g * TH, (g + 1) * TH)
            o_ref[rows, :] = acc[:, :HEAD_DIM] / acc[:, HEAD_DIM:]
            lse_ref[rows, :] = m + jnp.log2(acc[:, HEAD_DIM:HEAD_DIM + 1])

    nh = H
    return _carry_call(
        body, name="attn_fwd", grid=(H, nb), jobs=jobs, args=[zm, zm, zm, f2col, f2row],
        semantics=("arbitrary", "arbitrary"),
        in_specs=[
            pl.BlockSpec((T, HEAD_DIM), lambda h, i: (i, h)),
            pl.BlockSpec((S, HEAD_DIM), lambda h, i: (0, nh + h)),
            pl.BlockSpec((S, HEAD_DIM), lambda h, i: (0, 2 * nh + h)),
            pl.BlockSpec((None, T, 1), lambda h, i: (h, i, 0)),
            pl.BlockSpec((None, nb, 1, T), lambda h, i: (h, 0, 0, 0)),
        ],
        out_specs=[pl.BlockSpec((T, HEAD_DIM), lambda h, i: (i, h)), pl.BlockSpec((None, T, 1), lambda h, i: (h, i, 0))],
        out_shape=[jax.ShapeDtypeStruct((S, H * HEAD_DIM), F32), jax.ShapeDtypeStruct((H, S, 1), F32)],
        scratch_shapes=[pltpu.VMEM((S, 2 * HEAD_DIM), BF16)],
    )


def _attn2_bwd_dq(zm, dattn, f2col, f2row, lse2_col, delta_col, T, jobs=()):
    S = zm.shape[0]
    H = f2col.shape[0]
    nb = S // T
    scale = 1.0 / math.sqrt(HEAD_DIM)
    c2 = LOG2E * scale

    def body(q_ref, k_ref, v_ref, do_ref, fq_ref, fk_ref, lse_ref, dl_ref, dq_ref, rs_ref, bias_s, do_s):
        i = pl.program_id(1)
        keep = _attn_consts(T)[0]
        TH = T // N_SPLIT_DIAG
        bias_s[...] = fq_ref[...] - lse_ref[...]
        do_s[...] = do_ref[...].astype(BF16)

        def part(rows, j, nk, state, masked):
            acc, rs = state
            r0 = pl.multiple_of(j * T, T)
            kb = k_ref[pl.ds(r0, nk), :]
            s = _dot(q_ref[rows, :], kb, NT) * c2 + (bias_s[rows, :] - fk_ref[j][:, :nk])
            if masked:
                s = jnp.where(keep[rows, :nk], s, _NEG)
            ds = jnp.exp2(s) * (_dot(do_s[rows, :], v_ref[pl.ds(r0, nk), :], NT) - dl_ref[rows, :])
            return acc + _dot(ds.astype(BF16), kb), rs + jnp.sum(ds, axis=-1, keepdims=True)

        def step(j, state):
            return part(slice(0, T), j, T, state, False)

        acc, rs = lax.fori_loop(0, i, step, (jnp.zeros((T, HEAD_DIM), F32), jnp.zeros((T, 1), F32)))
        for g in range(N_SPLIT_DIAG):
            rows = slice(g * TH, (g + 1) * TH)
            acc_g, rs_g = part(rows, i, (g + 1) * TH, (acc[rows, :], rs[rows, :]), True)
            dq_ref[rows, :] = (acc_g * scale).astype(BF16)
            rs_ref[rows, :] = rs_g

    nh = H
    col = pl.BlockSpec((None, T, 1), lambda h, i: (h, i, 0))
    blk = pl.BlockSpec((T, HEAD_DIM), lambda h, i: (i, h))
    return _carry_call(
        body, name="attn_bwd_dq", grid=(H, nb), jobs=jobs,
        args=[zm, zm, zm, dattn, f2col, f2row, lse2_col, delta_col], semantics=("arbitrary", "arbitrary"),
        in_specs=[
            blk,
            pl.BlockSpec((S, HEAD_DIM), lambda h, i: (0, nh + h)),
            pl.BlockSpec((S, HEAD_DIM), lambda h, i: (0, 2 * nh + h)),
            blk, col,
            pl.BlockSpec((None, nb, 1, T), lambda h, i: (h, 0, 0, 0)),
            col, col,
        ],
        out_specs=[blk, col],
        out_shape=[jax.ShapeDtypeStruct((S, H * HEAD_DIM), BF16), jax.ShapeDtypeStruct((H, S, 1), F32)],
        scratch_shapes=[pltpu.VMEM((T, 1), F32), pltpu.VMEM((T, HEAD_DIM), BF16)],
    )


def _attn2_bwd_dkv(zm, dattn, f2col, f2row, lse2_row, delta_row, rowsum_row, T, jobs=()):
    S = zm.shape[0]
    H = f2col.shape[0]
    nb = S // T
    scale = 1.0 / math.sqrt(HEAD_DIM)
    c2 = LOG2E * scale

    def body(q_ref, k_ref, v_ref, do_ref, fk_ref, fq_ref, lse_ref, dl_ref, rs_ref, dk_ref, dv_ref, df_ref):
        j = pl.program_id(1)
        keep = _attn_consts(T)[1]
        TH = T // N_SPLIT_DIAG

        def part(rows, i, c0, state, masked):
            dk, dv, df = state
            r0 = pl.multiple_of(i * T + c0, TH)
            qb = q_ref[pl.ds(r0, T - c0), :]
            do = do_ref[pl.ds(r0, T - c0), :].astype(BF16)
            bias = (fq_ref[i] - lse_ref[i])[:, c0:]
            dl = (dl_ref[i] + rs_ref[i])[:, c0:]
            st = _dot(k_ref[rows, :], qb, NT) * c2 + (bias - fk_ref[rows, :])
            if masked:
                st = jnp.where(keep[rows, c0:], st, _NEG)
            pt = jnp.exp2(st)
            dst = pt * (_dot(v_ref[rows, :], do, NT) - dl)
            return (dk + _dot(dst.astype(BF16), qb), dv + _dot(pt.astype(BF16), do),
                    df - jnp.sum(dst, axis=-1, keepdims=True))

        groups = []
        for g in range(N_SPLIT_DIAG):
            zero = (jnp.zeros((TH, HEAD_DIM), F32), jnp.zeros((TH, HEAD_DIM), F32), jnp.zeros((TH, 1), F32))
            groups.append(part(slice(g * TH, (g + 1) * TH), j, g * TH, zero, True))
        state = tuple(jnp.concatenate([grp[n] for grp in groups], axis=0) for n in range(3))
        dk, dv, df = lax.fori_loop(j + 1, nb, lambda i, st: part(slice(0, T), i, 0, st, False), state)
        dk_ref[...] = (dk * scale).astype(BF16)
        dv_ref[...] = dv.astype(BF16)
        df_ref[...] = df

    nh = H
    row = pl.BlockSpec((None, nb, 1, T), lambda h, j: (h, 0, 0, 0))
    whole = pl.BlockSpec((S, HEAD_DIM), lambda h, j: (0, h))
    kv_out = pl.BlockSpec((T, HEAD_DIM), lambda h, j: (j, h))
    col = pl.BlockSpec((None, T, 1), lambda h, j: (h, j, 0))
    return _carry_call(
        body, name="attn_bwd_dkv", grid=(H, nb), jobs=jobs,
        args=[zm, zm, zm, dattn, f2col, f2row, lse2_row, delta_row, rowsum_row],
        semantics=("arbitrary", "arbitrary"),
        in_specs=[
            whole,
            pl.BlockSpec((T, HEAD_DIM), lambda h, j: (j, nh + h)),
            pl.BlockSpec((T, HEAD_DIM), lambda h, j: (j, 2 * nh + h)),
            whole, col, row, row, row, row,
        ],
        out_specs=[kv_out, kv_out, col],
        out_shape=[jax.ShapeDtypeStruct((S, H * HEAD_DIM), BF16), jax.ShapeDtypeStruct((S, H * HEAD_DIM), BF16),
                   jax.ShapeDtypeStruct((H, S, 1), F32)],
        scratch_shapes=[],
    )


def _attn_fwd(zm, fcol, frow, T, jobs=()):
    S = zm.shape[0]
    H = fcol.shape[0]
    nb = S // T
    scale = 1.0 / math.sqrt(HEAD_DIM)

    def body(q_ref, k_ref, v_ref, fq_ref, fk_ref, o_ref, lse_ref, m_s, l_s, acc_s):
        i = pl.program_id(1)
        j = pl.program_id(2)

        @pl.when(j == 0)
        def _():
            m_s[...] = jnp.full_like(m_s, _NEG)
            l_s[...] = jnp.zeros_like(l_s)
            acc_s[...] = jnp.zeros_like(acc_s)

        @pl.when(j <= i)
        def _():
            s = _dot(q_ref[...], k_ref[...], NT) * scale + (fq_ref[...] - fk_ref[...])
            keep = (_iota2((T, T), 1) + j * T) <= (_iota2((T, T), 0) + i * T)
            s = jnp.where(keep, s, _NEG)
            m_new = jnp.maximum(m_s[...], jnp.max(s, axis=-1, keepdims=True))
            alpha = jnp.exp(m_s[...] - m_new)
            p = jnp.exp(s - m_new)
            l_s[...] = alpha * l_s[...] + jnp.sum(p, axis=-1, keepdims=True)
            acc_s[...] = alpha * acc_s[...] + _dot(p.astype(BF16), v_ref[...])
            m_s[...] = m_new

        @pl.when(j == nb - 1)
        def _():
            o_ref[...] = acc_s[...] / l_s[...]
            lse_ref[...] = m_s[...] + jnp.log(l_s[...])

    nh = H
    return _carry_call(
        body, name="attn_fwd", grid=(H, nb, nb), jobs=jobs, args=[zm, zm, zm, fcol, frow],
        semantics=("parallel", "parallel", "arbitrary"),
        in_specs=[
            pl.BlockSpec((T, HEAD_DIM), lambda h, i, j: (i, h)),
            pl.BlockSpec((T, HEAD_DIM), lambda h, i, j: (jnp.minimum(j, i), nh + h)),
            pl.BlockSpec((T, HEAD_DIM), lambda h, i, j: (jnp.minimum(j, i), 2 * nh + h)),
            pl.BlockSpec((None, T, 1), lambda h, i, j: (h, i, 0)),
            pl.BlockSpec((None, 1, T), lambda h, i, j: (h, 0, jnp.minimum(j, i))),
        ],
        out_specs=[
            pl.BlockSpec((T, HEAD_DIM), lambda h, i, j: (i, h)),
            pl.BlockSpec((None, T, 1), lambda h, i, j: (h, i, 0)),
        ],
        out_shape=[jax.ShapeDtypeStruct((S, H * HEAD_DIM), F32), jax.ShapeDtypeStruct((H, S, 1), F32)],
        scratch_shapes=[pltpu.VMEM((T, 1), F32), pltpu.VMEM((T, 1), F32), pltpu.VMEM((T, HEAD_DIM), F32)],
    )


def _attn_delta(dattn, attn, tr):
    S, DA = attn.shape
    H = DA // HEAD_DIM

    def body(do_ref, o_ref, out_ref):
        lo = _iota2((DA, LANES), 1) * HEAD_DIM
        sel = ((_iota2((DA, LANES), 0) >= lo) & (_iota2((DA, LANES), 0) < lo + HEAD_DIM)).astype(BF16)
        d = _dot3(do_ref[...] * o_ref[...], sel)
        for c in range(tr // CHUNK):
            out_ref[:, c * CHUNK:(c + 1) * CHUNK] = d[c * CHUNK:(c + 1) * CHUNK, :].T[0:H, :]

    return pl.pallas_call(
        body, name="attn_delta", grid=(S // tr,),
        in_specs=[pl.BlockSpec((tr, DA), lambda i: (i, 0))] * 2,
        out_specs=pl.BlockSpec((H, tr), lambda i: (0, i)),
        out_shape=jax.ShapeDtypeStruct((H, S), F32),
        compiler_params=_params(("parallel",)),
    )(dattn, attn)


def _attn_bwd_dq(zm, dattn, fcol, frow, lse_col, delta_col, T, jobs=()):
    S = zm.shape[0]
    H = fcol.shape[0]
    nb = S // T
    scale = 1.0 / math.sqrt(HEAD_DIM)

    def body(q_ref, k_ref, v_ref, do_ref, fq_ref, fk_ref, lse_ref, dl_ref, dq_ref, rs_ref, acc_s, rs_s):
        i = pl.program_id(1)
        j = pl.program_id(2)

        @pl.when(j == 0)
        def _():
            acc_s[...] = jnp.zeros_like(acc_s)
            rs_s[...] = jnp.zeros_like(rs_s)

        @pl.when(j <= i)
        def _():
            s = _dot(q_ref[...], k_ref[...], NT) * scale + (fq_ref[...] - fk_ref[...])
            keep = (_iota2((T, T), 1) + j * T) <= (_iota2((T, T), 0) + i * T)
            p = jnp.exp(jnp.where(keep, s - lse_ref[...], _NEG))
            dp = _dot(do_ref[...].astype(BF16), v_ref[...], NT)
            ds = p * (dp - dl_ref[...])
            acc_s[...] += _dot(ds.astype(BF16), k_ref[...])
            rs_s[...] += jnp.sum(ds, axis=-1, keepdims=True)

        @pl.when(j == nb - 1)
        def _():
            dq_ref[...] = (acc_s[...] * scale).astype(BF16)
            rs_ref[...] = rs_s[...]

    nh = H
    col = pl.BlockSpec((None, T, 1), lambda h, i, j: (h, i, 0))
    return _carry_call(
        body, name="attn_bwd_dq", grid=(H, nb, nb), jobs=jobs,
        args=[zm, zm, zm, dattn, fcol, frow, lse_col, delta_col], semantics=("parallel", "parallel", "arbitrary"),
        in_specs=[
            pl.BlockSpec((T, HEAD_DIM), lambda h, i, j: (i, h)),
            pl.BlockSpec((T, HEAD_DIM), lambda h, i, j: (jnp.minimum(j, i), nh + h)),
            pl.BlockSpec((T, HEAD_DIM), lambda h, i, j: (jnp.minimum(j, i), 2 * nh + h)),
            pl.BlockSpec((T, HEAD_DIM), lambda h, i, j: (i, h)),
            col,
            pl.BlockSpec((None, 1, T), lambda h, i, j: (h, 0, jnp.minimum(j, i))),
            col, col,
        ],
        out_specs=[pl.BlockSpec((T, HEAD_DIM), lambda h, i, j: (i, h)), col],
        out_shape=[jax.ShapeDtypeStruct((S, H * HEAD_DIM), BF16), jax.ShapeDtypeStruct((H, S, 1), F32)],
        scratch_shapes=[pltpu.VMEM((T, HEAD_DIM), F32), pltpu.VMEM((T, 1), F32)],
    )


def _attn_bwd_dkv(zm, dattn, fcol, frow, lse_row, delta_row, rowsum_row, T, jobs=()):
    S = zm.shape[0]
    H = fcol.shape[0]
    nb = S // T
    scale = 1.0 / math.sqrt(HEAD_DIM)

    def body(q_ref, k_ref, v_ref, do_ref, fk_ref, fq_ref, lse_ref, dl_ref, rs_ref,
             dk_ref, dv_ref, df_ref, dk_s, dv_s, df_s):
        j = pl.program_id(1)
        i = pl.program_id(2)

        @pl.when(i == 0)
        def _():
            dk_s[...] = jnp.zeros_like(dk_s)
            dv_s[...] = jnp.zeros_like(dv_s)
            df_s[...] = jnp.zeros_like(df_s)

        @pl.when(i >= j)
        def _():
            st = _dot(k_ref[...], q_ref[...], NT) * scale + (fq_ref[...] - fk_ref[...])
            keep = (_iota2((T, T), 0) + j * T) <= (_iota2((T, T), 1) + i * T)
            pt = jnp.exp(jnp.where(keep, st - lse_ref[...], _NEG))
            do = do_ref[...].astype(BF16)
            dpt = _dot(v_ref[...], do, NT)
            dst = pt * (dpt - (dl_ref[...] + rs_ref[...]))
            dv_s[...] += _dot(pt.astype(BF16), do)
            dk_s[...] += _dot(dst.astype(BF16), q_ref[...])
            df_s[...] -= jnp.sum(dst, axis=-1, keepdims=True)

        @pl.when(i == nb - 1)
        def _():
            dk_ref[...] = (dk_s[...] * scale).astype(BF16)
            dv_ref[...] = dv_s[...].astype(BF16)
            df_ref[...] = df_s[...]

    nh = H
    row = pl.BlockSpec((None, 1, T), lambda h, j, i: (h, 0, jnp.maximum(i, j)))
    kv_out = pl.BlockSpec((T, HEAD_DIM), lambda h, j, i: (j, h))
    return _carry_call(
        body, name="attn_bwd_dkv", grid=(H, nb, nb), jobs=jobs,
        args=[zm, zm, zm, dattn, fcol, frow, lse_row, delta_row, rowsum_row],
        semantics=("parallel", "parallel", "arbitrary"),
        in_specs=[
            pl.BlockSpec((T, HEAD_DIM), lambda h, j, i: (jnp.maximum(i, j), h)),
            pl.BlockSpec((T, HEAD_DIM), lambda h, j, i: (j, nh + h)),
            pl.BlockSpec((T, HEAD_DIM), lambda h, j, i: (j, 2 * nh + h)),
            pl.BlockSpec((T, HEAD_DIM), lambda h, j, i: (jnp.maximum(i, j), h)),
            pl.BlockSpec((None, T, 1), lambda h, j, i: (h, j, 0)),
            row, row, row, row,
        ],
        out_specs=[kv_out, kv_out, pl.BlockSpec((None, T, 1), lambda h, j, i: (h, j, 0))],
        out_shape=[jax.ShapeDtypeStruct((S, H * HEAD_DIM), BF16), jax.ShapeDtypeStruct((S, H * HEAD_DIM), BF16),
                   jax.ShapeDtypeStruct((H, S, 1), F32)],
        scratch_shapes=[pltpu.VMEM((T, HEAD_DIM), F32), pltpu.VMEM((T, HEAD_DIM), F32), pltpu.VMEM((T, 1), F32)],
    )


def _ln_stats(x):
    mu = jnp.mean(x, axis=-1, keepdims=True)
    xc = x - mu
    rstd = lax.rsqrt(jnp.mean(xc * xc, axis=-1, keepdims=True) + EPS)
    return xc * rstd, rstd


def _tril_mask():
    return _iota2((CHUNK, CHUNK), 0) >= _iota2((CHUNK, CHUNK), 1)


def _gmlp_fwd(zm, ln_g, ln_b, w_s, bs_col, tr):
    S = zm.shape[0]
    H = w_s.shape[0]
    DG = H * HEAD_DIM

    def body(zu_ref, zv_ref, g_ref, b_ref, w_ref, bs_ref, out_ref):
        u = _gelu(zu_ref[...].astype(F32))
        y, _ = _ln_stats(_gelu(zv_ref[...].astype(F32)))
        v = (y * g_ref[...] + b_ref[...]).astype(BF16)
        mask = _tril_mask()
        for h in range(H):
            wc = jnp.where(mask, w_ref[h], 0.0).astype(BF16)
            cs = slice(h * HEAD_DIM, (h + 1) * HEAD_DIM)
            for c in range(tr // CHUNK):
                rs = slice(c * CHUNK, (c + 1) * CHUNK)
                mix = _dot(wc, v[rs, cs]) + bs_ref[h]
                out_ref[rs, cs] = u[rs, cs] * mix

    full = lambda a: pl.BlockSpec(a.shape, lambda i: (0,) * a.ndim)
    return pl.pallas_call(
        body, name="gmlp_fwd", grid=(S // tr,),
        in_specs=[pl.BlockSpec((tr, DG), lambda i: (i, 3)), pl.BlockSpec((tr, DG), lambda i: (i, 4)),
                  full(ln_g), full(ln_b), full(w_s), full(bs_col)],
        out_specs=pl.BlockSpec((tr, DG), lambda i: (i, 0)),
        out_shape=jax.ShapeDtypeStruct((S, DG), F32),
        compiler_params=_params(("parallel",)),
    )(zm, zm, ln_g, ln_b, w_s, bs_col)


def _gmlp_bwd(dgm, zm, ln_g, ln_b, w_s, w_st, bs_col, tr):
    S = zm.shape[0]
    H = w_s.shape[0]
    DG = H * HEAD_DIM

    def body(dg_ref, zu_ref, zv_ref, g_ref, b_ref, w_ref, wt_ref, bs_ref,
             dzu_ref, dzv_ref, dw_ref, dbs_ref, dlg_ref, dlb_ref, dv_s):
        @pl.when(pl.program_id(0) == 0)
        def _():
            dw_ref[...] = jnp.zeros_like(dw_ref)
            dbs_ref[...] = jnp.zeros_like(dbs_ref)
            dlg_ref[...] = jnp.zeros_like(dlg_ref)
            dlb_ref[...] = jnp.zeros_like(dlb_ref)

        zu = zu_ref[...].astype(F32)
        zv = zv_ref[...].astype(F32)
        u = _gelu(zu)
        y, rstd = _ln_stats(_gelu(zv))
        v = (y * g_ref[...] + b_ref[...]).astype(BF16)
        dgm_blk = dg_ref[...]
        mask = _tril_mask()
        mask_t = _iota2((CHUNK, CHUNK), 0) <= _iota2((CHUNK, CHUNK), 1)
        for h in range(H):
            wc = jnp.where(mask, w_ref[h], 0.0).astype(BF16)
            wct = jnp.where(mask_t, wt_ref[h], 0.0).astype(BF16)
            cs = slice(h * HEAD_DIM, (h + 1) * HEAD_DIM)
            dw = jnp.zeros((CHUNK, CHUNK), F32)
            dbs = jnp.zeros((CHUNK, 1), F32)
            for c in range(tr // CHUNK):
                rs = slice(c * CHUNK, (c + 1) * CHUNK)
                vch = v[rs, cs]
                mix = _dot(wc, vch) + bs_ref[h]
                dg = dgm_blk[rs, cs]
                dzu_ref[rs, cs] = (dg * mix * _gelu_grad(zu[rs, cs])).astype(BF16)
                dmix = dg * u[rs, cs]
                dbs = dbs + jnp.sum(dmix, axis=-1, keepdims=True)
                dmix_b = dmix.astype(BF16)
                dw = dw + _dot(dmix_b, vch, NT)
                dv_s[rs, cs] = _dot(wct, dmix_b)
            dw_ref[h] += jnp.where(mask, dw, 0.0)
            dbs_ref[h] += dbs
        dv = dv_s[...]
        dlg_ref[...] += jnp.sum(dv * y, axis=0, keepdims=True)
        dlb_ref[...] += jnp.sum(dv, axis=0, keepdims=True)
        dy = dv * g_ref[...]
        dgv = rstd * (dy - jnp.mean(dy, axis=-1, keepdims=True) - y * jnp.mean(dy * y, axis=-1, keepdims=True))
        dzv_ref[...] = (dgv * _gelu_grad(zv)).astype(BF16)

    full = lambda a: pl.BlockSpec(a.shape, lambda i: (0,) * a.ndim)
    rows = pl.BlockSpec((tr, DG), lambda i: (i, 0))
    return pl.pallas_call(
        body, name="gmlp_bwd", grid=(S // tr,),
        in_specs=[rows, pl.BlockSpec((tr, DG), lambda i: (i, 3)), pl.BlockSpec((tr, DG), lambda i: (i, 4)),
                  full(ln_g), full(ln_b), full(w_s), full(w_st), full(bs_col)],
        out_specs=[rows, rows, full(w_s), full(bs_col), full(ln_g), full(ln_b)],
        out_shape=[jax.ShapeDtypeStruct((S, DG), BF16), jax.ShapeDtypeStruct((S, DG), BF16),
                   jax.ShapeDtypeStruct(w_s.shape, F32), jax.ShapeDtypeStruct(bs_col.shape, F32),
                   jax.ShapeDtypeStruct(ln_g.shape, F32), jax.ShapeDtypeStruct(ln_b.shape, F32)],
        scratch_shapes=[pltpu.VMEM((tr, DG), F32)],
        compiler_params=_params(("arbitrary",)),
    )(dgm, zm, zm, ln_g, ln_b, w_s, w_st, bs_col)


def _all_gather(name, blk):
    R, C = blk.shape

    def body(x_ref, out_ref, send_sems, recv_sems, local_sem):
        x, y, c = _me()
        me, sibling = (x, y, c), (x, y, 1 - c)
        chips = [(1 - x, y), (x, 1 - y), (1 - x, 1 - y)]

        def slab(px, py, pc):
            return out_ref.at[4 * px + 2 * py + pc]

        def copy(k, block, to, src=None):
            return pltpu.make_async_remote_copy(
                src_ref=slab(*block) if src is None else src, dst_ref=slab(*block),
                send_sem=send_sems.at[k], recv_sem=recv_sems.at[k], device_id=to, device_id_type=MESH)

        mine = pltpu.make_async_copy(x_ref, slab(*me), local_sem)
        mine.start()
        first = [copy(0, me, sibling, src=x_ref)]
        first += [copy(1 + n, me, (*chip, c), src=x_ref) for n, chip in enumerate(chips)]
        for cp in first:
            cp.start()
        passed = [copy(4 + n, (*chip, c), sibling) for n, chip in enumerate(chips)]
        for n, chip in enumerate(chips):
            copy(1 + n, (*chip, c), me).wait_recv()
            passed[n].start()
        copy(0, sibling, me).wait_recv()
        for n, chip in enumerate(chips):
            copy(4 + n, (*chip, 1 - c), me).wait_recv()
        for cp in first + passed:
            cp.wait_send()
        mine.wait()

    return pl.pallas_call(
        body, name=name, out_shape=jax.ShapeDtypeStruct((N_DEV, R, C), blk.dtype),
        in_specs=[_ANY], out_specs=_ANY,
        scratch_shapes=[pltpu.SemaphoreType.DMA((7,)), pltpu.SemaphoreType.DMA((7,)), pltpu.SemaphoreType.DMA(())],
    )(blk)


def _row_tile(R, C, itemsize=4, target_bytes=2 * 1024 * 1024):
    tr = R
    while tr % 2 == 0 and tr * C * itemsize > target_bytes and (tr // 2) % 16 == 0:
        tr //= 2
    return tr


def _rs_add1(name, g4, recv, c_idx):
    _, _, R, C = g4.shape
    tr = _row_tile(R, C)

    def body(c_ref, g_ref, r_ref, hb_ref):
        hb_ref[...] = (g_ref[...] + r_ref[...].astype(F32)).astype(BF16)

    blk = pl.BlockSpec((None, tr, C), lambda p, i, c_ref: (p, i, 0))
    return pl.pallas_call(
        body, name=name,
        grid_spec=pltpu.PrefetchScalarGridSpec(
            num_scalar_prefetch=1, grid=(4, R // tr),
            in_specs=[pl.BlockSpec((None, None, tr, C), lambda p, i, c_ref: (p, c_ref[0], i, 0)), blk],
            out_specs=blk),
        out_shape=jax.ShapeDtypeStruct((4, R, C), BF16),
        compiler_params=_params(("parallel", "parallel")),
    )(c_idx, g4, recv)


def _rs_add2_own(name, g4, recv1, recv2, c_idx, p_idx):
    _, _, R, C = g4.shape
    tr = _row_tile(R, C)

    def body(c_ref, p_ref, g_ref, r1_ref, r2_ref, out_ref):
        h = g_ref[...] + r1_ref[...].astype(F32)
        out_ref[...] = ((h + r2_ref[0].astype(F32)) + r2_ref[1].astype(F32)) + r2_ref[2].astype(F32)

    return pl.pallas_call(
        body, name=name,
        grid_spec=pltpu.PrefetchScalarGridSpec(
            num_scalar_prefetch=2, grid=(R // tr,),
            in_specs=[pl.BlockSpec((None, None, tr, C), lambda i, c_ref, p_ref: (p_ref[0], c_ref[0], i, 0)),
                      pl.BlockSpec((None, tr, C), lambda i, c_ref, p_ref: (p_ref[0], i, 0)),
                      pl.BlockSpec((3, tr, C), lambda i, c_ref, p_ref: (0, i, 0))],
            out_specs=pl.BlockSpec((tr, C), lambda i, c_ref, p_ref: (i, 0))),
        out_shape=jax.ShapeDtypeStruct((R, C), F32),
        compiler_params=_params(("parallel",)),
    )(c_idx, p_idx, g4, recv1, recv2)


def _rs_add1_windows(name, g, recv, first_blocks):
    _, R, W = recv.shape
    nl = W // LANES

    def body(t_ref, *refs):
        r_ref, hb_ref = refs[nl], refs[nl + 1]
        for u in range(nl):
            cols = slice(u * LANES, (u + 1) * LANES)
            hb_ref[:, cols] = (refs[u][...] + r_ref[:, cols].astype(F32)).astype(BF16)

    blk = pl.BlockSpec((None, R, W), lambda p, t_ref: (p, 0, 0))
    return pl.pallas_call(
        body, name=name,
        grid_spec=pltpu.PrefetchScalarGridSpec(
            num_scalar_prefetch=1, grid=(4,),
            in_specs=[pl.BlockSpec((R, LANES), functools.partial(lambda u, p, t_ref: (0, t_ref[p] + u), u))
                      for u in range(nl)] + [blk],
            out_specs=blk),
        out_shape=jax.ShapeDtypeStruct((4, R, W), BF16),
        compiler_params=_params(("parallel",)),
    )(first_blocks, *([g] * nl), recv)


def _rs_add2_own_window(name, g, recv1, recv2, first_blocks, p_idx):
    _, R, W = recv1.shape
    nl = W // LANES

    def body(t_ref, p_ref, *refs):
        r1_ref, r2_ref, out_ref = refs[nl], refs[nl + 1], refs[nl + 2]
        for u in range(nl):
            cols = slice(u * LANES, (u + 1) * LANES)
            h = refs[u][...] + r1_ref[:, cols].astype(F32)
            out_ref[:, cols] = ((h + r2_ref[0, :, cols].astype(F32)) + r2_ref[1, :, cols].astype(F32)) \
                + r2_ref[2, :, cols].astype(F32)

    return pl.pallas_call(
        body, name=name,
        grid_spec=pltpu.PrefetchScalarGridSpec(
            num_scalar_prefetch=2, grid=(1,),
            in_specs=[pl.BlockSpec((R, LANES), functools.partial(lambda u, i, t, p: (0, t[p[0]] + u), u))
                      for u in range(nl)]
            + [pl.BlockSpec((None, R, W), lambda i, t, p: (p[0], 0, 0)), pl.BlockSpec((3, R, W), lambda i, t, p: (0, 0, 0))],
            out_specs=pl.BlockSpec((R, W), lambda i, t, p: (0, 0))),
        out_shape=jax.ShapeDtypeStruct((R, W), F32),
        compiler_params=_params(("arbitrary",)),
    )(first_blocks, p_idx, *([g] * nl), recv1, recv2)


def _add_windows(name, windows, first, second, n_blocks):
    _, R, W = windows.shape
    dev1 = jnp.asarray([d for d, _ in first], jnp.int32)
    blk1 = jnp.asarray([b for _, b in first], jnp.int32)
    dev2 = jnp.asarray([max(d, 0) for d, _ in second], jnp.int32)
    blk2 = jnp.asarray([b for _, b in second], jnp.int32)
    two = jnp.asarray([int(d >= 0) for d, _ in second], jnp.int32)

    G = 4
    assert n_blocks % G == 0

    def body(d1_ref, b1_ref, d2_ref, b2_ref, two_ref, *refs):
        out_ref = refs[2 * G]
        k = pl.program_id(0)
        for u in range(G):
            a_ref, b_ref = refs[u], refs[G + u]
            cols = slice(u * LANES, (u + 1) * LANES)

            @pl.when(two_ref[k * G + u] == 0)
            def _():
                out_ref[:, cols] = a_ref[...]

            @pl.when(two_ref[k * G + u] != 0)
            def _():
                out_ref[:, cols] = a_ref[...] + b_ref[...]

    def spec(u, second_owner):
        if second_owner:
            return pl.BlockSpec((None, R, LANES), lambda k, d1, b1, d2, b2, t: (d2[k * G + u], 0, b2[k * G + u]))
        return pl.BlockSpec((None, R, LANES), lambda k, d1, b1, d2, b2, t: (d1[k * G + u], 0, b1[k * G + u]))

    return pl.pallas_call(
        body, name=name,
        grid_spec=pltpu.PrefetchScalarGridSpec(
            num_scalar_prefetch=5, grid=(n_blocks // G,),
            in_specs=[spec(u, False) for u in range(G)] + [spec(u, True) for u in range(G)],
            out_specs=pl.BlockSpec((R, G * LANES), lambda k, d1, b1, d2, b2, t: (0, k))),
        out_shape=jax.ShapeDtypeStruct((R, n_blocks * LANES), windows.dtype),
        compiler_params=_params(("parallel",)),
    )(dev1, blk1, dev2, blk2, two, *([windows] * (2 * G)))


def _rs_add2(name, h, recv, p_idx):
    _, R, C = h.shape
    tr = _row_tile(R, C)

    def body(p_ref, h_ref, r_ref, out_ref):
        out_ref[...] = ((h_ref[...] + r_ref[0].astype(F32)) + r_ref[1].astype(F32)) + r_ref[2].astype(F32)

    return pl.pallas_call(
        body, name=name,
        grid_spec=pltpu.PrefetchScalarGridSpec(
            num_scalar_prefetch=1, grid=(R // tr,),
            in_specs=[pl.BlockSpec((None, tr, C), lambda i, p_ref: (p_ref[0], i, 0)),
                      pl.BlockSpec((3, tr, C), lambda i, p_ref: (0, i, 0))],
            out_specs=pl.BlockSpec((tr, C), lambda i, p_ref: (i, 0))),
        out_shape=jax.ShapeDtypeStruct((R, C), F32),
        compiler_params=_params(("parallel",)),
    )(p_idx, h, recv)


def _sum8(name, g):
    _, R, C = g.shape

    def body(g_ref, out_ref):
        acc = g_ref[0]
        for d in range(1, N_DEV):
            acc = acc + g_ref[d]
        out_ref[...] = acc

    return pl.pallas_call(body, name=name, out_shape=jax.ShapeDtypeStruct((R, C), F32),
                          compiler_params=_params())(g)


def _adamw_math(w, g, m, v):
    m = ADAM_B1 * m + (1.0 - ADAM_B1) * g
    v = ADAM_B2 * v + (1.0 - ADAM_B2) * (g * g)
    m_hat = m / (1.0 - ADAM_B1 ** ADAM_STEP)
    v_hat = v / (1.0 - ADAM_B2 ** ADAM_STEP)
    delta = -ADAM_LR * (m_hat / (jnp.sqrt(v_hat) + ADAM_EPS) + ADAM_WD * w)
    return delta, m, v


def _adamw(name, w, g, m, v):
    R, C = w.shape
    tr = _row_tile(R, C, target_bytes=1024 * 1024)
    return _row_call(name, lambda *a: (_adamw_math(*a), ()), [w, g, m, v], [], [(C, F32)] * 3, [], tr)


def _adamw_from_window(name, w, m, v, window, gate, where):
    R, C = w.shape
    W = window.shape[1]
    tr = _row_tile(R, C, target_bytes=1024 * 1024)

    def body(p_ref, w_ref, m_ref, v_ref, win_ref, gate_ref, g_out, d_out, m_out, v_out):
        off, nb, hg = p_ref[0], p_ref[1], p_ref[2]
        r, c = _iota2((W, C), 0), _iota2((W, C), 1)
        pick = jnp.logical_or(jnp.logical_and(c < nb, r == c + off),
                              jnp.logical_and(c >= nb + hg, r == c - hg + off)).astype(BF16)
        r2, c2 = _iota2((LANES, C), 0), _iota2((LANES, C), 1)
        pick_gate = jnp.logical_and(r2 < hg, c2 == nb + r2).astype(BF16)
        g = _dot3(win_ref[...], pick) + _dot3(gate_ref[...], pick_gate)
        g_out[...] = g
        d_out[...], m_out[...], v_out[...] = _adamw_math(w_ref[...], g, m_ref[...], v_ref[...])

    blk = pl.BlockSpec((tr, C), lambda i, p: (i, 0))
    return pl.pallas_call(
        body, name=name,
        grid_spec=pltpu.PrefetchScalarGridSpec(
            num_scalar_prefetch=1, grid=(R // tr,),
            in_specs=[blk, blk, blk, pl.BlockSpec((tr, W), lambda i, p: (i, 0)),
                      pl.BlockSpec((tr, LANES), lambda i, p: (i, 0))],
            out_specs=[blk] * 4),
        out_shape=[jax.ShapeDtypeStruct((R, C), F32)] * 4,
        compiler_params=_params(("parallel",)),
    )(where, w, m, v, window, gate)


def _adamw_many(name, ws, gs, ms, vs):
    n = len(ws)

    def body(*refs):
        ins, outs = refs[:4 * n], refs[4 * n:]
        for k in range(n):
            res = _adamw_math(ins[k][...], ins[n + k][...], ins[2 * n + k][...], ins[3 * n + k][...])
            for t in range(3):
                outs[t * n + k][...] = res[t]

    out = pl.pallas_call(
        body, name=name, out_shape=[jax.ShapeDtypeStruct(w.shape, F32) for _ in range(3) for w in ws],
        compiler_params=_params(),
    )(*ws, *gs, *ms, *vs)
    return out[:n], out[n:2 * n], out[2 * n:]


def _pack(parts):
    flat = []
    total = 0
    for a in parts:
        n = math.prod(a.shape)
        flat.append(a.reshape(-1).astype(F32))
        if n % LANES:
            flat.append(jnp.zeros((-n % LANES,), F32))
        total += n + (-n % LANES)
    if total % (8 * LANES):
        flat.append(jnp.zeros((-total % (8 * LANES),), F32))
    return jnp.concatenate(flat).reshape(-1, LANES)


def _unpack(packed, shapes):
    out = []
    r = 0
    for shp in shapes:
        n = math.prod(shp)
        nr = -(-n // LANES)
        out.append(packed[r:r + nr].reshape(-1)[:n].reshape(shp))
        r += nr
    return out


def kernel(x, norm_mix_g, w_in, b_f, gmlp_ln_g, gmlp_ln_b, w_s, b_s, attn_out_g, gmlp_out_g, w_out, norm_ffn_g, w_ff1, w_ff2, norm_final_g, loss_target, m_norm_mix_g, m_w_in, m_b_f, m_gmlp_ln_g, m_gmlp_ln_b, m_w_s, m_b_s, m_attn_out_g, m_gmlp_out_g, m_w_out, m_norm_ffn_g, m_w_ff1, m_w_ff2, m_norm_final_g, v_norm_mix_g, v_w_in, v_b_f, v_gmlp_ln_g, v_gmlp_ln_b, v_w_s, v_b_s, v_attn_out_g, v_gmlp_out_g, v_w_out, v_norm_ffn_g, v_w_ff1, v_w_ff2, v_norm_final_g):
    S, D = x.shape[1], x.shape[2]
    H = b_f.shape[1]
    DA = H * HEAD_DIM
    DG = gmlp_ln_g.shape[1]
    DQKV = 3 * DA
    DMAIN = DQKV + 2 * DG
    DIN = DMAIN + H
    DFF = w_ff1.shape[2] * N_DEV
    w_in_cols = w_in.shape[2]
    assert DIN == w_in_cols * N_DEV and DA == DG and D == DA + DG

    T_ATT = min(T_ATT_MAX, S)
    TR = min(TR_MAX, S)

    x0 = x[0]
    tgt = loss_target[0]
    g_final = norm_final_g.reshape(1, D)

    FB = DFF // N_DEV
    x_pos, y_pos, c_pos = _me()
    me_idx = 4 * x_pos + 2 * y_pos + c_pos

    WW = -(-(w_in_cols + LANES - 1) // LANES) * LANES
    to_main = lambda col: col if col <= DQKV else max(DQKV, col - H)
    lo = [to_main(n * w_in_cols) for n in range(N_DEV)]
    hi = [to_main((n + 1) * w_in_cols) for n in range(N_DEV)]
    starts = [v // LANES * LANES for v in lo]
    gate_dev = DQKV // w_in_cols
    n_before = DQKV - gate_dev * w_in_cols
    g0 = lo[gate_dev] - starts[gate_dev]
    stash = -(-(g0 + w_in_cols - H) // LANES) * LANES
    assert all(hi[n] <= starts[n] + WW <= DMAIN for n in range(N_DEV))
    assert gate_dev * w_in_cols <= DQKV and DQKV + H <= (gate_dev + 1) * w_in_cols and stash + LANES <= WW
    shard = w_in[0].astype(BF16)

    def my_window(n):
        if n != gate_dev:
            return lambda s: jnp.pad(s, ((0, 0), (lo[n] - starts[n], WW - w_in_cols - (lo[n] - starts[n]))))
        return lambda s: jnp.concatenate([
            jnp.zeros((D, g0), BF16), s[:, :n_before], s[:, n_before + H:],
            jnp.zeros((D, stash - g0 - (w_in_cols - H)), BF16), s[:, n_before:n_before + H],
            jnp.zeros((D, WW - stash - H), BF16)], axis=1)
    (windows_part,) = _run_jobs(
        "ag_w_in", [_job_gather_chips(lax.switch(me_idx, [my_window(n) for n in range(N_DEV)], shard))])[0]
    ((h,), _), ((windows,),) = _row_call(
        "rms_mix", lambda xb, g: ((_rms_fwd(xb, g),), ()), [x0], [norm_mix_g], [(D, BF16)], [], TR,
        jobs=[_job_gather_sibling(windows_part)])
    first, second = [], []
    for blk in range(DMAIN // LANES):
        c0 = blk * LANES
        owners = [(n, (c0 - starts[n]) // LANES) for n in range(N_DEV) if lo[n] < c0 + LANES and hi[n] > c0]
        assert 1 <= len(owners) <= 2
        first.append(owners[0])
        second.append(owners[1] if len(owners) == 2 else (-1, 0))
    w_main = _add_windows("w_in_windows", windows, first, second, DMAIN // LANES)
    w_f = windows[gate_dev, :, stash:stash + LANES]
    c_idx = jnp.reshape(c_pos, (1,)).astype(jnp.int32)
    p_idx = jnp.reshape(2 * x_pos + y_pos, (1,)).astype(jnp.int32)

    w_ff1_b = w_ff1[0].astype(BF16)
    (zm,), ((w_out_part,), (w_ff1_q1,)) = _mm_nn(
        "in_proj", h, w_main, [BF16], 2048, 1024, 2048,
        jobs=[_job_gather_chips(w_out[0].astype(BF16)), _job_gather_chips(w_ff1_b, part=(0, 1, 4))])
    (zf,) = _mm_nn("in_proj_f", h, w_f, [F32], 1024, LANES, 2048)
    bf_pad = jnp.pad(b_f, ((0, 0), (0, LANES - H)))
    f_row = _fgate_fwd(zf, bf_pad)
    NB = S // T_ATT
    f_col3 = f_row.reshape(H, S, 1)
    f_row3 = f_row.reshape(H, NB, 1, T_ATT)
    (attn, lse_col3), ((w_out_all,), (w_ff1_part,)) = _attn2_fwd(
        zm, f_col3, f_row3, T_ATT, jobs=[_job_gather_sibling(w_out_part),
                                         _job_gather_chips(w_ff1_b, part=(1, 4, 4), into=w_ff1_q1)])
    w_out_full = w_out_all.reshape(D, D)
    bs_col = b_s[0].reshape(H, CHUNK, 1)
    gm = _gmlp_fwd(zm, gmlp_ln_g, gmlp_ln_b, w_s[0], bs_col, TR)

    def merge_fn(a, g, ga, gg):
        return (jnp.concatenate([_rms_fwd(a, ga), _rms_fwd(g, gg)], axis=1),), ()
    (merged,), _ = _row_call("rms_merge", merge_fn, [attn, gm], [attn_out_g, gmlp_out_g], [(D, BF16)], [], TR)

    w_ff2_b = w_ff2[0].astype(BF16)
    TMR = min(512, S)

    def out_proj_fn(acc, res, g):
        xb = acc + res
        return (xb, _rms_fwd(xb, g)), ()
    ((x1, h2), _), ((w_ff1_all,), (w_ff2_q1,)) = _mm_rows(
        "out_proj", (S // TMR, 1), merged, pl.BlockSpec((TMR, D), lambda i, k: (i, 0)),
        w_out_full, pl.BlockSpec((D, D), lambda i, k: (0, 0)), NN, TMR, D, [x0], [norm_ffn_g],
        [(D, F32), (D, BF16)], [], out_proj_fn,
        jobs=[_job_gather_sibling(w_ff1_part), _job_gather_chips(w_ff2_b, part=(0, 1, 4))])

    tm, tn, tk = min(1024, S), min(1024, FB), min(2048, D)
    tm1 = min(2048, S)
    o_spec = pl.BlockSpec((tm1, tn), lambda i, j, k: (i, j))

    def relu_sq(acc):
        a = jnp.maximum(acc, 0.0)
        return a, a * a
    nj = FB // tn
    ff2_rest = [_job_gather_chips(w_ff2_b, part=(1, 4, 4), into=w_ff2_q1)]
    (a_act, a_sq), ((w_ff2_q2,),) = _mm(
        "ff1", (S // tm1, DFF // tn, D // tk), h2, pl.BlockSpec((tm1, tk), lambda i, j, k: (i, k)),
        w_ff1_all, pl.BlockSpec((None, tk, tn), lambda i, j, k: (j // nj, k, j % nj)), NN, (tm1, tn),
        [jax.ShapeDtypeStruct((S, DFF), BF16)] * 2, [o_spec] * 2, epilogue=relu_sq, jobs=ff2_rest)
    (w_ff2_all,) = _run_jobs("ag_w_ff2_sibling", [_job_gather_sibling(w_ff2_q2)])[0]
    w_ff2_full = w_ff2_all.reshape(DFF, D)
    def head_fn(acc, res, t, g):
        xb = acc + res
        rstd = lax.rsqrt(jnp.mean(xb * xb, axis=-1, keepdims=True) + EPS)
        xhat = xb * rstd
        err = xhat * g - t
        loss = 0.5 * jnp.sum(jnp.mean(err * err, axis=-1, keepdims=True), axis=0, keepdims=True)
        dy = err * (1.0 / D)
        dg = jnp.sum(dy * xhat, axis=0, keepdims=True)
        dxhat = dy * g
        dx = rstd * (dxhat - xhat * jnp.mean(dxhat * xhat, axis=-1, keepdims=True))
        return (dx, dx), (dg, jnp.broadcast_to(loss, (1, LANES)))
    tk_ff2 = min(1024, DFF)
    (dx2, dx2_b), (dg_final, loss_part) = _mm_rows(
        "ff2", (S // TMR, DFF // tk_ff2), a_sq, pl.BlockSpec((TMR, tk_ff2), lambda i, k: (i, k)),
        w_ff2_full, pl.BlockSpec((tk_ff2, D), lambda i, k: (k, 0)), NN, TMR, D, [x1, tgt], [g_final],
        [(D, F32), (D, BF16)], [D, LANES], head_fn)

    (da,) = _mm_nt("ff2_dx", dx2_b, w_ff2_full, [BF16], 2048, 1024, 2048, extras=[a_act],
                   epilogue=lambda acc, a: (2.0 * a.astype(F32) * acc,))
    dw_ff2, dw_ff2_b = _mm_tn("ff2_dw", a_sq, dx2_b, [F32, BF16], 1024, 2048, 1024)
    tm2, tk2 = min(2048, D), min(1024, S)
    dw1_spec = pl.BlockSpec((None, tm2, FB), lambda i, j, k: (j, i, 0))
    (dw_ff1, dw_ff1_b), ((r1_ff2,),) = _mm(
        "ff1_dw", (D // tm2, DFF // FB, S // tk2), h2, pl.BlockSpec((tk2, tm2), lambda i, j, k: (k, i)),
        da, pl.BlockSpec((tk2, FB), lambda i, j, k: (k, j)), TN, (tm2, FB),
        [jax.ShapeDtypeStruct((N_DEV, D, FB), F32), jax.ShapeDtypeStruct((N_DEV, D, FB), BF16)], [dw1_spec] * 2,
        epilogue=lambda acc: (acc, acc), jobs=[_job_scatter_sibling(dw_ff2_b.reshape(4, 2, FB, D))])
    hb_ff2 = _rs_add1("rs_add1_w_ff2", dw_ff2.reshape(4, 2, FB, D), r1_ff2, c_idx)
    def ffn_bwd_fn(dh, xb, dres, g):
        dx, dg = _rms_bwd(dh, xb, g)
        dx = dx + dres
        return (dx, dx), (dg,)
    tkb = min(1024, FB)
    nkb = FB // tkb
    ((dx1, dx1_b), (dg_ffn,)), ((r2_ff2,), (r1_ff1,)) = _mm_rows(
        "ff1_dx", (S // TMR, DFF // tkb), da, pl.BlockSpec((TMR, tkb), lambda i, k: (i, k)),
        w_ff1_all, pl.BlockSpec((None, D, tkb), lambda i, k: (k // nkb, 0, k % nkb)), NT, TMR, D, [x1, dx2],
        [norm_ffn_g], [(D, F32), (D, BF16)], [D], ffn_bwd_fn,
        jobs=[_job_scatter_chips(hb_ff2), _job_scatter_sibling(dw_ff1_b.reshape(4, 2, D, FB))])
    g_w_ff2 = _rs_add2_own("rs_add2_w_ff2", dw_ff2.reshape(4, 2, FB, D), r1_ff2, r2_ff2, c_idx, p_idx)
    hb_ff1 = _rs_add1("rs_add1_w_ff1", dw_ff1.reshape(4, 2, D, FB), r1_ff1, c_idx)

    def merge_bwd_fn(dm, a, g, ga, gg):
        da_, dga = _rms_bwd(dm[:, :DA], a, ga)
        dg_, dgg = _rms_bwd(dm[:, DA:], g, gg)
        return (da_, dg_), (dga, dgg)
    (dattn, dgm), (dg_attn, dg_gmlp) = _mm_rows(
        "out_proj_dx", (S // TMR, 1), dx1_b, pl.BlockSpec((TMR, D), lambda i, k: (i, 0)),
        w_out_full, pl.BlockSpec((D, D), lambda i, k: (0, 0)), NT, TMR, D, [attn, gm], [attn_out_g, gmlp_out_g],
        [(DA, F32), (DG, F32)], [DA, DG], merge_bwd_fn)
    dw_out, dw_out_b = _mm_tn("out_proj_dw", merged, dx1_b, [F32, BF16], 2048, 1024, 1024)

    w_st = jnp.swapaxes(w_s[0], 1, 2)
    dzu, dzv, dw_s, dbs_col, dln_g, dln_b = _gmlp_bwd(dgm, zm, gmlp_ln_g, gmlp_ln_b, w_s[0], w_st, bs_col, TR)

    delta_row = _attn_delta(dattn, attn, TR)
    lse_row3 = lse_col3.reshape(H, NB, 1, T_ATT)
    (dq, ds_rowsum), ((r2_ff1_a,), (r1_out,)) = _attn2_bwd_dq(
        zm, dattn, f_col3, f_row3, lse_col3, delta_row.reshape(H, S, 1), T_ATT,
        jobs=[_job_scatter_chips(hb_ff1, part=(0, 5, 8)),
              _job_scatter_sibling(dw_out_b.reshape(4, 2, D // N_DEV, D))])
    hb_out = _rs_add1("rs_add1_w_out", dw_out.reshape(4, 2, D // N_DEV, D), r1_out, c_idx)
    (dk, dv, df_col3), ((r2_ff1,), (r2_out,)) = _attn2_bwd_dkv(
        zm, dattn, f_col3, f_row3, lse_row3, delta_row.reshape(H, NB, 1, T_ATT),
        ds_rowsum.reshape(H, NB, 1, T_ATT), T_ATT,
        jobs=[_job_scatter_chips(hb_ff1, part=(5, 8, 8), into=r2_ff1_a), _job_scatter_chips(hb_out)])
    g_w_ff1 = _rs_add2_own("rs_add2_w_ff1", dw_ff1.reshape(4, 2, D, FB), r1_ff1, r2_ff1, c_idx, p_idx)
    g_w_out = _rs_add2_own("rs_add2_w_out", dw_out.reshape(4, 2, D // N_DEV, D), r1_out, r2_out, c_idx, p_idx)
    dzf, dbf = _fgate_bwd(df_col3.reshape(H, S), zf, bf_pad)

    dz_main = jnp.concatenate([dq, dk, dv, dzu, dzv], axis=1)
    dw_main, dw_main_b = _mm_tn("in_proj_dw", h, dz_main, [F32, BF16], 2048, 1024, 1024)
    (dw_f,), ((r1_in,),) = _mm_tn("in_proj_f_dw", h, dzf, [F32], 2048, LANES, 1024,
                                  jobs=[_job_scatter_sibling_windows(dw_main_b, starts, WW)])
    first_blocks = jnp.stack([jnp.where(c_pos == 0, starts[2 * p], starts[2 * p + 1]) // LANES
                              for p in range(4)]).astype(jnp.int32)
    hb_in = _rs_add1_windows("rs_add1_w_in", dw_main, r1_in, first_blocks)

    def mix_bwd_fn(dh_main, dz_gate, xb, dres, g, w_gate):
        dx, dg = _rms_bwd(dh_main + _dot(dz_gate, w_gate, NT), xb, g)
        return (dx + dres,), (dg,)
    tk_in = min(1024, DMAIN)
    ((grad_x,), (dg_mix,)), ((r2_in,),) = _mm_rows(
        "in_proj_dx", (S // TMR, DMAIN // tk_in), dz_main, pl.BlockSpec((TMR, tk_in), lambda i, k: (i, k)),
        w_main, pl.BlockSpec((D, tk_in), lambda i, k: (0, k)), NT, TMR, D, [dzf, x0, dx1], [norm_mix_g, w_f],
        [(D, F32)], [D], mix_bwd_fn, jobs=[_job_scatter_chips(hb_in)])
    g_window = _rs_add2_own_window("rs_add2_w_in", dw_main, r1_in, r2_in, first_blocks, p_idx)

    small_shapes = [norm_mix_g.shape, b_f.shape, gmlp_ln_g.shape, gmlp_ln_b.shape, w_s.shape, b_s.shape,
                    attn_out_g.shape, gmlp_out_g.shape, norm_ffn_g.shape, norm_final_g.shape]
    small_parts = [dg_mix, dbf[:, :H], dln_g, dln_b, dw_s, dbs_col, dg_attn, dg_gmlp, dg_ffn, dg_final]
    g_small = _sum8("small_sum", _all_gather("ag_small", _pack(small_parts + [dw_f[:, :H], loss_part])))
    *gs, g_gate, loss_sum = _unpack(g_small, small_shapes + [(D, H), (1, LANES)])
    two_d = lambda a: a.reshape(1, -1) if a.ndim == 1 else a
    ds, nms, nvs = _adamw_many(
        "adamw_small",
        [two_d(a) for a in (norm_mix_g, b_f, gmlp_ln_g, gmlp_ln_b, w_s, b_s, attn_out_g, gmlp_out_g, norm_ffn_g,
                            norm_final_g)],
        [two_d(a) for a in gs],
        [two_d(a) for a in (m_norm_mix_g, m_b_f, m_gmlp_ln_g, m_gmlp_ln_b, m_w_s, m_b_s, m_attn_out_g, m_gmlp_out_g,
                            m_norm_ffn_g, m_norm_final_g)],
        [two_d(a) for a in (v_norm_mix_g, v_b_f, v_gmlp_ln_g, v_gmlp_ln_b, v_w_s, v_b_s, v_attn_out_g, v_gmlp_out_g,
                            v_norm_ffn_g, v_norm_final_g)])
    ds, nms, nvs = [[a.reshape(s) for a, s in zip(lst, small_shapes)] for lst in (ds, nms, nvs)]

    is_gate_dev = me_idx == gate_dev
    where = jnp.stack([sum(jnp.where(me_idx == n, lo[n] - starts[n], 0) for n in range(N_DEV)),
                       jnp.where(is_gate_dev, n_before, w_in_cols), jnp.where(is_gate_dev, H, 0)]).astype(jnp.int32)
    big = {"w_in": tuple(a[None] for a in _adamw_from_window(
        "adamw_w_in", w_in[0], m_w_in[0], v_w_in[0], g_window, jnp.pad(g_gate, ((0, 0), (0, LANES - H))), where))}
    for nm, w, g, m, v in (("w_out", w_out, g_w_out, m_w_out, v_w_out),
                           ("w_ff1", w_ff1, g_w_ff1, m_w_ff1, v_w_ff1), ("w_ff2", w_ff2, g_w_ff2, m_w_ff2, v_w_ff2)):
        (d_, m_, v_), _ = _adamw("adamw_" + nm, w[0], g, m[0], v[0])
        big[nm] = (g[None], d_[None], m_[None], v_[None])

    loss = loss_sum[0, 0]

    def leaves(n):
        sm = (gs, ds, nms, nvs)[n]
        return [sm[0], big["w_in"][n], sm[1], sm[2], sm[3], sm[4], sm[5], sm[6], sm[7], big["w_out"][n], sm[8],
                big["w_ff1"][n], big["w_ff2"][n], sm[9]]

    return (loss, grad_x[None], *leaves(0), *leaves(1), *leaves(2), *leaves(3))
```

```python
import functools
import math

import jax
import jax.numpy as jnp
from jax import lax
from jax.experimental import pallas as pl
from jax.experimental.pallas import tpu as pltpu

F32 = jnp.float32
BF16 = jnp.bfloat16
MESH = pl.DeviceIdType.MESH

HEAD_DIM = 128
CHUNK = 128
EPS = 1e-6
LANES = 128
N_DEV = 8

ADAM_LR = 0.001
ADAM_B1 = 0.9
ADAM_B2 = 0.999
ADAM_EPS = 1e-08
ADAM_WD = 0.01
ADAM_STEP = 10

VMEM_LIMIT_BYTES = 56 * 1024 * 1024
T_ATT_MAX = 1024
TR_MAX = 512

NN = ((1,), (0,))
NT = ((1,), (1,))
TN = ((0,), (0,))


def _params(sem=None):
    return pltpu.CompilerParams(dimension_semantics=sem, vmem_limit_bytes=VMEM_LIMIT_BYTES)


def _dot(a, b, contract=NN):
    return lax.dot_general(a, b, (contract, ((), ())), preferred_element_type=F32)


def _dot3(x, t):
    x1 = x.astype(BF16)
    r1 = x - x1.astype(F32)
    x2 = r1.astype(BF16)
    x3 = (r1 - x2.astype(F32)).astype(BF16)
    return _dot(x1, t) + _dot(x2, t) + _dot(x3, t)


def _iota2(shape, dim):
    return lax.broadcasted_iota(jnp.int32, shape, dim)


def _row_call(name, fn, row_ins, bcast_ins, row_outs, acc_outs, tr, jobs=()):
    S = row_ins[0].shape[0]
    assert S % tr == 0
    n_ri, n_bi, n_ro, n_ao = len(row_ins), len(bcast_ins), len(row_outs), len(acc_outs)

    def body(*refs):
        ins = [r[...] for r in refs[:n_ri + n_bi]]
        ro_refs = refs[n_ri + n_bi:n_ri + n_bi + n_ro]
        ao_refs = refs[n_ri + n_bi + n_ro:]
        ro, ao = fn(*ins)
        for r, v in zip(ro_refs, ro):
            r[...] = v.astype(r.dtype)
        if n_ao:
            @pl.when(pl.program_id(0) == 0)
            def _():
                for r in ao_refs:
                    r[...] = jnp.zeros_like(r)
            for r, v in zip(ao_refs, ao):
                r[...] += v

    in_specs = [pl.BlockSpec((tr, a.shape[1]), lambda i: (i, 0)) for a in row_ins]
    in_specs += [pl.BlockSpec(a.shape, lambda i: (0, 0)) for a in bcast_ins]
    out_specs = [pl.BlockSpec((tr, d), lambda i: (i, 0)) for d, _ in row_outs]
    out_specs += [pl.BlockSpec((1, d), lambda i: (0, 0)) for d in acc_outs]
    out_shape = [jax.ShapeDtypeStruct((S, d), dt) for d, dt in row_outs]
    out_shape += [jax.ShapeDtypeStruct((1, d), F32) for d in acc_outs]
    outs, job_res = _carry_call(
        body, name=name, grid=(S // tr,), in_specs=in_specs, out_specs=out_specs, out_shape=out_shape,
        scratch_shapes=[], semantics=("arbitrary",) if n_ao else ("parallel",), args=list(row_ins) + list(bcast_ins),
        jobs=jobs)
    res = (outs[:n_ro], outs[n_ro:])
    return (res, job_res) if jobs else res


def _rms_fwd(x, g):
    rstd = lax.rsqrt(jnp.mean(x * x, axis=-1, keepdims=True) + EPS)
    return x * rstd * g


def _rms_bwd(dy, x, g):
    rstd = lax.rsqrt(jnp.mean(x * x, axis=-1, keepdims=True) + EPS)
    xhat = x * rstd
    dg = jnp.sum(dy * xhat, axis=0, keepdims=True)
    dxhat = dy * g
    dx = rstd * (dxhat - xhat * jnp.mean(dxhat * xhat, axis=-1, keepdims=True))
    return dx, dg


_GELU_C = math.sqrt(2.0 / math.pi)


def _gelu(x):
    return 0.5 * x * (1.0 + jnp.tanh(_GELU_C * (x + 0.044715 * (x * x * x))))


def _gelu_grad(x):
    t = jnp.tanh(_GELU_C * (x + 0.044715 * (x * x * x)))
    return 0.5 * (1.0 + t) + 0.5 * x * (1.0 - t * t) * (_GELU_C * (1.0 + 3.0 * 0.044715 * (x * x)))


def _me():
    return lax.axis_index("x"), lax.axis_index("y"), lax.axis_index("c")


def _other_chips(x, y):
    return [(1 - x, y), (x, 1 - y), (1 - x, 1 - y)]


_ANY = pl.BlockSpec(memory_space=pl.ANY)


class _Job:
    def __init__(self, ins, outs, n_sems, make, aliases=None):
        self.ins, self.outs, self.n_sems, self.make, self.aliases = ins, outs, n_sems, make, aliases or {}


def _job_gather_chips(blk, part=(0, 1, 1), into=None):
    R, C = blk.shape
    nr = R // part[2]
    rows = pl.ds(part[0] * nr, (part[1] - part[0]) * nr)

    def make(ins, outs, send_sems, recv_sems, base):
        x_ref, (out_ref,) = ins[0], outs
        x, y, c = _me()
        mine = 4 * x + 2 * y + c
        targets = [(x, y, 1 - c)] + [(cx, cy, c) for cx, cy in _other_chips(x, y)]

        def copy(k, slab, to):
            return pltpu.make_async_remote_copy(
                src_ref=x_ref.at[rows, :], dst_ref=out_ref.at[slab, rows, :], send_sem=send_sems.at[base + k],
                recv_sem=recv_sems.at[base + k], device_id=to, device_id_type=MESH)

        starts = [copy(k, mine, to) for k, to in enumerate(targets)]
        arrivals = [copy(k, 4 * tx + 2 * ty + tc, (tx, ty, tc)) for k, (tx, ty, tc) in enumerate(targets)]
        local = [pltpu.make_async_copy(x_ref.at[rows, :], out_ref.at[mine, rows, :], send_sems.at[base + 4])]
        return starts, arrivals, local

    out = jax.ShapeDtypeStruct((N_DEV, R, C), blk.dtype)
    if into is None:
        return _Job([blk], [out], 5, make)
    return _Job([blk, into], [out], 5, make, aliases={1: 0})


def _job_gather_sibling(part):
    def make(ins, outs, send_sems, recv_sems, base):
        (out_ref,) = outs
        x, y, c = _me()

        def copy(k, slab):
            return pltpu.make_async_remote_copy(
                src_ref=out_ref.at[slab], dst_ref=out_ref.at[slab], send_sem=send_sems.at[base + k],
                recv_sem=recv_sems.at[base + k], device_id=(x, y, 1 - c), device_id_type=MESH)

        chips = _other_chips(x, y)
        starts = [copy(k, 4 * cx + 2 * cy + c) for k, (cx, cy) in enumerate(chips)]
        arrivals = [copy(k, 4 * cx + 2 * cy + (1 - c)) for k, (cx, cy) in enumerate(chips)]
        return starts, arrivals, []

    return _Job([part], [jax.ShapeDtypeStruct(part.shape, part.dtype)], 3, make, aliases={0: 0})


def _job_scatter_sibling(gb):
    _, _, R, C = gb.shape

    def make(ins, outs, send_sems, recv_sems, base):
        (g_ref,), (recv_ref,) = ins, outs
        x, y, c = _me()
        copies = [pltpu.make_async_remote_copy(
            src_ref=g_ref.at[p, 1 - c], dst_ref=recv_ref.at[p], send_sem=send_sems.at[base + p],
            recv_sem=recv_sems.at[base + p], device_id=(x, y, 1 - c), device_id_type=MESH) for p in range(4)]
        return copies, copies, []

    return _Job([gb], [jax.ShapeDtypeStruct((4, R, C), gb.dtype)], 4, make)


def _job_scatter_sibling_windows(gb, starts, width):
    R, _ = gb.shape

    def make(ins, outs, send_sems, recv_sems, base):
        (g_ref,), (recv_ref,) = ins, outs
        x, y, c = _me()
        copies = []
        for p in range(4):
            start = pl.multiple_of(jnp.where(c == 0, starts[2 * p + 1], starts[2 * p]), LANES)
            copies.append(pltpu.make_async_remote_copy(
                src_ref=g_ref.at[:, pl.ds(start, width)], dst_ref=recv_ref.at[p], send_sem=send_sems.at[base + p],
                recv_sem=recv_sems.at[base + p], device_id=(x, y, 1 - c), device_id_type=MESH))
        return copies, copies, []

    return _Job([gb], [jax.ShapeDtypeStruct((4, R, width), gb.dtype)], 4, make)


def _job_scatter_chips(hb, part=(0, 1, 1), into=None):
    _, R, C = hb.shape
    nr = R // part[2]
    rows = pl.ds(part[0] * nr, (part[1] - part[0]) * nr)

    def make(ins, outs, send_sems, recv_sems, base):
        h_ref, (recv_ref,) = ins[0], outs
        x, y, c = _me()
        copies = [pltpu.make_async_remote_copy(
            src_ref=h_ref.at[2 * cx + cy, rows, :], dst_ref=recv_ref.at[n, rows, :], send_sem=send_sems.at[base + n],
            recv_sem=recv_sems.at[base + n], device_id=(cx, cy, c), device_id_type=MESH)
            for n, (cx, cy) in enumerate(_other_chips(x, y))]
        return copies, copies, []

    out = jax.ShapeDtypeStruct((3, R, C), hb.dtype)
    if into is None:
        return _Job([hb], [out], 3, make)
    return _Job([hb, into], [out], 3, make, aliases={1: 0})


def _carry_call(body, *, name, grid, in_specs, out_specs, out_shape, scratch_shapes, semantics, args, jobs=()):
    jobs = list(jobs)
    n_in, n_out, n_scr = len(in_specs), len(out_specs), len(scratch_shapes)
    j_ins = [a for j in jobs for a in j.ins]
    j_outs = [o for j in jobs for o in j.outs]
    n_sems = sum(j.n_sems for j in jobs)
    aliases = {}
    i0, o0 = n_in, n_out
    for j in jobs:
        for a, b in j.aliases.items():
            aliases[i0 + a] = o0 + b
        i0 += len(j.ins)
        o0 += len(j.outs)

    def full_body(*refs):
        ins = refs[:n_in]
        jin = refs[n_in:n_in + len(j_ins)]
        outs = refs[n_in + len(j_ins):n_in + len(j_ins) + n_out]
        jout = refs[n_in + len(j_ins) + n_out:n_in + len(j_ins) + n_out + len(j_outs)]
        scr = refs[n_in + len(j_ins) + n_out + len(j_outs):]
        if jobs:
            send_sems, recv_sems = scr[n_scr], scr[n_scr + 1]
            starts, arrivals, local = [], [], []
            base = i0 = o0 = 0
            for j in jobs:
                s, a, l = j.make(jin[i0:i0 + len(j.ins)], jout[o0:o0 + len(j.outs)], send_sems, recv_sems, base)
                starts += s
                arrivals += a
                local += l
                base += j.n_sems
                i0 += len(j.ins)
                o0 += len(j.outs)
            pids = [pl.program_id(d) for d in range(len(grid))]
            first = functools.reduce(jnp.logical_and, [p == 0 for p in pids])
            last = functools.reduce(jnp.logical_and, [p == n - 1 for p, n in zip(pids, grid)])

            @pl.when(first)
            def _():
                for cp in local + starts:
                    cp.start()

        body(*ins, *outs, *scr[:n_scr])

        if jobs:
            @pl.when(last)
            def _():
                for cp in arrivals:
                    cp.wait_recv()
                for cp in starts:
                    cp.wait_send()
                for cp in local:
                    cp.wait()

    sems = [pltpu.SemaphoreType.DMA((n_sems,)), pltpu.SemaphoreType.DMA((n_sems,))] if jobs else []
    res = pl.pallas_call(
        full_body, name=name, grid=grid,
        in_specs=list(in_specs) + [_ANY] * len(j_ins),
        out_specs=list(out_specs) + [_ANY] * len(j_outs),
        out_shape=list(out_shape) + j_outs,
        scratch_shapes=list(scratch_shapes) + sems,
        input_output_aliases=aliases,
        compiler_params=_params(("arbitrary",) * len(grid) if jobs else semantics),
    )(*args, *j_ins)
    body_res, job_res = res[:n_out], res[n_out:]
    per_job = []
    for j in jobs:
        per_job.append(job_res[:len(j.outs)])
        job_res = job_res[len(j.outs):]
    return body_res, per_job


def _run_jobs(name, jobs):
    def body(done_ref):
        done_ref[...] = jnp.zeros_like(done_ref)

    return _carry_call(body, name=name, grid=(1,), in_specs=[], out_specs=[pl.BlockSpec((8, LANES), lambda i: (0, 0))],
                       out_shape=[jax.ShapeDtypeStruct((8, LANES), F32)], scratch_shapes=[], semantics=("arbitrary",),
                       args=[], jobs=jobs)[1]


def _mm(name, grid, a, a_spec, b, b_spec, contract, acc_shape, out_shape, out_specs, extras=(), epilogue=None, jobs=()):
    nk = grid[2]
    n_e = len(extras)
    n_o = len(out_shape)
    if epilogue is None:
        epilogue = lambda acc: (acc,)

    def body(a_ref, b_ref, *rest):
        e_refs = rest[:n_e]
        o_refs = rest[n_e:n_e + n_o]

        def finish(total):
            res = epilogue(total, *[r[...] for r in e_refs])
            for o, r in zip(o_refs, res):
                o[...] = r.astype(o.dtype)

        if nk == 1:
            finish(_dot(a_ref[...], b_ref[...], contract))
            return
        acc = rest[n_e + n_o]
        k = pl.program_id(2)

        @pl.when(k == 0)
        def _():
            acc[...] = _dot(a_ref[...], b_ref[...], contract)

        @pl.when(jnp.logical_and(k > 0, k < nk - 1))
        def _():
            acc[...] += _dot(a_ref[...], b_ref[...], contract)

        @pl.when(k == nk - 1)
        def _():
            finish(acc[...] + _dot(a_ref[...], b_ref[...], contract))

    outs, job_res = _carry_call(
        body, name=name, grid=grid, in_specs=[a_spec, b_spec] + [s for _, s in extras],
        out_specs=list(out_specs), out_shape=list(out_shape),
        scratch_shapes=[pltpu.VMEM(acc_shape, F32)] if nk > 1 else [],
        semantics=("parallel", "parallel", "arbitrary"), args=[a, b] + [e for e, _ in extras], jobs=jobs)
    return (outs, job_res) if jobs else outs


def _mm_rows(name, grid, a, a_spec, b, b_spec, contract, tm, n, row_extras, bcast, row_outs, acc_outs, epilogue, jobs=(),
             single_buffer=False):
    nk = grid[1]
    M = grid[0] * tm
    n_x, n_b, n_ro, n_ao = len(row_extras), len(bcast), len(row_outs), len(acc_outs)

    def body(a_ref, b_ref, *rest):
        x_refs = rest[:n_x + n_b]
        ro_refs = rest[n_x + n_b:n_x + n_b + n_ro]
        ao_refs = rest[n_x + n_b + n_ro:n_x + n_b + n_ro + n_ao]
        i = pl.program_id(0)

        def finish(total):
            ro, ao = epilogue(total, *[r[...] for r in x_refs])
            for r, v in zip(ro_refs, ro):
                r[...] = v.astype(r.dtype)
            if n_ao:
                @pl.when(i == 0)
                def _():
                    for r, v in zip(ao_refs, ao):
                        r[...] = v

                @pl.when(i > 0)
                def _():
                    for r, v in zip(ao_refs, ao):
                        r[...] += v

        if nk == 1:
            finish(_dot(a_ref[...], b_ref[...], contract))
            return
        acc = rest[n_x + n_b + n_ro + n_ao]
        k = pl.program_id(1)

        @pl.when(k == 0)
        def _():
            acc[...] = _dot(a_ref[...], b_ref[...], contract)

        @pl.when(jnp.logical_and(k > 0, k < nk - 1))
        def _():
            acc[...] += _dot(a_ref[...], b_ref[...], contract)

        @pl.when(k == nk - 1)
        def _():
            finish(acc[...] + _dot(a_ref[...], b_ref[...], contract))

    mode = dict(pipeline_mode=pl.Buffered(1)) if single_buffer else {}
    in_specs = [a_spec, b_spec] + [pl.BlockSpec((tm, x.shape[1]), lambda i, k: (i, 0), **mode) for x in row_extras]
    in_specs += [pl.BlockSpec(x.shape, lambda i, k: (0,) * x.ndim) for x in bcast]
    out_specs = [pl.BlockSpec((tm, w), lambda i, k: (i, 0), **mode) for w, _ in row_outs]
    out_specs += [pl.BlockSpec((1, w), lambda i, k: (0, 0)) for w in acc_outs]
    out_shape = [jax.ShapeDtypeStruct((M, w), dt) for w, dt in row_outs]
    out_shape += [jax.ShapeDtypeStruct((1, w), F32) for w in acc_outs]
    outs, job_res = _carry_call(
        body, name=name, grid=grid, in_specs=in_specs, out_specs=out_specs, out_shape=out_shape,
        scratch_shapes=[pltpu.VMEM((tm, n), F32)] if nk > 1 else [],
        semantics=("arbitrary", "arbitrary"), args=[a, b] + list(row_extras) + list(bcast), jobs=jobs)
    res = (outs[:n_ro], outs[n_ro:])
    return (res, job_res) if jobs else res


def _mm_nn(name, a, b, out_dtypes, tm, tn, tk, extras=(), epilogue=None, jobs=()):
    M, K = a.shape
    N = b.shape[1]
    tm, tn, tk = min(tm, M), min(tn, N), min(tk, K)
    o_spec = pl.BlockSpec((tm, tn), lambda i, j, k: (i, j))
    return _mm(name, (M // tm, N // tn, K // tk),
               a, pl.BlockSpec((tm, tk), lambda i, j, k: (i, k)),
               b, pl.BlockSpec((tk, tn), lambda i, j, k: (k, j)), NN, (tm, tn),
               [jax.ShapeDtypeStruct((M, N), dt) for dt in out_dtypes], [o_spec] * len(out_dtypes),
               [(e, o_spec) for e in extras], epilogue, jobs)


def _mm_nt(name, a, b, out_dtypes, tm, tn, tk, extras=(), epilogue=None, jobs=()):
    M, K = a.shape
    N = b.shape[0]
    tm, tn, tk = min(tm, M), min(tn, N), min(tk, K)
    o_spec = pl.BlockSpec((tm, tn), lambda i, j, k: (i, j))
    return _mm(name, (M // tm, N // tn, K // tk),
               a, pl.BlockSpec((tm, tk), lambda i, j, k: (i, k)),
               b, pl.BlockSpec((tn, tk), lambda i, j, k: (j, k)), NT, (tm, tn),
               [jax.ShapeDtypeStruct((M, N), dt) for dt in out_dtypes], [o_spec] * len(out_dtypes),
               [(e, o_spec) for e in extras], epilogue, jobs)


def _mm_tn(name, a, b, out_dtypes, tm, tn, tk, jobs=()):
    K, M = a.shape
    N = b.shape[1]
    tm, tn, tk = min(tm, M), min(tn, N), min(tk, K)
    o_spec = pl.BlockSpec((tm, tn), lambda i, j, k: (i, j))
    return _mm(name, (M // tm, N // tn, K // tk),
               a, pl.BlockSpec((tk, tm), lambda i, j, k: (k, i)),
               b, pl.BlockSpec((tk, tn), lambda i, j, k: (k, j)), TN, (tm, tn),
               [jax.ShapeDtypeStruct((M, N), dt) for dt in out_dtypes], [o_spec] * len(out_dtypes),
               epilogue=lambda acc: (acc,) * len(out_dtypes), jobs=jobs)


def _fgate_fwd(zf, bf):
    S = zf.shape[0]
    nc = S // CHUNK

    def body(zf_ref, bf_ref, f_ref):
        upper = (_iota2((CHUNK, CHUNK), 0) <= _iota2((CHUNK, CHUNK), 1)).astype(BF16)
        carry = jnp.zeros((8, 1), F32)
        for c in range(nc):
            t = zf_ref[c * CHUNK:(c + 1) * CHUNK, :] + bf_ref[...]
            lf = jnp.minimum(t, 0.0) - jnp.log(1.0 + jnp.exp(-jnp.abs(t)))
            lf_rows = lf.T[0:8, :]
            f_ref[:, c * CHUNK:(c + 1) * CHUNK] = (_dot3(lf_rows, upper) + carry) * LOG2E
            carry = carry + jnp.sum(lf_rows, axis=-1, keepdims=True)

    return pl.pallas_call(
        body, name="fgate_fwd", out_shape=jax.ShapeDtypeStruct((8, S), F32),
        compiler_params=_params(),
    )(zf, bf)


def _fgate_bwd(df, zf, bf):
    S = zf.shape[0]
    nc = S // CHUNK

    def body(df_ref, zf_ref, bf_ref, dzf_ref, dbf_ref):
        lower = (_iota2((CHUNK, CHUNK), 0) >= _iota2((CHUNK, CHUNK), 1)).astype(BF16)
        carry = jnp.zeros((8, 1), F32)
        dbf = jnp.zeros((1, LANES), F32)
        for c in reversed(range(nc)):
            sl = slice(c * CHUNK, (c + 1) * CHUNK)
            df = df_ref[:, sl]
            r = _dot3(df, lower) + carry
            carry = carry + jnp.sum(df, axis=-1, keepdims=True)
            r_cols = jnp.concatenate([r, jnp.zeros((CHUNK - 8, CHUNK), F32)], axis=0).T
            t = zf_ref[sl, :] + bf_ref[...]
            dz = r_cols * (1.0 / (1.0 + jnp.exp(t)))
            dzf_ref[sl, :] = dz.astype(BF16)
            dbf = dbf + jnp.sum(dz, axis=0, keepdims=True)
        dbf_ref[...] = dbf

    return pl.pallas_call(
        body, name="fgate_bwd",
        out_shape=[jax.ShapeDtypeStruct((S, LANES), BF16), jax.ShapeDtypeStruct((1, LANES), F32)],
        compiler_params=_params(),
    )(df, zf, bf)


_NEG = -1e30
LOG2E = 1.4426950408889634
N_SPLIT = 8
N_SPLIT_DIAG = 4
DIAG_STEP = 1024


def _attn_consts(T):
    rows, cols = _iota2((T, T), 0), _iota2((T, T), 1)
    return cols <= rows, rows <= cols


def _attn2_fwd(zm, f2col, f2row, T, jobs=()):
    S = zm.shape[0]
    H = f2col.shape[0]
    nb = S // T
    c2 = LOG2E / math.sqrt(HEAD_DIM)

    def body(q_ref, k_ref, v_ref, fq_ref, fk_ref, o_ref, lse_ref, vaug_s):
        i = pl.program_id(1)

        @pl.when(i == 0)
        def _():
            vaug_s[:, :HEAD_DIM] = v_ref[...]
            vaug_s[:, HEAD_DIM:] = jnp.ones((S, HEAD_DIM), BF16)

        keep = _attn_consts(T)[0]
        TH = T // N_SPLIT

        def block(j, diagonal, state):
            r0 = pl.multiple_of(j * T, T)
            fk = fk_ref[j]
            new = []
            for g, (m_old, acc) in enumerate(state):
                rows = slice(g * TH, (g + 1) * TH)
                nk = min(T, -(-(g + 1) * TH // DIAG_STEP) * DIAG_STEP) if diagonal else T
                s = _dot(q_ref[rows, :], k_ref[pl.ds(r0, nk), :], NT) * c2 + (fq_ref[rows, :] - fk[:, :nk])
                if diagonal:
                    s = jnp.where(keep[rows, :nk], s, _NEG)
                m_new = jnp.maximum(m_old, jnp.max(s, axis=-1, keepdims=True))
                p = jnp.exp2(s - m_new).astype(BF16)
                new.append((m_new, jnp.exp2(m_old - m_new) * acc + _dot(p, vaug_s[pl.ds(r0, nk), :])))
            return tuple(new)

        init = tuple((jnp.full((TH, 1), _NEG, F32), jnp.zeros((TH, 2 * HEAD_DIM), F32)) for _ in range(N_SPLIT))
        state = lax.fori_loop(0, i, lambda j, st: block(j, False, st), init)
        state = block(i, True, state)
        for g, (m, acc) in enumerate(state):
            rows = slice(g * TH, (g + 1) * TH)
            o_ref[rows, :] = acc[:, :HEAD_DIM] / acc[:, HEAD_DIM:]
            lse_ref[rows, :] = m + jnp.log2(acc[:, HEAD_DIM:HEAD_DIM + 1])

    nh = H
    return _carry_call(
        body, name="attn_fwd", grid=(H, nb), jobs=jobs, args=[zm, zm, zm, f2col, f2row],
        semantics=("arbitrary", "arbitrary"),
        in_specs=[
            pl.BlockSpec((T, HEAD_DIM), lambda h, i: (i, h)),
            pl.BlockSpec((S, HEAD_DIM), lambda h, i: (0, nh + h)),
            pl.BlockSpec((S, HEAD_DIM), lambda h, i: (0, 2 * nh + h)),
            pl.BlockSpec((None, T, 1), lambda h, i: (h, i, 0)),
            pl.BlockSpec((None, nb, 1, T), lambda h, i: (h, 0, 0, 0)),
        ],
        out_specs=[pl.BlockSpec((T, HEAD_DIM), lambda h, i: (i, h)), pl.BlockSpec((None, T, 1), lambda h, i: (h, i, 0))],
        out_shape=[jax.ShapeDtypeStruct((S, H * HEAD_DIM), F32), jax.ShapeDtypeStruct((H, S, 1), F32)],
        scratch_shapes=[pltpu.VMEM((S, 2 * HEAD_DIM), BF16)],
    )


def _attn2_bwd_dq(zm, dattn, f2col, f2row, lse2_col, delta_col, T, jobs=()):
    S = zm.shape[0]
    H = f2col.shape[0]
    nb = S // T
    scale = 1.0 / math.sqrt(HEAD_DIM)
    c2 = LOG2E * scale

    def body(q_ref, k_ref, v_ref, do_ref, fq_ref, fk_ref, lse_ref, dl_ref, dq_ref, rs_ref, bias_s, do_s):
        i = pl.program_id(1)
        keep = _attn_consts(T)[0]
        TH = T // N_SPLIT_DIAG
        bias_s[...] = fq_ref[...] - lse_ref[...]
        do_s[...] = do_ref[...].astype(BF16)

        def part(rows, j, nk, state, masked):
            acc, rs = state
            r0 = pl.multiple_of(j * T, T)
            kb = k_ref[pl.ds(r0, nk), :]
            s = _dot(q_ref[rows, :], kb, NT) * c2 + (bias_s[rows, :] - fk_ref[j][:, :nk])
            if masked:
                s = jnp.where(keep[rows, :nk], s, _NEG)
            ds = jnp.exp2(s) * (_dot(do_s[rows, :], v_ref[pl.ds(r0, nk), :], NT) - dl_ref[rows, :])
            return acc + _dot(ds.astype(BF16), kb), rs + jnp.sum(ds, axis=-1, keepdims=True)

        def step(j, state):
            return part(slice(0, T), j, T, state, False)

        acc, rs = lax.fori_loop(0, i, step, (jnp.zeros((T, HEAD_DIM), F32), jnp.zeros((T, 1), F32)))
        for g in range(N_SPLIT_DIAG):
            rows = slice(g * TH, (g + 1) * TH)
            acc_g, rs_g = part(rows, i, (g + 1) * TH, (acc[rows, :], rs[rows, :]), True)
            dq_ref[rows, :] = (acc_g * scale).astype(BF16)
            rs_ref[rows, :] = rs_g

    nh = H
    col = pl.BlockSpec((None, T, 1), lambda h, i: (h, i, 0))
    blk = pl.BlockSpec((T, HEAD_DIM), lambda h, i: (i, h))
    return _carry_call(
        body, name="attn_bwd_dq", grid=(H, nb), jobs=jobs,
        args=[zm, zm, zm, dattn, f2col, f2row, lse2_col, delta_col], semantics=("arbitrary", "arbitrary"),
        in_specs=[
            blk,
            pl.BlockSpec((S, HEAD_DIM), lambda h, i: (0, nh + h)),
            pl.BlockSpec((S, HEAD_DIM), lambda h, i: (0, 2 * nh + h)),
            blk, col,
            pl.BlockSpec((None, nb, 1, T), lambda h, i: (h, 0, 0, 0)),
            col, col,
        ],
        out_specs=[blk, col],
        out_shape=[jax.ShapeDtypeStruct((S, H * HEAD_DIM), BF16), jax.ShapeDtypeStruct((H, S, 1), F32)],
        scratch_shapes=[pltpu.VMEM((T, 1), F32), pltpu.VMEM((T, HEAD_DIM), BF16)],
    )


def _attn2_bwd_dkv(zm, dattn, f2col, f2row, lse2_row, delta_row, rowsum_row, T, jobs=()):
    S = zm.shape[0]
    H = f2col.shape[0]
    nb = S // T
    scale = 1.0 / math.sqrt(HEAD_DIM)
    c2 = LOG2E * scale

    def body(q_ref, k_ref, v_ref, do_ref, fk_ref, fq_ref, lse_ref, dl_ref, rs_ref, dk_ref, dv_ref, df_ref):
        j = pl.program_id(1)
        keep = _attn_consts(T)[1]
        TH = T // N_SPLIT_DIAG

        def part(rows, i, c0, state, masked):
            dk, dv, df = state
            r0 = pl.multiple_of(i * T + c0, TH)
            qb = q_ref[pl.ds(r0, T - c0), :]
            do = do_ref[pl.ds(r0, T - c0), :].astype(BF16)
            bias = (fq_ref[i] - lse_ref[i])[:, c0:]
            dl = (dl_ref[i] + rs_ref[i])[:, c0:]
            st = _dot(k_ref[rows, :], qb, NT) * c2 + (bias - fk_ref[rows, :])
            if masked:
                st = jnp.where(keep[rows, c0:], st, _NEG)
            pt = jnp.exp2(st)
            dst = pt * (_dot(v_ref[rows, :], do, NT) - dl)
            return (dk + _dot(dst.astype(BF16), qb), dv + _dot(pt.astype(BF16), do),
                    df - jnp.sum(dst, axis=-1, keepdims=True))

        groups = []
        for g in range(N_SPLIT_DIAG):
            zero = (jnp.zeros((TH, HEAD_DIM), F32), jnp.zeros((TH, HEAD_DIM), F32), jnp.zeros((TH, 1), F32))
            groups.append(part(slice(g * TH, (g + 1) * TH), j, g * TH, zero, True))
        state = tuple(jnp.concatenate([grp[n] for grp in groups], axis=0) for n in range(3))
        dk, dv, df = lax.fori_loop(j + 1, nb, lambda i, st: part(slice(0, T), i, 0, st, False), state)
        dk_ref[...] = (dk * scale).astype(BF16)
        dv_ref[...] = dv.astype(BF16)
        df_ref[...] = df

    nh = H
    row = pl.BlockSpec((None, nb, 1, T), lambda h, j: (h, 0, 0, 0))
    whole = pl.BlockSpec((S, HEAD_DIM), lambda h, j: (0, h))
    kv_out = pl.BlockSpec((T, HEAD_DIM), lambda h, j: (j, h))
    col = pl.BlockSpec((None, T, 1), lambda h, j: (h, j, 0))
    return _carry_call(
        body, name="attn_bwd_dkv", grid=(H, nb), jobs=jobs,
        args=[zm, zm, zm, dattn, f2col, f2row, lse2_row, delta_row, rowsum_row],
        semantics=("arbitrary", "arbitrary"),
        in_specs=[
            whole,
            pl.BlockSpec((T, HEAD_DIM), lambda h, j: (j, nh + h)),
            pl.BlockSpec((T, HEAD_DIM), lambda h, j: (j, 2 * nh + h)),
            whole, col, row, row, row, row,
        ],
        out_specs=[kv_out, kv_out, col],
        out_shape=[jax.ShapeDtypeStruct((S, H * HEAD_DIM), BF16), jax.ShapeDtypeStruct((S, H * HEAD_DIM), BF16),
                   jax.ShapeDtypeStruct((H, S, 1), F32)],
        scratch_shapes=[],
    )


def _attn_fwd(zm, fcol, frow, T, jobs=()):
    S = zm.shape[0]
    H = fcol.shape[0]
    nb = S // T
    scale = 1.0 / math.sqrt(HEAD_DIM)

    def body(q_ref, k_ref, v_ref, fq_ref, fk_ref, o_ref, lse_ref, m_s, l_s, acc_s):
        i = pl.program_id(1)
        j = pl.program_id(2)

        @pl.when(j == 0)
        def _():
            m_s[...] = jnp.full_like(m_s, _NEG)
            l_s[...] = jnp.zeros_like(l_s)
            acc_s[...] = jnp.zeros_like(acc_s)

        @pl.when(j <= i)
        def _():
            s = _dot(q_ref[...], k_ref[...], NT) * scale + (fq_ref[...] - fk_ref[...])
            keep = (_iota2((T, T), 1) + j * T) <= (_iota2((T, T), 0) + i * T)
            s = jnp.where(keep, s, _NEG)
            m_new = jnp.maximum(m_s[...], jnp.max(s, axis=-1, keepdims=True))
            alpha = jnp.exp(m_s[...] - m_new)
            p = jnp.exp(s - m_new)
            l_s[...] = alpha * l_s[...] + jnp.sum(p, axis=-1, keepdims=True)
            acc_s[...] = alpha * acc_s[...] + _dot(p.astype(BF16), v_ref[...])
            m_s[...] = m_new

        @pl.when(j == nb - 1)
        def _():
            o_ref[...] = acc_s[...] / l_s[...]
            lse_ref[...] = m_s[...] + jnp.log(l_s[...])

    nh = H
    return _carry_call(
        body, name="attn_fwd", grid=(H, nb, nb), jobs=jobs, args=[zm, zm, zm, fcol, frow],
        semantics=("parallel", "parallel", "arbitrary"),
        in_specs=[
            pl.BlockSpec((T, HEAD_DIM), lambda h, i, j: (i, h)),
            pl.BlockSpec((T, HEAD_DIM), lambda h, i, j: (jnp.minimum(j, i), nh + h)),
            pl.BlockSpec((T, HEAD_DIM), lambda h, i, j: (jnp.minimum(j, i), 2 * nh + h)),
            pl.BlockSpec((None, T, 1), lambda h, i, j: (h, i, 0)),
            pl.BlockSpec((None, 1, T), lambda h, i, j: (h, 0, jnp.minimum(j, i))),
        ],
        out_specs=[
            pl.BlockSpec((T, HEAD_DIM), lambda h, i, j: (i, h)),
            pl.BlockSpec((None, T, 1), lambda h, i, j: (h, i, 0)),
        ],
        out_shape=[jax.ShapeDtypeStruct((S, H * HEAD_DIM), F32), jax.ShapeDtypeStruct((H, S, 1), F32)],
        scratch_shapes=[pltpu.VMEM((T, 1), F32), pltpu.VMEM((T, 1), F32), pltpu.VMEM((T, HEAD_DIM), F32)],
    )


def _attn_delta(dattn, attn, tr):
    S, DA = attn.shape
    H = DA // HEAD_DIM

    def body(do_ref, o_ref, out_ref):
        lo = _iota2((DA, LANES), 1) * HEAD_DIM
        sel = ((_iota2((DA, LANES), 0) >= lo) & (_iota2((DA, LANES), 0) < lo + HEAD_DIM)).astype(BF16)
        d = _dot3(do_ref[...] * o_ref[...], sel)
        for c in range(tr // CHUNK):
            out_ref[:, c * CHUNK:(c + 1) * CHUNK] = d[c * CHUNK:(c + 1) * CHUNK, :].T[0:H, :]

    return pl.pallas_call(
        body, name="attn_delta", grid=(S // tr,),
        in_specs=[pl.BlockSpec((tr, DA), lambda i: (i, 0))] * 2,
        out_specs=pl.BlockSpec((H, tr), lambda i: (0, i)),
        out_shape=jax.ShapeDtypeStruct((H, S), F32),
        compiler_params=_params(("parallel",)),
    )(dattn, attn)


def _attn_bwd_dq(zm, dattn, fcol, frow, lse_col, delta_col, T, jobs=()):
    S = zm.shape[0]
    H = fcol.shape[0]
    nb = S // T
    scale = 1.0 / math.sqrt(HEAD_DIM)

    def body(q_ref, k_ref, v_ref, do_ref, fq_ref, fk_ref, lse_ref, dl_ref, dq_ref, rs_ref, acc_s, rs_s):
        i = pl.program_id(1)
        j = pl.program_id(2)

        @pl.when(j == 0)
        def _():
            acc_s[...] = jnp.zeros_like(acc_s)
            rs_s[...] = jnp.zeros_like(rs_s)

        @pl.when(j <= i)
        def _():
            s = _dot(q_ref[...], k_ref[...], NT) * scale + (fq_ref[...] - fk_ref[...])
            keep = (_iota2((T, T), 1) + j * T) <= (_iota2((T, T), 0) + i * T)
            p = jnp.exp(jnp.where(keep, s - lse_ref[...], _NEG))
            dp = _dot(do_ref[...].astype(BF16), v_ref[...], NT)
            ds = p * (dp - dl_ref[...])
            acc_s[...] += _dot(ds.astype(BF16), k_ref[...])
            rs_s[...] += jnp.sum(ds, axis=-1, keepdims=True)

        @pl.when(j == nb - 1)
        def _():
            dq_ref[...] = (acc_s[...] * scale).astype(BF16)
            rs_ref[...] = rs_s[...]

    nh = H
    col = pl.BlockSpec((None, T, 1), lambda h, i, j: (h, i, 0))
    return _carry_call(
        body, name="attn_bwd_dq", grid=(H, nb, nb), jobs=jobs,
        args=[zm, zm, zm, dattn, fcol, frow, lse_col, delta_col], semantics=("parallel", "parallel", "arbitrary"),
        in_specs=[
            pl.BlockSpec((T, HEAD_DIM), lambda h, i, j: (i, h)),
            pl.BlockSpec((T, HEAD_DIM), lambda h, i, j: (jnp.minimum(j, i), nh + h)),
            pl.BlockSpec((T, HEAD_DIM), lambda h, i, j: (jnp.minimum(j, i), 2 * nh + h)),
            pl.BlockSpec((T, HEAD_DIM), lambda h, i, j: (i, h)),
            col,
            pl.BlockSpec((None, 1, T), lambda h, i, j: (h, 0, jnp.minimum(j, i))),
            col, col,
        ],
        out_specs=[pl.BlockSpec((T, HEAD_DIM), lambda h, i, j: (i, h)), col],
        out_shape=[jax.ShapeDtypeStruct((S, H * HEAD_DIM), BF16), jax.ShapeDtypeStruct((H, S, 1), F32)],
        scratch_shapes=[pltpu.VMEM((T, HEAD_DIM), F32), pltpu.VMEM((T, 1), F32)],
    )


def _attn_bwd_dkv(zm, dattn, fcol, frow, lse_row, delta_row, rowsum_row, T, jobs=()):
    S = zm.shape[0]
    H = fcol.shape[0]
    nb = S // T
    scale = 1.0 / math.sqrt(HEAD_DIM)

    def body(q_ref, k_ref, v_ref, do_ref, fk_ref, fq_ref, lse_ref, dl_ref, rs_ref,
             dk_ref, dv_ref, df_ref, dk_s, dv_s, df_s):
        j = pl.program_id(1)
        i = pl.program_id(2)

        @pl.when(i == 0)
        def _():
            dk_s[...] = jnp.zeros_like(dk_s)
            dv_s[...] = jnp.zeros_like(dv_s)
            df_s[...] = jnp.zeros_like(df_s)

        @pl.when(i >= j)
        def _():
            st = _dot(k_ref[...], q_ref[...], NT) * scale + (fq_ref[...] - fk_ref[...])
            keep = (_iota2((T, T), 0) + j * T) <= (_iota2((T, T), 1) + i * T)
            pt = jnp.exp(jnp.where(keep, st - lse_ref[...], _NEG))
            do = do_ref[...].astype(BF16)
            dpt = _dot(v_ref[...], do, NT)
            dst = pt * (dpt - (dl_ref[...] + rs_ref[...]))
            dv_s[...] += _dot(pt.astype(BF16), do)
            dk_s[...] += _dot(dst.astype(BF16), q_ref[...])
            df_s[...] -= jnp.sum(dst, axis=-1, keepdims=True)

        @pl.when(i == nb - 1)
        def _():
            dk_ref[...] = (dk_s[...] * scale).astype(BF16)
            dv_ref[...] = dv_s[...].astype(BF16)
            df_ref[...] = df_s[...]

    nh = H
    row = pl.BlockSpec((None, 1, T), lambda h, j, i: (h, 0, jnp.maximum(i, j)))
    kv_out = pl.BlockSpec((T, HEAD_DIM), lambda h, j, i: (j, h))
    return _carry_call(
        body, name="attn_bwd_dkv", grid=(H, nb, nb), jobs=jobs,
        args=[zm, zm, zm, dattn, fcol, frow, lse_row, delta_row, rowsum_row],
        semantics=("parallel", "parallel", "arbitrary"),
        in_specs=[
            pl.BlockSpec((T, HEAD_DIM), lambda h, j, i: (jnp.maximum(i, j), h)),
            pl.BlockSpec((T, HEAD_DIM), lambda h, j, i: (j, nh + h)),
            pl.BlockSpec((T, HEAD_DIM), lambda h, j, i: (j, 2 * nh + h)),
            pl.BlockSpec((T, HEAD_DIM), lambda h, j, i: (jnp.maximum(i, j), h)),
            pl.BlockSpec((None, T, 1), lambda h, j, i: (h, j, 0)),
            row, row, row, row,
        ],
        out_specs=[kv_out, kv_out, pl.BlockSpec((None, T, 1), lambda h, j, i: (h, j, 0))],
        out_shape=[jax.ShapeDtypeStruct((S, H * HEAD_DIM), BF16), jax.ShapeDtypeStruct((S, H * HEAD_DIM), BF16),
                   jax.ShapeDtypeStruct((H, S, 1), F32)],
        scratch_shapes=[pltpu.VMEM((T, HEAD_DIM), F32), pltpu.VMEM((T, HEAD_DIM), F32), pltpu.VMEM((T, 1), F32)],
    )


def _ln_stats(x):
    mu = jnp.mean(x, axis=-1, keepdims=True)
    xc = x - mu
    rstd = lax.rsqrt(jnp.mean(xc * xc, axis=-1, keepdims=True) + EPS)
    return xc * rstd, rstd


def _tril_mask():
    return _iota2((CHUNK, CHUNK), 0) >= _iota2((CHUNK, CHUNK), 1)


def _gmlp_fwd(zm, ln_g, ln_b, w_s, bs_col, tr):
    S = zm.shape[0]
    H = w_s.shape[0]
    DG = H * HEAD_DIM

    def body(zu_ref, zv_ref, g_ref, b_ref, w_ref, bs_ref, out_ref):
        u = _gelu(zu_ref[...].astype(F32))
        y, _ = _ln_stats(_gelu(zv_ref[...].astype(F32)))
        v = (y * g_ref[...] + b_ref[...]).astype(BF16)
        mask = _tril_mask()
        for h in range(H):
            wc = jnp.where(mask, w_ref[h], 0.0).astype(BF16)
            cs = slice(h * HEAD_DIM, (h + 1) * HEAD_DIM)
            for c in range(tr // CHUNK):
                rs = slice(c * CHUNK, (c + 1) * CHUNK)
                mix = _dot(wc, v[rs, cs]) + bs_ref[h]
                out_ref[rs, cs] = u[rs, cs] * mix

    full = lambda a: pl.BlockSpec(a.shape, lambda i: (0,) * a.ndim)
    return pl.pallas_call(
        body, name="gmlp_fwd", grid=(S // tr,),
        in_specs=[pl.BlockSpec((tr, DG), lambda i: (i, 3)), pl.BlockSpec((tr, DG), lambda i: (i, 4)),
                  full(ln_g), full(ln_b), full(w_s), full(bs_col)],
        out_specs=pl.BlockSpec((tr, DG), lambda i: (i, 0)),
        out_shape=jax.ShapeDtypeStruct((S, DG), F32),
        compiler_params=_params(("parallel",)),
    )(zm, zm, ln_g, ln_b, w_s, bs_col)


def _gmlp_bwd(dgm, zm, ln_g, ln_b, w_s, w_st, bs_col, tr):
    S = zm.shape[0]
    H = w_s.shape[0]
    DG = H * HEAD_DIM

    def body(dg_ref, zu_ref, zv_ref, g_ref, b_ref, w_ref, wt_ref, bs_ref,
             dzu_ref, dzv_ref, dw_ref, dbs_ref, dlg_ref, dlb_ref, dv_s):
        @pl.when(pl.program_id(0) == 0)
        def _():
            dw_ref[...] = jnp.zeros_like(dw_ref)
            dbs_ref[...] = jnp.zeros_like(dbs_ref)
            dlg_ref[...] = jnp.zeros_like(dlg_ref)
            dlb_ref[...] = jnp.zeros_like(dlb_ref)

        zu = zu_ref[...].astype(F32)
        zv = zv_ref[...].astype(F32)
        u = _gelu(zu)
        y, rstd = _ln_stats(_gelu(zv))
        v = (y * g_ref[...] + b_ref[...]).astype(BF16)
        dgm_blk = dg_ref[...]
        mask = _tril_mask()
        mask_t = _iota2((CHUNK, CHUNK), 0) <= _iota2((CHUNK, CHUNK), 1)
        for h in range(H):
            wc = jnp.where(mask, w_ref[h], 0.0).astype(BF16)
            wct = jnp.where(mask_t, wt_ref[h], 0.0).astype(BF16)
            cs = slice(h * HEAD_DIM, (h + 1) * HEAD_DIM)
            dw = jnp.zeros((CHUNK, CHUNK), F32)
            dbs = jnp.zeros((CHUNK, 1), F32)
            for c in range(tr // CHUNK):
                rs = slice(c * CHUNK, (c + 1) * CHUNK)
                vch = v[rs, cs]
                mix = _dot(wc, vch) + bs_ref[h]
                dg = dgm_blk[rs, cs]
                dzu_ref[rs, cs] = (dg * mix * _gelu_grad(zu[rs, cs])).astype(BF16)
                dmix = dg * u[rs, cs]
                dbs = dbs + jnp.sum(dmix, axis=-1, keepdims=True)
                dmix_b = dmix.astype(BF16)
                dw = dw + _dot(dmix_b, vch, NT)
                dv_s[rs, cs] = _dot(wct, dmix_b)
            dw_ref[h] += jnp.where(mask, dw, 0.0)
            dbs_ref[h] += dbs
        dv = dv_s[...]
        dlg_ref[...] += jnp.sum(dv * y, axis=0, keepdims=True)
        dlb_ref[...] += jnp.sum(dv, axis=0, keepdims=True)
        dy = dv * g_ref[...]
        dgv = rstd * (dy - jnp.mean(dy, axis=-1, keepdims=True) - y * jnp.mean(dy * y, axis=-1, keepdims=True))
        dzv_ref[...] = (dgv * _gelu_grad(zv)).astype(BF16)

    full = lambda a: pl.BlockSpec(a.shape, lambda i: (0,) * a.ndim)
    rows = pl.BlockSpec((tr, DG), lambda i: (i, 0))
    return pl.pallas_call(
        body, name="gmlp_bwd", grid=(S // tr,),
        in_specs=[rows, pl.BlockSpec((tr, DG), lambda i: (i, 3)), pl.BlockSpec((tr, DG), lambda i: (i, 4)),
                  full(ln_g), full(ln_b), full(w_s), full(w_st), full(bs_col)],
        out_specs=[rows, rows, full(w_s), full(bs_col), full(ln_g), full(ln_b)],
        out_shape=[jax.ShapeDtypeStruct((S, DG), BF16), jax.ShapeDtypeStruct((S, DG), BF16),
                   jax.ShapeDtypeStruct(w_s.shape, F32), jax.ShapeDtypeStruct(bs_col.shape, F32),
                   jax.ShapeDtypeStruct(ln_g.shape, F32), jax.ShapeDtypeStruct(ln_b.shape, F32)],
        scratch_shapes=[pltpu.VMEM((tr, DG), F32)],
        compiler_params=_params(("arbitrary",)),
    )(dgm, zm, zm, ln_g, ln_b, w_s, w_st, bs_col)


def _all_gather(name, blk):
    R, C = blk.shape

    def body(x_ref, out_ref, send_sems, recv_sems, local_sem):
        x, y, c = _me()
        me, sibling = (x, y, c), (x, y, 1 - c)
        chips = [(1 - x, y), (x, 1 - y), (1 - x, 1 - y)]

        def slab(px, py, pc):
            return out_ref.at[4 * px + 2 * py + pc]

        def copy(k, block, to, src=None):
            return pltpu.make_async_remote_copy(
                src_ref=slab(*block) if src is None else src, dst_ref=slab(*block),
                send_sem=send_sems.at[k], recv_sem=recv_sems.at[k], device_id=to, device_id_type=MESH)

        mine = pltpu.make_async_copy(x_ref, slab(*me), local_sem)
        mine.start()
        first = [copy(0, me, sibling, src=x_ref)]
        first += [copy(1 + n, me, (*chip, c), src=x_ref) for n, chip in enumerate(chips)]
        for cp in first:
            cp.start()
        passed = [copy(4 + n, (*chip, c), sibling) for n, chip in enumerate(chips)]
        for n, chip in enumerate(chips):
            copy(1 + n, (*chip, c), me).wait_recv()
            passed[n].start()
        copy(0, sibling, me).wait_recv()
        for n, chip in enumerate(chips):
            copy(4 + n, (*chip, 1 - c), me).wait_recv()
        for cp in first + passed:
            cp.wait_send()
        mine.wait()

    return pl.pallas_call(
        body, name=name, out_shape=jax.ShapeDtypeStruct((N_DEV, R, C), blk.dtype),
        in_specs=[_ANY], out_specs=_ANY,
        scratch_shapes=[pltpu.SemaphoreType.DMA((7,)), pltpu.SemaphoreType.DMA((7,)), pltpu.SemaphoreType.DMA(())],
    )(blk)


def _row_tile(R, C, itemsize=4, target_bytes=2 * 1024 * 1024):
    tr = R
    while tr % 2 == 0 and tr * C * itemsize > target_bytes and (tr // 2) % 16 == 0:
        tr //= 2
    return tr


def _rs_add1(name, g4, recv, c_idx):
    _, _, R, C = g4.shape
    tr = _row_tile(R, C)

    def body(c_ref, g_ref, r_ref, hb_ref):
        hb_ref[...] = (g_ref[...] + r_ref[...].astype(F32)).astype(BF16)

    blk = pl.BlockSpec((None, tr, C), lambda p, i, c_ref: (p, i, 0))
    return pl.pallas_call(
        body, name=name,
        grid_spec=pltpu.PrefetchScalarGridSpec(
            num_scalar_prefetch=1, grid=(4, R // tr),
            in_specs=[pl.BlockSpec((None, None, tr, C), lambda p, i, c_ref: (p, c_ref[0], i, 0)), blk],
            out_specs=blk),
        out_shape=jax.ShapeDtypeStruct((4, R, C), BF16),
        compiler_params=_params(("parallel", "parallel")),
    )(c_idx, g4, recv)


def _rs_add2_own(name, g4, recv1, recv2, c_idx, p_idx):
    _, _, R, C = g4.shape
    tr = _row_tile(R, C)

    def body(c_ref, p_ref, g_ref, r1_ref, r2_ref, out_ref):
        h = g_ref[...] + r1_ref[...].astype(F32)
        out_ref[...] = ((h + r2_ref[0].astype(F32)) + r2_ref[1].astype(F32)) + r2_ref[2].astype(F32)

    return pl.pallas_call(
        body, name=name,
        grid_spec=pltpu.PrefetchScalarGridSpec(
            num_scalar_prefetch=2, grid=(R // tr,),
            in_specs=[pl.BlockSpec((None, None, tr, C), lambda i, c_ref, p_ref: (p_ref[0], c_ref[0], i, 0)),
                      pl.BlockSpec((None, tr, C), lambda i, c_ref, p_ref: (p_ref[0], i, 0)),
                      pl.BlockSpec((3, tr, C), lambda i, c_ref, p_ref: (0, i, 0))],
            out_specs=pl.BlockSpec((tr, C), lambda i, c_ref, p_ref: (i, 0))),
        out_shape=jax.ShapeDtypeStruct((R, C), F32),
        compiler_params=_params(("parallel",)),
    )(c_idx, p_idx, g4, recv1, recv2)


def _rs_add1_windows(name, g, recv, first_blocks):
    _, R, W = recv.shape
    nl = W // LANES

    def body(t_ref, *refs):
        r_ref, hb_ref = refs[nl], refs[nl + 1]
        for u in range(nl):
            cols = slice(u * LANES, (u + 1) * LANES)
            hb_ref[:, cols] = (refs[u][...] + r_ref[:, cols].astype(F32)).astype(BF16)

    blk = pl.BlockSpec((None, R, W), lambda p, t_ref: (p, 0, 0))
    return pl.pallas_call(
        body, name=name,
        grid_spec=pltpu.PrefetchScalarGridSpec(
            num_scalar_prefetch=1, grid=(4,),
            in_specs=[pl.BlockSpec((R, LANES), functools.partial(lambda u, p, t_ref: (0, t_ref[p] + u), u))
                      for u in range(nl)] + [blk],
            out_specs=blk),
        out_shape=jax.ShapeDtypeStruct((4, R, W), BF16),
        compiler_params=_params(("parallel",)),
    )(first_blocks, *([g] * nl), recv)


def _rs_add2_own_window(name, g, recv1, recv2, first_blocks, p_idx):
    _, R, W = recv1.shape
    nl = W // LANES

    def body(t_ref, p_ref, *refs):
        r1_ref, r2_ref, out_ref = refs[nl], refs[nl + 1], refs[nl + 2]
        for u in range(nl):
            cols = slice(u * LANES, (u + 1) * LANES)
            h = refs[u][...] + r1_ref[:, cols].astype(F32)
            out_ref[:, cols] = ((h + r2_ref[0, :, cols].astype(F32)) + r2_ref[1, :, cols].astype(F32)) \
                + r2_ref[2, :, cols].astype(F32)

    return pl.pallas_call(
        body, name=name,
        grid_spec=pltpu.PrefetchScalarGridSpec(
            num_scalar_prefetch=2, grid=(1,),
            in_specs=[pl.BlockSpec((R, LANES), functools.partial(lambda u, i, t, p: (0, t[p[0]] + u), u))
                      for u in range(nl)]
            + [pl.BlockSpec((None, R, W), lambda i, t, p: (p[0], 0, 0)), pl.BlockSpec((3, R, W), lambda i, t, p: (0, 0, 0))],
            out_specs=pl.BlockSpec((R, W), lambda i, t, p: (0, 0))),
        out_shape=jax.ShapeDtypeStruct((R, W), F32),
        compiler_params=_params(("arbitrary",)),
    )(first_blocks, p_idx, *([g] * nl), recv1, recv2)


def _add_windows(name, windows, first, second, n_blocks):
    _, R, W = windows.shape
    dev1 = jnp.asarray([d for d, _ in first], jnp.int32)
    blk1 = jnp.asarray([b for _, b in first], jnp.int32)
    dev2 = jnp.asarray([max(d, 0) for d, _ in second], jnp.int32)
    blk2 = jnp.asarray([b for _, b in second], jnp.int32)
    two = jnp.asarray([int(d >= 0) for d, _ in second], jnp.int32)

    G = 4
    assert n_blocks % G == 0

    def body(d1_ref, b1_ref, d2_ref, b2_ref, two_ref, *refs):
        out_ref = refs[2 * G]
        k = pl.program_id(0)
        for u in range(G):
            a_ref, b_ref = refs[u], refs[G + u]
            cols = slice(u * LANES, (u + 1) * LANES)

            @pl.when(two_ref[k * G + u] == 0)
            def _():
                out_ref[:, cols] = a_ref[...]

            @pl.when(two_ref[k * G + u] != 0)
            def _():
                out_ref[:, cols] = a_ref[...] + b_ref[...]

    def spec(u, second_owner):
        if second_owner:
            return pl.BlockSpec((None, R, LANES), lambda k, d1, b1, d2, b2, t: (d2[k * G + u], 0, b2[k * G + u]))
        return pl.BlockSpec((None, R, LANES), lambda k, d1, b1, d2, b2, t: (d1[k * G + u], 0, b1[k * G + u]))

    return pl.pallas_call(
        body, name=name,
        grid_spec=pltpu.PrefetchScalarGridSpec(
            num_scalar_prefetch=5, grid=(n_blocks // G,),
            in_specs=[spec(u, False) for u in range(G)] + [spec(u, True) for u in range(G)],
            out_specs=pl.BlockSpec((R, G * LANES), lambda k, d1, b1, d2, b2, t: (0, k))),
        out_shape=jax.ShapeDtypeStruct((R, n_blocks * LANES), windows.dtype),
        compiler_params=_params(("parallel",)),
    )(dev1, blk1, dev2, blk2, two, *([windows] * (2 * G)))


def _rs_add2(name, h, recv, p_idx):
    _, R, C = h.shape
    tr = _row_tile(R, C)

    def body(p_ref, h_ref, r_ref, out_ref):
        out_ref[...] = ((h_ref[...] + r_ref[0].astype(F32)) + r_ref[1].astype(F32)) + r_ref[2].astype(F32)

    return pl.pallas_call(
        body, name=name,
        grid_spec=pltpu.PrefetchScalarGridSpec(
            num_scalar_prefetch=1, grid=(R // tr,),
            in_specs=[pl.BlockSpec((None, tr, C), lambda i, p_ref: (p_ref[0], i, 0)),
                      pl.BlockSpec((3, tr, C), lambda i, p_ref: (0, i, 0))],
            out_specs=pl.BlockSpec((tr, C), lambda i, p_ref: (i, 0))),
        out_shape=jax.ShapeDtypeStruct((R, C), F32),
        compiler_params=_params(("parallel",)),
    )(p_idx, h, recv)


def _sum8(name, g):
    _, R, C = g.shape

    def body(g_ref, out_ref):
        acc = g_ref[0]
        for d in range(1, N_DEV):
            acc = acc + g_ref[d]
        out_ref[...] = acc

    return pl.pallas_call(body, name=name, out_shape=jax.ShapeDtypeStruct((R, C), F32),
                          compiler_params=_params())(g)


def _adamw_math(w, g, m, v):
    m = ADAM_B1 * m + (1.0 - ADAM_B1) * g
    v = ADAM_B2 * v + (1.0 - ADAM_B2) * (g * g)
    m_hat = m / (1.0 - ADAM_B1 ** ADAM_STEP)
    v_hat = v / (1.0 - ADAM_B2 ** ADAM_STEP)
    delta = -ADAM_LR * (m_hat / (jnp.sqrt(v_hat) + ADAM_EPS) + ADAM_WD * w)
    return delta, m, v


def _adamw(name, w, g, m, v):
    R, C = w.shape
    tr = _row_tile(R, C, target_bytes=1024 * 1024)
    return _row_call(name, lambda *a: (_adamw_math(*a), ()), [w, g, m, v], [], [(C, F32)] * 3, [], tr)


def _adamw_from_window(name, w, m, v, window, gate, where):
    R, C = w.shape
    W = window.shape[1]
    tr = _row_tile(R, C, target_bytes=1024 * 1024)

    def body(p_ref, w_ref, m_ref, v_ref, win_ref, gate_ref, g_out, d_out, m_out, v_out):
        off, nb, hg = p_ref[0], p_ref[1], p_ref[2]
        r, c = _iota2((W, C), 0), _iota2((W, C), 1)
        pick = jnp.logical_or(jnp.logical_and(c < nb, r == c + off),
                              jnp.logical_and(c >= nb + hg, r == c - hg + off)).astype(BF16)
        r2, c2 = _iota2((LANES, C), 0), _iota2((LANES, C), 1)
        pick_gate = jnp.logical_and(r2 < hg, c2 == nb + r2).astype(BF16)
        g = _dot3(win_ref[...], pick) + _dot3(gate_ref[...], pick_gate)
        g_out[...] = g
        d_out[...], m_out[...], v_out[...] = _adamw_math(w_ref[...], g, m_ref[...], v_ref[...])

    blk = pl.BlockSpec((tr, C), lambda i, p: (i, 0))
    return pl.pallas_call(
        body, name=name,
        grid_spec=pltpu.PrefetchScalarGridSpec(
            num_scalar_prefetch=1, grid=(R // tr,),
            in_specs=[blk, blk, blk, pl.BlockSpec((tr, W), lambda i, p: (i, 0)),
                      pl.BlockSpec((tr, LANES), lambda i, p: (i, 0))],
            out_specs=[blk] * 4),
        out_shape=[jax.ShapeDtypeStruct((R, C), F32)] * 4,
        compiler_params=_params(("parallel",)),
    )(where, w, m, v, window, gate)


def _adamw_many(name, ws, gs, ms, vs):
    n = len(ws)

    def body(*refs):
        ins, outs = refs[:4 * n], refs[4 * n:]
        for k in range(n):
            res = _adamw_math(ins[k][...], ins[n + k][...], ins[2 * n + k][...], ins[3 * n + k][...])
            for t in range(3):
                outs[t * n + k][...] = res[t]

    out = pl.pallas_call(
        body, name=name, out_shape=[jax.ShapeDtypeStruct(w.shape, F32) for _ in range(3) for w in ws],
        compiler_params=_params(),
    )(*ws, *gs, *ms, *vs)
    return out[:n], out[n:2 * n], out[2 * n:]


def _pack(parts):
    flat = []
    total = 0
    for a in parts:
        n = math.prod(a.shape)
        flat.append(a.reshape(-1).astype(F32))
        if n % LANES:
            flat.append(jnp.zeros((-n % LANES,), F32))
        total += n + (-n % LANES)
    if total % (8 * LANES):
        flat.append(jnp.zeros((-total % (8 * LANES),), F32))
    return jnp.concatenate(flat).reshape(-1, LANES)


def _unpack(packed, shapes):
    out = []
    r = 0
    for shp in shapes:
        n = math.prod(shp)
        nr = -(-n // LANES)
        out.append(packed[r:r + nr].reshape(-1)[:n].reshape(shp))
        r += nr
    return out


def kernel(x, norm_mix_g, w_in, b_f, gmlp_ln_g, gmlp_ln_b, w_s, b_s, attn_out_g, gmlp_out_g, w_out, norm_ffn_g, w_ff1, w_ff2, norm_final_g, loss_target, m_norm_mix_g, m_w_in, m_b_f, m_gmlp_ln_g, m_gmlp_ln_b, m_w_s, m_b_s, m_attn_out_g, m_gmlp_out_g, m_w_out, m_norm_ffn_g, m_w_ff1, m_w_ff2, m_norm_final_g, v_norm_mix_g, v_w_in, v_b_f, v_gmlp_ln_g, v_gmlp_ln_b, v_w_s, v_b_s, v_attn_out_g, v_gmlp_out_g, v_w_out, v_norm_ffn_g, v_w_ff1, v_w_ff2, v_norm_final_g):
    S, D = x.shape[1], x.shape[2]
    H = b_f.shape[1]
    DA = H * HEAD_DIM
    DG = gmlp_ln_g.shape[1]
    DQKV = 3 * DA
    DMAIN = DQKV + 2 * DG
    DIN = DMAIN + H
    DFF = w_ff1.shape[2] * N_DEV
    w_in_cols = w_in.shape[2]
    assert DIN == w_in_cols * N_DEV and DA == DG and D == DA + DG

    T_ATT = min(T_ATT_MAX, S)
    TR = min(TR_MAX, S)

    x0 = x[0]
    tgt = loss_target[0]
    g_final = norm_final_g.reshape(1, D)

    FB = DFF // N_DEV
    x_pos, y_pos, c_pos = _me()
    me_idx = 4 * x_pos + 2 * y_pos + c_pos

    WW = -(-(w_in_cols + LANES - 1) // LANES) * LANES
    to_main = lambda col: col if col <= DQKV else max(DQKV, col - H)
    lo = [to_main(n * w_in_cols) for n in range(N_DEV)]
    hi = [to_main((n + 1) * w_in_cols) for n in range(N_DEV)]
    starts = [v // LANES * LANES for v in lo]
    gate_dev = DQKV // w_in_cols
    n_before = DQKV - gate_dev * w_in_cols
    g0 = lo[gate_dev] - starts[gate_dev]
    stash = -(-(g0 + w_in_cols - H) // LANES) * LANES
    assert all(hi[n] <= starts[n] + WW <= DMAIN for n in range(N_DEV))
    assert gate_dev * w_in_cols <= DQKV and DQKV + H <= (gate_dev + 1) * w_in_cols and stash + LANES <= WW
    shard = w_in[0].astype(BF16)

    def my_window(n):
        if n != gate_dev:
            return lambda s: jnp.pad(s, ((0, 0), (lo[n] - starts[n], WW - w_in_cols - (lo[n] - starts[n]))))
        return lambda s: jnp.concatenate([
            jnp.zeros((D, g0), BF16), s[:, :n_before], s[:, n_before + H:],
            jnp.zeros((D, stash - g0 - (w_in_cols - H)), BF16), s[:, n_before:n_before + H],
            jnp.zeros((D, WW - stash - H), BF16)], axis=1)
    (windows_part,) = _run_jobs(
        "ag_w_in", [_job_gather_chips(lax.switch(me_idx, [my_window(n) for n in range(N_DEV)], shard))])[0]
    ((h,), _), ((windows,),) = _row_call(
        "rms_mix", lambda xb, g: ((_rms_fwd(xb, g),), ()), [x0], [norm_mix_g], [(D, BF16)], [], TR,
        jobs=[_job_gather_sibling(windows_part)])
    first, second = [], []
    for blk in range(DMAIN // LANES):
        c0 = blk * LANES
        owners = [(n, (c0 - starts[n]) // LANES) for n in range(N_DEV) if lo[n] < c0 + LANES and hi[n] > c0]
        assert 1 <= len(owners) <= 2
        first.append(owners[0])
        second.append(owners[1] if len(owners) == 2 else (-1, 0))
    w_main = _add_windows("w_in_windows", windows, first, second, DMAIN // LANES)
    w_f = windows[gate_dev, :, stash:stash + LANES]
    c_idx = jnp.reshape(c_pos, (1,)).astype(jnp.int32)
    p_idx = jnp.reshape(2 * x_pos + y_pos, (1,)).astype(jnp.int32)

    w_ff1_b = w_ff1[0].astype(BF16)
    (zm,), ((w_out_part,), (w_ff1_q1,)) = _mm_nn(
        "in_proj", h, w_main, [BF16], 2048, 1024, 2048,
        jobs=[_job_gather_chips(w_out[0].astype(BF16)), _job_gather_chips(w_ff1_b, part=(0, 1, 4))])
    (zf,) = _mm_nn("in_proj_f", h, w_f, [F32], 1024, LANES, 2048)
    bf_pad = jnp.pad(b_f, ((0, 0), (0, LANES - H)))
    f_row = _fgate_fwd(zf, bf_pad)
    NB = S // T_ATT
    f_col3 = f_row.reshape(H, S, 1)
    f_row3 = f_row.reshape(H, NB, 1, T_ATT)
    (attn, lse_col3), ((w_out_all,), (w_ff1_part,)) = _attn2_fwd(
        zm, f_col3, f_row3, T_ATT, jobs=[_job_gather_sibling(w_out_part),
                                         _job_gather_chips(w_ff1_b, part=(1, 4, 4), into=w_ff1_q1)])
    w_out_full = w_out_all.reshape(D, D)
    bs_col = b_s[0].reshape(H, CHUNK, 1)
    gm = _gmlp_fwd(zm, gmlp_ln_g, gmlp_ln_b, w_s[0], bs_col, TR)

    def merge_fn(a, g, ga, gg):
        return (jnp.concatenate([_rms_fwd(a, ga), _rms_fwd(g, gg)], axis=1),), ()
    (merged,), _ = _row_call("rms_merge", merge_fn, [attn, gm], [attn_out_g, gmlp_out_g], [(D, BF16)], [], TR)

    w_ff2_b = w_ff2[0].astype(BF16)
    TMR = min(512, S)

    def out_proj_fn(acc, res, g):
        xb = acc + res
        return (xb, _rms_fwd(xb, g)), ()
    ((x1, h2), _), ((w_ff1_all,), (w_ff2_q1,)) = _mm_rows(
        "out_proj", (S // TMR, 1), merged, pl.BlockSpec((TMR, D), lambda i, k: (i, 0)),
        w_out_full, pl.BlockSpec((D, D), lambda i, k: (0, 0)), NN, TMR, D, [x0], [norm_ffn_g],
        [(D, F32), (D, BF16)], [], out_proj_fn,
        jobs=[_job_gather_sibling(w_ff1_part), _job_gather_chips(w_ff2_b, part=(0, 1, 4))])

    tm, tn, tk = min(1024, S), min(1024, FB), min(2048, D)
    tm1 = min(2048, S)
    o_spec = pl.BlockSpec((tm1, tn), lambda i, j, k: (i, j))

    def relu_sq(acc):
        a = jnp.maximum(acc, 0.0)
        return a, a * a
    nj = FB // tn
    ff2_rest = [_job_gather_chips(w_ff2_b, part=(1, 4, 4), into=w_ff2_q1)]
    (a_act, a_sq), ((w_ff2_q2,),) = _mm(
        "ff1", (S // tm1, DFF // tn, D // tk), h2, pl.BlockSpec((tm1, tk), lambda i, j, k: (i, k)),
        w_ff1_all, pl.BlockSpec((None, tk, tn), lambda i, j, k: (j // nj, k, j % nj)), NN, (tm1, tn),
        [jax.ShapeDtypeStruct((S, DFF), BF16)] * 2, [o_spec] * 2, epilogue=relu_sq, jobs=ff2_rest)
    (w_ff2_all,) = _run_jobs("ag_w_ff2_sibling", [_job_gather_sibling(w_ff2_q2)])[0]
    w_ff2_full = w_ff2_all.reshape(DFF, D)
    def head_fn(acc, res, t, g):
        xb = acc + res
        rstd = lax.rsqrt(jnp.mean(xb * xb, axis=-1, keepdims=True) + EPS)
        xhat = xb * rstd
        err = xhat * g - t
        loss = 0.5 * jnp.sum(jnp.mean(err * err, axis=-1, keepdims=True), axis=0, keepdims=True)
        dy = err * (1.0 / D)
        dg = jnp.sum(dy * xhat, axis=0, keepdims=True)
        dxhat = dy * g
        dx = rstd * (dxhat - xhat * jnp.mean(dxhat * xhat, axis=-1, keepdims=True))
        return (dx, dx), (dg, jnp.broadcast_to(loss, (1, LANES)))
    tk_ff2 = min(512, DFF)
    TMB = min(1024, S)
    (dx2, dx2_b), (dg_final, loss_part) = _mm_rows(
        "ff2", (S // TMB, DFF // tk_ff2), a_sq, pl.BlockSpec((TMB, tk_ff2), lambda i, k: (i, k)),
        w_ff2_full, pl.BlockSpec((tk_ff2, D), lambda i, k: (k, 0)), NN, TMB, D, [x1, tgt], [g_final],
        [(D, F32), (D, BF16)], [D, LANES], head_fn, single_buffer=True)

    (da,) = _mm_nt("ff2_dx", dx2_b, w_ff2_full, [BF16], 2048, 1024, 2048, extras=[a_act],
                   epilogue=lambda acc, a: (2.0 * a.astype(F32) * acc,))
    dw_ff2, dw_ff2_b = _mm_tn("ff2_dw", a_sq, dx2_b, [F32, BF16], 1024, 2048, 1024)
    tm2, tk2 = min(2048, D), min(1024, S)
    dw1_spec = pl.BlockSpec((None, tm2, FB), lambda i, j, k: (j, i, 0))
    (dw_ff1, dw_ff1_b), ((r1_ff2,),) = _mm(
        "ff1_dw", (D // tm2, DFF // FB, S // tk2), h2, pl.BlockSpec((tk2, tm2), lambda i, j, k: (k, i)),
        da, pl.BlockSpec((tk2, FB), lambda i, j, k: (k, j)), TN, (tm2, FB),
        [jax.ShapeDtypeStruct((N_DEV, D, FB), F32), jax.ShapeDtypeStruct((N_DEV, D, FB), BF16)], [dw1_spec] * 2,
        epilogue=lambda acc: (acc, acc), jobs=[_job_scatter_sibling(dw_ff2_b.reshape(4, 2, FB, D))])
    hb_ff2 = _rs_add1("rs_add1_w_ff2", dw_ff2.reshape(4, 2, FB, D), r1_ff2, c_idx)
    def ffn_bwd_fn(dh, xb, dres, g):
        dx, dg = _rms_bwd(dh, xb, g)
        dx = dx + dres
        return (dx, dx), (dg,)
    tkb = min(1024, FB)
    nkb = FB // tkb
    ((dx1, dx1_b), (dg_ffn,)), ((r2_ff2,), (r1_ff1,)) = _mm_rows(
        "ff1_dx", (S // TMR, DFF // tkb), da, pl.BlockSpec((TMR, tkb), lambda i, k: (i, k)),
        w_ff1_all, pl.BlockSpec((None, D, tkb), lambda i, k: (k // nkb, 0, k % nkb)), NT, TMR, D, [x1, dx2],
        [norm_ffn_g], [(D, F32), (D, BF16)], [D], ffn_bwd_fn,
        jobs=[_job_scatter_chips(hb_ff2), _job_scatter_sibling(dw_ff1_b.reshape(4, 2, D, FB))])
    g_w_ff2 = _rs_add2_own("rs_add2_w_ff2", dw_ff2.reshape(4, 2, FB, D), r1_ff2, r2_ff2, c_idx, p_idx)
    hb_ff1 = _rs_add1("rs_add1_w_ff1", dw_ff1.reshape(4, 2, D, FB), r1_ff1, c_idx)

    def merge_bwd_fn(dm, a, g, ga, gg):
        da_, dga = _rms_bwd(dm[:, :DA], a, ga)
        dg_, dgg = _rms_bwd(dm[:, DA:], g, gg)
        return (da_, dg_), (dga, dgg)
    (dattn, dgm), (dg_attn, dg_gmlp) = _mm_rows(
        "out_proj_dx", (S // TMR, 1), dx1_b, pl.BlockSpec((TMR, D), lambda i, k: (i, 0)),
        w_out_full, pl.BlockSpec((D, D), lambda i, k: (0, 0)), NT, TMR, D, [attn, gm], [attn_out_g, gmlp_out_g],
        [(DA, F32), (DG, F32)], [DA, DG], merge_bwd_fn)
    dw_out, dw_out_b = _mm_tn("out_proj_dw", merged, dx1_b, [F32, BF16], 2048, 1024, 1024)

    w_st = jnp.swapaxes(w_s[0], 1, 2)
    dzu, dzv, dw_s, dbs_col, dln_g, dln_b = _gmlp_bwd(dgm, zm, gmlp_ln_g, gmlp_ln_b, w_s[0], w_st, bs_col, TR)

    delta_row = _attn_delta(dattn, attn, TR)
    lse_row3 = lse_col3.reshape(H, NB, 1, T_ATT)
    (dq, ds_rowsum), ((r2_ff1_a,), (r1_out,)) = _attn2_bwd_dq(
        zm, dattn, f_col3, f_row3, lse_col3, delta_row.reshape(H, S, 1), T_ATT,
        jobs=[_job_scatter_chips(hb_ff1, part=(0, 5, 8)),
              _job_scatter_sibling(dw_out_b.reshape(4, 2, D // N_DEV, D))])
    hb_out = _rs_add1("rs_add1_w_out", dw_out.reshape(4, 2, D // N_DEV, D), r1_out, c_idx)
    (dk, dv, df_col3), ((r2_ff1,), (r2_out,)) = _attn2_bwd_dkv(
        zm, dattn, f_col3, f_row3, lse_row3, delta_row.reshape(H, NB, 1, T_ATT),
        ds_rowsum.reshape(H, NB, 1, T_ATT), T_ATT,
        jobs=[_job_scatter_chips(hb_ff1, part=(5, 8, 8), into=r2_ff1_a), _job_scatter_chips(hb_out)])
    g_w_ff1 = _rs_add2_own("rs_add2_w_ff1", dw_ff1.reshape(4, 2, D, FB), r1_ff1, r2_ff1, c_idx, p_idx)
    g_w_out = _rs_add2_own("rs_add2_w_out", dw_out.reshape(4, 2, D // N_DEV, D), r1_out, r2_out, c_idx, p_idx)
    dzf, dbf = _fgate_bwd(df_col3.reshape(H, S), zf, bf_pad)

    dz_main = jnp.concatenate([dq, dk, dv, dzu, dzv], axis=1)
    dw_main, dw_main_b = _mm_tn("in_proj_dw", h, dz_main, [F32, BF16], 2048, 1024, 1024)
    (dw_f,), ((r1_in,),) = _mm_tn("in_proj_f_dw", h, dzf, [F32], 2048, LANES, 1024,
                                  jobs=[_job_scatter_sibling_windows(dw_main_b, starts, WW)])
    first_blocks = jnp.stack([jnp.where(c_pos == 0, starts[2 * p], starts[2 * p + 1]) // LANES
                              for p in range(4)]).astype(jnp.int32)
    hb_in = _rs_add1_windows("rs_add1_w_in", dw_main, r1_in, first_blocks)

    def mix_bwd_fn(dh_main, dz_gate, xb, dres, g, w_gate):
        dx, dg = _rms_bwd(dh_main + _dot(dz_gate, w_gate, NT), xb, g)
        return (dx + dres,), (dg,)
    tk_in = min(1024, DMAIN)
    ((grad_x,), (dg_mix,)), ((r2_in,),) = _mm_rows(
        "in_proj_dx", (S // TMR, DMAIN // tk_in), dz_main, pl.BlockSpec((TMR, tk_in), lambda i, k: (i, k)),
        w_main, pl.BlockSpec((D, tk_in), lambda i, k: (0, k)), NT, TMR, D, [dzf, x0, dx1], [norm_mix_g, w_f],
        [(D, F32)], [D], mix_bwd_fn, jobs=[_job_scatter_chips(hb_in)])
    g_window = _rs_add2_own_window("rs_add2_w_in", dw_main, r1_in, r2_in, first_blocks, p_idx)

    small_shapes = [norm_mix_g.shape, b_f.shape, gmlp_ln_g.shape, gmlp_ln_b.shape, w_s.shape, b_s.shape,
                    attn_out_g.shape, gmlp_out_g.shape, norm_ffn_g.shape, norm_final_g.shape]
    small_parts = [dg_mix, dbf[:, :H], dln_g, dln_b, dw_s, dbs_col, dg_attn, dg_gmlp, dg_ffn, dg_final]
    g_small = _sum8("small_sum", _all_gather("ag_small", _pack(small_parts + [dw_f[:, :H], loss_part])))
    *gs, g_gate, loss_sum = _unpack(g_small, small_shapes + [(D, H), (1, LANES)])
    two_d = lambda a: a.reshape(1, -1) if a.ndim == 1 else a
    ds, nms, nvs = _adamw_many(
        "adamw_small",
        [two_d(a) for a in (norm_mix_g, b_f, gmlp_ln_g, gmlp_ln_b, w_s, b_s, attn_out_g, gmlp_out_g, norm_ffn_g,
                            norm_final_g)],
        [two_d(a) for a in gs],
        [two_d(a) for a in (m_norm_mix_g, m_b_f, m_gmlp_ln_g, m_gmlp_ln_b, m_w_s, m_b_s, m_attn_out_g, m_gmlp_out_g,
                            m_norm_ffn_g, m_norm_final_g)],
        [two_d(a) for a in (v_norm_mix_g, v_b_f, v_gmlp_ln_g, v_gmlp_ln_b, v_w_s, v_b_s, v_attn_out_g, v_gmlp_out_g,
                            v_norm_ffn_g, v_norm_final_g)])
    ds, nms, nvs = [[a.reshape(s) for a, s in zip(lst, small_shapes)] for lst in (ds, nms, nvs)]

    is_gate_dev = me_idx == gate_dev
    where = jnp.stack([sum(jnp.where(me_idx == n, lo[n] - starts[n], 0) for n in range(N_DEV)),
                       jnp.where(is_gate_dev, n_before, w_in_cols), jnp.where(is_gate_dev, H, 0)]).astype(jnp.int32)
    big = {"w_in": tuple(a[None] for a in _adamw_from_window(
        "adamw_w_in", w_in[0], m_w_in[0], v_w_in[0], g_window, jnp.pad(g_gate, ((0, 0), (0, LANES - H))), where))}
    for nm, w, g, m, v in (("w_out", w_out, g_w_out, m_w_out, v_w_out),
                           ("w_ff1", w_ff1, g_w_ff1, m_w_ff1, v_w_ff1), ("w_ff2", w_ff2, g_w_ff2, m_w_ff2, v_w_ff2)):
        (d_, m_, v_), _ = _adamw("adamw_" + nm, w[0], g, m[0], v[0])
        big[nm] = (g[None], d_[None], m_[None], v_[None])

    loss = loss_sum[0, 0]

    def leaves(n):
        sm = (gs, ds, nms, nvs)[n]
        return [sm[0], big["w_in"][n], sm[1], sm[2], sm[3], sm[4], sm[5], sm[6], sm[7], big["w_out"][n], sm[8],
                big["w_ff1"][n], big["w_ff2"][n], sm[9]]

    return (loss, grad_x[None], *leaves(0), *leaves(1), *leaves(2), *leaves(3))
```

```python
import functools
import math

import jax
import jax.numpy as jnp
from jax import lax
from jax.experimental import pallas as pl
from jax.experimental.pallas import tpu as pltpu

F32 = jnp.float32
BF16 = jnp.bfloat16
MESH = pl.DeviceIdType.MESH

HEAD_DIM = 128
CHUNK = 128
EPS = 1e-6
LANES = 128
N_DEV = 8

ADAM_LR = 0.001
ADAM_B1 = 0.9
ADAM_B2 = 0.999
ADAM_EPS = 1e-08
ADAM_WD = 0.01
ADAM_STEP = 10

VMEM_LIMIT_BYTES = 56 * 1024 * 1024
T_ATT_MAX = 1024
TR_MAX = 512

NN = ((1,), (0,))
NT = ((1,), (1,))
TN = ((0,), (0,))


def _params(sem=None):
    return pltpu.CompilerParams(dimension_semantics=sem, vmem_limit_bytes=VMEM_LIMIT_BYTES)


def _dot(a, b, contract=NN):
    return lax.dot_general(a, b, (contract, ((), ())), preferred_element_type=F32)


def _dot3(x, t):
    x1 = x.astype(BF16)
    r1 = x - x1.astype(F32)
    x2 = r1.astype(BF16)
    x3 = (r1 - x2.astype(F32)).astype(BF16)
    return _dot(x1, t) + _dot(x2, t) + _dot(x3, t)


def _iota2(shape, dim):
    return lax.broadcasted_iota(jnp.int32, shape, dim)


def _row_call(name, fn, row_ins, bcast_ins, row_outs, acc_outs, tr, jobs=()):
    S = row_ins[0].shape[0]
    assert S % tr == 0
    n_ri, n_bi, n_ro, n_ao = len(row_ins), len(bcast_ins), len(row_outs), len(acc_outs)

    def body(*refs):
        ins = [r[...] for r in refs[:n_ri + n_bi]]
        ro_refs = refs[n_ri + n_bi:n_ri + n_bi + n_ro]
        ao_refs = refs[n_ri + n_bi + n_ro:]
        ro, ao = fn(*ins)
        for r, v in zip(ro_refs, ro):
            r[...] = v.astype(r.dtype)
        if n_ao:
            @pl.when(pl.program_id(0) == 0)
            def _():
                for r in ao_refs:
                    r[...] = jnp.zeros_like(r)
            for r, v in zip(ao_refs, ao):
                r[...] += v

    in_specs = [pl.BlockSpec((tr, a.shape[1]), lambda i: (i, 0)) for a in row_ins]
    in_specs += [pl.BlockSpec(a.shape, lambda i: (0, 0)) for a in bcast_ins]
    out_specs = [pl.BlockSpec((tr, d), lambda i: (i, 0)) for d, _ in row_outs]
    out_specs += [pl.BlockSpec((1, d), lambda i: (0, 0)) for d in acc_outs]
    out_shape = [jax.ShapeDtypeStruct((S, d), dt) for d, dt in row_outs]
    out_shape += [jax.ShapeDtypeStruct((1, d), F32) for d in acc_outs]
    outs, job_res = _carry_call(
        body, name=name, grid=(S // tr,), in_specs=in_specs, out_specs=out_specs, out_shape=out_shape,
        scratch_shapes=[], semantics=("arbitrary",) if n_ao else ("parallel",), args=list(row_ins) + list(bcast_ins),
        jobs=jobs)
    res = (outs[:n_ro], outs[n_ro:])
    return (res, job_res) if jobs else res


def _rms_fwd(x, g):
    rstd = lax.rsqrt(jnp.mean(x * x, axis=-1, keepdims=True) + EPS)
    return x * rstd * g


def _rms_bwd(dy, x, g):
    rstd = lax.rsqrt(jnp.mean(x * x, axis=-1, keepdims=True) + EPS)
    xhat = x * rstd
    dg = jnp.sum(dy * xhat, axis=0, keepdims=True)
    dxhat = dy * g
    dx = rstd * (dxhat - xhat * jnp.mean(dxhat * xhat, axis=-1, keepdims=True))
    return dx, dg


_GELU_C = math.sqrt(2.0 / math.pi)


def _gelu(x):
    return 0.5 * x * (1.0 + jnp.tanh(_GELU_C * (x + 0.044715 * (x * x * x))))


def _gelu_grad(x):
    t = jnp.tanh(_GELU_C * (x + 0.044715 * (x * x * x)))
    return 0.5 * (1.0 + t) + 0.5 * x * (1.0 - t * t) * (_GELU_C * (1.0 + 3.0 * 0.044715 * (x * x)))


def _me():
    return lax.axis_index("x"), lax.axis_index("y"), lax.axis_index("c")


def _other_chips(x, y):
    return [(1 - x, y), (x, 1 - y), (1 - x, 1 - y)]


_ANY = pl.BlockSpec(memory_space=pl.ANY)


class _Job:
    def __init__(self, ins, outs, n_sems, make, aliases=None):
        self.ins, self.outs, self.n_sems, self.make, self.aliases = ins, outs, n_sems, make, aliases or {}


def _job_gather_chips(blk, part=(0, 1, 1), into=None):
    R, C = blk.shape
    nr = R // part[2]
    rows = pl.ds(part[0] * nr, (part[1] - part[0]) * nr)

    def make(ins, outs, send_sems, recv_sems, base):
        x_ref, (out_ref,) = ins[0], outs
        x, y, c = _me()
        mine = 4 * x + 2 * y + c
        targets = [(x, y, 1 - c)] + [(cx, cy, c) for cx, cy in _other_chips(x, y)]

        def copy(k, slab, to):
            return pltpu.make_async_remote_copy(
                src_ref=x_ref.at[rows, :], dst_ref=out_ref.at[slab, rows, :], send_sem=send_sems.at[base + k],
                recv_sem=recv_sems.at[base + k], device_id=to, device_id_type=MESH)

        starts = [copy(k, mine, to) for k, to in enumerate(targets)]
        arrivals = [copy(k, 4 * tx + 2 * ty + tc, (tx, ty, tc)) for k, (tx, ty, tc) in enumerate(targets)]
        local = [pltpu.make_async_copy(x_ref.at[rows, :], out_ref.at[mine, rows, :], send_sems.at[base + 4])]
        return starts, arrivals, local

    out = jax.ShapeDtypeStruct((N_DEV, R, C), blk.dtype)
    if into is None:
        return _Job([blk], [out], 5, make)
    return _Job([blk, into], [out], 5, make, aliases={1: 0})


def _job_gather_sibling(part):
    def make(ins, outs, send_sems, recv_sems, base):
        (out_ref,) = outs
        x, y, c = _me()

        def copy(k, slab):
            return pltpu.make_async_remote_copy(
                src_ref=out_ref.at[slab], dst_ref=out_ref.at[slab], send_sem=send_sems.at[base + k],
                recv_sem=recv_sems.at[base + k], device_id=(x, y, 1 - c), device_id_type=MESH)

        chips = _other_chips(x, y)
        starts = [copy(k, 4 * cx + 2 * cy + c) for k, (cx, cy) in enumerate(chips)]
        arrivals = [copy(k, 4 * cx + 2 * cy + (1 - c)) for k, (cx, cy) in enumerate(chips)]
        return starts, arrivals, []

    return _Job([part], [jax.ShapeDtypeStruct(part.shape, part.dtype)], 3, make, aliases={0: 0})


def _job_scatter_sibling(gb):
    _, _, R, C = gb.shape

    def make(ins, outs, send_sems, recv_sems, base):
        (g_ref,), (recv_ref,) = ins, outs
        x, y, c = _me()
        copies = [pltpu.make_async_remote_copy(
            src_ref=g_ref.at[p, 1 - c], dst_ref=recv_ref.at[p], send_sem=send_sems.at[base + p],
            recv_sem=recv_sems.at[base + p], device_id=(x, y, 1 - c), device_id_type=MESH) for p in range(4)]
        return copies, copies, []

    return _Job([gb], [jax.ShapeDtypeStruct((4, R, C), gb.dtype)], 4, make)


def _job_scatter_sibling_windows(gb, starts, width):
    R, _ = gb.shape

    def make(ins, outs, send_sems, recv_sems, base):
        (g_ref,), (recv_ref,) = ins, outs
        x, y, c = _me()
        copies = []
        for p in range(4):
            start = pl.multiple_of(jnp.where(c == 0, starts[2 * p + 1], starts[2 * p]), LANES)
            copies.append(pltpu.make_async_remote_copy(
                src_ref=g_ref.at[:, pl.ds(start, width)], dst_ref=recv_ref.at[p], send_sem=send_sems.at[base + p],
                recv_sem=recv_sems.at[base + p], device_id=(x, y, 1 - c), device_id_type=MESH))
        return copies, copies, []

    return _Job([gb], [jax.ShapeDtypeStruct((4, R, width), gb.dtype)], 4, make)


def _job_scatter_chips(hb, part=(0, 1, 1), into=None):
    _, R, C = hb.shape
    nr = R // part[2]
    rows = pl.ds(part[0] * nr, (part[1] - part[0]) * nr)

    def make(ins, outs, send_sems, recv_sems, base):
        h_ref, (recv_ref,) = ins[0], outs
        x, y, c = _me()
        copies = [pltpu.make_async_remote_copy(
            src_ref=h_ref.at[2 * cx + cy, rows, :], dst_ref=recv_ref.at[n, rows, :], send_sem=send_sems.at[base + n],
            recv_sem=recv_sems.at[base + n], device_id=(cx, cy, c), device_id_type=MESH)
            for n, (cx, cy) in enumerate(_other_chips(x, y))]
        return copies, copies, []

    out = jax.ShapeDtypeStruct((3, R, C), hb.dtype)
    if into is None:
        return _Job([hb], [out], 3, make)
    return _Job([hb, into], [out], 3, make, aliases={1: 0})


def _carry_call(body, *, name, grid, in_specs, out_specs, out_shape, scratch_shapes, semantics, args, jobs=()):
    jobs = list(jobs)
    n_in, n_out, n_scr = len(in_specs), len(out_specs), len(scratch_shapes)
    j_ins = [a for j in jobs for a in j.ins]
    j_outs = [o for j in jobs for o in j.outs]
    n_sems = sum(j.n_sems for j in jobs)
    aliases = {}
    i0, o0 = n_in, n_out
    for j in jobs:
        for a, b in j.aliases.items():
            aliases[i0 + a] = o0 + b
        i0 += len(j.ins)
        o0 += len(j.outs)

    def full_body(*refs):
        ins = refs[:n_in]
        jin = refs[n_in:n_in + len(j_ins)]
        outs = refs[n_in + len(j_ins):n_in + len(j_ins) + n_out]
        jout = refs[n_in + len(j_ins) + n_out:n_in + len(j_ins) + n_out + len(j_outs)]
        scr = refs[n_in + len(j_ins) + n_out + len(j_outs):]
        if jobs:
            send_sems, recv_sems = scr[n_scr], scr[n_scr + 1]
            starts, arrivals, local = [], [], []
            base = i0 = o0 = 0
            for j in jobs:
                s, a, l = j.make(jin[i0:i0 + len(j.ins)], jout[o0:o0 + len(j.outs)], send_sems, recv_sems, base)
                starts += s
                arrivals += a
                local += l
                base += j.n_sems
                i0 += len(j.ins)
                o0 += len(j.outs)
            pids = [pl.program_id(d) for d in range(len(grid))]
            first = functools.reduce(jnp.logical_and, [p == 0 for p in pids])
            last = functools.reduce(jnp.logical_and, [p == n - 1 for p, n in zip(pids, grid)])

            @pl.when(first)
            def _():
                for cp in local + starts:
                    cp.start()

        body(*ins, *outs, *scr[:n_scr])

        if jobs:
            @pl.when(last)
            def _():
                for cp in arrivals:
                    cp.wait_recv()
                for cp in starts:
                    cp.wait_send()
                for cp in local:
                    cp.wait()

    sems = [pltpu.SemaphoreType.DMA((n_sems,)), pltpu.SemaphoreType.DMA((n_sems,))] if jobs else []
    res = pl.pallas_call(
        full_body, name=name, grid=grid,
        in_specs=list(in_specs) + [_ANY] * len(j_ins),
        out_specs=list(out_specs) + [_ANY] * len(j_outs),
        out_shape=list(out_shape) + j_outs,
        scratch_shapes=list(scratch_shapes) + sems,
        input_output_aliases=aliases,
        compiler_params=_params(("arbitrary",) * len(grid) if jobs else semantics),
    )(*args, *j_ins)
    body_res, job_res = res[:n_out], res[n_out:]
    per_job = []
    for j in jobs:
        per_job.append(job_res[:len(j.outs)])
        job_res = job_res[len(j.outs):]
    return body_res, per_job


def _run_jobs(name, jobs):
    def body(done_ref):
        done_ref[...] = jnp.zeros_like(done_ref)

    return _carry_call(body, name=name, grid=(1,), in_specs=[], out_specs=[pl.BlockSpec((8, LANES), lambda i: (0, 0))],
                       out_shape=[jax.ShapeDtypeStruct((8, LANES), F32)], scratch_shapes=[], semantics=("arbitrary",),
                       args=[], jobs=jobs)[1]


def _mm(name, grid, a, a_spec, b, b_spec, contract, acc_shape, out_shape, out_specs, extras=(), epilogue=None, jobs=()):
    nk = grid[2]
    n_e = len(extras)
    n_o = len(out_shape)
    if epilogue is None:
        epilogue = lambda acc: (acc,)

    def body(a_ref, b_ref, *rest):
        e_refs = rest[:n_e]
        o_refs = rest[n_e:n_e + n_o]

        def finish(total):
            res = epilogue(total, *[r[...] for r in e_refs])
            for o, r in zip(o_refs, res):
                o[...] = r.astype(o.dtype)

        if nk == 1:
            finish(_dot(a_ref[...], b_ref[...], contract))
            return
        acc = rest[n_e + n_o]
        k = pl.program_id(2)

        @pl.when(k == 0)
        def _():
            acc[...] = _dot(a_ref[...], b_ref[...], contract)

        @pl.when(jnp.logical_and(k > 0, k < nk - 1))
        def _():
            acc[...] += _dot(a_ref[...], b_ref[...], contract)

        @pl.when(k == nk - 1)
        def _():
            finish(acc[...] + _dot(a_ref[...], b_ref[...], contract))

    outs, job_res = _carry_call(
        body, name=name, grid=grid, in_specs=[a_spec, b_spec] + [s for _, s in extras],
        out_specs=list(out_specs), out_shape=list(out_shape),
        scratch_shapes=[pltpu.VMEM(acc_shape, F32)] if nk > 1 else [],
        semantics=("parallel", "parallel", "arbitrary"), args=[a, b] + [e for e, _ in extras], jobs=jobs)
    return (outs, job_res) if jobs else outs


def _mm_rows(name, grid, a, a_spec, b, b_spec, contract, tm, n, row_extras, bcast, row_outs, acc_outs, epilogue, jobs=()):
    nk = grid[1]
    M = grid[0] * tm
    n_x, n_b, n_ro, n_ao = len(row_extras), len(bcast), len(row_outs), len(acc_outs)

    def body(a_ref, b_ref, *rest):
        x_refs = rest[:n_x + n_b]
        ro_refs = rest[n_x + n_b:n_x + n_b + n_ro]
        ao_refs = rest[n_x + n_b + n_ro:n_x + n_b + n_ro + n_ao]
        i = pl.program_id(0)

        def finish(total):
            ro, ao = epilogue(total, *[r[...] for r in x_refs])
            for r, v in zip(ro_refs, ro):
                r[...] = v.astype(r.dtype)
            if n_ao:
                @pl.when(i == 0)
                def _():
                    for r, v in zip(ao_refs, ao):
                        r[...] = v

                @pl.when(i > 0)
                def _():
                    for r, v in zip(ao_refs, ao):
                        r[...] += v

        if nk == 1:
            finish(_dot(a_ref[...], b_ref[...], contract))
            return
        acc = rest[n_x + n_b + n_ro + n_ao]
        k = pl.program_id(1)

        @pl.when(k == 0)
        def _():
            acc[...] = _dot(a_ref[...], b_ref[...], contract)

        @pl.when(jnp.logical_and(k > 0, k < nk - 1))
        def _():
            acc[...] += _dot(a_ref[...], b_ref[...], contract)

        @pl.when(k == nk - 1)
        def _():
            finish(acc[...] + _dot(a_ref[...], b_ref[...], contract))

    in_specs = [a_spec, b_spec] + [pl.BlockSpec((tm, x.shape[1]), lambda i, k: (i, 0)) for x in row_extras]
    in_specs += [pl.BlockSpec(x.shape, lambda i, k: (0,) * x.ndim) for x in bcast]
    out_specs = [pl.BlockSpec((tm, w), lambda i, k: (i, 0)) for w, _ in row_outs]
    out_specs += [pl.BlockSpec((1, w), lambda i, k: (0, 0)) for w in acc_outs]
    out_shape = [jax.ShapeDtypeStruct((M, w), dt) for w, dt in row_outs]
    out_shape += [jax.ShapeDtypeStruct((1, w), F32) for w in acc_outs]
    outs, job_res = _carry_call(
        body, name=name, grid=grid, in_specs=in_specs, out_specs=out_specs, out_shape=out_shape,
        scratch_shapes=[pltpu.VMEM((tm, n), F32)] if nk > 1 else [],
        semantics=("arbitrary", "arbitrary"), args=[a, b] + list(row_extras) + list(bcast), jobs=jobs)
    res = (outs[:n_ro], outs[n_ro:])
    return (res, job_res) if jobs else res


def _mm_nn(name, a, b, out_dtypes, tm, tn, tk, extras=(), epilogue=None, jobs=()):
    M, K = a.shape
    N = b.shape[1]
    tm, tn, tk = min(tm, M), min(tn, N), min(tk, K)
    o_spec = pl.BlockSpec((tm, tn), lambda i, j, k: (i, j))
    return _mm(name, (M // tm, N // tn, K // tk),
               a, pl.BlockSpec((tm, tk), lambda i, j, k: (i, k)),
               b, pl.BlockSpec((tk, tn), lambda i, j, k: (k, j)), NN, (tm, tn),
               [jax.ShapeDtypeStruct((M, N), dt) for dt in out_dtypes], [o_spec] * len(out_dtypes),
               [(e, o_spec) for e in extras], epilogue, jobs)


def _mm_nt(name, a, b, out_dtypes, tm, tn, tk, extras=(), epilogue=None, jobs=()):
    M, K = a.shape
    N = b.shape[0]
    tm, tn, tk = min(tm, M), min(tn, N), min(tk, K)
    o_spec = pl.BlockSpec((tm, tn), lambda i, j, k: (i, j))
    return _mm(name, (M // tm, N // tn, K // tk),
               a, pl.BlockSpec((tm, tk), lambda i, j, k: (i, k)),
               b, pl.BlockSpec((tn, tk), lambda i, j, k: (j, k)), NT, (tm, tn),
               [jax.ShapeDtypeStruct((M, N), dt) for dt in out_dtypes], [o_spec] * len(out_dtypes),
               [(e, o_spec) for e in extras], epilogue, jobs)


def _mm_tn(name, a, b, out_dtypes, tm, tn, tk, jobs=()):
    K, M = a.shape
    N = b.shape[1]
    tm, tn, tk = min(tm, M), min(tn, N), min(tk, K)
    o_spec = pl.BlockSpec((tm, tn), lambda i, j, k: (i, j))
    return _mm(name, (M // tm, N // tn, K // tk),
               a, pl.BlockSpec((tk, tm), lambda i, j, k: (k, i)),
               b, pl.BlockSpec((tk, tn), lambda i, j, k: (k, j)), TN, (tm, tn),
               [jax.ShapeDtypeStruct((M, N), dt) for dt in out_dtypes], [o_spec] * len(out_dtypes),
               epilogue=lambda acc: (acc,) * len(out_dtypes), jobs=jobs)


def _fgate_fwd(zf, bf):
    S = zf.shape[0]
    nc = S // CHUNK

    def body(zf_ref, bf_ref, f_ref):
        upper = (_iota2((CHUNK, CHUNK), 0) <= _iota2((CHUNK, CHUNK), 1)).astype(BF16)
        carry = jnp.zeros((8, 1), F32)
        for c in range(nc):
            t = zf_ref[c * CHUNK:(c + 1) * CHUNK, :] + bf_ref[...]
            lf = jnp.minimum(t, 0.0) - jnp.log(1.0 + jnp.exp(-jnp.abs(t)))
            lf_rows = lf.T[0:8, :]
            f_ref[:, c * CHUNK:(c + 1) * CHUNK] = (_dot3(lf_rows, upper) + carry) * LOG2E
            carry = carry + jnp.sum(lf_rows, axis=-1, keepdims=True)

    return pl.pallas_call(
        body, name="fgate_fwd", out_shape=jax.ShapeDtypeStruct((8, S), F32),
        compiler_params=_params(),
    )(zf, bf)


def _fgate_bwd(df, zf, bf):
    S = zf.shape[0]
    nc = S // CHUNK

    def body(df_ref, zf_ref, bf_ref, dzf_ref, dbf_ref):
        lower = (_iota2((CHUNK, CHUNK), 0) >= _iota2((CHUNK, CHUNK), 1)).astype(BF16)
        carry = jnp.zeros((8, 1), F32)
        dbf = jnp.zeros((1, LANES), F32)
        for c in reversed(range(nc)):
            sl = slice(c * CHUNK, (c + 1) * CHUNK)
            df = df_ref[:, sl]
            r = _dot3(df, lower) + carry
            carry = carry + jnp.sum(df, axis=-1, keepdims=True)
            r_cols = jnp.concatenate([r, jnp.zeros((CHUNK - 8, CHUNK), F32)], axis=0).T
            t = zf_ref[sl, :] + bf_ref[...]
            dz = r_cols * (1.0 / (1.0 + jnp.exp(t)))
            dzf_ref[sl, :] = dz.astype(BF16)
            dbf = dbf + jnp.sum(dz, axis=0, keepdims=True)
        dbf_ref[...] = dbf

    return pl.pallas_call(
        body, name="fgate_bwd",
        out_shape=[jax.ShapeDtypeStruct((S, LANES), BF16), jax.ShapeDtypeStruct((1, LANES), F32)],
        compiler_params=_params(),
    )(df, zf, bf)


_NEG = -1e30
LOG2E = 1.4426950408889634
N_SPLIT = 8
N_SPLIT_DIAG = 4
DIAG_STEP = 1024


def _attn_consts(T):
    rows, cols = _iota2((T, T), 0), _iota2((T, T), 1)
    return cols <= rows, rows <= cols


def _col_to_row(col):
    wide = jnp.broadcast_to(col, (col.shape[0], LANES))
    return jnp.concatenate([wide[r:r + LANES, :].T[0:1, :] for r in range(0, col.shape[0], LANES)], axis=1)


def _row_to_col(row):
    tall = jnp.broadcast_to(row, (LANES, row.shape[1]))
    return jnp.concatenate([tall[:, c:c + LANES].T[:, 0:1] for c in range(0, row.shape[1], LANES)], axis=0)


def _attn2_fwd(zm, f2row, T, jobs=()):
    S = zm.shape[0]
    H = f2row.shape[0]
    nb = S // T
    c2 = LOG2E / math.sqrt(HEAD_DIM)

    def body(q_ref, k_ref, v_ref, fk_ref, o_ref, lse_ref, vaug_s, fq_ref):
        i = pl.program_id(1)

        @pl.when(i == 0)
        def _():
            vaug_s[:, :HEAD_DIM] = v_ref[...]
            vaug_s[:, HEAD_DIM:] = jnp.ones((S, HEAD_DIM), BF16)

        fq_ref[...] = _row_to_col(fk_ref[i])
        keep = _attn_consts(T)[0]
        TH = T // N_SPLIT

        def block(j, diagonal, state):
            r0 = pl.multiple_of(j * T, T)
            fk = fk_ref[j]
            new = []
            for g, (m_old, acc) in enumerate(state):
                rows = slice(g * TH, (g + 1) * TH)
                nk = min(T, -(-(g + 1) * TH // DIAG_STEP) * DIAG_STEP) if diagonal else T
                s = _dot(q_ref[rows, :], k_ref[pl.ds(r0, nk), :], NT) * c2 + (fq_ref[rows, :] - fk[:, :nk])
                if diagonal:
                    s = jnp.where(keep[rows, :nk], s, _NEG)
                m_new = jnp.maximum(m_old, jnp.max(s, axis=-1, keepdims=True))
                p = jnp.exp2(s - m_new).astype(BF16)
                new.append((m_new, jnp.exp2(m_old - m_new) * acc + _dot(p, vaug_s[pl.ds(r0, nk), :])))
            return tuple(new)

        init = tuple((jnp.full((TH, 1), _NEG, F32), jnp.zeros((TH, 2 * HEAD_DIM), F32)) for _ in range(N_SPLIT))
        state = lax.fori_loop(0, i, lambda j, st: block(j, False, st), init)
        state = block(i, True, state)
        for g, (m, acc) in enumerate(state):
            rows = slice(g * TH, (g + 1) * TH)
            o_ref[rows, :] = acc[:, :HEAD_DIM] / acc[:, HEAD_DIM:]
            lse_ref[:, rows] = _col_to_row(m + jnp.log2(acc[:, HEAD_DIM:HEAD_DIM + 1]))

    nh = H
    return _carry_call(
        body, name="attn_fwd", grid=(H, nb), jobs=jobs, args=[zm, zm, zm, f2row],
        semantics=("arbitrary", "arbitrary"),
        in_specs=[
            pl.BlockSpec((T, HEAD_DIM), lambda h, i: (i, h)),
            pl.BlockSpec((S, HEAD_DIM), lambda h, i: (0, nh + h)),
            pl.BlockSpec((S, HEAD_DIM), lambda h, i: (0, 2 * nh + h)),
            pl.BlockSpec((None, nb, 1, T), lambda h, i: (h, 0, 0, 0)),
        ],
        out_specs=[pl.BlockSpec((T, HEAD_DIM), lambda h, i: (i, h)),
                   pl.BlockSpec((None, None, 1, T), lambda h, i: (h, i, 0, 0))],
        out_shape=[jax.ShapeDtypeStruct((S, H * HEAD_DIM), F32), jax.ShapeDtypeStruct((H, nb, 1, T), F32)],
        scratch_shapes=[pltpu.VMEM((S, 2 * HEAD_DIM), BF16), pltpu.VMEM((T, 1), F32)],
    )


def _attn2_bwd_dq(zm, dattn, f2row, lse2_row, delta_row, T, jobs=()):
    S = zm.shape[0]
    H = f2row.shape[0]
    nb = S // T
    scale = 1.0 / math.sqrt(HEAD_DIM)
    c2 = LOG2E * scale

    def body(q_ref, k_ref, v_ref, do_ref, fk_ref, lse_ref, dlr_ref, dq_ref, rs_ref, bias_s, do_s, dl_ref):
        i = pl.program_id(1)
        keep = _attn_consts(T)[0]
        TH = T // N_SPLIT_DIAG
        bias_s[...] = _row_to_col(fk_ref[i] - lse_ref[...])
        dl_ref[...] = _row_to_col(dlr_ref[...])
        do_s[...] = do_ref[...].astype(BF16)

        def part(rows, j, nk, state, masked):
            acc, rs = state
            r0 = pl.multiple_of(j * T, T)
            kb = k_ref[pl.ds(r0, nk), :]
            s = _dot(q_ref[rows, :], kb, NT) * c2 + (bias_s[rows, :] - fk_ref[j][:, :nk])
            if masked:
                s = jnp.where(keep[rows, :nk], s, _NEG)
            ds = jnp.exp2(s) * (_dot(do_s[rows, :], v_ref[pl.ds(r0, nk), :], NT) - dl_ref[rows, :])
            return acc + _dot(ds.astype(BF16), kb), rs + jnp.sum(ds, axis=-1, keepdims=True)

        def step(j, state):
            return part(slice(0, T), j, T, state, False)

        acc, rs = lax.fori_loop(0, i, step, (jnp.zeros((T, HEAD_DIM), F32), jnp.zeros((T, 1), F32)))
        for g in range(N_SPLIT_DIAG):
            rows = slice(g * TH, (g + 1) * TH)
            acc_g, rs_g = part(rows, i, (g + 1) * TH, (acc[rows, :], rs[rows, :]), True)
            dq_ref[rows, :] = (acc_g * scale).astype(BF16)
            rs_ref[:, rows] = _col_to_row(rs_g)

    nh = H
    row = pl.BlockSpec((None, None, 1, T), lambda h, i: (h, i, 0, 0))
    blk = pl.BlockSpec((T, HEAD_DIM), lambda h, i: (i, h))
    return _carry_call(
        body, name="attn_bwd_dq", grid=(H, nb), jobs=jobs,
        args=[zm, zm, zm, dattn, f2row, lse2_row, delta_row], semantics=("arbitrary", "arbitrary"),
        in_specs=[
            blk,
            pl.BlockSpec((S, HEAD_DIM), lambda h, i: (0, nh + h)),
            pl.BlockSpec((S, HEAD_DIM), lambda h, i: (0, 2 * nh + h)),
            blk,
            pl.BlockSpec((None, nb, 1, T), lambda h, i: (h, 0, 0, 0)),
            row, row,
        ],
        out_specs=[blk, row],
        out_shape=[jax.ShapeDtypeStruct((S, H * HEAD_DIM), BF16), jax.ShapeDtypeStruct((H, nb, 1, T), F32)],
        scratch_shapes=[pltpu.VMEM((T, 1), F32), pltpu.VMEM((T, HEAD_DIM), BF16), pltpu.VMEM((T, 1), F32)],
    )


def _attn2_bwd_dkv(zm, dattn, f2row, lse2_row, delta_row, rowsum_row, T, jobs=()):
    S = zm.shape[0]
    H = f2row.shape[0]
    nb = S // T
    scale = 1.0 / math.sqrt(HEAD_DIM)
    c2 = LOG2E * scale

    def body(q_ref, k_ref, v_ref, do_ref, fq_ref, lse_ref, dl_ref, rs_ref, dk_ref, dv_ref, df_ref, fk_ref):
        j = pl.program_id(1)
        keep = _attn_consts(T)[1]
        TH = T // N_SPLIT_DIAG
        fk_ref[...] = _row_to_col(fq_ref[j])

        def part(rows, i, c0, state, masked):
            dk, dv, df = state
            r0 = pl.multiple_of(i * T + c0, TH)
            qb = q_ref[pl.ds(r0, T - c0), :]
            do = do_ref[pl.ds(r0, T - c0), :].astype(BF16)
            bias = (fq_ref[i] - lse_ref[i])[:, c0:]
            dl = (dl_ref[i] + rs_ref[i])[:, c0:]
            st = _dot(k_ref[rows, :], qb, NT) * c2 + (bias - fk_ref[rows, :])
            if masked:
                st = jnp.where(keep[rows, c0:], st, _NEG)
            pt = jnp.exp2(st)
            dst = pt * (_dot(v_ref[rows, :], do, NT) - dl)
            return (dk + _dot(dst.astype(BF16), qb), dv + _dot(pt.astype(BF16), do),
                    df - jnp.sum(dst, axis=-1, keepdims=True))

        groups = []
        for g in range(N_SPLIT_DIAG):
            zero = (jnp.zeros((TH, HEAD_DIM), F32), jnp.zeros((TH, HEAD_DIM), F32), jnp.zeros((TH, 1), F32))
            groups.append(part(slice(g * TH, (g + 1) * TH), j, g * TH, zero, True))
        state = tuple(jnp.concatenate([grp[n] for grp in groups], axis=0) for n in range(3))
        dk, dv, df = lax.fori_loop(j + 1, nb, lambda i, st: part(slice(0, T), i, 0, st, False), state)
        dk_ref[...] = (dk * scale).astype(BF16)
        dv_ref[...] = dv.astype(BF16)
        df_ref[...] = _col_to_row(df)

    nh = H
    row = pl.BlockSpec((None, nb, 1, T), lambda h, j: (h, 0, 0, 0))
    whole = pl.BlockSpec((S, HEAD_DIM), lambda h, j: (0, h))
    kv_out = pl.BlockSpec((T, HEAD_DIM), lambda h, j: (j, h))
    return _carry_call(
        body, name="attn_bwd_dkv", grid=(H, nb), jobs=jobs,
        args=[zm, zm, zm, dattn, f2row, lse2_row, delta_row, rowsum_row],
        semantics=("arbitrary", "arbitrary"),
        in_specs=[
            whole,
            pl.BlockSpec((T, HEAD_DIM), lambda h, j: (j, nh + h)),
            pl.BlockSpec((T, HEAD_DIM), lambda h, j: (j, 2 * nh + h)),
            whole, row, row, row, row,
        ],
        out_specs=[kv_out, kv_out, pl.BlockSpec((None, None, 1, T), lambda h, j: (h, j, 0, 0))],
        out_shape=[jax.ShapeDtypeStruct((S, H * HEAD_DIM), BF16), jax.ShapeDtypeStruct((S, H * HEAD_DIM), BF16),
                   jax.ShapeDtypeStruct((H, nb, 1, T), F32)],
        scratch_shapes=[pltpu.VMEM((T, 1), F32)],
    )


def _attn_fwd(zm, fcol, frow, T, jobs=()):
    S = zm.shape[0]
    H = fcol.shape[0]
    nb = S // T
    scale = 1.0 / math.sqrt(HEAD_DIM)

    def body(q_ref, k_ref, v_ref, fq_ref, fk_ref, o_ref, lse_ref, m_s, l_s, acc_s):
        i = pl.program_id(1)
        j = pl.program_id(2)

        @pl.when(j == 0)
        def _():
            m_s[...] = jnp.full_like(m_s, _NEG)
            l_s[...] = jnp.zeros_like(l_s)
            acc_s[...] = jnp.zeros_like(acc_s)

        @pl.when(j <= i)
        def _():
            s = _dot(q_ref[...], k_ref[...], NT) * scale + (fq_ref[...] - fk_ref[...])
            keep = (_iota2((T, T), 1) + j * T) <= (_iota2((T, T), 0) + i * T)
            s = jnp.where(keep, s, _NEG)
            m_new = jnp.maximum(m_s[...], jnp.max(s, axis=-1, keepdims=True))
            alpha = jnp.exp(m_s[...] - m_new)
            p = jnp.exp(s - m_new)
            l_s[...] = alpha * l_s[...] + jnp.sum(p, axis=-1, keepdims=True)
            acc_s[...] = alpha * acc_s[...] + _dot(p.astype(BF16), v_ref[...])
            m_s[...] = m_new

        @pl.when(j == nb - 1)
        def _():
            o_ref[...] = acc_s[...] / l_s[...]
            lse_ref[...] = m_s[...] + jnp.log(l_s[...])

    nh = H
    return _carry_call(
        body, name="attn_fwd", grid=(H, nb, nb), jobs=jobs, args=[zm, zm, zm, fcol, frow],
        semantics=("parallel", "parallel", "arbitrary"),
        in_specs=[
            pl.BlockSpec((T, HEAD_DIM), lambda h, i, j: (i, h)),
            pl.BlockSpec((T, HEAD_DIM), lambda h, i, j: (jnp.minimum(j, i), nh + h)),
            pl.BlockSpec((T, HEAD_DIM), lambda h, i, j: (jnp.minimum(j, i), 2 * nh + h)),
            pl.BlockSpec((None, T, 1), lambda h, i, j: (h, i, 0)),
            pl.BlockSpec((None, 1, T), lambda h, i, j: (h, 0, jnp.minimum(j, i))),
        ],
        out_specs=[
            pl.BlockSpec((T, HEAD_DIM), lambda h, i, j: (i, h)),
            pl.BlockSpec((None, T, 1), lambda h, i, j: (h, i, 0)),
        ],
        out_shape=[jax.ShapeDtypeStruct((S, H * HEAD_DIM), F32), jax.ShapeDtypeStruct((H, S, 1), F32)],
        scratch_shapes=[pltpu.VMEM((T, 1), F32), pltpu.VMEM((T, 1), F32), pltpu.VMEM((T, HEAD_DIM), F32)],
    )


def _attn_delta(dattn, attn, tr):
    S, DA = attn.shape
    H = DA // HEAD_DIM

    def body(do_ref, o_ref, out_ref):
        lo = _iota2((DA, LANES), 1) * HEAD_DIM
        sel = ((_iota2((DA, LANES), 0) >= lo) & (_iota2((DA, LANES), 0) < lo + HEAD_DIM)).astype(BF16)
        d = _dot3(do_ref[...] * o_ref[...], sel)
        for c in range(tr // CHUNK):
            out_ref[:, c * CHUNK:(c + 1) * CHUNK] = d[c * CHUNK:(c + 1) * CHUNK, :].T[0:H, :]

    return pl.pallas_call(
        body, name="attn_delta", grid=(S // tr,),
        in_specs=[pl.BlockSpec((tr, DA), lambda i: (i, 0))] * 2,
        out_specs=pl.BlockSpec((H, tr), lambda i: (0, i)),
        out_shape=jax.ShapeDtypeStruct((H, S), F32),
        compiler_params=_params(("parallel",)),
    )(dattn, attn)


def _attn_bwd_dq(zm, dattn, fcol, frow, lse_col, delta_col, T, jobs=()):
    S = zm.shape[0]
    H = fcol.shape[0]
    nb = S // T
    scale = 1.0 / math.sqrt(HEAD_DIM)

    def body(q_ref, k_ref, v_ref, do_ref, fq_ref, fk_ref, lse_ref, dl_ref, dq_ref, rs_ref, acc_s, rs_s):
        i = pl.program_id(1)
        j = pl.program_id(2)

        @pl.when(j == 0)
        def _():
            acc_s[...] = jnp.zeros_like(acc_s)
            rs_s[...] = jnp.zeros_like(rs_s)

        @pl.when(j <= i)
        def _():
            s = _dot(q_ref[...], k_ref[...], NT) * scale + (fq_ref[...] - fk_ref[...])
            keep = (_iota2((T, T), 1) + j * T) <= (_iota2((T, T), 0) + i * T)
            p = jnp.exp(jnp.where(keep, s - lse_ref[...], _NEG))
            dp = _dot(do_ref[...].astype(BF16), v_ref[...], NT)
            ds = p * (dp - dl_ref[...])
            acc_s[...] += _dot(ds.astype(BF16), k_ref[...])
            rs_s[...] += jnp.sum(ds, axis=-1, keepdims=True)

        @pl.when(j == nb - 1)
        def _():
            dq_ref[...] = (acc_s[...] * scale).astype(BF16)
            rs_ref[...] = rs_s[...]

    nh = H
    col = pl.BlockSpec((None, T, 1), lambda h, i, j: (h, i, 0))
    return _carry_call(
        body, name="attn_bwd_dq", grid=(H, nb, nb), jobs=jobs,
        args=[zm, zm, zm, dattn, fcol, frow, lse_col, delta_col], semantics=("parallel", "parallel", "arbitrary"),
        in_specs=[
            pl.BlockSpec((T, HEAD_DIM), lambda h, i, j: (i, h)),
            pl.BlockSpec((T, HEAD_DIM), lambda h, i, j: (jnp.minimum(j, i), nh + h)),
            pl.BlockSpec((T, HEAD_DIM), lambda h, i, j: (jnp.minimum(j, i), 2 * nh + h)),
            pl.BlockSpec((T, HEAD_DIM), lambda h, i, j: (i, h)),
            col,
            pl.BlockSpec((None, 1, T), lambda h, i, j: (h, 0, jnp.minimum(j, i))),
            col, col,
        ],
        out_specs=[pl.BlockSpec((T, HEAD_DIM), lambda h, i, j: (i, h)), col],
        out_shape=[jax.ShapeDtypeStruct((S, H * HEAD_DIM), BF16), jax.ShapeDtypeStruct((H, S, 1), F32)],
        scratch_shapes=[pltpu.VMEM((T, HEAD_DIM), F32), pltpu.VMEM((T, 1), F32)],
    )


def _attn_bwd_dkv(zm, dattn, fcol, frow, lse_row, delta_row, rowsum_row, T, jobs=()):
    S = zm.shape[0]
    H = fcol.shape[0]
    nb = S // T
    scale = 1.0 / math.sqrt(HEAD_DIM)

    def body(q_ref, k_ref, v_ref, do_ref, fk_ref, fq_ref, lse_ref, dl_ref, rs_ref,
             dk_ref, dv_ref, df_ref, dk_s, dv_s, df_s):
        j = pl.program_id(1)
        i = pl.program_id(2)

        @pl.when(i == 0)
        def _():
            dk_s[...] = jnp.zeros_like(dk_s)
            dv_s[...] = jnp.zeros_like(dv_s)
            df_s[...] = jnp.zeros_like(df_s)

        @pl.when(i >= j)
        def _():
            st = _dot(k_ref[...], q_ref[...], NT) * scale + (fq_ref[...] - fk_ref[...])
            keep = (_iota2((T, T), 0) + j * T) <= (_iota2((T, T), 1) + i * T)
            pt = jnp.exp(jnp.where(keep, st - lse_ref[...], _NEG))
            do = do_ref[...].astype(BF16)
            dpt = _dot(v_ref[...], do, NT)
            dst = pt * (dpt - (dl_ref[...] + rs_ref[...]))
            dv_s[...] += _dot(pt.astype(BF16), do)
            dk_s[...] += _dot(dst.astype(BF16), q_ref[...])
            df_s[...] -= jnp.sum(dst, axis=-1, keepdims=True)

        @pl.when(i == nb - 1)
        def _():
            dk_ref[...] = (dk_s[...] * scale).astype(BF16)
            dv_ref[...] = dv_s[...].astype(BF16)
            df_ref[...] = df_s[...]

    nh = H
    row = pl.BlockSpec((None, 1, T), lambda h, j, i: (h, 0, jnp.maximum(i, j)))
    kv_out = pl.BlockSpec((T, HEAD_DIM), lambda h, j, i: (j, h))
    return _carry_call(
        body, name="attn_bwd_dkv", grid=(H, nb, nb), jobs=jobs,
        args=[zm, zm, zm, dattn, fcol, frow, lse_row, delta_row, rowsum_row],
        semantics=("parallel", "parallel", "arbitrary"),
        in_specs=[
            pl.BlockSpec((T, HEAD_DIM), lambda h, j, i: (jnp.maximum(i, j), h)),
            pl.BlockSpec((T, HEAD_DIM), lambda h, j, i: (j, nh + h)),
            pl.BlockSpec((T, HEAD_DIM), lambda h, j, i: (j, 2 * nh + h)),
            pl.BlockSpec((T, HEAD_DIM), lambda h, j, i: (jnp.maximum(i, j), h)),
            pl.BlockSpec((None, T, 1), lambda h, j, i: (h, j, 0)),
            row, row, row, row,
        ],
        out_specs=[kv_out, kv_out, pl.BlockSpec((None, T, 1), lambda h, j, i: (h, j, 0))],
        out_shape=[jax.ShapeDtypeStruct((S, H * HEAD_DIM), BF16), jax.ShapeDtypeStruct((S, H * HEAD_DIM), BF16),
                   jax.ShapeDtypeStruct((H, S, 1), F32)],
        scratch_shapes=[pltpu.VMEM((T, HEAD_DIM), F32), pltpu.VMEM((T, HEAD_DIM), F32), pltpu.VMEM((T, 1), F32)],
    )


def _ln_stats(x):
    mu = jnp.mean(x, axis=-1, keepdims=True)
    xc = x - mu
    rstd = lax.rsqrt(jnp.mean(xc * xc, axis=-1, keepdims=True) + EPS)
    return xc * rstd, rstd


def _tril_mask():
    return _iota2((CHUNK, CHUNK), 0) >= _iota2((CHUNK, CHUNK), 1)


def _gmlp_fwd(zm, ln_g, ln_b, w_s, bs_col, tr):
    S = zm.shape[0]
    H = w_s.shape[0]
    DG = H * HEAD_DIM

    def body(zu_ref, zv_ref, g_ref, b_ref, w_ref, bs_ref, out_ref):
        u = _gelu(zu_ref[...].astype(F32))
        y, _ = _ln_stats(_gelu(zv_ref[...].astype(F32)))
        v = (y * g_ref[...] + b_ref[...]).astype(BF16)
        mask = _tril_mask()
        for h in range(H):
            wc = jnp.where(mask, w_ref[h], 0.0).astype(BF16)
            cs = slice(h * HEAD_DIM, (h + 1) * HEAD_DIM)
            for c in range(tr // CHUNK):
                rs = slice(c * CHUNK, (c + 1) * CHUNK)
                mix = _dot(wc, v[rs, cs]) + bs_ref[h]
                out_ref[rs, cs] = u[rs, cs] * mix

    full = lambda a: pl.BlockSpec(a.shape, lambda i: (0,) * a.ndim)
    return pl.pallas_call(
        body, name="gmlp_fwd", grid=(S // tr,),
        in_specs=[pl.BlockSpec((tr, DG), lambda i: (i, 3)), pl.BlockSpec((tr, DG), lambda i: (i, 4)),
                  full(ln_g), full(ln_b), full(w_s), full(bs_col)],
        out_specs=pl.BlockSpec((tr, DG), lambda i: (i, 0)),
        out_shape=jax.ShapeDtypeStruct((S, DG), F32),
        compiler_params=_params(("parallel",)),
    )(zm, zm, ln_g, ln_b, w_s, bs_col)


def _gmlp_bwd(dgm, zm, ln_g, ln_b, w_s, w_st, bs_col, tr):
    S = zm.shape[0]
    H = w_s.shape[0]
    DG = H * HEAD_DIM

    def body(dg_ref, zu_ref, zv_ref, g_ref, b_ref, w_ref, wt_ref, bs_ref,
             dzu_ref, dzv_ref, dw_ref, dbs_ref, dlg_ref, dlb_ref, dv_s):
        @pl.when(pl.program_id(0) == 0)
        def _():
            dw_ref[...] = jnp.zeros_like(dw_ref)
            dbs_ref[...] = jnp.zeros_like(dbs_ref)
            dlg_ref[...] = jnp.zeros_like(dlg_ref)
            dlb_ref[...] = jnp.zeros_like(dlb_ref)

        zu = zu_ref[...].astype(F32)
        zv = zv_ref[...].astype(F32)
        u = _gelu(zu)
        y, rstd = _ln_stats(_gelu(zv))
        v = (y * g_ref[...] + b_ref[...]).astype(BF16)
        dgm_blk = dg_ref[...]
        mask = _tril_mask()
        mask_t = _iota2((CHUNK, CHUNK), 0) <= _iota2((CHUNK, CHUNK), 1)
        for h in range(H):
            wc = jnp.where(mask, w_ref[h], 0.0).astype(BF16)
            wct = jnp.where(mask_t, wt_ref[h], 0.0).astype(BF16)
            cs = slice(h * HEAD_DIM, (h + 1) * HEAD_DIM)
            dw = jnp.zeros((CHUNK, CHUNK), F32)
            dbs = jnp.zeros((CHUNK, 1), F32)
            for c in range(tr // CHUNK):
                rs = slice(c * CHUNK, (c + 1) * CHUNK)
                vch = v[rs, cs]
                mix = _dot(wc, vch) + bs_ref[h]
                dg = dgm_blk[rs, cs]
                dzu_ref[rs, cs] = (dg * mix * _gelu_grad(zu[rs, cs])).astype(BF16)
                dmix = dg * u[rs, cs]
                dbs = dbs + jnp.sum(dmix, axis=-1, keepdims=True)
                dmix_b = dmix.astype(BF16)
                dw = dw + _dot(dmix_b, vch, NT)
                dv_s[rs, cs] = _dot(wct, dmix_b)
            dw_ref[h] += jnp.where(mask, dw, 0.0)
            dbs_ref[h] += dbs
        dv = dv_s[...]
        dlg_ref[...] += jnp.sum(dv * y, axis=0, keepdims=True)
        dlb_ref[...] += jnp.sum(dv, axis=0, keepdims=True)
        dy = dv * g_ref[...]
        dgv = rstd * (dy - jnp.mean(dy, axis=-1, keepdims=True) - y * jnp.mean(dy * y, axis=-1, keepdims=True))
        dzv_ref[...] = (dgv * _gelu_grad(zv)).astype(BF16)

    full = lambda a: pl.BlockSpec(a.shape, lambda i: (0,) * a.ndim)
    rows = pl.BlockSpec((tr, DG), lambda i: (i, 0))
    return pl.pallas_call(
        body, name="gmlp_bwd", grid=(S // tr,),
        in_specs=[rows, pl.BlockSpec((tr, DG), lambda i: (i, 3)), pl.BlockSpec((tr, DG), lambda i: (i, 4)),
                  full(ln_g), full(ln_b), full(w_s), full(w_st), full(bs_col)],
        out_specs=[rows, rows, full(w_s), full(bs_col), full(ln_g), full(ln_b)],
        out_shape=[jax.ShapeDtypeStruct((S, DG), BF16), jax.ShapeDtypeStruct((S, DG), BF16),
                   jax.ShapeDtypeStruct(w_s.shape, F32), jax.ShapeDtypeStruct(bs_col.shape, F32),
                   jax.ShapeDtypeStruct(ln_g.shape, F32), jax.ShapeDtypeStruct(ln_b.shape, F32)],
        scratch_shapes=[pltpu.VMEM((tr, DG), F32)],
        compiler_params=_params(("arbitrary",)),
    )(dgm, zm, zm, ln_g, ln_b, w_s, w_st, bs_col)


def _all_gather(name, blk):
    R, C = blk.shape

    def body(x_ref, out_ref, send_sems, recv_sems, local_sem):
        x, y, c = _me()
        me, sibling = (x, y, c), (x, y, 1 - c)
        chips = [(1 - x, y), (x, 1 - y), (1 - x, 1 - y)]

        def slab(px, py, pc):
            return out_ref.at[4 * px + 2 * py + pc]

        def copy(k, block, to, src=None):
            return pltpu.make_async_remote_copy(
                src_ref=slab(*block) if src is None else src, dst_ref=slab(*block),
                send_sem=send_sems.at[k], recv_sem=recv_sems.at[k], device_id=to, device_id_type=MESH)

        mine = pltpu.make_async_copy(x_ref, slab(*me), local_sem)
        mine.start()
        first = [copy(0, me, sibling, src=x_ref)]
        first += [copy(1 + n, me, (*chip, c), src=x_ref) for n, chip in enumerate(chips)]
        for cp in first:
            cp.start()
        passed = [copy(4 + n, (*chip, c), sibling) for n, chip in enumerate(chips)]
        for n, chip in enumerate(chips):
            copy(1 + n, (*chip, c), me).wait_recv()
            passed[n].start()
        copy(0, sibling, me).wait_recv()
        for n, chip in enumerate(chips):
            copy(4 + n, (*chip, 1 - c), me).wait_recv()
        for cp in first + passed:
            cp.wait_send()
        mine.wait()

    return pl.pallas_call(
        body, name=name, out_shape=jax.ShapeDtypeStruct((N_DEV, R, C), blk.dtype),
        in_specs=[_ANY], out_specs=_ANY,
        scratch_shapes=[pltpu.SemaphoreType.DMA((7,)), pltpu.SemaphoreType.DMA((7,)), pltpu.SemaphoreType.DMA(())],
    )(blk)


def _row_tile(R, C, itemsize=4, target_bytes=2 * 1024 * 1024):
    tr = R
    while tr % 2 == 0 and tr * C * itemsize > target_bytes and (tr // 2) % 16 == 0:
        tr //= 2
    return tr


def _rs_add1(name, g4, recv, c_idx):
    _, _, R, C = g4.shape
    tr = _row_tile(R, C)

    def body(c_ref, g_ref, r_ref, hb_ref):
        hb_ref[...] = (g_ref[...] + r_ref[...].astype(F32)).astype(BF16)

    blk = pl.BlockSpec((None, tr, C), lambda p, i, c_ref: (p, i, 0))
    return pl.pallas_call(
        body, name=name,
        grid_spec=pltpu.PrefetchScalarGridSpec(
            num_scalar_prefetch=1, grid=(4, R // tr),
            in_specs=[pl.BlockSpec((None, None, tr, C), lambda p, i, c_ref: (p, c_ref[0], i, 0)), blk],
            out_specs=blk),
        out_shape=jax.ShapeDtypeStruct((4, R, C), BF16),
        compiler_params=_params(("parallel", "parallel")),
    )(c_idx, g4, recv)


def _rs_add2_own(name, g4, recv1, recv2, c_idx, p_idx):
    _, _, R, C = g4.shape
    tr = _row_tile(R, C)

    def body(c_ref, p_ref, g_ref, r1_ref, r2_ref, out_ref):
        h = g_ref[...] + r1_ref[...].astype(F32)
        out_ref[...] = ((h + r2_ref[0].astype(F32)) + r2_ref[1].astype(F32)) + r2_ref[2].astype(F32)

    return pl.pallas_call(
        body, name=name,
        grid_spec=pltpu.PrefetchScalarGridSpec(
            num_scalar_prefetch=2, grid=(R // tr,),
            in_specs=[pl.BlockSpec((None, None, tr, C), lambda i, c_ref, p_ref: (p_ref[0], c_ref[0], i, 0)),
                      pl.BlockSpec((None, tr, C), lambda i, c_ref, p_ref: (p_ref[0], i, 0)),
                      pl.BlockSpec((3, tr, C), lambda i, c_ref, p_ref: (0, i, 0))],
            out_specs=pl.BlockSpec((tr, C), lambda i, c_ref, p_ref: (i, 0))),
        out_shape=jax.ShapeDtypeStruct((R, C), F32),
        compiler_params=_params(("parallel",)),
    )(c_idx, p_idx, g4, recv1, recv2)


def _rs_add1_windows(name, g, recv, first_blocks):
    _, R, W = recv.shape
    nl = W // LANES

    def body(t_ref, *refs):
        r_ref, hb_ref = refs[nl], refs[nl + 1]
        for u in range(nl):
            cols = slice(u * LANES, (u + 1) * LANES)
            hb_ref[:, cols] = (refs[u][...] + r_ref[:, cols].astype(F32)).astype(BF16)

    blk = pl.BlockSpec((None, R, W), lambda p, t_ref: (p, 0, 0))
    return pl.pallas_call(
        body, name=name,
        grid_spec=pltpu.PrefetchScalarGridSpec(
            num_scalar_prefetch=1, grid=(4,),
            in_specs=[pl.BlockSpec((R, LANES), functools.partial(lambda u, p, t_ref: (0, t_ref[p] + u), u))
                      for u in range(nl)] + [blk],
            out_specs=blk),
        out_shape=jax.ShapeDtypeStruct((4, R, W), BF16),
        compiler_params=_params(("parallel",)),
    )(first_blocks, *([g] * nl), recv)


def _rs_add2_own_window(name, g, recv1, recv2, first_blocks, p_idx):
    _, R, W = recv1.shape
    nl = W // LANES

    def body(t_ref, p_ref, *refs):
        r1_ref, r2_ref, out_ref = refs[nl], refs[nl + 1], refs[nl + 2]
        for u in range(nl):
            cols = slice(u * LANES, (u + 1) * LANES)
            h = refs[u][...] + r1_ref[:, cols].astype(F32)
            out_ref[:, cols] = ((h + r2_ref[0, :, cols].astype(F32)) + r2_ref[1, :, cols].astype(F32)) \
                + r2_ref[2, :, cols].astype(F32)

    return pl.pallas_call(
        body, name=name,
        grid_spec=pltpu.PrefetchScalarGridSpec(
            num_scalar_prefetch=2, grid=(1,),
            in_specs=[pl.BlockSpec((R, LANES), functools.partial(lambda u, i, t, p: (0, t[p[0]] + u), u))
                      for u in range(nl)]
            + [pl.BlockSpec((None, R, W), lambda i, t, p: (p[0], 0, 0)), pl.BlockSpec((3, R, W), lambda i, t, p: (0, 0, 0))],
            out_specs=pl.BlockSpec((R, W), lambda i, t, p: (0, 0))),
        out_shape=jax.ShapeDtypeStruct((R, W), F32),
        compiler_params=_params(("arbitrary",)),
    )(first_blocks, p_idx, *([g] * nl), recv1, recv2)


def _add_windows(name, windows, first, second, n_blocks):
    _, R, W = windows.shape
    dev1 = jnp.asarray([d for d, _ in first], jnp.int32)
    blk1 = jnp.asarray([b for _, b in first], jnp.int32)
    dev2 = jnp.asarray([max(d, 0) for d, _ in second], jnp.int32)
    blk2 = jnp.asarray([b for _, b in second], jnp.int32)
    two = jnp.asarray([int(d >= 0) for d, _ in second], jnp.int32)

    G = 4
    assert n_blocks % G == 0

    def body(d1_ref, b1_ref, d2_ref, b2_ref, two_ref, *refs):
        out_ref = refs[2 * G]
        k = pl.program_id(0)
        for u in range(G):
            a_ref, b_ref = refs[u], refs[G + u]
            cols = slice(u * LANES, (u + 1) * LANES)

            @pl.when(two_ref[k * G + u] == 0)
            def _():
                out_ref[:, cols] = a_ref[...]

            @pl.when(two_ref[k * G + u] != 0)
            def _():
                out_ref[:, cols] = a_ref[...] + b_ref[...]

    def spec(u, second_owner):
        if second_owner:
            return pl.BlockSpec((None, R, LANES), lambda k, d1, b1, d2, b2, t: (d2[k * G + u], 0, b2[k * G + u]))
        return pl.BlockSpec((None, R, LANES), lambda k, d1, b1, d2, b2, t: (d1[k * G + u], 0, b1[k * G + u]))

    return pl.pallas_call(
        body, name=name,
        grid_spec=pltpu.PrefetchScalarGridSpec(
            num_scalar_prefetch=5, grid=(n_blocks // G,),
            in_specs=[spec(u, False) for u in range(G)] + [spec(u, True) for u in range(G)],
            out_specs=pl.BlockSpec((R, G * LANES), lambda k, d1, b1, d2, b2, t: (0, k))),
        out_shape=jax.ShapeDtypeStruct((R, n_blocks * LANES), windows.dtype),
        compiler_params=_params(("parallel",)),
    )(dev1, blk1, dev2, blk2, two, *([windows] * (2 * G)))


def _rs_add2(name, h, recv, p_idx):
    _, R, C = h.shape
    tr = _row_tile(R, C)

    def body(p_ref, h_ref, r_ref, out_ref):
        out_ref[...] = ((h_ref[...] + r_ref[0].astype(F32)) + r_ref[1].astype(F32)) + r_ref[2].astype(F32)

    return pl.pallas_call(
        body, name=name,
        grid_spec=pltpu.PrefetchScalarGridSpec(
            num_scalar_prefetch=1, grid=(R // tr,),
            in_specs=[pl.BlockSpec((None, tr, C), lambda i, p_ref: (p_ref[0], i, 0)),
                      pl.BlockSpec((3, tr, C), lambda i, p_ref: (0, i, 0))],
            out_specs=pl.BlockSpec((tr, C), lambda i, p_ref: (i, 0))),
        out_shape=jax.ShapeDtypeStruct((R, C), F32),
        compiler_params=_params(("parallel",)),
    )(p_idx, h, recv)


def _sum8(name, g):
    _, R, C = g.shape

    def body(g_ref, out_ref):
        acc = g_ref[0]
        for d in range(1, N_DEV):
            acc = acc + g_ref[d]
        out_ref[...] = acc

    return pl.pallas_call(body, name=name, out_shape=jax.ShapeDtypeStruct((R, C), F32),
                          compiler_params=_params())(g)


def _adamw_math(w, g, m, v):
    m = ADAM_B1 * m + (1.0 - ADAM_B1) * g
    v = ADAM_B2 * v + (1.0 - ADAM_B2) * (g * g)
    m_hat = m / (1.0 - ADAM_B1 ** ADAM_STEP)
    v_hat = v / (1.0 - ADAM_B2 ** ADAM_STEP)
    delta = -ADAM_LR * (m_hat / (jnp.sqrt(v_hat) + ADAM_EPS) + ADAM_WD * w)
    return delta, m, v


def _adamw(name, w, g, m, v):
    R, C = w.shape
    tr = _row_tile(R, C, target_bytes=1024 * 1024)
    return _row_call(name, lambda *a: (_adamw_math(*a), ()), [w, g, m, v], [], [(C, F32)] * 3, [], tr)


def _adamw_from_window(name, w, m, v, window, gate, where):
    R, C = w.shape
    W = window.shape[1]
    tr = _row_tile(R, C, target_bytes=1024 * 1024)

    def body(p_ref, w_ref, m_ref, v_ref, win_ref, gate_ref, g_out, d_out, m_out, v_out):
        off, nb, hg = p_ref[0], p_ref[1], p_ref[2]
        r, c = _iota2((W, C), 0), _iota2((W, C), 1)
        pick = jnp.logical_or(jnp.logical_and(c < nb, r == c + off),
                              jnp.logical_and(c >= nb + hg, r == c - hg + off)).astype(BF16)
        r2, c2 = _iota2((LANES, C), 0), _iota2((LANES, C), 1)
        pick_gate = jnp.logical_and(r2 < hg, c2 == nb + r2).astype(BF16)
        g = _dot3(win_ref[...], pick) + _dot3(gate_ref[...], pick_gate)
        g_out[...] = g
        d_out[...], m_out[...], v_out[...] = _adamw_math(w_ref[...], g, m_ref[...], v_ref[...])

    blk = pl.BlockSpec((tr, C), lambda i, p: (i, 0))
    return pl.pallas_call(
        body, name=name,
        grid_spec=pltpu.PrefetchScalarGridSpec(
            num_scalar_prefetch=1, grid=(R // tr,),
            in_specs=[blk, blk, blk, pl.BlockSpec((tr, W), lambda i, p: (i, 0)),
                      pl.BlockSpec((tr, LANES), lambda i, p: (i, 0))],
            out_specs=[blk] * 4),
        out_shape=[jax.ShapeDtypeStruct((R, C), F32)] * 4,
        compiler_params=_params(("parallel",)),
    )(where, w, m, v, window, gate)


def _adamw_many(name, ws, gs, ms, vs):
    n = len(ws)

    def body(*refs):
        ins, outs = refs[:4 * n], refs[4 * n:]
        for k in range(n):
            res = _adamw_math(ins[k][...], ins[n + k][...], ins[2 * n + k][...], ins[3 * n + k][...])
            for t in range(3):
                outs[t * n + k][...] = res[t]

    out = pl.pallas_call(
        body, name=name, out_shape=[jax.ShapeDtypeStruct(w.shape, F32) for _ in range(3) for w in ws],
        compiler_params=_params(),
    )(*ws, *gs, *ms, *vs)
    return out[:n], out[n:2 * n], out[2 * n:]


def _pack(parts):
    flat = []
    total = 0
    for a in parts:
        n = math.prod(a.shape)
        flat.append(a.reshape(-1).astype(F32))
        if n % LANES:
            flat.append(jnp.zeros((-n % LANES,), F32))
        total += n + (-n % LANES)
    if total % (8 * LANES):
        flat.append(jnp.zeros((-total % (8 * LANES),), F32))
    return jnp.concatenate(flat).reshape(-1, LANES)


def _unpack(packed, shapes):
    out = []
    r = 0
    for shp in shapes:
        n = math.prod(shp)
        nr = -(-n // LANES)
        out.append(packed[r:r + nr].reshape(-1)[:n].reshape(shp))
        r += nr
    return out


def kernel(x, norm_mix_g, w_in, b_f, gmlp_ln_g, gmlp_ln_b, w_s, b_s, attn_out_g, gmlp_out_g, w_out, norm_ffn_g, w_ff1, w_ff2, norm_final_g, loss_target, m_norm_mix_g, m_w_in, m_b_f, m_gmlp_ln_g, m_gmlp_ln_b, m_w_s, m_b_s, m_attn_out_g, m_gmlp_out_g, m_w_out, m_norm_ffn_g, m_w_ff1, m_w_ff2, m_norm_final_g, v_norm_mix_g, v_w_in, v_b_f, v_gmlp_ln_g, v_gmlp_ln_b, v_w_s, v_b_s, v_attn_out_g, v_gmlp_out_g, v_w_out, v_norm_ffn_g, v_w_ff1, v_w_ff2, v_norm_final_g):
    S, D = x.shape[1], x.shape[2]
    H = b_f.shape[1]
    DA = H * HEAD_DIM
    DG = gmlp_ln_g.shape[1]
    DQKV = 3 * DA
    DMAIN = DQKV + 2 * DG
    DIN = DMAIN + H
    DFF = w_ff1.shape[2] * N_DEV
    w_in_cols = w_in.shape[2]
    assert DIN == w_in_cols * N_DEV and DA == DG and D == DA + DG

    T_ATT = min(T_ATT_MAX, S)
    TR = min(TR_MAX, S)

    x0 = x[0]
    tgt = loss_target[0]
    g_final = norm_final_g.reshape(1, D)

    FB = DFF // N_DEV
    x_pos, y_pos, c_pos = _me()
    me_idx = 4 * x_pos + 2 * y_pos + c_pos

    WW = -(-(w_in_cols + LANES - 1) // LANES) * LANES
    to_main = lambda col: col if col <= DQKV else max(DQKV, col - H)
    lo = [to_main(n * w_in_cols) for n in range(N_DEV)]
    hi = [to_main((n + 1) * w_in_cols) for n in range(N_DEV)]
    starts = [v // LANES * LANES for v in lo]
    gate_dev = DQKV // w_in_cols
    n_before = DQKV - gate_dev * w_in_cols
    g0 = lo[gate_dev] - starts[gate_dev]
    stash = -(-(g0 + w_in_cols - H) // LANES) * LANES
    assert all(hi[n] <= starts[n] + WW <= DMAIN for n in range(N_DEV))
    assert gate_dev * w_in_cols <= DQKV and DQKV + H <= (gate_dev + 1) * w_in_cols and stash + LANES <= WW
    shard = w_in[0].astype(BF16)

    def my_window(n):
        if n != gate_dev:
            return lambda s: jnp.pad(s, ((0, 0), (lo[n] - starts[n], WW - w_in_cols - (lo[n] - starts[n]))))
        return lambda s: jnp.concatenate([
            jnp.zeros((D, g0), BF16), s[:, :n_before], s[:, n_before + H:],
            jnp.zeros((D, stash - g0 - (w_in_cols - H)), BF16), s[:, n_before:n_before + H],
            jnp.zeros((D, WW - stash - H), BF16)], axis=1)
    (windows_part,) = _run_jobs(
        "ag_w_in", [_job_gather_chips(lax.switch(me_idx, [my_window(n) for n in range(N_DEV)], shard))])[0]
    ((h,), _), ((windows,),) = _row_call(
        "rms_mix", lambda xb, g: ((_rms_fwd(xb, g),), ()), [x0], [norm_mix_g], [(D, BF16)], [], TR,
        jobs=[_job_gather_sibling(windows_part)])
    first, second = [], []
    for blk in range(DMAIN // LANES):
        c0 = blk * LANES
        owners = [(n, (c0 - starts[n]) // LANES) for n in range(N_DEV) if lo[n] < c0 + LANES and hi[n] > c0]
        assert 1 <= len(owners) <= 2
        first.append(owners[0])
        second.append(owners[1] if len(owners) == 2 else (-1, 0))
    w_main = _add_windows("w_in_windows", windows, first, second, DMAIN // LANES)
    w_f = windows[gate_dev, :, stash:stash + LANES]
    c_idx = jnp.reshape(c_pos, (1,)).astype(jnp.int32)
    p_idx = jnp.reshape(2 * x_pos + y_pos, (1,)).astype(jnp.int32)

    w_ff1_b = w_ff1[0].astype(BF16)
    (zm,), ((w_out_part,), (w_ff1_q1,)) = _mm_nn(
        "in_proj", h, w_main, [BF16], 2048, 1024, 2048,
        jobs=[_job_gather_chips(w_out[0].astype(BF16)), _job_gather_chips(w_ff1_b, part=(0, 1, 4))])
    (zf,) = _mm_nn("in_proj_f", h, w_f, [F32], 1024, LANES, 2048)
    bf_pad = jnp.pad(b_f, ((0, 0), (0, LANES - H)))
    f_row = _fgate_fwd(zf, bf_pad)
    NB = S // T_ATT
    f_row3 = f_row.reshape(H, NB, 1, T_ATT)
    (attn, lse_row3), ((w_out_all,), (w_ff1_part,)) = _attn2_fwd(
        zm, f_row3, T_ATT, jobs=[_job_gather_sibling(w_out_part),
                                 _job_gather_chips(w_ff1_b, part=(1, 4, 4), into=w_ff1_q1)])
    w_out_full = w_out_all.reshape(D, D)
    bs_col = b_s[0].reshape(H, CHUNK, 1)
    gm = _gmlp_fwd(zm, gmlp_ln_g, gmlp_ln_b, w_s[0], bs_col, TR)

    def merge_fn(a, g, ga, gg):
        return (jnp.concatenate([_rms_fwd(a, ga), _rms_fwd(g, gg)], axis=1),), ()
    (merged,), _ = _row_call("rms_merge", merge_fn, [attn, gm], [attn_out_g, gmlp_out_g], [(D, BF16)], [], TR)

    w_ff2_b = w_ff2[0].astype(BF16)
    TMR = min(512, S)

    def out_proj_fn(acc, res, g):
        xb = acc + res
        return (xb, _rms_fwd(xb, g)), ()
    ((x1, h2), _), ((w_ff1_all,), (w_ff2_q1,)) = _mm_rows(
        "out_proj", (S // TMR, 1), merged, pl.BlockSpec((TMR, D), lambda i, k: (i, 0)),
        w_out_full, pl.BlockSpec((D, D), lambda i, k: (0, 0)), NN, TMR, D, [x0], [norm_ffn_g],
        [(D, F32), (D, BF16)], [], out_proj_fn,
        jobs=[_job_gather_sibling(w_ff1_part), _job_gather_chips(w_ff2_b, part=(0, 1, 4))])

    tm, tn, tk = min(1024, S), min(1024, FB), min(2048, D)
    tm1 = min(2048, S)
    o_spec = pl.BlockSpec((tm1, tn), lambda i, j, k: (i, j))

    def relu_sq(acc):
        a = jnp.maximum(acc, 0.0)
        return a, a * a
    nj = FB // tn
    ff2_rest = [_job_gather_chips(w_ff2_b, part=(1, 4, 4), into=w_ff2_q1)]
    (a_act, a_sq), ((w_ff2_q2,),) = _mm(
        "ff1", (S // tm1, DFF // tn, D // tk), h2, pl.BlockSpec((tm1, tk), lambda i, j, k: (i, k)),
        w_ff1_all, pl.BlockSpec((None, tk, tn), lambda i, j, k: (j // nj, k, j % nj)), NN, (tm1, tn),
        [jax.ShapeDtypeStruct((S, DFF), BF16)] * 2, [o_spec] * 2, epilogue=relu_sq, jobs=ff2_rest)
    (w_ff2_all,) = _run_jobs("ag_w_ff2_sibling", [_job_gather_sibling(w_ff2_q2)])[0]
    w_ff2_full = w_ff2_all.reshape(DFF, D)
    def head_fn(acc, res, t, g):
        xb = acc + res
        rstd = lax.rsqrt(jnp.mean(xb * xb, axis=-1, keepdims=True) + EPS)
        xhat = xb * rstd
        err = xhat * g - t
        loss = 0.5 * jnp.sum(jnp.mean(err * err, axis=-1, keepdims=True), axis=0, keepdims=True)
        dy = err * (1.0 / D)
        dg = jnp.sum(dy * xhat, axis=0, keepdims=True)
        dxhat = dy * g
        dx = rstd * (dxhat - xhat * jnp.mean(dxhat * xhat, axis=-1, keepdims=True))
        return (dx, dx), (dg, jnp.broadcast_to(loss, (1, LANES)))
    tk_ff2 = min(1024, DFF)
    (dx2, dx2_b), (dg_final, loss_part) = _mm_rows(
        "ff2", (S // TMR, DFF // tk_ff2), a_sq, pl.BlockSpec((TMR, tk_ff2), lambda i, k: (i, k)),
        w_ff2_full, pl.BlockSpec((tk_ff2, D), lambda i, k: (k, 0)), NN, TMR, D, [x1, tgt], [g_final],
        [(D, F32), (D, BF16)], [D, LANES], head_fn)

    (da,) = _mm_nt("ff2_dx", dx2_b, w_ff2_full, [BF16], 2048, 1024, 2048, extras=[a_act],
                   epilogue=lambda acc, a: (2.0 * a.astype(F32) * acc,))
    dw_ff2, dw_ff2_b = _mm_tn("ff2_dw", a_sq, dx2_b, [F32, BF16], 1024, 2048, 1024)
    tm2, tk2 = min(2048, D), min(1024, S)
    dw1_spec = pl.BlockSpec((None, tm2, FB), lambda i, j, k: (j, i, 0))
    (dw_ff1, dw_ff1_b), ((r1_ff2,),) = _mm(
        "ff1_dw", (D // tm2, DFF // FB, S // tk2), h2, pl.BlockSpec((tk2, tm2), lambda i, j, k: (k, i)),
        da, pl.BlockSpec((tk2, FB), lambda i, j, k: (k, j)), TN, (tm2, FB),
        [jax.ShapeDtypeStruct((N_DEV, D, FB), F32), jax.ShapeDtypeStruct((N_DEV, D, FB), BF16)], [dw1_spec] * 2,
        epilogue=lambda acc: (acc, acc), jobs=[_job_scatter_sibling(dw_ff2_b.reshape(4, 2, FB, D))])
    hb_ff2 = _rs_add1("rs_add1_w_ff2", dw_ff2.reshape(4, 2, FB, D), r1_ff2, c_idx)
    def ffn_bwd_fn(dh, xb, dres, g):
        dx, dg = _rms_bwd(dh, xb, g)
        dx = dx + dres
        return (dx, dx), (dg,)
    tkb = min(1024, FB)
    nkb = FB // tkb
    ((dx1, dx1_b), (dg_ffn,)), ((r2_ff2,), (r1_ff1,)) = _mm_rows(
        "ff1_dx", (S // TMR, DFF // tkb), da, pl.BlockSpec((TMR, tkb), lambda i, k: (i, k)),
        w_ff1_all, pl.BlockSpec((None, D, tkb), lambda i, k: (k // nkb, 0, k % nkb)), NT, TMR, D, [x1, dx2],
        [norm_ffn_g], [(D, F32), (D, BF16)], [D], ffn_bwd_fn,
        jobs=[_job_scatter_chips(hb_ff2), _job_scatter_sibling(dw_ff1_b.reshape(4, 2, D, FB))])
    g_w_ff2 = _rs_add2_own("rs_add2_w_ff2", dw_ff2.reshape(4, 2, FB, D), r1_ff2, r2_ff2, c_idx, p_idx)
    hb_ff1 = _rs_add1("rs_add1_w_ff1", dw_ff1.reshape(4, 2, D, FB), r1_ff1, c_idx)

    def merge_bwd_fn(dm, a, g, ga, gg):
        da_, dga = _rms_bwd(dm[:, :DA], a, ga)
        dg_, dgg = _rms_bwd(dm[:, DA:], g, gg)
        return (da_, dg_), (dga, dgg)
    (dattn, dgm), (dg_attn, dg_gmlp) = _mm_rows(
        "out_proj_dx", (S // TMR, 1), dx1_b, pl.BlockSpec((TMR, D), lambda i, k: (i, 0)),
        w_out_full, pl.BlockSpec((D, D), lambda i, k: (0, 0)), NT, TMR, D, [attn, gm], [attn_out_g, gmlp_out_g],
        [(DA, F32), (DG, F32)], [DA, DG], merge_bwd_fn)
    dw_out, dw_out_b = _mm_tn("out_proj_dw", merged, dx1_b, [F32, BF16], 2048, 1024, 1024)

    w_st = jnp.swapaxes(w_s[0], 1, 2)
    dzu, dzv, dw_s, dbs_col, dln_g, dln_b = _gmlp_bwd(dgm, zm, gmlp_ln_g, gmlp_ln_b, w_s[0], w_st, bs_col, TR)

    delta_row3 = _attn_delta(dattn, attn, TR).reshape(H, NB, 1, T_ATT)
    (dq, ds_rowsum), ((r2_ff1_a,), (r1_out,)) = _attn2_bwd_dq(
        zm, dattn, f_row3, lse_row3, delta_row3, T_ATT,
        jobs=[_job_scatter_chips(hb_ff1, part=(0, 5, 8)),
              _job_scatter_sibling(dw_out_b.reshape(4, 2, D // N_DEV, D))])
    hb_out = _rs_add1("rs_add1_w_out", dw_out.reshape(4, 2, D // N_DEV, D), r1_out, c_idx)
    (dk, dv, df_row3), ((r2_ff1,), (r2_out,)) = _attn2_bwd_dkv(
        zm, dattn, f_row3, lse_row3, delta_row3, ds_rowsum, T_ATT,
        jobs=[_job_scatter_chips(hb_ff1, part=(5, 8, 8), into=r2_ff1_a), _job_scatter_chips(hb_out)])
    g_w_ff1 = _rs_add2_own("rs_add2_w_ff1", dw_ff1.reshape(4, 2, D, FB), r1_ff1, r2_ff1, c_idx, p_idx)
    g_w_out = _rs_add2_own("rs_add2_w_out", dw_out.reshape(4, 2, D // N_DEV, D), r1_out, r2_out, c_idx, p_idx)
    dzf, dbf = _fgate_bwd(df_row3.reshape(H, S), zf, bf_pad)

    dz_main = jnp.concatenate([dq, dk, dv, dzu, dzv], axis=1)
    dw_main, dw_main_b = _mm_tn("in_proj_dw", h, dz_main, [F32, BF16], 2048, 1024, 1024)
    (dw_f,), ((r1_in,),) = _mm_tn("in_proj_f_dw", h, dzf, [F32], 2048, LANES, 1024,
                                  jobs=[_job_scatter_sibling_windows(dw_main_b, starts, WW)])
    first_blocks = jnp.stack([jnp.where(c_pos == 0, starts[2 * p], starts[2 * p + 1]) // LANES
                              for p in range(4)]).astype(jnp.int32)
    hb_in = _rs_add1_windows("rs_add1_w_in", dw_main, r1_in, first_blocks)

    def mix_bwd_fn(dh_main, dz_gate, xb, dres, g, w_gate):
        dx, dg = _rms_bwd(dh_main + _dot(dz_gate, w_gate, NT), xb, g)
        return (dx + dres,), (dg,)
    tk_in = min(1024, DMAIN)
    ((grad_x,), (dg_mix,)), ((r2_in,),) = _mm_rows(
        "in_proj_dx", (S // TMR, DMAIN // tk_in), dz_main, pl.BlockSpec((TMR, tk_in), lambda i, k: (i, k)),
        w_main, pl.BlockSpec((D, tk_in), lambda i, k: (0, k)), NT, TMR, D, [dzf, x0, dx1], [norm_mix_g, w_f],
        [(D, F32)], [D], mix_bwd_fn, jobs=[_job_scatter_chips(hb_in)])
    g_window = _rs_add2_own_window("rs_add2_w_in", dw_main, r1_in, r2_in, first_blocks, p_idx)

    small_shapes = [norm_mix_g.shape, b_f.shape, gmlp_ln_g.shape, gmlp_ln_b.shape, w_s.shape, b_s.shape,
                    attn_out_g.shape, gmlp_out_g.shape, norm_ffn_g.shape, norm_final_g.shape]
    small_parts = [dg_mix, dbf[:, :H], dln_g, dln_b, dw_s, dbs_col, dg_attn, dg_gmlp, dg_ffn, dg_final]
    g_small = _sum8("small_sum", _all_gather("ag_small", _pack(small_parts + [dw_f[:, :H], loss_part])))
    *gs, g_gate, loss_sum = _unpack(g_small, small_shapes + [(D, H), (1, LANES)])
    two_d = lambda a: a.reshape(1, -1) if a.ndim == 1 else a
    ds, nms, nvs = _adamw_many(
        "adamw_small",
        [two_d(a) for a in (norm_mix_g, b_f, gmlp_ln_g, gmlp_ln_b, w_s, b_s, attn_out_g, gmlp_out_g, norm_ffn_g,
                            norm_final_g)],
        [two_d(a) for a in gs],
        [two_d(a) for a in (m_norm_mix_g, m_b_f, m_gmlp_ln_g, m_gmlp_ln_b, m_w_s, m_b_s, m_attn_out_g, m_gmlp_out_g,
                            m_norm_ffn_g, m_norm_final_g)],
        [two_d(a) for a in (v_norm_mix_g, v_b_f, v_gmlp_ln_g, v_gmlp_ln_b, v_w_s, v_b_s, v_attn_out_g, v_gmlp_out_g,
                            v_norm_ffn_g, v_norm_final_g)])
    ds, nms, nvs = [[a.reshape(s) for a, s in zip(lst, small_shapes)] for lst in (ds, nms, nvs)]

    is_gate_dev = me_idx == gate_dev
    where = jnp.stack([sum(jnp.where(me_idx == n, lo[n] - starts[n], 0) for n in range(N_DEV)),
                       jnp.where(is_gate_dev, n_before, w_in_cols), jnp.where(is_gate_dev, H, 0)]).astype(jnp.int32)
    big = {"w_in": tuple(a[None] for a in _adamw_from_window(
        "adamw_w_in", w_in[0], m_w_in[0], v_w_in[0], g_window, jnp.pad(g_gate, ((0, 0), (0, LANES - H))), where))}
    for nm, w, g, m, v in (("w_out", w_out, g_w_out, m_w_out, v_w_out),
                           ("w_ff1", w_ff1, g_w_ff1, m_w_ff1, v_w_ff1), ("w_ff2", w_ff2, g_w_ff2, m_w_ff2, v_w_ff2)):
        (d_, m_, v_), _ = _adamw("adamw_" + nm, w[0], g, m[0], v[0])
        big[nm] = (g[None], d_[None], m_[None], v_[None])

    loss = loss_sum[0, 0]

    def leaves(n):
        sm = (gs, ds, nms, nvs)[n]
        return [sm[0], big["w_in"][n], sm[1], sm[2], sm[3], sm[4], sm[5], sm[6], sm[7], big["w_out"][n], sm[8],
                big["w_ff1"][n], big["w_ff2"][n], sm[9]]

    return (loss, grad_x[None], *leaves(0), *leaves(1), *leaves(2), *leaves(3))
```

```python
import functools
import math

import jax
import jax.numpy as jnp
from jax import lax
from jax.experimental import pallas as pl
from jax.experimental.pallas import tpu as pltpu

F32 = jnp.float32
BF16 = jnp.bfloat16
MESH = pl.DeviceIdType.MESH

HEAD_DIM = 128
CHUNK = 128
EPS = 1e-6
LANES = 128
N_DEV = 8

ADAM_LR = 0.001
ADAM_B1 = 0.9
ADAM_B2 = 0.999
ADAM_EPS = 1e-08
ADAM_WD = 0.01
ADAM_STEP = 10

VMEM_LIMIT_BYTES = 56 * 1024 * 1024
T_ATT_MAX = 1024
TR_MAX = 512
TMR_MAX = 512

NN = ((1,), (0,))
NT = ((1,), (1,))
TN = ((0,), (0,))


def _params(sem=None):
    return pltpu.CompilerParams(dimension_semantics=sem, vmem_limit_bytes=VMEM_LIMIT_BYTES)


def _dot(a, b, contract=NN):
    return lax.dot_general(a, b, (contract, ((), ())), preferred_element_type=F32)


def _dot3(x, t):
    x1 = x.astype(BF16)
    r1 = x - x1.astype(F32)
    x2 = r1.astype(BF16)
    x3 = (r1 - x2.astype(F32)).astype(BF16)
    return _dot(x1, t) + _dot(x2, t) + _dot(x3, t)


def _iota2(shape, dim):
    return lax.broadcasted_iota(jnp.int32, shape, dim)


def _row_call(name, fn, row_ins, bcast_ins, row_outs, acc_outs, tr, jobs=()):
    S = row_ins[0].shape[0]
    assert S % tr == 0
    n_ri, n_bi, n_ro, n_ao = len(row_ins), len(bcast_ins), len(row_outs), len(acc_outs)

    def body(*refs):
        ins = [r[...] for r in refs[:n_ri + n_bi]]
        ro_refs = refs[n_ri + n_bi:n_ri + n_bi + n_ro]
        ao_refs = refs[n_ri + n_bi + n_ro:]
        ro, ao = fn(*ins)
        for r, v in zip(ro_refs, ro):
            r[...] = v.astype(r.dtype)
        if n_ao:
            @pl.when(pl.program_id(0) == 0)
            def _():
                for r in ao_refs:
                    r[...] = jnp.zeros_like(r)
            for r, v in zip(ao_refs, ao):
                r[...] += v

    in_specs = [pl.BlockSpec((tr, a.shape[1]), lambda i: (i, 0)) for a in row_ins]
    in_specs += [pl.BlockSpec(a.shape, lambda i: (0, 0)) for a in bcast_ins]
    out_specs = [pl.BlockSpec((tr, d), lambda i: (i, 0)) for d, _ in row_outs]
    out_specs += [pl.BlockSpec((1, d), lambda i: (0, 0)) for d in acc_outs]
    out_shape = [jax.ShapeDtypeStruct((S, d), dt) for d, dt in row_outs]
    out_shape += [jax.ShapeDtypeStruct((1, d), F32) for d in acc_outs]
    outs, job_res = _carry_call(
        body, name=name, grid=(S // tr,), in_specs=in_specs, out_specs=out_specs, out_shape=out_shape,
        scratch_shapes=[], semantics=("arbitrary",) if n_ao else ("parallel",), args=list(row_ins) + list(bcast_ins),
        jobs=jobs)
    res = (outs[:n_ro], outs[n_ro:])
    return (res, job_res) if jobs else res


def _rms_fwd(x, g):
    rstd = lax.rsqrt(jnp.mean(x * x, axis=-1, keepdims=True) + EPS)
    return x * rstd * g


def _rms_bwd(dy, x, g):
    rstd = lax.rsqrt(jnp.mean(x * x, axis=-1, keepdims=True) + EPS)
    xhat = x * rstd
    dg = jnp.sum(dy * xhat, axis=0, keepdims=True)
    dxhat = dy * g
    dx = rstd * (dxhat - xhat * jnp.mean(dxhat * xhat, axis=-1, keepdims=True))
    return dx, dg


_GELU_C = math.sqrt(2.0 / math.pi)


def _gelu(x):
    return 0.5 * x * (1.0 + jnp.tanh(_GELU_C * (x + 0.044715 * (x * x * x))))


def _gelu_grad(x):
    t = jnp.tanh(_GELU_C * (x + 0.044715 * (x * x * x)))
    return 0.5 * (1.0 + t) + 0.5 * x * (1.0 - t * t) * (_GELU_C * (1.0 + 3.0 * 0.044715 * (x * x)))


def _me():
    return lax.axis_index("x"), lax.axis_index("y"), lax.axis_index("c")


def _other_chips(x, y):
    return [(1 - x, y), (x, 1 - y), (1 - x, 1 - y)]


_ANY = pl.BlockSpec(memory_space=pl.ANY)


class _Job:
    def __init__(self, ins, outs, n_sems, make, aliases=None):
        self.ins, self.outs, self.n_sems, self.make, self.aliases = ins, outs, n_sems, make, aliases or {}


def _job_gather_chips(blk, part=(0, 1, 1), into=None):
    R, C = blk.shape
    nr = R // part[2]
    rows = pl.ds(part[0] * nr, (part[1] - part[0]) * nr)

    def make(ins, outs, send_sems, recv_sems, base):
        x_ref, (out_ref,) = ins[0], outs
        x, y, c = _me()
        mine = 4 * x + 2 * y + c
        targets = [(x, y, 1 - c)] + [(cx, cy, c) for cx, cy in _other_chips(x, y)]

        def copy(k, slab, to):
            return pltpu.make_async_remote_copy(
                src_ref=x_ref.at[rows, :], dst_ref=out_ref.at[slab, rows, :], send_sem=send_sems.at[base + k],
                recv_sem=recv_sems.at[base + k], device_id=to, device_id_type=MESH)

        starts = [copy(k, mine, to) for k, to in enumerate(targets)]
        arrivals = [copy(k, 4 * tx + 2 * ty + tc, (tx, ty, tc)) for k, (tx, ty, tc) in enumerate(targets)]
        local = [pltpu.make_async_copy(x_ref.at[rows, :], out_ref.at[mine, rows, :], send_sems.at[base + 4])]
        return starts, arrivals, local

    out = jax.ShapeDtypeStruct((N_DEV, R, C), blk.dtype)
    if into is None:
        return _Job([blk], [out], 5, make)
    return _Job([blk, into], [out], 5, make, aliases={1: 0})


def _job_gather_sibling(part):
    def make(ins, outs, send_sems, recv_sems, base):
        (out_ref,) = outs
        x, y, c = _me()

        def copy(k, slab):
            return pltpu.make_async_remote_copy(
                src_ref=out_ref.at[slab], dst_ref=out_ref.at[slab], send_sem=send_sems.at[base + k],
                recv_sem=recv_sems.at[base + k], device_id=(x, y, 1 - c), device_id_type=MESH)

        chips = _other_chips(x, y)
        starts = [copy(k, 4 * cx + 2 * cy + c) for k, (cx, cy) in enumerate(chips)]
        arrivals = [copy(k, 4 * cx + 2 * cy + (1 - c)) for k, (cx, cy) in enumerate(chips)]
        return starts, arrivals, []

    return _Job([part], [jax.ShapeDtypeStruct(part.shape, part.dtype)], 3, make, aliases={0: 0})


def _job_scatter_sibling(gb):
    _, _, R, C = gb.shape

    def make(ins, outs, send_sems, recv_sems, base):
        (g_ref,), (recv_ref,) = ins, outs
        x, y, c = _me()
        copies = [pltpu.make_async_remote_copy(
            src_ref=g_ref.at[p, 1 - c], dst_ref=recv_ref.at[p], send_sem=send_sems.at[base + p],
            recv_sem=recv_sems.at[base + p], device_id=(x, y, 1 - c), device_id_type=MESH) for p in range(4)]
        return copies, copies, []

    return _Job([gb], [jax.ShapeDtypeStruct((4, R, C), gb.dtype)], 4, make)


def _job_scatter_sibling_windows(gb, starts, width):
    R, _ = gb.shape

    def make(ins, outs, send_sems, recv_sems, base):
        (g_ref,), (recv_ref,) = ins, outs
        x, y, c = _me()
        copies = []
        for p in range(4):
            start = pl.multiple_of(jnp.where(c == 0, starts[2 * p + 1], starts[2 * p]), LANES)
            copies.append(pltpu.make_async_remote_copy(
                src_ref=g_ref.at[:, pl.ds(start, width)], dst_ref=recv_ref.at[p], send_sem=send_sems.at[base + p],
                recv_sem=recv_sems.at[base + p], device_id=(x, y, 1 - c), device_id_type=MESH))
        return copies, copies, []

    return _Job([gb], [jax.ShapeDtypeStruct((4, R, width), gb.dtype)], 4, make)


def _job_scatter_chips(hb, part=(0, 1, 1), into=None):
    _, R, C = hb.shape
    nr = R // part[2]
    rows = pl.ds(part[0] * nr, (part[1] - part[0]) * nr)

    def make(ins, outs, send_sems, recv_sems, base):
        h_ref, (recv_ref,) = ins[0], outs
        x, y, c = _me()
        copies = [pltpu.make_async_remote_copy(
            src_ref=h_ref.at[2 * cx + cy, rows, :], dst_ref=recv_ref.at[n, rows, :], send_sem=send_sems.at[base + n],
            recv_sem=recv_sems.at[base + n], device_id=(cx, cy, c), device_id_type=MESH)
            for n, (cx, cy) in enumerate(_other_chips(x, y))]
        return copies, copies, []

    out = jax.ShapeDtypeStruct((3, R, C), hb.dtype)
    if into is None:
        return _Job([hb], [out], 3, make)
    return _Job([hb, into], [out], 3, make, aliases={1: 0})


def _carry_call(body, *, name, grid, in_specs, out_specs, out_shape, scratch_shapes, semantics, args, jobs=(),
                io_aliases=None):
    jobs = list(jobs)
    n_in, n_out, n_scr = len(in_specs), len(out_specs), len(scratch_shapes)
    j_ins = [a for j in jobs for a in j.ins]
    j_outs = [o for j in jobs for o in j.outs]
    n_sems = sum(j.n_sems for j in jobs)
    aliases = dict(io_aliases or {})
    i0, o0 = n_in, n_out
    for j in jobs:
        for a, b in j.aliases.items():
            aliases[i0 + a] = o0 + b
        i0 += len(j.ins)
        o0 += len(j.outs)

    def full_body(*refs):
        ins = refs[:n_in]
        jin = refs[n_in:n_in + len(j_ins)]
        outs = refs[n_in + len(j_ins):n_in + len(j_ins) + n_out]
        jout = refs[n_in + len(j_ins) + n_out:n_in + len(j_ins) + n_out + len(j_outs)]
        scr = refs[n_in + len(j_ins) + n_out + len(j_outs):]
        if jobs:
            send_sems, recv_sems = scr[n_scr], scr[n_scr + 1]
            starts, arrivals, local = [], [], []
            base = i0 = o0 = 0
            for j in jobs:
                s, a, l = j.make(jin[i0:i0 + len(j.ins)], jout[o0:o0 + len(j.outs)], send_sems, recv_sems, base)
                starts += s
                arrivals += a
                local += l
                base += j.n_sems
                i0 += len(j.ins)
                o0 += len(j.outs)
            pids = [pl.program_id(d) for d in range(len(grid))]
            first = functools.reduce(jnp.logical_and, [p == 0 for p in pids])
            last = functools.reduce(jnp.logical_and, [p == n - 1 for p, n in zip(pids, grid)])

            @pl.when(first)
            def _():
                for cp in local + starts:
                    cp.start()

        body(*ins, *outs, *scr[:n_scr])

        if jobs:
            @pl.when(last)
            def _():
                for cp in arrivals:
                    cp.wait_recv()
                for cp in starts:
                    cp.wait_send()
                for cp in local:
                    cp.wait()

    sems = [pltpu.SemaphoreType.DMA((n_sems,)), pltpu.SemaphoreType.DMA((n_sems,))] if jobs else []
    res = pl.pallas_call(
        full_body, name=name, grid=grid,
        in_specs=list(in_specs) + [_ANY] * len(j_ins),
        out_specs=list(out_specs) + [_ANY] * len(j_outs),
        out_shape=list(out_shape) + j_outs,
        scratch_shapes=list(scratch_shapes) + sems,
        input_output_aliases=aliases,
        compiler_params=_params(("arbitrary",) * len(grid) if jobs else semantics),
    )(*args, *j_ins)
    body_res, job_res = res[:n_out], res[n_out:]
    per_job = []
    for j in jobs:
        per_job.append(job_res[:len(j.outs)])
        job_res = job_res[len(j.outs):]
    return body_res, per_job


def _run_jobs(name, jobs):
    def body(done_ref):
        done_ref[...] = jnp.zeros_like(done_ref)

    return _carry_call(body, name=name, grid=(1,), in_specs=[], out_specs=[pl.BlockSpec((8, LANES), lambda i: (0, 0))],
                       out_shape=[jax.ShapeDtypeStruct((8, LANES), F32)], scratch_shapes=[], semantics=("arbitrary",),
                       args=[], jobs=jobs)[1]


def _mm(name, grid, a, a_spec, b, b_spec, contract, acc_shape, out_shape, out_specs, extras=(), epilogue=None, jobs=()):
    nk = grid[2]
    n_e = len(extras)
    n_o = len(out_shape)
    if epilogue is None:
        epilogue = lambda acc: (acc,)

    def body(a_ref, b_ref, *rest):
        e_refs = rest[:n_e]
        o_refs = rest[n_e:n_e + n_o]

        def finish(total):
            res = epilogue(total, *[r[...] for r in e_refs])
            for o, r in zip(o_refs, res):
                o[...] = r.astype(o.dtype)

        if nk == 1:
            finish(_dot(a_ref[...], b_ref[...], contract))
            return
        acc = rest[n_e + n_o]
        k = pl.program_id(2)

        @pl.when(k == 0)
        def _():
            acc[...] = _dot(a_ref[...], b_ref[...], contract)

        @pl.when(jnp.logical_and(k > 0, k < nk - 1))
        def _():
            acc[...] += _dot(a_ref[...], b_ref[...], contract)

        @pl.when(k == nk - 1)
        def _():
            finish(acc[...] + _dot(a_ref[...], b_ref[...], contract))

    outs, job_res = _carry_call(
        body, name=name, grid=grid, in_specs=[a_spec, b_spec] + [s for _, s in extras],
        out_specs=list(out_specs), out_shape=list(out_shape),
        scratch_shapes=[pltpu.VMEM(acc_shape, F32)] if nk > 1 else [],
        semantics=("parallel", "parallel", "arbitrary"), args=[a, b] + [e for e, _ in extras], jobs=jobs)
    return (outs, job_res) if jobs else outs


def _mm_rows(name, grid, a, a_spec, b, b_spec, contract, tm, n, row_extras, bcast, row_outs, acc_outs, epilogue, jobs=(),
             block0=0, into=None):
    nk = grid[1]
    M = row_extras[0].shape[0]
    n_x, n_b, n_ro, n_ao = len(row_extras), len(bcast), len(row_outs), len(acc_outs)
    n_into = len(into) if into else 0

    def body(a_ref, b_ref, *rest):
        x_refs = rest[:n_x + n_b]
        rest = rest[n_x + n_b + n_into:]
        ro_refs = rest[:n_ro]
        ao_refs = rest[n_ro:n_ro + n_ao]
        i = pl.program_id(0)

        def finish(total):
            ro, ao = epilogue(total, *[r[...] for r in x_refs])
            for r, v in zip(ro_refs, ro):
                r[...] = v.astype(r.dtype)
            if n_ao:
                @pl.when(i == 0)
                def _():
                    for r, v in zip(ao_refs, ao):
                        r[...] = v

                @pl.when(i > 0)
                def _():
                    for r, v in zip(ao_refs, ao):
                        r[...] += v

        if nk == 1:
            finish(_dot(a_ref[...], b_ref[...], contract))
            return
        acc = rest[n_ro + n_ao]
        k = pl.program_id(1)

        @pl.when(k == 0)
        def _():
            acc[...] = _dot(a_ref[...], b_ref[...], contract)

        @pl.when(jnp.logical_and(k > 0, k < nk - 1))
        def _():
            acc[...] += _dot(a_ref[...], b_ref[...], contract)

        @pl.when(k == nk - 1)
        def _():
            finish(acc[...] + _dot(a_ref[...], b_ref[...], contract))

    in_specs = [a_spec, b_spec] + [pl.BlockSpec((tm, x.shape[1]), lambda i, k: (i + block0, 0)) for x in row_extras]
    in_specs += [pl.BlockSpec(x.shape, lambda i, k: (0,) * x.ndim) for x in bcast]
    in_specs += [_ANY] * n_into
    out_specs = [pl.BlockSpec((tm, w), lambda i, k: (i + block0, 0)) for w, _ in row_outs]
    out_specs += [pl.BlockSpec((1, w), lambda i, k: (0, 0)) for w in acc_outs]
    out_shape = [jax.ShapeDtypeStruct((M, w), dt) for w, dt in row_outs]
    out_shape += [jax.ShapeDtypeStruct((1, w), F32) for w in acc_outs]
    outs, job_res = _carry_call(
        body, name=name, grid=grid, in_specs=in_specs, out_specs=out_specs, out_shape=out_shape,
        scratch_shapes=[pltpu.VMEM((tm, n), F32)] if nk > 1 else [],
        semantics=("arbitrary", "arbitrary"), args=[a, b] + list(row_extras) + list(bcast) + list(into or []),
        jobs=jobs, io_aliases={2 + n_x + n_b + t: t for t in range(n_into)})
    res = (outs[:n_ro], outs[n_ro:])
    return (res, job_res) if jobs else res


def _mm_nn(name, a, b, out_dtypes, tm, tn, tk, extras=(), epilogue=None, jobs=()):
    M, K = a.shape
    N = b.shape[1]
    tm, tn, tk = min(tm, M), min(tn, N), min(tk, K)
    o_spec = pl.BlockSpec((tm, tn), lambda i, j, k: (i, j))
    return _mm(name, (M // tm, N // tn, K // tk),
               a, pl.BlockSpec((tm, tk), lambda i, j, k: (i, k)),
               b, pl.BlockSpec((tk, tn), lambda i, j, k: (k, j)), NN, (tm, tn),
               [jax.ShapeDtypeStruct((M, N), dt) for dt in out_dtypes], [o_spec] * len(out_dtypes),
               [(e, o_spec) for e in extras], epilogue, jobs)


def _mm_nt(name, a, b, out_dtypes, tm, tn, tk, extras=(), epilogue=None, jobs=()):
    M, K = a.shape
    N = b.shape[0]
    tm, tn, tk = min(tm, M), min(tn, N), min(tk, K)
    o_spec = pl.BlockSpec((tm, tn), lambda i, j, k: (i, j))
    return _mm(name, (M // tm, N // tn, K // tk),
               a, pl.BlockSpec((tm, tk), lambda i, j, k: (i, k)),
               b, pl.BlockSpec((tn, tk), lambda i, j, k: (j, k)), NT, (tm, tn),
               [jax.ShapeDtypeStruct((M, N), dt) for dt in out_dtypes], [o_spec] * len(out_dtypes),
               [(e, o_spec) for e in extras], epilogue, jobs)


def _mm_tn(name, a, b, out_dtypes, tm, tn, tk, jobs=()):
    K, M = a.shape
    N = b.shape[1]
    tm, tn, tk = min(tm, M), min(tn, N), min(tk, K)
    o_spec = pl.BlockSpec((tm, tn), lambda i, j, k: (i, j))
    return _mm(name, (M // tm, N // tn, K // tk),
               a, pl.BlockSpec((tk, tm), lambda i, j, k: (k, i)),
               b, pl.BlockSpec((tk, tn), lambda i, j, k: (k, j)), TN, (tm, tn),
               [jax.ShapeDtypeStruct((M, N), dt) for dt in out_dtypes], [o_spec] * len(out_dtypes),
               epilogue=lambda acc: (acc,) * len(out_dtypes), jobs=jobs)


def _fgate_fwd(zf, bf):
    S = zf.shape[0]
    nc = S // CHUNK

    def body(zf_ref, bf_ref, f_ref):
        upper = (_iota2((CHUNK, CHUNK), 0) <= _iota2((CHUNK, CHUNK), 1)).astype(BF16)
        carry = jnp.zeros((8, 1), F32)
        for c in range(nc):
            t = zf_ref[c * CHUNK:(c + 1) * CHUNK, :] + bf_ref[...]
            lf = jnp.minimum(t, 0.0) - jnp.log(1.0 + jnp.exp(-jnp.abs(t)))
            lf_rows = lf.T[0:8, :]
            f_ref[:, c * CHUNK:(c + 1) * CHUNK] = (_dot3(lf_rows, upper) + carry) * LOG2E
            carry = carry + jnp.sum(lf_rows, axis=-1, keepdims=True)

    return pl.pallas_call(
        body, name="fgate_fwd", out_shape=jax.ShapeDtypeStruct((8, S), F32),
        compiler_params=_params(),
    )(zf, bf)


def _fgate_bwd(df, zf, bf):
    S = zf.shape[0]
    nc = S // CHUNK

    def body(df_ref, zf_ref, bf_ref, dzf_ref, dbf_ref):
        lower = (_iota2((CHUNK, CHUNK), 0) >= _iota2((CHUNK, CHUNK), 1)).astype(BF16)
        carry = jnp.zeros((8, 1), F32)
        dbf = jnp.zeros((1, LANES), F32)
        for c in reversed(range(nc)):
            sl = slice(c * CHUNK, (c + 1) * CHUNK)
            df = df_ref[:, sl]
            r = _dot3(df, lower) + carry
            carry = carry + jnp.sum(df, axis=-1, keepdims=True)
            r_cols = jnp.concatenate([r, jnp.zeros((CHUNK - 8, CHUNK), F32)], axis=0).T
            t = zf_ref[sl, :] + bf_ref[...]
            dz = r_cols * (1.0 / (1.0 + jnp.exp(t)))
            dzf_ref[sl, :] = dz.astype(BF16)
            dbf = dbf + jnp.sum(dz, axis=0, keepdims=True)
        dbf_ref[...] = dbf

    return pl.pallas_call(
        body, name="fgate_bwd",
        out_shape=[jax.ShapeDtypeStruct((S, LANES), BF16), jax.ShapeDtypeStruct((1, LANES), F32)],
        compiler_params=_params(),
    )(df, zf, bf)


_NEG = -1e30
LOG2E = 1.4426950408889634
N_SPLIT = 8
N_SPLIT_DIAG = 4
DIAG_STEP = 1024


def _attn_consts(T):
    rows, cols = _iota2((T, T), 0), _iota2((T, T), 1)
    return cols <= rows, rows <= cols


def _col_to_row(col):
    wide = jnp.broadcast_to(col, (col.shape[0], LANES))
    return jnp.concatenate([wide[r:r + LANES, :].T[0:1, :] for r in range(0, col.shape[0], LANES)], axis=1)


def _row_to_col(row):
    tall = jnp.broadcast_to(row, (LANES, row.shape[1]))
    return jnp.concatenate([tall[:, c:c + LANES].T[:, 0:1] for c in range(0, row.shape[1], LANES)], axis=0)


def _attn2_fwd(zm, f2row, T, jobs=()):
    S = zm.shape[0]
    H = f2row.shape[0]
    nb = S // T
    c2 = LOG2E / math.sqrt(HEAD_DIM)

    def body(q_ref, k_ref, v_ref, fk_ref, o_ref, lse_ref, vaug_s, fq_ref):
        i = pl.program_id(1)

        @pl.when(i == 0)
        def _():
            vaug_s[:, :HEAD_DIM] = v_ref[...]
            vaug_s[:, HEAD_DIM:] = jnp.ones((S, HEAD_DIM), BF16)

        fq_ref[...] = _row_to_col(fk_ref[i])
        keep = _attn_consts(T)[0]
        TH = T // N_SPLIT

        def block(j, diagonal, state):
            r0 = pl.multiple_of(j * T, T)
            fk = fk_ref[j]
            new = []
            for g, (m_old, acc) in enumerate(state):
                rows = slice(g * TH, (g + 1) * TH)
                nk = min(T, -(-(g + 1) * TH // DIAG_STEP) * DIAG_STEP) if diagonal else T
                s = _dot(q_ref[rows, :], k_ref[pl.ds(r0, nk), :], NT) * c2 + (fq_ref[rows, :] - fk[:, :nk])
                if diagonal:
                    s = jnp.where(keep[rows, :nk], s, _NEG)
                m_new = jnp.maximum(m_old, jnp.max(s, axis=-1, keepdims=True))
                p = jnp.exp2(s - m_new).astype(BF16)
                new.append((m_new, jnp.exp2(m_old - m_new) * acc + _dot(p, vaug_s[pl.ds(r0, nk), :])))
            return tuple(new)

        init = tuple((jnp.full((TH, 1), _NEG, F32), jnp.zeros((TH, 2 * HEAD_DIM), F32)) for _ in range(N_SPLIT))
        state = lax.fori_loop(0, i, lambda j, st: block(j, False, st), init)
        state = block(i, True, state)
        for g, (m, acc) in enumerate(state):
            rows = slice(g * TH, (g + 1) * TH)
            o_ref[rows, :] = acc[:, :HEAD_DIM] / acc[:, HEAD_DIM:]
            lse_ref[:, rows] = _col_to_row(m + jnp.log2(acc[:, HEAD_DIM:HEAD_DIM + 1]))

    nh = H
    return _carry_call(
        body, name="attn_fwd", grid=(H, nb), jobs=jobs, args=[zm, zm, zm, f2row],
        semantics=("arbitrary", "arbitrary"),
        in_specs=[
            pl.BlockSpec((T, HEAD_DIM), lambda h, i: (i, h)),
            pl.BlockSpec((S, HEAD_DIM), lambda h, i: (0, nh + h)),
            pl.BlockSpec((S, HEAD_DIM), lambda h, i: (0, 2 * nh + h)),
            pl.BlockSpec((None, nb, 1, T), lambda h, i: (h, 0, 0, 0)),
        ],
        out_specs=[pl.BlockSpec((T, HEAD_DIM), lambda h, i: (i, h)),
                   pl.BlockSpec((None, None, 1, T), lambda h, i: (h, i, 0, 0))],
        out_shape=[jax.ShapeDtypeStruct((S, H * HEAD_DIM), F32), jax.ShapeDtypeStruct((H, nb, 1, T), F32)],
        scratch_shapes=[pltpu.VMEM((S, 2 * HEAD_DIM), BF16), pltpu.VMEM((T, 1), F32)],
    )


def _attn2_bwd_dq(zm, dattn, f2row, lse2_row, delta_row, T, jobs=()):
    S = zm.shape[0]
    H = f2row.shape[0]
    nb = S // T
    scale = 1.0 / math.sqrt(HEAD_DIM)
    c2 = LOG2E * scale

    def body(q_ref, k_ref, v_ref, do_ref, fk_ref, lse_ref, dlr_ref, dq_ref, rs_ref, bias_s, do_s, dl_ref):
        i = pl.program_id(1)
        keep = _attn_consts(T)[0]
        TH = T // N_SPLIT_DIAG
        bias_s[...] = _row_to_col(fk_ref[i] - lse_ref[...])
        dl_ref[...] = _row_to_col(dlr_ref[...])
        do_s[...] = do_ref[...].astype(BF16)

        def part(rows, j, nk, state, masked):
            acc, rs = state
            r0 = pl.multiple_of(j * T, T)
            kb = k_ref[pl.ds(r0, nk), :]
            s = _dot(q_ref[rows, :], kb, NT) * c2 + (bias_s[rows, :] - fk_ref[j][:, :nk])
            if masked:
                s = jnp.where(keep[rows, :nk], s, _NEG)
            ds = jnp.exp2(s) * (_dot(do_s[rows, :], v_ref[pl.ds(r0, nk), :], NT) - dl_ref[rows, :])
            return acc + _dot(ds.astype(BF16), kb), rs + jnp.sum(ds, axis=-1, keepdims=True)

        def step(j, state):
            return part(slice(0, T), j, T, state, False)

        acc, rs = lax.fori_loop(0, i, step, (jnp.zeros((T, HEAD_DIM), F32), jnp.zeros((T, 1), F32)))
        for g in range(N_SPLIT_DIAG):
            rows = slice(g * TH, (g + 1) * TH)
            acc_g, rs_g = part(rows, i, (g + 1) * TH, (acc[rows, :], rs[rows, :]), True)
            dq_ref[rows, :] = (acc_g * scale).astype(BF16)
            rs_ref[:, rows] = _col_to_row(rs_g)

    nh = H
    row = pl.BlockSpec((None, None, 1, T), lambda h, i: (h, i, 0, 0))
    blk = pl.BlockSpec((T, HEAD_DIM), lambda h, i: (i, h))
    return _carry_call(
        body, name="attn_bwd_dq", grid=(H, nb), jobs=jobs,
        args=[zm, zm, zm, dattn, f2row, lse2_row, delta_row], semantics=("arbitrary", "arbitrary"),
        in_specs=[
            blk,
            pl.BlockSpec((S, HEAD_DIM), lambda h, i: (0, nh + h)),
            pl.BlockSpec((S, HEAD_DIM), lambda h, i: (0, 2 * nh + h)),
            blk,
            pl.BlockSpec((None, nb, 1, T), lambda h, i: (h, 0, 0, 0)),
            row, row,
        ],
        out_specs=[blk, row],
        out_shape=[jax.ShapeDtypeStruct((S, H * HEAD_DIM), BF16), jax.ShapeDtypeStruct((H, nb, 1, T), F32)],
        scratch_shapes=[pltpu.VMEM((T, 1), F32), pltpu.VMEM((T, HEAD_DIM), BF16), pltpu.VMEM((T, 1), F32)],
    )


def _attn2_bwd_dkv(zm, dattn, f2row, lse2_row, delta_row, rowsum_row, T, jobs=()):
    S = zm.shape[0]
    H = f2row.shape[0]
    nb = S // T
    scale = 1.0 / math.sqrt(HEAD_DIM)
    c2 = LOG2E * scale

    def body(q_ref, k_ref, v_ref, do_ref, fq_ref, lse_ref, dl_ref, rs_ref, dk_ref, dv_ref, df_ref, fk_ref):
        j = pl.program_id(1)
        keep = _attn_consts(T)[1]
        TH = T // N_SPLIT_DIAG
        fk_ref[...] = _row_to_col(fq_ref[j])

        def part(rows, i, c0, state, masked):
            dk, dv, df = state
            r0 = pl.multiple_of(i * T + c0, TH)
            qb = q_ref[pl.ds(r0, T - c0), :]
            do = do_ref[pl.ds(r0, T - c0), :].astype(BF16)
            bias = (fq_ref[i] - lse_ref[i])[:, c0:]
            dl = (dl_ref[i] + rs_ref[i])[:, c0:]
            st = _dot(k_ref[rows, :], qb, NT) * c2 + (bias - fk_ref[rows, :])
            if masked:
                st = jnp.where(keep[rows, c0:], st, _NEG)
            pt = jnp.exp2(st)
            dst = pt * (_dot(v_ref[rows, :], do, NT) - dl)
            return (dk + _dot(dst.astype(BF16), qb), dv + _dot(pt.astype(BF16), do),
                    df - jnp.sum(dst, axis=-1, keepdims=True))

        groups = []
        for g in range(N_SPLIT_DIAG):
            zero = (jnp.zeros((TH, HEAD_DIM), F32), jnp.zeros((TH, HEAD_DIM), F32), jnp.zeros((TH, 1), F32))
            groups.append(part(slice(g * TH, (g + 1) * TH), j, g * TH, zero, True))
        state = tuple(jnp.concatenate([grp[n] for grp in groups], axis=0) for n in range(3))
        dk, dv, df = lax.fori_loop(j + 1, nb, lambda i, st: part(slice(0, T), i, 0, st, False), state)
        dk_ref[...] = (dk * scale).astype(BF16)
        dv_ref[...] = dv.astype(BF16)
        df_ref[...] = _col_to_row(df)

    nh = H
    row = pl.BlockSpec((None, nb, 1, T), lambda h, j: (h, 0, 0, 0))
    whole = pl.BlockSpec((S, HEAD_DIM), lambda h, j: (0, h))
    kv_out = pl.BlockSpec((T, HEAD_DIM), lambda h, j: (j, h))
    return _carry_call(
        body, name="attn_bwd_dkv", grid=(H, nb), jobs=jobs,
        args=[zm, zm, zm, dattn, f2row, lse2_row, delta_row, rowsum_row],
        semantics=("arbitrary", "arbitrary"),
        in_specs=[
            whole,
            pl.BlockSpec((T, HEAD_DIM), lambda h, j: (j, nh + h)),
            pl.BlockSpec((T, HEAD_DIM), lambda h, j: (j, 2 * nh + h)),
            whole, row, row, row, row,
        ],
        out_specs=[kv_out, kv_out, pl.BlockSpec((None, None, 1, T), lambda h, j: (h, j, 0, 0))],
        out_shape=[jax.ShapeDtypeStruct((S, H * HEAD_DIM), BF16), jax.ShapeDtypeStruct((S, H * HEAD_DIM), BF16),
                   jax.ShapeDtypeStruct((H, nb, 1, T), F32)],
        scratch_shapes=[pltpu.VMEM((T, 1), F32)],
    )


def _attn_delta(dattn, attn, tr):
    S, DA = attn.shape
    H = DA // HEAD_DIM

    def body(do_ref, o_ref, out_ref):
        lo = _iota2((DA, LANES), 1) * HEAD_DIM
        sel = ((_iota2((DA, LANES), 0) >= lo) & (_iota2((DA, LANES), 0) < lo + HEAD_DIM)).astype(BF16)
        d = _dot3(do_ref[...] * o_ref[...], sel)
        for c in range(tr // CHUNK):
            out_ref[:, c * CHUNK:(c + 1) * CHUNK] = d[c * CHUNK:(c + 1) * CHUNK, :].T[0:H, :]

    return pl.pallas_call(
        body, name="attn_delta", grid=(S // tr,),
        in_specs=[pl.BlockSpec((tr, DA), lambda i: (i, 0))] * 2,
        out_specs=pl.BlockSpec((H, tr), lambda i: (0, i)),
        out_shape=jax.ShapeDtypeStruct((H, S), F32),
        compiler_params=_params(("parallel",)),
    )(dattn, attn)


def _ln_stats(x):
    mu = jnp.mean(x, axis=-1, keepdims=True)
    xc = x - mu
    rstd = lax.rsqrt(jnp.mean(xc * xc, axis=-1, keepdims=True) + EPS)
    return xc * rstd, rstd


def _tril_mask():
    return _iota2((CHUNK, CHUNK), 0) >= _iota2((CHUNK, CHUNK), 1)


def _gmlp_fwd(zm, ln_g, ln_b, w_s, bs_col, tr):
    S = zm.shape[0]
    H = w_s.shape[0]
    DG = H * HEAD_DIM

    def body(zu_ref, zv_ref, g_ref, b_ref, w_ref, bs_ref, out_ref):
        u = _gelu(zu_ref[...].astype(F32))
        y, _ = _ln_stats(_gelu(zv_ref[...].astype(F32)))
        v = (y * g_ref[...] + b_ref[...]).astype(BF16)
        mask = _tril_mask()
        for h in range(H):
            wc = jnp.where(mask, w_ref[h], 0.0).astype(BF16)
            cs = slice(h * HEAD_DIM, (h + 1) * HEAD_DIM)
            for c in range(tr // CHUNK):
                rs = slice(c * CHUNK, (c + 1) * CHUNK)
                mix = _dot(wc, v[rs, cs]) + bs_ref[h]
                out_ref[rs, cs] = u[rs, cs] * mix

    full = lambda a: pl.BlockSpec(a.shape, lambda i: (0,) * a.ndim)
    return pl.pallas_call(
        body, name="gmlp_fwd", grid=(S // tr,),
        in_specs=[pl.BlockSpec((tr, DG), lambda i: (i, 3)), pl.BlockSpec((tr, DG), lambda i: (i, 4)),
                  full(ln_g), full(ln_b), full(w_s), full(bs_col)],
        out_specs=pl.BlockSpec((tr, DG), lambda i: (i, 0)),
        out_shape=jax.ShapeDtypeStruct((S, DG), F32),
        compiler_params=_params(("parallel",)),
    )(zm, zm, ln_g, ln_b, w_s, bs_col)


def _gmlp_bwd(dgm, zm, ln_g, ln_b, w_s, w_st, bs_col, tr):
    S = zm.shape[0]
    H = w_s.shape[0]
    DG = H * HEAD_DIM

    def body(dg_ref, zu_ref, zv_ref, g_ref, b_ref, w_ref, wt_ref, bs_ref,
             dzu_ref, dzv_ref, dw_ref, dbs_ref, dlg_ref, dlb_ref, dv_s):
        @pl.when(pl.program_id(0) == 0)
        def _():
            dw_ref[...] = jnp.zeros_like(dw_ref)
            dbs_ref[...] = jnp.zeros_like(dbs_ref)
            dlg_ref[...] = jnp.zeros_like(dlg_ref)
            dlb_ref[...] = jnp.zeros_like(dlb_ref)

        zu = zu_ref[...].astype(F32)
        zv = zv_ref[...].astype(F32)
        u = _gelu(zu)
        y, rstd = _ln_stats(_gelu(zv))
        v = (y * g_ref[...] + b_ref[...]).astype(BF16)
        dgm_blk = dg_ref[...]
        mask = _tril_mask()
        mask_t = _iota2((CHUNK, CHUNK), 0) <= _iota2((CHUNK, CHUNK), 1)
        for h in range(H):
            wc = jnp.where(mask, w_ref[h], 0.0).astype(BF16)
            wct = jnp.where(mask_t, wt_ref[h], 0.0).astype(BF16)
            cs = slice(h * HEAD_DIM, (h + 1) * HEAD_DIM)
            dw = jnp.zeros((CHUNK, CHUNK), F32)
            dbs = jnp.zeros((CHUNK, 1), F32)
            for c in range(tr // CHUNK):
                rs = slice(c * CHUNK, (c + 1) * CHUNK)
                vch = v[rs, cs]
                mix = _dot(wc, vch) + bs_ref[h]
                dg = dgm_blk[rs, cs]
                dzu_ref[rs, cs] = (dg * mix * _gelu_grad(zu[rs, cs])).astype(BF16)
                dmix = dg * u[rs, cs]
                dbs = dbs + jnp.sum(dmix, axis=-1, keepdims=True)
                dmix_b = dmix.astype(BF16)
                dw = dw + _dot(dmix_b, vch, NT)
                dv_s[rs, cs] = _dot(wct, dmix_b)
            dw_ref[h] += jnp.where(mask, dw, 0.0)
            dbs_ref[h] += dbs
        dv = dv_s[...]
        dlg_ref[...] += jnp.sum(dv * y, axis=0, keepdims=True)
        dlb_ref[...] += jnp.sum(dv, axis=0, keepdims=True)
        dy = dv * g_ref[...]
        dgv = rstd * (dy - jnp.mean(dy, axis=-1, keepdims=True) - y * jnp.mean(dy * y, axis=-1, keepdims=True))
        dzv_ref[...] = (dgv * _gelu_grad(zv)).astype(BF16)

    full = lambda a: pl.BlockSpec(a.shape, lambda i: (0,) * a.ndim)
    rows = pl.BlockSpec((tr, DG), lambda i: (i, 0))
    return pl.pallas_call(
        body, name="gmlp_bwd", grid=(S // tr,),
        in_specs=[rows, pl.BlockSpec((tr, DG), lambda i: (i, 3)), pl.BlockSpec((tr, DG), lambda i: (i, 4)),
                  full(ln_g), full(ln_b), full(w_s), full(w_st), full(bs_col)],
        out_specs=[rows, rows, full(w_s), full(bs_col), full(ln_g), full(ln_b)],
        out_shape=[jax.ShapeDtypeStruct((S, DG), BF16), jax.ShapeDtypeStruct((S, DG), BF16),
                   jax.ShapeDtypeStruct(w_s.shape, F32), jax.ShapeDtypeStruct(bs_col.shape, F32),
                   jax.ShapeDtypeStruct(ln_g.shape, F32), jax.ShapeDtypeStruct(ln_b.shape, F32)],
        scratch_shapes=[pltpu.VMEM((tr, DG), F32)],
        compiler_params=_params(("arbitrary",)),
    )(dgm, zm, zm, ln_g, ln_b, w_s, w_st, bs_col)


def _all_gather(name, blk):
    R, C = blk.shape

    def body(x_ref, out_ref, send_sems, recv_sems, local_sem):
        x, y, c = _me()
        me, sibling = (x, y, c), (x, y, 1 - c)
        chips = [(1 - x, y), (x, 1 - y), (1 - x, 1 - y)]

        def slab(px, py, pc):
            return out_ref.at[4 * px + 2 * py + pc]

        def copy(k, block, to, src=None):
            return pltpu.make_async_remote_copy(
                src_ref=slab(*block) if src is None else src, dst_ref=slab(*block),
                send_sem=send_sems.at[k], recv_sem=recv_sems.at[k], device_id=to, device_id_type=MESH)

        mine = pltpu.make_async_copy(x_ref, slab(*me), local_sem)
        mine.start()
        first = [copy(0, me, sibling, src=x_ref)]
        first += [copy(1 + n, me, (*chip, c), src=x_ref) for n, chip in enumerate(chips)]
        for cp in first:
            cp.start()
        passed = [copy(4 + n, (*chip, c), sibling) for n, chip in enumerate(chips)]
        for n, chip in enumerate(chips):
            copy(1 + n, (*chip, c), me).wait_recv()
            passed[n].start()
        copy(0, sibling, me).wait_recv()
        for n, chip in enumerate(chips):
            copy(4 + n, (*chip, 1 - c), me).wait_recv()
        for cp in first + passed:
            cp.wait_send()
        mine.wait()

    return pl.pallas_call(
        body, name=name, out_shape=jax.ShapeDtypeStruct((N_DEV, R, C), blk.dtype),
        in_specs=[_ANY], out_specs=_ANY,
        scratch_shapes=[pltpu.SemaphoreType.DMA((7,)), pltpu.SemaphoreType.DMA((7,)), pltpu.SemaphoreType.DMA(())],
    )(blk)


def _row_tile(R, C, itemsize=4, target_bytes=2 * 1024 * 1024):
    tr = R
    while tr % 2 == 0 and tr * C * itemsize > target_bytes and (tr // 2) % 16 == 0:
        tr //= 2
    return tr


def _rs_add1(name, g4, recv, c_idx):
    _, _, R, C = g4.shape
    tr = _row_tile(R, C)

    def body(c_ref, g_ref, r_ref, hb_ref):
        hb_ref[...] = (g_ref[...] + r_ref[...].astype(F32)).astype(BF16)

    blk = pl.BlockSpec((None, tr, C), lambda p, i, c_ref: (p, i, 0))
    return pl.pallas_call(
        body, name=name,
        grid_spec=pltpu.PrefetchScalarGridSpec(
            num_scalar_prefetch=1, grid=(4, R // tr),
            in_specs=[pl.BlockSpec((None, None, tr, C), lambda p, i, c_ref: (p, c_ref[0], i, 0)), blk],
            out_specs=blk),
        out_shape=jax.ShapeDtypeStruct((4, R, C), BF16),
        compiler_params=_params(("parallel", "parallel")),
    )(c_idx, g4, recv)


def _rs_add2_own(name, g4, recv1, recv2, c_idx, p_idx):
    _, _, R, C = g4.shape
    tr = _row_tile(R, C)

    def body(c_ref, p_ref, g_ref, r1_ref, r2_ref, out_ref):
        h = g_ref[...] + r1_ref[...].astype(F32)
        out_ref[...] = ((h + r2_ref[0].astype(F32)) + r2_ref[1].astype(F32)) + r2_ref[2].astype(F32)

    return pl.pallas_call(
        body, name=name,
        grid_spec=pltpu.PrefetchScalarGridSpec(
            num_scalar_prefetch=2, grid=(R // tr,),
            in_specs=[pl.BlockSpec((None, None, tr, C), lambda i, c_ref, p_ref: (p_ref[0], c_ref[0], i, 0)),
                      pl.BlockSpec((None, tr, C), lambda i, c_ref, p_ref: (p_ref[0], i, 0)),
                      pl.BlockSpec((3, tr, C), lambda i, c_ref, p_ref: (0, i, 0))],
            out_specs=pl.BlockSpec((tr, C), lambda i, c_ref, p_ref: (i, 0))),
        out_shape=jax.ShapeDtypeStruct((R, C), F32),
        compiler_params=_params(("parallel",)),
    )(c_idx, p_idx, g4, recv1, recv2)


def _rs_add1_windows(name, g, recv, first_blocks):
    _, R, W = recv.shape
    nl = W // LANES

    def body(t_ref, *refs):
        r_ref, hb_ref = refs[nl], refs[nl + 1]
        for u in range(nl):
            cols = slice(u * LANES, (u + 1) * LANES)
            hb_ref[:, cols] = (refs[u][...] + r_ref[:, cols].astype(F32)).astype(BF16)

    blk = pl.BlockSpec((None, R, W), lambda p, t_ref: (p, 0, 0))
    return pl.pallas_call(
        body, name=name,
        grid_spec=pltpu.PrefetchScalarGridSpec(
            num_scalar_prefetch=1, grid=(4,),
            in_specs=[pl.BlockSpec((R, LANES), functools.partial(lambda u, p, t_ref: (0, t_ref[p] + u), u))
                      for u in range(nl)] + [blk],
            out_specs=blk),
        out_shape=jax.ShapeDtypeStruct((4, R, W), BF16),
        compiler_params=_params(("parallel",)),
    )(first_blocks, *([g] * nl), recv)


def _rs_add2_own_window(name, g, recv1, recv2, first_blocks, p_idx):
    _, R, W = recv1.shape
    nl = W // LANES

    def body(t_ref, p_ref, *refs):
        r1_ref, r2_ref, out_ref = refs[nl], refs[nl + 1], refs[nl + 2]
        for u in range(nl):
            cols = slice(u * LANES, (u + 1) * LANES)
            h = refs[u][...] + r1_ref[:, cols].astype(F32)
            out_ref[:, cols] = ((h + r2_ref[0, :, cols].astype(F32)) + r2_ref[1, :, cols].astype(F32)) \
                + r2_ref[2, :, cols].astype(F32)

    return pl.pallas_call(
        body, name=name,
        grid_spec=pltpu.PrefetchScalarGridSpec(
            num_scalar_prefetch=2, grid=(1,),
            in_specs=[pl.BlockSpec((R, LANES), functools.partial(lambda u, i, t, p: (0, t[p[0]] + u), u))
                      for u in range(nl)]
            + [pl.BlockSpec((None, R, W), lambda i, t, p: (p[0], 0, 0)), pl.BlockSpec((3, R, W), lambda i, t, p: (0, 0, 0))],
            out_specs=pl.BlockSpec((R, W), lambda i, t, p: (0, 0))),
        out_shape=jax.ShapeDtypeStruct((R, W), F32),
        compiler_params=_params(("arbitrary",)),
    )(first_blocks, p_idx, *([g] * nl), recv1, recv2)


def _add_windows(name, windows, first, second, n_blocks):
    _, R, W = windows.shape
    dev1 = jnp.asarray([d for d, _ in first], jnp.int32)
    blk1 = jnp.asarray([b for _, b in first], jnp.int32)
    dev2 = jnp.asarray([max(d, 0) for d, _ in second], jnp.int32)
    blk2 = jnp.asarray([b for _, b in second], jnp.int32)
    two = jnp.asarray([int(d >= 0) for d, _ in second], jnp.int32)

    G = 4
    assert n_blocks % G == 0

    def body(d1_ref, b1_ref, d2_ref, b2_ref, two_ref, *refs):
        out_ref = refs[2 * G]
        k = pl.program_id(0)
        for u in range(G):
            a_ref, b_ref = refs[u], refs[G + u]
            cols = slice(u * LANES, (u + 1) * LANES)

            @pl.when(two_ref[k * G + u] == 0)
            def _():
                out_ref[:, cols] = a_ref[...]

            @pl.when(two_ref[k * G + u] != 0)
            def _():
                out_ref[:, cols] = a_ref[...] + b_ref[...]

    def spec(u, second_owner):
        if second_owner:
            return pl.BlockSpec((None, R, LANES), lambda k, d1, b1, d2, b2, t: (d2[k * G + u], 0, b2[k * G + u]))
        return pl.BlockSpec((None, R, LANES), lambda k, d1, b1, d2, b2, t: (d1[k * G + u], 0, b1[k * G + u]))

    return pl.pallas_call(
        body, name=name,
        grid_spec=pltpu.PrefetchScalarGridSpec(
            num_scalar_prefetch=5, grid=(n_blocks // G,),
            in_specs=[spec(u, False) for u in range(G)] + [spec(u, True) for u in range(G)],
            out_specs=pl.BlockSpec((R, G * LANES), lambda k, d1, b1, d2, b2, t: (0, k))),
        out_shape=jax.ShapeDtypeStruct((R, n_blocks * LANES), windows.dtype),
        compiler_params=_params(("parallel",)),
    )(dev1, blk1, dev2, blk2, two, *([windows] * (2 * G)))


def _sum8(name, g):
    _, R, C = g.shape

    def body(g_ref, out_ref):
        acc = g_ref[0]
        for d in range(1, N_DEV):
            acc = acc + g_ref[d]
        out_ref[...] = acc

    return pl.pallas_call(body, name=name, out_shape=jax.ShapeDtypeStruct((R, C), F32),
                          compiler_params=_params())(g)


def _adamw_math(w, g, m, v):
    m = ADAM_B1 * m + (1.0 - ADAM_B1) * g
    v = ADAM_B2 * v + (1.0 - ADAM_B2) * (g * g)
    m_hat = m / (1.0 - ADAM_B1 ** ADAM_STEP)
    v_hat = v / (1.0 - ADAM_B2 ** ADAM_STEP)
    delta = -ADAM_LR * (m_hat / (jnp.sqrt(v_hat) + ADAM_EPS) + ADAM_WD * w)
    return delta, m, v


def _adamw(name, w, g, m, v):
    R, C = w.shape
    tr = _row_tile(R, C, target_bytes=1024 * 1024)
    return _row_call(name, lambda *a: (_adamw_math(*a), ()), [w, g, m, v], [], [(C, F32)] * 3, [], tr)


def _adamw_from_window(name, w, m, v, window, gate, where):
    R, C = w.shape
    W = window.shape[1]
    tr = _row_tile(R, C, target_bytes=1024 * 1024)

    def body(p_ref, w_ref, m_ref, v_ref, win_ref, gate_ref, g_out, d_out, m_out, v_out):
        off, nb, hg = p_ref[0], p_ref[1], p_ref[2]
        r, c = _iota2((W, C), 0), _iota2((W, C), 1)
        pick = jnp.logical_or(jnp.logical_and(c < nb, r == c + off),
                              jnp.logical_and(c >= nb + hg, r == c - hg + off)).astype(BF16)
        r2, c2 = _iota2((LANES, C), 0), _iota2((LANES, C), 1)
        pick_gate = jnp.logical_and(r2 < hg, c2 == nb + r2).astype(BF16)
        g = _dot3(win_ref[...], pick) + _dot3(gate_ref[...], pick_gate)
        g_out[...] = g
        d_out[...], m_out[...], v_out[...] = _adamw_math(w_ref[...], g, m_ref[...], v_ref[...])

    blk = pl.BlockSpec((tr, C), lambda i, p: (i, 0))
    return pl.pallas_call(
        body, name=name,
        grid_spec=pltpu.PrefetchScalarGridSpec(
            num_scalar_prefetch=1, grid=(R // tr,),
            in_specs=[blk, blk, blk, pl.BlockSpec((tr, W), lambda i, p: (i, 0)),
                      pl.BlockSpec((tr, LANES), lambda i, p: (i, 0))],
            out_specs=[blk] * 4),
        out_shape=[jax.ShapeDtypeStruct((R, C), F32)] * 4,
        compiler_params=_params(("parallel",)),
    )(where, w, m, v, window, gate)


def _adamw_many(name, ws, gs, ms, vs):
    n = len(ws)

    def body(*refs):
        ins, outs = refs[:4 * n], refs[4 * n:]
        for k in range(n):
            res = _adamw_math(ins[k][...], ins[n + k][...], ins[2 * n + k][...], ins[3 * n + k][...])
            for t in range(3):
                outs[t * n + k][...] = res[t]

    out = pl.pallas_call(
        body, name=name, out_shape=[jax.ShapeDtypeStruct(w.shape, F32) for _ in range(3) for w in ws],
        compiler_params=_params(),
    )(*ws, *gs, *ms, *vs)
    return out[:n], out[n:2 * n], out[2 * n:]


def _pack(parts):
    flat = []
    total = 0
    for a in parts:
        n = math.prod(a.shape)
        flat.append(a.reshape(-1).astype(F32))
        if n % LANES:
            flat.append(jnp.zeros((-n % LANES,), F32))
        total += n + (-n % LANES)
    if total % (8 * LANES):
        flat.append(jnp.zeros((-total % (8 * LANES),), F32))
    return jnp.concatenate(flat).reshape(-1, LANES)


def _unpack(packed, shapes):
    out = []
    r = 0
    for shp in shapes:
        n = math.prod(shp)
        nr = -(-n // LANES)
        out.append(packed[r:r + nr].reshape(-1)[:n].reshape(shp))
        r += nr
    return out


def kernel(x, norm_mix_g, w_in, b_f, gmlp_ln_g, gmlp_ln_b, w_s, b_s, attn_out_g, gmlp_out_g, w_out, norm_ffn_g, w_ff1, w_ff2, norm_final_g, loss_target, m_norm_mix_g, m_w_in, m_b_f, m_gmlp_ln_g, m_gmlp_ln_b, m_w_s, m_b_s, m_attn_out_g, m_gmlp_out_g, m_w_out, m_norm_ffn_g, m_w_ff1, m_w_ff2, m_norm_final_g, v_norm_mix_g, v_w_in, v_b_f, v_gmlp_ln_g, v_gmlp_ln_b, v_w_s, v_b_s, v_attn_out_g, v_gmlp_out_g, v_w_out, v_norm_ffn_g, v_w_ff1, v_w_ff2, v_norm_final_g):
    S, D = x.shape[1], x.shape[2]
    H = b_f.shape[1]
    DA = H * HEAD_DIM
    DG = gmlp_ln_g.shape[1]
    DQKV = 3 * DA
    DMAIN = DQKV + 2 * DG
    DIN = DMAIN + H
    DFF = w_ff1.shape[2] * N_DEV
    w_in_cols = w_in.shape[2]
    assert DIN == w_in_cols * N_DEV and DA == DG and D == DA + DG

    T_ATT = min(T_ATT_MAX, S)
    TR = min(TR_MAX, S)

    x0 = x[0]
    tgt = loss_target[0]
    g_final = norm_final_g.reshape(1, D)

    FB = DFF // N_DEV
    x_pos, y_pos, c_pos = _me()
    me_idx = 4 * x_pos + 2 * y_pos + c_pos

    WW = -(-(w_in_cols + LANES - 1) // LANES) * LANES
    to_main = lambda col: col if col <= DQKV else max(DQKV, col - H)
    lo = [to_main(n * w_in_cols) for n in range(N_DEV)]
    hi = [to_main((n + 1) * w_in_cols) for n in range(N_DEV)]
    starts = [v // LANES * LANES for v in lo]
    gate_dev = DQKV // w_in_cols
    n_before = DQKV - gate_dev * w_in_cols
    g0 = lo[gate_dev] - starts[gate_dev]
    stash = -(-(g0 + w_in_cols - H) // LANES) * LANES
    assert all(hi[n] <= starts[n] + WW <= DMAIN for n in range(N_DEV))
    assert gate_dev * w_in_cols <= DQKV and DQKV + H <= (gate_dev + 1) * w_in_cols and stash + LANES <= WW
    shard = w_in[0].astype(BF16)

    def my_window(n):
        if n != gate_dev:
            return lambda s: jnp.pad(s, ((0, 0), (lo[n] - starts[n], WW - w_in_cols - (lo[n] - starts[n]))))
        return lambda s: jnp.concatenate([
            jnp.zeros((D, g0), BF16), s[:, :n_before], s[:, n_before + H:],
            jnp.zeros((D, stash - g0 - (w_in_cols - H)), BF16), s[:, n_before:n_before + H],
            jnp.zeros((D, WW - stash - H), BF16)], axis=1)
    (windows_part,) = _run_jobs(
        "ag_w_in", [_job_gather_chips(lax.switch(me_idx, [my_window(n) for n in range(N_DEV)], shard))])[0]
    ((h,), _), ((windows,),) = _row_call(
        "rms_mix", lambda xb, g: ((_rms_fwd(xb, g),), ()), [x0], [norm_mix_g], [(D, BF16)], [], TR,
        jobs=[_job_gather_sibling(windows_part)])
    first, second = [], []
    for blk in range(DMAIN // LANES):
        c0 = blk * LANES
        owners = [(n, (c0 - starts[n]) // LANES) for n in range(N_DEV) if lo[n] < c0 + LANES and hi[n] > c0]
        assert 1 <= len(owners) <= 2
        first.append(owners[0])
        second.append(owners[1] if len(owners) == 2 else (-1, 0))
    w_main = _add_windows("w_in_windows", windows, first, second, DMAIN // LANES)
    w_f = windows[gate_dev, :, stash:stash + LANES]
    c_idx = jnp.reshape(c_pos, (1,)).astype(jnp.int32)
    p_idx = jnp.reshape(2 * x_pos + y_pos, (1,)).astype(jnp.int32)

    w_ff1_b = w_ff1[0].astype(BF16)
    (zm,), ((w_out_part,), (w_ff1_q1,)) = _mm_nn(
        "in_proj", h, w_main, [BF16], 2048, 1024, 2048,
        jobs=[_job_gather_chips(w_out[0].astype(BF16)), _job_gather_chips(w_ff1_b, part=(0, 1, 4))])
    (zf,) = _mm_nn("in_proj_f", h, w_f, [F32], 1024, LANES, 2048)
    bf_pad = jnp.pad(b_f, ((0, 0), (0, LANES - H)))
    f_row = _fgate_fwd(zf, bf_pad)
    NB = S // T_ATT
    f_row3 = f_row.reshape(H, NB, 1, T_ATT)
    (attn, lse_row3), ((w_out_all,), (w_ff1_part,)) = _attn2_fwd(
        zm, f_row3, T_ATT, jobs=[_job_gather_sibling(w_out_part),
                                 _job_gather_chips(w_ff1_b, part=(1, 4, 4), into=w_ff1_q1)])
    w_out_full = w_out_all.reshape(D, D)
    bs_col = b_s[0].reshape(H, CHUNK, 1)
    gm = _gmlp_fwd(zm, gmlp_ln_g, gmlp_ln_b, w_s[0], bs_col, TR)

    def merge_fn(a, g, ga, gg):
        return (jnp.concatenate([_rms_fwd(a, ga), _rms_fwd(g, gg)], axis=1),), ()
    (merged,), _ = _row_call("rms_merge", merge_fn, [attn, gm], [attn_out_g, gmlp_out_g], [(D, BF16)], [], TR)

    w_ff2_b = w_ff2[0].astype(BF16)
    TMR = min(TMR_MAX, S)

    def out_proj_fn(acc, res, g):
        xb = acc + res
        return (xb, _rms_fwd(xb, g)), ()
    ((x1, h2), _), ((w_ff1_all,), (w_ff2_q1,)) = _mm_rows(
        "out_proj", (S // TMR, 1), merged, pl.BlockSpec((TMR, D), lambda i, k: (i, 0)),
        w_out_full, pl.BlockSpec((D, D), lambda i, k: (0, 0)), NN, TMR, D, [x0], [norm_ffn_g],
        [(D, F32), (D, BF16)], [], out_proj_fn,
        jobs=[_job_gather_sibling(w_ff1_part), _job_gather_chips(w_ff2_b, part=(0, 1, 4))])

    tm, tn, tk = min(1024, S), min(1024, FB), min(2048, D)
    tm1 = min(2048, S)
    o_spec = pl.BlockSpec((tm1, tn), lambda i, j, k: (i, j))

    def relu_sq(acc):
        a = jnp.maximum(acc, 0.0)
        return a, a * a
    nj = FB // tn
    ff2_rest = [_job_gather_chips(w_ff2_b, part=(1, 4, 4), into=w_ff2_q1)]
    (a_act, a_sq), ((w_ff2_q2,),) = _mm(
        "ff1", (S // tm1, DFF // tn, D // tk), h2, pl.BlockSpec((tm1, tk), lambda i, j, k: (i, k)),
        w_ff1_all, pl.BlockSpec((None, tk, tn), lambda i, j, k: (j // nj, k, j % nj)), NN, (tm1, tn),
        [jax.ShapeDtypeStruct((S, DFF), BF16)] * 2, [o_spec] * 2, epilogue=relu_sq, jobs=ff2_rest)
    (w_ff2_all,) = _run_jobs("ag_w_ff2_sibling", [_job_gather_sibling(w_ff2_q2)])[0]
    w_ff2_full = w_ff2_all.reshape(DFF, D)
    def head_fn(acc, res, t, g):
        xb = acc + res
        rstd = lax.rsqrt(jnp.mean(xb * xb, axis=-1, keepdims=True) + EPS)
        xhat = xb * rstd
        err = xhat * g - t
        loss = 0.5 * jnp.sum(jnp.mean(err * err, axis=-1, keepdims=True), axis=0, keepdims=True)
        dy = err * (1.0 / D)
        dg = jnp.sum(dy * xhat, axis=0, keepdims=True)
        dxhat = dy * g
        dx = rstd * (dxhat - xhat * jnp.mean(dxhat * xhat, axis=-1, keepdims=True))
        return (dx, dx), (dg, jnp.broadcast_to(loss, (1, LANES)))
    tk_ff2 = min(1024, DFF)
    (dx2, dx2_b), (dg_final, loss_part) = _mm_rows(
        "ff2", (S // TMR, DFF // tk_ff2), a_sq, pl.BlockSpec((TMR, tk_ff2), lambda i, k: (i, k)),
        w_ff2_full, pl.BlockSpec((tk_ff2, D), lambda i, k: (k, 0)), NN, TMR, D, [x1, tgt], [g_final],
        [(D, F32), (D, BF16)], [D, LANES], head_fn)

    (da,) = _mm_nt("ff2_dx", dx2_b, w_ff2_full, [BF16], 2048, 1024, 2048, extras=[a_act],
                   epilogue=lambda acc, a: (2.0 * a.astype(F32) * acc,))
    dw_ff2, dw_ff2_b = _mm_tn("ff2_dw", a_sq, dx2_b, [F32, BF16], 1024, 2048, 1024)
    tm2, tk2 = min(2048, D), min(1024, S)
    dw1_spec = pl.BlockSpec((None, tm2, FB), lambda i, j, k: (j, i, 0))
    (dw_ff1, dw_ff1_b), ((r1_ff2,),) = _mm(
        "ff1_dw", (D // tm2, DFF // FB, S // tk2), h2, pl.BlockSpec((tk2, tm2), lambda i, j, k: (k, i)),
        da, pl.BlockSpec((tk2, FB), lambda i, j, k: (k, j)), TN, (tm2, FB),
        [jax.ShapeDtypeStruct((N_DEV, D, FB), F32), jax.ShapeDtypeStruct((N_DEV, D, FB), BF16)], [dw1_spec] * 2,
        epilogue=lambda acc: (acc, acc), jobs=[_job_scatter_sibling(dw_ff2_b.reshape(4, 2, FB, D))])
    hb_ff2 = _rs_add1("rs_add1_w_ff2", dw_ff2.reshape(4, 2, FB, D), r1_ff2, c_idx)
    def ffn_bwd_fn(dh, xb, dres, g):
        dx, dg = _rms_bwd(dh, xb, g)
        dx = dx + dres
        return (dx, dx), (dg,)
    tkb = min(1024, FB)
    nkb = FB // tkb
    ((dx1, dx1_b), (dg_ffn,)), ((r2_ff2,), (r1_ff1,)) = _mm_rows(
        "ff1_dx", (S // TMR, DFF // tkb), da, pl.BlockSpec((TMR, tkb), lambda i, k: (i, k)),
        w_ff1_all, pl.BlockSpec((None, D, tkb), lambda i, k: (k // nkb, 0, k % nkb)), NT, TMR, D, [x1, dx2],
        [norm_ffn_g], [(D, F32), (D, BF16)], [D], ffn_bwd_fn,
        jobs=[_job_scatter_chips(hb_ff2), _job_scatter_sibling(dw_ff1_b.reshape(4, 2, D, FB))])
    g_w_ff2 = _rs_add2_own("rs_add2_w_ff2", dw_ff2.reshape(4, 2, FB, D), r1_ff2, r2_ff2, c_idx, p_idx)
    hb_ff1 = _rs_add1("rs_add1_w_ff1", dw_ff1.reshape(4, 2, D, FB), r1_ff1, c_idx)

    def merge_bwd_fn(dm, a, g, ga, gg):
        da_, dga = _rms_bwd(dm[:, :DA], a, ga)
        dg_, dgg = _rms_bwd(dm[:, DA:], g, gg)
        return (da_, dg_), (dga, dgg)
    (dattn, dgm), (dg_attn, dg_gmlp) = _mm_rows(
        "out_proj_dx", (S // TMR, 1), dx1_b, pl.BlockSpec((TMR, D), lambda i, k: (i, 0)),
        w_out_full, pl.BlockSpec((D, D), lambda i, k: (0, 0)), NT, TMR, D, [attn, gm], [attn_out_g, gmlp_out_g],
        [(DA, F32), (DG, F32)], [DA, DG], merge_bwd_fn)
    dw_out, dw_out_b = _mm_tn("out_proj_dw", merged, dx1_b, [F32, BF16], 2048, 1024, 1024)

    w_st = jnp.swapaxes(w_s[0], 1, 2)
    dzu, dzv, dw_s, dbs_col, dln_g, dln_b = _gmlp_bwd(dgm, zm, gmlp_ln_g, gmlp_ln_b, w_s[0], w_st, bs_col, TR)

    delta_row3 = _attn_delta(dattn, attn, TR).reshape(H, NB, 1, T_ATT)
    (dq, ds_rowsum), ((r2_ff1_a,), (r1_out,)) = _attn2_bwd_dq(
        zm, dattn, f_row3, lse_row3, delta_row3, T_ATT,
        jobs=[_job_scatter_chips(hb_ff1, part=(0, 5, 8)),
              _job_scatter_sibling(dw_out_b.reshape(4, 2, D // N_DEV, D))])
    hb_out = _rs_add1("rs_add1_w_out", dw_out.reshape(4, 2, D // N_DEV, D), r1_out, c_idx)
    (dk, dv, df_row3), ((r2_ff1,), (r2_out,)) = _attn2_bwd_dkv(
        zm, dattn, f_row3, lse_row3, delta_row3, ds_rowsum, T_ATT,
        jobs=[_job_scatter_chips(hb_ff1, part=(5, 8, 8), into=r2_ff1_a), _job_scatter_chips(hb_out)])
    g_w_ff1 = _rs_add2_own("rs_add2_w_ff1", dw_ff1.reshape(4, 2, D, FB), r1_ff1, r2_ff1, c_idx, p_idx)
    g_w_out = _rs_add2_own("rs_add2_w_out", dw_out.reshape(4, 2, D // N_DEV, D), r1_out, r2_out, c_idx, p_idx)
    dzf, dbf = _fgate_bwd(df_row3.reshape(H, S), zf, bf_pad)

    dz_main = jnp.concatenate([dq, dk, dv, dzu, dzv], axis=1)
    dw_main, dw_main_b = _mm_tn("in_proj_dw", h, dz_main, [F32, BF16], 2048, 1024, 1024)
    (dw_f,), ((r1_in,),) = _mm_tn("in_proj_f_dw", h, dzf, [F32], 2048, LANES, 1024,
                                  jobs=[_job_scatter_sibling_windows(dw_main_b, starts, WW)])
    first_blocks = jnp.stack([jnp.where(c_pos == 0, starts[2 * p], starts[2 * p + 1]) // LANES
                              for p in range(4)]).astype(jnp.int32)
    hb_in = _rs_add1_windows("rs_add1_w_in", dw_main, r1_in, first_blocks)

    def mix_bwd_fn(dh_main, dz_gate, xb, dres, g, w_gate):
        dx, dg = _rms_bwd(dh_main + _dot(dz_gate, w_gate, NT), xb, g)
        return (dx + dres,), (dg,)
    tk_in = min(1024, DMAIN)
    n_rb = S // TMR

    def in_proj_dx(name, blocks, block0, more, fn, **kw):
        return _mm_rows(
            name, (blocks, DMAIN // tk_in), dz_main, pl.BlockSpec((TMR, tk_in), lambda i, k: (i + block0, k)),
            w_main, pl.BlockSpec((D, tk_in), lambda i, k: (0, k)), NT, TMR, D, [dzf, x0, dx1],
            [norm_mix_g, w_f] + more, [(D, F32)], [D], fn, block0=block0, **kw)

    def mix_bwd_last_fn(dh_main, dz_gate, xb, dres, g, w_gate, dg_so_far):
        (dx,), (dg,) = mix_bwd_fn(dh_main, dz_gate, xb, dres, g, w_gate)
        return (dx,), (dg_so_far + dg,)
    ((gx_part,), (dg_mix_part,)), ((r2_in,),) = in_proj_dx(
        "in_proj_dx", n_rb - 1, 0, [], mix_bwd_fn, jobs=[_job_scatter_chips(hb_in)])
    (grad_x,), (dg_mix,) = in_proj_dx("in_proj_dx_last", 1, n_rb - 1, [dg_mix_part], mix_bwd_last_fn, into=[gx_part])
    g_window = _rs_add2_own_window("rs_add2_w_in", dw_main, r1_in, r2_in, first_blocks, p_idx)

    small_shapes = [norm_mix_g.shape, b_f.shape, gmlp_ln_g.shape, gmlp_ln_b.shape, w_s.shape, b_s.shape,
                    attn_out_g.shape, gmlp_out_g.shape, norm_ffn_g.shape, norm_final_g.shape]
    small_parts = [dg_mix, dbf[:, :H], dln_g, dln_b, dw_s, dbs_col, dg_attn, dg_gmlp, dg_ffn, dg_final]
    g_small = _sum8("small_sum", _all_gather("ag_small", _pack(small_parts + [dw_f[:, :H], loss_part])))
    *gs, g_gate, loss_sum = _unpack(g_small, small_shapes + [(D, H), (1, LANES)])
    two_d = lambda a: a.reshape(1, -1) if a.ndim == 1 else a
    ds, nms, nvs = _adamw_many(
        "adamw_small",
        [two_d(a) for a in (norm_mix_g, b_f, gmlp_ln_g, gmlp_ln_b, w_s, b_s, attn_out_g, gmlp_out_g, norm_ffn_g,
                            norm_final_g)],
        [two_d(a) for a in gs],
        [two_d(a) for a in (m_norm_mix_g, m_b_f, m_gmlp_ln_g, m_gmlp_ln_b, m_w_s, m_b_s, m_attn_out_g, m_gmlp_out_g,
                            m_norm_ffn_g, m_norm_final_g)],
        [two_d(a) for a in (v_norm_mix_g, v_b_f, v_gmlp_ln_g, v_gmlp_ln_b, v_w_s, v_b_s, v_attn_out_g, v_gmlp_out_g,
                            v_norm_ffn_g, v_norm_final_g)])
    ds, nms, nvs = [[a.reshape(s) for a, s in zip(lst, small_shapes)] for lst in (ds, nms, nvs)]

    is_gate_dev = me_idx == gate_dev
    where = jnp.stack([sum(jnp.where(me_idx == n, lo[n] - starts[n], 0) for n in range(N_DEV)),
                       jnp.where(is_gate_dev, n_before, w_in_cols), jnp.where(is_gate_dev, H, 0)]).astype(jnp.int32)
    big = {"w_in": tuple(a[None] for a in _adamw_from_window(
        "adamw_w_in", w_in[0], m_w_in[0], v_w_in[0], g_window, jnp.pad(g_gate, ((0, 0), (0, LANES - H))), where))}
    for nm, w, g, m, v in (("w_out", w_out, g_w_out, m_w_out, v_w_out),
                           ("w_ff1", w_ff1, g_w_ff1, m_w_ff1, v_w_ff1), ("w_ff2", w_ff2, g_w_ff2, m_w_ff2, v_w_ff2)):
        (d_, m_, v_), _ = _adamw("adamw_" + nm, w[0], g, m[0], v[0])
        big[nm] = (g[None], d_[None], m_[None], v_[None])

    loss = loss_sum[0, 0]

    def leaves(n):
        sm = (gs, ds, nms, nvs)[n]
        return [sm[0], big["w_in"][n], sm[1], sm[2], sm[3], sm[4], sm[5], sm[6], sm[7], big["w_out"][n], sm[8],
                big["w_ff1"][n], big["w_ff2"][n], sm[9]]

    return (loss, grad_x[None], *leaves(0), *leaves(1), *leaves(2), *leaves(3))
```

```python
import functools
import math

import jax
import jax.numpy as jnp
from jax import lax
from jax.experimental import pallas as pl
from jax.experimental.pallas import tpu as pltpu

F32 = jnp.float32
BF16 = jnp.bfloat16
MESH = pl.DeviceIdType.MESH

HEAD_DIM = 128
CHUNK = 128
EPS = 1e-6
LANES = 128
N_DEV = 8

ADAM_LR = 0.001
ADAM_B1 = 0.9
ADAM_B2 = 0.999
ADAM_EPS = 1e-08
ADAM_WD = 0.01
ADAM_STEP = 10

VMEM_LIMIT_BYTES = 56 * 1024 * 1024
T_ATT_MAX = 1024
TR_MAX = 512
TMR_MAX = 512

NN = ((1,), (0,))
NT = ((1,), (1,))
TN = ((0,), (0,))


def _params(sem=None):
    return pltpu.CompilerParams(dimension_semantics=sem, vmem_limit_bytes=VMEM_LIMIT_BYTES)


def _dot(a, b, contract=NN):
    return lax.dot_general(a, b, (contract, ((), ())), preferred_element_type=F32)


def _dot3(x, t):
    x1 = x.astype(BF16)
    r1 = x - x1.astype(F32)
    x2 = r1.astype(BF16)
    x3 = (r1 - x2.astype(F32)).astype(BF16)
    return _dot(x1, t) + _dot(x2, t) + _dot(x3, t)


def _iota2(shape, dim):
    return lax.broadcasted_iota(jnp.int32, shape, dim)


def _row_call(name, fn, row_ins, bcast_ins, row_outs, acc_outs, tr, jobs=()):
    S = row_ins[0].shape[0]
    assert S % tr == 0
    n_ri, n_bi, n_ro, n_ao = len(row_ins), len(bcast_ins), len(row_outs), len(acc_outs)

    def body(*refs):
        ins = [r[...] for r in refs[:n_ri + n_bi]]
        ro_refs = refs[n_ri + n_bi:n_ri + n_bi + n_ro]
        ao_refs = refs[n_ri + n_bi + n_ro:]
        ro, ao = fn(*ins)
        for r, v in zip(ro_refs, ro):
            r[...] = v.astype(r.dtype)
        if n_ao:
            @pl.when(pl.program_id(0) == 0)
            def _():
                for r in ao_refs:
                    r[...] = jnp.zeros_like(r)
            for r, v in zip(ao_refs, ao):
                r[...] += v

    in_specs = [pl.BlockSpec((tr, a.shape[1]), lambda i: (i, 0)) for a in row_ins]
    in_specs += [pl.BlockSpec(a.shape, lambda i: (0, 0)) for a in bcast_ins]
    out_specs = [pl.BlockSpec((tr, d), lambda i: (i, 0)) for d, _ in row_outs]
    out_specs += [pl.BlockSpec((1, d), lambda i: (0, 0)) for d in acc_outs]
    out_shape = [jax.ShapeDtypeStruct((S, d), dt) for d, dt in row_outs]
    out_shape += [jax.ShapeDtypeStruct((1, d), F32) for d in acc_outs]
    outs, job_res = _carry_call(
        body, name=name, grid=(S // tr,), in_specs=in_specs, out_specs=out_specs, out_shape=out_shape,
        scratch_shapes=[], semantics=("arbitrary",) if n_ao else ("parallel",), args=list(row_ins) + list(bcast_ins),
        jobs=jobs)
    res = (outs[:n_ro], outs[n_ro:])
    return (res, job_res) if jobs else res


def _rms_fwd(x, g):
    rstd = lax.rsqrt(jnp.mean(x * x, axis=-1, keepdims=True) + EPS)
    return x * rstd * g


def _rms_bwd(dy, x, g):
    rstd = lax.rsqrt(jnp.mean(x * x, axis=-1, keepdims=True) + EPS)
    xhat = x * rstd
    dg = jnp.sum(dy * xhat, axis=0, keepdims=True)
    dxhat = dy * g
    dx = rstd * (dxhat - xhat * jnp.mean(dxhat * xhat, axis=-1, keepdims=True))
    return dx, dg


_GELU_C = math.sqrt(2.0 / math.pi)


def _gelu(x):
    return 0.5 * x * (1.0 + jnp.tanh(_GELU_C * (x + 0.044715 * (x * x * x))))


def _gelu_grad(x):
    t = jnp.tanh(_GELU_C * (x + 0.044715 * (x * x * x)))
    return 0.5 * (1.0 + t) + 0.5 * x * (1.0 - t * t) * (_GELU_C * (1.0 + 3.0 * 0.044715 * (x * x)))


def _me():
    return lax.axis_index("x"), lax.axis_index("y"), lax.axis_index("c")


def _other_chips(x, y):
    return [(1 - x, y), (x, 1 - y), (1 - x, 1 - y)]


_ANY = pl.BlockSpec(memory_space=pl.ANY)


class _Job:
    def __init__(self, ins, outs, n_sems, make, aliases=None):
        self.ins, self.outs, self.n_sems, self.make, self.aliases = ins, outs, n_sems, make, aliases or {}


def _job_gather_chips(blk, part=(0, 1, 1), into=None):
    R, C = blk.shape
    nr = R // part[2]
    rows = pl.ds(part[0] * nr, (part[1] - part[0]) * nr)

    def make(ins, outs, send_sems, recv_sems, base):
        x_ref, (out_ref,) = ins[0], outs
        x, y, c = _me()
        mine = 4 * x + 2 * y + c
        targets = [(x, y, 1 - c)] + [(cx, cy, c) for cx, cy in _other_chips(x, y)]

        def copy(k, slab, to):
            return pltpu.make_async_remote_copy(
                src_ref=x_ref.at[rows, :], dst_ref=out_ref.at[slab, rows, :], send_sem=send_sems.at[base + k],
                recv_sem=recv_sems.at[base + k], device_id=to, device_id_type=MESH)

        starts = [copy(k, mine, to) for k, to in enumerate(targets)]
        arrivals = [copy(k, 4 * tx + 2 * ty + tc, (tx, ty, tc)) for k, (tx, ty, tc) in enumerate(targets)]
        local = [pltpu.make_async_copy(x_ref.at[rows, :], out_ref.at[mine, rows, :], send_sems.at[base + 4])]
        return starts, arrivals, local

    out = jax.ShapeDtypeStruct((N_DEV, R, C), blk.dtype)
    if into is None:
        return _Job([blk], [out], 5, make)
    return _Job([blk, into], [out], 5, make, aliases={1: 0})


def _job_gather_sibling(part):
    def make(ins, outs, send_sems, recv_sems, base):
        (out_ref,) = outs
        x, y, c = _me()

        def copy(k, slab):
            return pltpu.make_async_remote_copy(
                src_ref=out_ref.at[slab], dst_ref=out_ref.at[slab], send_sem=send_sems.at[base + k],
                recv_sem=recv_sems.at[base + k], device_id=(x, y, 1 - c), device_id_type=MESH)

        chips = _other_chips(x, y)
        starts = [copy(k, 4 * cx + 2 * cy + c) for k, (cx, cy) in enumerate(chips)]
        arrivals = [copy(k, 4 * cx + 2 * cy + (1 - c)) for k, (cx, cy) in enumerate(chips)]
        return starts, arrivals, []

    return _Job([part], [jax.ShapeDtypeStruct(part.shape, part.dtype)], 3, make, aliases={0: 0})


def _job_scatter_sibling(gb):
    _, _, R, C = gb.shape

    def make(ins, outs, send_sems, recv_sems, base):
        (g_ref,), (recv_ref,) = ins, outs
        x, y, c = _me()
        copies = [pltpu.make_async_remote_copy(
            src_ref=g_ref.at[p, 1 - c], dst_ref=recv_ref.at[p], send_sem=send_sems.at[base + p],
            recv_sem=recv_sems.at[base + p], device_id=(x, y, 1 - c), device_id_type=MESH) for p in range(4)]
        return copies, copies, []

    return _Job([gb], [jax.ShapeDtypeStruct((4, R, C), gb.dtype)], 4, make)


def _job_scatter_sibling_windows(gb, starts, width):
    R, _ = gb.shape

    def make(ins, outs, send_sems, recv_sems, base):
        (g_ref,), (recv_ref,) = ins, outs
        x, y, c = _me()
        copies = []
        for p in range(4):
            start = pl.multiple_of(jnp.where(c == 0, starts[2 * p + 1], starts[2 * p]), LANES)
            copies.append(pltpu.make_async_remote_copy(
                src_ref=g_ref.at[:, pl.ds(start, width)], dst_ref=recv_ref.at[p], send_sem=send_sems.at[base + p],
                recv_sem=recv_sems.at[base + p], device_id=(x, y, 1 - c), device_id_type=MESH))
        return copies, copies, []

    return _Job([gb], [jax.ShapeDtypeStruct((4, R, width), gb.dtype)], 4, make)


def _job_scatter_chips(hb, part=(0, 1, 1), into=None):
    _, R, C = hb.shape
    nr = R // part[2]
    rows = pl.ds(part[0] * nr, (part[1] - part[0]) * nr)

    def make(ins, outs, send_sems, recv_sems, base):
        h_ref, (recv_ref,) = ins[0], outs
        x, y, c = _me()
        copies = [pltpu.make_async_remote_copy(
            src_ref=h_ref.at[2 * cx + cy, rows, :], dst_ref=recv_ref.at[n, rows, :], send_sem=send_sems.at[base + n],
            recv_sem=recv_sems.at[base + n], device_id=(cx, cy, c), device_id_type=MESH)
            for n, (cx, cy) in enumerate(_other_chips(x, y))]
        return copies, copies, []

    out = jax.ShapeDtypeStruct((3, R, C), hb.dtype)
    if into is None:
        return _Job([hb], [out], 3, make)
    return _Job([hb, into], [out], 3, make, aliases={1: 0})


def _carry_call(body, *, name, grid, in_specs, out_specs, out_shape, scratch_shapes, semantics, args, jobs=()):
    jobs = list(jobs)
    n_in, n_out, n_scr = len(in_specs), len(out_specs), len(scratch_shapes)
    j_ins = [a for j in jobs for a in j.ins]
    j_outs = [o for j in jobs for o in j.outs]
    n_sems = sum(j.n_sems for j in jobs)
    aliases = {}
    i0, o0 = n_in, n_out
    for j in jobs:
        for a, b in j.aliases.items():
            aliases[i0 + a] = o0 + b
        i0 += len(j.ins)
        o0 += len(j.outs)

    def full_body(*refs):
        ins = refs[:n_in]
        jin = refs[n_in:n_in + len(j_ins)]
        outs = refs[n_in + len(j_ins):n_in + len(j_ins) + n_out]
        jout = refs[n_in + len(j_ins) + n_out:n_in + len(j_ins) + n_out + len(j_outs)]
        scr = refs[n_in + len(j_ins) + n_out + len(j_outs):]
        if jobs:
            send_sems, recv_sems = scr[n_scr], scr[n_scr + 1]
            starts, arrivals, local = [], [], []
            base = i0 = o0 = 0
            for j in jobs:
                s, a, l = j.make(jin[i0:i0 + len(j.ins)], jout[o0:o0 + len(j.outs)], send_sems, recv_sems, base)
                starts += s
                arrivals += a
                local += l
                base += j.n_sems
                i0 += len(j.ins)
                o0 += len(j.outs)
            pids = [pl.program_id(d) for d in range(len(grid))]
            first = functools.reduce(jnp.logical_and, [p == 0 for p in pids])
            last = functools.reduce(jnp.logical_and, [p == n - 1 for p, n in zip(pids, grid)])

            @pl.when(first)
            def _():
                for cp in local + starts:
                    cp.start()

        body(*ins, *outs, *scr[:n_scr])

        if jobs:
            @pl.when(last)
            def _():
                for cp in arrivals:
                    cp.wait_recv()
                for cp in starts:
                    cp.wait_send()
                for cp in local:
                    cp.wait()

    sems = [pltpu.SemaphoreType.DMA((n_sems,)), pltpu.SemaphoreType.DMA((n_sems,))] if jobs else []
    res = pl.pallas_call(
        full_body, name=name, grid=grid,
        in_specs=list(in_specs) + [_ANY] * len(j_ins),
        out_specs=list(out_specs) + [_ANY] * len(j_outs),
        out_shape=list(out_shape) + j_outs,
        scratch_shapes=list(scratch_shapes) + sems,
        input_output_aliases=aliases,
        compiler_params=_params(("arbitrary",) * len(grid) if jobs else semantics),
    )(*args, *j_ins)
    body_res, job_res = res[:n_out], res[n_out:]
    per_job = []
    for j in jobs:
        per_job.append(job_res[:len(j.outs)])
        job_res = job_res[len(j.outs):]
    return body_res, per_job


def _run_jobs(name, jobs):
    def body(done_ref):
        done_ref[...] = jnp.zeros_like(done_ref)

    return _carry_call(body, name=name, grid=(1,), in_specs=[], out_specs=[pl.BlockSpec((8, LANES), lambda i: (0, 0))],
                       out_shape=[jax.ShapeDtypeStruct((8, LANES), F32)], scratch_shapes=[], semantics=("arbitrary",),
                       args=[], jobs=jobs)[1]


def _mm(name, grid, a, a_spec, b, b_spec, contract, acc_shape, out_shape, out_specs, extras=(), epilogue=None, jobs=()):
    nk = grid[2]
    n_e = len(extras)
    n_o = len(out_shape)
    if epilogue is None:
        epilogue = lambda acc: (acc,)

    def body(a_ref, b_ref, *rest):
        e_refs = rest[:n_e]
        o_refs = rest[n_e:n_e + n_o]

        def finish(total):
            res = epilogue(total, *[r[...] for r in e_refs])
            for o, r in zip(o_refs, res):
                o[...] = r.astype(o.dtype)

        if nk == 1:
            finish(_dot(a_ref[...], b_ref[...], contract))
            return
        acc = rest[n_e + n_o]
        k = pl.program_id(2)

        @pl.when(k == 0)
        def _():
            acc[...] = _dot(a_ref[...], b_ref[...], contract)

        @pl.when(jnp.logical_and(k > 0, k < nk - 1))
        def _():
            acc[...] += _dot(a_ref[...], b_ref[...], contract)

        @pl.when(k == nk - 1)
        def _():
            finish(acc[...] + _dot(a_ref[...], b_ref[...], contract))

    outs, job_res = _carry_call(
        body, name=name, grid=grid, in_specs=[a_spec, b_spec] + [s for _, s in extras],
        out_specs=list(out_specs), out_shape=list(out_shape),
        scratch_shapes=[pltpu.VMEM(acc_shape, F32)] if nk > 1 else [],
        semantics=("parallel", "parallel", "arbitrary"), args=[a, b] + [e for e, _ in extras], jobs=jobs)
    return (outs, job_res) if jobs else outs


def _mm_rows(name, grid, a, a_spec, b, b_spec, contract, tm, n, row_extras, bcast, row_outs, acc_outs, epilogue, jobs=()):
    nk = grid[1]
    M = grid[0] * tm
    n_x, n_b, n_ro, n_ao = len(row_extras), len(bcast), len(row_outs), len(acc_outs)

    def body(a_ref, b_ref, *rest):
        x_refs = rest[:n_x + n_b]
        rest = rest[n_x + n_b:]
        ro_refs = rest[:n_ro]
        ao_refs = rest[n_ro:n_ro + n_ao]
        i = pl.program_id(0)

        def finish(total):
            ro, ao = epilogue(total, *[r[...] for r in x_refs])
            for r, v in zip(ro_refs, ro):
                r[...] = v.astype(r.dtype)
            if n_ao:
                @pl.when(i == 0)
                def _():
                    for r, v in zip(ao_refs, ao):
                        r[...] = v

                @pl.when(i > 0)
                def _():
                    for r, v in zip(ao_refs, ao):
                        r[...] += v

        if nk == 1:
            finish(_dot(a_ref[...], b_ref[...], contract))
            return
        acc = rest[n_ro + n_ao]
        k = pl.program_id(1)

        @pl.when(k == 0)
        def _():
            acc[...] = _dot(a_ref[...], b_ref[...], contract)

        @pl.when(jnp.logical_and(k > 0, k < nk - 1))
        def _():
            acc[...] += _dot(a_ref[...], b_ref[...], contract)

        @pl.when(k == nk - 1)
        def _():
            finish(acc[...] + _dot(a_ref[...], b_ref[...], contract))

    in_specs = [a_spec, b_spec] + [pl.BlockSpec((tm, x.shape[1]), lambda i, k: (i, 0)) for x in row_extras]
    in_specs += [pl.BlockSpec(x.shape, lambda i, k: (0,) * x.ndim) for x in bcast]
    out_specs = [pl.BlockSpec((tm, w), lambda i, k: (i, 0)) for w, _ in row_outs]
    out_specs += [pl.BlockSpec((1, w), lambda i, k: (0, 0)) for w in acc_outs]
    out_shape = [jax.ShapeDtypeStruct((M, w), dt) for w, dt in row_outs]
    out_shape += [jax.ShapeDtypeStruct((1, w), F32) for w in acc_outs]
    outs, job_res = _carry_call(
        body, name=name, grid=grid, in_specs=in_specs, out_specs=out_specs, out_shape=out_shape,
        scratch_shapes=[pltpu.VMEM((tm, n), F32)] if nk > 1 else [],
        semantics=("arbitrary", "arbitrary"), args=[a, b] + list(row_extras) + list(bcast), jobs=jobs)
    res = (outs[:n_ro], outs[n_ro:])
    return (res, job_res) if jobs else res


def _mm_nn(name, a, b, out_dtypes, tm, tn, tk, extras=(), epilogue=None, jobs=()):
    M, K = a.shape
    N = b.shape[1]
    tm, tn, tk = min(tm, M), min(tn, N), min(tk, K)
    o_spec = pl.BlockSpec((tm, tn), lambda i, j, k: (i, j))
    return _mm(name, (M // tm, N // tn, K // tk),
               a, pl.BlockSpec((tm, tk), lambda i, j, k: (i, k)),
               b, pl.BlockSpec((tk, tn), lambda i, j, k: (k, j)), NN, (tm, tn),
               [jax.ShapeDtypeStruct((M, N), dt) for dt in out_dtypes], [o_spec] * len(out_dtypes),
               [(e, o_spec) for e in extras], epilogue, jobs)


def _mm_nt(name, a, b, out_dtypes, tm, tn, tk, extras=(), epilogue=None, jobs=()):
    M, K = a.shape
    N = b.shape[0]
    tm, tn, tk = min(tm, M), min(tn, N), min(tk, K)
    o_spec = pl.BlockSpec((tm, tn), lambda i, j, k: (i, j))
    return _mm(name, (M // tm, N // tn, K // tk),
               a, pl.BlockSpec((tm, tk), lambda i, j, k: (i, k)),
               b, pl.BlockSpec((tn, tk), lambda i, j, k: (j, k)), NT, (tm, tn),
               [jax.ShapeDtypeStruct((M, N), dt) for dt in out_dtypes], [o_spec] * len(out_dtypes),
               [(e, o_spec) for e in extras], epilogue, jobs)


def _mm_tn(name, a, b, out_dtypes, tm, tn, tk, jobs=()):
    K, M = a.shape
    N = b.shape[1]
    tm, tn, tk = min(tm, M), min(tn, N), min(tk, K)
    o_spec = pl.BlockSpec((tm, tn), lambda i, j, k: (i, j))
    return _mm(name, (M // tm, N // tn, K // tk),
               a, pl.BlockSpec((tk, tm), lambda i, j, k: (k, i)),
               b, pl.BlockSpec((tk, tn), lambda i, j, k: (k, j)), TN, (tm, tn),
               [jax.ShapeDtypeStruct((M, N), dt) for dt in out_dtypes], [o_spec] * len(out_dtypes),
               epilogue=lambda acc: (acc,) * len(out_dtypes), jobs=jobs)


def _fgate_fwd(zf, bf):
    S = zf.shape[0]
    nc = S // CHUNK

    def body(zf_ref, bf_ref, f_ref):
        upper = (_iota2((CHUNK, CHUNK), 0) <= _iota2((CHUNK, CHUNK), 1)).astype(BF16)
        carry = jnp.zeros((8, 1), F32)
        for c in range(nc):
            t = zf_ref[c * CHUNK:(c + 1) * CHUNK, :] + bf_ref[...]
            lf = jnp.minimum(t, 0.0) - jnp.log(1.0 + jnp.exp(-jnp.abs(t)))
            lf_rows = lf.T[0:8, :]
            f_ref[:, c * CHUNK:(c + 1) * CHUNK] = (_dot3(lf_rows, upper) + carry) * LOG2E
            carry = carry + jnp.sum(lf_rows, axis=-1, keepdims=True)

    return pl.pallas_call(
        body, name="fgate_fwd", out_shape=jax.ShapeDtypeStruct((8, S), F32),
        compiler_params=_params(),
    )(zf, bf)


def _fgate_bwd(df, zf, bf):
    S = zf.shape[0]
    nc = S // CHUNK

    def body(df_ref, zf_ref, bf_ref, dzf_ref, dbf_ref):
        lower = (_iota2((CHUNK, CHUNK), 0) >= _iota2((CHUNK, CHUNK), 1)).astype(BF16)
        carry = jnp.zeros((8, 1), F32)
        dbf = jnp.zeros((1, LANES), F32)
        for c in reversed(range(nc)):
            sl = slice(c * CHUNK, (c + 1) * CHUNK)
            df = df_ref[:, sl]
            r = _dot3(df, lower) + carry
            carry = carry + jnp.sum(df, axis=-1, keepdims=True)
            r_cols = jnp.concatenate([r, jnp.zeros((CHUNK - 8, CHUNK), F32)], axis=0).T
            t = zf_ref[sl, :] + bf_ref[...]
            dz = r_cols * (1.0 / (1.0 + jnp.exp(t)))
            dzf_ref[sl, :] = dz.astype(BF16)
            dbf = dbf + jnp.sum(dz, axis=0, keepdims=True)
        dbf_ref[...] = dbf

    return pl.pallas_call(
        body, name="fgate_bwd",
        out_shape=[jax.ShapeDtypeStruct((S, LANES), BF16), jax.ShapeDtypeStruct((1, LANES), F32)],
        compiler_params=_params(),
    )(df, zf, bf)


_NEG = -1e30
LOG2E = 1.4426950408889634
N_SPLIT = 8
N_SPLIT_DIAG = 4
DIAG_STEP = 1024


def _attn_consts(T):
    rows, cols = _iota2((T, T), 0), _iota2((T, T), 1)
    return cols <= rows, rows <= cols


def _col_to_row(col):
    wide = jnp.broadcast_to(col, (col.shape[0], LANES))
    return jnp.concatenate([wide[r:r + LANES, :].T[0:1, :] for r in range(0, col.shape[0], LANES)], axis=1)


def _row_to_col(row):
    tall = jnp.broadcast_to(row, (LANES, row.shape[1]))
    return jnp.concatenate([tall[:, c:c + LANES].T[:, 0:1] for c in range(0, row.shape[1], LANES)], axis=0)


def _attn2_fwd(zm, f2row, T, jobs=()):
    S = zm.shape[0]
    H = f2row.shape[0]
    nb = S // T
    c2 = LOG2E / math.sqrt(HEAD_DIM)

    def body(q_ref, k_ref, v_ref, fk_ref, o_ref, lse_ref, vaug_s, fq_ref):
        i = pl.program_id(1)

        @pl.when(i == 0)
        def _():
            vaug_s[:, :HEAD_DIM] = v_ref[...]
            vaug_s[:, HEAD_DIM:] = jnp.ones((S, HEAD_DIM), BF16)

        fq_ref[...] = _row_to_col(fk_ref[i])
        keep = _attn_consts(T)[0]
        TH = T // N_SPLIT

        def block(j, diagonal, state):
            r0 = pl.multiple_of(j * T, T)
            fk = fk_ref[j]
            new = []
            for g, (m_old, acc) in enumerate(state):
                rows = slice(g * TH, (g + 1) * TH)
                nk = min(T, -(-(g + 1) * TH // DIAG_STEP) * DIAG_STEP) if diagonal else T
                s = _dot(q_ref[rows, :], k_ref[pl.ds(r0, nk), :], NT) * c2 + (fq_ref[rows, :] - fk[:, :nk])
                if diagonal:
                    s = jnp.where(keep[rows, :nk], s, _NEG)
                m_new = jnp.maximum(m_old, jnp.max(s, axis=-1, keepdims=True))
                p = jnp.exp2(s - m_new).astype(BF16)
                new.append((m_new, jnp.exp2(m_old - m_new) * acc + _dot(p, vaug_s[pl.ds(r0, nk), :])))
            return tuple(new)

        init = tuple((jnp.full((TH, 1), _NEG, F32), jnp.zeros((TH, 2 * HEAD_DIM), F32)) for _ in range(N_SPLIT))
        state = lax.fori_loop(0, i, lambda j, st: block(j, False, st), init)
        state = block(i, True, state)
        for g, (m, acc) in enumerate(state):
            rows = slice(g * TH, (g + 1) * TH)
            o_ref[rows, :] = acc[:, :HEAD_DIM] / acc[:, HEAD_DIM:]
            lse_ref[:, rows] = _col_to_row(m + jnp.log2(acc[:, HEAD_DIM:HEAD_DIM + 1]))

    nh = H
    return _carry_call(
        body, name="attn_fwd", grid=(H, nb), jobs=jobs, args=[zm, zm, zm, f2row],
        semantics=("arbitrary", "arbitrary"),
        in_specs=[
            pl.BlockSpec((T, HEAD_DIM), lambda h, i: (i, h)),
            pl.BlockSpec((S, HEAD_DIM), lambda h, i: (0, nh + h)),
            pl.BlockSpec((S, HEAD_DIM), lambda h, i: (0, 2 * nh + h)),
            pl.BlockSpec((None, nb, 1, T), lambda h, i: (h, 0, 0, 0)),
        ],
        out_specs=[pl.BlockSpec((T, HEAD_DIM), lambda h, i: (i, h)),
                   pl.BlockSpec((None, None, 1, T), lambda h, i: (h, i, 0, 0))],
        out_shape=[jax.ShapeDtypeStruct((S, H * HEAD_DIM), F32), jax.ShapeDtypeStruct((H, nb, 1, T), F32)],
        scratch_shapes=[pltpu.VMEM((S, 2 * HEAD_DIM), BF16), pltpu.VMEM((T, 1), F32)],
    )


def _attn2_bwd_dq(zm, dattn, f2row, lse2_row, delta_row, T, jobs=()):
    S = zm.shape[0]
    H = f2row.shape[0]
    nb = S // T
    scale = 1.0 / math.sqrt(HEAD_DIM)
    c2 = LOG2E * scale

    def body(q_ref, k_ref, v_ref, do_ref, fk_ref, lse_ref, dlr_ref, dq_ref, rs_ref, bias_s, do_s, dl_ref):
        i = pl.program_id(1)
        keep = _attn_consts(T)[0]
        TH = T // N_SPLIT_DIAG
        bias_s[...] = _row_to_col(fk_ref[i] - lse_ref[...])
        dl_ref[...] = _row_to_col(dlr_ref[...])
        do_s[...] = do_ref[...].astype(BF16)

        def part(rows, j, nk, state, masked):
            acc, rs = state
            r0 = pl.multiple_of(j * T, T)
            kb = k_ref[pl.ds(r0, nk), :]
            s = _dot(q_ref[rows, :], kb, NT) * c2 + (bias_s[rows, :] - fk_ref[j][:, :nk])
            if masked:
                s = jnp.where(keep[rows, :nk], s, _NEG)
            ds = jnp.exp2(s) * (_dot(do_s[rows, :], v_ref[pl.ds(r0, nk), :], NT) - dl_ref[rows, :])
            return acc + _dot(ds.astype(BF16), kb), rs + jnp.sum(ds, axis=-1, keepdims=True)

        def step(j, state):
            return part(slice(0, T), j, T, state, False)

        acc, rs = lax.fori_loop(0, i, step, (jnp.zeros((T, HEAD_DIM), F32), jnp.zeros((T, 1), F32)))
        for g in range(N_SPLIT_DIAG):
            rows = slice(g * TH, (g + 1) * TH)
            acc_g, rs_g = part(rows, i, (g + 1) * TH, (acc[rows, :], rs[rows, :]), True)
            dq_ref[rows, :] = (acc_g * scale).astype(BF16)
            rs_ref[:, rows] = _col_to_row(rs_g)

    nh = H
    row = pl.BlockSpec((None, None, 1, T), lambda h, i: (h, i, 0, 0))
    blk = pl.BlockSpec((T, HEAD_DIM), lambda h, i: (i, h))
    return _carry_call(
        body, name="attn_bwd_dq", grid=(H, nb), jobs=jobs,
        args=[zm, zm, zm, dattn, f2row, lse2_row, delta_row], semantics=("arbitrary", "arbitrary"),
        in_specs=[
            blk,
            pl.BlockSpec((S, HEAD_DIM), lambda h, i: (0, nh + h)),
            pl.BlockSpec((S, HEAD_DIM), lambda h, i: (0, 2 * nh + h)),
            blk,
            pl.BlockSpec((None, nb, 1, T), lambda h, i: (h, 0, 0, 0)),
            row, row,
        ],
        out_specs=[blk, row],
        out_shape=[jax.ShapeDtypeStruct((S, H * HEAD_DIM), BF16), jax.ShapeDtypeStruct((H, nb, 1, T), F32)],
        scratch_shapes=[pltpu.VMEM((T, 1), F32), pltpu.VMEM((T, HEAD_DIM), BF16), pltpu.VMEM((T, 1), F32)],
    )


def _attn2_bwd_dkv(zm, dattn, f2row, lse2_row, delta_row, rowsum_row, T, jobs=()):
    S = zm.shape[0]
    H = f2row.shape[0]
    nb = S // T
    scale = 1.0 / math.sqrt(HEAD_DIM)
    c2 = LOG2E * scale

    def body(q_ref, k_ref, v_ref, do_ref, fq_ref, lse_ref, dl_ref, rs_ref, dk_ref, dv_ref, df_ref, fk_ref):
        j = pl.program_id(1)
        keep = _attn_consts(T)[1]
        TH = T // N_SPLIT_DIAG
        fk_ref[...] = _row_to_col(fq_ref[j])

        def part(rows, i, c0, state, masked):
            dk, dv, df = state
            r0 = pl.multiple_of(i * T + c0, TH)
            qb = q_ref[pl.ds(r0, T - c0), :]
            do = do_ref[pl.ds(r0, T - c0), :].astype(BF16)
            bias = (fq_ref[i] - lse_ref[i])[:, c0:]
            dl = (dl_ref[i] + rs_ref[i])[:, c0:]
            st = _dot(k_ref[rows, :], qb, NT) * c2 + (bias - fk_ref[rows, :])
            if masked:
                st = jnp.where(keep[rows, c0:], st, _NEG)
            pt = jnp.exp2(st)
            dst = pt * (_dot(v_ref[rows, :], do, NT) - dl)
            return (dk + _dot(dst.astype(BF16), qb), dv + _dot(pt.astype(BF16), do),
                    df - jnp.sum(dst, axis=-1, keepdims=True))

        groups = []
        for g in range(N_SPLIT_DIAG):
            zero = (jnp.zeros((TH, HEAD_DIM), F32), jnp.zeros((TH, HEAD_DIM), F32), jnp.zeros((TH, 1), F32))
            groups.append(part(slice(g * TH, (g + 1) * TH), j, g * TH, zero, True))
        state = tuple(jnp.concatenate([grp[n] for grp in groups], axis=0) for n in range(3))
        dk, dv, df = lax.fori_loop(j + 1, nb, lambda i, st: part(slice(0, T), i, 0, st, False), state)
        dk_ref[...] = (dk * scale).astype(BF16)
        dv_ref[...] = dv.astype(BF16)
        df_ref[...] = _col_to_row(df)

    nh = H
    row = pl.BlockSpec((None, nb, 1, T), lambda h, j: (h, 0, 0, 0))
    whole = pl.BlockSpec((S, HEAD_DIM), lambda h, j: (0, h))
    kv_out = pl.BlockSpec((T, HEAD_DIM), lambda h, j: (j, h))
    return _carry_call(
        body, name="attn_bwd_dkv", grid=(H, nb), jobs=jobs,
        args=[zm, zm, zm, dattn, f2row, lse2_row, delta_row, rowsum_row],
        semantics=("arbitrary", "arbitrary"),
        in_specs=[
            whole,
            pl.BlockSpec((T, HEAD_DIM), lambda h, j: (j, nh + h)),
            pl.BlockSpec((T, HEAD_DIM), lambda h, j: (j, 2 * nh + h)),
            whole, row, row, row, row,
        ],
        out_specs=[kv_out, kv_out, pl.BlockSpec((None, None, 1, T), lambda h, j: (h, j, 0, 0))],
        out_shape=[jax.ShapeDtypeStruct((S, H * HEAD_DIM), BF16), jax.ShapeDtypeStruct((S, H * HEAD_DIM), BF16),
                   jax.ShapeDtypeStruct((H, nb, 1, T), F32)],
        scratch_shapes=[pltpu.VMEM((T, 1), F32)],
    )


def _attn_delta(dattn, attn, tr):
    S, DA = attn.shape
    H = DA // HEAD_DIM

    def body(do_ref, o_ref, out_ref):
        lo = _iota2((DA, LANES), 1) * HEAD_DIM
        sel = ((_iota2((DA, LANES), 0) >= lo) & (_iota2((DA, LANES), 0) < lo + HEAD_DIM)).astype(BF16)
        d = _dot3(do_ref[...] * o_ref[...], sel)
        for c in range(tr // CHUNK):
            out_ref[:, c * CHUNK:(c + 1) * CHUNK] = d[c * CHUNK:(c + 1) * CHUNK, :].T[0:H, :]

    return pl.pallas_call(
        body, name="attn_delta", grid=(S // tr,),
        in_specs=[pl.BlockSpec((tr, DA), lambda i: (i, 0))] * 2,
        out_specs=pl.BlockSpec((H, tr), lambda i: (0, i)),
        out_shape=jax.ShapeDtypeStruct((H, S), F32),
        compiler_params=_params(("parallel",)),
    )(dattn, attn)


def _ln_stats(x):
    mu = jnp.mean(x, axis=-1, keepdims=True)
    xc = x - mu
    rstd = lax.rsqrt(jnp.mean(xc * xc, axis=-1, keepdims=True) + EPS)
    return xc * rstd, rstd


def _tril_mask():
    return _iota2((CHUNK, CHUNK), 0) >= _iota2((CHUNK, CHUNK), 1)


def _gmlp_fwd(zm, ln_g, ln_b, w_s, bs_col, tr):
    S = zm.shape[0]
    H = w_s.shape[0]
    DG = H * HEAD_DIM

    def body(zu_ref, zv_ref, g_ref, b_ref, w_ref, bs_ref, out_ref):
        u = _gelu(zu_ref[...].astype(F32))
        y, _ = _ln_stats(_gelu(zv_ref[...].astype(F32)))
        v = (y * g_ref[...] + b_ref[...]).astype(BF16)
        mask = _tril_mask()
        for h in range(H):
            wc = jnp.where(mask, w_ref[h], 0.0).astype(BF16)
            cs = slice(h * HEAD_DIM, (h + 1) * HEAD_DIM)
            for c in range(tr // CHUNK):
                rs = slice(c * CHUNK, (c + 1) * CHUNK)
                mix = _dot(wc, v[rs, cs]) + bs_ref[h]
                out_ref[rs, cs] = u[rs, cs] * mix

    full = lambda a: pl.BlockSpec(a.shape, lambda i: (0,) * a.ndim)
    return pl.pallas_call(
        body, name="gmlp_fwd", grid=(S // tr,),
        in_specs=[pl.BlockSpec((tr, DG), lambda i: (i, 3)), pl.BlockSpec((tr, DG), lambda i: (i, 4)),
                  full(ln_g), full(ln_b), full(w_s), full(bs_col)],
        out_specs=pl.BlockSpec((tr, DG), lambda i: (i, 0)),
        out_shape=jax.ShapeDtypeStruct((S, DG), F32),
        compiler_params=_params(("parallel",)),
    )(zm, zm, ln_g, ln_b, w_s, bs_col)


def _gmlp_bwd(dgm, zm, ln_g, ln_b, w_s, w_st, bs_col, tr):
    S = zm.shape[0]
    H = w_s.shape[0]
    DG = H * HEAD_DIM

    def body(dg_ref, zu_ref, zv_ref, g_ref, b_ref, w_ref, wt_ref, bs_ref,
             dzu_ref, dzv_ref, dw_ref, dbs_ref, dlg_ref, dlb_ref, dv_s):
        @pl.when(pl.program_id(0) == 0)
        def _():
            dw_ref[...] = jnp.zeros_like(dw_ref)
            dbs_ref[...] = jnp.zeros_like(dbs_ref)
            dlg_ref[...] = jnp.zeros_like(dlg_ref)
            dlb_ref[...] = jnp.zeros_like(dlb_ref)

        zu = zu_ref[...].astype(F32)
        zv = zv_ref[...].astype(F32)
        u = _gelu(zu)
        y, rstd = _ln_stats(_gelu(zv))
        v = (y * g_ref[...] + b_ref[...]).astype(BF16)
        dgm_blk = dg_ref[...]
        mask = _tril_mask()
        mask_t = _iota2((CHUNK, CHUNK), 0) <= _iota2((CHUNK, CHUNK), 1)
        for h in range(H):
            wc = jnp.where(mask, w_ref[h], 0.0).astype(BF16)
            wct = jnp.where(mask_t, wt_ref[h], 0.0).astype(BF16)
            cs = slice(h * HEAD_DIM, (h + 1) * HEAD_DIM)
            dw = jnp.zeros((CHUNK, CHUNK), F32)
            dbs = jnp.zeros((CHUNK, 1), F32)
            for c in range(tr // CHUNK):
                rs = slice(c * CHUNK, (c + 1) * CHUNK)
                vch = v[rs, cs]
                mix = _dot(wc, vch) + bs_ref[h]
                dg = dgm_blk[rs, cs]
                dzu_ref[rs, cs] = (dg * mix * _gelu_grad(zu[rs, cs])).astype(BF16)
                dmix = dg * u[rs, cs]
                dbs = dbs + jnp.sum(dmix, axis=-1, keepdims=True)
                dmix_b = dmix.astype(BF16)
                dw = dw + _dot(dmix_b, vch, NT)
                dv_s[rs, cs] = _dot(wct, dmix_b)
            dw_ref[h] += jnp.where(mask, dw, 0.0)
            dbs_ref[h] += dbs
        dv = dv_s[...]
        dlg_ref[...] += jnp.sum(dv * y, axis=0, keepdims=True)
        dlb_ref[...] += jnp.sum(dv, axis=0, keepdims=True)
        dy = dv * g_ref[...]
        dgv = rstd * (dy - jnp.mean(dy, axis=-1, keepdims=True) - y * jnp.mean(dy * y, axis=-1, keepdims=True))
        dzv_ref[...] = (dgv * _gelu_grad(zv)).astype(BF16)

    full = lambda a: pl.BlockSpec(a.shape, lambda i: (0,) * a.ndim)
    rows = pl.BlockSpec((tr, DG), lambda i: (i, 0))
    return pl.pallas_call(
        body, name="gmlp_bwd", grid=(S // tr,),
        in_specs=[rows, pl.BlockSpec((tr, DG), lambda i: (i, 3)), pl.BlockSpec((tr, DG), lambda i: (i, 4)),
                  full(ln_g), full(ln_b), full(w_s), full(w_st), full(bs_col)],
        out_specs=[rows, rows, full(w_s), full(bs_col), full(ln_g), full(ln_b)],
        out_shape=[jax.ShapeDtypeStruct((S, DG), BF16), jax.ShapeDtypeStruct((S, DG), BF16),
                   jax.ShapeDtypeStruct(w_s.shape, F32), jax.ShapeDtypeStruct(bs_col.shape, F32),
                   jax.ShapeDtypeStruct(ln_g.shape, F32), jax.ShapeDtypeStruct(ln_b.shape, F32)],
        scratch_shapes=[pltpu.VMEM((tr, DG), F32)],
        compiler_params=_params(("arbitrary",)),
    )(dgm, zm, zm, ln_g, ln_b, w_s, w_st, bs_col)


def _all_gather(name, blk):
    R, C = blk.shape

    def body(x_ref, out_ref, send_sems, recv_sems, local_sem):
        x, y, c = _me()
        me, sibling = (x, y, c), (x, y, 1 - c)
        chips = [(1 - x, y), (x, 1 - y), (1 - x, 1 - y)]

        def slab(px, py, pc):
            return out_ref.at[4 * px + 2 * py + pc]

        def copy(k, block, to, src=None):
            return pltpu.make_async_remote_copy(
                src_ref=slab(*block) if src is None else src, dst_ref=slab(*block),
                send_sem=send_sems.at[k], recv_sem=recv_sems.at[k], device_id=to, device_id_type=MESH)

        mine = pltpu.make_async_copy(x_ref, slab(*me), local_sem)
        mine.start()
        first = [copy(0, me, sibling, src=x_ref)]
        first += [copy(1 + n, me, (*chip, c), src=x_ref) for n, chip in enumerate(chips)]
        for cp in first:
            cp.start()
        passed = [copy(4 + n, (*chip, c), sibling) for n, chip in enumerate(chips)]
        for n, chip in enumerate(chips):
            copy(1 + n, (*chip, c), me).wait_recv()
            passed[n].start()
        copy(0, sibling, me).wait_recv()
        for n, chip in enumerate(chips):
            copy(4 + n, (*chip, 1 - c), me).wait_recv()
        for cp in first + passed:
            cp.wait_send()
        mine.wait()

    return pl.pallas_call(
        body, name=name, out_shape=jax.ShapeDtypeStruct((N_DEV, R, C), blk.dtype),
        in_specs=[_ANY], out_specs=_ANY,
        scratch_shapes=[pltpu.SemaphoreType.DMA((7,)), pltpu.SemaphoreType.DMA((7,)), pltpu.SemaphoreType.DMA(())],
    )(blk)


def _row_tile(R, C, itemsize=4, target_bytes=2 * 1024 * 1024):
    tr = R
    while tr % 2 == 0 and tr * C * itemsize > target_bytes and (tr // 2) % 16 == 0:
        tr //= 2
    return tr


def _rs_add1(name, g4, recv, c_idx):
    _, _, R, C = g4.shape
    tr = _row_tile(R, C)

    def body(c_ref, g_ref, r_ref, hb_ref):
        hb_ref[...] = (g_ref[...] + r_ref[...].astype(F32)).astype(BF16)

    blk = pl.BlockSpec((None, tr, C), lambda p, i, c_ref: (p, i, 0))
    return pl.pallas_call(
        body, name=name,
        grid_spec=pltpu.PrefetchScalarGridSpec(
            num_scalar_prefetch=1, grid=(4, R // tr),
            in_specs=[pl.BlockSpec((None, None, tr, C), lambda p, i, c_ref: (p, c_ref[0], i, 0)), blk],
            out_specs=blk),
        out_shape=jax.ShapeDtypeStruct((4, R, C), BF16),
        compiler_params=_params(("parallel", "parallel")),
    )(c_idx, g4, recv)


def _rs_add2_own(name, g4, recv1, recv2, c_idx, p_idx):
    _, _, R, C = g4.shape
    tr = _row_tile(R, C)

    def body(c_ref, p_ref, g_ref, r1_ref, r2_ref, out_ref):
        h = g_ref[...] + r1_ref[...].astype(F32)
        out_ref[...] = ((h + r2_ref[0].astype(F32)) + r2_ref[1].astype(F32)) + r2_ref[2].astype(F32)

    return pl.pallas_call(
        body, name=name,
        grid_spec=pltpu.PrefetchScalarGridSpec(
            num_scalar_prefetch=2, grid=(R // tr,),
            in_specs=[pl.BlockSpec((None, None, tr, C), lambda i, c_ref, p_ref: (p_ref[0], c_ref[0], i, 0)),
                      pl.BlockSpec((None, tr, C), lambda i, c_ref, p_ref: (p_ref[0], i, 0)),
                      pl.BlockSpec((3, tr, C), lambda i, c_ref, p_ref: (0, i, 0))],
            out_specs=pl.BlockSpec((tr, C), lambda i, c_ref, p_ref: (i, 0))),
        out_shape=jax.ShapeDtypeStruct((R, C), F32),
        compiler_params=_params(("parallel",)),
    )(c_idx, p_idx, g4, recv1, recv2)


def _rs_add1_windows(name, g, recv, first_blocks):
    _, R, W = recv.shape
    nl = W // LANES

    def body(t_ref, *refs):
        r_ref, hb_ref = refs[nl], refs[nl + 1]
        for u in range(nl):
            cols = slice(u * LANES, (u + 1) * LANES)
            hb_ref[:, cols] = (refs[u][...] + r_ref[:, cols].astype(F32)).astype(BF16)

    blk = pl.BlockSpec((None, R, W), lambda p, t_ref: (p, 0, 0))
    return pl.pallas_call(
        body, name=name,
        grid_spec=pltpu.PrefetchScalarGridSpec(
            num_scalar_prefetch=1, grid=(4,),
            in_specs=[pl.BlockSpec((R, LANES), functools.partial(lambda u, p, t_ref: (0, t_ref[p] + u), u))
                      for u in range(nl)] + [blk],
            out_specs=blk),
        out_shape=jax.ShapeDtypeStruct((4, R, W), BF16),
        compiler_params=_params(("parallel",)),
    )(first_blocks, *([g] * nl), recv)


def _rs_add2_own_window(name, g, recv1, recv2, first_blocks, p_idx):
    _, R, W = recv1.shape
    nl = W // LANES

    def body(t_ref, p_ref, *refs):
        r1_ref, r2_ref, out_ref = refs[nl], refs[nl + 1], refs[nl + 2]
        for u in range(nl):
            cols = slice(u * LANES, (u + 1) * LANES)
            h = refs[u][...] + r1_ref[:, cols].astype(F32)
            out_ref[:, cols] = ((h + r2_ref[0, :, cols].astype(F32)) + r2_ref[1, :, cols].astype(F32)) \
                + r2_ref[2, :, cols].astype(F32)

    return pl.pallas_call(
        body, name=name,
        grid_spec=pltpu.PrefetchScalarGridSpec(
            num_scalar_prefetch=2, grid=(1,),
            in_specs=[pl.BlockSpec((R, LANES), functools.partial(lambda u, i, t, p: (0, t[p[0]] + u), u))
                      for u in range(nl)]
            + [pl.BlockSpec((None, R, W), lambda i, t, p: (p[0], 0, 0)), pl.BlockSpec((3, R, W), lambda i, t, p: (0, 0, 0))],
            out_specs=pl.BlockSpec((R, W), lambda i, t, p: (0, 0))),
        out_shape=jax.ShapeDtypeStruct((R, W), F32),
        compiler_params=_params(("arbitrary",)),
    )(first_blocks, p_idx, *([g] * nl), recv1, recv2)


def _add_windows(name, windows, first, second, n_blocks):
    _, R, W = windows.shape
    dev1 = jnp.asarray([d for d, _ in first], jnp.int32)
    blk1 = jnp.asarray([b for _, b in first], jnp.int32)
    dev2 = jnp.asarray([max(d, 0) for d, _ in second], jnp.int32)
    blk2 = jnp.asarray([b for _, b in second], jnp.int32)
    two = jnp.asarray([int(d >= 0) for d, _ in second], jnp.int32)

    G = 4
    assert n_blocks % G == 0

    def body(d1_ref, b1_ref, d2_ref, b2_ref, two_ref, *refs):
        out_ref = refs[2 * G]
        k = pl.program_id(0)
        for u in range(G):
            a_ref, b_ref = refs[u], refs[G + u]
            cols = slice(u * LANES, (u + 1) * LANES)

            @pl.when(two_ref[k * G + u] == 0)
            def _():
                out_ref[:, cols] = a_ref[...]

            @pl.when(two_ref[k * G + u] != 0)
            def _():
                out_ref[:, cols] = a_ref[...] + b_ref[...]

    def spec(u, second_owner):
        if second_owner:
            return pl.BlockSpec((None, R, LANES), lambda k, d1, b1, d2, b2, t: (d2[k * G + u], 0, b2[k * G + u]))
        return pl.BlockSpec((None, R, LANES), lambda k, d1, b1, d2, b2, t: (d1[k * G + u], 0, b1[k * G + u]))

    return pl.pallas_call(
        body, name=name,
        grid_spec=pltpu.PrefetchScalarGridSpec(
            num_scalar_prefetch=5, grid=(n_blocks // G,),
            in_specs=[spec(u, False) for u in range(G)] + [spec(u, True) for u in range(G)],
            out_specs=pl.BlockSpec((R, G * LANES), lambda k, d1, b1, d2, b2, t: (0, k))),
        out_shape=jax.ShapeDtypeStruct((R, n_blocks * LANES), windows.dtype),
        compiler_params=_params(("parallel",)),
    )(dev1, blk1, dev2, blk2, two, *([windows] * (2 * G)))


def _sum8(name, g):
    _, R, C = g.shape

    def body(g_ref, out_ref):
        acc = g_ref[0]
        for d in range(1, N_DEV):
            acc = acc + g_ref[d]
        out_ref[...] = acc

    return pl.pallas_call(body, name=name, out_shape=jax.ShapeDtypeStruct((R, C), F32),
                          compiler_params=_params())(g)


def _adamw_math(w, g, m, v):
    m = ADAM_B1 * m + (1.0 - ADAM_B1) * g
    v = ADAM_B2 * v + (1.0 - ADAM_B2) * (g * g)
    m_hat = m / (1.0 - ADAM_B1 ** ADAM_STEP)
    v_hat = v / (1.0 - ADAM_B2 ** ADAM_STEP)
    delta = -ADAM_LR * (m_hat / (jnp.sqrt(v_hat) + ADAM_EPS) + ADAM_WD * w)
    return delta, m, v


def _adamw(name, w, g, m, v):
    R, C = w.shape
    tr = _row_tile(R, C, target_bytes=1024 * 1024)
    return _row_call(name, lambda *a: (_adamw_math(*a), ()), [w, g, m, v], [], [(C, F32)] * 3, [], tr)


def _adamw_from_window(name, w, m, v, window, gate, where):
    R, C = w.shape
    W = window.shape[1]
    tr = _row_tile(R, C, target_bytes=1024 * 1024)

    def body(p_ref, w_ref, m_ref, v_ref, win_ref, gate_ref, g_out, d_out, m_out, v_out):
        off, nb, hg = p_ref[0], p_ref[1], p_ref[2]
        r, c = _iota2((W, C), 0), _iota2((W, C), 1)
        pick = jnp.logical_or(jnp.logical_and(c < nb, r == c + off),
                              jnp.logical_and(c >= nb + hg, r == c - hg + off)).astype(BF16)
        r2, c2 = _iota2((LANES, C), 0), _iota2((LANES, C), 1)
        pick_gate = jnp.logical_and(r2 < hg, c2 == nb + r2).astype(BF16)
        g = _dot3(win_ref[...], pick) + _dot3(gate_ref[...], pick_gate)
        g_out[...] = g
        d_out[...], m_out[...], v_out[...] = _adamw_math(w_ref[...], g, m_ref[...], v_ref[...])

    blk = pl.BlockSpec((tr, C), lambda i, p: (i, 0))
    return pl.pallas_call(
        body, name=name,
        grid_spec=pltpu.PrefetchScalarGridSpec(
            num_scalar_prefetch=1, grid=(R // tr,),
            in_specs=[blk, blk, blk, pl.BlockSpec((tr, W), lambda i, p: (i, 0)),
                      pl.BlockSpec((tr, LANES), lambda i, p: (i, 0))],
            out_specs=[blk] * 4),
        out_shape=[jax.ShapeDtypeStruct((R, C), F32)] * 4,
        compiler_params=_params(("parallel",)),
    )(where, w, m, v, window, gate)


def _adamw_many(name, ws, gs, ms, vs):
    n = len(ws)

    def body(*refs):
        ins, outs = refs[:4 * n], refs[4 * n:]
        for k in range(n):
            res = _adamw_math(ins[k][...], ins[n + k][...], ins[2 * n + k][...], ins[3 * n + k][...])
            for t in range(3):
                outs[t * n + k][...] = res[t]

    out = pl.pallas_call(
        body, name=name, out_shape=[jax.ShapeDtypeStruct(w.shape, F32) for _ in range(3) for w in ws],
        compiler_params=_params(),
    )(*ws, *gs, *ms, *vs)
    return out[:n], out[n:2 * n], out[2 * n:]


def _pack(parts):
    flat = []
    total = 0
    for a in parts:
        n = math.prod(a.shape)
        flat.append(a.reshape(-1).astype(F32))
        if n % LANES:
            flat.append(jnp.zeros((-n % LANES,), F32))
        total += n + (-n % LANES)
    if total % (8 * LANES):
        flat.append(jnp.zeros((-total % (8 * LANES),), F32))
    return jnp.concatenate(flat).reshape(-1, LANES)


def _unpack(packed, shapes):
    out = []
    r = 0
    for shp in shapes:
        n = math.prod(shp)
        nr = -(-n // LANES)
        out.append(packed[r:r + nr].reshape(-1)[:n].reshape(shp))
        r += nr
    return out


def kernel(x, norm_mix_g, w_in, b_f, gmlp_ln_g, gmlp_ln_b, w_s, b_s, attn_out_g, gmlp_out_g, w_out, norm_ffn_g, w_ff1, w_ff2, norm_final_g, loss_target, m_norm_mix_g, m_w_in, m_b_f, m_gmlp_ln_g, m_gmlp_ln_b, m_w_s, m_b_s, m_attn_out_g, m_gmlp_out_g, m_w_out, m_norm_ffn_g, m_w_ff1, m_w_ff2, m_norm_final_g, v_norm_mix_g, v_w_in, v_b_f, v_gmlp_ln_g, v_gmlp_ln_b, v_w_s, v_b_s, v_attn_out_g, v_gmlp_out_g, v_w_out, v_norm_ffn_g, v_w_ff1, v_w_ff2, v_norm_final_g):
    S, D = x.shape[1], x.shape[2]
    H = b_f.shape[1]
    DA = H * HEAD_DIM
    DG = gmlp_ln_g.shape[1]
    DQKV = 3 * DA
    DMAIN = DQKV + 2 * DG
    DIN = DMAIN + H
    DFF = w_ff1.shape[2] * N_DEV
    w_in_cols = w_in.shape[2]
    assert DIN == w_in_cols * N_DEV and DA == DG and D == DA + DG

    T_ATT = min(T_ATT_MAX, S)
    TR = min(TR_MAX, S)

    x0 = x[0]
    tgt = loss_target[0]
    g_final = norm_final_g.reshape(1, D)

    FB = DFF // N_DEV
    x_pos, y_pos, c_pos = _me()
    me_idx = 4 * x_pos + 2 * y_pos + c_pos

    WW = -(-(w_in_cols + LANES - 1) // LANES) * LANES
    to_main = lambda col: col if col <= DQKV else max(DQKV, col - H)
    lo = [to_main(n * w_in_cols) for n in range(N_DEV)]
    hi = [to_main((n + 1) * w_in_cols) for n in range(N_DEV)]
    starts = [v // LANES * LANES for v in lo]
    gate_dev = DQKV // w_in_cols
    n_before = DQKV - gate_dev * w_in_cols
    g0 = lo[gate_dev] - starts[gate_dev]
    stash = -(-(g0 + w_in_cols - H) // LANES) * LANES
    assert all(hi[n] <= starts[n] + WW <= DMAIN for n in range(N_DEV))
    assert gate_dev * w_in_cols <= DQKV and DQKV + H <= (gate_dev + 1) * w_in_cols and stash + LANES <= WW
    shard = w_in[0].astype(BF16)

    def my_window(n):
        if n != gate_dev:
            return lambda s: jnp.pad(s, ((0, 0), (lo[n] - starts[n], WW - w_in_cols - (lo[n] - starts[n]))))
        return lambda s: jnp.concatenate([
            jnp.zeros((D, g0), BF16), s[:, :n_before], s[:, n_before + H:],
            jnp.zeros((D, stash - g0 - (w_in_cols - H)), BF16), s[:, n_before:n_before + H],
            jnp.zeros((D, WW - stash - H), BF16)], axis=1)
    (windows_part,) = _run_jobs(
        "ag_w_in", [_job_gather_chips(lax.switch(me_idx, [my_window(n) for n in range(N_DEV)], shard))])[0]
    ((h,), _), ((windows,),) = _row_call(
        "rms_mix", lambda xb, g: ((_rms_fwd(xb, g),), ()), [x0], [norm_mix_g], [(D, BF16)], [], TR,
        jobs=[_job_gather_sibling(windows_part)])
    first, second = [], []
    for blk in range(DMAIN // LANES):
        c0 = blk * LANES
        owners = [(n, (c0 - starts[n]) // LANES) for n in range(N_DEV) if lo[n] < c0 + LANES and hi[n] > c0]
        assert 1 <= len(owners) <= 2
        first.append(owners[0])
        second.append(owners[1] if len(owners) == 2 else (-1, 0))
    w_main = _add_windows("w_in_windows", windows, first, second, DMAIN // LANES)
    w_f = windows[gate_dev, :, stash:stash + LANES]
    c_idx = jnp.reshape(c_pos, (1,)).astype(jnp.int32)
    p_idx = jnp.reshape(2 * x_pos + y_pos, (1,)).astype(jnp.int32)

    w_ff1_b = w_ff1[0].astype(BF16)
    (zm,), ((w_out_part,), (w_ff1_q1,)) = _mm_nn(
        "in_proj", h, w_main, [BF16], 2048, 1024, 2048,
        jobs=[_job_gather_chips(w_out[0].astype(BF16)), _job_gather_chips(w_ff1_b, part=(0, 1, 4))])
    (zf,) = _mm_nn("in_proj_f", h, w_f, [F32], 1024, LANES, 2048)
    bf_pad = jnp.pad(b_f, ((0, 0), (0, LANES - H)))
    f_row = _fgate_fwd(zf, bf_pad)
    NB = S // T_ATT
    f_row3 = f_row.reshape(H, NB, 1, T_ATT)
    (attn, lse_row3), ((w_out_all,), (w_ff1_part,)) = _attn2_fwd(
        zm, f_row3, T_ATT, jobs=[_job_gather_sibling(w_out_part),
                                 _job_gather_chips(w_ff1_b, part=(1, 4, 4), into=w_ff1_q1)])
    w_out_full = w_out_all.reshape(D, D)
    bs_col = b_s[0].reshape(H, CHUNK, 1)
    gm = _gmlp_fwd(zm, gmlp_ln_g, gmlp_ln_b, w_s[0], bs_col, TR)

    def merge_fn(a, g, ga, gg):
        return (jnp.concatenate([_rms_fwd(a, ga), _rms_fwd(g, gg)], axis=1),), ()
    (merged,), _ = _row_call("rms_merge", merge_fn, [attn, gm], [attn_out_g, gmlp_out_g], [(D, BF16)], [], TR)

    w_ff2_b = w_ff2[0].astype(BF16)
    TMR = min(TMR_MAX, S)

    def out_proj_fn(acc, res, g):
        xb = acc + res
        return (xb, _rms_fwd(xb, g)), ()
    ((x1, h2), _), ((w_ff1_all,), (w_ff2_q1,)) = _mm_rows(
        "out_proj", (S // TMR, 1), merged, pl.BlockSpec((TMR, D), lambda i, k: (i, 0)),
        w_out_full, pl.BlockSpec((D, D), lambda i, k: (0, 0)), NN, TMR, D, [x0], [norm_ffn_g],
        [(D, F32), (D, BF16)], [], out_proj_fn,
        jobs=[_job_gather_sibling(w_ff1_part), _job_gather_chips(w_ff2_b, part=(0, 1, 4))])

    tm, tn, tk = min(1024, S), min(1024, FB), min(2048, D)
    tm1 = min(2048, S)
    o_spec = pl.BlockSpec((tm1, tn), lambda i, j, k: (i, j))

    def relu_sq(acc):
        a = jnp.maximum(acc, 0.0)
        return a, a * a
    nj = FB // tn
    ff2_rest = [_job_gather_chips(w_ff2_b, part=(1, 4, 4), into=w_ff2_q1)]
    (a_act, a_sq), ((w_ff2_q2,),) = _mm(
        "ff1", (S // tm1, DFF // tn, D // tk), h2, pl.BlockSpec((tm1, tk), lambda i, j, k: (i, k)),
        w_ff1_all, pl.BlockSpec((None, tk, tn), lambda i, j, k: (j // nj, k, j % nj)), NN, (tm1, tn),
        [jax.ShapeDtypeStruct((S, DFF), BF16)] * 2, [o_spec] * 2, epilogue=relu_sq, jobs=ff2_rest)
    (w_ff2_all,) = _run_jobs("ag_w_ff2_sibling", [_job_gather_sibling(w_ff2_q2)])[0]
    w_ff2_full = w_ff2_all.reshape(DFF, D)
    def head_fn(acc, res, t, g):
        xb = acc + res
        rstd = lax.rsqrt(jnp.mean(xb * xb, axis=-1, keepdims=True) + EPS)
        xhat = xb * rstd
        err = xhat * g - t
        loss = 0.5 * jnp.sum(jnp.mean(err * err, axis=-1, keepdims=True), axis=0, keepdims=True)
        dy = err * (1.0 / D)
        dg = jnp.sum(dy * xhat, axis=0, keepdims=True)
        dxhat = dy * g
        dx = rstd * (dxhat - xhat * jnp.mean(dxhat * xhat, axis=-1, keepdims=True))
        return (dx, dx), (dg, jnp.broadcast_to(loss, (1, LANES)))
    tk_ff2 = min(1024, DFF)
    (dx2, dx2_b), (dg_final, loss_part) = _mm_rows(
        "ff2", (S // TMR, DFF // tk_ff2), a_sq, pl.BlockSpec((TMR, tk_ff2), lambda i, k: (i, k)),
        w_ff2_full, pl.BlockSpec((tk_ff2, D), lambda i, k: (k, 0)), NN, TMR, D, [x1, tgt], [g_final],
        [(D, F32), (D, BF16)], [D, LANES], head_fn)

    (da,) = _mm_nt("ff2_dx", dx2_b, w_ff2_full, [BF16], 2048, 1024, 2048, extras=[a_act],
                   epilogue=lambda acc, a: (2.0 * a.astype(F32) * acc,))
    dw_ff2, dw_ff2_b = _mm_tn("ff2_dw", a_sq, dx2_b, [F32, BF16], 1024, 2048, 1024)
    tm2, tk2 = min(2048, D), min(1024, S)
    dw1_spec = pl.BlockSpec((None, tm2, FB), lambda i, j, k: (j, i, 0))
    (dw_ff1, dw_ff1_b), ((r1_ff2,),) = _mm(
        "ff1_dw", (D // tm2, DFF // FB, S // tk2), h2, pl.BlockSpec((tk2, tm2), lambda i, j, k: (k, i)),
        da, pl.BlockSpec((tk2, FB), lambda i, j, k: (k, j)), TN, (tm2, FB),
        [jax.ShapeDtypeStruct((N_DEV, D, FB), F32), jax.ShapeDtypeStruct((N_DEV, D, FB), BF16)], [dw1_spec] * 2,
        epilogue=lambda acc: (acc, acc), jobs=[_job_scatter_sibling(dw_ff2_b.reshape(4, 2, FB, D))])
    hb_ff2 = _rs_add1("rs_add1_w_ff2", dw_ff2.reshape(4, 2, FB, D), r1_ff2, c_idx)
    def ffn_bwd_fn(dh, xb, dres, g):
        dx, dg = _rms_bwd(dh, xb, g)
        dx = dx + dres
        return (dx, dx), (dg,)
    tkb = min(1024, FB)
    nkb = FB // tkb
    ((dx1, dx1_b), (dg_ffn,)), ((r2_ff2_a,), (r1_ff1,)) = _mm_rows(
        "ff1_dx", (S // TMR, DFF // tkb), da, pl.BlockSpec((TMR, tkb), lambda i, k: (i, k)),
        w_ff1_all, pl.BlockSpec((None, D, tkb), lambda i, k: (k // nkb, 0, k % nkb)), NT, TMR, D, [x1, dx2],
        [norm_ffn_g], [(D, F32), (D, BF16)], [D], ffn_bwd_fn,
        jobs=[_job_scatter_chips(hb_ff2, part=(0, 6, 8)), _job_scatter_sibling(dw_ff1_b.reshape(4, 2, D, FB))])
    hb_ff1 = _rs_add1("rs_add1_w_ff1", dw_ff1.reshape(4, 2, D, FB), r1_ff1, c_idx)

    def merge_bwd_fn(dm, a, g, ga, gg):
        da_, dga = _rms_bwd(dm[:, :DA], a, ga)
        dg_, dgg = _rms_bwd(dm[:, DA:], g, gg)
        return (da_, dg_), (dga, dgg)
    (dattn, dgm), (dg_attn, dg_gmlp) = _mm_rows(
        "out_proj_dx", (S // TMR, 1), dx1_b, pl.BlockSpec((TMR, D), lambda i, k: (i, 0)),
        w_out_full, pl.BlockSpec((D, D), lambda i, k: (0, 0)), NT, TMR, D, [attn, gm], [attn_out_g, gmlp_out_g],
        [(DA, F32), (DG, F32)], [DA, DG], merge_bwd_fn)
    (dw_out, dw_out_b), ((r2_ff2,),) = _mm_tn(
        "out_proj_dw", merged, dx1_b, [F32, BF16], 2048, 1024, 1024,
        jobs=[_job_scatter_chips(hb_ff2, part=(6, 8, 8), into=r2_ff2_a)])
    g_w_ff2 = _rs_add2_own("rs_add2_w_ff2", dw_ff2.reshape(4, 2, FB, D), r1_ff2, r2_ff2, c_idx, p_idx)

    w_st = jnp.swapaxes(w_s[0], 1, 2)
    dzu, dzv, dw_s, dbs_col, dln_g, dln_b = _gmlp_bwd(dgm, zm, gmlp_ln_g, gmlp_ln_b, w_s[0], w_st, bs_col, TR)

    delta_row3 = _attn_delta(dattn, attn, TR).reshape(H, NB, 1, T_ATT)
    (dq, ds_rowsum), ((r2_ff1_a,), (r1_out,)) = _attn2_bwd_dq(
        zm, dattn, f_row3, lse_row3, delta_row3, T_ATT,
        jobs=[_job_scatter_chips(hb_ff1, part=(0, 5, 8)),
              _job_scatter_sibling(dw_out_b.reshape(4, 2, D // N_DEV, D))])
    hb_out = _rs_add1("rs_add1_w_out", dw_out.reshape(4, 2, D // N_DEV, D), r1_out, c_idx)
    (dk, dv, df_row3), ((r2_ff1,), (r2_out,)) = _attn2_bwd_dkv(
        zm, dattn, f_row3, lse_row3, delta_row3, ds_rowsum, T_ATT,
        jobs=[_job_scatter_chips(hb_ff1, part=(5, 8, 8), into=r2_ff1_a), _job_scatter_chips(hb_out)])
    g_w_ff1 = _rs_add2_own("rs_add2_w_ff1", dw_ff1.reshape(4, 2, D, FB), r1_ff1, r2_ff1, c_idx, p_idx)
    g_w_out = _rs_add2_own("rs_add2_w_out", dw_out.reshape(4, 2, D // N_DEV, D), r1_out, r2_out, c_idx, p_idx)
    dzf, dbf = _fgate_bwd(df_row3.reshape(H, S), zf, bf_pad)

    dz_main = jnp.concatenate([dq, dk, dv, dzu, dzv], axis=1)
    dw_main, dw_main_b = _mm_tn("in_proj_dw", h, dz_main, [F32, BF16], 2048, 1024, 1024)
    (dw_f,), ((r1_in,),) = _mm_tn("in_proj_f_dw", h, dzf, [F32], 2048, LANES, 1024,
                                  jobs=[_job_scatter_sibling_windows(dw_main_b, starts, WW)])
    first_blocks = jnp.stack([jnp.where(c_pos == 0, starts[2 * p], starts[2 * p + 1]) // LANES
                              for p in range(4)]).astype(jnp.int32)
    hb_in = _rs_add1_windows("rs_add1_w_in", dw_main, r1_in, first_blocks)

    def mix_bwd_fn(dh_main, dz_gate, xb, dres, g, w_gate):
        dx, dg = _rms_bwd(dh_main + _dot(dz_gate, w_gate, NT), xb, g)
        return (dx + dres,), (dg,)
    tk_in = min(1024, DMAIN)
    ((grad_x,), (dg_mix,)), ((r2_in,),) = _mm_rows(
        "in_proj_dx", (S // TMR, DMAIN // tk_in), dz_main, pl.BlockSpec((TMR, tk_in), lambda i, k: (i, k)),
        w_main, pl.BlockSpec((D, tk_in), lambda i, k: (0, k)), NT, TMR, D, [dzf, x0, dx1], [norm_mix_g, w_f],
        [(D, F32)], [D], mix_bwd_fn, jobs=[_job_scatter_chips(hb_in)])
    g_window = _rs_add2_own_window("rs_add2_w_in", dw_main, r1_in, r2_in, first_blocks, p_idx)

    small_shapes = [norm_mix_g.shape, b_f.shape, gmlp_ln_g.shape, gmlp_ln_b.shape, w_s.shape, b_s.shape,
                    attn_out_g.shape, gmlp_out_g.shape, norm_ffn_g.shape, norm_final_g.shape]
    small_parts = [dg_mix, dbf[:, :H], dln_g, dln_b, dw_s, dbs_col, dg_attn, dg_gmlp, dg_ffn, dg_final]
    g_small = _sum8("small_sum", _all_gather("ag_small", _pack(small_parts + [dw_f[:, :H], loss_part])))
    *gs, g_gate, loss_sum = _unpack(g_small, small_shapes + [(D, H), (1, LANES)])
    two_d = lambda a: a.reshape(1, -1) if a.ndim == 1 else a
    ds, nms, nvs = _adamw_many(
        "adamw_small",
        [two_d(a) for a in (norm_mix_g, b_f, gmlp_ln_g, gmlp_ln_b, w_s, b_s, attn_out_g, gmlp_out_g, norm_ffn_g,
                            norm_final_g)],
        [two_d(a) for a in gs],
        [two_d(a) for a in (m_norm_mix_g, m_b_f, m_gmlp_ln_g, m_gmlp_ln_b, m_w_s, m_b_s, m_attn_out_g, m_gmlp_out_g,
                            m_norm_ffn_g, m_norm_final_g)],
        [two_d(a) for a in (v_norm_mix_g, v_b_f, v_gmlp_ln_g, v_gmlp_ln_b, v_w_s, v_b_s, v_attn_out_g, v_gmlp_out_g,
                            v_norm_ffn_g, v_norm_final_g)])
    ds, nms, nvs = [[a.reshape(s) for a, s in zip(lst, small_shapes)] for lst in (ds, nms, nvs)]

    is_gate_dev = me_idx == gate_dev
    where = jnp.stack([sum(jnp.where(me_idx == n, lo[n] - starts[n], 0) for n in range(N_DEV)),
                       jnp.where(is_gate_dev, n_before, w_in_cols), jnp.where(is_gate_dev, H, 0)]).astype(jnp.int32)
    big = {"w_in": tuple(a[None] for a in _adamw_from_window(
        "adamw_w_in", w_in[0], m_w_in[0], v_w_in[0], g_window, jnp.pad(g_gate, ((0, 0), (0, LANES - H))), where))}
    for nm, w, g, m, v in (("w_out", w_out, g_w_out, m_w_out, v_w_out),
                           ("w_ff1", w_ff1, g_w_ff1, m_w_ff1, v_w_ff1), ("w_ff2", w_ff2, g_w_ff2, m_w_ff2, v_w_ff2)):
        (d_, m_, v_), _ = _adamw("adamw_" + nm, w[0], g, m[0], v[0])
        big[nm] = (g[None], d_[None], m_[None], v_[None])

    loss = loss_sum[0, 0]

    def leaves(n):
        sm = (gs, ds, nms, nvs)[n]
        return [sm[0], big["w_in"][n], sm[1], sm[2], sm[3], sm[4], sm[5], sm[6], sm[7], big["w_out"][n], sm[8],
                big["w_ff1"][n], big["w_ff2"][n], sm[9]]

    return (loss, grad_x[None], *leaves(0), *leaves(1), *leaves(2), *leaves(3))
```

```python
import functools
import math

import jax
import jax.numpy as jnp
from jax import lax
from jax.experimental import pallas as pl
from jax.experimental.pallas import tpu as pltpu

F32 = jnp.float32
BF16 = jnp.bfloat16
MESH = pl.DeviceIdType.MESH

HEAD_DIM = 128
CHUNK = 128
EPS = 1e-6
LANES = 128
N_DEV = 8

ADAM_LR = 0.001
ADAM_B1 = 0.9
ADAM_B2 = 0.999
ADAM_EPS = 1e-08
ADAM_WD = 0.01
ADAM_STEP = 10

VMEM_LIMIT_BYTES = 56 * 1024 * 1024
T_ATT_MAX = 1024
TR_MAX = 512
TMR_MAX = 512

NN = ((1,), (0,))
NT = ((1,), (1,))
TN = ((0,), (0,))


def _params(sem=None):
    return pltpu.CompilerParams(dimension_semantics=sem, vmem_limit_bytes=VMEM_LIMIT_BYTES)


def _dot(a, b, contract=NN):
    return lax.dot_general(a, b, (contract, ((), ())), preferred_element_type=F32)


def _dot3(x, t):
    x1 = x.astype(BF16)
    r1 = x - x1.astype(F32)
    x2 = r1.astype(BF16)
    x3 = (r1 - x2.astype(F32)).astype(BF16)
    return _dot(x1, t) + _dot(x2, t) + _dot(x3, t)


def _iota2(shape, dim):
    return lax.broadcasted_iota(jnp.int32, shape, dim)


def _row_call(name, fn, row_ins, bcast_ins, row_outs, acc_outs, tr, jobs=()):
    S = row_ins[0].shape[0]
    assert S % tr == 0
    n_ri, n_bi, n_ro, n_ao = len(row_ins), len(bcast_ins), len(row_outs), len(acc_outs)

    def body(*refs):
        ins = [r[...] for r in refs[:n_ri + n_bi]]
        ro_refs = refs[n_ri + n_bi:n_ri + n_bi + n_ro]
        ao_refs = refs[n_ri + n_bi + n_ro:]
        ro, ao = fn(*ins)
        for r, v in zip(ro_refs, ro):
            r[...] = v.astype(r.dtype)
        if n_ao:
            @pl.when(pl.program_id(0) == 0)
            def _():
                for r in ao_refs:
                    r[...] = jnp.zeros_like(r)
            for r, v in zip(ao_refs, ao):
                r[...] += v

    in_specs = [pl.BlockSpec((tr, a.shape[1]), lambda i: (i, 0)) for a in row_ins]
    in_specs += [pl.BlockSpec(a.shape, lambda i: (0, 0)) for a in bcast_ins]
    out_specs = [pl.BlockSpec((tr, d), lambda i: (i, 0)) for d, _ in row_outs]
    out_specs += [pl.BlockSpec((1, d), lambda i: (0, 0)) for d in acc_outs]
    out_shape = [jax.ShapeDtypeStruct((S, d), dt) for d, dt in row_outs]
    out_shape += [jax.ShapeDtypeStruct((1, d), F32) for d in acc_outs]
    outs, job_res = _carry_call(
        body, name=name, grid=(S // tr,), in_specs=in_specs, out_specs=out_specs, out_shape=out_shape,
        scratch_shapes=[], semantics=("arbitrary",) if n_ao else ("parallel",), args=list(row_ins) + list(bcast_ins),
        jobs=jobs)
    res = (outs[:n_ro], outs[n_ro:])
    return (res, job_res) if jobs else res


def _rms_fwd(x, g):
    rstd = lax.rsqrt(jnp.mean(x * x, axis=-1, keepdims=True) + EPS)
    return x * rstd * g


def _rms_bwd(dy, x, g):
    rstd = lax.rsqrt(jnp.mean(x * x, axis=-1, keepdims=True) + EPS)
    xhat = x * rstd
    dg = jnp.sum(dy * xhat, axis=0, keepdims=True)
    dxhat = dy * g
    dx = rstd * (dxhat - xhat * jnp.mean(dxhat * xhat, axis=-1, keepdims=True))
    return dx, dg


_GELU_C = math.sqrt(2.0 / math.pi)


def _gelu(x):
    return 0.5 * x * (1.0 + jnp.tanh(_GELU_C * (x + 0.044715 * (x * x * x))))


def _gelu_grad(x):
    t = jnp.tanh(_GELU_C * (x + 0.044715 * (x * x * x)))
    return 0.5 * (1.0 + t) + 0.5 * x * (1.0 - t * t) * (_GELU_C * (1.0 + 3.0 * 0.044715 * (x * x)))


def _me():
    return lax.axis_index("x"), lax.axis_index("y"), lax.axis_index("c")


def _other_chips(x, y):
    return [(1 - x, y), (x, 1 - y), (1 - x, 1 - y)]


_ANY = pl.BlockSpec(memory_space=pl.ANY)


class _Job:
    def __init__(self, ins, outs, n_sems, make, aliases=None):
        self.ins, self.outs, self.n_sems, self.make, self.aliases = ins, outs, n_sems, make, aliases or {}


def _job_gather_chips(blk, part=(0, 1, 1), into=None):
    R, C = blk.shape
    nr = R // part[2]
    rows = pl.ds(part[0] * nr, (part[1] - part[0]) * nr)

    def make(ins, outs, send_sems, recv_sems, base):
        x_ref, (out_ref,) = ins[0], outs
        x, y, c = _me()
        mine = 4 * x + 2 * y + c
        targets = [(x, y, 1 - c)] + [(cx, cy, c) for cx, cy in _other_chips(x, y)]

        def copy(k, slab, to):
            return pltpu.make_async_remote_copy(
                src_ref=x_ref.at[rows, :], dst_ref=out_ref.at[slab, rows, :], send_sem=send_sems.at[base + k],
                recv_sem=recv_sems.at[base + k], device_id=to, device_id_type=MESH)

        starts = [copy(k, mine, to) for k, to in enumerate(targets)]
        arrivals = [copy(k, 4 * tx + 2 * ty + tc, (tx, ty, tc)) for k, (tx, ty, tc) in enumerate(targets)]
        local = [pltpu.make_async_copy(x_ref.at[rows, :], out_ref.at[mine, rows, :], send_sems.at[base + 4])]
        return starts, arrivals, local

    out = jax.ShapeDtypeStruct((N_DEV, R, C), blk.dtype)
    if into is None:
        return _Job([blk], [out], 5, make)
    return _Job([blk, into], [out], 5, make, aliases={1: 0})


def _job_gather_sibling(part):
    def make(ins, outs, send_sems, recv_sems, base):
        (out_ref,) = outs
        x, y, c = _me()

        def copy(k, slab):
            return pltpu.make_async_remote_copy(
                src_ref=out_ref.at[slab], dst_ref=out_ref.at[slab], send_sem=send_sems.at[base + k],
                recv_sem=recv_sems.at[base + k], device_id=(x, y, 1 - c), device_id_type=MESH)

        chips = _other_chips(x, y)
        starts = [copy(k, 4 * cx + 2 * cy + c) for k, (cx, cy) in enumerate(chips)]
        arrivals = [copy(k, 4 * cx + 2 * cy + (1 - c)) for k, (cx, cy) in enumerate(chips)]
        return starts, arrivals, []

    return _Job([part], [jax.ShapeDtypeStruct(part.shape, part.dtype)], 3, make, aliases={0: 0})


def _job_scatter_sibling(gb):
    _, _, R, C = gb.shape

    def make(ins, outs, send_sems, recv_sems, base):
        (g_ref,), (recv_ref,) = ins, outs
        x, y, c = _me()
        copies = [pltpu.make_async_remote_copy(
            src_ref=g_ref.at[p, 1 - c], dst_ref=recv_ref.at[p], send_sem=send_sems.at[base + p],
            recv_sem=recv_sems.at[base + p], device_id=(x, y, 1 - c), device_id_type=MESH) for p in range(4)]
        return copies, copies, []

    return _Job([gb], [jax.ShapeDtypeStruct((4, R, C), gb.dtype)], 4, make)


def _job_scatter_sibling_windows(gb, starts, width):
    R, _ = gb.shape

    def make(ins, outs, send_sems, recv_sems, base):
        (g_ref,), (recv_ref,) = ins, outs
        x, y, c = _me()
        copies = []
        for p in range(4):
            start = pl.multiple_of(jnp.where(c == 0, starts[2 * p + 1], starts[2 * p]), LANES)
            copies.append(pltpu.make_async_remote_copy(
                src_ref=g_ref.at[:, pl.ds(start, width)], dst_ref=recv_ref.at[p], send_sem=send_sems.at[base + p],
                recv_sem=recv_sems.at[base + p], device_id=(x, y, 1 - c), device_id_type=MESH))
        return copies, copies, []

    return _Job([gb], [jax.ShapeDtypeStruct((4, R, width), gb.dtype)], 4, make)


def _job_scatter_chips(hb, part=(0, 1, 1), into=None):
    _, R, C = hb.shape
    nr = R // part[2]
    rows = pl.ds(part[0] * nr, (part[1] - part[0]) * nr)

    def make(ins, outs, send_sems, recv_sems, base):
        h_ref, (recv_ref,) = ins[0], outs
        x, y, c = _me()
        copies = [pltpu.make_async_remote_copy(
            src_ref=h_ref.at[2 * cx + cy, rows, :], dst_ref=recv_ref.at[n, rows, :], send_sem=send_sems.at[base + n],
            recv_sem=recv_sems.at[base + n], device_id=(cx, cy, c), device_id_type=MESH)
            for n, (cx, cy) in enumerate(_other_chips(x, y))]
        return copies, copies, []

    out = jax.ShapeDtypeStruct((3, R, C), hb.dtype)
    if into is None:
        return _Job([hb], [out], 3, make)
    return _Job([hb, into], [out], 3, make, aliases={1: 0})


def _carry_call(body, *, name, grid, in_specs, out_specs, out_shape, scratch_shapes, semantics, args, jobs=()):
    jobs = list(jobs)
    n_in, n_out, n_scr = len(in_specs), len(out_specs), len(scratch_shapes)
    j_ins = [a for j in jobs for a in j.ins]
    j_outs = [o for j in jobs for o in j.outs]
    n_sems = sum(j.n_sems for j in jobs)
    aliases = {}
    i0, o0 = n_in, n_out
    for j in jobs:
        for a, b in j.aliases.items():
            aliases[i0 + a] = o0 + b
        i0 += len(j.ins)
        o0 += len(j.outs)

    def full_body(*refs):
        ins = refs[:n_in]
        jin = refs[n_in:n_in + len(j_ins)]
        outs = refs[n_in + len(j_ins):n_in + len(j_ins) + n_out]
        jout = refs[n_in + len(j_ins) + n_out:n_in + len(j_ins) + n_out + len(j_outs)]
        scr = refs[n_in + len(j_ins) + n_out + len(j_outs):]
        if jobs:
            send_sems, recv_sems = scr[n_scr], scr[n_scr + 1]
            starts, arrivals, local = [], [], []
            base = i0 = o0 = 0
            for j in jobs:
                s, a, l = j.make(jin[i0:i0 + len(j.ins)], jout[o0:o0 + len(j.outs)], send_sems, recv_sems, base)
                starts += s
                arrivals += a
                local += l
                base += j.n_sems
                i0 += len(j.ins)
                o0 += len(j.outs)
            pids = [pl.program_id(d) for d in range(len(grid))]
            first = functools.reduce(jnp.logical_and, [p == 0 for p in pids])
            last = functools.reduce(jnp.logical_and, [p == n - 1 for p, n in zip(pids, grid)])

            @pl.when(first)
            def _():
                for cp in local + starts:
                    cp.start()

        body(*ins, *outs, *scr[:n_scr])

        if jobs:
            @pl.when(last)
            def _():
                for cp in arrivals:
                    cp.wait_recv()
                for cp in starts:
                    cp.wait_send()
                for cp in local:
                    cp.wait()

    sems = [pltpu.SemaphoreType.DMA((n_sems,)), pltpu.SemaphoreType.DMA((n_sems,))] if jobs else []
    res = pl.pallas_call(
        full_body, name=name, grid=grid,
        in_specs=list(in_specs) + [_ANY] * len(j_ins),
        out_specs=list(out_specs) + [_ANY] * len(j_outs),
        out_shape=list(out_shape) + j_outs,
        scratch_shapes=list(scratch_shapes) + sems,
        input_output_aliases=aliases,
        compiler_params=_params(("arbitrary",) * len(grid) if jobs else semantics),
    )(*args, *j_ins)
    body_res, job_res = res[:n_out], res[n_out:]
    per_job = []
    for j in jobs:
        per_job.append(job_res[:len(j.outs)])
        job_res = job_res[len(j.outs):]
    return body_res, per_job


def _run_jobs(name, jobs):
    def body(done_ref):
        done_ref[...] = jnp.zeros_like(done_ref)

    return _carry_call(body, name=name, grid=(1,), in_specs=[], out_specs=[pl.BlockSpec((8, LANES), lambda i: (0, 0))],
                       out_shape=[jax.ShapeDtypeStruct((8, LANES), F32)], scratch_shapes=[], semantics=("arbitrary",),
                       args=[], jobs=jobs)[1]


def _mm(name, grid, a, a_spec, b, b_spec, contract, acc_shape, out_shape, out_specs, extras=(), epilogue=None, jobs=()):
    nk = grid[2]
    n_e = len(extras)
    n_o = len(out_shape)
    if epilogue is None:
        epilogue = lambda acc: (acc,)

    def body(a_ref, b_ref, *rest):
        e_refs = rest[:n_e]
        o_refs = rest[n_e:n_e + n_o]

        def finish(total):
            res = epilogue(total, *[r[...] for r in e_refs])
            for o, r in zip(o_refs, res):
                o[...] = r.astype(o.dtype)

        if nk == 1:
            finish(_dot(a_ref[...], b_ref[...], contract))
            return
        acc = rest[n_e + n_o]
        k = pl.program_id(2)

        @pl.when(k == 0)
        def _():
            acc[...] = _dot(a_ref[...], b_ref[...], contract)

        @pl.when(jnp.logical_and(k > 0, k < nk - 1))
        def _():
            acc[...] += _dot(a_ref[...], b_ref[...], contract)

        @pl.when(k == nk - 1)
        def _():
            finish(acc[...] + _dot(a_ref[...], b_ref[...], contract))

    outs, job_res = _carry_call(
        body, name=name, grid=grid, in_specs=[a_spec, b_spec] + [s for _, s in extras],
        out_specs=list(out_specs), out_shape=list(out_shape),
        scratch_shapes=[pltpu.VMEM(acc_shape, F32)] if nk > 1 else [],
        semantics=("parallel", "parallel", "arbitrary"), args=[a, b] + [e for e, _ in extras], jobs=jobs)
    return (outs, job_res) if jobs else outs


def _mm_rows(name, grid, a, a_spec, b, b_spec, contract, tm, n, row_extras, bcast, row_outs, acc_outs, epilogue, jobs=()):
    nk = grid[1]
    M = grid[0] * tm
    n_x, n_b, n_ro, n_ao = len(row_extras), len(bcast), len(row_outs), len(acc_outs)

    def body(a_ref, b_ref, *rest):
        x_refs = rest[:n_x + n_b]
        rest = rest[n_x + n_b:]
        ro_refs = rest[:n_ro]
        ao_refs = rest[n_ro:n_ro + n_ao]
        i = pl.program_id(0)

        def finish(total):
            ro, ao = epilogue(total, *[r[...] for r in x_refs])
            for r, v in zip(ro_refs, ro):
                r[...] = v.astype(r.dtype)
            if n_ao:
                @pl.when(i == 0)
                def _():
                    for r, v in zip(ao_refs, ao):
                        r[...] = v

                @pl.when(i > 0)
                def _():
                    for r, v in zip(ao_refs, ao):
                        r[...] += v

        if nk == 1:
            finish(_dot(a_ref[...], b_ref[...], contract))
            return
        acc = rest[n_ro + n_ao]
        k = pl.program_id(1)

        @pl.when(k == 0)
        def _():
            acc[...] = _dot(a_ref[...], b_ref[...], contract)

        @pl.when(jnp.logical_and(k > 0, k < nk - 1))
        def _():
            acc[...] += _dot(a_ref[...], b_ref[...], contract)

        @pl.when(k == nk - 1)
        def _():
            finish(acc[...] + _dot(a_ref[...], b_ref[...], contract))

    in_specs = [a_spec, b_spec] + [pl.BlockSpec((tm, x.shape[1]), lambda i, k: (i, 0)) for x in row_extras]
    in_specs += [pl.BlockSpec(x.shape, lambda i, k: (0,) * x.ndim) for x in bcast]
    out_specs = [pl.BlockSpec((tm, w), lambda i, k: (i, 0)) for w, _ in row_outs]
    out_specs += [pl.BlockSpec((1, w), lambda i, k: (0, 0)) for w in acc_outs]
    out_shape = [jax.ShapeDtypeStruct((M, w), dt) for w, dt in row_outs]
    out_shape += [jax.ShapeDtypeStruct((1, w), F32) for w in acc_outs]
    outs, job_res = _carry_call(
        body, name=name, grid=grid, in_specs=in_specs, out_specs=out_specs, out_shape=out_shape,
        scratch_shapes=[pltpu.VMEM((tm, n), F32)] if nk > 1 else [],
        semantics=("arbitrary", "arbitrary"), args=[a, b] + list(row_extras) + list(bcast), jobs=jobs)
    res = (outs[:n_ro], outs[n_ro:])
    return (res, job_res) if jobs else res


def _mm_nn(name, a, b, out_dtypes, tm, tn, tk, extras=(), epilogue=None, jobs=()):
    M, K = a.shape
    N = b.shape[1]
    tm, tn, tk = min(tm, M), min(tn, N), min(tk, K)
    o_spec = pl.BlockSpec((tm, tn), lambda i, j, k: (i, j))
    return _mm(name, (M // tm, N // tn, K // tk),
               a, pl.BlockSpec((tm, tk), lambda i, j, k: (i, k)),
               b, pl.BlockSpec((tk, tn), lambda i, j, k: (k, j)), NN, (tm, tn),
               [jax.ShapeDtypeStruct((M, N), dt) for dt in out_dtypes], [o_spec] * len(out_dtypes),
               [(e, o_spec) for e in extras], epilogue, jobs)


def _mm_nt(name, a, b, out_dtypes, tm, tn, tk, extras=(), epilogue=None, jobs=()):
    M, K = a.shape
    N = b.shape[0]
    tm, tn, tk = min(tm, M), min(tn, N), min(tk, K)
    o_spec = pl.BlockSpec((tm, tn), lambda i, j, k: (i, j))
    return _mm(name, (M // tm, N // tn, K // tk),
               a, pl.BlockSpec((tm, tk), lambda i, j, k: (i, k)),
               b, pl.BlockSpec((tn, tk), lambda i, j, k: (j, k)), NT, (tm, tn),
               [jax.ShapeDtypeStruct((M, N), dt) for dt in out_dtypes], [o_spec] * len(out_dtypes),
               [(e, o_spec) for e in extras], epilogue, jobs)


def _mm_tn(name, a, b, out_dtypes, tm, tn, tk, jobs=()):
    K, M = a.shape
    N = b.shape[1]
    tm, tn, tk = min(tm, M), min(tn, N), min(tk, K)
    o_spec = pl.BlockSpec((tm, tn), lambda i, j, k: (i, j))
    return _mm(name, (M // tm, N // tn, K // tk),
               a, pl.BlockSpec((tk, tm), lambda i, j, k: (k, i)),
               b, pl.BlockSpec((tk, tn), lambda i, j, k: (k, j)), TN, (tm, tn),
               [jax.ShapeDtypeStruct((M, N), dt) for dt in out_dtypes], [o_spec] * len(out_dtypes),
               epilogue=lambda acc: (acc,) * len(out_dtypes), jobs=jobs)


def _fgate_fwd(zf, bf):
    S = zf.shape[0]
    nc = S // CHUNK

    def body(zf_ref, bf_ref, f_ref):
        upper = (_iota2((CHUNK, CHUNK), 0) <= _iota2((CHUNK, CHUNK), 1)).astype(BF16)
        carry = jnp.zeros((8, 1), F32)
        for c in range(nc):
            t = zf_ref[c * CHUNK:(c + 1) * CHUNK, :] + bf_ref[...]
            lf = jnp.minimum(t, 0.0) - jnp.log(1.0 + jnp.exp(-jnp.abs(t)))
            lf_rows = lf.T[0:8, :]
            f_ref[:, c * CHUNK:(c + 1) * CHUNK] = (_dot3(lf_rows, upper) + carry) * LOG2E
            carry = carry + jnp.sum(lf_rows, axis=-1, keepdims=True)

    return pl.pallas_call(
        body, name="fgate_fwd", out_shape=jax.ShapeDtypeStruct((8, S), F32),
        compiler_params=_params(),
    )(zf, bf)


def _fgate_bwd(df, zf, bf):
    S = zf.shape[0]
    nc = S // CHUNK

    def body(df_ref, zf_ref, bf_ref, dzf_ref, dbf_ref):
        lower = (_iota2((CHUNK, CHUNK), 0) >= _iota2((CHUNK, CHUNK), 1)).astype(BF16)
        carry = jnp.zeros((8, 1), F32)
        dbf = jnp.zeros((1, LANES), F32)
        for c in reversed(range(nc)):
            sl = slice(c * CHUNK, (c + 1) * CHUNK)
            df = df_ref[:, sl]
            r = _dot3(df, lower) + carry
            carry = carry + jnp.sum(df, axis=-1, keepdims=True)
            r_cols = jnp.concatenate([r, jnp.zeros((CHUNK - 8, CHUNK), F32)], axis=0).T
            t = zf_ref[sl, :] + bf_ref[...]
            dz = r_cols * (1.0 / (1.0 + jnp.exp(t)))
            dzf_ref[sl, :] = dz.astype(BF16)
            dbf = dbf + jnp.sum(dz, axis=0, keepdims=True)
        dbf_ref[...] = dbf

    return pl.pallas_call(
        body, name="fgate_bwd",
        out_shape=[jax.ShapeDtypeStruct((S, LANES), BF16), jax.ShapeDtypeStruct((1, LANES), F32)],
        compiler_params=_params(),
    )(df, zf, bf)


_NEG = -1e30
LOG2E = 1.4426950408889634
N_SPLIT = 8
N_SPLIT_DIAG = 4
DIAG_STEP = 1024


def _attn_consts(T):
    rows, cols = _iota2((T, T), 0), _iota2((T, T), 1)
    return cols <= rows, rows <= cols


def _col_to_row(col):
    wide = jnp.broadcast_to(col, (col.shape[0], LANES))
    return jnp.concatenate([wide[r:r + LANES, :].T[0:1, :] for r in range(0, col.shape[0], LANES)], axis=1)


def _row_to_col(row):
    tall = jnp.broadcast_to(row, (LANES, row.shape[1]))
    return jnp.concatenate([tall[:, c:c + LANES].T[:, 0:1] for c in range(0, row.shape[1], LANES)], axis=0)


def _attn2_fwd(zm, f2row, T, jobs=()):
    S = zm.shape[0]
    H = f2row.shape[0]
    nb = S // T
    c2 = LOG2E / math.sqrt(HEAD_DIM)

    def body(q_ref, k_ref, v_ref, fk_ref, o_ref, lse_ref, vaug_s, fq_ref):
        i = pl.program_id(1)

        @pl.when(i == 0)
        def _():
            vaug_s[:, :HEAD_DIM] = v_ref[...]
            vaug_s[:, HEAD_DIM:] = jnp.ones((S, HEAD_DIM), BF16)

        fq_ref[...] = _row_to_col(fk_ref[i])
        keep = _attn_consts(T)[0]
        TH = T // N_SPLIT

        def block(j, diagonal, state):
            r0 = pl.multiple_of(j * T, T)
            fk = fk_ref[j]
            new = []
            for g, (m_old, acc) in enumerate(state):
                rows = slice(g * TH, (g + 1) * TH)
                nk = min(T, -(-(g + 1) * TH // DIAG_STEP) * DIAG_STEP) if diagonal else T
                s = _dot(q_ref[rows, :], k_ref[pl.ds(r0, nk), :], NT) * c2 + (fq_ref[rows, :] - fk[:, :nk])
                if diagonal:
                    s = jnp.where(keep[rows, :nk], s, _NEG)
                m_new = jnp.maximum(m_old, jnp.max(s, axis=-1, keepdims=True))
                p = jnp.exp2(s - m_new).astype(BF16)
                new.append((m_new, jnp.exp2(m_old - m_new) * acc + _dot(p, vaug_s[pl.ds(r0, nk), :])))
            return tuple(new)

        init = tuple((jnp.full((TH, 1), _NEG, F32), jnp.zeros((TH, 2 * HEAD_DIM), F32)) for _ in range(N_SPLIT))
        state = lax.fori_loop(0, i, lambda j, st: block(j, False, st), init)
        state = block(i, True, state)
        for g, (m, acc) in enumerate(state):
            rows = slice(g * TH, (g + 1) * TH)
            o_ref[rows, :] = acc[:, :HEAD_DIM] / acc[:, HEAD_DIM:]
            lse_ref[:, rows] = _col_to_row(m + jnp.log2(acc[:, HEAD_DIM:HEAD_DIM + 1]))

    nh = H
    return _carry_call(
        body, name="attn_fwd", grid=(H, nb), jobs=jobs, args=[zm, zm, zm, f2row],
        semantics=("arbitrary", "arbitrary"),
        in_specs=[
            pl.BlockSpec((T, HEAD_DIM), lambda h, i: (i, h)),
            pl.BlockSpec((S, HEAD_DIM), lambda h, i: (0, nh + h)),
            pl.BlockSpec((S, HEAD_DIM), lambda h, i: (0, 2 * nh + h)),
            pl.BlockSpec((None, nb, 1, T), lambda h, i: (h, 0, 0, 0)),
        ],
        out_specs=[pl.BlockSpec((T, HEAD_DIM), lambda h, i: (i, h)),
                   pl.BlockSpec((None, None, 1, T), lambda h, i: (h, i, 0, 0))],
        out_shape=[jax.ShapeDtypeStruct((S, H * HEAD_DIM), F32), jax.ShapeDtypeStruct((H, nb, 1, T), F32)],
        scratch_shapes=[pltpu.VMEM((S, 2 * HEAD_DIM), BF16), pltpu.VMEM((T, 1), F32)],
    )


def _attn2_bwd_dq(zm, dattn, f2row, lse2_row, delta_row, T, jobs=()):
    S = zm.shape[0]
    H = f2row.shape[0]
    nb = S // T
    scale = 1.0 / math.sqrt(HEAD_DIM)
    c2 = LOG2E * scale

    def body(q_ref, k_ref, v_ref, do_ref, fk_ref, lse_ref, dlr_ref, dq_ref, rs_ref, bias_s, do_s, dl_ref):
        i = pl.program_id(1)
        keep = _attn_consts(T)[0]
        TH = T // N_SPLIT_DIAG
        bias_s[...] = _row_to_col(fk_ref[i] - lse_ref[...])
        dl_ref[...] = _row_to_col(dlr_ref[...])
        do_s[...] = do_ref[...].astype(BF16)

        def part(rows, j, nk, state, masked):
            acc, rs = state
            r0 = pl.multiple_of(j * T, T)
            kb = k_ref[pl.ds(r0, nk), :]
            s = _dot(q_ref[rows, :], kb, NT) * c2 + (bias_s[rows, :] - fk_ref[j][:, :nk])
            if masked:
                s = jnp.where(keep[rows, :nk], s, _NEG)
            ds = jnp.exp2(s) * (_dot(do_s[rows, :], v_ref[pl.ds(r0, nk), :], NT) - dl_ref[rows, :])
            return acc + _dot(ds.astype(BF16), kb), rs + jnp.sum(ds, axis=-1, keepdims=True)

        def step(j, state):
            return part(slice(0, T), j, T, state, False)

        acc, rs = lax.fori_loop(0, i, step, (jnp.zeros((T, HEAD_DIM), F32), jnp.zeros((T, 1), F32)))
        for g in range(N_SPLIT_DIAG):
            rows = slice(g * TH, (g + 1) * TH)
            acc_g, rs_g = part(rows, i, (g + 1) * TH, (acc[rows, :], rs[rows, :]), True)
            dq_ref[rows, :] = (acc_g * scale).astype(BF16)
            rs_ref[:, rows] = _col_to_row(rs_g)

    nh = H
    row = pl.BlockSpec((None, None, 1, T), lambda h, i: (h, i, 0, 0))
    blk = pl.BlockSpec((T, HEAD_DIM), lambda h, i: (i, h))
    return _carry_call(
        body, name="attn_bwd_dq", grid=(H, nb), jobs=jobs,
        args=[zm, zm, zm, dattn, f2row, lse2_row, delta_row], semantics=("arbitrary", "arbitrary"),
        in_specs=[
            blk,
            pl.BlockSpec((S, HEAD_DIM), lambda h, i: (0, nh + h)),
            pl.BlockSpec((S, HEAD_DIM), lambda h, i: (0, 2 * nh + h)),
            blk,
            pl.BlockSpec((None, nb, 1, T), lambda h, i: (h, 0, 0, 0)),
            row, row,
        ],
        out_specs=[blk, row],
        out_shape=[jax.ShapeDtypeStruct((S, H * HEAD_DIM), BF16), jax.ShapeDtypeStruct((H, nb, 1, T), F32)],
        scratch_shapes=[pltpu.VMEM((T, 1), F32), pltpu.VMEM((T, HEAD_DIM), BF16), pltpu.VMEM((T, 1), F32)],
    )


def _attn2_bwd_dkv(zm, dattn, f2row, lse2_row, delta_row, rowsum_row, T, jobs=()):
    S = zm.shape[0]
    H = f2row.shape[0]
    nb = S // T
    scale = 1.0 / math.sqrt(HEAD_DIM)
    c2 = LOG2E * scale

    def body(q_ref, k_ref, v_ref, do_ref, fq_ref, lse_ref, dl_ref, rs_ref, dk_ref, dv_ref, df_ref, fk_ref):
        j = pl.program_id(1)
        keep = _attn_consts(T)[1]
        TH = T // N_SPLIT_DIAG
        fk_ref[...] = _row_to_col(fq_ref[j])

        def part(rows, i, c0, state, masked):
            dk, dv, df = state
            r0 = pl.multiple_of(i * T + c0, TH)
            qb = q_ref[pl.ds(r0, T - c0), :]
            do = do_ref[pl.ds(r0, T - c0), :].astype(BF16)
            bias = (fq_ref[i] - lse_ref[i])[:, c0:]
            dl = (dl_ref[i] + rs_ref[i])[:, c0:]
            st = _dot(k_ref[rows, :], qb, NT) * c2 + (bias - fk_ref[rows, :])
            if masked:
                st = jnp.where(keep[rows, c0:], st, _NEG)
            pt = jnp.exp2(st)
            dst = pt * (_dot(v_ref[rows, :], do, NT) - dl)
            return (dk + _dot(dst.astype(BF16), qb), dv + _dot(pt.astype(BF16), do),
                    df - jnp.sum(dst, axis=-1, keepdims=True))

        groups = []
        for g in range(N_SPLIT_DIAG):
            zero = (jnp.zeros((TH, HEAD_DIM), F32), jnp.zeros((TH, HEAD_DIM), F32), jnp.zeros((TH, 1), F32))
            groups.append(part(slice(g * TH, (g + 1) * TH), j, g * TH, zero, True))
        state = tuple(jnp.concatenate([grp[n] for grp in groups], axis=0) for n in range(3))
        dk, dv, df = lax.fori_loop(j + 1, nb, lambda i, st: part(slice(0, T), i, 0, st, False), state)
        dk_ref[...] = (dk * scale).astype(BF16)
        dv_ref[...] = dv.astype(BF16)
        df_ref[...] = _col_to_row(df)

    nh = H
    row = pl.BlockSpec((None, nb, 1, T), lambda h, j: (h, 0, 0, 0))
    whole = pl.BlockSpec((S, HEAD_DIM), lambda h, j: (0, h))
    kv_out = pl.BlockSpec((T, HEAD_DIM), lambda h, j: (j, h))
    return _carry_call(
        body, name="attn_bwd_dkv", grid=(H, nb), jobs=jobs,
        args=[zm, zm, zm, dattn, f2row, lse2_row, delta_row, rowsum_row],
        semantics=("arbitrary", "arbitrary"),
        in_specs=[
            whole,
            pl.BlockSpec((T, HEAD_DIM), lambda h, j: (j, nh + h)),
            pl.BlockSpec((T, HEAD_DIM), lambda h, j: (j, 2 * nh + h)),
            whole, row, row, row, row,
        ],
        out_specs=[kv_out, kv_out, pl.BlockSpec((None, None, 1, T), lambda h, j: (h, j, 0, 0))],
        out_shape=[jax.ShapeDtypeStruct((S, H * HEAD_DIM), BF16), jax.ShapeDtypeStruct((S, H * HEAD_DIM), BF16),
                   jax.ShapeDtypeStruct((H, nb, 1, T), F32)],
        scratch_shapes=[pltpu.VMEM((T, 1), F32)],
    )


def _attn_delta(dattn, attn, tr):
    S, DA = attn.shape
    H = DA // HEAD_DIM

    def body(do_ref, o_ref, out_ref):
        lo = _iota2((DA, LANES), 1) * HEAD_DIM
        sel = ((_iota2((DA, LANES), 0) >= lo) & (_iota2((DA, LANES), 0) < lo + HEAD_DIM)).astype(BF16)
        d = _dot3(do_ref[...] * o_ref[...], sel)
        for c in range(tr // CHUNK):
            out_ref[:, c * CHUNK:(c + 1) * CHUNK] = d[c * CHUNK:(c + 1) * CHUNK, :].T[0:H, :]

    return pl.pallas_call(
        body, name="attn_delta", grid=(S // tr,),
        in_specs=[pl.BlockSpec((tr, DA), lambda i: (i, 0))] * 2,
        out_specs=pl.BlockSpec((H, tr), lambda i: (0, i)),
        out_shape=jax.ShapeDtypeStruct((H, S), F32),
        compiler_params=_params(("parallel",)),
    )(dattn, attn)


def _ln_stats(x):
    mu = jnp.mean(x, axis=-1, keepdims=True)
    xc = x - mu
    rstd = lax.rsqrt(jnp.mean(xc * xc, axis=-1, keepdims=True) + EPS)
    return xc * rstd, rstd


def _tril_mask():
    return _iota2((CHUNK, CHUNK), 0) >= _iota2((CHUNK, CHUNK), 1)


def _gmlp_fwd(zm, ln_g, ln_b, w_s, bs_col, attn, attn_g, gm_g, tr):
    S = zm.shape[0]
    H = w_s.shape[0]
    DG = H * HEAD_DIM
    DA = attn.shape[1]

    def body(zu_ref, zv_ref, g_ref, b_ref, w_ref, bs_ref, a_ref, ag_ref, gg_ref, out_ref, merged_ref):
        u = _gelu(zu_ref[...].astype(F32))
        y, _ = _ln_stats(_gelu(zv_ref[...].astype(F32)))
        v = (y * g_ref[...] + b_ref[...]).astype(BF16)
        mask = _tril_mask()
        for h in range(H):
            wc = jnp.where(mask, w_ref[h], 0.0).astype(BF16)
            cs = slice(h * HEAD_DIM, (h + 1) * HEAD_DIM)
            for c in range(tr // CHUNK):
                rs = slice(c * CHUNK, (c + 1) * CHUNK)
                mix = _dot(wc, v[rs, cs]) + bs_ref[h]
                out_ref[rs, cs] = u[rs, cs] * mix
        merged_ref[:, :DA] = _rms_fwd(a_ref[...], ag_ref[...]).astype(BF16)
        merged_ref[:, DA:] = _rms_fwd(out_ref[...], gg_ref[...]).astype(BF16)

    full = lambda a: pl.BlockSpec(a.shape, lambda i: (0,) * a.ndim)
    return pl.pallas_call(
        body, name="gmlp_fwd", grid=(S // tr,),
        in_specs=[pl.BlockSpec((tr, DG), lambda i: (i, 3)), pl.BlockSpec((tr, DG), lambda i: (i, 4)),
                  full(ln_g), full(ln_b), full(w_s), full(bs_col),
                  pl.BlockSpec((tr, DA), lambda i: (i, 0)), full(attn_g), full(gm_g)],
        out_specs=[pl.BlockSpec((tr, DG), lambda i: (i, 0)), pl.BlockSpec((tr, DA + DG), lambda i: (i, 0))],
        out_shape=[jax.ShapeDtypeStruct((S, DG), F32), jax.ShapeDtypeStruct((S, DA + DG), BF16)],
        compiler_params=_params(("parallel",)),
    )(zm, zm, ln_g, ln_b, w_s, bs_col, attn, attn_g, gm_g)


def _gmlp_bwd(dgm, zm, ln_g, ln_b, w_s, w_st, bs_col, tr):
    S = zm.shape[0]
    H = w_s.shape[0]
    DG = H * HEAD_DIM

    def body(dg_ref, zu_ref, zv_ref, g_ref, b_ref, w_ref, wt_ref, bs_ref,
             dzu_ref, dzv_ref, dw_ref, dbs_ref, dlg_ref, dlb_ref, dv_s):
        @pl.when(pl.program_id(0) == 0)
        def _():
            dw_ref[...] = jnp.zeros_like(dw_ref)
            dbs_ref[...] = jnp.zeros_like(dbs_ref)
            dlg_ref[...] = jnp.zeros_like(dlg_ref)
            dlb_ref[...] = jnp.zeros_like(dlb_ref)

        zu = zu_ref[...].astype(F32)
        zv = zv_ref[...].astype(F32)
        u = _gelu(zu)
        y, rstd = _ln_stats(_gelu(zv))
        v = (y * g_ref[...] + b_ref[...]).astype(BF16)
        dgm_blk = dg_ref[...]
        mask = _tril_mask()
        mask_t = _iota2((CHUNK, CHUNK), 0) <= _iota2((CHUNK, CHUNK), 1)
        for h in range(H):
            wc = jnp.where(mask, w_ref[h], 0.0).astype(BF16)
            wct = jnp.where(mask_t, wt_ref[h], 0.0).astype(BF16)
            cs = slice(h * HEAD_DIM, (h + 1) * HEAD_DIM)
            dw = jnp.zeros((CHUNK, CHUNK), F32)
            dbs = jnp.zeros((CHUNK, 1), F32)
            for c in range(tr // CHUNK):
                rs = slice(c * CHUNK, (c + 1) * CHUNK)
                vch = v[rs, cs]
                mix = _dot(wc, vch) + bs_ref[h]
                dg = dgm_blk[rs, cs]
                dzu_ref[rs, cs] = (dg * mix * _gelu_grad(zu[rs, cs])).astype(BF16)
                dmix = dg * u[rs, cs]
                dbs = dbs + jnp.sum(dmix, axis=-1, keepdims=True)
                dmix_b = dmix.astype(BF16)
                dw = dw + _dot(dmix_b, vch, NT)
                dv_s[rs, cs] = _dot(wct, dmix_b)
            dw_ref[h] += jnp.where(mask, dw, 0.0)
            dbs_ref[h] += dbs
        dv = dv_s[...]
        dlg_ref[...] += jnp.sum(dv * y, axis=0, keepdims=True)
        dlb_ref[...] += jnp.sum(dv, axis=0, keepdims=True)
        dy = dv * g_ref[...]
        dgv = rstd * (dy - jnp.mean(dy, axis=-1, keepdims=True) - y * jnp.mean(dy * y, axis=-1, keepdims=True))
        dzv_ref[...] = (dgv * _gelu_grad(zv)).astype(BF16)

    full = lambda a: pl.BlockSpec(a.shape, lambda i: (0,) * a.ndim)
    rows = pl.BlockSpec((tr, DG), lambda i: (i, 0))
    return pl.pallas_call(
        body, name="gmlp_bwd", grid=(S // tr,),
        in_specs=[rows, pl.BlockSpec((tr, DG), lambda i: (i, 3)), pl.BlockSpec((tr, DG), lambda i: (i, 4)),
                  full(ln_g), full(ln_b), full(w_s), full(w_st), full(bs_col)],
        out_specs=[rows, rows, full(w_s), full(bs_col), full(ln_g), full(ln_b)],
        out_shape=[jax.ShapeDtypeStruct((S, DG), BF16), jax.ShapeDtypeStruct((S, DG), BF16),
                   jax.ShapeDtypeStruct(w_s.shape, F32), jax.ShapeDtypeStruct(bs_col.shape, F32),
                   jax.ShapeDtypeStruct(ln_g.shape, F32), jax.ShapeDtypeStruct(ln_b.shape, F32)],
        scratch_shapes=[pltpu.VMEM((tr, DG), F32)],
        compiler_params=_params(("arbitrary",)),
    )(dgm, zm, zm, ln_g, ln_b, w_s, w_st, bs_col)


def _all_gather(name, blk):
    R, C = blk.shape

    def body(x_ref, out_ref, send_sems, recv_sems, local_sem):
        x, y, c = _me()
        me, sibling = (x, y, c), (x, y, 1 - c)
        chips = [(1 - x, y), (x, 1 - y), (1 - x, 1 - y)]

        def slab(px, py, pc):
            return out_ref.at[4 * px + 2 * py + pc]

        def copy(k, block, to, src=None):
            return pltpu.make_async_remote_copy(
                src_ref=slab(*block) if src is None else src, dst_ref=slab(*block),
                send_sem=send_sems.at[k], recv_sem=recv_sems.at[k], device_id=to, device_id_type=MESH)

        mine = pltpu.make_async_copy(x_ref, slab(*me), local_sem)
        mine.start()
        first = [copy(0, me, sibling, src=x_ref)]
        first += [copy(1 + n, me, (*chip, c), src=x_ref) for n, chip in enumerate(chips)]
        for cp in first:
            cp.start()
        passed = [copy(4 + n, (*chip, c), sibling) for n, chip in enumerate(chips)]
        for n, chip in enumerate(chips):
            copy(1 + n, (*chip, c), me).wait_recv()
            passed[n].start()
        copy(0, sibling, me).wait_recv()
        for n, chip in enumerate(chips):
            copy(4 + n, (*chip, 1 - c), me).wait_recv()
        for cp in first + passed:
            cp.wait_send()
        mine.wait()

    return pl.pallas_call(
        body, name=name, out_shape=jax.ShapeDtypeStruct((N_DEV, R, C), blk.dtype),
        in_specs=[_ANY], out_specs=_ANY,
        scratch_shapes=[pltpu.SemaphoreType.DMA((7,)), pltpu.SemaphoreType.DMA((7,)), pltpu.SemaphoreType.DMA(())],
    )(blk)


def _row_tile(R, C, itemsize=4, target_bytes=2 * 1024 * 1024):
    tr = R
    while tr % 2 == 0 and tr * C * itemsize > target_bytes and (tr // 2) % 16 == 0:
        tr //= 2
    return tr


def _rs_add1(name, g4, recv, c_idx):
    _, _, R, C = g4.shape
    tr = _row_tile(R, C)

    def body(c_ref, g_ref, r_ref, hb_ref):
        hb_ref[...] = (g_ref[...] + r_ref[...].astype(F32)).astype(BF16)

    blk = pl.BlockSpec((None, tr, C), lambda p, i, c_ref: (p, i, 0))
    return pl.pallas_call(
        body, name=name,
        grid_spec=pltpu.PrefetchScalarGridSpec(
            num_scalar_prefetch=1, grid=(4, R // tr),
            in_specs=[pl.BlockSpec((None, None, tr, C), lambda p, i, c_ref: (p, c_ref[0], i, 0)), blk],
            out_specs=blk),
        out_shape=jax.ShapeDtypeStruct((4, R, C), BF16),
        compiler_params=_params(("parallel", "parallel")),
    )(c_idx, g4, recv)


def _rs_add2_own(name, g4, recv1, recv2, c_idx, p_idx):
    _, _, R, C = g4.shape
    tr = _row_tile(R, C)

    def body(c_ref, p_ref, g_ref, r1_ref, r2_ref, out_ref):
        h = g_ref[...] + r1_ref[...].astype(F32)
        out_ref[...] = ((h + r2_ref[0].astype(F32)) + r2_ref[1].astype(F32)) + r2_ref[2].astype(F32)

    return pl.pallas_call(
        body, name=name,
        grid_spec=pltpu.PrefetchScalarGridSpec(
            num_scalar_prefetch=2, grid=(R // tr,),
            in_specs=[pl.BlockSpec((None, None, tr, C), lambda i, c_ref, p_ref: (p_ref[0], c_ref[0], i, 0)),
                      pl.BlockSpec((None, tr, C), lambda i, c_ref, p_ref: (p_ref[0], i, 0)),
                      pl.BlockSpec((3, tr, C), lambda i, c_ref, p_ref: (0, i, 0))],
            out_specs=pl.BlockSpec((tr, C), lambda i, c_ref, p_ref: (i, 0))),
        out_shape=jax.ShapeDtypeStruct((R, C), F32),
        compiler_params=_params(("parallel",)),
    )(c_idx, p_idx, g4, recv1, recv2)


def _rs_add1_windows(name, g, recv, first_blocks):
    _, R, W = recv.shape
    nl = W // LANES

    def body(t_ref, *refs):
        r_ref, hb_ref = refs[nl], refs[nl + 1]
        for u in range(nl):
            cols = slice(u * LANES, (u + 1) * LANES)
            hb_ref[:, cols] = (refs[u][...] + r_ref[:, cols].astype(F32)).astype(BF16)

    blk = pl.BlockSpec((None, R, W), lambda p, t_ref: (p, 0, 0))
    return pl.pallas_call(
        body, name=name,
        grid_spec=pltpu.PrefetchScalarGridSpec(
            num_scalar_prefetch=1, grid=(4,),
            in_specs=[pl.BlockSpec((R, LANES), functools.partial(lambda u, p, t_ref: (0, t_ref[p] + u), u))
                      for u in range(nl)] + [blk],
            out_specs=blk),
        out_shape=jax.ShapeDtypeStruct((4, R, W), BF16),
        compiler_params=_params(("parallel",)),
    )(first_blocks, *([g] * nl), recv)


def _rs_add2_own_window(name, g, recv1, recv2, first_blocks, p_idx):
    _, R, W = recv1.shape
    nl = W // LANES

    def body(t_ref, p_ref, *refs):
        r1_ref, r2_ref, out_ref = refs[nl], refs[nl + 1], refs[nl + 2]
        for u in range(nl):
            cols = slice(u * LANES, (u + 1) * LANES)
            h = refs[u][...] + r1_ref[:, cols].astype(F32)
            out_ref[:, cols] = ((h + r2_ref[0, :, cols].astype(F32)) + r2_ref[1, :, cols].astype(F32)) \
                + r2_ref[2, :, cols].astype(F32)

    return pl.pallas_call(
        body, name=name,
        grid_spec=pltpu.PrefetchScalarGridSpec(
            num_scalar_prefetch=2, grid=(1,),
            in_specs=[pl.BlockSpec((R, LANES), functools.partial(lambda u, i, t, p: (0, t[p[0]] + u), u))
                      for u in range(nl)]
            + [pl.BlockSpec((None, R, W), lambda i, t, p: (p[0], 0, 0)), pl.BlockSpec((3, R, W), lambda i, t, p: (0, 0, 0))],
            out_specs=pl.BlockSpec((R, W), lambda i, t, p: (0, 0))),
        out_shape=jax.ShapeDtypeStruct((R, W), F32),
        compiler_params=_params(("arbitrary",)),
    )(first_blocks, p_idx, *([g] * nl), recv1, recv2)


def _add_windows(name, windows, first, second, n_blocks):
    _, R, W = windows.shape
    dev1 = jnp.asarray([d for d, _ in first], jnp.int32)
    blk1 = jnp.asarray([b for _, b in first], jnp.int32)
    dev2 = jnp.asarray([max(d, 0) for d, _ in second], jnp.int32)
    blk2 = jnp.asarray([b for _, b in second], jnp.int32)
    two = jnp.asarray([int(d >= 0) for d, _ in second], jnp.int32)

    G = 4
    assert n_blocks % G == 0

    def body(d1_ref, b1_ref, d2_ref, b2_ref, two_ref, *refs):
        out_ref = refs[2 * G]
        k = pl.program_id(0)
        for u in range(G):
            a_ref, b_ref = refs[u], refs[G + u]
            cols = slice(u * LANES, (u + 1) * LANES)

            @pl.when(two_ref[k * G + u] == 0)
            def _():
                out_ref[:, cols] = a_ref[...]

            @pl.when(two_ref[k * G + u] != 0)
            def _():
                out_ref[:, cols] = a_ref[...] + b_ref[...]

    def spec(u, second_owner):
        if second_owner:
            return pl.BlockSpec((None, R, LANES), lambda k, d1, b1, d2, b2, t: (d2[k * G + u], 0, b2[k * G + u]))
        return pl.BlockSpec((None, R, LANES), lambda k, d1, b1, d2, b2, t: (d1[k * G + u], 0, b1[k * G + u]))

    return pl.pallas_call(
        body, name=name,
        grid_spec=pltpu.PrefetchScalarGridSpec(
            num_scalar_prefetch=5, grid=(n_blocks // G,),
            in_specs=[spec(u, False) for u in range(G)] + [spec(u, True) for u in range(G)],
            out_specs=pl.BlockSpec((R, G * LANES), lambda k, d1, b1, d2, b2, t: (0, k))),
        out_shape=jax.ShapeDtypeStruct((R, n_blocks * LANES), windows.dtype),
        compiler_params=_params(("parallel",)),
    )(dev1, blk1, dev2, blk2, two, *([windows] * (2 * G)))


def _sum8(name, g):
    _, R, C = g.shape

    def body(g_ref, out_ref):
        acc = g_ref[0]
        for d in range(1, N_DEV):
            acc = acc + g_ref[d]
        out_ref[...] = acc

    return pl.pallas_call(body, name=name, out_shape=jax.ShapeDtypeStruct((R, C), F32),
                          compiler_params=_params())(g)


def _adamw_math(w, g, m, v):
    m = ADAM_B1 * m + (1.0 - ADAM_B1) * g
    v = ADAM_B2 * v + (1.0 - ADAM_B2) * (g * g)
    m_hat = m / (1.0 - ADAM_B1 ** ADAM_STEP)
    v_hat = v / (1.0 - ADAM_B2 ** ADAM_STEP)
    delta = -ADAM_LR * (m_hat / (jnp.sqrt(v_hat) + ADAM_EPS) + ADAM_WD * w)
    return delta, m, v


def _adamw(name, w, g, m, v):
    R, C = w.shape
    tr = _row_tile(R, C, target_bytes=1024 * 1024)
    return _row_call(name, lambda *a: (_adamw_math(*a), ()), [w, g, m, v], [], [(C, F32)] * 3, [], tr)


def _adamw_from_window(name, w, m, v, window, gate, where):
    R, C = w.shape
    W = window.shape[1]
    tr = _row_tile(R, C, target_bytes=1024 * 1024)

    def body(p_ref, w_ref, m_ref, v_ref, win_ref, gate_ref, g_out, d_out, m_out, v_out):
        off, nb, hg = p_ref[0], p_ref[1], p_ref[2]
        r, c = _iota2((W, C), 0), _iota2((W, C), 1)
        pick = jnp.logical_or(jnp.logical_and(c < nb, r == c + off),
                              jnp.logical_and(c >= nb + hg, r == c - hg + off)).astype(BF16)
        r2, c2 = _iota2((LANES, C), 0), _iota2((LANES, C), 1)
        pick_gate = jnp.logical_and(r2 < hg, c2 == nb + r2).astype(BF16)
        g = _dot3(win_ref[...], pick) + _dot3(gate_ref[...], pick_gate)
        g_out[...] = g
        d_out[...], m_out[...], v_out[...] = _adamw_math(w_ref[...], g, m_ref[...], v_ref[...])

    blk = pl.BlockSpec((tr, C), lambda i, p: (i, 0))
    return pl.pallas_call(
        body, name=name,
        grid_spec=pltpu.PrefetchScalarGridSpec(
            num_scalar_prefetch=1, grid=(R // tr,),
            in_specs=[blk, blk, blk, pl.BlockSpec((tr, W), lambda i, p: (i, 0)),
                      pl.BlockSpec((tr, LANES), lambda i, p: (i, 0))],
            out_specs=[blk] * 4),
        out_shape=[jax.ShapeDtypeStruct((R, C), F32)] * 4,
        compiler_params=_params(("parallel",)),
    )(where, w, m, v, window, gate)


def _adamw_many(name, ws, gs, ms, vs):
    n = len(ws)

    def body(*refs):
        ins, outs = refs[:4 * n], refs[4 * n:]
        for k in range(n):
            res = _adamw_math(ins[k][...], ins[n + k][...], ins[2 * n + k][...], ins[3 * n + k][...])
            for t in range(3):
                outs[t * n + k][...] = res[t]

    out = pl.pallas_call(
        body, name=name, out_shape=[jax.ShapeDtypeStruct(w.shape, F32) for _ in range(3) for w in ws],
        compiler_params=_params(),
    )(*ws, *gs, *ms, *vs)
    return out[:n], out[n:2 * n], out[2 * n:]


def _pack(parts):
    flat = []
    total = 0
    for a in parts:
        n = math.prod(a.shape)
        flat.append(a.reshape(-1).astype(F32))
        if n % LANES:
            flat.append(jnp.zeros((-n % LANES,), F32))
        total += n + (-n % LANES)
    if total % (8 * LANES):
        flat.append(jnp.zeros((-total % (8 * LANES),), F32))
    return jnp.concatenate(flat).reshape(-1, LANES)


def _unpack(packed, shapes):
    out = []
    r = 0
    for shp in shapes:
        n = math.prod(shp)
        nr = -(-n // LANES)
        out.append(packed[r:r + nr].reshape(-1)[:n].reshape(shp))
        r += nr
    return out


def kernel(x, norm_mix_g, w_in, b_f, gmlp_ln_g, gmlp_ln_b, w_s, b_s, attn_out_g, gmlp_out_g, w_out, norm_ffn_g, w_ff1, w_ff2, norm_final_g, loss_target, m_norm_mix_g, m_w_in, m_b_f, m_gmlp_ln_g, m_gmlp_ln_b, m_w_s, m_b_s, m_attn_out_g, m_gmlp_out_g, m_w_out, m_norm_ffn_g, m_w_ff1, m_w_ff2, m_norm_final_g, v_norm_mix_g, v_w_in, v_b_f, v_gmlp_ln_g, v_gmlp_ln_b, v_w_s, v_b_s, v_attn_out_g, v_gmlp_out_g, v_w_out, v_norm_ffn_g, v_w_ff1, v_w_ff2, v_norm_final_g):
    S, D = x.shape[1], x.shape[2]
    H = b_f.shape[1]
    DA = H * HEAD_DIM
    DG = gmlp_ln_g.shape[1]
    DQKV = 3 * DA
    DMAIN = DQKV + 2 * DG
    DIN = DMAIN + H
    DFF = w_ff1.shape[2] * N_DEV
    w_in_cols = w_in.shape[2]
    assert DIN == w_in_cols * N_DEV and DA == DG and D == DA + DG

    T_ATT = min(T_ATT_MAX, S)
    TR = min(TR_MAX, S)

    x0 = x[0]
    tgt = loss_target[0]
    g_final = norm_final_g.reshape(1, D)

    FB = DFF // N_DEV
    x_pos, y_pos, c_pos = _me()
    me_idx = 4 * x_pos + 2 * y_pos + c_pos

    WW = -(-(w_in_cols + LANES - 1) // LANES) * LANES
    to_main = lambda col: col if col <= DQKV else max(DQKV, col - H)
    lo = [to_main(n * w_in_cols) for n in range(N_DEV)]
    hi = [to_main((n + 1) * w_in_cols) for n in range(N_DEV)]
    starts = [v // LANES * LANES for v in lo]
    gate_dev = DQKV // w_in_cols
    n_before = DQKV - gate_dev * w_in_cols
    g0 = lo[gate_dev] - starts[gate_dev]
    stash = -(-(g0 + w_in_cols - H) // LANES) * LANES
    assert all(hi[n] <= starts[n] + WW <= DMAIN for n in range(N_DEV))
    assert gate_dev * w_in_cols <= DQKV and DQKV + H <= (gate_dev + 1) * w_in_cols and stash + LANES <= WW
    shard = w_in[0].astype(BF16)

    def my_window(n):
        if n != gate_dev:
            return lambda s: jnp.pad(s, ((0, 0), (lo[n] - starts[n], WW - w_in_cols - (lo[n] - starts[n]))))
        return lambda s: jnp.concatenate([
            jnp.zeros((D, g0), BF16), s[:, :n_before], s[:, n_before + H:],
            jnp.zeros((D, stash - g0 - (w_in_cols - H)), BF16), s[:, n_before:n_before + H],
            jnp.zeros((D, WW - stash - H), BF16)], axis=1)
    (windows_part,) = _run_jobs(
        "ag_w_in", [_job_gather_chips(lax.switch(me_idx, [my_window(n) for n in range(N_DEV)], shard))])[0]
    ((h,), _), ((windows,),) = _row_call(
        "rms_mix", lambda xb, g: ((_rms_fwd(xb, g),), ()), [x0], [norm_mix_g], [(D, BF16)], [], TR,
        jobs=[_job_gather_sibling(windows_part)])
    first, second = [], []
    for blk in range(DMAIN // LANES):
        c0 = blk * LANES
        owners = [(n, (c0 - starts[n]) // LANES) for n in range(N_DEV) if lo[n] < c0 + LANES and hi[n] > c0]
        assert 1 <= len(owners) <= 2
        first.append(owners[0])
        second.append(owners[1] if len(owners) == 2 else (-1, 0))
    w_main = _add_windows("w_in_windows", windows, first, second, DMAIN // LANES)
    w_f = windows[gate_dev, :, stash:stash + LANES]
    c_idx = jnp.reshape(c_pos, (1,)).astype(jnp.int32)
    p_idx = jnp.reshape(2 * x_pos + y_pos, (1,)).astype(jnp.int32)

    w_ff1_b = w_ff1[0].astype(BF16)
    (zm,), ((w_out_part,), (w_ff1_q1,)) = _mm_nn(
        "in_proj", h, w_main, [BF16], 2048, 1024, 2048,
        jobs=[_job_gather_chips(w_out[0].astype(BF16)), _job_gather_chips(w_ff1_b, part=(0, 1, 4))])
    (zf,) = _mm_nn("in_proj_f", h, w_f, [F32], 1024, LANES, 2048)
    bf_pad = jnp.pad(b_f, ((0, 0), (0, LANES - H)))
    f_row = _fgate_fwd(zf, bf_pad)
    NB = S // T_ATT
    f_row3 = f_row.reshape(H, NB, 1, T_ATT)
    (attn, lse_row3), ((w_out_all,), (w_ff1_part,)) = _attn2_fwd(
        zm, f_row3, T_ATT, jobs=[_job_gather_sibling(w_out_part),
                                 _job_gather_chips(w_ff1_b, part=(1, 4, 4), into=w_ff1_q1)])
    w_out_full = w_out_all.reshape(D, D)
    bs_col = b_s[0].reshape(H, CHUNK, 1)
    gm, merged = _gmlp_fwd(zm, gmlp_ln_g, gmlp_ln_b, w_s[0], bs_col, attn, attn_out_g, gmlp_out_g, TR)

    w_ff2_b = w_ff2[0].astype(BF16)
    TMR = min(TMR_MAX, S)

    def out_proj_fn(acc, res, g):
        xb = acc + res
        return (xb, _rms_fwd(xb, g)), ()
    ((x1, h2), _), ((w_ff1_all,), (w_ff2_q1,)) = _mm_rows(
        "out_proj", (S // TMR, 1), merged, pl.BlockSpec((TMR, D), lambda i, k: (i, 0)),
        w_out_full, pl.BlockSpec((D, D), lambda i, k: (0, 0)), NN, TMR, D, [x0], [norm_ffn_g],
        [(D, F32), (D, BF16)], [], out_proj_fn,
        jobs=[_job_gather_sibling(w_ff1_part), _job_gather_chips(w_ff2_b, part=(0, 1, 4))])

    tm, tn, tk = min(1024, S), min(1024, FB), min(2048, D)
    tm1 = min(2048, S)
    o_spec = pl.BlockSpec((tm1, tn), lambda i, j, k: (i, j))

    def relu_sq(acc):
        a = jnp.maximum(acc, 0.0)
        return a, a * a
    nj = FB // tn
    ff2_rest = [_job_gather_chips(w_ff2_b, part=(1, 4, 4), into=w_ff2_q1)]
    (a_act, a_sq), ((w_ff2_q2,),) = _mm(
        "ff1", (S // tm1, DFF // tn, D // tk), h2, pl.BlockSpec((tm1, tk), lambda i, j, k: (i, k)),
        w_ff1_all, pl.BlockSpec((None, tk, tn), lambda i, j, k: (j // nj, k, j % nj)), NN, (tm1, tn),
        [jax.ShapeDtypeStruct((S, DFF), BF16)] * 2, [o_spec] * 2, epilogue=relu_sq, jobs=ff2_rest)
    (w_ff2_all,) = _run_jobs("ag_w_ff2_sibling", [_job_gather_sibling(w_ff2_q2)])[0]
    w_ff2_full = w_ff2_all.reshape(DFF, D)
    def head_fn(acc, res, t, g):
        xb = acc + res
        rstd = lax.rsqrt(jnp.mean(xb * xb, axis=-1, keepdims=True) + EPS)
        xhat = xb * rstd
        err = xhat * g - t
        loss = 0.5 * jnp.sum(jnp.mean(err * err, axis=-1, keepdims=True), axis=0, keepdims=True)
        dy = err * (1.0 / D)
        dg = jnp.sum(dy * xhat, axis=0, keepdims=True)
        dxhat = dy * g
        dx = rstd * (dxhat - xhat * jnp.mean(dxhat * xhat, axis=-1, keepdims=True))
        return (dx, dx), (dg, jnp.broadcast_to(loss, (1, LANES)))
    tk_ff2 = min(1024, DFF)
    (dx2, dx2_b), (dg_final, loss_part) = _mm_rows(
        "ff2", (S // TMR, DFF // tk_ff2), a_sq, pl.BlockSpec((TMR, tk_ff2), lambda i, k: (i, k)),
        w_ff2_full, pl.BlockSpec((tk_ff2, D), lambda i, k: (k, 0)), NN, TMR, D, [x1, tgt], [g_final],
        [(D, F32), (D, BF16)], [D, LANES], head_fn)

    (da,) = _mm_nt("ff2_dx", dx2_b, w_ff2_full, [BF16], 2048, 1024, 2048, extras=[a_act],
                   epilogue=lambda acc, a: (2.0 * a.astype(F32) * acc,))
    dw_ff2, dw_ff2_b = _mm_tn("ff2_dw", a_sq, dx2_b, [F32, BF16], 1024, 2048, 1024)
    tm2, tk2 = min(2048, D), min(1024, S)
    dw1_spec = pl.BlockSpec((None, tm2, FB), lambda i, j, k: (j, i, 0))
    (dw_ff1, dw_ff1_b), ((r1_ff2,),) = _mm(
        "ff1_dw", (D // tm2, DFF // FB, S // tk2), h2, pl.BlockSpec((tk2, tm2), lambda i, j, k: (k, i)),
        da, pl.BlockSpec((tk2, FB), lambda i, j, k: (k, j)), TN, (tm2, FB),
        [jax.ShapeDtypeStruct((N_DEV, D, FB), F32), jax.ShapeDtypeStruct((N_DEV, D, FB), BF16)], [dw1_spec] * 2,
        epilogue=lambda acc: (acc, acc), jobs=[_job_scatter_sibling(dw_ff2_b.reshape(4, 2, FB, D))])
    hb_ff2 = _rs_add1("rs_add1_w_ff2", dw_ff2.reshape(4, 2, FB, D), r1_ff2, c_idx)
    def ffn_bwd_fn(dh, xb, dres, g):
        dx, dg = _rms_bwd(dh, xb, g)
        dx = dx + dres
        return (dx, dx), (dg,)
    tkb = min(1024, FB)
    nkb = FB // tkb
    ((dx1, dx1_b), (dg_ffn,)), ((r2_ff2,), (r1_ff1,)) = _mm_rows(
        "ff1_dx", (S // TMR, DFF // tkb), da, pl.BlockSpec((TMR, tkb), lambda i, k: (i, k)),
        w_ff1_all, pl.BlockSpec((None, D, tkb), lambda i, k: (k // nkb, 0, k % nkb)), NT, TMR, D, [x1, dx2],
        [norm_ffn_g], [(D, F32), (D, BF16)], [D], ffn_bwd_fn,
        jobs=[_job_scatter_chips(hb_ff2), _job_scatter_sibling(dw_ff1_b.reshape(4, 2, D, FB))])
    g_w_ff2 = _rs_add2_own("rs_add2_w_ff2", dw_ff2.reshape(4, 2, FB, D), r1_ff2, r2_ff2, c_idx, p_idx)
    hb_ff1 = _rs_add1("rs_add1_w_ff1", dw_ff1.reshape(4, 2, D, FB), r1_ff1, c_idx)

    def merge_bwd_fn(dm, a, g, ga, gg):
        da_, dga = _rms_bwd(dm[:, :DA], a, ga)
        dg_, dgg = _rms_bwd(dm[:, DA:], g, gg)
        return (da_, dg_), (dga, dgg)
    (dattn, dgm), (dg_attn, dg_gmlp) = _mm_rows(
        "out_proj_dx", (S // TMR, 1), dx1_b, pl.BlockSpec((TMR, D), lambda i, k: (i, 0)),
        w_out_full, pl.BlockSpec((D, D), lambda i, k: (0, 0)), NT, TMR, D, [attn, gm], [attn_out_g, gmlp_out_g],
        [(DA, F32), (DG, F32)], [DA, DG], merge_bwd_fn)
    dw_out, dw_out_b = _mm_tn("out_proj_dw", merged, dx1_b, [F32, BF16], 2048, 1024, 1024)

    w_st = jnp.swapaxes(w_s[0], 1, 2)
    dzu, dzv, dw_s, dbs_col, dln_g, dln_b = _gmlp_bwd(dgm, zm, gmlp_ln_g, gmlp_ln_b, w_s[0], w_st, bs_col, TR)

    delta_row3 = _attn_delta(dattn, attn, TR).reshape(H, NB, 1, T_ATT)
    (dq, ds_rowsum), ((r2_ff1_a,), (r1_out,)) = _attn2_bwd_dq(
        zm, dattn, f_row3, lse_row3, delta_row3, T_ATT,
        jobs=[_job_scatter_chips(hb_ff1, part=(0, 5, 8)),
              _job_scatter_sibling(dw_out_b.reshape(4, 2, D // N_DEV, D))])
    hb_out = _rs_add1("rs_add1_w_out", dw_out.reshape(4, 2, D // N_DEV, D), r1_out, c_idx)
    (dk, dv, df_row3), ((r2_ff1,), (r2_out,)) = _attn2_bwd_dkv(
        zm, dattn, f_row3, lse_row3, delta_row3, ds_rowsum, T_ATT,
        jobs=[_job_scatter_chips(hb_ff1, part=(5, 8, 8), into=r2_ff1_a), _job_scatter_chips(hb_out)])
    g_w_ff1 = _rs_add2_own("rs_add2_w_ff1", dw_ff1.reshape(4, 2, D, FB), r1_ff1, r2_ff1, c_idx, p_idx)
    g_w_out = _rs_add2_own("rs_add2_w_out", dw_out.reshape(4, 2, D // N_DEV, D), r1_out, r2_out, c_idx, p_idx)
    dzf, dbf = _fgate_bwd(df_row3.reshape(H, S), zf, bf_pad)

    dz_main = jnp.concatenate([dq, dk, dv, dzu, dzv], axis=1)
    dw_main, dw_main_b = _mm_tn("in_proj_dw", h, dz_main, [F32, BF16], 2048, 1024, 1024)
    (dw_f,), ((r1_in,),) = _mm_tn("in_proj_f_dw", h, dzf, [F32], 2048, LANES, 1024,
                                  jobs=[_job_scatter_sibling_windows(dw_main_b, starts, WW)])
    first_blocks = jnp.stack([jnp.where(c_pos == 0, starts[2 * p], starts[2 * p + 1]) // LANES
                              for p in range(4)]).astype(jnp.int32)
    hb_in = _rs_add1_windows("rs_add1_w_in", dw_main, r1_in, first_blocks)

    def mix_bwd_fn(dh_main, dz_gate, xb, dres, g, w_gate):
        dx, dg = _rms_bwd(dh_main + _dot(dz_gate, w_gate, NT), xb, g)
        return (dx + dres,), (dg,)
    tk_in = min(1024, DMAIN)
    ((grad_x,), (dg_mix,)), ((r2_in,),) = _mm_rows(
        "in_proj_dx", (S // TMR, DMAIN // tk_in), dz_main, pl.BlockSpec((TMR, tk_in), lambda i, k: (i, k)),
        w_main, pl.BlockSpec((D, tk_in), lambda i, k: (0, k)), NT, TMR, D, [dzf, x0, dx1], [norm_mix_g, w_f],
        [(D, F32)], [D], mix_bwd_fn, jobs=[_job_scatter_chips(hb_in)])
    g_window = _rs_add2_own_window("rs_add2_w_in", dw_main, r1_in, r2_in, first_blocks, p_idx)

    small_shapes = [norm_mix_g.shape, b_f.shape, gmlp_ln_g.shape, gmlp_ln_b.shape, w_s.shape, b_s.shape,
                    attn_out_g.shape, gmlp_out_g.shape, norm_ffn_g.shape, norm_final_g.shape]
    small_parts = [dg_mix, dbf[:, :H], dln_g, dln_b, dw_s, dbs_col, dg_attn, dg_gmlp, dg_ffn, dg_final]
    g_small = _sum8("small_sum", _all_gather("ag_small", _pack(small_parts + [dw_f[:, :H], loss_part])))
    *gs, g_gate, loss_sum = _unpack(g_small, small_shapes + [(D, H), (1, LANES)])
    two_d = lambda a: a.reshape(1, -1) if a.ndim == 1 else a
    ds, nms, nvs = _adamw_many(
        "adamw_small",
        [two_d(a) for a in (norm_mix_g, b_f, gmlp_ln_g, gmlp_ln_b, w_s, b_s, attn_out_g, gmlp_out_g, norm_ffn_g,
                            norm_final_g)],
        [two_d(a) for a in gs],
        [two_d(a) for a in (m_norm_mix_g, m_b_f, m_gmlp_ln_g, m_gmlp_ln_b, m_w_s, m_b_s, m_attn_out_g, m_gmlp_out_g,
                            m_norm_ffn_g, m_norm_final_g)],
        [two_d(a) for a in (v_norm_mix_g, v_b_f, v_gmlp_ln_g, v_gmlp_ln_b, v_w_s, v_b_s, v_attn_out_g, v_gmlp_out_g,
                            v_norm_ffn_g, v_norm_final_g)])
    ds, nms, nvs = [[a.reshape(s) for a, s in zip(lst, small_shapes)] for lst in (ds, nms, nvs)]

    is_gate_dev = me_idx == gate_dev
    where = jnp.stack([sum(jnp.where(me_idx == n, lo[n] - starts[n], 0) for n in range(N_DEV)),
                       jnp.where(is_gate_dev, n_before, w_in_cols), jnp.where(is_gate_dev, H, 0)]).astype(jnp.int32)
    big = {"w_in": tuple(a[None] for a in _adamw_from_window(
        "adamw_w_in", w_in[0], m_w_in[0], v_w_in[0], g_window, jnp.pad(g_gate, ((0, 0), (0, LANES - H))), where))}
    for nm, w, g, m, v in (("w_out", w_out, g_w_out, m_w_out, v_w_out),
                           ("w_ff1", w_ff1, g_w_ff1, m_w_ff1, v_w_ff1), ("w_ff2", w_ff2, g_w_ff2, m_w_ff2, v_w_ff2)):
        (d_, m_, v_), _ = _adamw("adamw_" + nm, w[0], g, m[0], v[0])
        big[nm] = (g[None], d_[None], m_[None], v_[None])

    loss = loss_sum[0, 0]

    def leaves(n):
        sm = (gs, ds, nms, nvs)[n]
        return [sm[0], big["w_in"][n], sm[1], sm[2], sm[3], sm[4], sm[5], sm[6], sm[7], big["w_out"][n], sm[8],
                big["w_ff1"][n], big["w_ff2"][n], sm[9]]

    return (loss, grad_x[None], *leaves(0), *leaves(1), *leaves(2), *leaves(3))
```

```python
import functools
import math

import jax
import jax.numpy as jnp
from jax import lax
from jax.experimental import pallas as pl
from jax.experimental.pallas import tpu as pltpu

F32 = jnp.float32
BF16 = jnp.bfloat16
MESH = pl.DeviceIdType.MESH

HEAD_DIM = 128
CHUNK = 128
EPS = 1e-6
LANES = 128
N_DEV = 8

ADAM_LR = 0.001
ADAM_B1 = 0.9
ADAM_B2 = 0.999
ADAM_EPS = 1e-08
ADAM_WD = 0.01
ADAM_STEP = 10

VMEM_LIMIT_BYTES = 56 * 1024 * 1024
T_ATT_MAX = 1024
TR_MAX = 1024
TMR_MAX = 512

NN = ((1,), (0,))
NT = ((1,), (1,))
TN = ((0,), (0,))


def _params(sem=None):
    return pltpu.CompilerParams(dimension_semantics=sem, vmem_limit_bytes=VMEM_LIMIT_BYTES)


def _dot(a, b, contract=NN):
    return lax.dot_general(a, b, (contract, ((), ())), preferred_element_type=F32)


def _dot3(x, t):
    x1 = x.astype(BF16)
    r1 = x - x1.astype(F32)
    x2 = r1.astype(BF16)
    x3 = (r1 - x2.astype(F32)).astype(BF16)
    return _dot(x1, t) + _dot(x2, t) + _dot(x3, t)


def _iota2(shape, dim):
    return lax.broadcasted_iota(jnp.int32, shape, dim)


def _row_call(name, fn, row_ins, bcast_ins, row_outs, acc_outs, tr, jobs=()):
    S = row_ins[0].shape[0]
    assert S % tr == 0
    n_ri, n_bi, n_ro, n_ao = len(row_ins), len(bcast_ins), len(row_outs), len(acc_outs)

    def body(*refs):
        ins = [r[...] for r in refs[:n_ri + n_bi]]
        ro_refs = refs[n_ri + n_bi:n_ri + n_bi + n_ro]
        ao_refs = refs[n_ri + n_bi + n_ro:]
        ro, ao = fn(*ins)
        for r, v in zip(ro_refs, ro):
            r[...] = v.astype(r.dtype)
        if n_ao:
            @pl.when(pl.program_id(0) == 0)
            def _():
                for r in ao_refs:
                    r[...] = jnp.zeros_like(r)
            for r, v in zip(ao_refs, ao):
                r[...] += v

    in_specs = [pl.BlockSpec((tr, a.shape[1]), lambda i: (i, 0)) for a in row_ins]
    in_specs += [pl.BlockSpec(a.shape, lambda i: (0, 0)) for a in bcast_ins]
    out_specs = [pl.BlockSpec((tr, d), lambda i: (i, 0)) for d, _ in row_outs]
    out_specs += [pl.BlockSpec((1, d), lambda i: (0, 0)) for d in acc_outs]
    out_shape = [jax.ShapeDtypeStruct((S, d), dt) for d, dt in row_outs]
    out_shape += [jax.ShapeDtypeStruct((1, d), F32) for d in acc_outs]
    outs, job_res = _carry_call(
        body, name=name, grid=(S // tr,), in_specs=in_specs, out_specs=out_specs, out_shape=out_shape,
        scratch_shapes=[], semantics=("arbitrary",) if n_ao else ("parallel",), args=list(row_ins) + list(bcast_ins),
        jobs=jobs)
    res = (outs[:n_ro], outs[n_ro:])
    return (res, job_res) if jobs else res


def _rms_fwd(x, g):
    rstd = lax.rsqrt(jnp.mean(x * x, axis=-1, keepdims=True) + EPS)
    return x * rstd * g


def _rms_bwd(dy, x, g):
    rstd = lax.rsqrt(jnp.mean(x * x, axis=-1, keepdims=True) + EPS)
    xhat = x * rstd
    dg = jnp.sum(dy * xhat, axis=0, keepdims=True)
    dxhat = dy * g
    dx = rstd * (dxhat - xhat * jnp.mean(dxhat * xhat, axis=-1, keepdims=True))
    return dx, dg


_GELU_C = math.sqrt(2.0 / math.pi)


def _gelu(x):
    return 0.5 * x * (1.0 + jnp.tanh(_GELU_C * (x + 0.044715 * (x * x * x))))


def _gelu_grad(x):
    t = jnp.tanh(_GELU_C * (x + 0.044715 * (x * x * x)))
    return 0.5 * (1.0 + t) + 0.5 * x * (1.0 - t * t) * (_GELU_C * (1.0 + 3.0 * 0.044715 * (x * x)))


def _me():
    return lax.axis_index("x"), lax.axis_index("y"), lax.axis_index("c")


def _other_chips(x, y):
    return [(1 - x, y), (x, 1 - y), (1 - x, 1 - y)]


_ANY = pl.BlockSpec(memory_space=pl.ANY)


class _Job:
    def __init__(self, ins, outs, n_sems, make, aliases=None):
        self.ins, self.outs, self.n_sems, self.make, self.aliases = ins, outs, n_sems, make, aliases or {}


def _job_gather_chips(blk, part=(0, 1, 1), into=None):
    R, C = blk.shape
    nr = R // part[2]
    rows = pl.ds(part[0] * nr, (part[1] - part[0]) * nr)

    def make(ins, outs, send_sems, recv_sems, base):
        x_ref, (out_ref,) = ins[0], outs
        x, y, c = _me()
        mine = 4 * x + 2 * y + c
        targets = [(x, y, 1 - c)] + [(cx, cy, c) for cx, cy in _other_chips(x, y)]

        def copy(k, slab, to):
            return pltpu.make_async_remote_copy(
                src_ref=x_ref.at[rows, :], dst_ref=out_ref.at[slab, rows, :], send_sem=send_sems.at[base + k],
                recv_sem=recv_sems.at[base + k], device_id=to, device_id_type=MESH)

        starts = [copy(k, mine, to) for k, to in enumerate(targets)]
        arrivals = [copy(k, 4 * tx + 2 * ty + tc, (tx, ty, tc)) for k, (tx, ty, tc) in enumerate(targets)]
        local = [pltpu.make_async_copy(x_ref.at[rows, :], out_ref.at[mine, rows, :], send_sems.at[base + 4])]
        return starts, arrivals, local

    out = jax.ShapeDtypeStruct((N_DEV, R, C), blk.dtype)
    if into is None:
        return _Job([blk], [out], 5, make)
    return _Job([blk, into], [out], 5, make, aliases={1: 0})


def _job_gather_sibling(part):
    def make(ins, outs, send_sems, recv_sems, base):
        (out_ref,) = outs
        x, y, c = _me()

        def copy(k, slab):
            return pltpu.make_async_remote_copy(
                src_ref=out_ref.at[slab], dst_ref=out_ref.at[slab], send_sem=send_sems.at[base + k],
                recv_sem=recv_sems.at[base + k], device_id=(x, y, 1 - c), device_id_type=MESH)

        chips = _other_chips(x, y)
        starts = [copy(k, 4 * cx + 2 * cy + c) for k, (cx, cy) in enumerate(chips)]
        arrivals = [copy(k, 4 * cx + 2 * cy + (1 - c)) for k, (cx, cy) in enumerate(chips)]
        return starts, arrivals, []

    return _Job([part], [jax.ShapeDtypeStruct(part.shape, part.dtype)], 3, make, aliases={0: 0})


def _job_scatter_sibling(gb):
    _, _, R, C = gb.shape

    def make(ins, outs, send_sems, recv_sems, base):
        (g_ref,), (recv_ref,) = ins, outs
        x, y, c = _me()
        copies = [pltpu.make_async_remote_copy(
            src_ref=g_ref.at[p, 1 - c], dst_ref=recv_ref.at[p], send_sem=send_sems.at[base + p],
            recv_sem=recv_sems.at[base + p], device_id=(x, y, 1 - c), device_id_type=MESH) for p in range(4)]
        return copies, copies, []

    return _Job([gb], [jax.ShapeDtypeStruct((4, R, C), gb.dtype)], 4, make)


def _job_scatter_sibling_windows(gb, starts, width):
    R, _ = gb.shape

    def make(ins, outs, send_sems, recv_sems, base):
        (g_ref,), (recv_ref,) = ins, outs
        x, y, c = _me()
        copies = []
        for p in range(4):
            start = pl.multiple_of(jnp.where(c == 0, starts[2 * p + 1], starts[2 * p]), LANES)
            copies.append(pltpu.make_async_remote_copy(
                src_ref=g_ref.at[:, pl.ds(start, width)], dst_ref=recv_ref.at[p], send_sem=send_sems.at[base + p],
                recv_sem=recv_sems.at[base + p], device_id=(x, y, 1 - c), device_id_type=MESH))
        return copies, copies, []

    return _Job([gb], [jax.ShapeDtypeStruct((4, R, width), gb.dtype)], 4, make)


def _job_scatter_chips(hb, part=(0, 1, 1), into=None):
    _, R, C = hb.shape
    nr = R // part[2]
    rows = pl.ds(part[0] * nr, (part[1] - part[0]) * nr)

    def make(ins, outs, send_sems, recv_sems, base):
        h_ref, (recv_ref,) = ins[0], outs
        x, y, c = _me()
        copies = [pltpu.make_async_remote_copy(
            src_ref=h_ref.at[2 * cx + cy, rows, :], dst_ref=recv_ref.at[n, rows, :], send_sem=send_sems.at[base + n],
            recv_sem=recv_sems.at[base + n], device_id=(cx, cy, c), device_id_type=MESH)
            for n, (cx, cy) in enumerate(_other_chips(x, y))]
        return copies, copies, []

    out = jax.ShapeDtypeStruct((3, R, C), hb.dtype)
    if into is None:
        return _Job([hb], [out], 3, make)
    return _Job([hb, into], [out], 3, make, aliases={1: 0})


def _carry_call(body, *, name, grid, in_specs, out_specs, out_shape, scratch_shapes, semantics, args, jobs=()):
    jobs = list(jobs)
    n_in, n_out, n_scr = len(in_specs), len(out_specs), len(scratch_shapes)
    j_ins = [a for j in jobs for a in j.ins]
    j_outs = [o for j in jobs for o in j.outs]
    n_sems = sum(j.n_sems for j in jobs)
    aliases = {}
    i0, o0 = n_in, n_out
    for j in jobs:
        for a, b in j.aliases.items():
            aliases[i0 + a] = o0 + b
        i0 += len(j.ins)
        o0 += len(j.outs)

    def full_body(*refs):
        ins = refs[:n_in]
        jin = refs[n_in:n_in + len(j_ins)]
        outs = refs[n_in + len(j_ins):n_in + len(j_ins) + n_out]
        jout = refs[n_in + len(j_ins) + n_out:n_in + len(j_ins) + n_out + len(j_outs)]
        scr = refs[n_in + len(j_ins) + n_out + len(j_outs):]
        if jobs:
            send_sems, recv_sems = scr[n_scr], scr[n_scr + 1]
            starts, arrivals, local = [], [], []
            base = i0 = o0 = 0
            for j in jobs:
                s, a, l = j.make(jin[i0:i0 + len(j.ins)], jout[o0:o0 + len(j.outs)], send_sems, recv_sems, base)
                starts += s
                arrivals += a
                local += l
                base += j.n_sems
                i0 += len(j.ins)
                o0 += len(j.outs)
            pids = [pl.program_id(d) for d in range(len(grid))]
            first = functools.reduce(jnp.logical_and, [p == 0 for p in pids])
            last = functools.reduce(jnp.logical_and, [p == n - 1 for p, n in zip(pids, grid)])

            @pl.when(first)
            def _():
                for cp in local + starts:
                    cp.start()

        body(*ins, *outs, *scr[:n_scr])

        if jobs:
            @pl.when(last)
            def _():
                for cp in arrivals:
                    cp.wait_recv()
                for cp in starts:
                    cp.wait_send()
                for cp in local:
                    cp.wait()

    sems = [pltpu.SemaphoreType.DMA((n_sems,)), pltpu.SemaphoreType.DMA((n_sems,))] if jobs else []
    res = pl.pallas_call(
        full_body, name=name, grid=grid,
        in_specs=list(in_specs) + [_ANY] * len(j_ins),
        out_specs=list(out_specs) + [_ANY] * len(j_outs),
        out_shape=list(out_shape) + j_outs,
        scratch_shapes=list(scratch_shapes) + sems,
        input_output_aliases=aliases,
        compiler_params=_params(("arbitrary",) * len(grid) if jobs else semantics),
    )(*args, *j_ins)
    body_res, job_res = res[:n_out], res[n_out:]
    per_job = []
    for j in jobs:
        per_job.append(job_res[:len(j.outs)])
        job_res = job_res[len(j.outs):]
    return body_res, per_job


def _run_jobs(name, jobs):
    def body(done_ref):
        done_ref[...] = jnp.zeros_like(done_ref)

    return _carry_call(body, name=name, grid=(1,), in_specs=[], out_specs=[pl.BlockSpec((8, LANES), lambda i: (0, 0))],
                       out_shape=[jax.ShapeDtypeStruct((8, LANES), F32)], scratch_shapes=[], semantics=("arbitrary",),
                       args=[], jobs=jobs)[1]


def _mm(name, grid, a, a_spec, b, b_spec, contract, acc_shape, out_shape, out_specs, extras=(), epilogue=None, jobs=()):
    nk = grid[2]
    n_e = len(extras)
    n_o = len(out_shape)
    if epilogue is None:
        epilogue = lambda acc: (acc,)

    def body(a_ref, b_ref, *rest):
        e_refs = rest[:n_e]
        o_refs = rest[n_e:n_e + n_o]

        def finish(total):
            res = epilogue(total, *[r[...] for r in e_refs])
            for o, r in zip(o_refs, res):
                o[...] = r.astype(o.dtype)

        if nk == 1:
            finish(_dot(a_ref[...], b_ref[...], contract))
            return
        acc = rest[n_e + n_o]
        k = pl.program_id(2)

        @pl.when(k == 0)
        def _():
            acc[...] = _dot(a_ref[...], b_ref[...], contract)

        @pl.when(jnp.logical_and(k > 0, k < nk - 1))
        def _():
            acc[...] += _dot(a_ref[...], b_ref[...], contract)

        @pl.when(k == nk - 1)
        def _():
            finish(acc[...] + _dot(a_ref[...], b_ref[...], contract))

    outs, job_res = _carry_call(
        body, name=name, grid=grid, in_specs=[a_spec, b_spec] + [s for _, s in extras],
        out_specs=list(out_specs), out_shape=list(out_shape),
        scratch_shapes=[pltpu.VMEM(acc_shape, F32)] if nk > 1 else [],
        semantics=("parallel", "parallel", "arbitrary"), args=[a, b] + [e for e, _ in extras], jobs=jobs)
    return (outs, job_res) if jobs else outs


def _mm_rows(name, grid, a, a_spec, b, b_spec, contract, tm, n, row_extras, bcast, row_outs, acc_outs, epilogue, jobs=()):
    nk = grid[1]
    M = grid[0] * tm
    n_x, n_b, n_ro, n_ao = len(row_extras), len(bcast), len(row_outs), len(acc_outs)

    def body(a_ref, b_ref, *rest):
        x_refs = rest[:n_x + n_b]
        rest = rest[n_x + n_b:]
        ro_refs = rest[:n_ro]
        ao_refs = rest[n_ro:n_ro + n_ao]
        i = pl.program_id(0)

        def finish(total):
            ro, ao = epilogue(total, *[r[...] for r in x_refs])
            for r, v in zip(ro_refs, ro):
                r[...] = v.astype(r.dtype)
            if n_ao:
                @pl.when(i == 0)
                def _():
                    for r, v in zip(ao_refs, ao):
                        r[...] = v

                @pl.when(i > 0)
                def _():
                    for r, v in zip(ao_refs, ao):
                        r[...] += v

        if nk == 1:
            finish(_dot(a_ref[...], b_ref[...], contract))
            return
        acc = rest[n_ro + n_ao]
        k = pl.program_id(1)

        @pl.when(k == 0)
        def _():
            acc[...] = _dot(a_ref[...], b_ref[...], contract)

        @pl.when(jnp.logical_and(k > 0, k < nk - 1))
        def _():
            acc[...] += _dot(a_ref[...], b_ref[...], contract)

        @pl.when(k == nk - 1)
        def _():
            finish(acc[...] + _dot(a_ref[...], b_ref[...], contract))

    in_specs = [a_spec, b_spec] + [pl.BlockSpec((tm, x.shape[1]), lambda i, k: (i, 0)) for x in row_extras]
    in_specs += [pl.BlockSpec(x.shape, lambda i, k: (0,) * x.ndim) for x in bcast]
    out_specs = [pl.BlockSpec((tm, w), lambda i, k: (i, 0)) for w, _ in row_outs]
    out_specs += [pl.BlockSpec((1, w), lambda i, k: (0, 0)) for w in acc_outs]
    out_shape = [jax.ShapeDtypeStruct((M, w), dt) for w, dt in row_outs]
    out_shape += [jax.ShapeDtypeStruct((1, w), F32) for w in acc_outs]
    outs, job_res = _carry_call(
        body, name=name, grid=grid, in_specs=in_specs, out_specs=out_specs, out_shape=out_shape,
        scratch_shapes=[pltpu.VMEM((tm, n), F32)] if nk > 1 else [],
        semantics=("arbitrary", "arbitrary"), args=[a, b] + list(row_extras) + list(bcast), jobs=jobs)
    res = (outs[:n_ro], outs[n_ro:])
    return (res, job_res) if jobs else res


def _mm_nn(name, a, b, out_dtypes, tm, tn, tk, extras=(), epilogue=None, jobs=()):
    M, K = a.shape
    N = b.shape[1]
    tm, tn, tk = min(tm, M), min(tn, N), min(tk, K)
    o_spec = pl.BlockSpec((tm, tn), lambda i, j, k: (i, j))
    return _mm(name, (M // tm, N // tn, K // tk),
               a, pl.BlockSpec((tm, tk), lambda i, j, k: (i, k)),
               b, pl.BlockSpec((tk, tn), lambda i, j, k: (k, j)), NN, (tm, tn),
               [jax.ShapeDtypeStruct((M, N), dt) for dt in out_dtypes], [o_spec] * len(out_dtypes),
               [(e, o_spec) for e in extras], epilogue, jobs)


def _mm_nt(name, a, b, out_dtypes, tm, tn, tk, extras=(), epilogue=None, jobs=()):
    M, K = a.shape
    N = b.shape[0]
    tm, tn, tk = min(tm, M), min(tn, N), min(tk, K)
    o_spec = pl.BlockSpec((tm, tn), lambda i, j, k: (i, j))
    return _mm(name, (M // tm, N // tn, K // tk),
               a, pl.BlockSpec((tm, tk), lambda i, j, k: (i, k)),
               b, pl.BlockSpec((tn, tk), lambda i, j, k: (j, k)), NT, (tm, tn),
               [jax.ShapeDtypeStruct((M, N), dt) for dt in out_dtypes], [o_spec] * len(out_dtypes),
               [(e, o_spec) for e in extras], epilogue, jobs)


def _mm_tn(name, a, b, out_dtypes, tm, tn, tk, jobs=()):
    K, M = a.shape
    N = b.shape[1]
    tm, tn, tk = min(tm, M), min(tn, N), min(tk, K)
    o_spec = pl.BlockSpec((tm, tn), lambda i, j, k: (i, j))
    return _mm(name, (M // tm, N // tn, K // tk),
               a, pl.BlockSpec((tk, tm), lambda i, j, k: (k, i)),
               b, pl.BlockSpec((tk, tn), lambda i, j, k: (k, j)), TN, (tm, tn),
               [jax.ShapeDtypeStruct((M, N), dt) for dt in out_dtypes], [o_spec] * len(out_dtypes),
               epilogue=lambda acc: (acc,) * len(out_dtypes), jobs=jobs)


def _fgate_fwd(zf, bf):
    S = zf.shape[0]
    nc = S // CHUNK

    def body(zf_ref, bf_ref, f_ref):
        upper = (_iota2((CHUNK, CHUNK), 0) <= _iota2((CHUNK, CHUNK), 1)).astype(BF16)
        carry = jnp.zeros((8, 1), F32)
        for c in range(nc):
            t = zf_ref[c * CHUNK:(c + 1) * CHUNK, :] + bf_ref[...]
            lf = jnp.minimum(t, 0.0) - jnp.log(1.0 + jnp.exp(-jnp.abs(t)))
            lf_rows = lf.T[0:8, :]
            f_ref[:, c * CHUNK:(c + 1) * CHUNK] = (_dot3(lf_rows, upper) + carry) * LOG2E
            carry = carry + jnp.sum(lf_rows, axis=-1, keepdims=True)

    return pl.pallas_call(
        body, name="fgate_fwd", out_shape=jax.ShapeDtypeStruct((8, S), F32),
        compiler_params=_params(),
    )(zf, bf)


def _fgate_bwd(df, zf, bf):
    S = zf.shape[0]
    nc = S // CHUNK

    def body(df_ref, zf_ref, bf_ref, dzf_ref, dbf_ref):
        lower = (_iota2((CHUNK, CHUNK), 0) >= _iota2((CHUNK, CHUNK), 1)).astype(BF16)
        carry = jnp.zeros((8, 1), F32)
        dbf = jnp.zeros((1, LANES), F32)
        for c in reversed(range(nc)):
            sl = slice(c * CHUNK, (c + 1) * CHUNK)
            df = df_ref[:, sl]
            r = _dot3(df, lower) + carry
            carry = carry + jnp.sum(df, axis=-1, keepdims=True)
            r_cols = jnp.concatenate([r, jnp.zeros((CHUNK - 8, CHUNK), F32)], axis=0).T
            t = zf_ref[sl, :] + bf_ref[...]
            dz = r_cols * (1.0 / (1.0 + jnp.exp(t)))
            dzf_ref[sl, :] = dz.astype(BF16)
            dbf = dbf + jnp.sum(dz, axis=0, keepdims=True)
        dbf_ref[...] = dbf

    return pl.pallas_call(
        body, name="fgate_bwd",
        out_shape=[jax.ShapeDtypeStruct((S, LANES), BF16), jax.ShapeDtypeStruct((1, LANES), F32)],
        compiler_params=_params(),
    )(df, zf, bf)


_NEG = -1e30
LOG2E = 1.4426950408889634
N_SPLIT = 8
N_SPLIT_DIAG = 4
DIAG_STEP = 1024


def _attn_consts(T):
    rows, cols = _iota2((T, T), 0), _iota2((T, T), 1)
    return cols <= rows, rows <= cols


def _col_to_row(col):
    wide = jnp.broadcast_to(col, (col.shape[0], LANES))
    return jnp.concatenate([wide[r:r + LANES, :].T[0:1, :] for r in range(0, col.shape[0], LANES)], axis=1)


def _row_to_col(row):
    tall = jnp.broadcast_to(row, (LANES, row.shape[1]))
    return jnp.concatenate([tall[:, c:c + LANES].T[:, 0:1] for c in range(0, row.shape[1], LANES)], axis=0)


def _attn2_fwd(zm, f2row, T, jobs=()):
    S = zm.shape[0]
    H = f2row.shape[0]
    nb = S // T
    c2 = LOG2E / math.sqrt(HEAD_DIM)

    def body(q_ref, k_ref, v_ref, fk_ref, o_ref, lse_ref, vaug_s, fq_ref):
        i = pl.program_id(1)

        @pl.when(i == 0)
        def _():
            vaug_s[:, :HEAD_DIM] = v_ref[...]
            vaug_s[:, HEAD_DIM:] = jnp.ones((S, HEAD_DIM), BF16)

        fq_ref[...] = _row_to_col(fk_ref[i])
        keep = _attn_consts(T)[0]
        TH = T // N_SPLIT

        def block(j, diagonal, state):
            r0 = pl.multiple_of(j * T, T)
            fk = fk_ref[j]
            new = []
            for g, (m_old, acc) in enumerate(state):
                rows = slice(g * TH, (g + 1) * TH)
                nk = min(T, -(-(g + 1) * TH // DIAG_STEP) * DIAG_STEP) if diagonal else T
                s = _dot(q_ref[rows, :], k_ref[pl.ds(r0, nk), :], NT) * c2 + (fq_ref[rows, :] - fk[:, :nk])
                if diagonal:
                    s = jnp.where(keep[rows, :nk], s, _NEG)
                m_new = jnp.maximum(m_old, jnp.max(s, axis=-1, keepdims=True))
                p = jnp.exp2(s - m_new).astype(BF16)
                new.append((m_new, jnp.exp2(m_old - m_new) * acc + _dot(p, vaug_s[pl.ds(r0, nk), :])))
            return tuple(new)

        init = tuple((jnp.full((TH, 1), _NEG, F32), jnp.zeros((TH, 2 * HEAD_DIM), F32)) for _ in range(N_SPLIT))
        state = lax.fori_loop(0, i, lambda j, st: block(j, False, st), init)
        state = block(i, True, state)
        for g, (m, acc) in enumerate(state):
            rows = slice(g * TH, (g + 1) * TH)
            o_ref[rows, :] = acc[:, :HEAD_DIM] / acc[:, HEAD_DIM:]
            lse_ref[:, rows] = _col_to_row(m + jnp.log2(acc[:, HEAD_DIM:HEAD_DIM + 1]))

    nh = H
    return _carry_call(
        body, name="attn_fwd", grid=(H, nb), jobs=jobs, args=[zm, zm, zm, f2row],
        semantics=("arbitrary", "arbitrary"),
        in_specs=[
            pl.BlockSpec((T, HEAD_DIM), lambda h, i: (i, h)),
            pl.BlockSpec((S, HEAD_DIM), lambda h, i: (0, nh + h)),
            pl.BlockSpec((S, HEAD_DIM), lambda h, i: (0, 2 * nh + h)),
            pl.BlockSpec((None, nb, 1, T), lambda h, i: (h, 0, 0, 0)),
        ],
        out_specs=[pl.BlockSpec((T, HEAD_DIM), lambda h, i: (i, h)),
                   pl.BlockSpec((None, None, 1, T), lambda h, i: (h, i, 0, 0))],
        out_shape=[jax.ShapeDtypeStruct((S, H * HEAD_DIM), F32), jax.ShapeDtypeStruct((H, nb, 1, T), F32)],
        scratch_shapes=[pltpu.VMEM((S, 2 * HEAD_DIM), BF16), pltpu.VMEM((T, 1), F32)],
    )


def _attn2_bwd_dq(zm, dattn, f2row, lse2_row, delta_row, T, jobs=()):
    S = zm.shape[0]
    H = f2row.shape[0]
    nb = S // T
    scale = 1.0 / math.sqrt(HEAD_DIM)
    c2 = LOG2E * scale

    def body(q_ref, k_ref, v_ref, do_ref, fk_ref, lse_ref, dlr_ref, dq_ref, rs_ref, bias_s, do_s, dl_ref):
        i = pl.program_id(1)
        keep = _attn_consts(T)[0]
        TH = T // N_SPLIT_DIAG
        bias_s[...] = _row_to_col(fk_ref[i] - lse_ref[...])
        dl_ref[...] = _row_to_col(dlr_ref[...])
        do_s[...] = do_ref[...].astype(BF16)

        def part(rows, j, nk, state, masked):
            acc, rs = state
            r0 = pl.multiple_of(j * T, T)
            kb = k_ref[pl.ds(r0, nk), :]
            s = _dot(q_ref[rows, :], kb, NT) * c2 + (bias_s[rows, :] - fk_ref[j][:, :nk])
            if masked:
                s = jnp.where(keep[rows, :nk], s, _NEG)
            ds = jnp.exp2(s) * (_dot(do_s[rows, :], v_ref[pl.ds(r0, nk), :], NT) - dl_ref[rows, :])
            return acc + _dot(ds.astype(BF16), kb), rs + jnp.sum(ds, axis=-1, keepdims=True)

        def step(j, state):
            return part(slice(0, T), j, T, state, False)

        acc, rs = lax.fori_loop(0, i, step, (jnp.zeros((T, HEAD_DIM), F32), jnp.zeros((T, 1), F32)))
        for g in range(N_SPLIT_DIAG):
            rows = slice(g * TH, (g + 1) * TH)
            acc_g, rs_g = part(rows, i, (g + 1) * TH, (acc[rows, :], rs[rows, :]), True)
            dq_ref[rows, :] = (acc_g * scale).astype(BF16)
            rs_ref[:, rows] = _col_to_row(rs_g)

    nh = H
    row = pl.BlockSpec((None, None, 1, T), lambda h, i: (h, i, 0, 0))
    blk = pl.BlockSpec((T, HEAD_DIM), lambda h, i: (i, h))
    return _carry_call(
        body, name="attn_bwd_dq", grid=(H, nb), jobs=jobs,
        args=[zm, zm, zm, dattn, f2row, lse2_row, delta_row], semantics=("arbitrary", "arbitrary"),
        in_specs=[
            blk,
            pl.BlockSpec((S, HEAD_DIM), lambda h, i: (0, nh + h)),
            pl.BlockSpec((S, HEAD_DIM), lambda h, i: (0, 2 * nh + h)),
            blk,
            pl.BlockSpec((None, nb, 1, T), lambda h, i: (h, 0, 0, 0)),
            row, row,
        ],
        out_specs=[blk, row],
        out_shape=[jax.ShapeDtypeStruct((S, H * HEAD_DIM), BF16), jax.ShapeDtypeStruct((H, nb, 1, T), F32)],
        scratch_shapes=[pltpu.VMEM((T, 1), F32), pltpu.VMEM((T, HEAD_DIM), BF16), pltpu.VMEM((T, 1), F32)],
    )


def _attn2_bwd_dkv(zm, dattn, f2row, lse2_row, delta_row, rowsum_row, T, jobs=()):
    S = zm.shape[0]
    H = f2row.shape[0]
    nb = S // T
    scale = 1.0 / math.sqrt(HEAD_DIM)
    c2 = LOG2E * scale

    def body(q_ref, k_ref, v_ref, do_ref, fq_ref, lse_ref, dl_ref, rs_ref, dk_ref, dv_ref, df_ref, fk_ref):
        j = pl.program_id(1)
        keep = _attn_consts(T)[1]
        TH = T // N_SPLIT_DIAG
        fk_ref[...] = _row_to_col(fq_ref[j])

        def part(rows, i, c0, state, masked):
            dk, dv, df = state
            r0 = pl.multiple_of(i * T + c0, TH)
            qb = q_ref[pl.ds(r0, T - c0), :]
            do = do_ref[pl.ds(r0, T - c0), :].astype(BF16)
            bias = (fq_ref[i] - lse_ref[i])[:, c0:]
            dl = (dl_ref[i] + rs_ref[i])[:, c0:]
            st = _dot(k_ref[rows, :], qb, NT) * c2 + (bias - fk_ref[rows, :])
            if masked:
                st = jnp.where(keep[rows, c0:], st, _NEG)
            pt = jnp.exp2(st)
            dst = pt * (_dot(v_ref[rows, :], do, NT) - dl)
            return (dk + _dot(dst.astype(BF16), qb), dv + _dot(pt.astype(BF16), do),
                    df - jnp.sum(dst, axis=-1, keepdims=True))

        groups = []
        for g in range(N_SPLIT_DIAG):
            zero = (jnp.zeros((TH, HEAD_DIM), F32), jnp.zeros((TH, HEAD_DIM), F32), jnp.zeros((TH, 1), F32))
            groups.append(part(slice(g * TH, (g + 1) * TH), j, g * TH, zero, True))
        state = tuple(jnp.concatenate([grp[n] for grp in groups], axis=0) for n in range(3))
        dk, dv, df = lax.fori_loop(j + 1, nb, lambda i, st: part(slice(0, T), i, 0, st, False), state)
        dk_ref[...] = (dk * scale).astype(BF16)
        dv_ref[...] = dv.astype(BF16)
        df_ref[...] = _col_to_row(df)

    nh = H
    row = pl.BlockSpec((None, nb, 1, T), lambda h, j: (h, 0, 0, 0))
    whole = pl.BlockSpec((S, HEAD_DIM), lambda h, j: (0, h))
    kv_out = pl.BlockSpec((T, HEAD_DIM), lambda h, j: (j, h))
    return _carry_call(
        body, name="attn_bwd_dkv", grid=(H, nb), jobs=jobs,
        args=[zm, zm, zm, dattn, f2row, lse2_row, delta_row, rowsum_row],
        semantics=("arbitrary", "arbitrary"),
        in_specs=[
            whole,
            pl.BlockSpec((T, HEAD_DIM), lambda h, j: (j, nh + h)),
            pl.BlockSpec((T, HEAD_DIM), lambda h, j: (j, 2 * nh + h)),
            whole, row, row, row, row,
        ],
        out_specs=[kv_out, kv_out, pl.BlockSpec((None, None, 1, T), lambda h, j: (h, j, 0, 0))],
        out_shape=[jax.ShapeDtypeStruct((S, H * HEAD_DIM), BF16), jax.ShapeDtypeStruct((S, H * HEAD_DIM), BF16),
                   jax.ShapeDtypeStruct((H, nb, 1, T), F32)],
        scratch_shapes=[pltpu.VMEM((T, 1), F32)],
    )


def _attn_delta(dattn, attn, tr):
    S, DA = attn.shape
    H = DA // HEAD_DIM

    def body(do_ref, o_ref, out_ref):
        lo = _iota2((DA, LANES), 1) * HEAD_DIM
        sel = ((_iota2((DA, LANES), 0) >= lo) & (_iota2((DA, LANES), 0) < lo + HEAD_DIM)).astype(BF16)
        d = _dot3(do_ref[...] * o_ref[...], sel)
        for c in range(tr // CHUNK):
            out_ref[:, c * CHUNK:(c + 1) * CHUNK] = d[c * CHUNK:(c + 1) * CHUNK, :].T[0:H, :]

    return pl.pallas_call(
        body, name="attn_delta", grid=(S // tr,),
        in_specs=[pl.BlockSpec((tr, DA), lambda i: (i, 0))] * 2,
        out_specs=pl.BlockSpec((H, tr), lambda i: (0, i)),
        out_shape=jax.ShapeDtypeStruct((H, S), F32),
        compiler_params=_params(("parallel",)),
    )(dattn, attn)


def _ln_stats(x):
    mu = jnp.mean(x, axis=-1, keepdims=True)
    xc = x - mu
    rstd = lax.rsqrt(jnp.mean(xc * xc, axis=-1, keepdims=True) + EPS)
    return xc * rstd, rstd


def _tril_mask():
    return _iota2((CHUNK, CHUNK), 0) >= _iota2((CHUNK, CHUNK), 1)


def _gmlp_fwd(zm, ln_g, ln_b, w_s, bs_col, attn, attn_g, gm_g, tr):
    S = zm.shape[0]
    H = w_s.shape[0]
    DG = H * HEAD_DIM
    DA = attn.shape[1]

    def body(zu_ref, zv_ref, g_ref, b_ref, w_ref, bs_ref, a_ref, ag_ref, gg_ref, out_ref, merged_ref):
        u = _gelu(zu_ref[...].astype(F32))
        y, _ = _ln_stats(_gelu(zv_ref[...].astype(F32)))
        v = (y * g_ref[...] + b_ref[...]).astype(BF16)
        mask = _tril_mask()
        for h in range(H):
            wc = jnp.where(mask, w_ref[h], 0.0).astype(BF16)
            cs = slice(h * HEAD_DIM, (h + 1) * HEAD_DIM)
            for c in range(tr // CHUNK):
                rs = slice(c * CHUNK, (c + 1) * CHUNK)
                mix = _dot(wc, v[rs, cs]) + bs_ref[h]
                out_ref[rs, cs] = u[rs, cs] * mix
        merged_ref[:, :DA] = _rms_fwd(a_ref[...], ag_ref[...]).astype(BF16)
        merged_ref[:, DA:] = _rms_fwd(out_ref[...], gg_ref[...]).astype(BF16)

    full = lambda a: pl.BlockSpec(a.shape, lambda i: (0,) * a.ndim)
    return pl.pallas_call(
        body, name="gmlp_fwd", grid=(S // tr,),
        in_specs=[pl.BlockSpec((tr, DG), lambda i: (i, 3)), pl.BlockSpec((tr, DG), lambda i: (i, 4)),
                  full(ln_g), full(ln_b), full(w_s), full(bs_col),
                  pl.BlockSpec((tr, DA), lambda i: (i, 0)), full(attn_g), full(gm_g)],
        out_specs=[pl.BlockSpec((tr, DG), lambda i: (i, 0)), pl.BlockSpec((tr, DA + DG), lambda i: (i, 0))],
        out_shape=[jax.ShapeDtypeStruct((S, DG), F32), jax.ShapeDtypeStruct((S, DA + DG), BF16)],
        compiler_params=_params(("parallel",)),
    )(zm, zm, ln_g, ln_b, w_s, bs_col, attn, attn_g, gm_g)


def _gmlp_bwd(dgm, zm, ln_g, ln_b, w_s, w_st, bs_col, tr):
    S = zm.shape[0]
    H = w_s.shape[0]
    DG = H * HEAD_DIM

    def body(dg_ref, zu_ref, zv_ref, g_ref, b_ref, w_ref, wt_ref, bs_ref,
             dzu_ref, dzv_ref, dw_ref, dbs_ref, dlg_ref, dlb_ref, dv_s):
        @pl.when(pl.program_id(0) == 0)
        def _():
            dw_ref[...] = jnp.zeros_like(dw_ref)
            dbs_ref[...] = jnp.zeros_like(dbs_ref)
            dlg_ref[...] = jnp.zeros_like(dlg_ref)
            dlb_ref[...] = jnp.zeros_like(dlb_ref)

        zu = zu_ref[...].astype(F32)
        zv = zv_ref[...].astype(F32)
        u = _gelu(zu)
        y, rstd = _ln_stats(_gelu(zv))
        v = (y * g_ref[...] + b_ref[...]).astype(BF16)
        dgm_blk = dg_ref[...]
        mask = _tril_mask()
        mask_t = _iota2((CHUNK, CHUNK), 0) <= _iota2((CHUNK, CHUNK), 1)
        for h in range(H):
            wc = jnp.where(mask, w_ref[h], 0.0).astype(BF16)
            wct = jnp.where(mask_t, wt_ref[h], 0.0).astype(BF16)
            cs = slice(h * HEAD_DIM, (h + 1) * HEAD_DIM)
            dw = jnp.zeros((CHUNK, CHUNK), F32)
            dbs = jnp.zeros((CHUNK, 1), F32)
            for c in range(tr // CHUNK):
                rs = slice(c * CHUNK, (c + 1) * CHUNK)
                vch = v[rs, cs]
                mix = _dot(wc, vch) + bs_ref[h]
                dg = dgm_blk[rs, cs]
                dzu_ref[rs, cs] = (dg * mix * _gelu_grad(zu[rs, cs])).astype(BF16)
                dmix = dg * u[rs, cs]
                dbs = dbs + jnp.sum(dmix, axis=-1, keepdims=True)
                dmix_b = dmix.astype(BF16)
                dw = dw + _dot(dmix_b, vch, NT)
                dv_s[rs, cs] = _dot(wct, dmix_b)
            dw_ref[h] += jnp.where(mask, dw, 0.0)
            dbs_ref[h] += dbs
        dv = dv_s[...]
        dlg_ref[...] += jnp.sum(dv * y, axis=0, keepdims=True)
        dlb_ref[...] += jnp.sum(dv, axis=0, keepdims=True)
        dy = dv * g_ref[...]
        dgv = rstd * (dy - jnp.mean(dy, axis=-1, keepdims=True) - y * jnp.mean(dy * y, axis=-1, keepdims=True))
        dzv_ref[...] = (dgv * _gelu_grad(zv)).astype(BF16)

    full = lambda a: pl.BlockSpec(a.shape, lambda i: (0,) * a.ndim)
    rows = pl.BlockSpec((tr, DG), lambda i: (i, 0))
    return pl.pallas_call(
        body, name="gmlp_bwd", grid=(S // tr,),
        in_specs=[rows, pl.BlockSpec((tr, DG), lambda i: (i, 3)), pl.BlockSpec((tr, DG), lambda i: (i, 4)),
                  full(ln_g), full(ln_b), full(w_s), full(w_st), full(bs_col)],
        out_specs=[rows, rows, full(w_s), full(bs_col), full(ln_g), full(ln_b)],
        out_shape=[jax.ShapeDtypeStruct((S, DG), BF16), jax.ShapeDtypeStruct((S, DG), BF16),
                   jax.ShapeDtypeStruct(w_s.shape, F32), jax.ShapeDtypeStruct(bs_col.shape, F32),
                   jax.ShapeDtypeStruct(ln_g.shape, F32), jax.ShapeDtypeStruct(ln_b.shape, F32)],
        scratch_shapes=[pltpu.VMEM((tr, DG), F32)],
        compiler_params=_params(("arbitrary",)),
    )(dgm, zm, zm, ln_g, ln_b, w_s, w_st, bs_col)


def _all_gather(name, blk):
    R, C = blk.shape

    def body(x_ref, out_ref, send_sems, recv_sems, local_sem):
        x, y, c = _me()
        me, sibling = (x, y, c), (x, y, 1 - c)
        chips = [(1 - x, y), (x, 1 - y), (1 - x, 1 - y)]

        def slab(px, py, pc):
            return out_ref.at[4 * px + 2 * py + pc]

        def copy(k, block, to, src=None):
            return pltpu.make_async_remote_copy(
                src_ref=slab(*block) if src is None else src, dst_ref=slab(*block),
                send_sem=send_sems.at[k], recv_sem=recv_sems.at[k], device_id=to, device_id_type=MESH)

        mine = pltpu.make_async_copy(x_ref, slab(*me), local_sem)
        mine.start()
        first = [copy(0, me, sibling, src=x_ref)]
        first += [copy(1 + n, me, (*chip, c), src=x_ref) for n, chip in enumerate(chips)]
        for cp in first:
            cp.start()
        passed = [copy(4 + n, (*chip, c), sibling) for n, chip in enumerate(chips)]
        for n, chip in enumerate(chips):
            copy(1 + n, (*chip, c), me).wait_recv()
            passed[n].start()
        copy(0, sibling, me).wait_recv()
        for n, chip in enumerate(chips):
            copy(4 + n, (*chip, 1 - c), me).wait_recv()
        for cp in first + passed:
            cp.wait_send()
        mine.wait()

    return pl.pallas_call(
        body, name=name, out_shape=jax.ShapeDtypeStruct((N_DEV, R, C), blk.dtype),
        in_specs=[_ANY], out_specs=_ANY,
        scratch_shapes=[pltpu.SemaphoreType.DMA((7,)), pltpu.SemaphoreType.DMA((7,)), pltpu.SemaphoreType.DMA(())],
    )(blk)


def _row_tile(R, C, itemsize=4, target_bytes=2 * 1024 * 1024):
    tr = R
    while tr % 2 == 0 and tr * C * itemsize > target_bytes and (tr // 2) % 16 == 0:
        tr //= 2
    return tr


def _rs_add1(name, g4, recv, c_idx):
    _, _, R, C = g4.shape
    tr = _row_tile(R, C)

    def body(c_ref, g_ref, r_ref, hb_ref):
        hb_ref[...] = (g_ref[...] + r_ref[...].astype(F32)).astype(BF16)

    blk = pl.BlockSpec((None, tr, C), lambda p, i, c_ref: (p, i, 0))
    return pl.pallas_call(
        body, name=name,
        grid_spec=pltpu.PrefetchScalarGridSpec(
            num_scalar_prefetch=1, grid=(4, R // tr),
            in_specs=[pl.BlockSpec((None, None, tr, C), lambda p, i, c_ref: (p, c_ref[0], i, 0)), blk],
            out_specs=blk),
        out_shape=jax.ShapeDtypeStruct((4, R, C), BF16),
        compiler_params=_params(("parallel", "parallel")),
    )(c_idx, g4, recv)


def _rs_add2_own(name, g4, recv1, recv2, c_idx, p_idx):
    _, _, R, C = g4.shape
    tr = _row_tile(R, C)

    def body(c_ref, p_ref, g_ref, r1_ref, r2_ref, out_ref):
        h = g_ref[...] + r1_ref[...].astype(F32)
        out_ref[...] = ((h + r2_ref[0].astype(F32)) + r2_ref[1].astype(F32)) + r2_ref[2].astype(F32)

    return pl.pallas_call(
        body, name=name,
        grid_spec=pltpu.PrefetchScalarGridSpec(
            num_scalar_prefetch=2, grid=(R // tr,),
            in_specs=[pl.BlockSpec((None, None, tr, C), lambda i, c_ref, p_ref: (p_ref[0], c_ref[0], i, 0)),
                      pl.BlockSpec((None, tr, C), lambda i, c_ref, p_ref: (p_ref[0], i, 0)),
                      pl.BlockSpec((3, tr, C), lambda i, c_ref, p_ref: (0, i, 0))],
            out_specs=pl.BlockSpec((tr, C), lambda i, c_ref, p_ref: (i, 0))),
        out_shape=jax.ShapeDtypeStruct((R, C), F32),
        compiler_params=_params(("parallel",)),
    )(c_idx, p_idx, g4, recv1, recv2)


def _rs_add1_windows(name, g, recv, first_blocks):
    _, R, W = recv.shape
    nl = W // LANES

    def body(t_ref, *refs):
        r_ref, hb_ref = refs[nl], refs[nl + 1]
        for u in range(nl):
            cols = slice(u * LANES, (u + 1) * LANES)
            hb_ref[:, cols] = (refs[u][...] + r_ref[:, cols].astype(F32)).astype(BF16)

    blk = pl.BlockSpec((None, R, W), lambda p, t_ref: (p, 0, 0))
    return pl.pallas_call(
        body, name=name,
        grid_spec=pltpu.PrefetchScalarGridSpec(
            num_scalar_prefetch=1, grid=(4,),
            in_specs=[pl.BlockSpec((R, LANES), functools.partial(lambda u, p, t_ref: (0, t_ref[p] + u), u))
                      for u in range(nl)] + [blk],
            out_specs=blk),
        out_shape=jax.ShapeDtypeStruct((4, R, W), BF16),
        compiler_params=_params(("parallel",)),
    )(first_blocks, *([g] * nl), recv)


def _rs_add2_own_window(name, g, recv1, recv2, first_blocks, p_idx):
    _, R, W = recv1.shape
    nl = W // LANES

    def body(t_ref, p_ref, *refs):
        r1_ref, r2_ref, out_ref = refs[nl], refs[nl + 1], refs[nl + 2]
        for u in range(nl):
            cols = slice(u * LANES, (u + 1) * LANES)
            h = refs[u][...] + r1_ref[:, cols].astype(F32)
            out_ref[:, cols] = ((h + r2_ref[0, :, cols].astype(F32)) + r2_ref[1, :, cols].astype(F32)) \
                + r2_ref[2, :, cols].astype(F32)

    return pl.pallas_call(
        body, name=name,
        grid_spec=pltpu.PrefetchScalarGridSpec(
            num_scalar_prefetch=2, grid=(1,),
            in_specs=[pl.BlockSpec((R, LANES), functools.partial(lambda u, i, t, p: (0, t[p[0]] + u), u))
                      for u in range(nl)]
            + [pl.BlockSpec((None, R, W), lambda i, t, p: (p[0], 0, 0)), pl.BlockSpec((3, R, W), lambda i, t, p: (0, 0, 0))],
            out_specs=pl.BlockSpec((R, W), lambda i, t, p: (0, 0))),
        out_shape=jax.ShapeDtypeStruct((R, W), F32),
        compiler_params=_params(("arbitrary",)),
    )(first_blocks, p_idx, *([g] * nl), recv1, recv2)


def _add_windows(name, windows, first, second, n_blocks):
    _, R, W = windows.shape
    dev1 = jnp.asarray([d for d, _ in first], jnp.int32)
    blk1 = jnp.asarray([b for _, b in first], jnp.int32)
    dev2 = jnp.asarray([max(d, 0) for d, _ in second], jnp.int32)
    blk2 = jnp.asarray([b for _, b in second], jnp.int32)
    two = jnp.asarray([int(d >= 0) for d, _ in second], jnp.int32)

    G = 4
    assert n_blocks % G == 0

    def body(d1_ref, b1_ref, d2_ref, b2_ref, two_ref, *refs):
        out_ref = refs[2 * G]
        k = pl.program_id(0)
        for u in range(G):
            a_ref, b_ref = refs[u], refs[G + u]
            cols = slice(u * LANES, (u + 1) * LANES)

            @pl.when(two_ref[k * G + u] == 0)
            def _():
                out_ref[:, cols] = a_ref[...]

            @pl.when(two_ref[k * G + u] != 0)
            def _():
                out_ref[:, cols] = a_ref[...] + b_ref[...]

    def spec(u, second_owner):
        if second_owner:
            return pl.BlockSpec((None, R, LANES), lambda k, d1, b1, d2, b2, t: (d2[k * G + u], 0, b2[k * G + u]))
        return pl.BlockSpec((None, R, LANES), lambda k, d1, b1, d2, b2, t: (d1[k * G + u], 0, b1[k * G + u]))

    return pl.pallas_call(
        body, name=name,
        grid_spec=pltpu.PrefetchScalarGridSpec(
            num_scalar_prefetch=5, grid=(n_blocks // G,),
            in_specs=[spec(u, False) for u in range(G)] + [spec(u, True) for u in range(G)],
            out_specs=pl.BlockSpec((R, G * LANES), lambda k, d1, b1, d2, b2, t: (0, k))),
        out_shape=jax.ShapeDtypeStruct((R, n_blocks * LANES), windows.dtype),
        compiler_params=_params(("parallel",)),
    )(dev1, blk1, dev2, blk2, two, *([windows] * (2 * G)))


def _sum8(name, g):
    _, R, C = g.shape

    def body(g_ref, out_ref):
        acc = g_ref[0]
        for d in range(1, N_DEV):
            acc = acc + g_ref[d]
        out_ref[...] = acc

    return pl.pallas_call(body, name=name, out_shape=jax.ShapeDtypeStruct((R, C), F32),
                          compiler_params=_params())(g)


def _adamw_math(w, g, m, v):
    m = ADAM_B1 * m + (1.0 - ADAM_B1) * g
    v = ADAM_B2 * v + (1.0 - ADAM_B2) * (g * g)
    m_hat = m / (1.0 - ADAM_B1 ** ADAM_STEP)
    v_hat = v / (1.0 - ADAM_B2 ** ADAM_STEP)
    delta = -ADAM_LR * (m_hat / (jnp.sqrt(v_hat) + ADAM_EPS) + ADAM_WD * w)
    return delta, m, v


def _adamw(name, w, g, m, v):
    R, C = w.shape
    tr = _row_tile(R, C, target_bytes=1024 * 1024)
    return _row_call(name, lambda *a: (_adamw_math(*a), ()), [w, g, m, v], [], [(C, F32)] * 3, [], tr)


def _adamw_from_window(name, w, m, v, window, gate, where):
    R, C = w.shape
    W = window.shape[1]
    tr = _row_tile(R, C, target_bytes=1024 * 1024)

    def body(p_ref, w_ref, m_ref, v_ref, win_ref, gate_ref, g_out, d_out, m_out, v_out):
        off, nb, hg = p_ref[0], p_ref[1], p_ref[2]
        r, c = _iota2((W, C), 0), _iota2((W, C), 1)
        pick = jnp.logical_or(jnp.logical_and(c < nb, r == c + off),
                              jnp.logical_and(c >= nb + hg, r == c - hg + off)).astype(BF16)
        r2, c2 = _iota2((LANES, C), 0), _iota2((LANES, C), 1)
        pick_gate = jnp.logical_and(r2 < hg, c2 == nb + r2).astype(BF16)
        g = _dot3(win_ref[...], pick) + _dot3(gate_ref[...], pick_gate)
        g_out[...] = g
        d_out[...], m_out[...], v_out[...] = _adamw_math(w_ref[...], g, m_ref[...], v_ref[...])

    blk = pl.BlockSpec((tr, C), lambda i, p: (i, 0))
    return pl.pallas_call(
        body, name=name,
        grid_spec=pltpu.PrefetchScalarGridSpec(
            num_scalar_prefetch=1, grid=(R // tr,),
            in_specs=[blk, blk, blk, pl.BlockSpec((tr, W), lambda i, p: (i, 0)),
                      pl.BlockSpec((tr, LANES), lambda i, p: (i, 0))],
            out_specs=[blk] * 4),
        out_shape=[jax.ShapeDtypeStruct((R, C), F32)] * 4,
        compiler_params=_params(("parallel",)),
    )(where, w, m, v, window, gate)


def _adamw_many(name, ws, gs, ms, vs):
    n = len(ws)

    def body(*refs):
        ins, outs = refs[:4 * n], refs[4 * n:]
        for k in range(n):
            res = _adamw_math(ins[k][...], ins[n + k][...], ins[2 * n + k][...], ins[3 * n + k][...])
            for t in range(3):
                outs[t * n + k][...] = res[t]

    out = pl.pallas_call(
        body, name=name, out_shape=[jax.ShapeDtypeStruct(w.shape, F32) for _ in range(3) for w in ws],
        compiler_params=_params(),
    )(*ws, *gs, *ms, *vs)
    return out[:n], out[n:2 * n], out[2 * n:]


def _pack(parts):
    flat = []
    total = 0
    for a in parts:
        n = math.prod(a.shape)
        flat.append(a.reshape(-1).astype(F32))
        if n % LANES:
            flat.append(jnp.zeros((-n % LANES,), F32))
        total += n + (-n % LANES)
    if total % (8 * LANES):
        flat.append(jnp.zeros((-total % (8 * LANES),), F32))
    return jnp.concatenate(flat).reshape(-1, LANES)


def _unpack(packed, shapes):
    out = []
    r = 0
    for shp in shapes:
        n = math.prod(shp)
        nr = -(-n // LANES)
        out.append(packed[r:r + nr].reshape(-1)[:n].reshape(shp))
        r += nr
    return out


def kernel(x, norm_mix_g, w_in, b_f, gmlp_ln_g, gmlp_ln_b, w_s, b_s, attn_out_g, gmlp_out_g, w_out, norm_ffn_g, w_ff1, w_ff2, norm_final_g, loss_target, m_norm_mix_g, m_w_in, m_b_f, m_gmlp_ln_g, m_gmlp_ln_b, m_w_s, m_b_s, m_attn_out_g, m_gmlp_out_g, m_w_out, m_norm_ffn_g, m_w_ff1, m_w_ff2, m_norm_final_g, v_norm_mix_g, v_w_in, v_b_f, v_gmlp_ln_g, v_gmlp_ln_b, v_w_s, v_b_s, v_attn_out_g, v_gmlp_out_g, v_w_out, v_norm_ffn_g, v_w_ff1, v_w_ff2, v_norm_final_g):
    S, D = x.shape[1], x.shape[2]
    H = b_f.shape[1]
    DA = H * HEAD_DIM
    DG = gmlp_ln_g.shape[1]
    DQKV = 3 * DA
    DMAIN = DQKV + 2 * DG
    DIN = DMAIN + H
    DFF = w_ff1.shape[2] * N_DEV
    w_in_cols = w_in.shape[2]
    assert DIN == w_in_cols * N_DEV and DA == DG and D == DA + DG

    T_ATT = min(T_ATT_MAX, S)
    TR = min(TR_MAX, S)

    x0 = x[0]
    tgt = loss_target[0]
    g_final = norm_final_g.reshape(1, D)

    FB = DFF // N_DEV
    x_pos, y_pos, c_pos = _me()
    me_idx = 4 * x_pos + 2 * y_pos + c_pos

    WW = -(-(w_in_cols + LANES - 1) // LANES) * LANES
    to_main = lambda col: col if col <= DQKV else max(DQKV, col - H)
    lo = [to_main(n * w_in_cols) for n in range(N_DEV)]
    hi = [to_main((n + 1) * w_in_cols) for n in range(N_DEV)]
    starts = [v // LANES * LANES for v in lo]
    gate_dev = DQKV // w_in_cols
    n_before = DQKV - gate_dev * w_in_cols
    g0 = lo[gate_dev] - starts[gate_dev]
    stash = -(-(g0 + w_in_cols - H) // LANES) * LANES
    assert all(hi[n] <= starts[n] + WW <= DMAIN for n in range(N_DEV))
    assert gate_dev * w_in_cols <= DQKV and DQKV + H <= (gate_dev + 1) * w_in_cols and stash + LANES <= WW
    shard = w_in[0].astype(BF16)

    def my_window(n):
        if n != gate_dev:
            return lambda s: jnp.pad(s, ((0, 0), (lo[n] - starts[n], WW - w_in_cols - (lo[n] - starts[n]))))
        return lambda s: jnp.concatenate([
            jnp.zeros((D, g0), BF16), s[:, :n_before], s[:, n_before + H:],
            jnp.zeros((D, stash - g0 - (w_in_cols - H)), BF16), s[:, n_before:n_before + H],
            jnp.zeros((D, WW - stash - H), BF16)], axis=1)
    (windows_part,) = _run_jobs(
        "ag_w_in", [_job_gather_chips(lax.switch(me_idx, [my_window(n) for n in range(N_DEV)], shard))])[0]
    ((h,), _), ((windows,),) = _row_call(
        "rms_mix", lambda xb, g: ((_rms_fwd(xb, g),), ()), [x0], [norm_mix_g], [(D, BF16)], [], TR,
        jobs=[_job_gather_sibling(windows_part)])
    first, second = [], []
    for blk in range(DMAIN // LANES):
        c0 = blk * LANES
        owners = [(n, (c0 - starts[n]) // LANES) for n in range(N_DEV) if lo[n] < c0 + LANES and hi[n] > c0]
        assert 1 <= len(owners) <= 2
        first.append(owners[0])
        second.append(owners[1] if len(owners) == 2 else (-1, 0))
    w_main = _add_windows("w_in_windows", windows, first, second, DMAIN // LANES)
    w_f = windows[gate_dev, :, stash:stash + LANES]
    c_idx = jnp.reshape(c_pos, (1,)).astype(jnp.int32)
    p_idx = jnp.reshape(2 * x_pos + y_pos, (1,)).astype(jnp.int32)

    w_ff1_b = w_ff1[0].astype(BF16)
    (zm,), ((w_out_part,), (w_ff1_q1,)) = _mm_nn(
        "in_proj", h, w_main, [BF16], 2048, 1024, 2048,
        jobs=[_job_gather_chips(w_out[0].astype(BF16)), _job_gather_chips(w_ff1_b, part=(0, 1, 4))])
    (zf,) = _mm_nn("in_proj_f", h, w_f, [F32], 1024, LANES, 2048)
    bf_pad = jnp.pad(b_f, ((0, 0), (0, LANES - H)))
    f_row = _fgate_fwd(zf, bf_pad)
    NB = S // T_ATT
    f_row3 = f_row.reshape(H, NB, 1, T_ATT)
    (attn, lse_row3), ((w_out_all,), (w_ff1_part,)) = _attn2_fwd(
        zm, f_row3, T_ATT, jobs=[_job_gather_sibling(w_out_part),
                                 _job_gather_chips(w_ff1_b, part=(1, 4, 4), into=w_ff1_q1)])
    w_out_full = w_out_all.reshape(D, D)
    bs_col = b_s[0].reshape(H, CHUNK, 1)
    gm, merged = _gmlp_fwd(zm, gmlp_ln_g, gmlp_ln_b, w_s[0], bs_col, attn, attn_out_g, gmlp_out_g, TR)

    w_ff2_b = w_ff2[0].astype(BF16)
    TMR = min(TMR_MAX, S)

    def out_proj_fn(acc, res, g):
        xb = acc + res
        return (xb, _rms_fwd(xb, g)), ()
    ((x1, h2), _), ((w_ff1_all,), (w_ff2_q1,)) = _mm_rows(
        "out_proj", (S // TMR, 1), merged, pl.BlockSpec((TMR, D), lambda i, k: (i, 0)),
        w_out_full, pl.BlockSpec((D, D), lambda i, k: (0, 0)), NN, TMR, D, [x0], [norm_ffn_g],
        [(D, F32), (D, BF16)], [], out_proj_fn,
        jobs=[_job_gather_sibling(w_ff1_part), _job_gather_chips(w_ff2_b, part=(0, 1, 4))])

    tm, tn, tk = min(1024, S), min(1024, FB), min(2048, D)
    tm1 = min(2048, S)
    o_spec = pl.BlockSpec((tm1, tn), lambda i, j, k: (i, j))

    def relu_sq(acc):
        a = jnp.maximum(acc, 0.0)
        return a, a * a
    nj = FB // tn
    ff2_rest = [_job_gather_chips(w_ff2_b, part=(1, 4, 4), into=w_ff2_q1)]
    (a_act, a_sq), ((w_ff2_q2,),) = _mm(
        "ff1", (S // tm1, DFF // tn, D // tk), h2, pl.BlockSpec((tm1, tk), lambda i, j, k: (i, k)),
        w_ff1_all, pl.BlockSpec((None, tk, tn), lambda i, j, k: (j // nj, k, j % nj)), NN, (tm1, tn),
        [jax.ShapeDtypeStruct((S, DFF), BF16)] * 2, [o_spec] * 2, epilogue=relu_sq, jobs=ff2_rest)
    (w_ff2_all,) = _run_jobs("ag_w_ff2_sibling", [_job_gather_sibling(w_ff2_q2)])[0]
    w_ff2_full = w_ff2_all.reshape(DFF, D)
    def head_fn(acc, res, t, g):
        xb = acc + res
        rstd = lax.rsqrt(jnp.mean(xb * xb, axis=-1, keepdims=True) + EPS)
        xhat = xb * rstd
        err = xhat * g - t
        loss = 0.5 * jnp.sum(jnp.mean(err * err, axis=-1, keepdims=True), axis=0, keepdims=True)
        dy = err * (1.0 / D)
        dg = jnp.sum(dy * xhat, axis=0, keepdims=True)
        dxhat = dy * g
        dx = rstd * (dxhat - xhat * jnp.mean(dxhat * xhat, axis=-1, keepdims=True))
        return (dx, dx), (dg, jnp.broadcast_to(loss, (1, LANES)))
    tk_ff2 = min(1024, DFF)
    (dx2, dx2_b), (dg_final, loss_part) = _mm_rows(
        "ff2", (S // TMR, DFF // tk_ff2), a_sq, pl.BlockSpec((TMR, tk_ff2), lambda i, k: (i, k)),
        w_ff2_full, pl.BlockSpec((tk_ff2, D), lambda i, k: (k, 0)), NN, TMR, D, [x1, tgt], [g_final],
        [(D, F32), (D, BF16)], [D, LANES], head_fn)

    (da,) = _mm_nt("ff2_dx", dx2_b, w_ff2_full, [BF16], 2048, 1024, 2048, extras=[a_act],
                   epilogue=lambda acc, a: (2.0 * a.astype(F32) * acc,))
    dw_ff2, dw_ff2_b = _mm_tn("ff2_dw", a_sq, dx2_b, [F32, BF16], 1024, 2048, 1024)
    tm2, tk2 = min(2048, D), min(1024, S)
    dw1_spec = pl.BlockSpec((None, tm2, FB), lambda i, j, k: (j, i, 0))
    (dw_ff1, dw_ff1_b), ((r1_ff2,),) = _mm(
        "ff1_dw", (D // tm2, DFF // FB, S // tk2), h2, pl.BlockSpec((tk2, tm2), lambda i, j, k: (k, i)),
        da, pl.BlockSpec((tk2, FB), lambda i, j, k: (k, j)), TN, (tm2, FB),
        [jax.ShapeDtypeStruct((N_DEV, D, FB), F32), jax.ShapeDtypeStruct((N_DEV, D, FB), BF16)], [dw1_spec] * 2,
        epilogue=lambda acc: (acc, acc), jobs=[_job_scatter_sibling(dw_ff2_b.reshape(4, 2, FB, D))])
    hb_ff2 = _rs_add1("rs_add1_w_ff2", dw_ff2.reshape(4, 2, FB, D), r1_ff2, c_idx)
    def ffn_bwd_fn(dh, xb, dres, g):
        dx, dg = _rms_bwd(dh, xb, g)
        dx = dx + dres
        return (dx, dx), (dg,)
    tkb = min(1024, FB)
    nkb = FB // tkb
    ((dx1, dx1_b), (dg_ffn,)), ((r2_ff2,), (r1_ff1,)) = _mm_rows(
        "ff1_dx", (S // TMR, DFF // tkb), da, pl.BlockSpec((TMR, tkb), lambda i, k: (i, k)),
        w_ff1_all, pl.BlockSpec((None, D, tkb), lambda i, k: (k // nkb, 0, k % nkb)), NT, TMR, D, [x1, dx2],
        [norm_ffn_g], [(D, F32), (D, BF16)], [D], ffn_bwd_fn,
        jobs=[_job_scatter_chips(hb_ff2), _job_scatter_sibling(dw_ff1_b.reshape(4, 2, D, FB))])
    g_w_ff2 = _rs_add2_own("rs_add2_w_ff2", dw_ff2.reshape(4, 2, FB, D), r1_ff2, r2_ff2, c_idx, p_idx)
    hb_ff1 = _rs_add1("rs_add1_w_ff1", dw_ff1.reshape(4, 2, D, FB), r1_ff1, c_idx)

    def merge_bwd_fn(dm, a, g, ga, gg):
        da_, dga = _rms_bwd(dm[:, :DA], a, ga)
        dg_, dgg = _rms_bwd(dm[:, DA:], g, gg)
        return (da_, dg_), (dga, dgg)
    (dattn, dgm), (dg_attn, dg_gmlp) = _mm_rows(
        "out_proj_dx", (S // TMR, 1), dx1_b, pl.BlockSpec((TMR, D), lambda i, k: (i, 0)),
        w_out_full, pl.BlockSpec((D, D), lambda i, k: (0, 0)), NT, TMR, D, [attn, gm], [attn_out_g, gmlp_out_g],
        [(DA, F32), (DG, F32)], [DA, DG], merge_bwd_fn)
    dw_out, dw_out_b = _mm_tn("out_proj_dw", merged, dx1_b, [F32, BF16], 2048, 1024, 1024)

    w_st = jnp.swapaxes(w_s[0], 1, 2)
    dzu, dzv, dw_s, dbs_col, dln_g, dln_b = _gmlp_bwd(dgm, zm, gmlp_ln_g, gmlp_ln_b, w_s[0], w_st, bs_col, TR)

    delta_row3 = _attn_delta(dattn, attn, TR).reshape(H, NB, 1, T_ATT)
    (dq, ds_rowsum), ((r2_ff1_a,), (r1_out,)) = _attn2_bwd_dq(
        zm, dattn, f_row3, lse_row3, delta_row3, T_ATT,
        jobs=[_job_scatter_chips(hb_ff1, part=(0, 5, 8)),
              _job_scatter_sibling(dw_out_b.reshape(4, 2, D // N_DEV, D))])
    hb_out = _rs_add1("rs_add1_w_out", dw_out.reshape(4, 2, D // N_DEV, D), r1_out, c_idx)
    (dk, dv, df_row3), ((r2_ff1,), (r2_out,)) = _attn2_bwd_dkv(
        zm, dattn, f_row3, lse_row3, delta_row3, ds_rowsum, T_ATT,
        jobs=[_job_scatter_chips(hb_ff1, part=(5, 8, 8), into=r2_ff1_a), _job_scatter_chips(hb_out)])
    g_w_ff1 = _rs_add2_own("rs_add2_w_ff1", dw_ff1.reshape(4, 2, D, FB), r1_ff1, r2_ff1, c_idx, p_idx)
    g_w_out = _rs_add2_own("rs_add2_w_out", dw_out.reshape(4, 2, D // N_DEV, D), r1_out, r2_out, c_idx, p_idx)
    dzf, dbf = _fgate_bwd(df_row3.reshape(H, S), zf, bf_pad)

    dz_main = jnp.concatenate([dq, dk, dv, dzu, dzv], axis=1)
    dw_main, dw_main_b = _mm_tn("in_proj_dw", h, dz_main, [F32, BF16], 2048, 1024, 1024)
    (dw_f,), ((r1_in,),) = _mm_tn("in_proj_f_dw", h, dzf, [F32], 2048, LANES, 1024,
                                  jobs=[_job_scatter_sibling_windows(dw_main_b, starts, WW)])
    first_blocks = jnp.stack([jnp.where(c_pos == 0, starts[2 * p], starts[2 * p + 1]) // LANES
                              for p in range(4)]).astype(jnp.int32)
    hb_in = _rs_add1_windows("rs_add1_w_in", dw_main, r1_in, first_blocks)

    def mix_bwd_fn(dh_main, dz_gate, xb, dres, g, w_gate):
        dx, dg = _rms_bwd(dh_main + _dot(dz_gate, w_gate, NT), xb, g)
        return (dx + dres,), (dg,)
    tk_in = min(1024, DMAIN)
    ((grad_x,), (dg_mix,)), ((r2_in,),) = _mm_rows(
        "in_proj_dx", (S // TMR, DMAIN // tk_in), dz_main, pl.BlockSpec((TMR, tk_in), lambda i, k: (i, k)),
        w_main, pl.BlockSpec((D, tk_in), lambda i, k: (0, k)), NT, TMR, D, [dzf, x0, dx1], [norm_mix_g, w_f],
        [(D, F32)], [D], mix_bwd_fn, jobs=[_job_scatter_chips(hb_in)])
    g_window = _rs_add2_own_window("rs_add2_w_in", dw_main, r1_in, r2_in, first_blocks, p_idx)

    small_shapes = [norm_mix_g.shape, b_f.shape, gmlp_ln_g.shape, gmlp_ln_b.shape, w_s.shape, b_s.shape,
                    attn_out_g.shape, gmlp_out_g.shape, norm_ffn_g.shape, norm_final_g.shape]
    small_parts = [dg_mix, dbf[:, :H], dln_g, dln_b, dw_s, dbs_col, dg_attn, dg_gmlp, dg_ffn, dg_final]
    g_small = _sum8("small_sum", _all_gather("ag_small", _pack(small_parts + [dw_f[:, :H], loss_part])))
    *gs, g_gate, loss_sum = _unpack(g_small, small_shapes + [(D, H), (1, LANES)])
    two_d = lambda a: a.reshape(1, -1) if a.ndim == 1 else a
    ds, nms, nvs = _adamw_many(
        "adamw_small",
        [two_d(a) for a in (norm_mix_g, b_f, gmlp_ln_g, gmlp_ln_b, w_s, b_s, attn_out_g, gmlp_out_g, norm_ffn_g,
                            norm_final_g)],
        [two_d(a) for a in gs],
        [two_d(a) for a in (m_norm_mix_g, m_b_f, m_gmlp_ln_g, m_gmlp_ln_b, m_w_s, m_b_s, m_attn_out_g, m_gmlp_out_g,
                            m_norm_ffn_g, m_norm_final_g)],
        [two_d(a) for a in (v_norm_mix_g, v_b_f, v_gmlp_ln_g, v_gmlp_ln_b, v_w_s, v_b_s, v_attn_out_g, v_gmlp_out_g,
                            v_norm_ffn_g, v_norm_final_g)])
    ds, nms, nvs = [[a.reshape(s) for a, s in zip(lst, small_shapes)] for lst in (ds, nms, nvs)]

    is_gate_dev = me_idx == gate_dev
    where = jnp.stack([sum(jnp.where(me_idx == n, lo[n] - starts[n], 0) for n in range(N_DEV)),
                       jnp.where(is_gate_dev, n_before, w_in_cols), jnp.where(is_gate_dev, H, 0)]).astype(jnp.int32)
    big = {"w_in": tuple(a[None] for a in _adamw_from_window(
        "adamw_w_in", w_in[0], m_w_in[0], v_w_in[0], g_window, jnp.pad(g_gate, ((0, 0), (0, LANES - H))), where))}
    for nm, w, g, m, v in (("w_out", w_out, g_w_out, m_w_out, v_w_out),
                           ("w_ff1", w_ff1, g_w_ff1, m_w_ff1, v_w_ff1), ("w_ff2", w_ff2, g_w_ff2, m_w_ff2, v_w_ff2)):
        (d_, m_, v_), _ = _adamw("adamw_" + nm, w[0], g, m[0], v[0])
        big[nm] = (g[None], d_[None], m_[None], v_[None])

    loss = loss_sum[0, 0]

    def leaves(n):
        sm = (gs, ds, nms, nvs)[n]
        return [sm[0], big["w_in"][n], sm[1], sm[2], sm[3], sm[4], sm[5], sm[6], sm[7], big["w_out"][n], sm[8],
                big["w_ff1"][n], big["w_ff2"][n], sm[9]]

    return (loss, grad_x[None], *leaves(0), *leaves(1), *leaves(2), *leaves(3))
```

```python
import functools
import math

import jax
import jax.numpy as jnp
from jax import lax
from jax.experimental import pallas as pl
from jax.experimental.pallas import tpu as pltpu

F32 = jnp.float32
BF16 = jnp.bfloat16
MESH = pl.DeviceIdType.MESH

HEAD_DIM = 128
CHUNK = 128
EPS = 1e-6
LANES = 128
N_DEV = 8

ADAM_LR = 0.001
ADAM_B1 = 0.9
ADAM_B2 = 0.999
ADAM_EPS = 1e-08
ADAM_WD = 0.01
ADAM_STEP = 10

VMEM_LIMIT_BYTES = 56 * 1024 * 1024
T_ATT_MAX = 1024
TR_MAX = 512
TMR_MAX = 512

NN = ((1,), (0,))
NT = ((1,), (1,))
TN = ((0,), (0,))


def _params(sem=None):
    return pltpu.CompilerParams(dimension_semantics=sem, vmem_limit_bytes=VMEM_LIMIT_BYTES)


def _dot(a, b, contract=NN):
    return lax.dot_general(a, b, (contract, ((), ())), preferred_element_type=F32)


def _dot3(x, t):
    x1 = x.astype(BF16)
    r1 = x - x1.astype(F32)
    x2 = r1.astype(BF16)
    x3 = (r1 - x2.astype(F32)).astype(BF16)
    return _dot(x1, t) + _dot(x2, t) + _dot(x3, t)


def _iota2(shape, dim):
    return lax.broadcasted_iota(jnp.int32, shape, dim)


def _row_call(name, fn, row_ins, bcast_ins, row_outs, acc_outs, tr, jobs=()):
    S = row_ins[0].shape[0]
    assert S % tr == 0
    n_ri, n_bi, n_ro, n_ao = len(row_ins), len(bcast_ins), len(row_outs), len(acc_outs)

    def body(*refs):
        ins = [r[...] for r in refs[:n_ri + n_bi]]
        ro_refs = refs[n_ri + n_bi:n_ri + n_bi + n_ro]
        ao_refs = refs[n_ri + n_bi + n_ro:]
        ro, ao = fn(*ins)
        for r, v in zip(ro_refs, ro):
            r[...] = v.astype(r.dtype)
        if n_ao:
            @pl.when(pl.program_id(0) == 0)
            def _():
                for r in ao_refs:
                    r[...] = jnp.zeros_like(r)
            for r, v in zip(ao_refs, ao):
                r[...] += v

    in_specs = [pl.BlockSpec((tr, a.shape[1]), lambda i: (i, 0)) for a in row_ins]
    in_specs += [pl.BlockSpec(a.shape, lambda i: (0, 0)) for a in bcast_ins]
    out_specs = [pl.BlockSpec((tr, d), lambda i: (i, 0)) for d, _ in row_outs]
    out_specs += [pl.BlockSpec((1, d), lambda i: (0, 0)) for d in acc_outs]
    out_shape = [jax.ShapeDtypeStruct((S, d), dt) for d, dt in row_outs]
    out_shape += [jax.ShapeDtypeStruct((1, d), F32) for d in acc_outs]
    outs, job_res = _carry_call(
        body, name=name, grid=(S // tr,), in_specs=in_specs, out_specs=out_specs, out_shape=out_shape,
        scratch_shapes=[], semantics=("arbitrary",) if n_ao else ("parallel",), args=list(row_ins) + list(bcast_ins),
        jobs=jobs)
    res = (outs[:n_ro], outs[n_ro:])
    return (res, job_res) if jobs else res


def _rms_fwd(x, g):
    rstd = lax.rsqrt(jnp.mean(x * x, axis=-1, keepdims=True) + EPS)
    return x * rstd * g


def _rms_bwd(dy, x, g):
    rstd = lax.rsqrt(jnp.mean(x * x, axis=-1, keepdims=True) + EPS)
    xhat = x * rstd
    dg = jnp.sum(dy * xhat, axis=0, keepdims=True)
    dxhat = dy * g
    dx = rstd * (dxhat - xhat * jnp.mean(dxhat * xhat, axis=-1, keepdims=True))
    return dx, dg


_GELU_C = math.sqrt(2.0 / math.pi)


def _gelu(x):
    return 0.5 * x * (1.0 + jnp.tanh(_GELU_C * (x + 0.044715 * (x * x * x))))


def _gelu_grad(x):
    t = jnp.tanh(_GELU_C * (x + 0.044715 * (x * x * x)))
    return 0.5 * (1.0 + t) + 0.5 * x * (1.0 - t * t) * (_GELU_C * (1.0 + 3.0 * 0.044715 * (x * x)))


def _me():
    return lax.axis_index("x"), lax.axis_index("y"), lax.axis_index("c")


def _other_chips(x, y):
    return [(1 - x, y), (x, 1 - y), (1 - x, 1 - y)]


_ANY = pl.BlockSpec(memory_space=pl.ANY)


class _Job:
    def __init__(self, ins, outs, n_sems, make, aliases=None):
        self.ins, self.outs, self.n_sems, self.make, self.aliases = ins, outs, n_sems, make, aliases or {}


def _job_gather_chips(blk, part=(0, 1, 1), into=None):
    R, C = blk.shape
    nr = R // part[2]
    rows = pl.ds(part[0] * nr, (part[1] - part[0]) * nr)

    def make(ins, outs, send_sems, recv_sems, base):
        x_ref, (out_ref,) = ins[0], outs
        x, y, c = _me()
        mine = 4 * x + 2 * y + c
        targets = [(x, y, 1 - c)] + [(cx, cy, c) for cx, cy in _other_chips(x, y)]

        def copy(k, slab, to):
            return pltpu.make_async_remote_copy(
                src_ref=x_ref.at[rows, :], dst_ref=out_ref.at[slab, rows, :], send_sem=send_sems.at[base + k],
                recv_sem=recv_sems.at[base + k], device_id=to, device_id_type=MESH)

        starts = [copy(k, mine, to) for k, to in enumerate(targets)]
        arrivals = [copy(k, 4 * tx + 2 * ty + tc, (tx, ty, tc)) for k, (tx, ty, tc) in enumerate(targets)]
        local = [pltpu.make_async_copy(x_ref.at[rows, :], out_ref.at[mine, rows, :], send_sems.at[base + 4])]
        return starts, arrivals, local

    out = jax.ShapeDtypeStruct((N_DEV, R, C), blk.dtype)
    if into is None:
        return _Job([blk], [out], 5, make)
    return _Job([blk, into], [out], 5, make, aliases={1: 0})


def _job_gather_sibling(part):
    def make(ins, outs, send_sems, recv_sems, base):
        (out_ref,) = outs
        x, y, c = _me()

        def copy(k, slab):
            return pltpu.make_async_remote_copy(
                src_ref=out_ref.at[slab], dst_ref=out_ref.at[slab], send_sem=send_sems.at[base + k],
                recv_sem=recv_sems.at[base + k], device_id=(x, y, 1 - c), device_id_type=MESH)

        chips = _other_chips(x, y)
        starts = [copy(k, 4 * cx + 2 * cy + c) for k, (cx, cy) in enumerate(chips)]
        arrivals = [copy(k, 4 * cx + 2 * cy + (1 - c)) for k, (cx, cy) in enumerate(chips)]
        return starts, arrivals, []

    return _Job([part], [jax.ShapeDtypeStruct(part.shape, part.dtype)], 3, make, aliases={0: 0})


def _job_scatter_sibling(gb):
    _, _, R, C = gb.shape

    def make(ins, outs, send_sems, recv_sems, base):
        (g_ref,), (recv_ref,) = ins, outs
        x, y, c = _me()
        copies = [pltpu.make_async_remote_copy(
            src_ref=g_ref.at[p, 1 - c], dst_ref=recv_ref.at[p], send_sem=send_sems.at[base + p],
            recv_sem=recv_sems.at[base + p], device_id=(x, y, 1 - c), device_id_type=MESH) for p in range(4)]
        return copies, copies, []

    return _Job([gb], [jax.ShapeDtypeStruct((4, R, C), gb.dtype)], 4, make)


def _job_scatter_sibling_windows(gb, starts, width):
    R, _ = gb.shape

    def make(ins, outs, send_sems, recv_sems, base):
        (g_ref,), (recv_ref,) = ins, outs
        x, y, c = _me()
        copies = []
        for p in range(4):
            start = pl.multiple_of(jnp.where(c == 0, starts[2 * p + 1], starts[2 * p]), LANES)
            copies.append(pltpu.make_async_remote_copy(
                src_ref=g_ref.at[:, pl.ds(start, width)], dst_ref=recv_ref.at[p], send_sem=send_sems.at[base + p],
                recv_sem=recv_sems.at[base + p], device_id=(x, y, 1 - c), device_id_type=MESH))
        return copies, copies, []

    return _Job([gb], [jax.ShapeDtypeStruct((4, R, width), gb.dtype)], 4, make)


def _job_scatter_chips(hb, part=(0, 1, 1), into=None):
    _, R, C = hb.shape
    nr = R // part[2]
    rows = pl.ds(part[0] * nr, (part[1] - part[0]) * nr)

    def make(ins, outs, send_sems, recv_sems, base):
        h_ref, (recv_ref,) = ins[0], outs
        x, y, c = _me()
        copies = [pltpu.make_async_remote_copy(
            src_ref=h_ref.at[2 * cx + cy, rows, :], dst_ref=recv_ref.at[n, rows, :], send_sem=send_sems.at[base + n],
            recv_sem=recv_sems.at[base + n], device_id=(cx, cy, c), device_id_type=MESH)
            for n, (cx, cy) in enumerate(_other_chips(x, y))]
        return copies, copies, []

    out = jax.ShapeDtypeStruct((3, R, C), hb.dtype)
    if into is None:
        return _Job([hb], [out], 3, make)
    return _Job([hb, into], [out], 3, make, aliases={1: 0})


def _carry_call(body, *, name, grid, in_specs, out_specs, out_shape, scratch_shapes, semantics, args, jobs=()):
    jobs = list(jobs)
    n_in, n_out, n_scr = len(in_specs), len(out_specs), len(scratch_shapes)
    j_ins = [a for j in jobs for a in j.ins]
    j_outs = [o for j in jobs for o in j.outs]
    n_sems = sum(j.n_sems for j in jobs)
    aliases = {}
    i0, o0 = n_in, n_out
    for j in jobs:
        for a, b in j.aliases.items():
            aliases[i0 + a] = o0 + b
        i0 += len(j.ins)
        o0 += len(j.outs)

    def full_body(*refs):
        ins = refs[:n_in]
        jin = refs[n_in:n_in + len(j_ins)]
        outs = refs[n_in + len(j_ins):n_in + len(j_ins) + n_out]
        jout = refs[n_in + len(j_ins) + n_out:n_in + len(j_ins) + n_out + len(j_outs)]
        scr = refs[n_in + len(j_ins) + n_out + len(j_outs):]
        if jobs:
            send_sems, recv_sems = scr[n_scr], scr[n_scr + 1]
            starts, arrivals, local = [], [], []
            base = i0 = o0 = 0
            for j in jobs:
                s, a, l = j.make(jin[i0:i0 + len(j.ins)], jout[o0:o0 + len(j.outs)], send_sems, recv_sems, base)
                starts += s
                arrivals += a
                local += l
                base += j.n_sems
                i0 += len(j.ins)
                o0 += len(j.outs)
            pids = [pl.program_id(d) for d in range(len(grid))]
            first = functools.reduce(jnp.logical_and, [p == 0 for p in pids])
            last = functools.reduce(jnp.logical_and, [p == n - 1 for p, n in zip(pids, grid)])

            @pl.when(first)
            def _():
                for cp in local + starts:
                    cp.start()

        body(*ins, *outs, *scr[:n_scr])

        if jobs:
            @pl.when(last)
            def _():
                for cp in arrivals:
                    cp.wait_recv()
                for cp in starts:
                    cp.wait_send()
                for cp in local:
                    cp.wait()

    sems = [pltpu.SemaphoreType.DMA((n_sems,)), pltpu.SemaphoreType.DMA((n_sems,))] if jobs else []
    res = pl.pallas_call(
        full_body, name=name, grid=grid,
        in_specs=list(in_specs) + [_ANY] * len(j_ins),
        out_specs=list(out_specs) + [_ANY] * len(j_outs),
        out_shape=list(out_shape) + j_outs,
        scratch_shapes=list(scratch_shapes) + sems,
        input_output_aliases=aliases,
        compiler_params=_params(("arbitrary",) * len(grid) if jobs else semantics),
    )(*args, *j_ins)
    body_res, job_res = res[:n_out], res[n_out:]
    per_job = []
    for j in jobs:
        per_job.append(job_res[:len(j.outs)])
        job_res = job_res[len(j.outs):]
    return body_res, per_job


def _run_jobs(name, jobs):
    def body(done_ref):
        done_ref[...] = jnp.zeros_like(done_ref)

    return _carry_call(body, name=name, grid=(1,), in_specs=[], out_specs=[pl.BlockSpec((8, LANES), lambda i: (0, 0))],
                       out_shape=[jax.ShapeDtypeStruct((8, LANES), F32)], scratch_shapes=[], semantics=("arbitrary",),
                       args=[], jobs=jobs)[1]


def _mm(name, grid, a, a_spec, b, b_spec, contract, acc_shape, out_shape, out_specs, extras=(), epilogue=None, jobs=(),
        a_fn=lambda t: t):
    nk = grid[2]
    n_e = len(extras)
    n_o = len(out_shape)
    if epilogue is None:
        epilogue = lambda acc: (acc,)

    def body(a_ref, b_ref, *rest):
        e_refs = rest[:n_e]
        o_refs = rest[n_e:n_e + n_o]

        def finish(total):
            res = epilogue(total, *[r[...] for r in e_refs])
            for o, r in zip(o_refs, res):
                o[...] = r.astype(o.dtype)

        if nk == 1:
            finish(_dot(a_fn(a_ref[...]), b_ref[...], contract))
            return
        acc = rest[n_e + n_o]
        k = pl.program_id(2)

        @pl.when(k == 0)
        def _():
            acc[...] = _dot(a_fn(a_ref[...]), b_ref[...], contract)

        @pl.when(jnp.logical_and(k > 0, k < nk - 1))
        def _():
            acc[...] += _dot(a_fn(a_ref[...]), b_ref[...], contract)

        @pl.when(k == nk - 1)
        def _():
            finish(acc[...] + _dot(a_fn(a_ref[...]), b_ref[...], contract))

    outs, job_res = _carry_call(
        body, name=name, grid=grid, in_specs=[a_spec, b_spec] + [s for _, s in extras],
        out_specs=list(out_specs), out_shape=list(out_shape),
        scratch_shapes=[pltpu.VMEM(acc_shape, F32)] if nk > 1 else [],
        semantics=("parallel", "parallel", "arbitrary"), args=[a, b] + [e for e, _ in extras], jobs=jobs)
    return (outs, job_res) if jobs else outs


def _mm_rows(name, grid, a, a_spec, b, b_spec, contract, tm, n, row_extras, bcast, row_outs, acc_outs, epilogue, jobs=(),
             a_fn=lambda t: t):
    nk = grid[1]
    M = grid[0] * tm
    n_x, n_b, n_ro, n_ao = len(row_extras), len(bcast), len(row_outs), len(acc_outs)

    def body(a_ref, b_ref, *rest):
        x_refs = rest[:n_x + n_b]
        rest = rest[n_x + n_b:]
        ro_refs = rest[:n_ro]
        ao_refs = rest[n_ro:n_ro + n_ao]
        i = pl.program_id(0)

        def finish(total):
            ro, ao = epilogue(total, *[r[...] for r in x_refs])
            for r, v in zip(ro_refs, ro):
                r[...] = v.astype(r.dtype)
            if n_ao:
                @pl.when(i == 0)
                def _():
                    for r, v in zip(ao_refs, ao):
                        r[...] = v

                @pl.when(i > 0)
                def _():
                    for r, v in zip(ao_refs, ao):
                        r[...] += v

        if nk == 1:
            finish(_dot(a_fn(a_ref[...]), b_ref[...], contract))
            return
        acc = rest[n_ro + n_ao]
        k = pl.program_id(1)

        @pl.when(k == 0)
        def _():
            acc[...] = _dot(a_fn(a_ref[...]), b_ref[...], contract)

        @pl.when(jnp.logical_and(k > 0, k < nk - 1))
        def _():
            acc[...] += _dot(a_fn(a_ref[...]), b_ref[...], contract)

        @pl.when(k == nk - 1)
        def _():
            finish(acc[...] + _dot(a_fn(a_ref[...]), b_ref[...], contract))

    in_specs = [a_spec, b_spec] + [pl.BlockSpec((tm, x.shape[1]), lambda i, k: (i, 0)) for x in row_extras]
    in_specs += [pl.BlockSpec(x.shape, lambda i, k: (0,) * x.ndim) for x in bcast]
    out_specs = [pl.BlockSpec((tm, w), lambda i, k: (i, 0)) for w, _ in row_outs]
    out_specs += [pl.BlockSpec((1, w), lambda i, k: (0, 0)) for w in acc_outs]
    out_shape = [jax.ShapeDtypeStruct((M, w), dt) for w, dt in row_outs]
    out_shape += [jax.ShapeDtypeStruct((1, w), F32) for w in acc_outs]
    outs, job_res = _carry_call(
        body, name=name, grid=grid, in_specs=in_specs, out_specs=out_specs, out_shape=out_shape,
        scratch_shapes=[pltpu.VMEM((tm, n), F32)] if nk > 1 else [],
        semantics=("arbitrary", "arbitrary"), args=[a, b] + list(row_extras) + list(bcast), jobs=jobs)
    res = (outs[:n_ro], outs[n_ro:])
    return (res, job_res) if jobs else res


def _mm_nn(name, a, b, out_dtypes, tm, tn, tk, extras=(), epilogue=None, jobs=()):
    M, K = a.shape
    N = b.shape[1]
    tm, tn, tk = min(tm, M), min(tn, N), min(tk, K)
    o_spec = pl.BlockSpec((tm, tn), lambda i, j, k: (i, j))
    return _mm(name, (M // tm, N // tn, K // tk),
               a, pl.BlockSpec((tm, tk), lambda i, j, k: (i, k)),
               b, pl.BlockSpec((tk, tn), lambda i, j, k: (k, j)), NN, (tm, tn),
               [jax.ShapeDtypeStruct((M, N), dt) for dt in out_dtypes], [o_spec] * len(out_dtypes),
               [(e, o_spec) for e in extras], epilogue, jobs)


def _mm_nt(name, a, b, out_dtypes, tm, tn, tk, extras=(), epilogue=None, jobs=()):
    M, K = a.shape
    N = b.shape[0]
    tm, tn, tk = min(tm, M), min(tn, N), min(tk, K)
    o_spec = pl.BlockSpec((tm, tn), lambda i, j, k: (i, j))
    return _mm(name, (M // tm, N // tn, K // tk),
               a, pl.BlockSpec((tm, tk), lambda i, j, k: (i, k)),
               b, pl.BlockSpec((tn, tk), lambda i, j, k: (j, k)), NT, (tm, tn),
               [jax.ShapeDtypeStruct((M, N), dt) for dt in out_dtypes], [o_spec] * len(out_dtypes),
               [(e, o_spec) for e in extras], epilogue, jobs)


def _mm_tn(name, a, b, out_dtypes, tm, tn, tk, jobs=(), a_fn=lambda t: t):
    K, M = a.shape
    N = b.shape[1]
    tm, tn, tk = min(tm, M), min(tn, N), min(tk, K)
    o_spec = pl.BlockSpec((tm, tn), lambda i, j, k: (i, j))
    return _mm(name, (M // tm, N // tn, K // tk),
               a, pl.BlockSpec((tk, tm), lambda i, j, k: (k, i)),
               b, pl.BlockSpec((tk, tn), lambda i, j, k: (k, j)), TN, (tm, tn),
               [jax.ShapeDtypeStruct((M, N), dt) for dt in out_dtypes], [o_spec] * len(out_dtypes),
               epilogue=lambda acc: (acc,) * len(out_dtypes), jobs=jobs, a_fn=a_fn)


def _fgate_fwd(zf, bf):
    S = zf.shape[0]
    nc = S // CHUNK

    def body(zf_ref, bf_ref, f_ref):
        upper = (_iota2((CHUNK, CHUNK), 0) <= _iota2((CHUNK, CHUNK), 1)).astype(BF16)
        carry = jnp.zeros((8, 1), F32)
        for c in range(nc):
            t = zf_ref[c * CHUNK:(c + 1) * CHUNK, :] + bf_ref[...]
            lf = jnp.minimum(t, 0.0) - jnp.log(1.0 + jnp.exp(-jnp.abs(t)))
            lf_rows = lf.T[0:8, :]
            f_ref[:, c * CHUNK:(c + 1) * CHUNK] = (_dot3(lf_rows, upper) + carry) * LOG2E
            carry = carry + jnp.sum(lf_rows, axis=-1, keepdims=True)

    return pl.pallas_call(
        body, name="fgate_fwd", out_shape=jax.ShapeDtypeStruct((8, S), F32),
        compiler_params=_params(),
    )(zf, bf)


def _fgate_bwd(df, zf, bf):
    S = zf.shape[0]
    nc = S // CHUNK

    def body(df_ref, zf_ref, bf_ref, dzf_ref, dbf_ref):
        lower = (_iota2((CHUNK, CHUNK), 0) >= _iota2((CHUNK, CHUNK), 1)).astype(BF16)
        carry = jnp.zeros((8, 1), F32)
        dbf = jnp.zeros((1, LANES), F32)
        for c in reversed(range(nc)):
            sl = slice(c * CHUNK, (c + 1) * CHUNK)
            df = df_ref[:, sl]
            r = _dot3(df, lower) + carry
            carry = carry + jnp.sum(df, axis=-1, keepdims=True)
            r_cols = jnp.concatenate([r, jnp.zeros((CHUNK - 8, CHUNK), F32)], axis=0).T
            t = zf_ref[sl, :] + bf_ref[...]
            dz = r_cols * (1.0 / (1.0 + jnp.exp(t)))
            dzf_ref[sl, :] = dz.astype(BF16)
            dbf = dbf + jnp.sum(dz, axis=0, keepdims=True)
        dbf_ref[...] = dbf

    return pl.pallas_call(
        body, name="fgate_bwd",
        out_shape=[jax.ShapeDtypeStruct((S, LANES), BF16), jax.ShapeDtypeStruct((1, LANES), F32)],
        compiler_params=_params(),
    )(df, zf, bf)


_NEG = -1e30
LOG2E = 1.4426950408889634
N_SPLIT = 8
N_SPLIT_DIAG = 4
DIAG_STEP = 1024


def _attn_consts(T):
    rows, cols = _iota2((T, T), 0), _iota2((T, T), 1)
    return cols <= rows, rows <= cols


def _col_to_row(col):
    wide = jnp.broadcast_to(col, (col.shape[0], LANES))
    return jnp.concatenate([wide[r:r + LANES, :].T[0:1, :] for r in range(0, col.shape[0], LANES)], axis=1)


def _row_to_col(row):
    tall = jnp.broadcast_to(row, (LANES, row.shape[1]))
    return jnp.concatenate([tall[:, c:c + LANES].T[:, 0:1] for c in range(0, row.shape[1], LANES)], axis=0)


def _attn2_fwd(zm, f2row, T, jobs=()):
    S = zm.shape[0]
    H = f2row.shape[0]
    nb = S // T
    c2 = LOG2E / math.sqrt(HEAD_DIM)

    def body(q_ref, k_ref, v_ref, fk_ref, o_ref, lse_ref, vaug_s, fq_ref):
        i = pl.program_id(1)

        @pl.when(i == 0)
        def _():
            vaug_s[:, :HEAD_DIM] = v_ref[...]
            vaug_s[:, HEAD_DIM:] = jnp.ones((S, HEAD_DIM), BF16)

        fq_ref[...] = _row_to_col(fk_ref[i])
        keep = _attn_consts(T)[0]
        TH = T // N_SPLIT

        def block(j, diagonal, state):
            r0 = pl.multiple_of(j * T, T)
            fk = fk_ref[j]
            new = []
            for g, (m_old, acc) in enumerate(state):
                rows = slice(g * TH, (g + 1) * TH)
                nk = min(T, -(-(g + 1) * TH // DIAG_STEP) * DIAG_STEP) if diagonal else T
                s = _dot(q_ref[rows, :], k_ref[pl.ds(r0, nk), :], NT) * c2 + (fq_ref[rows, :] - fk[:, :nk])
                if diagonal:
                    s = jnp.where(keep[rows, :nk], s, _NEG)
                m_new = jnp.maximum(m_old, jnp.max(s, axis=-1, keepdims=True))
                p = jnp.exp2(s - m_new).astype(BF16)
                new.append((m_new, jnp.exp2(m_old - m_new) * acc + _dot(p, vaug_s[pl.ds(r0, nk), :])))
            return tuple(new)

        init = tuple((jnp.full((TH, 1), _NEG, F32), jnp.zeros((TH, 2 * HEAD_DIM), F32)) for _ in range(N_SPLIT))
        state = lax.fori_loop(0, i, lambda j, st: block(j, False, st), init)
        state = block(i, True, state)
        for g, (m, acc) in enumerate(state):
            rows = slice(g * TH, (g + 1) * TH)
            o_ref[rows, :] = acc[:, :HEAD_DIM] / acc[:, HEAD_DIM:]
            lse_ref[:, rows] = _col_to_row(m + jnp.log2(acc[:, HEAD_DIM:HEAD_DIM + 1]))

    nh = H
    return _carry_call(
        body, name="attn_fwd", grid=(H, nb), jobs=jobs, args=[zm, zm, zm, f2row],
        semantics=("arbitrary", "arbitrary"),
        in_specs=[
            pl.BlockSpec((T, HEAD_DIM), lambda h, i: (i, h)),
            pl.BlockSpec((S, HEAD_DIM), lambda h, i: (0, nh + h)),
            pl.BlockSpec((S, HEAD_DIM), lambda h, i: (0, 2 * nh + h)),
            pl.BlockSpec((None, nb, 1, T), lambda h, i: (h, 0, 0, 0)),
        ],
        out_specs=[pl.BlockSpec((T, HEAD_DIM), lambda h, i: (i, h)),
                   pl.BlockSpec((None, None, 1, T), lambda h, i: (h, i, 0, 0))],
        out_shape=[jax.ShapeDtypeStruct((S, H * HEAD_DIM), F32), jax.ShapeDtypeStruct((H, nb, 1, T), F32)],
        scratch_shapes=[pltpu.VMEM((S, 2 * HEAD_DIM), BF16), pltpu.VMEM((T, 1), F32)],
    )


def _attn2_bwd_dq(zm, dattn, f2row, lse2_row, delta_row, T, jobs=()):
    S = zm.shape[0]
    H = f2row.shape[0]
    nb = S // T
    scale = 1.0 / math.sqrt(HEAD_DIM)
    c2 = LOG2E * scale

    def body(q_ref, k_ref, v_ref, do_ref, fk_ref, lse_ref, dlr_ref, dq_ref, rs_ref, bias_s, do_s, dl_ref):
        i = pl.program_id(1)
        keep = _attn_consts(T)[0]
        TH = T // N_SPLIT_DIAG
        bias_s[...] = _row_to_col(fk_ref[i] - lse_ref[...])
        dl_ref[...] = _row_to_col(dlr_ref[...])
        do_s[...] = do_ref[...].astype(BF16)

        def part(rows, j, nk, state, masked):
            acc, rs = state
            r0 = pl.multiple_of(j * T, T)
            kb = k_ref[pl.ds(r0, nk), :]
            s = _dot(q_ref[rows, :], kb, NT) * c2 + (bias_s[rows, :] - fk_ref[j][:, :nk])
            if masked:
                s = jnp.where(keep[rows, :nk], s, _NEG)
            ds = jnp.exp2(s) * (_dot(do_s[rows, :], v_ref[pl.ds(r0, nk), :], NT) - dl_ref[rows, :])
            return acc + _dot(ds.astype(BF16), kb), rs + jnp.sum(ds, axis=-1, keepdims=True)

        def step(j, state):
            return part(slice(0, T), j, T, state, False)

        acc, rs = lax.fori_loop(0, i, step, (jnp.zeros((T, HEAD_DIM), F32), jnp.zeros((T, 1), F32)))
        for g in range(N_SPLIT_DIAG):
            rows = slice(g * TH, (g + 1) * TH)
            acc_g, rs_g = part(rows, i, (g + 1) * TH, (acc[rows, :], rs[rows, :]), True)
            dq_ref[rows, :] = (acc_g * scale).astype(BF16)
            rs_ref[:, rows] = _col_to_row(rs_g)

    nh = H
    row = pl.BlockSpec((None, None, 1, T), lambda h, i: (h, i, 0, 0))
    blk = pl.BlockSpec((T, HEAD_DIM), lambda h, i: (i, h))
    return _carry_call(
        body, name="attn_bwd_dq", grid=(H, nb), jobs=jobs,
        args=[zm, zm, zm, dattn, f2row, lse2_row, delta_row], semantics=("arbitrary", "arbitrary"),
        in_specs=[
            blk,
            pl.BlockSpec((S, HEAD_DIM), lambda h, i: (0, nh + h)),
            pl.BlockSpec((S, HEAD_DIM), lambda h, i: (0, 2 * nh + h)),
            blk,
            pl.BlockSpec((None, nb, 1, T), lambda h, i: (h, 0, 0, 0)),
            row, row,
        ],
        out_specs=[blk, row],
        out_shape=[jax.ShapeDtypeStruct((S, H * HEAD_DIM), BF16), jax.ShapeDtypeStruct((H, nb, 1, T), F32)],
        scratch_shapes=[pltpu.VMEM((T, 1), F32), pltpu.VMEM((T, HEAD_DIM), BF16), pltpu.VMEM((T, 1), F32)],
    )


def _attn2_bwd_dkv(zm, dattn, f2row, lse2_row, delta_row, rowsum_row, T, jobs=()):
    S = zm.shape[0]
    H = f2row.shape[0]
    nb = S // T
    scale = 1.0 / math.sqrt(HEAD_DIM)
    c2 = LOG2E * scale

    def body(q_ref, k_ref, v_ref, do_ref, fq_ref, lse_ref, dl_ref, rs_ref, dk_ref, dv_ref, df_ref, fk_ref):
        j = pl.program_id(1)
        keep = _attn_consts(T)[1]
        TH = T // N_SPLIT_DIAG
        fk_ref[...] = _row_to_col(fq_ref[j])

        def part(rows, i, c0, state, masked):
            dk, dv, df = state
            r0 = pl.multiple_of(i * T + c0, TH)
            qb = q_ref[pl.ds(r0, T - c0), :]
            do = do_ref[pl.ds(r0, T - c0), :].astype(BF16)
            bias = (fq_ref[i] - lse_ref[i])[:, c0:]
            dl = (dl_ref[i] + rs_ref[i])[:, c0:]
            st = _dot(k_ref[rows, :], qb, NT) * c2 + (bias - fk_ref[rows, :])
            if masked:
                st = jnp.where(keep[rows, c0:], st, _NEG)
            pt = jnp.exp2(st)
            dst = pt * (_dot(v_ref[rows, :], do, NT) - dl)
            return (dk + _dot(dst.astype(BF16), qb), dv + _dot(pt.astype(BF16), do),
                    df - jnp.sum(dst, axis=-1, keepdims=True))

        groups = []
        for g in range(N_SPLIT_DIAG):
            zero = (jnp.zeros((TH, HEAD_DIM), F32), jnp.zeros((TH, HEAD_DIM), F32), jnp.zeros((TH, 1), F32))
            groups.append(part(slice(g * TH, (g + 1) * TH), j, g * TH, zero, True))
        state = tuple(jnp.concatenate([grp[n] for grp in groups], axis=0) for n in range(3))
        dk, dv, df = lax.fori_loop(j + 1, nb, lambda i, st: part(slice(0, T), i, 0, st, False), state)
        dk_ref[...] = (dk * scale).astype(BF16)
        dv_ref[...] = dv.astype(BF16)
        df_ref[...] = _col_to_row(df)

    nh = H
    row = pl.BlockSpec((None, nb, 1, T), lambda h, j: (h, 0, 0, 0))
    whole = pl.BlockSpec((S, HEAD_DIM), lambda h, j: (0, h))
    kv_out = pl.BlockSpec((T, HEAD_DIM), lambda h, j: (j, h))
    return _carry_call(
        body, name="attn_bwd_dkv", grid=(H, nb), jobs=jobs,
        args=[zm, zm, zm, dattn, f2row, lse2_row, delta_row, rowsum_row],
        semantics=("arbitrary", "arbitrary"),
        in_specs=[
            whole,
            pl.BlockSpec((T, HEAD_DIM), lambda h, j: (j, nh + h)),
            pl.BlockSpec((T, HEAD_DIM), lambda h, j: (j, 2 * nh + h)),
            whole, row, row, row, row,
        ],
        out_specs=[kv_out, kv_out, pl.BlockSpec((None, None, 1, T), lambda h, j: (h, j, 0, 0))],
        out_shape=[jax.ShapeDtypeStruct((S, H * HEAD_DIM), BF16), jax.ShapeDtypeStruct((S, H * HEAD_DIM), BF16),
                   jax.ShapeDtypeStruct((H, nb, 1, T), F32)],
        scratch_shapes=[pltpu.VMEM((T, 1), F32)],
    )


def _attn_delta(dattn, attn, tr):
    S, DA = attn.shape
    H = DA // HEAD_DIM

    def body(do_ref, o_ref, out_ref):
        lo = _iota2((DA, LANES), 1) * HEAD_DIM
        sel = ((_iota2((DA, LANES), 0) >= lo) & (_iota2((DA, LANES), 0) < lo + HEAD_DIM)).astype(BF16)
        d = _dot3(do_ref[...] * o_ref[...], sel)
        for c in range(tr // CHUNK):
            out_ref[:, c * CHUNK:(c + 1) * CHUNK] = d[c * CHUNK:(c + 1) * CHUNK, :].T[0:H, :]

    return pl.pallas_call(
        body, name="attn_delta", grid=(S // tr,),
        in_specs=[pl.BlockSpec((tr, DA), lambda i: (i, 0))] * 2,
        out_specs=pl.BlockSpec((H, tr), lambda i: (0, i)),
        out_shape=jax.ShapeDtypeStruct((H, S), F32),
        compiler_params=_params(("parallel",)),
    )(dattn, attn)


def _ln_stats(x):
    mu = jnp.mean(x, axis=-1, keepdims=True)
    xc = x - mu
    rstd = lax.rsqrt(jnp.mean(xc * xc, axis=-1, keepdims=True) + EPS)
    return xc * rstd, rstd


def _tril_mask():
    return _iota2((CHUNK, CHUNK), 0) >= _iota2((CHUNK, CHUNK), 1)


def _gmlp_fwd(zm, ln_g, ln_b, w_s, bs_col, attn, attn_g, gm_g, tr):
    S = zm.shape[0]
    H = w_s.shape[0]
    DG = H * HEAD_DIM
    DA = attn.shape[1]

    def body(zu_ref, zv_ref, g_ref, b_ref, w_ref, bs_ref, a_ref, ag_ref, gg_ref, out_ref, merged_ref):
        u = _gelu(zu_ref[...].astype(F32))
        y, _ = _ln_stats(_gelu(zv_ref[...].astype(F32)))
        v = (y * g_ref[...] + b_ref[...]).astype(BF16)
        mask = _tril_mask()
        for h in range(H):
            wc = jnp.where(mask, w_ref[h], 0.0).astype(BF16)
            cs = slice(h * HEAD_DIM, (h + 1) * HEAD_DIM)
            for c in range(tr // CHUNK):
                rs = slice(c * CHUNK, (c + 1) * CHUNK)
                mix = _dot(wc, v[rs, cs]) + bs_ref[h]
                out_ref[rs, cs] = u[rs, cs] * mix
        merged_ref[:, :DA] = _rms_fwd(a_ref[...], ag_ref[...]).astype(BF16)
        merged_ref[:, DA:] = _rms_fwd(out_ref[...], gg_ref[...]).astype(BF16)

    full = lambda a: pl.BlockSpec(a.shape, lambda i: (0,) * a.ndim)
    return pl.pallas_call(
        body, name="gmlp_fwd", grid=(S // tr,),
        in_specs=[pl.BlockSpec((tr, DG), lambda i: (i, 3)), pl.BlockSpec((tr, DG), lambda i: (i, 4)),
                  full(ln_g), full(ln_b), full(w_s), full(bs_col),
                  pl.BlockSpec((tr, DA), lambda i: (i, 0)), full(attn_g), full(gm_g)],
        out_specs=[pl.BlockSpec((tr, DG), lambda i: (i, 0)), pl.BlockSpec((tr, DA + DG), lambda i: (i, 0))],
        out_shape=[jax.ShapeDtypeStruct((S, DG), F32), jax.ShapeDtypeStruct((S, DA + DG), BF16)],
        compiler_params=_params(("parallel",)),
    )(zm, zm, ln_g, ln_b, w_s, bs_col, attn, attn_g, gm_g)


def _gmlp_bwd(dgm, zm, ln_g, ln_b, w_s, w_st, bs_col, tr):
    S = zm.shape[0]
    H = w_s.shape[0]
    DG = H * HEAD_DIM

    def body(dg_ref, zu_ref, zv_ref, g_ref, b_ref, w_ref, wt_ref, bs_ref,
             dzu_ref, dzv_ref, dw_ref, dbs_ref, dlg_ref, dlb_ref, dv_s):
        @pl.when(pl.program_id(0) == 0)
        def _():
            dw_ref[...] = jnp.zeros_like(dw_ref)
            dbs_ref[...] = jnp.zeros_like(dbs_ref)
            dlg_ref[...] = jnp.zeros_like(dlg_ref)
            dlb_ref[...] = jnp.zeros_like(dlb_ref)

        zu = zu_ref[...].astype(F32)
        zv = zv_ref[...].astype(F32)
        u = _gelu(zu)
        y, rstd = _ln_stats(_gelu(zv))
        v = (y * g_ref[...] + b_ref[...]).astype(BF16)
        dgm_blk = dg_ref[...]
        mask = _tril_mask()
        mask_t = _iota2((CHUNK, CHUNK), 0) <= _iota2((CHUNK, CHUNK), 1)
        for h in range(H):
            wc = jnp.where(mask, w_ref[h], 0.0).astype(BF16)
            wct = jnp.where(mask_t, wt_ref[h], 0.0).astype(BF16)
            cs = slice(h * HEAD_DIM, (h + 1) * HEAD_DIM)
            dw = jnp.zeros((CHUNK, CHUNK), F32)
            dbs = jnp.zeros((CHUNK, 1), F32)
            for c in range(tr // CHUNK):
                rs = slice(c * CHUNK, (c + 1) * CHUNK)
                vch = v[rs, cs]
                mix = _dot(wc, vch) + bs_ref[h]
                dg = dgm_blk[rs, cs]
                dzu_ref[rs, cs] = (dg * mix * _gelu_grad(zu[rs, cs])).astype(BF16)
                dmix = dg * u[rs, cs]
                dbs = dbs + jnp.sum(dmix, axis=-1, keepdims=True)
                dmix_b = dmix.astype(BF16)
                dw = dw + _dot(dmix_b, vch, NT)
                dv_s[rs, cs] = _dot(wct, dmix_b)
            dw_ref[h] += jnp.where(mask, dw, 0.0)
            dbs_ref[h] += dbs
        dv = dv_s[...]
        dlg_ref[...] += jnp.sum(dv * y, axis=0, keepdims=True)
        dlb_ref[...] += jnp.sum(dv, axis=0, keepdims=True)
        dy = dv * g_ref[...]
        dgv = rstd * (dy - jnp.mean(dy, axis=-1, keepdims=True) - y * jnp.mean(dy * y, axis=-1, keepdims=True))
        dzv_ref[...] = (dgv * _gelu_grad(zv)).astype(BF16)

    full = lambda a: pl.BlockSpec(a.shape, lambda i: (0,) * a.ndim)
    rows = pl.BlockSpec((tr, DG), lambda i: (i, 0))
    return pl.pallas_call(
        body, name="gmlp_bwd", grid=(S // tr,),
        in_specs=[rows, pl.BlockSpec((tr, DG), lambda i: (i, 3)), pl.BlockSpec((tr, DG), lambda i: (i, 4)),
                  full(ln_g), full(ln_b), full(w_s), full(w_st), full(bs_col)],
        out_specs=[rows, rows, full(w_s), full(bs_col), full(ln_g), full(ln_b)],
        out_shape=[jax.ShapeDtypeStruct((S, DG), BF16), jax.ShapeDtypeStruct((S, DG), BF16),
                   jax.ShapeDtypeStruct(w_s.shape, F32), jax.ShapeDtypeStruct(bs_col.shape, F32),
                   jax.ShapeDtypeStruct(ln_g.shape, F32), jax.ShapeDtypeStruct(ln_b.shape, F32)],
        scratch_shapes=[pltpu.VMEM((tr, DG), F32)],
        compiler_params=_params(("arbitrary",)),
    )(dgm, zm, zm, ln_g, ln_b, w_s, w_st, bs_col)


def _all_gather(name, blk):
    R, C = blk.shape

    def body(x_ref, out_ref, send_sems, recv_sems, local_sem):
        x, y, c = _me()
        me, sibling = (x, y, c), (x, y, 1 - c)
        chips = [(1 - x, y), (x, 1 - y), (1 - x, 1 - y)]

        def slab(px, py, pc):
            return out_ref.at[4 * px + 2 * py + pc]

        def copy(k, block, to, src=None):
            return pltpu.make_async_remote_copy(
                src_ref=slab(*block) if src is None else src, dst_ref=slab(*block),
                send_sem=send_sems.at[k], recv_sem=recv_sems.at[k], device_id=to, device_id_type=MESH)

        mine = pltpu.make_async_copy(x_ref, slab(*me), local_sem)
        mine.start()
        first = [copy(0, me, sibling, src=x_ref)]
        first += [copy(1 + n, me, (*chip, c), src=x_ref) for n, chip in enumerate(chips)]
        for cp in first:
            cp.start()
        passed = [copy(4 + n, (*chip, c), sibling) for n, chip in enumerate(chips)]
        for n, chip in enumerate(chips):
            copy(1 + n, (*chip, c), me).wait_recv()
            passed[n].start()
        copy(0, sibling, me).wait_recv()
        for n, chip in enumerate(chips):
            copy(4 + n, (*chip, 1 - c), me).wait_recv()
        for cp in first + passed:
            cp.wait_send()
        mine.wait()

    return pl.pallas_call(
        body, name=name, out_shape=jax.ShapeDtypeStruct((N_DEV, R, C), blk.dtype),
        in_specs=[_ANY], out_specs=_ANY,
        scratch_shapes=[pltpu.SemaphoreType.DMA((7,)), pltpu.SemaphoreType.DMA((7,)), pltpu.SemaphoreType.DMA(())],
    )(blk)


def _row_tile(R, C, itemsize=4, target_bytes=2 * 1024 * 1024):
    tr = R
    while tr % 2 == 0 and tr * C * itemsize > target_bytes and (tr // 2) % 16 == 0:
        tr //= 2
    return tr


def _rs_add1(name, g4, recv, c_idx):
    _, _, R, C = g4.shape
    tr = _row_tile(R, C)

    def body(c_ref, g_ref, r_ref, hb_ref):
        hb_ref[...] = (g_ref[...] + r_ref[...].astype(F32)).astype(BF16)

    blk = pl.BlockSpec((None, tr, C), lambda p, i, c_ref: (p, i, 0))
    return pl.pallas_call(
        body, name=name,
        grid_spec=pltpu.PrefetchScalarGridSpec(
            num_scalar_prefetch=1, grid=(4, R // tr),
            in_specs=[pl.BlockSpec((None, None, tr, C), lambda p, i, c_ref: (p, c_ref[0], i, 0)), blk],
            out_specs=blk),
        out_shape=jax.ShapeDtypeStruct((4, R, C), BF16),
        compiler_params=_params(("parallel", "parallel")),
    )(c_idx, g4, recv)


def _rs_add2_own(name, g4, recv1, recv2, c_idx, p_idx):
    _, _, R, C = g4.shape
    tr = _row_tile(R, C)

    def body(c_ref, p_ref, g_ref, r1_ref, r2_ref, out_ref):
        h = g_ref[...] + r1_ref[...].astype(F32)
        out_ref[...] = ((h + r2_ref[0].astype(F32)) + r2_ref[1].astype(F32)) + r2_ref[2].astype(F32)

    return pl.pallas_call(
        body, name=name,
        grid_spec=pltpu.PrefetchScalarGridSpec(
            num_scalar_prefetch=2, grid=(R // tr,),
            in_specs=[pl.BlockSpec((None, None, tr, C), lambda i, c_ref, p_ref: (p_ref[0], c_ref[0], i, 0)),
                      pl.BlockSpec((None, tr, C), lambda i, c_ref, p_ref: (p_ref[0], i, 0)),
                      pl.BlockSpec((3, tr, C), lambda i, c_ref, p_ref: (0, i, 0))],
            out_specs=pl.BlockSpec((tr, C), lambda i, c_ref, p_ref: (i, 0))),
        out_shape=jax.ShapeDtypeStruct((R, C), F32),
        compiler_params=_params(("parallel",)),
    )(c_idx, p_idx, g4, recv1, recv2)


def _rs_add1_windows(name, g, recv, first_blocks):
    _, R, W = recv.shape
    nl = W // LANES

    def body(t_ref, *refs):
        r_ref, hb_ref = refs[nl], refs[nl + 1]
        for u in range(nl):
            cols = slice(u * LANES, (u + 1) * LANES)
            hb_ref[:, cols] = (refs[u][...] + r_ref[:, cols].astype(F32)).astype(BF16)

    blk = pl.BlockSpec((None, R, W), lambda p, t_ref: (p, 0, 0))
    return pl.pallas_call(
        body, name=name,
        grid_spec=pltpu.PrefetchScalarGridSpec(
            num_scalar_prefetch=1, grid=(4,),
            in_specs=[pl.BlockSpec((R, LANES), functools.partial(lambda u, p, t_ref: (0, t_ref[p] + u), u))
                      for u in range(nl)] + [blk],
            out_specs=blk),
        out_shape=jax.ShapeDtypeStruct((4, R, W), BF16),
        compiler_params=_params(("parallel",)),
    )(first_blocks, *([g] * nl), recv)


def _rs_add2_own_window(name, g, recv1, recv2, first_blocks, p_idx):
    _, R, W = recv1.shape
    nl = W // LANES

    def body(t_ref, p_ref, *refs):
        r1_ref, r2_ref, out_ref = refs[nl], refs[nl + 1], refs[nl + 2]
        for u in range(nl):
            cols = slice(u * LANES, (u + 1) * LANES)
            h = refs[u][...] + r1_ref[:, cols].astype(F32)
            out_ref[:, cols] = ((h + r2_ref[0, :, cols].astype(F32)) + r2_ref[1, :, cols].astype(F32)) \
                + r2_ref[2, :, cols].astype(F32)

    return pl.pallas_call(
        body, name=name,
        grid_spec=pltpu.PrefetchScalarGridSpec(
            num_scalar_prefetch=2, grid=(1,),
            in_specs=[pl.BlockSpec((R, LANES), functools.partial(lambda u, i, t, p: (0, t[p[0]] + u), u))
                      for u in range(nl)]
            + [pl.BlockSpec((None, R, W), lambda i, t, p: (p[0], 0, 0)), pl.BlockSpec((3, R, W), lambda i, t, p: (0, 0, 0))],
            out_specs=pl.BlockSpec((R, W), lambda i, t, p: (0, 0))),
        out_shape=jax.ShapeDtypeStruct((R, W), F32),
        compiler_params=_params(("arbitrary",)),
    )(first_blocks, p_idx, *([g] * nl), recv1, recv2)


def _add_windows(name, windows, first, second, n_blocks):
    _, R, W = windows.shape
    dev1 = jnp.asarray([d for d, _ in first], jnp.int32)
    blk1 = jnp.asarray([b for _, b in first], jnp.int32)
    dev2 = jnp.asarray([max(d, 0) for d, _ in second], jnp.int32)
    blk2 = jnp.asarray([b for _, b in second], jnp.int32)
    two = jnp.asarray([int(d >= 0) for d, _ in second], jnp.int32)

    G = 4
    assert n_blocks % G == 0

    def body(d1_ref, b1_ref, d2_ref, b2_ref, two_ref, *refs):
        out_ref = refs[2 * G]
        k = pl.program_id(0)
        for u in range(G):
            a_ref, b_ref = refs[u], refs[G + u]
            cols = slice(u * LANES, (u + 1) * LANES)

            @pl.when(two_ref[k * G + u] == 0)
            def _():
                out_ref[:, cols] = a_ref[...]

            @pl.when(two_ref[k * G + u] != 0)
            def _():
                out_ref[:, cols] = a_ref[...] + b_ref[...]

    def spec(u, second_owner):
        if second_owner:
            return pl.BlockSpec((None, R, LANES), lambda k, d1, b1, d2, b2, t: (d2[k * G + u], 0, b2[k * G + u]))
        return pl.BlockSpec((None, R, LANES), lambda k, d1, b1, d2, b2, t: (d1[k * G + u], 0, b1[k * G + u]))

    return pl.pallas_call(
        body, name=name,
        grid_spec=pltpu.PrefetchScalarGridSpec(
            num_scalar_prefetch=5, grid=(n_blocks // G,),
            in_specs=[spec(u, False) for u in range(G)] + [spec(u, True) for u in range(G)],
            out_specs=pl.BlockSpec((R, G * LANES), lambda k, d1, b1, d2, b2, t: (0, k))),
        out_shape=jax.ShapeDtypeStruct((R, n_blocks * LANES), windows.dtype),
        compiler_params=_params(("parallel",)),
    )(dev1, blk1, dev2, blk2, two, *([windows] * (2 * G)))


def _sum8(name, g):
    _, R, C = g.shape

    def body(g_ref, out_ref):
        acc = g_ref[0]
        for d in range(1, N_DEV):
            acc = acc + g_ref[d]
        out_ref[...] = acc

    return pl.pallas_call(body, name=name, out_shape=jax.ShapeDtypeStruct((R, C), F32),
                          compiler_params=_params())(g)


def _adamw_math(w, g, m, v):
    m = ADAM_B1 * m + (1.0 - ADAM_B1) * g
    v = ADAM_B2 * v + (1.0 - ADAM_B2) * (g * g)
    m_hat = m / (1.0 - ADAM_B1 ** ADAM_STEP)
    v_hat = v / (1.0 - ADAM_B2 ** ADAM_STEP)
    delta = -ADAM_LR * (m_hat / (jnp.sqrt(v_hat) + ADAM_EPS) + ADAM_WD * w)
    return delta, m, v


def _adamw(name, w, g, m, v):
    R, C = w.shape
    tr = _row_tile(R, C, target_bytes=1024 * 1024)
    return _row_call(name, lambda *a: (_adamw_math(*a), ()), [w, g, m, v], [], [(C, F32)] * 3, [], tr)


def _adamw_from_window(name, w, m, v, window, gate, where):
    R, C = w.shape
    W = window.shape[1]
    tr = _row_tile(R, C, target_bytes=1024 * 1024)

    def body(p_ref, w_ref, m_ref, v_ref, win_ref, gate_ref, g_out, d_out, m_out, v_out):
        off, nb, hg = p_ref[0], p_ref[1], p_ref[2]
        r, c = _iota2((W, C), 0), _iota2((W, C), 1)
        pick = jnp.logical_or(jnp.logical_and(c < nb, r == c + off),
                              jnp.logical_and(c >= nb + hg, r == c - hg + off)).astype(BF16)
        r2, c2 = _iota2((LANES, C), 0), _iota2((LANES, C), 1)
        pick_gate = jnp.logical_and(r2 < hg, c2 == nb + r2).astype(BF16)
        g = _dot3(win_ref[...], pick) + _dot3(gate_ref[...], pick_gate)
        g_out[...] = g
        d_out[...], m_out[...], v_out[...] = _adamw_math(w_ref[...], g, m_ref[...], v_ref[...])

    blk = pl.BlockSpec((tr, C), lambda i, p: (i, 0))
    return pl.pallas_call(
        body, name=name,
        grid_spec=pltpu.PrefetchScalarGridSpec(
            num_scalar_prefetch=1, grid=(R // tr,),
            in_specs=[blk, blk, blk, pl.BlockSpec((tr, W), lambda i, p: (i, 0)),
                      pl.BlockSpec((tr, LANES), lambda i, p: (i, 0))],
            out_specs=[blk] * 4),
        out_shape=[jax.ShapeDtypeStruct((R, C), F32)] * 4,
        compiler_params=_params(("parallel",)),
    )(where, w, m, v, window, gate)


def _adamw_many(name, ws, gs, ms, vs):
    n = len(ws)

    def body(*refs):
        ins, outs = refs[:4 * n], refs[4 * n:]
        for k in range(n):
            res = _adamw_math(ins[k][...], ins[n + k][...], ins[2 * n + k][...], ins[3 * n + k][...])
            for t in range(3):
                outs[t * n + k][...] = res[t]

    out = pl.pallas_call(
        body, name=name, out_shape=[jax.ShapeDtypeStruct(w.shape, F32) for _ in range(3) for w in ws],
        compiler_params=_params(),
    )(*ws, *gs, *ms, *vs)
    return out[:n], out[n:2 * n], out[2 * n:]


def _pack(parts):
    flat = []
    total = 0
    for a in parts:
        n = math.prod(a.shape)
        flat.append(a.reshape(-1).astype(F32))
        if n % LANES:
            flat.append(jnp.zeros((-n % LANES,), F32))
        total += n + (-n % LANES)
    if total % (8 * LANES):
        flat.append(jnp.zeros((-total % (8 * LANES),), F32))
    return jnp.concatenate(flat).reshape(-1, LANES)


def _unpack(packed, shapes):
    out = []
    r = 0
    for shp in shapes:
        n = math.prod(shp)
        nr = -(-n // LANES)
        out.append(packed[r:r + nr].reshape(-1)[:n].reshape(shp))
        r += nr
    return out


def kernel(x, norm_mix_g, w_in, b_f, gmlp_ln_g, gmlp_ln_b, w_s, b_s, attn_out_g, gmlp_out_g, w_out, norm_ffn_g, w_ff1, w_ff2, norm_final_g, loss_target, m_norm_mix_g, m_w_in, m_b_f, m_gmlp_ln_g, m_gmlp_ln_b, m_w_s, m_b_s, m_attn_out_g, m_gmlp_out_g, m_w_out, m_norm_ffn_g, m_w_ff1, m_w_ff2, m_norm_final_g, v_norm_mix_g, v_w_in, v_b_f, v_gmlp_ln_g, v_gmlp_ln_b, v_w_s, v_b_s, v_attn_out_g, v_gmlp_out_g, v_w_out, v_norm_ffn_g, v_w_ff1, v_w_ff2, v_norm_final_g):
    S, D = x.shape[1], x.shape[2]
    H = b_f.shape[1]
    DA = H * HEAD_DIM
    DG = gmlp_ln_g.shape[1]
    DQKV = 3 * DA
    DMAIN = DQKV + 2 * DG
    DIN = DMAIN + H
    DFF = w_ff1.shape[2] * N_DEV
    w_in_cols = w_in.shape[2]
    assert DIN == w_in_cols * N_DEV and DA == DG and D == DA + DG

    T_ATT = min(T_ATT_MAX, S)
    TR = min(TR_MAX, S)

    x0 = x[0]
    tgt = loss_target[0]
    g_final = norm_final_g.reshape(1, D)

    FB = DFF // N_DEV
    x_pos, y_pos, c_pos = _me()
    me_idx = 4 * x_pos + 2 * y_pos + c_pos

    WW = -(-(w_in_cols + LANES - 1) // LANES) * LANES
    to_main = lambda col: col if col <= DQKV else max(DQKV, col - H)
    lo = [to_main(n * w_in_cols) for n in range(N_DEV)]
    hi = [to_main((n + 1) * w_in_cols) for n in range(N_DEV)]
    starts = [v // LANES * LANES for v in lo]
    gate_dev = DQKV // w_in_cols
    n_before = DQKV - gate_dev * w_in_cols
    g0 = lo[gate_dev] - starts[gate_dev]
    stash = -(-(g0 + w_in_cols - H) // LANES) * LANES
    assert all(hi[n] <= starts[n] + WW <= DMAIN for n in range(N_DEV))
    assert gate_dev * w_in_cols <= DQKV and DQKV + H <= (gate_dev + 1) * w_in_cols and stash + LANES <= WW
    shard = w_in[0].astype(BF16)

    def my_window(n):
        if n != gate_dev:
            return lambda s: jnp.pad(s, ((0, 0), (lo[n] - starts[n], WW - w_in_cols - (lo[n] - starts[n]))))
        return lambda s: jnp.concatenate([
            jnp.zeros((D, g0), BF16), s[:, :n_before], s[:, n_before + H:],
            jnp.zeros((D, stash - g0 - (w_in_cols - H)), BF16), s[:, n_before:n_before + H],
            jnp.zeros((D, WW - stash - H), BF16)], axis=1)
    (windows_part,) = _run_jobs(
        "ag_w_in", [_job_gather_chips(lax.switch(me_idx, [my_window(n) for n in range(N_DEV)], shard))])[0]
    ((h,), _), ((windows,),) = _row_call(
        "rms_mix", lambda xb, g: ((_rms_fwd(xb, g),), ()), [x0], [norm_mix_g], [(D, BF16)], [], TR,
        jobs=[_job_gather_sibling(windows_part)])
    first, second = [], []
    for blk in range(DMAIN // LANES):
        c0 = blk * LANES
        owners = [(n, (c0 - starts[n]) // LANES) for n in range(N_DEV) if lo[n] < c0 + LANES and hi[n] > c0]
        assert 1 <= len(owners) <= 2
        first.append(owners[0])
        second.append(owners[1] if len(owners) == 2 else (-1, 0))
    w_main = _add_windows("w_in_windows", windows, first, second, DMAIN // LANES)
    w_f = windows[gate_dev, :, stash:stash + LANES]
    c_idx = jnp.reshape(c_pos, (1,)).astype(jnp.int32)
    p_idx = jnp.reshape(2 * x_pos + y_pos, (1,)).astype(jnp.int32)

    w_ff1_b = w_ff1[0].astype(BF16)
    (zm,), ((w_out_part,), (w_ff1_q1,)) = _mm_nn(
        "in_proj", h, w_main, [BF16], 2048, 1024, 2048,
        jobs=[_job_gather_chips(w_out[0].astype(BF16)), _job_gather_chips(w_ff1_b, part=(0, 1, 4))])
    (zf,) = _mm_nn("in_proj_f", h, w_f, [F32], 1024, LANES, 2048)
    bf_pad = jnp.pad(b_f, ((0, 0), (0, LANES - H)))
    f_row = _fgate_fwd(zf, bf_pad)
    NB = S // T_ATT
    f_row3 = f_row.reshape(H, NB, 1, T_ATT)
    (attn, lse_row3), ((w_out_all,), (w_ff1_part,)) = _attn2_fwd(
        zm, f_row3, T_ATT, jobs=[_job_gather_sibling(w_out_part),
                                 _job_gather_chips(w_ff1_b, part=(1, 4, 4), into=w_ff1_q1)])
    w_out_full = w_out_all.reshape(D, D)
    bs_col = b_s[0].reshape(H, CHUNK, 1)
    gm, merged = _gmlp_fwd(zm, gmlp_ln_g, gmlp_ln_b, w_s[0], bs_col, attn, attn_out_g, gmlp_out_g, TR)

    w_ff2_b = w_ff2[0].astype(BF16)
    TMR = min(TMR_MAX, S)

    def out_proj_fn(acc, res, g):
        xb = acc + res
        return (xb, _rms_fwd(xb, g)), ()
    ((x1, h2), _), ((w_ff1_all,), (w_ff2_q1,)) = _mm_rows(
        "out_proj", (S // TMR, 1), merged, pl.BlockSpec((TMR, D), lambda i, k: (i, 0)),
        w_out_full, pl.BlockSpec((D, D), lambda i, k: (0, 0)), NN, TMR, D, [x0], [norm_ffn_g],
        [(D, F32), (D, BF16)], [], out_proj_fn,
        jobs=[_job_gather_sibling(w_ff1_part), _job_gather_chips(w_ff2_b, part=(0, 1, 4))])

    tm, tn, tk = min(1024, S), min(1024, FB), min(2048, D)
    tm1 = min(2048, S)
    o_spec = pl.BlockSpec((tm1, tn), lambda i, j, k: (i, j))

    square = lambda t: (t.astype(F32) * t.astype(F32)).astype(BF16)
    nj = FB // tn
    ff2_rest = [_job_gather_chips(w_ff2_b, part=(1, 4, 4), into=w_ff2_q1)]
    (a_act,), ((w_ff2_q2,),) = _mm(
        "ff1", (S // tm1, DFF // tn, D // tk), h2, pl.BlockSpec((tm1, tk), lambda i, j, k: (i, k)),
        w_ff1_all, pl.BlockSpec((None, tk, tn), lambda i, j, k: (j // nj, k, j % nj)), NN, (tm1, tn),
        [jax.ShapeDtypeStruct((S, DFF), BF16)], [o_spec], epilogue=lambda acc: (jnp.maximum(acc, 0.0),),
        jobs=ff2_rest)
    (w_ff2_all,) = _run_jobs("ag_w_ff2_sibling", [_job_gather_sibling(w_ff2_q2)])[0]
    w_ff2_full = w_ff2_all.reshape(DFF, D)
    def head_fn(acc, res, t, g):
        xb = acc + res
        rstd = lax.rsqrt(jnp.mean(xb * xb, axis=-1, keepdims=True) + EPS)
        xhat = xb * rstd
        err = xhat * g - t
        loss = 0.5 * jnp.sum(jnp.mean(err * err, axis=-1, keepdims=True), axis=0, keepdims=True)
        dy = err * (1.0 / D)
        dg = jnp.sum(dy * xhat, axis=0, keepdims=True)
        dxhat = dy * g
        dx = rstd * (dxhat - xhat * jnp.mean(dxhat * xhat, axis=-1, keepdims=True))
        return (dx, dx), (dg, jnp.broadcast_to(loss, (1, LANES)))
    tk_ff2 = min(1024, DFF)
    (dx2, dx2_b), (dg_final, loss_part) = _mm_rows(
        "ff2", (S // TMR, DFF // tk_ff2), a_act, pl.BlockSpec((TMR, tk_ff2), lambda i, k: (i, k)),
        w_ff2_full, pl.BlockSpec((tk_ff2, D), lambda i, k: (k, 0)), NN, TMR, D, [x1, tgt], [g_final],
        [(D, F32), (D, BF16)], [D, LANES], head_fn, a_fn=square)

    (da,) = _mm_nt("ff2_dx", dx2_b, w_ff2_full, [BF16], 2048, 1024, 2048, extras=[a_act],
                   epilogue=lambda acc, a: (2.0 * a.astype(F32) * acc,))
    dw_ff2, dw_ff2_b = _mm_tn("ff2_dw", a_act, dx2_b, [F32, BF16], 1024, 2048, 1024, a_fn=square)
    tm2, tk2 = min(2048, D), min(1024, S)
    dw1_spec = pl.BlockSpec((None, tm2, FB), lambda i, j, k: (j, i, 0))
    (dw_ff1, dw_ff1_b), ((r1_ff2,),) = _mm(
        "ff1_dw", (D // tm2, DFF // FB, S // tk2), h2, pl.BlockSpec((tk2, tm2), lambda i, j, k: (k, i)),
        da, pl.BlockSpec((tk2, FB), lambda i, j, k: (k, j)), TN, (tm2, FB),
        [jax.ShapeDtypeStruct((N_DEV, D, FB), F32), jax.ShapeDtypeStruct((N_DEV, D, FB), BF16)], [dw1_spec] * 2,
        epilogue=lambda acc: (acc, acc), jobs=[_job_scatter_sibling(dw_ff2_b.reshape(4, 2, FB, D))])
    hb_ff2 = _rs_add1("rs_add1_w_ff2", dw_ff2.reshape(4, 2, FB, D), r1_ff2, c_idx)
    def ffn_bwd_fn(dh, xb, dres, g):
        dx, dg = _rms_bwd(dh, xb, g)
        dx = dx + dres
        return (dx, dx), (dg,)
    tkb = min(1024, FB)
    nkb = FB // tkb
    ((dx1, dx1_b), (dg_ffn,)), ((r2_ff2,), (r1_ff1,)) = _mm_rows(
        "ff1_dx", (S // TMR, DFF // tkb), da, pl.BlockSpec((TMR, tkb), lambda i, k: (i, k)),
        w_ff1_all, pl.BlockSpec((None, D, tkb), lambda i, k: (k // nkb, 0, k % nkb)), NT, TMR, D, [x1, dx2],
        [norm_ffn_g], [(D, F32), (D, BF16)], [D], ffn_bwd_fn,
        jobs=[_job_scatter_chips(hb_ff2), _job_scatter_sibling(dw_ff1_b.reshape(4, 2, D, FB))])
    g_w_ff2 = _rs_add2_own("rs_add2_w_ff2", dw_ff2.reshape(4, 2, FB, D), r1_ff2, r2_ff2, c_idx, p_idx)
    hb_ff1 = _rs_add1("rs_add1_w_ff1", dw_ff1.reshape(4, 2, D, FB), r1_ff1, c_idx)

    def merge_bwd_fn(dm, a, g, ga, gg):
        da_, dga = _rms_bwd(dm[:, :DA], a, ga)
        dg_, dgg = _rms_bwd(dm[:, DA:], g, gg)
        return (da_, dg_), (dga, dgg)
    (dattn, dgm), (dg_attn, dg_gmlp) = _mm_rows(
        "out_proj_dx", (S // TMR, 1), dx1_b, pl.BlockSpec((TMR, D), lambda i, k: (i, 0)),
        w_out_full, pl.BlockSpec((D, D), lambda i, k: (0, 0)), NT, TMR, D, [attn, gm], [attn_out_g, gmlp_out_g],
        [(DA, F32), (DG, F32)], [DA, DG], merge_bwd_fn)
    dw_out, dw_out_b = _mm_tn("out_proj_dw", merged, dx1_b, [F32, BF16], 2048, 1024, 1024)

    w_st = jnp.swapaxes(w_s[0], 1, 2)
    dzu, dzv, dw_s, dbs_col, dln_g, dln_b = _gmlp_bwd(dgm, zm, gmlp_ln_g, gmlp_ln_b, w_s[0], w_st, bs_col, TR)

    delta_row3 = _attn_delta(dattn, attn, TR).reshape(H, NB, 1, T_ATT)
    (dq, ds_rowsum), ((r2_ff1_a,), (r1_out,)) = _attn2_bwd_dq(
        zm, dattn, f_row3, lse_row3, delta_row3, T_ATT,
        jobs=[_job_scatter_chips(hb_ff1, part=(0, 5, 8)),
              _job_scatter_sibling(dw_out_b.reshape(4, 2, D // N_DEV, D))])
    hb_out = _rs_add1("rs_add1_w_out", dw_out.reshape(4, 2, D // N_DEV, D), r1_out, c_idx)
    (dk, dv, df_row3), ((r2_ff1,), (r2_out,)) = _attn2_bwd_dkv(
        zm, dattn, f_row3, lse_row3, delta_row3, ds_rowsum, T_ATT,
        jobs=[_job_scatter_chips(hb_ff1, part=(5, 8, 8), into=r2_ff1_a), _job_scatter_chips(hb_out)])
    g_w_ff1 = _rs_add2_own("rs_add2_w_ff1", dw_ff1.reshape(4, 2, D, FB), r1_ff1, r2_ff1, c_idx, p_idx)
    g_w_out = _rs_add2_own("rs_add2_w_out", dw_out.reshape(4, 2, D // N_DEV, D), r1_out, r2_out, c_idx, p_idx)
    dzf, dbf = _fgate_bwd(df_row3.reshape(H, S), zf, bf_pad)

    dz_main = jnp.concatenate([dq, dk, dv, dzu, dzv], axis=1)
    dw_main, dw_main_b = _mm_tn("in_proj_dw", h, dz_main, [F32, BF16], 2048, 1024, 1024)
    (dw_f,), ((r1_in,),) = _mm_tn("in_proj_f_dw", h, dzf, [F32], 2048, LANES, 1024,
                                  jobs=[_job_scatter_sibling_windows(dw_main_b, starts, WW)])
    first_blocks = jnp.stack([jnp.where(c_pos == 0, starts[2 * p], starts[2 * p + 1]) // LANES
                              for p in range(4)]).astype(jnp.int32)
    hb_in = _rs_add1_windows("rs_add1_w_in", dw_main, r1_in, first_blocks)

    def mix_bwd_fn(dh_main, dz_gate, xb, dres, g, w_gate):
        dx, dg = _rms_bwd(dh_main + _dot(dz_gate, w_gate, NT), xb, g)
        return (dx + dres,), (dg,)
    tk_in = min(1024, DMAIN)
    ((grad_x,), (dg_mix,)), ((r2_in,),) = _mm_rows(
        "in_proj_dx", (S // TMR, DMAIN // tk_in), dz_main, pl.BlockSpec((TMR, tk_in), lambda i, k: (i, k)),
        w_main, pl.BlockSpec((D, tk_in), lambda i, k: (0, k)), NT, TMR, D, [dzf, x0, dx1], [norm_mix_g, w_f],
        [(D, F32)], [D], mix_bwd_fn, jobs=[_job_scatter_chips(hb_in)])
    g_window = _rs_add2_own_window("rs_add2_w_in", dw_main, r1_in, r2_in, first_blocks, p_idx)

    small_shapes = [norm_mix_g.shape, b_f.shape, gmlp_ln_g.shape, gmlp_ln_b.shape, w_s.shape, b_s.shape,
                    attn_out_g.shape, gmlp_out_g.shape, norm_ffn_g.shape, norm_final_g.shape]
    small_parts = [dg_mix, dbf[:, :H], dln_g, dln_b, dw_s, dbs_col, dg_attn, dg_gmlp, dg_ffn, dg_final]
    g_small = _sum8("small_sum", _all_gather("ag_small", _pack(small_parts + [dw_f[:, :H], loss_part])))
    *gs, g_gate, loss_sum = _unpack(g_small, small_shapes + [(D, H), (1, LANES)])
    two_d = lambda a: a.reshape(1, -1) if a.ndim == 1 else a
    ds, nms, nvs = _adamw_many(
        "adamw_small",
        [two_d(a) for a in (norm_mix_g, b_f, gmlp_ln_g, gmlp_ln_b, w_s, b_s, attn_out_g, gmlp_out_g, norm_ffn_g,
                            norm_final_g)],
        [two_d(a) for a in gs],
        [two_d(a) for a in (m_norm_mix_g, m_b_f, m_gmlp_ln_g, m_gmlp_ln_b, m_w_s, m_b_s, m_attn_out_g, m_gmlp_out_g,
                            m_norm_ffn_g, m_norm_final_g)],
        [two_d(a) for a in (v_norm_mix_g, v_b_f, v_gmlp_ln_g, v_gmlp_ln_b, v_w_s, v_b_s, v_attn_out_g, v_gmlp_out_g,
                            v_norm_ffn_g, v_norm_final_g)])
    ds, nms, nvs = [[a.reshape(s) for a, s in zip(lst, small_shapes)] for lst in (ds, nms, nvs)]

    is_gate_dev = me_idx == gate_dev
    where = jnp.stack([sum(jnp.where(me_idx == n, lo[n] - starts[n], 0) for n in range(N_DEV)),
                       jnp.where(is_gate_dev, n_before, w_in_cols), jnp.where(is_gate_dev, H, 0)]).astype(jnp.int32)
    big = {"w_in": tuple(a[None] for a in _adamw_from_window(
        "adamw_w_in", w_in[0], m_w_in[0], v_w_in[0], g_window, jnp.pad(g_gate, ((0, 0), (0, LANES - H))), where))}
    for nm, w, g, m, v in (("w_out", w_out, g_w_out, m_w_out, v_w_out),
                           ("w_ff1", w_ff1, g_w_ff1, m_w_ff1, v_w_ff1), ("w_ff2", w_ff2, g_w_ff2, m_w_ff2, v_w_ff2)):
        (d_, m_, v_), _ = _adamw("adamw_" + nm, w[0], g, m[0], v[0])
        big[nm] = (g[None], d_[None], m_[None], v_[None])

    loss = loss_sum[0, 0]

    def leaves(n):
        sm = (gs, ds, nms, nvs)[n]
        return [sm[0], big["w_in"][n], sm[1], sm[2], sm[3], sm[4], sm[5], sm[6], sm[7], big["w_out"][n], sm[8],
                big["w_ff1"][n], big["w_ff2"][n], sm[9]]

    return (loss, grad_x[None], *leaves(0), *leaves(1), *leaves(2), *leaves(3))
```
